```python
import math
import jax, jax.numpy as jnp
from jax import lax
import numpy as np

D_MODEL = 1024
BATCH = 32
SEQ = 256
DEPTH = 1
DEC_BATCH = 2
DEC_SEQ = 4096
PAST_LEN = 512

GRID_W = 64
D_CONV = 1024
CONV_K = 31
N_SSM_HEADS = 16
SSM_HEADDIM = 64
D_SSM = N_SSM_HEADS * SSM_HEADDIM
N_SSM_GROUPS = 4
HEADS_PER_GROUP = N_SSM_HEADS // N_SSM_GROUPS
D_STATE = 128
SSM_CONV_K = 4
SSD_CHUNK = 128
D_XBC = D_SSM + 2 * N_SSM_GROUPS * D_STATE
D_MIX = D_CONV + D_SSM
D_IN_PROJ = 2 * D_CONV + D_SSM + D_XBC + 2 * N_SSM_HEADS
N_EXPERTS = 256
TOP_K = 8
N_EXPERT_GROUPS = 8
TOPK_GROUPS = 4
D_EXPERT = 256
D_SHARED = 256
ROUTED_SCALE = 2.5
MOE_BLOCK = 128
LN_EPS = 1e-5

kernel_name = 'hybrid_conformer_ssd_moe_diffusion_step'


def _layernorm(x, g=None, b=None):
    xf = x.astype(jnp.float32)
    mu = jnp.mean(xf, axis=-1, keepdims=True)
    var = jnp.mean(jnp.square(xf - mu), axis=-1, keepdims=True)
    y = (xf - mu) * lax.rsqrt(var + LN_EPS)
    if g is not None:
        y = y * g.astype(jnp.float32) + b.astype(jnp.float32)
    return y.astype(x.dtype)


def _dwconv(x, w, b):
    k = w.shape[0]
    y = lax.conv_general_dilated(x, w[:, None, :], window_strides=(1,),
                                 padding=[((k - 1) // 2, k // 2)],
                                 dimension_numbers=('NWC', 'WIO', 'NWC'),
                                 feature_group_count=x.shape[-1])
    return y + b


def _grid_pos_embed(n_tokens, dtype):
    rows = n_tokens // GRID_W
    quarter = D_MODEL // 4
    freq = jnp.exp(-math.log(10000.0) * jnp.arange(quarter, dtype=jnp.float32) / quarter)
    r = jnp.broadcast_to(jnp.arange(rows, dtype=jnp.float32)[:, None, None] * freq, (rows, GRID_W, quarter))
    cl = jnp.broadcast_to(jnp.arange(GRID_W, dtype=jnp.float32)[None, :, None] * freq, (rows, GRID_W, quarter))
    emb = jnp.concatenate([jnp.sin(r), jnp.cos(r), jnp.sin(cl), jnp.cos(cl)], axis=-1)
    return emb.reshape(rows * GRID_W, D_MODEL).astype(dtype)


def _ssd_scan(xdt, a, bmat, cmat, h0):
    b, L, G, R, P = xdt.shape
    nc = L // SSD_CHUNK
    xc = xdt.reshape(b, nc, SSD_CHUNK, G, R, P)
    ac = a.reshape(b, nc, SSD_CHUNK, G, R)
    bc = bmat.reshape(b, nc, SSD_CHUNK, G, -1)
    cc = cmat.reshape(b, nc, SSD_CHUNK, G, -1)
    acs = jnp.cumsum(ac, axis=2)
    causal = jnp.tril(jnp.ones((SSD_CHUNK, SSD_CHUNK), bool))[None, None, :, :, None, None]
    seg = jnp.exp(jnp.where(causal, acs[:, :, :, None] - acs[:, :, None, :], -jnp.inf))
    cb = jnp.einsum('bclgn,bcsgn->bclsg', cc, bc)
    y_diag = jnp.einsum('bclsgr,bcsgrp->bclgrp', cb[..., None] * seg, xc)
    decay_to_end = jnp.exp(acs[:, :, -1:] - acs)
    chunk_states = jnp.einsum('bclgn,bclgrp->bcgrpn', bc, xc * decay_to_end[..., None])
    chunk_decay = jnp.exp(acs[:, :, -1])

    def step(h, inp):
        s_c, d_c = inp
        return h * d_c[..., None, None] + s_c, h

    h_final, h_in = lax.scan(step, h0, (jnp.moveaxis(chunk_states, 1, 0), jnp.moveaxis(chunk_decay, 1, 0)))
    h_in = jnp.moveaxis(h_in, 0, 1)
    y_off = jnp.einsum('bclgn,bcgrpn->bclgrp', cc, h_in) * jnp.exp(acs)[..., None]
    return (y_diag + y_off).reshape(b, L, G, R, P), h_final


def _ssd_mixer(xbc, z, dt_raw, conv_w, conv_b, dt_bias, a_log, d_skip, norm_g, h0f, h0b):
    b, L, _ = xbc.shape
    G, R, P, N = N_SSM_GROUPS, HEADS_PER_GROUP, SSM_HEADDIM, D_STATE
    xbc = jax.nn.silu(_dwconv(xbc, conv_w, conv_b)).astype(jnp.float32)
    xs, bm, cm = jnp.split(xbc, [D_SSM, D_SSM + G * N], axis=-1)
    xs = xs.reshape(b, L, G, R, P)
    bm = bm.reshape(b, L, G, N)
    cm = cm.reshape(b, L, G, N)
    dt = jax.nn.softplus(dt_raw.astype(jnp.float32).reshape(b, L, 2, N_SSM_HEADS)
                         + dt_bias.astype(jnp.float32)).reshape(b, L, 2, G, R)
    a_neg = -jnp.exp(a_log.astype(jnp.float32)).reshape(2, G, R)
    flip = lambda t: jnp.flip(t, axis=1)
    dtf, dtb = dt[:, :, 0], dt[:, :, 1]
    y_f, hf = _ssd_scan(xs * dtf[..., None], a_neg[0] * dtf, bm, cm, h0f.astype(jnp.float32))
    y_b, hb = _ssd_scan(flip(xs * dtb[..., None]), flip(a_neg[1] * dtb), flip(bm), flip(cm),
                        h0b.astype(jnp.float32))
    y = y_f + flip(y_b) + d_skip.astype(jnp.float32).reshape(G, R)[..., None] * xs
    y = y.reshape(b, L, D_SSM) * jax.nn.silu(z.astype(jnp.float32))
    yg = y.reshape(b, L, G, D_SSM // G)
    yg = yg * lax.rsqrt(jnp.mean(jnp.square(yg), axis=-1, keepdims=True) + LN_EPS)
    y = yg.reshape(b, L, D_SSM) * norm_g.astype(jnp.float32)
    return y.astype(z.dtype), hf.astype(z.dtype), hb.astype(z.dtype)


def _conformer_conv(glu_a, glu_g, conv_w, conv_b, ln_g, ln_b):
    u = glu_a * jax.nn.sigmoid(glu_g)
    u = _dwconv(u, conv_w, conv_b)
    u = _layernorm(u, ln_g, ln_b)
    return jax.nn.silu(u)


def _moe(h, w_router, router_bias, w_exp_gu, w_exp_down, w_sh_gu, w_sh_down):
    b, L, D = h.shape
    t = h.reshape(b * L, D)
    T = t.shape[0]
    s = jax.nn.sigmoid((t @ w_router).astype(jnp.float32))
    sel = s + router_bias.astype(jnp.float32)
    grp = sel.reshape(T, N_EXPERT_GROUPS, N_EXPERTS // N_EXPERT_GROUPS)
    grp_score = lax.top_k(grp, 2)[0].sum(-1)
    _, top_grp = lax.top_k(grp_score, TOPK_GROUPS)
    grp_mask = jax.nn.one_hot(top_grp, N_EXPERT_GROUPS, dtype=jnp.float32).sum(1) > 0
    sel = jnp.where(jnp.repeat(grp_mask, N_EXPERTS // N_EXPERT_GROUPS, axis=1), sel, -jnp.inf)
    _, idx = lax.top_k(sel, TOP_K)
    wts = jnp.take_along_axis(s, idx, axis=1)
    wts = wts / wts.sum(-1, keepdims=True) * ROUTED_SCALE
    n_assign = T * TOP_K
    flat_e = idx.reshape(-1)
    order = jnp.argsort(flat_e)
    e_sorted = flat_e[order]
    tok_sorted = (order // TOP_K).astype(jnp.int32)
    w_sorted = wts.reshape(-1)[order]
    counts = jnp.bincount(flat_e, length=N_EXPERTS)
    padded = (counts + MOE_BLOCK - 1) // MOE_BLOCK * MOE_BLOCK
    pad_end = jnp.cumsum(padded)
    pad_start = pad_end - padded
    start = jnp.cumsum(counts) - counts
    dest = pad_start[e_sorted] + jnp.arange(n_assign) - start[e_sorted]
    n_blocks = -(-n_assign // MOE_BLOCK) + N_EXPERTS
    slot_tok = jnp.full((n_blocks * MOE_BLOCK,), T, jnp.int32).at[dest].set(tok_sorted)
    block_exp = jnp.minimum(jnp.searchsorted(pad_end, jnp.arange(n_blocks) * MOE_BLOCK, side='right'),
                            N_EXPERTS - 1)
    t_pad = jnp.concatenate([t, jnp.zeros((1, D), t.dtype)], axis=0)

    def expert_block(args):
        slots, e = args
        gate, up = jnp.split(t_pad[slots] @ w_exp_gu[e], 2, axis=-1)
        return (jax.nn.silu(gate) * up) @ w_exp_down[e]

    y_slots = lax.map(expert_block, (slot_tok.reshape(n_blocks, MOE_BLOCK), block_exp)).reshape(-1, D)
    routed = jax.ops.segment_sum(y_slots[dest] * w_sorted[:, None].astype(t.dtype), tok_sorted,
                                 num_segments=T)
    sg, su = jnp.split(t @ w_sh_gu, 2, axis=-1)
    shared = (jax.nn.silu(sg) * su) @ w_sh_down
    return (routed + shared).reshape(b, L, D)


def _layer(x, cond, lp, h0f, h0b, alpha):
    (w_ada, b_ada, w_in, conv_w, conv_b, conv_ln_g, conv_ln_b, ssm_conv_w, ssm_conv_b, dt_bias, a_log,
     d_skip, ssm_norm_g, w_out, ln1_g, ln1_b, w_router, router_bias, w_exp_gu, w_exp_down, w_sh_gu,
     w_sh_down, ln2_g, ln2_b) = lp
    mod = (jax.nn.silu(cond) @ w_ada + b_ada)[:, None, :]
    sh1, sc1, g1, sh2, sc2, g2 = jnp.split(mod, 6, axis=-1)
    h = _layernorm(x) * (1 + sc1) + sh1
    proj = h @ w_in
    glu_a, glu_g, z, xbc, dt_raw = jnp.split(
        proj, [D_CONV, 2 * D_CONV, 2 * D_CONV + D_SSM, 2 * D_CONV + D_SSM + D_XBC], axis=-1)
    conv_out = _conformer_conv(glu_a, glu_g, conv_w, conv_b, conv_ln_g, conv_ln_b)
    ssm_out, hf, hb = _ssd_mixer(xbc, z, dt_raw, ssm_conv_w, ssm_conv_b, dt_bias, a_log, d_skip,
                                 ssm_norm_g, h0f, h0b)
    mix = jnp.concatenate([conv_out, ssm_out], axis=-1) @ w_out
    x = _layernorm(alpha * x + g1 * mix, ln1_g, ln1_b)
    h = _layernorm(x) * (1 + sc2) + sh2
    moe = _moe(h, w_router, router_bias, w_exp_gu, w_exp_down, w_sh_gu, w_sh_down)
    x = _layernorm(alpha * x + g2 * moe, ln2_g, ln2_b)
    return x, hf, hb


def setup_inputs(seed: int = 0) -> dict:
    key = jax.random.key(seed)
    k = jax.random.split(key, 32)
    f32 = jnp.float32
    nrm = lambda kk, shape, scale: jax.random.normal(kk, shape, f32) * scale
    beta = (8.0 * DEPTH) ** -0.25
    Lw = DEPTH
    H, P, N = N_SSM_HEADS, SSM_HEADDIM, D_STATE
    dt0 = jnp.exp(jax.random.uniform(k[14], (Lw, 2, H), f32) * (math.log(0.1) - math.log(0.001))
                  + math.log(0.001))
    return {
        'x_prompt': nrm(k[0], (BATCH, SEQ, D_MODEL), 1.0),
        'x_sample': nrm(k[1], (DEC_BATCH, DEC_SEQ, D_MODEL), 1.0),
        'state_ssd_fwd': nrm(k[2], (DEC_BATCH, DEPTH, H, P, N), 0.5),
        'state_ssd_bwd': nrm(k[3], (DEC_BATCH, DEPTH, H, P, N), 0.5),
        'c': nrm(k[4], (DEC_BATCH, D_MODEL), 1.0),
        'c_ctx': nrm(k[5], (D_MODEL,), 1.0),
        'w_ada': nrm(k[6], (Lw, D_MODEL, 6 * D_MODEL), 0.5 * D_MODEL ** -0.5),
        'b_ada': nrm(k[7], (Lw, 6 * D_MODEL), 0.02),
        'w_in': nrm(k[8], (Lw, D_MODEL, D_IN_PROJ), D_MODEL ** -0.5),
        'conv_w': nrm(k[9], (Lw, CONV_K, D_CONV), CONV_K ** -0.5),
        'conv_b': nrm(k[10], (Lw, D_CONV), 0.02),
        'conv_ln_g': 1.0 + nrm(k[11], (Lw, D_CONV), 0.02),
        'conv_ln_b': nrm(k[12], (Lw, D_CONV), 0.02),
        'ssm_conv_w': nrm(k[13], (Lw, SSM_CONV_K, D_XBC), SSM_CONV_K ** -0.5),
        'ssm_conv_b': nrm(k[15], (Lw, D_XBC), 0.02),
        'dt_bias': dt0 + jnp.log(-jnp.expm1(-dt0)),
        'a_log': jnp.log(jax.random.uniform(k[16], (Lw, 2, H), f32, 1.0, 16.0)),
        'd_skip': 1.0 + nrm(k[17], (Lw, H), 0.02),
        'ssm_norm_g': 1.0 + nrm(k[18], (Lw, D_SSM), 0.02),
        'w_out': nrm(k[19], (Lw, D_MIX, D_MODEL), beta * D_MIX ** -0.5),
        'ln1_g': 1.0 + nrm(k[20], (Lw, D_MODEL), 0.02),
        'ln1_b': nrm(k[21], (Lw, D_MODEL), 0.02),
        'w_router': nrm(k[22], (Lw, D_MODEL, N_EXPERTS), D_MODEL ** -0.5),
        'router_bias': nrm(k[23], (Lw, N_EXPERTS), 0.01),
        'w_exp_gu': nrm(k[24], (Lw, N_EXPERTS, D_MODEL, 2 * D_EXPERT), D_MODEL ** -0.5),
        'w_exp_down': nrm(k[25], (Lw, N_EXPERTS, D_EXPERT, D_MODEL), beta * D_EXPERT ** -0.5),
        'w_sh_gu': nrm(k[26], (Lw, D_MODEL, 2 * D_SHARED), D_MODEL ** -0.5),
        'w_sh_down': nrm(k[27], (Lw, D_SHARED, D_MODEL), beta * D_SHARED ** -0.5),
        'ln2_g': 1.0 + nrm(k[28], (Lw, D_MODEL), 0.02),
        'ln2_b': nrm(k[29], (Lw, D_MODEL), 0.02),
    }


def reference(x_prompt, x_sample, state_ssd_fwd, state_ssd_bwd, c, c_ctx, w_ada, b_ada, w_in, conv_w,
              conv_b, conv_ln_g, conv_ln_b, ssm_conv_w, ssm_conv_b, dt_bias, a_log, d_skip, ssm_norm_g,
              w_out, ln1_g, ln1_b, w_router, router_bias, w_exp_gu, w_exp_down, w_sh_gu, w_sh_down,
              ln2_g, ln2_b):
    alpha = (2.0 * DEPTH) ** 0.25
    G, R, P, N = N_SSM_GROUPS, HEADS_PER_GROUP, SSM_HEADDIM, D_STATE
    stacked = (w_ada, b_ada, w_in, conv_w, conv_b, conv_ln_g, conv_ln_b, ssm_conv_w, ssm_conv_b,
               dt_bias, a_log, d_skip, ssm_norm_g, w_out, ln1_g, ln1_b, w_router, router_bias,
               w_exp_gu, w_exp_down, w_sh_gu, w_sh_down, ln2_g, ln2_b)

    bp = x_prompt.shape[0]
    zero_state = jnp.zeros((bp, G, R, P, N), jnp.float32)
    ctx = x_prompt
    fwd_states, bwd_states = [], []
    for l in range(DEPTH):
        lp = [w[l] for w in stacked]
        ctx, hf, hb = _layer(ctx, c_ctx[None, :], lp, zero_state, zero_state, alpha)
        fwd_states.append(hf.reshape(bp, N_SSM_HEADS, P, N))
        bwd_states.append(hb.reshape(bp, N_SSM_HEADS, P, N))

    bd, n_lat, _ = x_sample.shape
    lat = x_sample + _grid_pos_embed(n_lat, x_sample.dtype)[None]
    for l in range(DEPTH):
        lp = [w[l] for w in stacked]
        h0f = state_ssd_fwd[:, l].reshape(bd, G, R, P, N)
        h0b = state_ssd_bwd[:, l].reshape(bd, G, R, P, N)
        lat, _, _ = _layer(lat, c, lp, h0f, h0b, alpha)

    new_state_ssd_fwd = jnp.stack(fwd_states, axis=1)
    new_state_ssd_bwd = jnp.stack(bwd_states, axis=1)
    return (ctx, lat, new_state_ssd_fwd, new_state_ssd_bwd)
```

```python
import functools
import math

import numpy as np
import jax
import jax.numpy as jnp
from jax import lax
from jax.experimental import pallas as pl
from jax.experimental.pallas import tpu as pltpu

F32 = jnp.float32
BF16 = jnp.bfloat16
I32 = jnp.int32
HI = lax.Precision.HIGHEST

D_MODEL = 1024
GRID_W = 64
D_CONV = 1024
CONV_K = 31
N_HEADS = 16
HEADDIM = 64
D_SSM = N_HEADS * HEADDIM
N_GROUPS = 4
HEADS_PER_GROUP = N_HEADS // N_GROUPS
D_STATE = 128
SSM_CONV_K = 4
CHUNK = 128
D_XBC = D_SSM + 2 * N_GROUPS * D_STATE
N_EXPERTS = 256
TOP_K = 8
N_EXPERT_GROUPS = 8
EXPERTS_PER_GROUP = N_EXPERTS // N_EXPERT_GROUPS
TOPK_GROUPS = 4
D_EXPERT = 256
D_SHARED = 256
ROUTED_SCALE = 2.5
LN_EPS = 1e-5

CONV_TILE = 256
HALO = 16
EXPERT_BLOCK = 256
VMEM_LIMIT = 56 * 1024 * 1024


def _cparams(sem, vmem=VMEM_LIMIT):
    return pltpu.CompilerParams(dimension_semantics=sem, vmem_limit_bytes=vmem)


def _silu(x):
    return x * jax.nn.sigmoid(x)


def _ln_rows(x):
    mu = jnp.mean(x, axis=-1, keepdims=True)
    xc = x - mu
    var = jnp.mean(xc * xc, axis=-1, keepdims=True)
    return xc * lax.rsqrt(var + LN_EPS)


def _iota(shape, dim):
    return lax.broadcasted_iota(I32, shape, dim)


def _expand_matrix(n_in, width):
    return (_iota((n_in, n_in * width), 0) == _iota((n_in, n_in * width), 1) // width).astype(F32)


def _dot_hi(a, b):
    return jnp.dot(a, b, precision=HI, preferred_element_type=F32)


def _ada_kernel(c_ref, w_ref, b_ref, o_ref):
    o_ref[...] = _dot_hi(_silu(c_ref[...]), w_ref[...]) + b_ref[...]


def _ada(cond8, w_ada, b_ada):
    n = w_ada.shape[1]
    tn = 1024
    return pl.pallas_call(
        _ada_kernel,
        grid=(n // tn,),
        in_specs=[pl.BlockSpec((8, D_MODEL), lambda j: (0, 0)),
                  pl.BlockSpec((D_MODEL, tn), lambda j: (0, j)),
                  pl.BlockSpec((1, tn), lambda j: (0, j))],
        out_specs=pl.BlockSpec((8, tn), lambda j: (0, j)),
        out_shape=jax.ShapeDtypeStruct((8, n), F32),
        compiler_params=_cparams(("arbitrary",)),
        name="ada",
    )(cond8, w_ada, b_ada.reshape(1, n))


def _inproj_kernel(row_ref, posb_ref, flag_ref, xp_ref, xs_ref, pos_ref, mod_ref, wm_ref, wdt_ref,
                   glu_ref, z_ref, xbc_ref, dt_ref):
    i = pl.program_id(0)
    x = jnp.where(flag_ref[i] == 1, xs_ref[...] + pos_ref[...], xp_ref[...])
    r = row_ref[i]
    sh1 = mod_ref[pl.ds(r, 1), 0:D_MODEL]
    sc1 = mod_ref[pl.ds(r, 1), D_MODEL:2 * D_MODEL]
    h = (_ln_rows(x) * (1.0 + sc1) + sh1).astype(BF16)
    glu_ref[...] = jnp.dot(h, wm_ref[:, 0:2 * D_CONV], preferred_element_type=F32).astype(BF16)
    z_ref[...] = jnp.dot(h, wm_ref[:, 2 * D_CONV:2 * D_CONV + D_SSM], preferred_element_type=F32).astype(BF16)
    xbc_ref[...] = jnp.dot(h, wm_ref[:, 2 * D_CONV + D_SSM:], preferred_element_type=F32).astype(BF16)
    dt_ref[...] = jnp.dot(h, wdt_ref[...], preferred_element_type=F32)


def _inproj(lay, xp, xs, pos, mod, w_main, w_dt, tm):
    T = lay.n_tokens
    row, posb, flag = lay.token_tile_tables(tm)
    npt = lay.n_prompt_tokens // tm
    n_main = w_main.shape[1]
    gs = pltpu.PrefetchScalarGridSpec(
        num_scalar_prefetch=3,
        grid=(T // tm,),
        in_specs=[pl.BlockSpec((tm, D_MODEL), lambda i, r, p, f: (jnp.minimum(i, npt - 1), 0)),
                  pl.BlockSpec((tm, D_MODEL), lambda i, r, p, f: (jnp.maximum(i - npt, 0), 0)),
                  pl.BlockSpec((tm, D_MODEL), lambda i, r, p, f: (p[i], 0)),
                  pl.BlockSpec((8, 6 * D_MODEL), lambda i, r, p, f: (0, 0)),
                  pl.BlockSpec((D_MODEL, n_main), lambda i, r, p, f: (0, 0)),
                  pl.BlockSpec((D_MODEL, 128), lambda i, r, p, f: (0, 0))],
        out_specs=[pl.BlockSpec((tm, 2 * D_CONV), lambda i, r, p, f: (i, 0)),
                   pl.BlockSpec((tm, D_SSM), lambda i, r, p, f: (i, 0)),
                   pl.BlockSpec((tm, D_XBC), lambda i, r, p, f: (i, 0)),
                   pl.BlockSpec((tm, 128), lambda i, r, p, f: (i, 0))])
    return pl.pallas_call(
        _inproj_kernel, grid_spec=gs,
        out_shape=(jax.ShapeDtypeStruct((T, 2 * D_CONV), BF16),
                   jax.ShapeDtypeStruct((T, D_SSM), BF16),
                   jax.ShapeDtypeStruct((T, D_XBC), BF16),
                   jax.ShapeDtypeStruct((T, 128), F32)),
        compiler_params=_cparams(("arbitrary",)),
        name="inproj",
    )(row, posb, flag, xp, xs, pos, mod, w_main, w_dt)


_N_SHIFT = 8
_SHIFT_ROWS = CONV_TILE + 2 * HALO - _N_SHIFT
_ROW_BLOCK = 32


def _conv_kernel(lok_ref, rok_ref, glu_ref, glul_ref, glur_ref, xbc_ref, xbcl_ref, xbcr_ref,
                 cw_ref, cb_ref, lng_ref, lnb_ref, sw_ref, sb_ref, co_ref, xo_ref,
                 ext_ref, sh_ref, acc_ref, ext2_ref):
    i = pl.program_id(0)
    lok = lok_ref[i] == 1
    rok = rok_ref[i] == 1

    def glu(ref):
        v = ref[...].astype(F32)
        return v[:, 0:D_CONV] * jax.nn.sigmoid(v[:, D_CONV:])

    ext_ref[0:HALO, :] = jnp.where(lok, glu(glul_ref), 0.0)
    ext_ref[HALO:HALO + CONV_TILE, :] = glu(glu_ref)
    ext_ref[HALO + CONV_TILE:, :] = jnp.where(rok, glu(glur_ref), 0.0)
    for r in range(_N_SHIFT):
        sh_ref[r] = ext_ref[r:r + _SHIFT_ROWS, :]

    first = HALO - (CONV_K - 1) // 2

    def row_block(rb, carry):
        base = pl.multiple_of(rb * _ROW_BLOCK, _ROW_BLOCK)
        for j in range(D_CONV // 128):
            lanes = slice(j * 128, (j + 1) * 128)
            acc = jnp.zeros((_ROW_BLOCK, 128), F32) + cb_ref[:, lanes]
            for k in range(CONV_K):
                o = first + k
                win = sh_ref[o % _N_SHIFT, pl.ds(base + (o // _N_SHIFT) * _N_SHIFT, _ROW_BLOCK), lanes]
                acc = acc + win * cw_ref[k:k + 1, lanes]
            acc_ref[pl.ds(base, _ROW_BLOCK), lanes] = acc
        return carry

    lax.fori_loop(0, CONV_TILE // _ROW_BLOCK, row_block, 0)
    u = _ln_rows(acc_ref[...]) * lng_ref[...] + lnb_ref[...]
    co_ref[...] = _silu(u).astype(BF16)

    ext2_ref[0:HALO, :] = jnp.where(lok, xbcl_ref[...].astype(F32), 0.0)
    ext2_ref[HALO:HALO + CONV_TILE, :] = xbc_ref[...].astype(F32)
    ext2_ref[HALO + CONV_TILE:, :] = jnp.where(rok, xbcr_ref[...].astype(F32), 0.0)
    first2 = HALO - (SSM_CONV_K - 1) // 2
    half = D_XBC // 2
    for hcol in range(2):
        lanes = slice(hcol * half, (hcol + 1) * half)
        y = jnp.zeros((CONV_TILE, half), F32) + sb_ref[:, lanes]
        for k in range(SSM_CONV_K):
            y = y + ext2_ref[first2 + k:first2 + k + CONV_TILE, lanes] * sw_ref[k:k + 1, lanes]
        xo_ref[:, lanes] = _silu(y).astype(BF16)


def _conv(lay, glu, xbc, conv_w, conv_b, ln_g, ln_b, ssm_w, ssm_b):
    T = lay.n_tokens
    lok, rok = lay.conv_tile_tables()
    n_tiles = T // CONV_TILE
    hb = CONV_TILE // HALO
    n_hb = T // HALO

    def cur(i, l, r):
        return (i, 0)

    def left(i, l, r):
        return (jnp.maximum(i * hb - 1, 0), 0)

    def right(i, l, r):
        return (jnp.minimum((i + 1) * hb, n_hb - 1), 0)

    def const(i, l, r):
        return (0, 0)

    gs = pltpu.PrefetchScalarGridSpec(
        num_scalar_prefetch=2,
        grid=(n_tiles,),
        in_specs=[pl.BlockSpec((CONV_TILE, 2 * D_CONV), cur),
                  pl.BlockSpec((HALO, 2 * D_CONV), left),
                  pl.BlockSpec((HALO, 2 * D_CONV), right),
                  pl.BlockSpec((CONV_TILE, D_XBC), cur),
                  pl.BlockSpec((HALO, D_XBC), left),
                  pl.BlockSpec((HALO, D_XBC), right),
                  pl.BlockSpec((CONV_K, D_CONV), const),
                  pl.BlockSpec((1, D_CONV), const),
                  pl.BlockSpec((1, D_CONV), const),
                  pl.BlockSpec((1, D_CONV), const),
                  pl.BlockSpec((SSM_CONV_K, D_XBC), const),
                  pl.BlockSpec((1, D_XBC), const)],
        out_specs=[pl.BlockSpec((CONV_TILE, D_CONV), cur),
                   pl.BlockSpec((CONV_TILE, D_XBC), cur)],
        scratch_shapes=[pltpu.VMEM((CONV_TILE + 2 * HALO, D_CONV), F32),
                        pltpu.VMEM((_N_SHIFT, _SHIFT_ROWS, D_CONV), F32),
                        pltpu.VMEM((CONV_TILE, D_CONV), F32),
                        pltpu.VMEM((CONV_TILE + 2 * HALO, D_XBC), F32)])
    return pl.pallas_call(
        _conv_kernel, grid_spec=gs,
        out_shape=(jax.ShapeDtypeStruct((T, D_CONV), BF16),
                   jax.ShapeDtypeStruct((T, D_XBC), BF16)),
        compiler_params=_cparams(("arbitrary",)),
        name="conv",
    )(lok, rok, glu, glu, glu, xbc, xbc, xbc, conv_w, conv_b.reshape(1, -1), ln_g.reshape(1, -1),
      ln_b.reshape(1, -1), ssm_w, ssm_b.reshape(1, -1))


_BN = N_GROUPS * D_STATE


def _ssd_kernel(chunk_ref, yidx_ref, phase_ref, first_ref, last_ref, zero_ref, sin_ref, sout_ref, cloc_ref,
                xbc_ref, z_ref, dt_ref, h0f_ref, h0b_ref, dtb_ref, alog_ref, dsk_ref, ng_ref,
                y_ref, hf_out_ref, hb_out_ref,
                hf_ref, g_ref, gin_ref, ybuf_ref):
    s = pl.program_id(0)
    phase = phase_ref[s]
    first = first_ref[s] == 1
    last = last_ref[s] == 1
    zero = zero_ref[s] == 1
    cloc = cloc_ref[s]
    H, P, N = N_HEADS, HEADDIM, D_STATE

    xs = xbc_ref[:, 0:D_SSM]
    dt = dt_ref[:, 0:2 * H] + dtb_ref[...]
    dt = jnp.maximum(dt, 0.0) + jnp.log1p(jnp.exp(-jnp.abs(dt)))
    a = dt * (-jnp.exp(alog_ref[...]))
    tri = (_iota((CHUNK, CHUNK), 1) <= _iota((CHUNK, CHUNK), 0)).astype(F32)
    acs = _dot_hi(tri, a)
    e64 = _expand_matrix(H, P)
    e128 = _expand_matrix(H, N)
    tot = acs[CHUNK - 8:CHUNK, :]
    dec = _dot_hi(jnp.exp(tot[:, 0:H]), e128)[7:8, :]
    decb = _dot_hi(jnp.exp(tot[:, H:2 * H]), e128)[7:8, :]
    exb = acs[:, H:2 * H] - a[:, H:2 * H]

    @pl.when(phase == 0)
    def _backward_states():
        @pl.when(first)
        def _():
            g_ref[...] = jnp.where(zero, 0.0, h0b_ref[0])

        wb = dt[:, H:2 * H] * jnp.exp(exb)
        xw = (xs.astype(F32) * _dot_hi(wb, e64)).astype(BF16)
        for h in range(H):
            g = h // HEADS_PER_GROUP
            bg = xbc_ref[:, D_SSM + g * N:D_SSM + (g + 1) * N]
            gh = g_ref[h]
            gin_ref[cloc, h] = gh.astype(BF16)
            upd = lax.dot_general(xw[:, h * P:(h + 1) * P], bg, (((0,), (0,)), ((), ())),
                                  preferred_element_type=F32)
            g_ref[h] = gh * decb[:, h * N:(h + 1) * N] + upd

        @pl.when(last)
        def _():
            hb_out_ref[0] = g_ref[...]

    @pl.when(phase == 1)
    def _forward_and_outputs():
        @pl.when(first)
        def _():
            hf_ref[...] = jnp.where(zero, 0.0, h0f_ref[0])

        acsf = acs[:, 0:H]
        dtf = dt[:, 0:H]
        dtb = dt[:, H:2 * H]
        totf = acs[CHUNK - 1:CHUNK, 0:H]
        totb = acs[CHUNK - 1:CHUNK, H:2 * H]
        col = _dot_hi(jnp.concatenate([acsf, exb], axis=1), _expand_matrix(2 * H, N))
        q = jnp.concatenate([acsf, exb, dtf, dtb], axis=1)
        eye = (_iota((4 * H, 4 * H), 0) == _iota((4 * H, 4 * H), 1)).astype(F32)
        qt = lax.dot_general(eye, q, (((1,), (1,)), ((), ())), precision=HI,
                             preferred_element_type=F32)
        wide = jnp.concatenate([dtf * jnp.exp(totf - acsf), jnp.exp(acsf), jnp.exp(totb - exb)], axis=1)
        wide = _dot_hi(wide, _expand_matrix(3 * H, P))
        xsf = xs.astype(F32)
        xw = (xsf * wide[:, 0:D_SSM]).astype(BF16)
        lower = _iota((CHUNK, CHUNK), 1) <= _iota((CHUNK, CHUNK), 0)
        upper = _iota((CHUNK, CHUNK), 1) >= _iota((CHUNK, CHUNK), 0)
        for g in range(N_GROUPS):
            bg = xbc_ref[:, D_SSM + g * N:D_SSM + (g + 1) * N]
            cg = xbc_ref[:, D_SSM + _BN + g * N:D_SSM + _BN + (g + 1) * N]
            cb = lax.dot_general(cg, bg, (((1,), (1,)), ((), ())), preferred_element_type=F32)
            for r in range(HEADS_PER_GROUP):
                h = g * HEADS_PER_GROUP + r
                colf = col[:, h * N:(h + 1) * N]
                colb = col[:, (H + h) * N:(H + h + 1) * N]
                mf = jnp.where(lower, jnp.exp(colf - qt[h:h + 1, :]), 0.0) * qt[2 * H + h:2 * H + h + 1, :]
                mb = jnp.where(upper, jnp.exp(qt[H + h:H + h + 1, :] - colb), 0.0) * qt[3 * H + h:3 * H + h + 1, :]
                m = (cb * (mf + mb)).astype(BF16)
                hs = slice(h * P, (h + 1) * P)
                y = jnp.dot(m, xs[:, hs], preferred_element_type=F32)
                hfh = hf_ref[h]
                yf = lax.dot_general(cg, hfh.astype(BF16), (((1,), (1,)), ((), ())),
                                     preferred_element_type=F32)
                yb = lax.dot_general(cg, gin_ref[cloc, h], (((1,), (1,)), ((), ())),
                                     preferred_element_type=F32)
                y = y + yf * wide[:, D_SSM + h * P:D_SSM + (h + 1) * P] \
                      + yb * wide[:, 2 * D_SSM + h * P:2 * D_SSM + (h + 1) * P]
                ybuf_ref[:, hs] = y
                upd = lax.dot_general(xw[:, hs], bg, (((0,), (0,)), ((), ())), preferred_element_type=F32)
                hf_ref[h] = hfh * dec[:, h * N:(h + 1) * N] + upd

        yt = (ybuf_ref[...] + dsk_ref[...] * xsf) * _silu(z_ref[...].astype(F32))
        gw = D_SSM // N_GROUPS
        for g in range(N_GROUPS):
            seg = yt[:, g * gw:(g + 1) * gw]
            ms = jnp.mean(seg * seg, axis=-1, keepdims=True)
            y_ref[:, g * gw:(g + 1) * gw] = (seg * lax.rsqrt(ms + LN_EPS) * ng_ref[:, g * gw:(g + 1) * gw]).astype(BF16)

        @pl.when(last)
        def _():
            hf_out_ref[0] = hf_ref[...]


def _ssd(lay, xbc_c, z, dt_raw, h0f, h0b, dt_bias, a_log, d_skip, norm_g):
    T = lay.n_tokens
    tabs = lay.ssd_step_tables()
    n_steps = tabs[0].shape[0]
    nsp = len(tabs)

    def by_chunk(s, *t):
        return (t[0][s], 0)

    def by_y(s, *t):
        return (t[1][s], 0)

    def by_sin(s, *t):
        return (t[6][s], 0, 0, 0)

    def by_sout(s, *t):
        return (t[7][s], 0, 0, 0)

    def const(s, *t):
        return (0, 0)

    sshape = (1, N_HEADS, HEADDIM, D_STATE)
    gs = pltpu.PrefetchScalarGridSpec(
        num_scalar_prefetch=nsp,
        grid=(n_steps,),
        in_specs=[pl.BlockSpec((CHUNK, D_XBC), by_chunk),
                  pl.BlockSpec((CHUNK, D_SSM), by_chunk),
                  pl.BlockSpec((CHUNK, 128), by_chunk),
                  pl.BlockSpec(sshape, by_sin),
                  pl.BlockSpec(sshape, by_sin),
                  pl.BlockSpec((1, 2 * N_HEADS), const),
                  pl.BlockSpec((1, 2 * N_HEADS), const),
                  pl.BlockSpec((1, D_SSM), const),
                  pl.BlockSpec((1, D_SSM), const)],
        out_specs=[pl.BlockSpec((CHUNK, D_SSM), by_y),
                   pl.BlockSpec(sshape, by_sout),
                   pl.BlockSpec(sshape, by_sout)],
        scratch_shapes=[pltpu.VMEM((N_HEADS, HEADDIM, D_STATE), F32),
                        pltpu.VMEM((N_HEADS, HEADDIM, D_STATE), F32),
                        pltpu.VMEM((lay.max_chunks, N_HEADS, HEADDIM, D_STATE), BF16),
                        pltpu.VMEM((CHUNK, D_SSM), F32)])
    n_out = lay.n_prompt_seqs
    return pl.pallas_call(
        _ssd_kernel, grid_spec=gs,
        out_shape=(jax.ShapeDtypeStruct((T, D_SSM), BF16),
                   jax.ShapeDtypeStruct((n_out,) + sshape[1:], F32),
                   jax.ShapeDtypeStruct((n_out,) + sshape[1:], F32)),
        compiler_params=_cparams(("arbitrary",)),
        name="ssd",
    )(*tabs, xbc_c, z, dt_raw, h0f, h0b, dt_bias.reshape(1, -1), a_log.reshape(1, -1),
      jnp.repeat(d_skip, HEADDIM).reshape(1, -1), norm_g.reshape(1, -1))


def _outproj_kernel(row_ref, posb_ref, flag_ref, xp_ref, xs_ref, pos_ref, mod_ref, co_ref, ys_ref, wo_ref,
                    g_ref, b_ref, x1_ref, h2_ref, *, alpha):
    i = pl.program_id(0)
    x = jnp.where(flag_ref[i] == 1, xs_ref[...] + pos_ref[...], xp_ref[...])
    r = row_ref[i]
    g1 = mod_ref[pl.ds(r, 1), 2 * D_MODEL:3 * D_MODEL]
    sh2 = mod_ref[pl.ds(r, 1), 3 * D_MODEL:4 * D_MODEL]
    sc2 = mod_ref[pl.ds(r, 1), 4 * D_MODEL:5 * D_MODEL]
    mix = jnp.dot(co_ref[...], wo_ref[0:D_CONV, :], preferred_element_type=F32) \
        + jnp.dot(ys_ref[...], wo_ref[D_CONV:, :], preferred_element_type=F32)
    x1 = _ln_rows(alpha * x + g1 * mix) * g_ref[...] + b_ref[...]
    x1_ref[...] = x1
    h2_ref[...] = _ln_rows(x1) * (1.0 + sc2) + sh2


def _outproj(lay, xp, xs, pos, mod, conv_out, y_ssm, w_out, ln_g, ln_b, alpha, tm):
    T = lay.n_tokens
    row, posb, flag = lay.token_tile_tables(tm)
    npt = lay.n_prompt_tokens // tm

    def const(i, r, p, f):
        return (0, 0)

    def cur(i, r, p, f):
        return (i, 0)

    gs = pltpu.PrefetchScalarGridSpec(
        num_scalar_prefetch=3,
        grid=(T // tm,),
        in_specs=[pl.BlockSpec((tm, D_MODEL), lambda i, r, p, f: (jnp.minimum(i, npt - 1), 0)),
                  pl.BlockSpec((tm, D_MODEL), lambda i, r, p, f: (jnp.maximum(i - npt, 0), 0)),
                  pl.BlockSpec((tm, D_MODEL), lambda i, r, p, f: (p[i], 0)),
                  pl.BlockSpec((8, 6 * D_MODEL), const),
                  pl.BlockSpec((tm, D_CONV), cur),
                  pl.BlockSpec((tm, D_SSM), cur),
                  pl.BlockSpec((D_CONV + D_SSM, D_MODEL), const),
                  pl.BlockSpec((1, D_MODEL), const),
                  pl.BlockSpec((1, D_MODEL), const)],
        out_specs=[pl.BlockSpec((tm, D_MODEL), cur),
                   pl.BlockSpec((tm, D_MODEL), cur)])
    return pl.pallas_call(
        functools.partial(_outproj_kernel, alpha=alpha), grid_spec=gs,
        out_shape=(jax.ShapeDtypeStruct((T, D_MODEL), F32),
                   jax.ShapeDtypeStruct((T, D_MODEL), F32)),
        compiler_params=_cparams(("arbitrary",)),
        name="outproj",
    )(row, posb, flag, xp, xs, pos, mod, conv_out, y_ssm, w_out, ln_g.reshape(1, -1), ln_b.reshape(1, -1))


def _route_kernel(h2_ref, wrt_ref, bias_ref, idx_ref, wts_ref, pos_ref, cnt_ref, carry_ref, *, tm):
    i = pl.program_id(0)

    @pl.when(i == 0)
    def _():
        carry_ref[...] = jnp.zeros_like(carry_ref)

    E, NG, EG = N_EXPERTS, N_EXPERT_GROUPS, EXPERTS_PER_GROUP
    neg = -jnp.inf
    logits = lax.dot_general(wrt_ref[...], h2_ref[...], (((1,), (1,)), ((), ())), precision=HI,
                             preferred_element_type=F32)
    s = jax.nn.sigmoid(logits)
    sel = s + bias_ref[...]
    sel3 = sel.reshape(NG, EG, tm)
    io3 = _iota((NG, EG, tm), 1)
    m1 = jnp.max(sel3, axis=1, keepdims=True)
    f1 = jnp.min(jnp.where(sel3 == m1, io3, EG), axis=1, keepdims=True)
    m2 = jnp.max(jnp.where(io3 == f1, neg, sel3), axis=1, keepdims=True)
    gscore = (m1 + m2).reshape(NG, tm)
    gio = _iota((NG, tm), 0)
    beaten = jnp.zeros((NG, tm), I32)
    for g in range(NG):
        row = gscore[g:g + 1, :]
        beats = jnp.where(row > gscore, 1, jnp.where(row == gscore, jnp.where(g < gio, 1, 0), 0))
        beaten = beaten + beats
    keep = (beaten < TOPK_GROUPS).astype(F32).reshape(NG, 1, tm)
    selm = jnp.where(keep > 0.5, sel3, neg).reshape(E, tm)
    eio = _iota((E, tm), 0)
    chosen = jnp.zeros((E, tm), F32)
    idxs, ws = [], []
    for k in range(TOP_K):
        m = jnp.max(selm, axis=0, keepdims=True)
        am = jnp.minimum(jnp.min(jnp.where(selm == m, eio, E), axis=0, keepdims=True), E - 1)
        hit = eio == am
        ws.append(jnp.sum(jnp.where(hit, s, 0.0), axis=0, keepdims=True))
        idxs.append(am)
        selm = jnp.where(hit, neg, selm)
        chosen = jnp.where(hit, 1.0, chosen)
    wsum = ws[0]
    for k in range(1, TOP_K):
        wsum = wsum + ws[k]
    before = (_iota((tm, tm), 0) < _iota((tm, tm), 1)).astype(BF16)
    prior = jnp.dot(chosen.astype(BF16), before, preferred_element_type=F32)
    carry = carry_ref[...]
    prior = prior + jnp.concatenate([carry] * (tm // 128), axis=1)
    for k in range(TOP_K):
        idx_ref[k:k + 1, :] = idxs[k]
        wts_ref[k:k + 1, :] = ws[k] / wsum * ROUTED_SCALE
        pos_ref[k:k + 1, :] = jnp.sum(jnp.where(eio == idxs[k], prior, 0.0), axis=0, keepdims=True).astype(I32)
    total = jnp.dot(chosen.astype(BF16), jnp.ones((tm, 128), BF16), preferred_element_type=F32)
    carry = carry + total
    carry_ref[...] = carry
    cnt_ref[...] = carry.astype(I32)


def _route(h2, w_router_t, router_bias, tm):
    T = h2.shape[0]
    bias_b = jnp.broadcast_to(router_bias.astype(F32)[:, None], (N_EXPERTS, tm))
    return pl.pallas_call(
        functools.partial(_route_kernel, tm=tm),
        grid=(T // tm,),
        in_specs=[pl.BlockSpec((tm, D_MODEL), lambda i: (i, 0)),
                  pl.BlockSpec((N_EXPERTS, D_MODEL), lambda i: (0, 0)),
                  pl.BlockSpec((N_EXPERTS, tm), lambda i: (0, 0))],
        out_specs=[pl.BlockSpec((TOP_K, tm), lambda i: (0, i)),
                   pl.BlockSpec((TOP_K, tm), lambda i: (0, i)),
                   pl.BlockSpec((TOP_K, tm), lambda i: (0, i)),
                   pl.BlockSpec((N_EXPERTS, 128), lambda i: (0, 0))],
        out_shape=(jax.ShapeDtypeStruct((TOP_K, T), I32),
                   jax.ShapeDtypeStruct((TOP_K, T), F32),
                   jax.ShapeDtypeStruct((TOP_K, T), I32),
                   jax.ShapeDtypeStruct((N_EXPERTS, 128), I32)),
        scratch_shapes=[pltpu.VMEM((N_EXPERTS, 128), F32)],
        compiler_params=_cparams(("arbitrary",)),
        name="route",
    )(h2, w_router_t, bias_b)


def _dest_kernel(idx_ref, pos_ref, start_ref, dest_ref):
    tm = idx_ref.shape[1]
    eio = _iota((N_EXPERTS, tm), 0)
    start = start_ref[...]
    for k in range(TOP_K):
        base = jnp.sum(jnp.where(eio == idx_ref[k:k + 1, :], start, 0.0), axis=0, keepdims=True)
        dest_ref[k:k + 1, :] = base.astype(I32) + pos_ref[k:k + 1, :]


def _dest(idx, pos, pad_start, tm):
    T = idx.shape[1]
    start_b = jnp.broadcast_to(pad_start.astype(F32)[:, None], (N_EXPERTS, tm))
    return pl.pallas_call(
        _dest_kernel,
        grid=(T // tm,),
        in_specs=[pl.BlockSpec((TOP_K, tm), lambda i: (0, i)),
                  pl.BlockSpec((TOP_K, tm), lambda i: (0, i)),
                  pl.BlockSpec((N_EXPERTS, tm), lambda i: (0, 0))],
        out_specs=pl.BlockSpec((TOP_K, tm), lambda i: (0, i)),
        out_shape=jax.ShapeDtypeStruct((TOP_K, T), I32),
        compiler_params=_cparams(("arbitrary",)),
        name="dest",
    )(idx, pos, start_b)


def _dispatch_kernel(dest_ref, lastblk_ref, h2_ref, xs_hbm, zero_ref, zsem, sem, *, tm, n_tokens):
    i = pl.program_id(0)

    def zero_copy(e):
        row = pl.multiple_of(jnp.maximum(lastblk_ref[e], 0) * EXPERT_BLOCK, EXPERT_BLOCK)
        return pltpu.make_async_copy(zero_ref, xs_hbm.at[pl.ds(row, EXPERT_BLOCK)], zsem)

    @pl.when(i == 0)
    def _():
        zero_ref[...] = jnp.zeros_like(zero_ref)

        def start(e, c):
            @pl.when(lastblk_ref[e] >= 0)
            def _():
                zero_copy(e).start()
            return c

        def wait(e, c):
            @pl.when(lastblk_ref[e] >= 0)
            def _():
                zero_copy(e).wait()
            return c

        lax.fori_loop(0, N_EXPERTS, start, 0)
        lax.fori_loop(0, N_EXPERTS, wait, 0)

    def row_copy(r, k):
        d = dest_ref[k * n_tokens + i * tm + r]
        return pltpu.make_async_copy(h2_ref.at[pl.ds(r, 1)], xs_hbm.at[pl.ds(d, 1)], sem)

    def start(r, c):
        for k in range(TOP_K):
            row_copy(r, k).start()
        return c

    def wait(r, c):
        for k in range(TOP_K):
            row_copy(r, k).wait()
        return c

    lax.fori_loop(0, tm, start, 0)
    lax.fori_loop(0, tm, wait, 0)


def _dispatch(h2, dest_flat, last_block, n_slots, tm):
    T = h2.shape[0]
    gs = pltpu.PrefetchScalarGridSpec(
        num_scalar_prefetch=2,
        grid=(T // tm,),
        in_specs=[pl.BlockSpec((tm, D_MODEL), lambda i, d, p: (i, 0))],
        out_specs=pl.BlockSpec(memory_space=pl.ANY),
        scratch_shapes=[pltpu.VMEM((EXPERT_BLOCK, D_MODEL), F32),
                        pltpu.SemaphoreType.DMA(()),
                        pltpu.SemaphoreType.DMA(())])
    return pl.pallas_call(
        functools.partial(_dispatch_kernel, tm=tm, n_tokens=T), grid_spec=gs,
        out_shape=jax.ShapeDtypeStruct((n_slots, D_MODEL), F32),
        compiler_params=_cparams(("arbitrary",)),
        name="dispatch",
    )(dest_flat, last_block, h2)


def _expert_kernel(bexp_ref, bidx_ref, new_ref, live_ref, x_ref, wgu_ref, wd_ref, y_ref, wgu_bf, wd_bf):
    i = pl.program_id(0)

    @pl.when(new_ref[i] == 1)
    def _():
        wgu_bf[...] = wgu_ref[0].astype(BF16)
        wd_bf[...] = wd_ref[0].astype(BF16)

    @pl.when(live_ref[i] == 1)
    def _():
        gu = jnp.dot(x_ref[...].astype(BF16), wgu_bf[...], preferred_element_type=F32)
        act = (_silu(gu[:, 0:D_EXPERT]) * gu[:, D_EXPERT:]).astype(BF16)
        y_ref[...] = jnp.dot(act, wd_bf[...], preferred_element_type=F32)


def _expert(x_sorted, w_gu, w_down, bexp, bidx, new, live, n_blocks):
    n_rows = x_sorted.shape[0]
    gs = pltpu.PrefetchScalarGridSpec(
        num_scalar_prefetch=4,
        grid=(n_blocks,),
        in_specs=[pl.BlockSpec((EXPERT_BLOCK, D_MODEL), lambda i, e, b, n, l: (b[i], 0)),
                  pl.BlockSpec((1, D_MODEL, 2 * D_EXPERT), lambda i, e, b, n, l: (e[i], 0, 0)),
                  pl.BlockSpec((1, D_EXPERT, D_MODEL), lambda i, e, b, n, l: (e[i], 0, 0))],
        out_specs=pl.BlockSpec((EXPERT_BLOCK, D_MODEL), lambda i, e, b, n, l: (b[i], 0)),
        scratch_shapes=[pltpu.VMEM((D_MODEL, 2 * D_EXPERT), BF16),
                        pltpu.VMEM((D_EXPERT, D_MODEL), BF16)])
    return pl.pallas_call(
        _expert_kernel, grid_spec=gs,
        out_shape=jax.ShapeDtypeStruct((n_rows, D_MODEL), F32),
        compiler_params=_cparams(("arbitrary",)),
        name="expert",
    )(bexp, bidx, new, live, x_sorted, w_gu, w_down)


def _combine_kernel(dest_ref, row_ref, h2_ref, x1_ref, wt_ref, mod_ref, wsg_ref, wsd_ref, g_ref, b_ref, y_hbm,
                    op_ref, os_ref, ybuf, sem, *, tm, n_tokens, n_prompt_tiles, alpha):
    i = pl.program_id(0)

    def row_copy(r, k):
        d = dest_ref[k * n_tokens + i * tm + r]
        return pltpu.make_async_copy(y_hbm.at[pl.ds(d, 1)], ybuf.at[k, pl.ds(r, 1)], sem)

    def start(r, c):
        for k in range(TOP_K):
            row_copy(r, k).start()
        return c

    def wait(r, c):
        for k in range(TOP_K):
            row_copy(r, k).wait()
        return c

    lax.fori_loop(0, tm, start, 0)
    h2 = h2_ref[...].astype(BF16)
    su = jnp.dot(h2, wsg_ref[...], preferred_element_type=F32)
    act = (_silu(su[:, 0:D_SHARED]) * su[:, D_SHARED:]).astype(BF16)
    moe = jnp.dot(act, wsd_ref[...], preferred_element_type=F32)
    lax.fori_loop(0, tm, wait, 0)
    wt = wt_ref[...]
    for k in range(TOP_K):
        moe = moe + ybuf[k] * wt[:, k:k + 1]
    g2 = mod_ref[pl.ds(row_ref[i], 1), 5 * D_MODEL:6 * D_MODEL]
    out = _ln_rows(alpha * x1_ref[...] + g2 * moe) * g_ref[...] + b_ref[...]

    @pl.when(i < n_prompt_tiles)
    def _():
        op_ref[...] = out

    @pl.when(i >= n_prompt_tiles)
    def _():
        os_ref[...] = out


def _combine(lay, dest_flat, h2, x1, wts_tok, mod, w_sh_gu, w_sh_down, ln_g, ln_b, y_sorted, alpha, tm):
    T = lay.n_tokens
    row, _, _ = lay.token_tile_tables(tm)
    npt = lay.n_prompt_tokens // tm

    def cur(i, d, r):
        return (i, 0)

    def const(i, d, r):
        return (0, 0)

    gs = pltpu.PrefetchScalarGridSpec(
        num_scalar_prefetch=2,
        grid=(T // tm,),
        in_specs=[pl.BlockSpec((tm, D_MODEL), cur),
                  pl.BlockSpec((tm, D_MODEL), cur),
                  pl.BlockSpec((tm, TOP_K), cur),
                  pl.BlockSpec((8, 6 * D_MODEL), const),
                  pl.BlockSpec((D_MODEL, 2 * D_SHARED), const),
                  pl.BlockSpec((D_SHARED, D_MODEL), const),
                  pl.BlockSpec((1, D_MODEL), const),
                  pl.BlockSpec((1, D_MODEL), const),
                  pl.BlockSpec(memory_space=pl.ANY)],
        out_specs=[pl.BlockSpec((tm, D_MODEL), lambda i, d, r: (jnp.minimum(i, npt - 1), 0)),
                   pl.BlockSpec((tm, D_MODEL), lambda i, d, r: (jnp.maximum(i - npt, 0), 0))],
        scratch_shapes=[pltpu.VMEM((TOP_K, tm, D_MODEL), F32),
                        pltpu.SemaphoreType.DMA(())])
    return pl.pallas_call(
        functools.partial(_combine_kernel, tm=tm, n_tokens=T, n_prompt_tiles=npt, alpha=alpha), grid_spec=gs,
        out_shape=(jax.ShapeDtypeStruct((lay.n_prompt_tokens, D_MODEL), F32),
                   jax.ShapeDtypeStruct((T - lay.n_prompt_tokens, D_MODEL), F32)),
        compiler_params=_cparams(("arbitrary",)),
        name="combine",
    )(dest_flat, row, h2, x1, wts_tok, mod, w_sh_gu, w_sh_down, ln_g.reshape(1, -1), ln_b.reshape(1, -1), y_sorted)


class _Layout:
    def __init__(self, n_prompt_seqs, prompt_len, n_sample_seqs, sample_len):
        self.n_prompt_seqs, self.prompt_len = n_prompt_seqs, prompt_len
        self.n_sample_seqs, self.sample_len = n_sample_seqs, sample_len
        self.n_prompt_tokens = n_prompt_seqs * prompt_len
        self.n_tokens = self.n_prompt_tokens + n_sample_seqs * sample_len
        assert prompt_len % CONV_TILE == 0 and sample_len % CONV_TILE == 0
        self.max_chunks = max(prompt_len, sample_len) // CHUNK

    def token_tile_tables(self, tm):
        assert self.n_prompt_tokens % tm == 0 and self.sample_len % tm == 0
        npt = self.n_prompt_tokens // tm
        per_seq = self.sample_len // tm
        n = self.n_tokens // tm
        row = np.zeros(n, np.int32)
        posb = np.zeros(n, np.int32)
        flag = np.zeros(n, np.int32)
        for i in range(npt, n):
            j = i - npt
            row[i] = 1 + j // per_seq
            posb[i] = j % per_seq
            flag[i] = 1
        return jnp.asarray(row), jnp.asarray(posb), jnp.asarray(flag)

    def conv_tile_tables(self):
        lok, rok = [], []
        for n_seq, length in ((self.n_prompt_seqs, self.prompt_len), (self.n_sample_seqs, self.sample_len)):
            per = length // CONV_TILE
            for _ in range(n_seq):
                for j in range(per):
                    lok.append(int(j > 0))
                    rok.append(int(j < per - 1))
        return jnp.asarray(np.array(lok, np.int32)), jnp.asarray(np.array(rok, np.int32))

    def ssd_step_tables(self):
        cols = [[] for _ in range(9)]
        seqs = []
        c0 = self.n_prompt_tokens // CHUNK
        for j in range(self.n_sample_seqs):
            nc = self.sample_len // CHUNK
            seqs.append((c0 + j * nc, nc, 0, j, 0))
        for j in range(self.n_prompt_seqs):
            nc = self.prompt_len // CHUNK
            seqs.append((j * nc, nc, 1, 0, j))
        for base, nc, zero, sin, sout in seqs:
            for phase in (0, 1):
                order = range(nc - 1, -1, -1) if phase == 0 else range(nc)
                for n, c in enumerate(order):
                    vals = (base + c, base if phase == 0 else base + c, phase, int(n == 0), int(n == nc - 1),
                            zero, sin, sout, c)
                    for col, v in zip(cols, vals):
                        col.append(v)
        return tuple(jnp.asarray(np.array(col, np.int32)) for col in cols)


def _grid_pos_embed(n_tokens):
    rows = n_tokens // GRID_W
    quarter = D_MODEL // 4
    freq = jnp.exp(-math.log(10000.0) * jnp.arange(quarter, dtype=F32) / quarter)
    r = jnp.broadcast_to(jnp.arange(rows, dtype=F32)[:, None, None] * freq, (rows, GRID_W, quarter))
    cl = jnp.broadcast_to(jnp.arange(GRID_W, dtype=F32)[None, :, None] * freq, (rows, GRID_W, quarter))
    emb = jnp.concatenate([jnp.sin(r), jnp.cos(r), jnp.sin(cl), jnp.cos(cl)], axis=-1)
    return emb.reshape(rows * GRID_W, D_MODEL)


def _moe_plan(counts, n_blocks):
    blk = EXPERT_BLOCK
    padded = (counts + blk - 1) // blk * blk
    pad_end = jnp.cumsum(padded)
    pad_start = pad_end - padded
    n_used = pad_end[-1] // blk
    b = jnp.arange(n_blocks, dtype=I32)
    live = (b < n_used).astype(I32)
    bidx = jnp.minimum(b, jnp.maximum(n_used - 1, 0)).astype(I32)
    bexp = jnp.minimum(jnp.searchsorted(pad_end, bidx * blk, side='right'), N_EXPERTS - 1).astype(I32)
    new = jnp.concatenate([jnp.ones((1,), I32), (bexp[1:] != bexp[:-1]).astype(I32)])
    last_block = jnp.where(counts > 0, pad_end // blk - 1, -1)
    return pad_start.astype(I32), last_block.astype(I32), bexp, bidx, new, live


def _layer(lay, xp, xs, pos, cond8, h0f, h0b, lp, alpha, tm_proj=512, tm_route=256, tm_disp=256, tm_comb=128):
    (w_ada, b_ada, w_in, conv_w, conv_b, conv_ln_g, conv_ln_b, ssm_conv_w, ssm_conv_b, dt_bias, a_log,
     d_skip, ssm_norm_g, w_out, ln1_g, ln1_b, w_router, router_bias, w_exp_gu, w_exp_down, w_sh_gu,
     w_sh_down, ln2_g, ln2_b) = lp
    T = lay.n_tokens
    n_main = 2 * D_CONV + D_SSM + D_XBC
    w_main = w_in[:, :n_main].astype(BF16)
    w_dt = jnp.pad(w_in[:, n_main:], ((0, 0), (0, 128 - 2 * N_HEADS))).astype(BF16)

    mod = _ada(cond8, w_ada, b_ada)
    glu, z, xbc, dt_raw = _inproj(lay, xp, xs, pos, mod, w_main, w_dt, tm_proj)
    conv_out, xbc_c = _conv(lay, glu, xbc, conv_w, conv_b, conv_ln_g, conv_ln_b, ssm_conv_w, ssm_conv_b)
    y_ssm, hf, hb = _ssd(lay, xbc_c, z, dt_raw, h0f, h0b, dt_bias, a_log, d_skip, ssm_norm_g)
    x1, h2 = _outproj(lay, xp, xs, pos, mod, conv_out, y_ssm, w_out.astype(BF16), ln1_g, ln1_b, alpha, tm_proj)

    idx, wts, posn, cnt = _route(h2, w_router.T, router_bias, tm_route)
    n_blocks = -(-T * TOP_K // EXPERT_BLOCK) + N_EXPERTS
    pad_start, last_block, bexp, bidx, new, live = _moe_plan(cnt[:, 0], n_blocks)
    dest = _dest(idx, posn, pad_start, 512).reshape(-1)
    x_sorted = _dispatch(h2, dest, last_block, n_blocks * EXPERT_BLOCK, tm_disp)
    y_sorted = _expert(x_sorted, w_exp_gu, w_exp_down, bexp, bidx, new, live, n_blocks)
    out_p, out_s = _combine(lay, dest, h2, x1, wts.T, mod, w_sh_gu.astype(BF16), w_sh_down.astype(BF16),
                            ln2_g, ln2_b, y_sorted, alpha, tm_comb)
    return out_p, out_s, hf, hb


def kernel(x_prompt, x_sample, state_ssd_fwd, state_ssd_bwd, c, c_ctx, w_ada, b_ada, w_in, conv_w, conv_b, conv_ln_g, conv_ln_b, ssm_conv_w, ssm_conv_b, dt_bias, a_log, d_skip, ssm_norm_g, w_out, ln1_g, ln1_b, w_router, router_bias, w_exp_gu, w_exp_down, w_sh_gu, w_sh_down, ln2_g, ln2_b):
    depth = w_ada.shape[0]
    assert depth == 1, "the prompt and latent passes are fused per layer; one layer is supported"
    bp, lp_, _ = x_prompt.shape
    bd, ld, _ = x_sample.shape
    lay = _Layout(bp, lp_, bd, ld)
    alpha = (2.0 * depth) ** 0.25
    stacked = (w_ada, b_ada, w_in, conv_w, conv_b, conv_ln_g, conv_ln_b, ssm_conv_w, ssm_conv_b,
               dt_bias, a_log, d_skip, ssm_norm_g, w_out, ln1_g, ln1_b, w_router, router_bias,
               w_exp_gu, w_exp_down, w_sh_gu, w_sh_down, ln2_g, ln2_b)
    lp = [w[0] for w in stacked]
    cond8 = jnp.concatenate([c_ctx[None, :], c, jnp.zeros((8 - 1 - bd, D_MODEL), F32)], axis=0)
    pos = _grid_pos_embed(ld)
    sshape = (bd, N_HEADS, HEADDIM, D_STATE)
    out_p, out_s, hf, hb = _layer(lay, x_prompt.reshape(bp * lp_, D_MODEL), x_sample.reshape(bd * ld, D_MODEL),
                                  pos, cond8, state_ssd_fwd[:, 0].reshape(sshape),
                                  state_ssd_bwd[:, 0].reshape(sshape), lp, alpha)
    return (out_p.reshape(bp, lp_, D_MODEL), out_s.reshape(bd, ld, D_MODEL),
            hf[:, None], hb[:, None])
```

```python
import functools
import math

import numpy as np
import jax
import jax.numpy as jnp
from jax import lax
from jax.experimental import pallas as pl
from jax.experimental.pallas import tpu as pltpu

F32 = jnp.float32
BF16 = jnp.bfloat16
I32 = jnp.int32
HI = lax.Precision.HIGHEST

D_MODEL = 1024
GRID_W = 64
D_CONV = 1024
CONV_K = 31
N_HEADS = 16
HEADDIM = 64
D_SSM = N_HEADS * HEADDIM
N_GROUPS = 4
HEADS_PER_GROUP = N_HEADS // N_GROUPS
D_STATE = 128
SSM_CONV_K = 4
CHUNK = 128
D_XBC = D_SSM + 2 * N_GROUPS * D_STATE
N_EXPERTS = 256
TOP_K = 8
N_EXPERT_GROUPS = 8
EXPERTS_PER_GROUP = N_EXPERTS // N_EXPERT_GROUPS
TOPK_GROUPS = 4
D_EXPERT = 256
D_SHARED = 256
ROUTED_SCALE = 2.5
LN_EPS = 1e-5

CONV_TILE = 256
HALO = 16
EXPERT_BLOCK = 256
VMEM_LIMIT = 56 * 1024 * 1024


def _cparams(sem, vmem=VMEM_LIMIT):
    return pltpu.CompilerParams(dimension_semantics=sem, vmem_limit_bytes=vmem)


def _silu(x):
    return x * jax.nn.sigmoid(x)


def _ln_rows(x):
    mu = jnp.mean(x, axis=-1, keepdims=True)
    xc = x - mu
    var = jnp.mean(xc * xc, axis=-1, keepdims=True)
    return xc * lax.rsqrt(var + LN_EPS)


def _iota(shape, dim):
    return lax.broadcasted_iota(I32, shape, dim)


def _expand_matrix(n_in, width):
    return (_iota((n_in, n_in * width), 0) == _iota((n_in, n_in * width), 1) // width).astype(F32)


def _dot_hi(a, b):
    return jnp.dot(a, b, precision=HI, preferred_element_type=F32)


def _split3(x):
    hi = x.astype(BF16)
    r1 = x - hi.astype(F32)
    mid = r1.astype(BF16)
    lo = (r1 - mid.astype(F32)).astype(BF16)
    return jnp.concatenate([hi, mid, lo], axis=1)


def _expand3(n, width):
    rows = np.arange(3 * n)[:, None] % n
    cols = np.arange(n * width)[None, :] // width
    return jnp.asarray(rows == cols, dtype=BF16)


def _expand_exact(x, e3):
    return jnp.dot(_split3(x), e3, preferred_element_type=F32)


def _ada_kernel(c_ref, w_ref, b_ref, o_ref):
    o_ref[...] = _dot_hi(_silu(c_ref[...]), w_ref[...]) + b_ref[...]


def _ada(cond8, w_ada, b_ada):
    n = w_ada.shape[1]
    tn = 1024
    return pl.pallas_call(
        _ada_kernel,
        grid=(n // tn,),
        in_specs=[pl.BlockSpec((8, D_MODEL), lambda j: (0, 0)),
                  pl.BlockSpec((D_MODEL, tn), lambda j: (0, j)),
                  pl.BlockSpec((1, tn), lambda j: (0, j))],
        out_specs=pl.BlockSpec((8, tn), lambda j: (0, j)),
        out_shape=jax.ShapeDtypeStruct((8, n), F32),
        compiler_params=_cparams(("arbitrary",)),
        name="ada",
    )(cond8, w_ada, b_ada.reshape(1, n))


def _inproj_kernel(row_ref, posb_ref, flag_ref, xp_ref, xs_ref, pos_ref, mod_ref, wm_ref, wdt_ref,
                   glu_ref, z_ref, xbc_ref, dt_ref):
    i = pl.program_id(0)
    x = jnp.where(flag_ref[i] == 1, xs_ref[...] + pos_ref[...], xp_ref[...])
    r = row_ref[i]
    sh1 = mod_ref[pl.ds(r, 1), 0:D_MODEL]
    sc1 = mod_ref[pl.ds(r, 1), D_MODEL:2 * D_MODEL]
    h = (_ln_rows(x) * (1.0 + sc1) + sh1).astype(BF16)
    glu_ref[...] = jnp.dot(h, wm_ref[:, 0:2 * D_CONV], preferred_element_type=F32).astype(BF16)
    z_ref[...] = jnp.dot(h, wm_ref[:, 2 * D_CONV:2 * D_CONV + D_SSM], preferred_element_type=F32).astype(BF16)
    xbc_ref[...] = jnp.dot(h, wm_ref[:, 2 * D_CONV + D_SSM:], preferred_element_type=F32).astype(BF16)
    dt_ref[...] = jnp.dot(h, wdt_ref[...], preferred_element_type=F32)


def _inproj(lay, xp, xs, pos, mod, w_main, w_dt, tm):
    T = lay.n_tokens
    row, posb, flag = lay.token_tile_tables(tm)
    npt = lay.n_prompt_tokens // tm
    n_main = w_main.shape[1]
    gs = pltpu.PrefetchScalarGridSpec(
        num_scalar_prefetch=3,
        grid=(T // tm,),
        in_specs=[pl.BlockSpec((tm, D_MODEL), lambda i, r, p, f: (jnp.minimum(i, npt - 1), 0)),
                  pl.BlockSpec((tm, D_MODEL), lambda i, r, p, f: (jnp.maximum(i - npt, 0), 0)),
                  pl.BlockSpec((tm, D_MODEL), lambda i, r, p, f: (p[i], 0)),
                  pl.BlockSpec((8, 6 * D_MODEL), lambda i, r, p, f: (0, 0)),
                  pl.BlockSpec((D_MODEL, n_main), lambda i, r, p, f: (0, 0)),
                  pl.BlockSpec((D_MODEL, 128), lambda i, r, p, f: (0, 0))],
        out_specs=[pl.BlockSpec((tm, 2 * D_CONV), lambda i, r, p, f: (i, 0)),
                   pl.BlockSpec((tm, D_SSM), lambda i, r, p, f: (i, 0)),
                   pl.BlockSpec((tm, D_XBC), lambda i, r, p, f: (i, 0)),
                   pl.BlockSpec((tm, 128), lambda i, r, p, f: (i, 0))])
    return pl.pallas_call(
        _inproj_kernel, grid_spec=gs,
        out_shape=(jax.ShapeDtypeStruct((T, 2 * D_CONV), BF16),
                   jax.ShapeDtypeStruct((T, D_SSM), BF16),
                   jax.ShapeDtypeStruct((T, D_XBC), BF16),
                   jax.ShapeDtypeStruct((T, 128), F32)),
        compiler_params=_cparams(("arbitrary",)),
        name="inproj",
    )(row, posb, flag, xp, xs, pos, mod, w_main, w_dt)


_N_SHIFT = 8
_SHIFT_ROWS = CONV_TILE + 2 * HALO - _N_SHIFT
_ROW_BLOCK = 32


def _conv_kernel(lok_ref, rok_ref, glu_ref, glul_ref, glur_ref, xbc_ref, xbcl_ref, xbcr_ref,
                 cw_ref, cb_ref, lng_ref, lnb_ref, sw_ref, sb_ref, co_ref, xo_ref,
                 ext_ref, sh_ref, acc_ref, ext2_ref):
    i = pl.program_id(0)
    lok = lok_ref[i] == 1
    rok = rok_ref[i] == 1

    def glu(ref):
        v = ref[...].astype(F32)
        return v[:, 0:D_CONV] * jax.nn.sigmoid(v[:, D_CONV:])

    ext_ref[0:HALO, :] = jnp.where(lok, glu(glul_ref), 0.0)
    ext_ref[HALO:HALO + CONV_TILE, :] = glu(glu_ref)
    ext_ref[HALO + CONV_TILE:, :] = jnp.where(rok, glu(glur_ref), 0.0)
    for r in range(_N_SHIFT):
        sh_ref[r] = ext_ref[r:r + _SHIFT_ROWS, :]

    first = HALO - (CONV_K - 1) // 2

    def row_block(rb, carry):
        base = pl.multiple_of(rb * _ROW_BLOCK, _ROW_BLOCK)
        for j in range(D_CONV // 128):
            lanes = slice(j * 128, (j + 1) * 128)
            acc = jnp.zeros((_ROW_BLOCK, 128), F32) + cb_ref[:, lanes]
            for k in range(CONV_K):
                o = first + k
                win = sh_ref[o % _N_SHIFT, pl.ds(base + (o // _N_SHIFT) * _N_SHIFT, _ROW_BLOCK), lanes]
                acc = acc + win * cw_ref[k:k + 1, lanes]
            acc_ref[pl.ds(base, _ROW_BLOCK), lanes] = acc
        return carry

    lax.fori_loop(0, CONV_TILE // _ROW_BLOCK, row_block, 0)
    u = _ln_rows(acc_ref[...]) * lng_ref[...] + lnb_ref[...]
    co_ref[...] = _silu(u).astype(BF16)

    ext2_ref[0:HALO, :] = jnp.where(lok, xbcl_ref[...].astype(F32), 0.0)
    ext2_ref[HALO:HALO + CONV_TILE, :] = xbc_ref[...].astype(F32)
    ext2_ref[HALO + CONV_TILE:, :] = jnp.where(rok, xbcr_ref[...].astype(F32), 0.0)
    first2 = HALO - (SSM_CONV_K - 1) // 2
    half = D_XBC // 2
    for hcol in range(2):
        lanes = slice(hcol * half, (hcol + 1) * half)
        y = jnp.zeros((CONV_TILE, half), F32) + sb_ref[:, lanes]
        for k in range(SSM_CONV_K):
            y = y + ext2_ref[first2 + k:first2 + k + CONV_TILE, lanes] * sw_ref[k:k + 1, lanes]
        xo_ref[:, lanes] = _silu(y).astype(BF16)


def _conv(lay, glu, xbc, conv_w, conv_b, ln_g, ln_b, ssm_w, ssm_b):
    T = lay.n_tokens
    lok, rok = lay.conv_tile_tables()
    n_tiles = T // CONV_TILE
    hb = CONV_TILE // HALO
    n_hb = T // HALO

    def cur(i, l, r):
        return (i, 0)

    def left(i, l, r):
        return (jnp.maximum(i * hb - 1, 0), 0)

    def right(i, l, r):
        return (jnp.minimum((i + 1) * hb, n_hb - 1), 0)

    def const(i, l, r):
        return (0, 0)

    gs = pltpu.PrefetchScalarGridSpec(
        num_scalar_prefetch=2,
        grid=(n_tiles,),
        in_specs=[pl.BlockSpec((CONV_TILE, 2 * D_CONV), cur),
                  pl.BlockSpec((HALO, 2 * D_CONV), left),
                  pl.BlockSpec((HALO, 2 * D_CONV), right),
                  pl.BlockSpec((CONV_TILE, D_XBC), cur),
                  pl.BlockSpec((HALO, D_XBC), left),
                  pl.BlockSpec((HALO, D_XBC), right),
                  pl.BlockSpec((CONV_K, D_CONV), const),
                  pl.BlockSpec((1, D_CONV), const),
                  pl.BlockSpec((1, D_CONV), const),
                  pl.BlockSpec((1, D_CONV), const),
                  pl.BlockSpec((SSM_CONV_K, D_XBC), const),
                  pl.BlockSpec((1, D_XBC), const)],
        out_specs=[pl.BlockSpec((CONV_TILE, D_CONV), cur),
                   pl.BlockSpec((CONV_TILE, D_XBC), cur)],
        scratch_shapes=[pltpu.VMEM((CONV_TILE + 2 * HALO, D_CONV), F32),
                        pltpu.VMEM((_N_SHIFT, _SHIFT_ROWS, D_CONV), F32),
                        pltpu.VMEM((CONV_TILE, D_CONV), F32),
                        pltpu.VMEM((CONV_TILE + 2 * HALO, D_XBC), F32)])
    return pl.pallas_call(
        _conv_kernel, grid_spec=gs,
        out_shape=(jax.ShapeDtypeStruct((T, D_CONV), BF16),
                   jax.ShapeDtypeStruct((T, D_XBC), BF16)),
        compiler_params=_cparams(("arbitrary",)),
        name="conv",
    )(lok, rok, glu, glu, glu, xbc, xbc, xbc, conv_w, conv_b.reshape(1, -1), ln_g.reshape(1, -1),
      ln_b.reshape(1, -1), ssm_w, ssm_b.reshape(1, -1))


_BN = N_GROUPS * D_STATE


def _ssd_kernel(chunk_ref, yidx_ref, phase_ref, first_ref, last_ref, zero_ref, sin_ref, sout_ref, cloc_ref,
                xbc_ref, z_ref, dt_ref, h0f_ref, h0b_ref, dtb_ref, alog_ref, dsk_ref, ng_ref,
                tri_ref, edec_ref, ewb_ref, ecol_ref, ewide_ref, eye3_ref,
                y_ref, hf_out_ref, hb_out_ref,
                hf_ref, g_ref, gin_ref, ybuf_ref):
    s = pl.program_id(0)
    phase = phase_ref[s]
    first = first_ref[s] == 1
    last = last_ref[s] == 1
    zero = zero_ref[s] == 1
    cloc = cloc_ref[s]
    H, P, N = N_HEADS, HEADDIM, D_STATE

    GW = HEADS_PER_GROUP * P
    xs = xbc_ref[:, 0:D_SSM]
    dt = dt_ref[:, 0:2 * H] + dtb_ref[...]
    dt = jnp.maximum(dt, 0.0) + jnp.log1p(jnp.exp(-jnp.abs(dt)))
    a = dt * (-jnp.exp(alog_ref[...]))
    a3 = jnp.dot(tri_ref[...], _split3(a), preferred_element_type=F32)
    acs = a3[:, 0:2 * H] + a3[:, 2 * H:4 * H] + a3[:, 4 * H:6 * H]
    tot = acs[CHUNK - 8:CHUNK, :]
    dec = _expand_exact(jnp.exp(tot), edec_ref[...])[7:8, :]
    exb = acs[:, H:2 * H] - a[:, H:2 * H]

    def load_state(src_ref, dst_ref):
        for j in range(H // 2):
            pair = jnp.concatenate([src_ref[0, 2 * j], src_ref[0, 2 * j + 1]], axis=0)
            dst_ref[:, 2 * j * P:(2 * j + 2) * P] = jnp.where(zero, 0.0, pair.T)

    def store_state(src_ref, dst_ref):
        for j in range(H // 2):
            pair = src_ref[:, 2 * j * P:(2 * j + 2) * P].T
            dst_ref[0, 2 * j] = pair[0:P]
            dst_ref[0, 2 * j + 1] = pair[P:2 * P]

    @pl.when(phase == 0)
    def _backward_states():
        @pl.when(first)
        def _():
            load_state(h0b_ref, g_ref)

        wb = dt[:, H:2 * H] * jnp.exp(exb)
        xw = (xs.astype(F32) * _expand_exact(wb, ewb_ref[...])).astype(BF16)
        for g in range(N_GROUPS):
            cols = slice(g * GW, (g + 1) * GW)
            bg = xbc_ref[:, D_SSM + g * N:D_SSM + (g + 1) * N]
            gg = g_ref[:, cols]
            gin_ref[cloc, :, cols] = gg.astype(BF16)
            upd = lax.dot_general(bg, xw[:, cols], (((0,), (0,)), ((), ())), preferred_element_type=F32)
            g_ref[:, cols] = gg * dec[:, D_SSM + g * GW:D_SSM + (g + 1) * GW] + upd

        @pl.when(last)
        def _():
            store_state(g_ref, hb_out_ref)

    @pl.when(phase == 1)
    def _forward_and_outputs():
        @pl.when(first)
        def _():
            load_state(h0f_ref, hf_ref)

        acsf = acs[:, 0:H]
        dtf = dt[:, 0:H]
        dtb = dt[:, H:2 * H]
        totf = acs[CHUNK - 1:CHUNK, 0:H]
        totb = acs[CHUNK - 1:CHUNK, H:2 * H]
        col = _expand_exact(jnp.concatenate([acsf, exb], axis=1), ecol_ref[...])
        q3 = _split3(jnp.concatenate([acsf, exb, dtf, dtb], axis=1))
        qt = lax.dot_general(eye3_ref[...], q3, (((1,), (1,)), ((), ())),
                             preferred_element_type=F32)
        wide = jnp.concatenate([dtf * jnp.exp(totf - acsf), jnp.exp(acsf), jnp.exp(totb - exb)], axis=1)
        wide = _expand_exact(wide, ewide_ref[...])
        xsf = xs.astype(F32)
        xw = (xsf * wide[:, 0:D_SSM]).astype(BF16)
        lower = _iota((CHUNK, CHUNK), 1) <= _iota((CHUNK, CHUNK), 0)
        upper = _iota((CHUNK, CHUNK), 1) >= _iota((CHUNK, CHUNK), 0)
        for g in range(N_GROUPS):
            cols = slice(g * GW, (g + 1) * GW)
            bg = xbc_ref[:, D_SSM + g * N:D_SSM + (g + 1) * N]
            cg = xbc_ref[:, D_SSM + _BN + g * N:D_SSM + _BN + (g + 1) * N]
            cb = lax.dot_general(cg, bg, (((1,), (1,)), ((), ())), preferred_element_type=F32)
            hfg = hf_ref[:, cols]
            yf = jnp.dot(cg, hfg.astype(BF16), preferred_element_type=F32)
            yb = jnp.dot(cg, gin_ref[cloc, :, cols], preferred_element_type=F32)
            ybuf_ref[:, cols] = yf * wide[:, D_SSM + g * GW:D_SSM + (g + 1) * GW] \
                + yb * wide[:, 2 * D_SSM + g * GW:2 * D_SSM + (g + 1) * GW]
            upd = lax.dot_general(bg, xw[:, cols], (((0,), (0,)), ((), ())), preferred_element_type=F32)
            hf_ref[:, cols] = hfg * dec[:, cols] + upd
            for r in range(HEADS_PER_GROUP):
                h = g * HEADS_PER_GROUP + r
                colf = col[:, h * N:(h + 1) * N]
                colb = col[:, (H + h) * N:(H + h + 1) * N]
                mf = jnp.where(lower, jnp.exp(colf - qt[h:h + 1, :]), 0.0) * qt[2 * H + h:2 * H + h + 1, :]
                mb = jnp.where(upper, jnp.exp(qt[H + h:H + h + 1, :] - colb), 0.0) * qt[3 * H + h:3 * H + h + 1, :]
                m = (cb * (mf + mb)).astype(BF16)
                hs = slice(h * P, (h + 1) * P)
                ybuf_ref[:, hs] += jnp.dot(m, xs[:, hs], preferred_element_type=F32)

        yt = (ybuf_ref[...] + dsk_ref[...] * xsf) * _silu(z_ref[...].astype(F32))
        gw = D_SSM // N_GROUPS
        for g in range(N_GROUPS):
            seg = yt[:, g * gw:(g + 1) * gw]
            ms = jnp.mean(seg * seg, axis=-1, keepdims=True)
            y_ref[:, g * gw:(g + 1) * gw] = (seg * lax.rsqrt(ms + LN_EPS) * ng_ref[:, g * gw:(g + 1) * gw]).astype(BF16)

        @pl.when(last)
        def _():
            store_state(hf_ref, hf_out_ref)


def _ssd(lay, xbc_c, z, dt_raw, h0f, h0b, dt_bias, a_log, d_skip, norm_g):
    T = lay.n_tokens
    tabs = lay.ssd_step_tables()
    n_steps = tabs[0].shape[0]
    nsp = len(tabs)

    def by_chunk(s, *t):
        return (t[0][s], 0)

    def by_y(s, *t):
        return (t[1][s], 0)

    def by_sin(s, *t):
        return (t[6][s], 0, 0, 0)

    def by_sout(s, *t):
        return (t[7][s], 0, 0, 0)

    def const(s, *t):
        return (0, 0)

    H = N_HEADS
    tri = jnp.asarray(np.tril(np.ones((CHUNK, CHUNK))), dtype=BF16)
    eye3 = jnp.asarray(np.arange(4 * H)[:, None] == np.arange(12 * H)[None, :] % (4 * H), dtype=BF16)
    consts = [tri, _expand3(2 * H, HEADDIM), _expand3(H, HEADDIM), _expand3(2 * H, D_STATE),
              _expand3(3 * H, HEADDIM), eye3]
    sshape = (1, N_HEADS, HEADDIM, D_STATE)
    gs = pltpu.PrefetchScalarGridSpec(
        num_scalar_prefetch=nsp,
        grid=(n_steps,),
        in_specs=[pl.BlockSpec((CHUNK, D_XBC), by_chunk),
                  pl.BlockSpec((CHUNK, D_SSM), by_chunk),
                  pl.BlockSpec((CHUNK, 128), by_chunk),
                  pl.BlockSpec(sshape, by_sin),
                  pl.BlockSpec(sshape, by_sin),
                  pl.BlockSpec((1, 2 * N_HEADS), const),
                  pl.BlockSpec((1, 2 * N_HEADS), const),
                  pl.BlockSpec((1, D_SSM), const),
                  pl.BlockSpec((1, D_SSM), const)] + [pl.BlockSpec(c.shape, const) for c in consts],
        out_specs=[pl.BlockSpec((CHUNK, D_SSM), by_y),
                   pl.BlockSpec(sshape, by_sout),
                   pl.BlockSpec(sshape, by_sout)],
        scratch_shapes=[pltpu.VMEM((D_STATE, D_SSM), F32),
                        pltpu.VMEM((D_STATE, D_SSM), F32),
                        pltpu.VMEM((lay.max_chunks, D_STATE, D_SSM), BF16),
                        pltpu.VMEM((CHUNK, D_SSM), F32)])
    n_out = lay.n_prompt_seqs
    return pl.pallas_call(
        _ssd_kernel, grid_spec=gs,
        out_shape=(jax.ShapeDtypeStruct((T, D_SSM), BF16),
                   jax.ShapeDtypeStruct((n_out,) + sshape[1:], F32),
                   jax.ShapeDtypeStruct((n_out,) + sshape[1:], F32)),
        compiler_params=_cparams(("arbitrary",)),
        name="ssd",
    )(*tabs, xbc_c, z, dt_raw, h0f, h0b, dt_bias.reshape(1, -1), a_log.reshape(1, -1),
      jnp.repeat(d_skip, HEADDIM).reshape(1, -1), norm_g.reshape(1, -1), *consts)


def _outproj_kernel(row_ref, posb_ref, flag_ref, xp_ref, xs_ref, pos_ref, mod_ref, co_ref, ys_ref, wo_ref,
                    g_ref, b_ref, x1_ref, h2_ref, *, alpha):
    i = pl.program_id(0)
    x = jnp.where(flag_ref[i] == 1, xs_ref[...] + pos_ref[...], xp_ref[...])
    r = row_ref[i]
    g1 = mod_ref[pl.ds(r, 1), 2 * D_MODEL:3 * D_MODEL]
    sh2 = mod_ref[pl.ds(r, 1), 3 * D_MODEL:4 * D_MODEL]
    sc2 = mod_ref[pl.ds(r, 1), 4 * D_MODEL:5 * D_MODEL]
    mix = jnp.dot(co_ref[...], wo_ref[0:D_CONV, :], preferred_element_type=F32) \
        + jnp.dot(ys_ref[...], wo_ref[D_CONV:, :], preferred_element_type=F32)
    x1 = _ln_rows(alpha * x + g1 * mix) * g_ref[...] + b_ref[...]
    x1_ref[...] = x1
    h2_ref[...] = _ln_rows(x1) * (1.0 + sc2) + sh2


def _outproj(lay, xp, xs, pos, mod, conv_out, y_ssm, w_out, ln_g, ln_b, alpha, tm):
    T = lay.n_tokens
    row, posb, flag = lay.token_tile_tables(tm)
    npt = lay.n_prompt_tokens // tm

    def const(i, r, p, f):
        return (0, 0)

    def cur(i, r, p, f):
        return (i, 0)

    gs = pltpu.PrefetchScalarGridSpec(
        num_scalar_prefetch=3,
        grid=(T // tm,),
        in_specs=[pl.BlockSpec((tm, D_MODEL), lambda i, r, p, f: (jnp.minimum(i, npt - 1), 0)),
                  pl.BlockSpec((tm, D_MODEL), lambda i, r, p, f: (jnp.maximum(i - npt, 0), 0)),
                  pl.BlockSpec((tm, D_MODEL), lambda i, r, p, f: (p[i], 0)),
                  pl.BlockSpec((8, 6 * D_MODEL), const),
                  pl.BlockSpec((tm, D_CONV), cur),
                  pl.BlockSpec((tm, D_SSM), cur),
                  pl.BlockSpec((D_CONV + D_SSM, D_MODEL), const),
                  pl.BlockSpec((1, D_MODEL), const),
                  pl.BlockSpec((1, D_MODEL), const)],
        out_specs=[pl.BlockSpec((tm, D_MODEL), cur),
                   pl.BlockSpec((tm, D_MODEL), cur)])
    return pl.pallas_call(
        functools.partial(_outproj_kernel, alpha=alpha), grid_spec=gs,
        out_shape=(jax.ShapeDtypeStruct((T, D_MODEL), F32),
                   jax.ShapeDtypeStruct((T, D_MODEL), F32)),
        compiler_params=_cparams(("arbitrary",)),
        name="outproj",
    )(row, posb, flag, xp, xs, pos, mod, conv_out, y_ssm, w_out, ln_g.reshape(1, -1), ln_b.reshape(1, -1))


def _route_kernel(h2_ref, wrt_ref, bias_ref, idx_ref, wts_ref, pos_ref, cnt_ref, carry_ref, *, tm):
    i = pl.program_id(0)

    @pl.when(i == 0)
    def _():
        carry_ref[...] = jnp.zeros_like(carry_ref)

    E, NG, EG = N_EXPERTS, N_EXPERT_GROUPS, EXPERTS_PER_GROUP
    neg = -jnp.inf
    logits = lax.dot_general(wrt_ref[...], h2_ref[...], (((1,), (1,)), ((), ())), precision=HI,
                             preferred_element_type=F32)
    s = jax.nn.sigmoid(logits)
    sel = s + bias_ref[...]
    sel3 = sel.reshape(NG, EG, tm)
    io3 = _iota((NG, EG, tm), 1)
    m1 = jnp.max(sel3, axis=1, keepdims=True)
    f1 = jnp.min(jnp.where(sel3 == m1, io3, EG), axis=1, keepdims=True)
    m2 = jnp.max(jnp.where(io3 == f1, neg, sel3), axis=1, keepdims=True)
    gscore = (m1 + m2).reshape(NG, tm)
    gio = _iota((NG, tm), 0)
    beaten = jnp.zeros((NG, tm), I32)
    for g in range(NG):
        row = gscore[g:g + 1, :]
        beats = jnp.where(row > gscore, 1, jnp.where(row == gscore, jnp.where(g < gio, 1, 0), 0))
        beaten = beaten + beats
    keep = (beaten < TOPK_GROUPS).astype(F32).reshape(NG, 1, tm)
    selm = jnp.where(keep > 0.5, sel3, neg).reshape(E, tm)
    eio = _iota((E, tm), 0)
    chosen = jnp.zeros((E, tm), F32)
    idxs, ws = [], []
    for k in range(TOP_K):
        m = jnp.max(selm, axis=0, keepdims=True)
        am = jnp.minimum(jnp.min(jnp.where(selm == m, eio, E), axis=0, keepdims=True), E - 1)
        hit = eio == am
        ws.append(jnp.sum(jnp.where(hit, s, 0.0), axis=0, keepdims=True))
        idxs.append(am)
        selm = jnp.where(hit, neg, selm)
        chosen = jnp.where(hit, 1.0, chosen)
    wsum = ws[0]
    for k in range(1, TOP_K):
        wsum = wsum + ws[k]
    before = (_iota((tm, tm), 0) < _iota((tm, tm), 1)).astype(BF16)
    prior = jnp.dot(chosen.astype(BF16), before, preferred_element_type=F32)
    carry = carry_ref[...]
    prior = prior + jnp.concatenate([carry] * (tm // 128), axis=1)
    for k in range(TOP_K):
        idx_ref[k:k + 1, :] = idxs[k]
        wts_ref[k:k + 1, :] = ws[k] / wsum * ROUTED_SCALE
        pos_ref[k:k + 1, :] = jnp.sum(jnp.where(eio == idxs[k], prior, 0.0), axis=0, keepdims=True).astype(I32)
    total = jnp.dot(chosen.astype(BF16), jnp.ones((tm, 128), BF16), preferred_element_type=F32)
    carry = carry + total
    carry_ref[...] = carry
    cnt_ref[...] = carry.astype(I32)


def _route(h2, w_router_t, router_bias, tm):
    T = h2.shape[0]
    bias_b = jnp.broadcast_to(router_bias.astype(F32)[:, None], (N_EXPERTS, tm))
    return pl.pallas_call(
        functools.partial(_route_kernel, tm=tm),
        grid=(T // tm,),
        in_specs=[pl.BlockSpec((tm, D_MODEL), lambda i: (i, 0)),
                  pl.BlockSpec((N_EXPERTS, D_MODEL), lambda i: (0, 0)),
                  pl.BlockSpec((N_EXPERTS, tm), lambda i: (0, 0))],
        out_specs=[pl.BlockSpec((TOP_K, tm), lambda i: (0, i)),
                   pl.BlockSpec((TOP_K, tm), lambda i: (0, i)),
                   pl.BlockSpec((TOP_K, tm), lambda i: (0, i)),
                   pl.BlockSpec((N_EXPERTS, 128), lambda i: (0, 0))],
        out_shape=(jax.ShapeDtypeStruct((TOP_K, T), I32),
                   jax.ShapeDtypeStruct((TOP_K, T), F32),
                   jax.ShapeDtypeStruct((TOP_K, T), I32),
                   jax.ShapeDtypeStruct((N_EXPERTS, 128), I32)),
        scratch_shapes=[pltpu.VMEM((N_EXPERTS, 128), F32)],
        compiler_params=_cparams(("arbitrary",)),
        name="route",
    )(h2, w_router_t, bias_b)


def _dest_kernel(idx_ref, pos_ref, start_ref, dest_ref):
    tm = idx_ref.shape[1]
    eio = _iota((N_EXPERTS, tm), 0)
    start = start_ref[...]
    for k in range(TOP_K):
        base = jnp.sum(jnp.where(eio == idx_ref[k:k + 1, :], start, 0.0), axis=0, keepdims=True)
        dest_ref[k:k + 1, :] = base.astype(I32) + pos_ref[k:k + 1, :]


def _dest(idx, pos, pad_start, tm):
    T = idx.shape[1]
    start_b = jnp.broadcast_to(pad_start.astype(F32)[:, None], (N_EXPERTS, tm))
    return pl.pallas_call(
        _dest_kernel,
        grid=(T // tm,),
        in_specs=[pl.BlockSpec((TOP_K, tm), lambda i: (0, i)),
                  pl.BlockSpec((TOP_K, tm), lambda i: (0, i)),
                  pl.BlockSpec((N_EXPERTS, tm), lambda i: (0, 0))],
        out_specs=pl.BlockSpec((TOP_K, tm), lambda i: (0, i)),
        out_shape=jax.ShapeDtypeStruct((TOP_K, T), I32),
        compiler_params=_cparams(("arbitrary",)),
        name="dest",
    )(idx, pos, start_b)


def _dispatch_kernel(dest_ref, lastblk_ref, h2_ref, xs_hbm, zero_ref, zsem, sem, *, tm, n_tokens):
    i = pl.program_id(0)

    def zero_copy(e):
        row = pl.multiple_of(jnp.maximum(lastblk_ref[e], 0) * EXPERT_BLOCK, EXPERT_BLOCK)
        return pltpu.make_async_copy(zero_ref, xs_hbm.at[pl.ds(row, EXPERT_BLOCK)], zsem)

    @pl.when(i == 0)
    def _():
        zero_ref[...] = jnp.zeros_like(zero_ref)

        def start(e, c):
            @pl.when(lastblk_ref[e] >= 0)
            def _():
                zero_copy(e).start()
            return c

        def wait(e, c):
            @pl.when(lastblk_ref[e] >= 0)
            def _():
                zero_copy(e).wait()
            return c

        lax.fori_loop(0, N_EXPERTS, start, 0)
        lax.fori_loop(0, N_EXPERTS, wait, 0)

    def row_copy(r, k):
        d = dest_ref[k * n_tokens + i * tm + r]
        return pltpu.make_async_copy(h2_ref.at[pl.ds(r, 1)], xs_hbm.at[pl.ds(d, 1)], sem)

    def start(r, c):
        for k in range(TOP_K):
            row_copy(r, k).start()
        return c

    def wait(r, c):
        for k in range(TOP_K):
            row_copy(r, k).wait()
        return c

    lax.fori_loop(0, tm, start, 0)
    lax.fori_loop(0, tm, wait, 0)


def _dispatch(h2, dest_flat, last_block, n_slots, tm):
    T = h2.shape[0]
    gs = pltpu.PrefetchScalarGridSpec(
        num_scalar_prefetch=2,
        grid=(T // tm,),
        in_specs=[pl.BlockSpec((tm, D_MODEL), lambda i, d, p: (i, 0))],
        out_specs=pl.BlockSpec(memory_space=pl.ANY),
        scratch_shapes=[pltpu.VMEM((EXPERT_BLOCK, D_MODEL), F32),
                        pltpu.SemaphoreType.DMA(()),
                        pltpu.SemaphoreType.DMA(())])
    return pl.pallas_call(
        functools.partial(_dispatch_kernel, tm=tm, n_tokens=T), grid_spec=gs,
        out_shape=jax.ShapeDtypeStruct((n_slots, D_MODEL), F32),
        compiler_params=_cparams(("arbitrary",)),
        name="dispatch",
    )(dest_flat, last_block, h2)


def _expert_kernel(bexp_ref, bidx_ref, new_ref, live_ref, x_ref, wgu_ref, wd_ref, y_ref, wgu_bf, wd_bf):
    i = pl.program_id(0)

    @pl.when(new_ref[i] == 1)
    def _():
        wgu_bf[...] = wgu_ref[0].astype(BF16)
        wd_bf[...] = wd_ref[0].astype(BF16)

    @pl.when(live_ref[i] == 1)
    def _():
        gu = jnp.dot(x_ref[...].astype(BF16), wgu_bf[...], preferred_element_type=F32)
        act = (_silu(gu[:, 0:D_EXPERT]) * gu[:, D_EXPERT:]).astype(BF16)
        y_ref[...] = jnp.dot(act, wd_bf[...], preferred_element_type=F32)


def _expert(x_sorted, w_gu, w_down, bexp, bidx, new, live, n_blocks):
    n_rows = x_sorted.shape[0]
    gs = pltpu.PrefetchScalarGridSpec(
        num_scalar_prefetch=4,
        grid=(n_blocks,),
        in_specs=[pl.BlockSpec((EXPERT_BLOCK, D_MODEL), lambda i, e, b, n, l: (b[i], 0)),
                  pl.BlockSpec((1, D_MODEL, 2 * D_EXPERT), lambda i, e, b, n, l: (e[i], 0, 0)),
                  pl.BlockSpec((1, D_EXPERT, D_MODEL), lambda i, e, b, n, l: (e[i], 0, 0))],
        out_specs=pl.BlockSpec((EXPERT_BLOCK, D_MODEL), lambda i, e, b, n, l: (b[i], 0)),
        scratch_shapes=[pltpu.VMEM((D_MODEL, 2 * D_EXPERT), BF16),
                        pltpu.VMEM((D_EXPERT, D_MODEL), BF16)])
    return pl.pallas_call(
        _expert_kernel, grid_spec=gs,
        out_shape=jax.ShapeDtypeStruct((n_rows, D_MODEL), F32),
        compiler_params=_cparams(("arbitrary",)),
        name="expert",
    )(bexp, bidx, new, live, x_sorted, w_gu, w_down)


def _combine_kernel(dest_ref, row_ref, h2_ref, x1_ref, wt_ref, mod_ref, wsg_ref, wsd_ref, g_ref, b_ref, y_hbm,
                    op_ref, os_ref, ybuf, sem, *, tm, n_tokens, n_prompt_tiles, alpha):
    i = pl.program_id(0)

    def row_copy(r, k):
        d = dest_ref[k * n_tokens + i * tm + r]
        return pltpu.make_async_copy(y_hbm.at[pl.ds(d, 1)], ybuf.at[k, pl.ds(r, 1)], sem)

    def start(r, c):
        for k in range(TOP_K):
            row_copy(r, k).start()
        return c

    def wait(r, c):
        for k in range(TOP_K):
            row_copy(r, k).wait()
        return c

    lax.fori_loop(0, tm, start, 0)
    h2 = h2_ref[...].astype(BF16)
    su = jnp.dot(h2, wsg_ref[...], preferred_element_type=F32)
    act = (_silu(su[:, 0:D_SHARED]) * su[:, D_SHARED:]).astype(BF16)
    moe = jnp.dot(act, wsd_ref[...], preferred_element_type=F32)
    lax.fori_loop(0, tm, wait, 0)
    wt = wt_ref[...]
    for k in range(TOP_K):
        moe = moe + ybuf[k] * wt[:, k:k + 1]
    g2 = mod_ref[pl.ds(row_ref[i], 1), 5 * D_MODEL:6 * D_MODEL]
    out = _ln_rows(alpha * x1_ref[...] + g2 * moe) * g_ref[...] + b_ref[...]

    @pl.when(i < n_prompt_tiles)
    def _():
        op_ref[...] = out

    @pl.when(i >= n_prompt_tiles)
    def _():
        os_ref[...] = out


def _combine(lay, dest_flat, h2, x1, wts_tok, mod, w_sh_gu, w_sh_down, ln_g, ln_b, y_sorted, alpha, tm):
    T = lay.n_tokens
    row, _, _ = lay.token_tile_tables(tm)
    npt = lay.n_prompt_tokens // tm

    def cur(i, d, r):
        return (i, 0)

    def const(i, d, r):
        return (0, 0)

    gs = pltpu.PrefetchScalarGridSpec(
        num_scalar_prefetch=2,
        grid=(T // tm,),
        in_specs=[pl.BlockSpec((tm, D_MODEL), cur),
                  pl.BlockSpec((tm, D_MODEL), cur),
                  pl.BlockSpec((tm, TOP_K), cur),
                  pl.BlockSpec((8, 6 * D_MODEL), const),
                  pl.BlockSpec((D_MODEL, 2 * D_SHARED), const),
                  pl.BlockSpec((D_SHARED, D_MODEL), const),
                  pl.BlockSpec((1, D_MODEL), const),
                  pl.BlockSpec((1, D_MODEL), const),
                  pl.BlockSpec(memory_space=pl.ANY)],
        out_specs=[pl.BlockSpec((tm, D_MODEL), lambda i, d, r: (jnp.minimum(i, npt - 1), 0)),
                   pl.BlockSpec((tm, D_MODEL), lambda i, d, r: (jnp.maximum(i - npt, 0), 0))],
        scratch_shapes=[pltpu.VMEM((TOP_K, tm, D_MODEL), F32),
                        pltpu.SemaphoreType.DMA(())])
    return pl.pallas_call(
        functools.partial(_combine_kernel, tm=tm, n_tokens=T, n_prompt_tiles=npt, alpha=alpha), grid_spec=gs,
        out_shape=(jax.ShapeDtypeStruct((lay.n_prompt_tokens, D_MODEL), F32),
                   jax.ShapeDtypeStruct((T - lay.n_prompt_tokens, D_MODEL), F32)),
        compiler_params=_cparams(("arbitrary",)),
        name="combine",
    )(dest_flat, row, h2, x1, wts_tok, mod, w_sh_gu, w_sh_down, ln_g.reshape(1, -1), ln_b.reshape(1, -1), y_sorted)


class _Layout:
    def __init__(self, n_prompt_seqs, prompt_len, n_sample_seqs, sample_len):
        self.n_prompt_seqs, self.prompt_len = n_prompt_seqs, prompt_len
        self.n_sample_seqs, self.sample_len = n_sample_seqs, sample_len
        self.n_prompt_tokens = n_prompt_seqs * prompt_len
        self.n_tokens = self.n_prompt_tokens + n_sample_seqs * sample_len
        assert prompt_len % CONV_TILE == 0 and sample_len % CONV_TILE == 0
        self.max_chunks = max(prompt_len, sample_len) // CHUNK

    def token_tile_tables(self, tm):
        assert self.n_prompt_tokens % tm == 0 and self.sample_len % tm == 0
        npt = self.n_prompt_tokens // tm
        per_seq = self.sample_len // tm
        n = self.n_tokens // tm
        row = np.zeros(n, np.int32)
        posb = np.zeros(n, np.int32)
        flag = np.zeros(n, np.int32)
        for i in range(npt, n):
            j = i - npt
            row[i] = 1 + j // per_seq
            posb[i] = j % per_seq
            flag[i] = 1
        return jnp.asarray(row), jnp.asarray(posb), jnp.asarray(flag)

    def conv_tile_tables(self):
        lok, rok = [], []
        for n_seq, length in ((self.n_prompt_seqs, self.prompt_len), (self.n_sample_seqs, self.sample_len)):
            per = length // CONV_TILE
            for _ in range(n_seq):
                for j in range(per):
                    lok.append(int(j > 0))
                    rok.append(int(j < per - 1))
        return jnp.asarray(np.array(lok, np.int32)), jnp.asarray(np.array(rok, np.int32))

    def ssd_step_tables(self):
        cols = [[] for _ in range(9)]
        seqs = []
        c0 = self.n_prompt_tokens // CHUNK
        for j in range(self.n_sample_seqs):
            nc = self.sample_len // CHUNK
            seqs.append((c0 + j * nc, nc, 0, j, 0))
        for j in range(self.n_prompt_seqs):
            nc = self.prompt_len // CHUNK
            seqs.append((j * nc, nc, 1, 0, j))
        for base, nc, zero, sin, sout in seqs:
            for phase in (0, 1):
                order = range(nc - 1, -1, -1) if phase == 0 else range(nc)
                for n, c in enumerate(order):
                    vals = (base + c, base if phase == 0 else base + c, phase, int(n == 0), int(n == nc - 1),
                            zero, sin, sout, c)
                    for col, v in zip(cols, vals):
                        col.append(v)
        return tuple(jnp.asarray(np.array(col, np.int32)) for col in cols)


def _grid_pos_embed(n_tokens):
    rows = n_tokens // GRID_W
    quarter = D_MODEL // 4
    freq = jnp.exp(-math.log(10000.0) * jnp.arange(quarter, dtype=F32) / quarter)
    r = jnp.broadcast_to(jnp.arange(rows, dtype=F32)[:, None, None] * freq, (rows, GRID_W, quarter))
    cl = jnp.broadcast_to(jnp.arange(GRID_W, dtype=F32)[None, :, None] * freq, (rows, GRID_W, quarter))
    emb = jnp.concatenate([jnp.sin(r), jnp.cos(r), jnp.sin(cl), jnp.cos(cl)], axis=-1)
    return emb.reshape(rows * GRID_W, D_MODEL)


def _moe_plan(counts, n_blocks):
    blk = EXPERT_BLOCK
    padded = (counts + blk - 1) // blk * blk
    pad_end = jnp.cumsum(padded)
    pad_start = pad_end - padded
    n_used = pad_end[-1] // blk
    b = jnp.arange(n_blocks, dtype=I32)
    live = (b < n_used).astype(I32)
    bidx = jnp.minimum(b, jnp.maximum(n_used - 1, 0)).astype(I32)
    bexp = jnp.sum((pad_end[None, :] <= (bidx * blk)[:, None]).astype(I32), axis=1)
    bexp = jnp.minimum(bexp, N_EXPERTS - 1).astype(I32)
    new = jnp.concatenate([jnp.ones((1,), I32), (bexp[1:] != bexp[:-1]).astype(I32)])
    last_block = jnp.where(counts > 0, pad_end // blk - 1, -1)
    return pad_start.astype(I32), last_block.astype(I32), bexp, bidx, new, live


def _layer(lay, xp, xs, pos, cond8, h0f, h0b, lp, alpha, tm_proj=512, tm_route=256, tm_disp=256, tm_comb=128):
    (w_ada, b_ada, w_in, conv_w, conv_b, conv_ln_g, conv_ln_b, ssm_conv_w, ssm_conv_b, dt_bias, a_log,
     d_skip, ssm_norm_g, w_out, ln1_g, ln1_b, w_router, router_bias, w_exp_gu, w_exp_down, w_sh_gu,
     w_sh_down, ln2_g, ln2_b) = lp
    T = lay.n_tokens
    n_main = 2 * D_CONV + D_SSM + D_XBC
    w_main = w_in[:, :n_main].astype(BF16)
    w_dt = jnp.pad(w_in[:, n_main:], ((0, 0), (0, 128 - 2 * N_HEADS))).astype(BF16)

    mod = _ada(cond8, w_ada, b_ada)
    glu, z, xbc, dt_raw = _inproj(lay, xp, xs, pos, mod, w_main, w_dt, tm_proj)
    conv_out, xbc_c = _conv(lay, glu, xbc, conv_w, conv_b, conv_ln_g, conv_ln_b, ssm_conv_w, ssm_conv_b)
    y_ssm, hf, hb = _ssd(lay, xbc_c, z, dt_raw, h0f, h0b, dt_bias, a_log, d_skip, ssm_norm_g)
    x1, h2 = _outproj(lay, xp, xs, pos, mod, conv_out, y_ssm, w_out.astype(BF16), ln1_g, ln1_b, alpha, tm_proj)

    idx, wts, posn, cnt = _route(h2, w_router.T, router_bias, tm_route)
    n_blocks = -(-T * TOP_K // EXPERT_BLOCK) + N_EXPERTS
    pad_start, last_block, bexp, bidx, new, live = _moe_plan(cnt[:, 0], n_blocks)
    dest = _dest(idx, posn, pad_start, 512).reshape(-1)
    x_sorted = _dispatch(h2, dest, last_block, n_blocks * EXPERT_BLOCK, tm_disp)
    y_sorted = _expert(x_sorted, w_exp_gu, w_exp_down, bexp, bidx, new, live, n_blocks)
    out_p, out_s = _combine(lay, dest, h2, x1, wts.T, mod, w_sh_gu.astype(BF16), w_sh_down.astype(BF16),
                            ln2_g, ln2_b, y_sorted, alpha, tm_comb)
    return out_p, out_s, hf, hb


def kernel(x_prompt, x_sample, state_ssd_fwd, state_ssd_bwd, c, c_ctx, w_ada, b_ada, w_in, conv_w, conv_b, conv_ln_g, conv_ln_b, ssm_conv_w, ssm_conv_b, dt_bias, a_log, d_skip, ssm_norm_g, w_out, ln1_g, ln1_b, w_router, router_bias, w_exp_gu, w_exp_down, w_sh_gu, w_sh_down, ln2_g, ln2_b):
    depth = w_ada.shape[0]
    assert depth == 1, "the prompt and latent passes are fused per layer; one layer is supported"
    bp, lp_, _ = x_prompt.shape
    bd, ld, _ = x_sample.shape
    lay = _Layout(bp, lp_, bd, ld)
    alpha = (2.0 * depth) ** 0.25
    stacked = (w_ada, b_ada, w_in, conv_w, conv_b, conv_ln_g, conv_ln_b, ssm_conv_w, ssm_conv_b,
               dt_bias, a_log, d_skip, ssm_norm_g, w_out, ln1_g, ln1_b, w_router, router_bias,
               w_exp_gu, w_exp_down, w_sh_gu, w_sh_down, ln2_g, ln2_b)
    lp = [w[0] for w in stacked]
    cond8 = jnp.concatenate([c_ctx[None, :], c, jnp.zeros((8 - 1 - bd, D_MODEL), F32)], axis=0)
    pos = _grid_pos_embed(ld)
    sshape = (bd, N_HEADS, HEADDIM, D_STATE)
    out_p, out_s, hf, hb = _layer(lay, x_prompt.reshape(bp * lp_, D_MODEL), x_sample.reshape(bd * ld, D_MODEL),
                                  pos, cond8, state_ssd_fwd[:, 0].reshape(sshape),
                                  state_ssd_bwd[:, 0].reshape(sshape), lp, alpha)
    return (out_p.reshape(bp, lp_, D_MODEL), out_s.reshape(bd, ld, D_MODEL),
            hf[:, None], hb[:, None])
```

```python
import functools
import math

import numpy as np
import jax
import jax.numpy as jnp
from jax import lax
from jax.experimental import pallas as pl
from jax.experimental.pallas import tpu as pltpu
from jax.experimental.pallas import tpu_sc as plsc

F32 = jnp.float32
BF16 = jnp.bfloat16
I32 = jnp.int32
HI = lax.Precision.HIGHEST

D_MODEL = 1024
GRID_W = 64
D_CONV = 1024
CONV_K = 31
N_HEADS = 16
HEADDIM = 64
D_SSM = N_HEADS * HEADDIM
N_GROUPS = 4
HEADS_PER_GROUP = N_HEADS // N_GROUPS
D_STATE = 128
SSM_CONV_K = 4
CHUNK = 128
D_XBC = D_SSM + 2 * N_GROUPS * D_STATE
N_EXPERTS = 256
TOP_K = 8
N_EXPERT_GROUPS = 8
EXPERTS_PER_GROUP = N_EXPERTS // N_EXPERT_GROUPS
TOPK_GROUPS = 4
D_EXPERT = 256
D_SHARED = 256
ROUTED_SCALE = 2.5
LN_EPS = 1e-5

CONV_TILE = 256
HALO = 16
EXPERT_BLOCK = 256
ROW_PARTS = 2
PART_WORDS = D_MODEL // 2 // ROW_PARTS
VMEM_LIMIT = 56 * 1024 * 1024


def _cparams(sem, vmem=VMEM_LIMIT):
    return pltpu.CompilerParams(dimension_semantics=sem, vmem_limit_bytes=vmem)


def _silu(x):
    return x * jax.nn.sigmoid(x)


def _ln_rows(x):
    mu = jnp.mean(x, axis=-1, keepdims=True)
    xc = x - mu
    var = jnp.mean(xc * xc, axis=-1, keepdims=True)
    return xc * lax.rsqrt(var + LN_EPS)


def _iota(shape, dim):
    return lax.broadcasted_iota(I32, shape, dim)


def _expand_matrix(n_in, width):
    return (_iota((n_in, n_in * width), 0) == _iota((n_in, n_in * width), 1) // width).astype(F32)


def _dot_hi(a, b):
    return jnp.dot(a, b, precision=HI, preferred_element_type=F32)


def _split3(x):
    hi = x.astype(BF16)
    r1 = x - hi.astype(F32)
    mid = r1.astype(BF16)
    lo = (r1 - mid.astype(F32)).astype(BF16)
    return jnp.concatenate([hi, mid, lo], axis=1)


def _expand3(n, width):
    rows = np.arange(3 * n)[:, None] % n
    cols = np.arange(n * width)[None, :] // width
    return jnp.asarray(rows == cols, dtype=BF16)


def _expand_exact(x, e3):
    return jnp.dot(_split3(x), e3, preferred_element_type=F32)


def _ada_kernel(c_ref, w_ref, b_ref, o_ref):
    o_ref[...] = _dot_hi(_silu(c_ref[...]), w_ref[...]) + b_ref[...]


def _ada(cond8, w_ada, b_ada):
    n = w_ada.shape[1]
    tn = 1024
    return pl.pallas_call(
        _ada_kernel,
        grid=(n // tn,),
        in_specs=[pl.BlockSpec((8, D_MODEL), lambda j: (0, 0)),
                  pl.BlockSpec((D_MODEL, tn), lambda j: (0, j)),
                  pl.BlockSpec((1, tn), lambda j: (0, j))],
        out_specs=pl.BlockSpec((8, tn), lambda j: (0, j)),
        out_shape=jax.ShapeDtypeStruct((8, n), F32),
        compiler_params=_cparams(("arbitrary",)),
        name="ada",
    )(cond8, w_ada, b_ada.reshape(1, n))


def _inproj_kernel(row_ref, posb_ref, flag_ref, xp_ref, xs_ref, pos_ref, mod_ref, wm_ref, wdt_ref,
                   glu_ref, z_ref, xbc_ref, dt_ref):
    i = pl.program_id(0)
    x = jnp.where(flag_ref[i] == 1, xs_ref[...] + pos_ref[...], xp_ref[...])
    r = row_ref[i]
    sh1 = mod_ref[pl.ds(r, 1), 0:D_MODEL]
    sc1 = mod_ref[pl.ds(r, 1), D_MODEL:2 * D_MODEL]
    h = (_ln_rows(x) * (1.0 + sc1) + sh1).astype(BF16)
    glu_ref[...] = jnp.dot(h, wm_ref[:, 0:2 * D_CONV], preferred_element_type=F32).astype(BF16)
    z_ref[...] = jnp.dot(h, wm_ref[:, 2 * D_CONV:2 * D_CONV + D_SSM], preferred_element_type=F32).astype(BF16)
    xbc_ref[...] = jnp.dot(h, wm_ref[:, 2 * D_CONV + D_SSM:], preferred_element_type=F32).astype(BF16)
    dt_ref[...] = jnp.dot(h, wdt_ref[...], preferred_element_type=F32)


def _inproj(lay, xp, xs, pos, mod, w_main, w_dt, tm):
    T = lay.n_tokens
    row, posb, flag = lay.token_tile_tables(tm)
    npt = lay.n_prompt_tokens // tm
    n_main = w_main.shape[1]
    gs = pltpu.PrefetchScalarGridSpec(
        num_scalar_prefetch=3,
        grid=(T // tm,),
        in_specs=[pl.BlockSpec((tm, D_MODEL), lambda i, r, p, f: (jnp.minimum(i, npt - 1), 0)),
                  pl.BlockSpec((tm, D_MODEL), lambda i, r, p, f: (jnp.maximum(i - npt, 0), 0)),
                  pl.BlockSpec((tm, D_MODEL), lambda i, r, p, f: (p[i], 0)),
                  pl.BlockSpec((8, 6 * D_MODEL), lambda i, r, p, f: (0, 0)),
                  pl.BlockSpec((D_MODEL, n_main), lambda i, r, p, f: (0, 0)),
                  pl.BlockSpec((D_MODEL, 128), lambda i, r, p, f: (0, 0))],
        out_specs=[pl.BlockSpec((tm, 2 * D_CONV), lambda i, r, p, f: (i, 0)),
                   pl.BlockSpec((tm, D_SSM), lambda i, r, p, f: (i, 0)),
                   pl.BlockSpec((tm, D_XBC), lambda i, r, p, f: (i, 0)),
                   pl.BlockSpec((tm, 128), lambda i, r, p, f: (i, 0))])
    return pl.pallas_call(
        _inproj_kernel, grid_spec=gs,
        out_shape=(jax.ShapeDtypeStruct((T, 2 * D_CONV), BF16),
                   jax.ShapeDtypeStruct((T, D_SSM), BF16),
                   jax.ShapeDtypeStruct((T, D_XBC), BF16),
                   jax.ShapeDtypeStruct((T, 128), F32)),
        compiler_params=_cparams(("arbitrary",)),
        name="inproj",
    )(row, posb, flag, xp, xs, pos, mod, w_main, w_dt)


_N_SHIFT = 8
_SHIFT_ROWS = CONV_TILE + 2 * HALO - _N_SHIFT
_ROW_BLOCK = 32


def _conv_kernel(lok_ref, rok_ref, glu_ref, glul_ref, glur_ref, xbc_ref, xbcl_ref, xbcr_ref,
                 cw_ref, cb_ref, lng_ref, lnb_ref, sw_ref, sb_ref, co_ref, xo_ref,
                 ext_ref, sh_ref, acc_ref, ext2_ref):
    i = pl.program_id(0)
    lok = lok_ref[i] == 1
    rok = rok_ref[i] == 1

    def glu(ref):
        v = ref[...].astype(F32)
        return v[:, 0:D_CONV] * jax.nn.sigmoid(v[:, D_CONV:])

    ext_ref[0:HALO, :] = jnp.where(lok, glu(glul_ref), 0.0)
    ext_ref[HALO:HALO + CONV_TILE, :] = glu(glu_ref)
    ext_ref[HALO + CONV_TILE:, :] = jnp.where(rok, glu(glur_ref), 0.0)
    for r in range(_N_SHIFT):
        sh_ref[r] = ext_ref[r:r + _SHIFT_ROWS, :]

    first = HALO - (CONV_K - 1) // 2

    def row_block(rb, carry):
        base = pl.multiple_of(rb * _ROW_BLOCK, _ROW_BLOCK)
        for j in range(D_CONV // 128):
            lanes = slice(j * 128, (j + 1) * 128)
            acc = jnp.zeros((_ROW_BLOCK, 128), F32) + cb_ref[:, lanes]
            for k in range(CONV_K):
                o = first + k
                win = sh_ref[o % _N_SHIFT, pl.ds(base + (o // _N_SHIFT) * _N_SHIFT, _ROW_BLOCK), lanes]
                acc = acc + win * cw_ref[k:k + 1, lanes]
            acc_ref[pl.ds(base, _ROW_BLOCK), lanes] = acc
        return carry

    lax.fori_loop(0, CONV_TILE // _ROW_BLOCK, row_block, 0)
    u = _ln_rows(acc_ref[...]) * lng_ref[...] + lnb_ref[...]
    co_ref[...] = _silu(u).astype(BF16)

    ext2_ref[0:HALO, :] = jnp.where(lok, xbcl_ref[...].astype(F32), 0.0)
    ext2_ref[HALO:HALO + CONV_TILE, :] = xbc_ref[...].astype(F32)
    ext2_ref[HALO + CONV_TILE:, :] = jnp.where(rok, xbcr_ref[...].astype(F32), 0.0)
    first2 = HALO - (SSM_CONV_K - 1) // 2
    half = D_XBC // 2
    for hcol in range(2):
        lanes = slice(hcol * half, (hcol + 1) * half)
        y = jnp.zeros((CONV_TILE, half), F32) + sb_ref[:, lanes]
        for k in range(SSM_CONV_K):
            y = y + ext2_ref[first2 + k:first2 + k + CONV_TILE, lanes] * sw_ref[k:k + 1, lanes]
        xo_ref[:, lanes] = _silu(y).astype(BF16)


def _conv(lay, glu, xbc, conv_w, conv_b, ln_g, ln_b, ssm_w, ssm_b):
    T = lay.n_tokens
    lok, rok = lay.conv_tile_tables()
    n_tiles = T // CONV_TILE
    hb = CONV_TILE // HALO
    n_hb = T // HALO

    def cur(i, l, r):
        return (i, 0)

    def left(i, l, r):
        return (jnp.maximum(i * hb - 1, 0), 0)

    def right(i, l, r):
        return (jnp.minimum((i + 1) * hb, n_hb - 1), 0)

    def const(i, l, r):
        return (0, 0)

    gs = pltpu.PrefetchScalarGridSpec(
        num_scalar_prefetch=2,
        grid=(n_tiles,),
        in_specs=[pl.BlockSpec((CONV_TILE, 2 * D_CONV), cur),
                  pl.BlockSpec((HALO, 2 * D_CONV), left),
                  pl.BlockSpec((HALO, 2 * D_CONV), right),
                  pl.BlockSpec((CONV_TILE, D_XBC), cur),
                  pl.BlockSpec((HALO, D_XBC), left),
                  pl.BlockSpec((HALO, D_XBC), right),
                  pl.BlockSpec((CONV_K, D_CONV), const),
                  pl.BlockSpec((1, D_CONV), const),
                  pl.BlockSpec((1, D_CONV), const),
                  pl.BlockSpec((1, D_CONV), const),
                  pl.BlockSpec((SSM_CONV_K, D_XBC), const),
                  pl.BlockSpec((1, D_XBC), const)],
        out_specs=[pl.BlockSpec((CONV_TILE, D_CONV), cur),
                   pl.BlockSpec((CONV_TILE, D_XBC), cur)],
        scratch_shapes=[pltpu.VMEM((CONV_TILE + 2 * HALO, D_CONV), F32),
                        pltpu.VMEM((_N_SHIFT, _SHIFT_ROWS, D_CONV), F32),
                        pltpu.VMEM((CONV_TILE, D_CONV), F32),
                        pltpu.VMEM((CONV_TILE + 2 * HALO, D_XBC), F32)])
    return pl.pallas_call(
        _conv_kernel, grid_spec=gs,
        out_shape=(jax.ShapeDtypeStruct((T, D_CONV), BF16),
                   jax.ShapeDtypeStruct((T, D_XBC), BF16)),
        compiler_params=_cparams(("arbitrary",)),
        name="conv",
    )(lok, rok, glu, glu, glu, xbc, xbc, xbc, conv_w, conv_b.reshape(1, -1), ln_g.reshape(1, -1),
      ln_b.reshape(1, -1), ssm_w, ssm_b.reshape(1, -1))


_BN = N_GROUPS * D_STATE


def _ssd_kernel(chunk_ref, yidx_ref, phase_ref, first_ref, last_ref, zero_ref, sin_ref, sout_ref, cloc_ref,
                xbc_ref, z_ref, dt_ref, h0f_ref, h0b_ref, dtb_ref, alog_ref, dsk_ref, ng_ref,
                tri_ref, edec_ref, ewb_ref, ecol_ref, ewide_ref, eye3_ref,
                y_ref, hf_out_ref, hb_out_ref,
                hf_ref, g_ref, gin_ref, ybuf_ref):
    s = pl.program_id(0)
    phase = phase_ref[s]
    first = first_ref[s] == 1
    last = last_ref[s] == 1
    zero = zero_ref[s] == 1
    cloc = cloc_ref[s]
    H, P, N = N_HEADS, HEADDIM, D_STATE

    GW = HEADS_PER_GROUP * P
    xs = xbc_ref[:, 0:D_SSM]
    dt = dt_ref[:, 0:2 * H] + dtb_ref[...]
    dt = jnp.maximum(dt, 0.0) + jnp.log1p(jnp.exp(-jnp.abs(dt)))
    a = dt * (-jnp.exp(alog_ref[...]))
    a3 = jnp.dot(tri_ref[...], _split3(a), preferred_element_type=F32)
    acs = a3[:, 0:2 * H] + a3[:, 2 * H:4 * H] + a3[:, 4 * H:6 * H]
    tot = acs[CHUNK - 8:CHUNK, :]
    dec = _expand_exact(jnp.exp(tot), edec_ref[...])[7:8, :]
    exb = acs[:, H:2 * H] - a[:, H:2 * H]

    def load_state(src_ref, dst_ref):
        for j in range(H // 2):
            pair = jnp.concatenate([src_ref[0, 2 * j], src_ref[0, 2 * j + 1]], axis=0)
            dst_ref[:, 2 * j * P:(2 * j + 2) * P] = jnp.where(zero, 0.0, pair.T)

    def store_state(src_ref, dst_ref):
        for j in range(H // 2):
            pair = src_ref[:, 2 * j * P:(2 * j + 2) * P].T
            dst_ref[0, 2 * j] = pair[0:P]
            dst_ref[0, 2 * j + 1] = pair[P:2 * P]

    @pl.when(phase == 0)
    def _backward_states():
        @pl.when(first)
        def _():
            load_state(h0b_ref, g_ref)

        wb = dt[:, H:2 * H] * jnp.exp(exb)
        xw = (xs.astype(F32) * _expand_exact(wb, ewb_ref[...])).astype(BF16)
        for g in range(N_GROUPS):
            cols = slice(g * GW, (g + 1) * GW)
            bg = xbc_ref[:, D_SSM + g * N:D_SSM + (g + 1) * N]
            gg = g_ref[:, cols]
            gin_ref[cloc, :, cols] = gg.astype(BF16)
            upd = lax.dot_general(bg, xw[:, cols], (((0,), (0,)), ((), ())), preferred_element_type=F32)
            g_ref[:, cols] = gg * dec[:, D_SSM + g * GW:D_SSM + (g + 1) * GW] + upd

        @pl.when(last)
        def _():
            store_state(g_ref, hb_out_ref)

    @pl.when(phase == 1)
    def _forward_and_outputs():
        @pl.when(first)
        def _():
            load_state(h0f_ref, hf_ref)

        acsf = acs[:, 0:H]
        dtf = dt[:, 0:H]
        dtb = dt[:, H:2 * H]
        totf = acs[CHUNK - 1:CHUNK, 0:H]
        totb = acs[CHUNK - 1:CHUNK, H:2 * H]
        col = _expand_exact(jnp.concatenate([acsf, exb], axis=1), ecol_ref[...])
        q3 = _split3(jnp.concatenate([acsf, exb, dtf, dtb], axis=1))
        qt = lax.dot_general(eye3_ref[...], q3, (((1,), (1,)), ((), ())),
                             preferred_element_type=F32)
        wide = jnp.concatenate([dtf * jnp.exp(totf - acsf), jnp.exp(acsf), jnp.exp(totb - exb)], axis=1)
        wide = _expand_exact(wide, ewide_ref[...])
        xsf = xs.astype(F32)
        xw = (xsf * wide[:, 0:D_SSM]).astype(BF16)
        lower = _iota((CHUNK, CHUNK), 1) <= _iota((CHUNK, CHUNK), 0)
        upper = _iota((CHUNK, CHUNK), 1) >= _iota((CHUNK, CHUNK), 0)
        for g in range(N_GROUPS):
            cols = slice(g * GW, (g + 1) * GW)
            bg = xbc_ref[:, D_SSM + g * N:D_SSM + (g + 1) * N]
            cg = xbc_ref[:, D_SSM + _BN + g * N:D_SSM + _BN + (g + 1) * N]
            cb = lax.dot_general(cg, bg, (((1,), (1,)), ((), ())), preferred_element_type=F32)
            hfg = hf_ref[:, cols]
            yf = jnp.dot(cg, hfg.astype(BF16), preferred_element_type=F32)
            yb = jnp.dot(cg, gin_ref[cloc, :, cols], preferred_element_type=F32)
            ybuf_ref[:, cols] = yf * wide[:, D_SSM + g * GW:D_SSM + (g + 1) * GW] \
                + yb * wide[:, 2 * D_SSM + g * GW:2 * D_SSM + (g + 1) * GW]
            upd = lax.dot_general(bg, xw[:, cols], (((0,), (0,)), ((), ())), preferred_element_type=F32)
            hf_ref[:, cols] = hfg * dec[:, cols] + upd
            for r in range(HEADS_PER_GROUP):
                h = g * HEADS_PER_GROUP + r
                colf = col[:, h * N:(h + 1) * N]
                colb = col[:, (H + h) * N:(H + h + 1) * N]
                mf = jnp.where(lower, jnp.exp(colf - qt[h:h + 1, :]), 0.0) * qt[2 * H + h:2 * H + h + 1, :]
                mb = jnp.where(upper, jnp.exp(qt[H + h:H + h + 1, :] - colb), 0.0) * qt[3 * H + h:3 * H + h + 1, :]
                m = (cb * (mf + mb)).astype(BF16)
                hs = slice(h * P, (h + 1) * P)
                ybuf_ref[:, hs] += jnp.dot(m, xs[:, hs], preferred_element_type=F32)

        yt = (ybuf_ref[...] + dsk_ref[...] * xsf) * _silu(z_ref[...].astype(F32))
        gw = D_SSM // N_GROUPS
        for g in range(N_GROUPS):
            seg = yt[:, g * gw:(g + 1) * gw]
            ms = jnp.mean(seg * seg, axis=-1, keepdims=True)
            y_ref[:, g * gw:(g + 1) * gw] = (seg * lax.rsqrt(ms + LN_EPS) * ng_ref[:, g * gw:(g + 1) * gw]).astype(BF16)

        @pl.when(last)
        def _():
            store_state(hf_ref, hf_out_ref)


def _ssd(lay, xbc_c, z, dt_raw, h0f, h0b, dt_bias, a_log, d_skip, norm_g):
    T = lay.n_tokens
    tabs = lay.ssd_step_tables()
    n_steps = tabs[0].shape[0]
    nsp = len(tabs)

    def by_chunk(s, *t):
        return (t[0][s], 0)

    def by_y(s, *t):
        return (t[1][s], 0)

    def by_sin(s, *t):
        return (t[6][s], 0, 0, 0)

    def by_sout(s, *t):
        return (t[7][s], 0, 0, 0)

    def const(s, *t):
        return (0, 0)

    H = N_HEADS
    tri = jnp.asarray(np.tril(np.ones((CHUNK, CHUNK))), dtype=BF16)
    eye3 = jnp.asarray(np.arange(4 * H)[:, None] == np.arange(12 * H)[None, :] % (4 * H), dtype=BF16)
    consts = [tri, _expand3(2 * H, HEADDIM), _expand3(H, HEADDIM), _expand3(2 * H, D_STATE),
              _expand3(3 * H, HEADDIM), eye3]
    sshape = (1, N_HEADS, HEADDIM, D_STATE)
    gs = pltpu.PrefetchScalarGridSpec(
        num_scalar_prefetch=nsp,
        grid=(n_steps,),
        in_specs=[pl.BlockSpec((CHUNK, D_XBC), by_chunk),
                  pl.BlockSpec((CHUNK, D_SSM), by_chunk),
                  pl.BlockSpec((CHUNK, 128), by_chunk),
                  pl.BlockSpec(sshape, by_sin),
                  pl.BlockSpec(sshape, by_sin),
                  pl.BlockSpec((1, 2 * N_HEADS), const),
                  pl.BlockSpec((1, 2 * N_HEADS), const),
                  pl.BlockSpec((1, D_SSM), const),
                  pl.BlockSpec((1, D_SSM), const)] + [pl.BlockSpec(c.shape, const) for c in consts],
        out_specs=[pl.BlockSpec((CHUNK, D_SSM), by_y),
                   pl.BlockSpec(sshape, by_sout),
                   pl.BlockSpec(sshape, by_sout)],
        scratch_shapes=[pltpu.VMEM((D_STATE, D_SSM), F32),
                        pltpu.VMEM((D_STATE, D_SSM), F32),
                        pltpu.VMEM((lay.max_chunks, D_STATE, D_SSM), BF16),
                        pltpu.VMEM((CHUNK, D_SSM), F32)])
    n_out = lay.n_prompt_seqs
    return pl.pallas_call(
        _ssd_kernel, grid_spec=gs,
        out_shape=(jax.ShapeDtypeStruct((T, D_SSM), BF16),
                   jax.ShapeDtypeStruct((n_out,) + sshape[1:], F32),
                   jax.ShapeDtypeStruct((n_out,) + sshape[1:], F32)),
        compiler_params=_cparams(("arbitrary",)),
        name="ssd",
    )(*tabs, xbc_c, z, dt_raw, h0f, h0b, dt_bias.reshape(1, -1), a_log.reshape(1, -1),
      jnp.repeat(d_skip, HEADDIM).reshape(1, -1), norm_g.reshape(1, -1), *consts)


def _outproj_kernel(row_ref, posb_ref, flag_ref, xp_ref, xs_ref, pos_ref, mod_ref, co_ref, ys_ref, wo_ref,
                    g_ref, b_ref, x1_ref, h2_ref, *, alpha):
    i = pl.program_id(0)
    x = jnp.where(flag_ref[i] == 1, xs_ref[...] + pos_ref[...], xp_ref[...])
    r = row_ref[i]
    g1 = mod_ref[pl.ds(r, 1), 2 * D_MODEL:3 * D_MODEL]
    sh2 = mod_ref[pl.ds(r, 1), 3 * D_MODEL:4 * D_MODEL]
    sc2 = mod_ref[pl.ds(r, 1), 4 * D_MODEL:5 * D_MODEL]
    mix = jnp.dot(co_ref[...], wo_ref[0:D_CONV, :], preferred_element_type=F32) \
        + jnp.dot(ys_ref[...], wo_ref[D_CONV:, :], preferred_element_type=F32)
    x1 = _ln_rows(alpha * x + g1 * mix) * g_ref[...] + b_ref[...]
    x1_ref[...] = x1
    h2_ref[...] = _ln_rows(x1) * (1.0 + sc2) + sh2


def _outproj(lay, xp, xs, pos, mod, conv_out, y_ssm, w_out, ln_g, ln_b, alpha, tm):
    T = lay.n_tokens
    row, posb, flag = lay.token_tile_tables(tm)
    npt = lay.n_prompt_tokens // tm

    def const(i, r, p, f):
        return (0, 0)

    def cur(i, r, p, f):
        return (i, 0)

    gs = pltpu.PrefetchScalarGridSpec(
        num_scalar_prefetch=3,
        grid=(T // tm,),
        in_specs=[pl.BlockSpec((tm, D_MODEL), lambda i, r, p, f: (jnp.minimum(i, npt - 1), 0)),
                  pl.BlockSpec((tm, D_MODEL), lambda i, r, p, f: (jnp.maximum(i - npt, 0), 0)),
                  pl.BlockSpec((tm, D_MODEL), lambda i, r, p, f: (p[i], 0)),
                  pl.BlockSpec((8, 6 * D_MODEL), const),
                  pl.BlockSpec((tm, D_CONV), cur),
                  pl.BlockSpec((tm, D_SSM), cur),
                  pl.BlockSpec((D_CONV + D_SSM, D_MODEL), const),
                  pl.BlockSpec((1, D_MODEL), const),
                  pl.BlockSpec((1, D_MODEL), const)],
        out_specs=[pl.BlockSpec((tm, D_MODEL), cur),
                   pl.BlockSpec((tm, D_MODEL), cur)])
    return pl.pallas_call(
        functools.partial(_outproj_kernel, alpha=alpha), grid_spec=gs,
        out_shape=(jax.ShapeDtypeStruct((T, D_MODEL), F32),
                   jax.ShapeDtypeStruct((T, D_MODEL), F32)),
        compiler_params=_cparams(("arbitrary",)),
        name="outproj",
    )(row, posb, flag, xp, xs, pos, mod, conv_out, y_ssm, w_out, ln_g.reshape(1, -1), ln_b.reshape(1, -1))


def _route_kernel(h2_ref, wrt_ref, bias_ref, idx_ref, wts_ref, pos_ref, cnt_ref, carry_ref, *, tm):
    i = pl.program_id(0)

    @pl.when(i == 0)
    def _():
        carry_ref[...] = jnp.zeros_like(carry_ref)

    E, NG, EG = N_EXPERTS, N_EXPERT_GROUPS, EXPERTS_PER_GROUP
    neg = -jnp.inf
    logits = lax.dot_general(wrt_ref[...], h2_ref[...], (((1,), (1,)), ((), ())), precision=HI,
                             preferred_element_type=F32)
    s = jax.nn.sigmoid(logits)
    sel = s + bias_ref[...]
    sel3 = sel.reshape(NG, EG, tm)
    io3 = _iota((NG, EG, tm), 1)
    m1 = jnp.max(sel3, axis=1, keepdims=True)
    f1 = jnp.min(jnp.where(sel3 == m1, io3, EG), axis=1, keepdims=True)
    m2 = jnp.max(jnp.where(io3 == f1, neg, sel3), axis=1, keepdims=True)
    gscore = (m1 + m2).reshape(NG, tm)
    gio = _iota((NG, tm), 0)
    beaten = jnp.zeros((NG, tm), I32)
    for g in range(NG):
        row = gscore[g:g + 1, :]
        beats = jnp.where(row > gscore, 1, jnp.where(row == gscore, jnp.where(g < gio, 1, 0), 0))
        beaten = beaten + beats
    keep = (beaten < TOPK_GROUPS).astype(F32).reshape(NG, 1, tm)
    selm = jnp.where(keep > 0.5, sel3, neg).reshape(E, tm)
    eio = _iota((E, tm), 0)
    chosen = jnp.zeros((E, tm), F32)
    idxs, ws = [], []
    for k in range(TOP_K):
        m = jnp.max(selm, axis=0, keepdims=True)
        am = jnp.minimum(jnp.min(jnp.where(selm == m, eio, E), axis=0, keepdims=True), E - 1)
        hit = eio == am
        ws.append(jnp.sum(jnp.where(hit, s, 0.0), axis=0, keepdims=True))
        idxs.append(am)
        selm = jnp.where(hit, neg, selm)
        chosen = jnp.where(hit, 1.0, chosen)
    wsum = ws[0]
    for k in range(1, TOP_K):
        wsum = wsum + ws[k]
    before = (_iota((tm, tm), 0) < _iota((tm, tm), 1)).astype(BF16)
    prior = jnp.dot(chosen.astype(BF16), before, preferred_element_type=F32)
    carry = carry_ref[...]
    prior = prior + jnp.concatenate([carry] * (tm // 128), axis=1)
    for k in range(TOP_K):
        idx_ref[k:k + 1, :] = idxs[k]
        wts_ref[k:k + 1, :] = ws[k] / wsum * ROUTED_SCALE
        pos_ref[k:k + 1, :] = jnp.sum(jnp.where(eio == idxs[k], prior, 0.0), axis=0, keepdims=True).astype(I32)
    total = jnp.dot(chosen.astype(BF16), jnp.ones((tm, 128), BF16), preferred_element_type=F32)
    carry = carry + total
    carry_ref[...] = carry
    cnt_ref[...] = carry.astype(I32)


def _route(h2, w_router_t, router_bias, tm):
    T = h2.shape[0]
    bias_b = jnp.broadcast_to(router_bias.astype(F32)[:, None], (N_EXPERTS, tm))
    return pl.pallas_call(
        functools.partial(_route_kernel, tm=tm),
        grid=(T // tm,),
        in_specs=[pl.BlockSpec((tm, D_MODEL), lambda i: (i, 0)),
                  pl.BlockSpec((N_EXPERTS, D_MODEL), lambda i: (0, 0)),
                  pl.BlockSpec((N_EXPERTS, tm), lambda i: (0, 0))],
        out_specs=[pl.BlockSpec((TOP_K, tm), lambda i: (0, i)),
                   pl.BlockSpec((TOP_K, tm), lambda i: (0, i)),
                   pl.BlockSpec((TOP_K, tm), lambda i: (0, i)),
                   pl.BlockSpec((N_EXPERTS, 128), lambda i: (0, 0))],
        out_shape=(jax.ShapeDtypeStruct((TOP_K, T), I32),
                   jax.ShapeDtypeStruct((TOP_K, T), F32),
                   jax.ShapeDtypeStruct((TOP_K, T), I32),
                   jax.ShapeDtypeStruct((N_EXPERTS, 128), I32)),
        scratch_shapes=[pltpu.VMEM((N_EXPERTS, 128), F32)],
        compiler_params=_cparams(("arbitrary",)),
        name="route",
    )(h2, w_router_t, bias_b)


def _dest_kernel(idx_ref, pos_ref, start_ref, dest_ref):
    tm = idx_ref.shape[1]
    eio = _iota((N_EXPERTS, tm), 0)
    start = start_ref[...]
    for k in range(TOP_K):
        base = jnp.sum(jnp.where(eio == idx_ref[k:k + 1, :], start, 0.0), axis=0, keepdims=True)
        dest_ref[k:k + 1, :] = base.astype(I32) + pos_ref[k:k + 1, :]


def _dest(idx, pos, pad_start, tm):
    T = idx.shape[1]
    start_b = jnp.broadcast_to(pad_start.astype(F32)[:, None], (N_EXPERTS, tm))
    return pl.pallas_call(
        _dest_kernel,
        grid=(T // tm,),
        in_specs=[pl.BlockSpec((TOP_K, tm), lambda i: (0, i)),
                  pl.BlockSpec((TOP_K, tm), lambda i: (0, i)),
                  pl.BlockSpec((N_EXPERTS, tm), lambda i: (0, 0))],
        out_specs=pl.BlockSpec((TOP_K, tm), lambda i: (0, i)),
        out_shape=jax.ShapeDtypeStruct((TOP_K, T), I32),
        compiler_params=_cparams(("arbitrary",)),
        name="dest",
    )(idx, pos, start_b)


def _dispatch_kernel(dest_ref, lastblk_ref, h2_ref, xs_hbm, zero_ref, zsem, sem, *, tm, n_tokens):
    i = pl.program_id(0)

    def zero_copy(e):
        row = pl.multiple_of(jnp.maximum(lastblk_ref[e], 0) * EXPERT_BLOCK, EXPERT_BLOCK)
        return pltpu.make_async_copy(zero_ref, xs_hbm.at[pl.ds(row, EXPERT_BLOCK)], zsem)

    @pl.when(i == 0)
    def _():
        zero_ref[...] = jnp.zeros_like(zero_ref)

        def start(e, c):
            @pl.when(lastblk_ref[e] >= 0)
            def _():
                zero_copy(e).start()
            return c

        def wait(e, c):
            @pl.when(lastblk_ref[e] >= 0)
            def _():
                zero_copy(e).wait()
            return c

        lax.fori_loop(0, N_EXPERTS, start, 0)
        lax.fori_loop(0, N_EXPERTS, wait, 0)

    def row_copy(r, k):
        d = dest_ref[k * n_tokens + i * tm + r]
        return pltpu.make_async_copy(h2_ref.at[pl.ds(r, 1)], xs_hbm.at[pl.ds(d, 1)], sem)

    def start(r, c):
        for k in range(TOP_K):
            row_copy(r, k).start()
        return c

    def wait(r, c):
        for k in range(TOP_K):
            row_copy(r, k).wait()
        return c

    lax.fori_loop(0, tm, start, 0)
    lax.fori_loop(0, tm, wait, 0)


def _dispatch(h2, dest_flat, last_block, n_slots, tm):
    T = h2.shape[0]
    gs = pltpu.PrefetchScalarGridSpec(
        num_scalar_prefetch=2,
        grid=(T // tm,),
        in_specs=[pl.BlockSpec((tm, D_MODEL), lambda i, d, p: (i, 0))],
        out_specs=pl.BlockSpec(memory_space=pl.ANY),
        scratch_shapes=[pltpu.VMEM((EXPERT_BLOCK, D_MODEL), F32),
                        pltpu.SemaphoreType.DMA(()),
                        pltpu.SemaphoreType.DMA(())])
    return pl.pallas_call(
        functools.partial(_dispatch_kernel, tm=tm, n_tokens=T), grid_spec=gs,
        out_shape=jax.ShapeDtypeStruct((n_slots, D_MODEL), F32),
        compiler_params=_cparams(("arbitrary",)),
        name="dispatch",
    )(dest_flat, last_block, h2)


def _expert_kernel(bexp_ref, bidx_ref, new_ref, live_ref, x_ref, wgu_ref, wd_ref, y_ref, wgu_bf, wd_bf):
    i = pl.program_id(0)

    @pl.when(new_ref[i] == 1)
    def _():
        wgu_bf[...] = wgu_ref[0].astype(BF16)
        wd_bf[...] = wd_ref[0].astype(BF16)

    @pl.when(live_ref[i] == 1)
    def _():
        gu = jnp.dot(x_ref[...].astype(BF16), wgu_bf[...], preferred_element_type=F32)
        act = (_silu(gu[:, 0:D_EXPERT]) * gu[:, D_EXPERT:]).astype(BF16)
        packed = _pack_halves(jnp.dot(act, wd_bf[...], preferred_element_type=F32))
        for c in range(ROW_PARTS):
            y_ref[c] = packed[:, c * PART_WORDS:(c + 1) * PART_WORDS]


def _pack_halves(x):
    n = x.shape[1] // 2
    hi = lax.bitcast_convert_type(x[:, :n].astype(BF16).astype(F32), jnp.uint32)
    lo = lax.bitcast_convert_type(x[:, n:].astype(BF16).astype(F32), jnp.uint32)
    return hi | (lo >> 16)


def _unpack_halves(p):
    hi = lax.bitcast_convert_type(p & jnp.uint32(0xFFFF0000), F32)
    lo = lax.bitcast_convert_type(p << 16, F32)
    return hi, lo


def _expert(x_sorted, w_gu, w_down, bexp, bidx, new, live, n_blocks):
    n_rows = x_sorted.shape[0]
    gs = pltpu.PrefetchScalarGridSpec(
        num_scalar_prefetch=4,
        grid=(n_blocks,),
        in_specs=[pl.BlockSpec((EXPERT_BLOCK, D_MODEL), lambda i, e, b, n, l: (b[i], 0)),
                  pl.BlockSpec((1, D_MODEL, 2 * D_EXPERT), lambda i, e, b, n, l: (e[i], 0, 0)),
                  pl.BlockSpec((1, D_EXPERT, D_MODEL), lambda i, e, b, n, l: (e[i], 0, 0))],
        out_specs=pl.BlockSpec((ROW_PARTS, EXPERT_BLOCK, PART_WORDS), lambda i, e, b, n, l: (0, b[i], 0)),
        scratch_shapes=[pltpu.VMEM((D_MODEL, 2 * D_EXPERT), BF16),
                        pltpu.VMEM((D_EXPERT, D_MODEL), BF16)])
    return pl.pallas_call(
        _expert_kernel, grid_spec=gs,
        out_shape=jax.ShapeDtypeStruct((ROW_PARTS, n_rows, PART_WORDS), jnp.uint32),
        compiler_params=_cparams(("arbitrary",)),
        name="expert",
    )(bexp, bidx, new, live, x_sorted, w_gu, w_down)


def _combine_kernel(row_ref, h2_ref, x1_ref, wt_ref, mod_ref, wsg_ref, wsd_ref, g_ref, b_ref, yt_ref,
                    op_ref, os_ref, *, n_prompt_tiles, alpha):
    i = pl.program_id(0)
    h2 = h2_ref[...].astype(BF16)
    su = jnp.dot(h2, wsg_ref[...], preferred_element_type=F32)
    act = (_silu(su[:, 0:D_SHARED]) * su[:, D_SHARED:]).astype(BF16)
    moe = jnp.dot(act, wsd_ref[...], preferred_element_type=F32)
    wt = wt_ref[...]
    his, los = [], []
    for c in range(ROW_PARTS):
        rh = jnp.zeros((h2.shape[0], PART_WORDS), F32)
        rl = jnp.zeros((h2.shape[0], PART_WORDS), F32)
        for k in range(TOP_K):
            hi, lo = _unpack_halves(yt_ref[c, k])
            w = wt[:, k:k + 1]
            rh = rh + hi * w
            rl = rl + lo * w
        his.append(rh)
        los.append(rl)
    moe = moe + jnp.concatenate(his + los, axis=1)
    g2 = mod_ref[pl.ds(row_ref[i], 1), 5 * D_MODEL:6 * D_MODEL]
    out = _ln_rows(alpha * x1_ref[...] + g2 * moe) * g_ref[...] + b_ref[...]

    @pl.when(i < n_prompt_tiles)
    def _():
        op_ref[...] = out

    @pl.when(i >= n_prompt_tiles)
    def _():
        os_ref[...] = out


def _combine(lay, h2, x1, wts_tok, mod, w_sh_gu, w_sh_down, ln_g, ln_b, y_tok, alpha, tm):
    T = lay.n_tokens
    row, _, _ = lay.token_tile_tables(tm)
    npt = lay.n_prompt_tokens // tm

    def cur(i, r):
        return (i, 0)

    def const(i, r):
        return (0, 0)

    gs = pltpu.PrefetchScalarGridSpec(
        num_scalar_prefetch=1,
        grid=(T // tm,),
        in_specs=[pl.BlockSpec((tm, D_MODEL), cur),
                  pl.BlockSpec((tm, D_MODEL), cur),
                  pl.BlockSpec((tm, TOP_K), cur),
                  pl.BlockSpec((8, 6 * D_MODEL), const),
                  pl.BlockSpec((D_MODEL, 2 * D_SHARED), const),
                  pl.BlockSpec((D_SHARED, D_MODEL), const),
                  pl.BlockSpec((1, D_MODEL), const),
                  pl.BlockSpec((1, D_MODEL), const),
                  pl.BlockSpec((ROW_PARTS, TOP_K, tm, PART_WORDS), lambda i, r: (0, 0, i, 0))],
        out_specs=[pl.BlockSpec((tm, D_MODEL), lambda i, r: (jnp.minimum(i, npt - 1), 0)),
                   pl.BlockSpec((tm, D_MODEL), lambda i, r: (jnp.maximum(i - npt, 0), 0))])
    return pl.pallas_call(
        functools.partial(_combine_kernel, n_prompt_tiles=npt, alpha=alpha), grid_spec=gs,
        out_shape=(jax.ShapeDtypeStruct((lay.n_prompt_tokens, D_MODEL), F32),
                   jax.ShapeDtypeStruct((T - lay.n_prompt_tokens, D_MODEL), F32)),
        compiler_params=_cparams(("arbitrary",)),
        name="combine",
    )(row, h2, x1, wts_tok, mod, w_sh_gu, w_sh_down, ln_g.reshape(1, -1), ln_b.reshape(1, -1), y_tok)


_GATHER_WINDOW = 128


def _sc_gather(table, idx):
    n, d = idx.shape[0], table.shape[1]
    mesh = plsc.VectorSubcoreMesh(core_axis_name="core", subcore_axis_name="subcore")

    @pl.kernel(out_type=jax.ShapeDtypeStruct((n, d), table.dtype), mesh=mesh)
    def gather_kernel(table_hbm, idx_hbm, out_hbm):
        def body(idx_vmem, out_vmem):
            pltpu.sync_copy(table_hbm.at[idx_vmem.at[0]], out_vmem)

        pltpu.emit_pipeline(
            body,
            grid=(n // _GATHER_WINDOW,),
            in_specs=[pl.BlockSpec((1, _GATHER_WINDOW), index_map=lambda i: (0, i))],
            out_specs=[pl.BlockSpec((_GATHER_WINDOW, d), index_map=lambda i: (i, 0))],
            core_axis_name=("core", "subcore"),
            dimension_semantics=(pltpu.PARALLEL,),
        )(idx_hbm, out_hbm)

    return gather_kernel(table, idx.reshape(1, n))


class _Layout:
    def __init__(self, n_prompt_seqs, prompt_len, n_sample_seqs, sample_len):
        self.n_prompt_seqs, self.prompt_len = n_prompt_seqs, prompt_len
        self.n_sample_seqs, self.sample_len = n_sample_seqs, sample_len
        self.n_prompt_tokens = n_prompt_seqs * prompt_len
        self.n_tokens = self.n_prompt_tokens + n_sample_seqs * sample_len
        assert prompt_len % CONV_TILE == 0 and sample_len % CONV_TILE == 0
        self.max_chunks = max(prompt_len, sample_len) // CHUNK

    def token_tile_tables(self, tm):
        assert self.n_prompt_tokens % tm == 0 and self.sample_len % tm == 0
        npt = self.n_prompt_tokens // tm
        per_seq = self.sample_len // tm
        n = self.n_tokens // tm
        row = np.zeros(n, np.int32)
        posb = np.zeros(n, np.int32)
        flag = np.zeros(n, np.int32)
        for i in range(npt, n):
            j = i - npt
            row[i] = 1 + j // per_seq
            posb[i] = j % per_seq
            flag[i] = 1
        return jnp.asarray(row), jnp.asarray(posb), jnp.asarray(flag)

    def conv_tile_tables(self):
        lok, rok = [], []
        for n_seq, length in ((self.n_prompt_seqs, self.prompt_len), (self.n_sample_seqs, self.sample_len)):
            per = length // CONV_TILE
            for _ in range(n_seq):
                for j in range(per):
                    lok.append(int(j > 0))
                    rok.append(int(j < per - 1))
        return jnp.asarray(np.array(lok, np.int32)), jnp.asarray(np.array(rok, np.int32))

    def ssd_step_tables(self):
        cols = [[] for _ in range(9)]
        seqs = []
        c0 = self.n_prompt_tokens // CHUNK
        for j in range(self.n_sample_seqs):
            nc = self.sample_len // CHUNK
            seqs.append((c0 + j * nc, nc, 0, j, 0))
        for j in range(self.n_prompt_seqs):
            nc = self.prompt_len // CHUNK
            seqs.append((j * nc, nc, 1, 0, j))
        for base, nc, zero, sin, sout in seqs:
            for phase in (0, 1):
                order = range(nc - 1, -1, -1) if phase == 0 else range(nc)
                for n, c in enumerate(order):
                    vals = (base + c, base if phase == 0 else base + c, phase, int(n == 0), int(n == nc - 1),
                            zero, sin, sout, c)
                    for col, v in zip(cols, vals):
                        col.append(v)
        return tuple(jnp.asarray(np.array(col, np.int32)) for col in cols)


def _grid_pos_embed(n_tokens):
    rows = n_tokens // GRID_W
    quarter = D_MODEL // 4
    freq = jnp.exp(-math.log(10000.0) * jnp.arange(quarter, dtype=F32) / quarter)
    r = jnp.broadcast_to(jnp.arange(rows, dtype=F32)[:, None, None] * freq, (rows, GRID_W, quarter))
    cl = jnp.broadcast_to(jnp.arange(GRID_W, dtype=F32)[None, :, None] * freq, (rows, GRID_W, quarter))
    emb = jnp.concatenate([jnp.sin(r), jnp.cos(r), jnp.sin(cl), jnp.cos(cl)], axis=-1)
    return emb.reshape(rows * GRID_W, D_MODEL)


def _moe_plan(counts, n_blocks):
    blk = EXPERT_BLOCK
    padded = (counts + blk - 1) // blk * blk
    pad_end = jnp.cumsum(padded)
    pad_start = pad_end - padded
    n_used = pad_end[-1] // blk
    b = jnp.arange(n_blocks, dtype=I32)
    live = (b < n_used).astype(I32)
    bidx = jnp.minimum(b, jnp.maximum(n_used - 1, 0)).astype(I32)
    bexp = jnp.sum((pad_end[None, :] <= (bidx * blk)[:, None]).astype(I32), axis=1)
    bexp = jnp.minimum(bexp, N_EXPERTS - 1).astype(I32)
    new = jnp.concatenate([jnp.ones((1,), I32), (bexp[1:] != bexp[:-1]).astype(I32)])
    last_block = jnp.where(counts > 0, pad_end // blk - 1, -1)
    return pad_start.astype(I32), last_block.astype(I32), bexp, bidx, new, live


def _layer(lay, xp, xs, pos, cond8, h0f, h0b, lp, alpha, tm_proj=512, tm_route=256, tm_disp=256, tm_comb=256):
    (w_ada, b_ada, w_in, conv_w, conv_b, conv_ln_g, conv_ln_b, ssm_conv_w, ssm_conv_b, dt_bias, a_log,
     d_skip, ssm_norm_g, w_out, ln1_g, ln1_b, w_router, router_bias, w_exp_gu, w_exp_down, w_sh_gu,
     w_sh_down, ln2_g, ln2_b) = lp
    T = lay.n_tokens
    n_main = 2 * D_CONV + D_SSM + D_XBC
    w_main = w_in[:, :n_main].astype(BF16)
    w_dt = jnp.pad(w_in[:, n_main:], ((0, 0), (0, 128 - 2 * N_HEADS))).astype(BF16)

    mod = _ada(cond8, w_ada, b_ada)
    glu, z, xbc, dt_raw = _inproj(lay, xp, xs, pos, mod, w_main, w_dt, tm_proj)
    conv_out, xbc_c = _conv(lay, glu, xbc, conv_w, conv_b, conv_ln_g, conv_ln_b, ssm_conv_w, ssm_conv_b)
    y_ssm, hf, hb = _ssd(lay, xbc_c, z, dt_raw, h0f, h0b, dt_bias, a_log, d_skip, ssm_norm_g)
    x1, h2 = _outproj(lay, xp, xs, pos, mod, conv_out, y_ssm, w_out.astype(BF16), ln1_g, ln1_b, alpha, tm_proj)

    idx, wts, posn, cnt = _route(h2, w_router.T, router_bias, tm_route)
    n_blocks = -(-T * TOP_K // EXPERT_BLOCK) + N_EXPERTS
    pad_start, last_block, bexp, bidx, new, live = _moe_plan(cnt[:, 0], n_blocks)
    dest = _dest(idx, posn, pad_start, 512).reshape(-1)
    x_sorted = _dispatch(h2, dest, last_block, n_blocks * EXPERT_BLOCK, tm_disp)
    y_sorted = _expert(x_sorted, w_exp_gu, w_exp_down, bexp, bidx, new, live, n_blocks)
    n_rows = y_sorted.shape[1]
    idx_parts = jnp.concatenate([dest + c * n_rows for c in range(ROW_PARTS)])
    y_tok = _sc_gather(y_sorted.reshape(ROW_PARTS * n_rows, PART_WORDS), idx_parts)
    y_tok = y_tok.reshape(ROW_PARTS, TOP_K, T, PART_WORDS)
    out_p, out_s = _combine(lay, h2, x1, wts.T, mod, w_sh_gu.astype(BF16), w_sh_down.astype(BF16),
                            ln2_g, ln2_b, y_tok, alpha, tm_comb)
    return out_p, out_s, hf, hb


def kernel(x_prompt, x_sample, state_ssd_fwd, state_ssd_bwd, c, c_ctx, w_ada, b_ada, w_in, conv_w, conv_b, conv_ln_g, conv_ln_b, ssm_conv_w, ssm_conv_b, dt_bias, a_log, d_skip, ssm_norm_g, w_out, ln1_g, ln1_b, w_router, router_bias, w_exp_gu, w_exp_down, w_sh_gu, w_sh_down, ln2_g, ln2_b):
    depth = w_ada.shape[0]
    assert depth == 1, "the prompt and latent passes are fused per layer; one layer is supported"
    bp, lp_, _ = x_prompt.shape
    bd, ld, _ = x_sample.shape
    lay = _Layout(bp, lp_, bd, ld)
    alpha = (2.0 * depth) ** 0.25
    stacked = (w_ada, b_ada, w_in, conv_w, conv_b, conv_ln_g, conv_ln_b, ssm_conv_w, ssm_conv_b,
               dt_bias, a_log, d_skip, ssm_norm_g, w_out, ln1_g, ln1_b, w_router, router_bias,
               w_exp_gu, w_exp_down, w_sh_gu, w_sh_down, ln2_g, ln2_b)
    lp = [w[0] for w in stacked]
    cond8 = jnp.concatenate([c_ctx[None, :], c, jnp.zeros((8 - 1 - bd, D_MODEL), F32)], axis=0)
    pos = _grid_pos_embed(ld)
    sshape = (bd, N_HEADS, HEADDIM, D_STATE)
    out_p, out_s, hf, hb = _layer(lay, x_prompt.reshape(bp * lp_, D_MODEL), x_sample.reshape(bd * ld, D_MODEL),
                                  pos, cond8, state_ssd_fwd[:, 0].reshape(sshape),
                                  state_ssd_bwd[:, 0].reshape(sshape), lp, alpha)
    return (out_p.reshape(bp, lp_, D_MODEL), out_s.reshape(bd, ld, D_MODEL),
            hf[:, None], hb[:, None])
```

```python
import functools
import math

import numpy as np
import jax
import jax.numpy as jnp
from jax import lax
from jax.experimental import pallas as pl
from jax.experimental.pallas import tpu as pltpu
from jax.experimental.pallas import tpu_sc as plsc

F32 = jnp.float32
BF16 = jnp.bfloat16
I32 = jnp.int32
HI = lax.Precision.HIGHEST

D_MODEL = 1024
GRID_W = 64
D_CONV = 1024
CONV_K = 31
N_HEADS = 16
HEADDIM = 64
D_SSM = N_HEADS * HEADDIM
N_GROUPS = 4
HEADS_PER_GROUP = N_HEADS // N_GROUPS
D_STATE = 128
SSM_CONV_K = 4
CHUNK = 128
D_XBC = D_SSM + 2 * N_GROUPS * D_STATE
N_EXPERTS = 256
TOP_K = 8
N_EXPERT_GROUPS = 8
EXPERTS_PER_GROUP = N_EXPERTS // N_EXPERT_GROUPS
TOPK_GROUPS = 4
D_EXPERT = 256
D_SHARED = 256
ROUTED_SCALE = 2.5
LN_EPS = 1e-5

CONV_TILE = 256
HALO = 16
EXPERT_BLOCK = 256
ROW_PARTS = 2
PART_WORDS = D_MODEL // 2 // ROW_PARTS
VMEM_LIMIT = 56 * 1024 * 1024


def _cparams(sem, vmem=VMEM_LIMIT):
    return pltpu.CompilerParams(dimension_semantics=sem, vmem_limit_bytes=vmem)


def _silu(x):
    return x * jax.nn.sigmoid(x)


def _ln_rows(x):
    mu = jnp.mean(x, axis=-1, keepdims=True)
    xc = x - mu
    var = jnp.mean(xc * xc, axis=-1, keepdims=True)
    return xc * lax.rsqrt(var + LN_EPS)


def _iota(shape, dim):
    return lax.broadcasted_iota(I32, shape, dim)


def _expand_matrix(n_in, width):
    return (_iota((n_in, n_in * width), 0) == _iota((n_in, n_in * width), 1) // width).astype(F32)


def _dot_hi(a, b):
    return jnp.dot(a, b, precision=HI, preferred_element_type=F32)


def _split3(x):
    hi = x.astype(BF16)
    r1 = x - hi.astype(F32)
    mid = r1.astype(BF16)
    lo = (r1 - mid.astype(F32)).astype(BF16)
    return jnp.concatenate([hi, mid, lo], axis=1)


def _expand3(n, width):
    rows = np.arange(3 * n)[:, None] % n
    cols = np.arange(n * width)[None, :] // width
    return jnp.asarray(rows == cols, dtype=BF16)


def _expand_exact(x, e3):
    return jnp.dot(_split3(x), e3, preferred_element_type=F32)


def _ada_kernel(c_ref, w_ref, b_ref, o_ref):
    o_ref[...] = _dot_hi(_silu(c_ref[...]), w_ref[...]) + b_ref[...]


def _ada(cond8, w_ada, b_ada):
    n = w_ada.shape[1]
    tn = 1024
    return pl.pallas_call(
        _ada_kernel,
        grid=(n // tn,),
        in_specs=[pl.BlockSpec((8, D_MODEL), lambda j: (0, 0)),
                  pl.BlockSpec((D_MODEL, tn), lambda j: (0, j)),
                  pl.BlockSpec((1, tn), lambda j: (0, j))],
        out_specs=pl.BlockSpec((8, tn), lambda j: (0, j)),
        out_shape=jax.ShapeDtypeStruct((8, n), F32),
        compiler_params=_cparams(("arbitrary",)),
        name="ada",
    )(cond8, w_ada, b_ada.reshape(1, n))


def _inproj_kernel(row_ref, posb_ref, flag_ref, xp_ref, xs_ref, pos_ref, mod_ref, wm_ref, wdt_ref,
                   glu_ref, z_ref, xbc_ref, dt_ref):
    i = pl.program_id(0)
    x = jnp.where(flag_ref[i] == 1, xs_ref[...] + pos_ref[...], xp_ref[...])
    r = row_ref[i]
    sh1 = mod_ref[pl.ds(r, 1), 0:D_MODEL]
    sc1 = mod_ref[pl.ds(r, 1), D_MODEL:2 * D_MODEL]
    h = (_ln_rows(x) * (1.0 + sc1) + sh1).astype(BF16)
    glu_ref[...] = jnp.dot(h, wm_ref[:, 0:2 * D_CONV], preferred_element_type=F32).astype(BF16)
    z_ref[...] = jnp.dot(h, wm_ref[:, 2 * D_CONV:2 * D_CONV + D_SSM], preferred_element_type=F32).astype(BF16)
    xbc_ref[...] = jnp.dot(h, wm_ref[:, 2 * D_CONV + D_SSM:], preferred_element_type=F32).astype(BF16)
    dt_ref[...] = jnp.dot(h, wdt_ref[...], preferred_element_type=F32)


def _inproj(lay, xp, xs, pos, mod, w_main, w_dt, tm):
    T = lay.n_tokens
    row, posb, flag = lay.token_tile_tables(tm)
    npt = lay.n_prompt_tokens // tm
    n_main = w_main.shape[1]
    gs = pltpu.PrefetchScalarGridSpec(
        num_scalar_prefetch=3,
        grid=(T // tm,),
        in_specs=[pl.BlockSpec((tm, D_MODEL), lambda i, r, p, f: (jnp.minimum(i, npt - 1), 0)),
                  pl.BlockSpec((tm, D_MODEL), lambda i, r, p, f: (jnp.maximum(i - npt, 0), 0)),
                  pl.BlockSpec((tm, D_MODEL), lambda i, r, p, f: (p[i], 0)),
                  pl.BlockSpec((8, 6 * D_MODEL), lambda i, r, p, f: (0, 0)),
                  pl.BlockSpec((D_MODEL, n_main), lambda i, r, p, f: (0, 0)),
                  pl.BlockSpec((D_MODEL, 128), lambda i, r, p, f: (0, 0))],
        out_specs=[pl.BlockSpec((tm, 2 * D_CONV), lambda i, r, p, f: (i, 0)),
                   pl.BlockSpec((tm, D_SSM), lambda i, r, p, f: (i, 0)),
                   pl.BlockSpec((tm, D_XBC), lambda i, r, p, f: (i, 0)),
                   pl.BlockSpec((tm, 128), lambda i, r, p, f: (i, 0))])
    return pl.pallas_call(
        _inproj_kernel, grid_spec=gs,
        out_shape=(jax.ShapeDtypeStruct((T, 2 * D_CONV), BF16),
                   jax.ShapeDtypeStruct((T, D_SSM), BF16),
                   jax.ShapeDtypeStruct((T, D_XBC), BF16),
                   jax.ShapeDtypeStruct((T, 128), F32)),
        compiler_params=_cparams(("arbitrary",)),
        name="inproj",
    )(row, posb, flag, xp, xs, pos, mod, w_main, w_dt)


_N_SHIFT = 8
_SHIFT_ROWS = CONV_TILE + 2 * HALO - _N_SHIFT
_ROW_BLOCK = 32


def _conv_kernel(lok_ref, rok_ref, glu_ref, glul_ref, glur_ref, xbc_ref, xbcl_ref, xbcr_ref,
                 cw_ref, cb_ref, lng_ref, lnb_ref, sw_ref, sb_ref, co_ref, xo_ref,
                 ext_ref, sh_ref, acc_ref, ext2_ref):
    i = pl.program_id(0)
    lok = lok_ref[i] == 1
    rok = rok_ref[i] == 1

    def glu(ref):
        v = ref[...].astype(F32)
        return v[:, 0:D_CONV] * jax.nn.sigmoid(v[:, D_CONV:])

    ext_ref[0:HALO, :] = jnp.where(lok, glu(glul_ref), 0.0)
    ext_ref[HALO:HALO + CONV_TILE, :] = glu(glu_ref)
    ext_ref[HALO + CONV_TILE:, :] = jnp.where(rok, glu(glur_ref), 0.0)
    for r in range(_N_SHIFT):
        sh_ref[r] = ext_ref[r:r + _SHIFT_ROWS, :]

    first = HALO - (CONV_K - 1) // 2

    def row_block(rb, carry):
        base = pl.multiple_of(rb * _ROW_BLOCK, _ROW_BLOCK)
        for j in range(D_CONV // 128):
            lanes = slice(j * 128, (j + 1) * 128)
            acc = jnp.zeros((_ROW_BLOCK, 128), F32) + cb_ref[:, lanes]
            for k in range(CONV_K):
                o = first + k
                win = sh_ref[o % _N_SHIFT, pl.ds(base + (o // _N_SHIFT) * _N_SHIFT, _ROW_BLOCK), lanes]
                acc = acc + win * cw_ref[k:k + 1, lanes]
            acc_ref[pl.ds(base, _ROW_BLOCK), lanes] = acc
        return carry

    lax.fori_loop(0, CONV_TILE // _ROW_BLOCK, row_block, 0)
    u = _ln_rows(acc_ref[...]) * lng_ref[...] + lnb_ref[...]
    co_ref[...] = _silu(u).astype(BF16)

    ext2_ref[0:HALO, :] = jnp.where(lok, xbcl_ref[...].astype(F32), 0.0)
    ext2_ref[HALO:HALO + CONV_TILE, :] = xbc_ref[...].astype(F32)
    ext2_ref[HALO + CONV_TILE:, :] = jnp.where(rok, xbcr_ref[...].astype(F32), 0.0)
    first2 = HALO - (SSM_CONV_K - 1) // 2
    half = D_XBC // 2
    for hcol in range(2):
        lanes = slice(hcol * half, (hcol + 1) * half)
        y = jnp.zeros((CONV_TILE, half), F32) + sb_ref[:, lanes]
        for k in range(SSM_CONV_K):
            y = y + ext2_ref[first2 + k:first2 + k + CONV_TILE, lanes] * sw_ref[k:k + 1, lanes]
        xo_ref[:, lanes] = _silu(y).astype(BF16)


def _conv(lay, glu, xbc, conv_w, conv_b, ln_g, ln_b, ssm_w, ssm_b):
    T = lay.n_tokens
    lok, rok = lay.conv_tile_tables()
    n_tiles = T // CONV_TILE
    hb = CONV_TILE // HALO
    n_hb = T // HALO

    def cur(i, l, r):
        return (i, 0)

    def left(i, l, r):
        return (jnp.maximum(i * hb - 1, 0), 0)

    def right(i, l, r):
        return (jnp.minimum((i + 1) * hb, n_hb - 1), 0)

    def const(i, l, r):
        return (0, 0)

    gs = pltpu.PrefetchScalarGridSpec(
        num_scalar_prefetch=2,
        grid=(n_tiles,),
        in_specs=[pl.BlockSpec((CONV_TILE, 2 * D_CONV), cur),
                  pl.BlockSpec((HALO, 2 * D_CONV), left),
                  pl.BlockSpec((HALO, 2 * D_CONV), right),
                  pl.BlockSpec((CONV_TILE, D_XBC), cur),
                  pl.BlockSpec((HALO, D_XBC), left),
                  pl.BlockSpec((HALO, D_XBC), right),
                  pl.BlockSpec((CONV_K, D_CONV), const),
                  pl.BlockSpec((1, D_CONV), const),
                  pl.BlockSpec((1, D_CONV), const),
                  pl.BlockSpec((1, D_CONV), const),
                  pl.BlockSpec((SSM_CONV_K, D_XBC), const),
                  pl.BlockSpec((1, D_XBC), const)],
        out_specs=[pl.BlockSpec((CONV_TILE, D_CONV), cur),
                   pl.BlockSpec((CONV_TILE, D_XBC), cur)],
        scratch_shapes=[pltpu.VMEM((CONV_TILE + 2 * HALO, D_CONV), F32),
                        pltpu.VMEM((_N_SHIFT, _SHIFT_ROWS, D_CONV), F32),
                        pltpu.VMEM((CONV_TILE, D_CONV), F32),
                        pltpu.VMEM((CONV_TILE + 2 * HALO, D_XBC), F32)])
    return pl.pallas_call(
        _conv_kernel, grid_spec=gs,
        out_shape=(jax.ShapeDtypeStruct((T, D_CONV), BF16),
                   jax.ShapeDtypeStruct((T, D_XBC), BF16)),
        compiler_params=_cparams(("arbitrary",)),
        name="conv",
    )(lok, rok, glu, glu, glu, xbc, xbc, xbc, conv_w, conv_b.reshape(1, -1), ln_g.reshape(1, -1),
      ln_b.reshape(1, -1), ssm_w, ssm_b.reshape(1, -1))


_BN = N_GROUPS * D_STATE


def _ssd_kernel(chunk_ref, yidx_ref, phase_ref, first_ref, last_ref, zero_ref, sin_ref, sout_ref, cloc_ref,
                xbc_ref, z_ref, dt_ref, h0f_ref, h0b_ref, dtb_ref, alog_ref, dsk_ref, ng_ref,
                tri_ref, edec_ref, ewb_ref, ecol_ref, ewide_ref, eye3_ref,
                y_ref, hf_out_ref, hb_out_ref,
                hf_ref, g_ref, gin_ref, ybuf_ref):
    s = pl.program_id(0)
    phase = phase_ref[s]
    first = first_ref[s] == 1
    last = last_ref[s] == 1
    zero = zero_ref[s] == 1
    cloc = cloc_ref[s]
    H, P, N = N_HEADS, HEADDIM, D_STATE

    GW = HEADS_PER_GROUP * P
    xs = xbc_ref[:, 0:D_SSM]
    dt = dt_ref[:, 0:2 * H] + dtb_ref[...]
    dt = jnp.maximum(dt, 0.0) + jnp.log1p(jnp.exp(-jnp.abs(dt)))
    a = dt * (-jnp.exp(alog_ref[...]))
    a3 = jnp.dot(tri_ref[...], _split3(a), preferred_element_type=F32)
    acs = a3[:, 0:2 * H] + a3[:, 2 * H:4 * H] + a3[:, 4 * H:6 * H]
    tot = acs[CHUNK - 8:CHUNK, :]
    dec = _expand_exact(jnp.exp(tot), edec_ref[...])[7:8, :]
    exb = acs[:, H:2 * H] - a[:, H:2 * H]

    def load_state(src_ref, dst_ref):
        for j in range(H // 2):
            pair = jnp.concatenate([src_ref[0, 2 * j], src_ref[0, 2 * j + 1]], axis=0)
            dst_ref[:, 2 * j * P:(2 * j + 2) * P] = jnp.where(zero, 0.0, pair.T)

    def store_state(src_ref, dst_ref):
        for j in range(H // 2):
            pair = src_ref[:, 2 * j * P:(2 * j + 2) * P].T
            dst_ref[0, 2 * j] = pair[0:P]
            dst_ref[0, 2 * j + 1] = pair[P:2 * P]

    @pl.when(phase == 0)
    def _backward_states():
        @pl.when(first)
        def _():
            load_state(h0b_ref, g_ref)

        wb = dt[:, H:2 * H] * jnp.exp(exb)
        xw = (xs.astype(F32) * _expand_exact(wb, ewb_ref[...])).astype(BF16)
        for g in range(N_GROUPS):
            cols = slice(g * GW, (g + 1) * GW)
            bg = xbc_ref[:, D_SSM + g * N:D_SSM + (g + 1) * N]
            gg = g_ref[:, cols]
            gin_ref[cloc, :, cols] = gg.astype(BF16)
            upd = lax.dot_general(bg, xw[:, cols], (((0,), (0,)), ((), ())), preferred_element_type=F32)
            g_ref[:, cols] = gg * dec[:, D_SSM + g * GW:D_SSM + (g + 1) * GW] + upd

        @pl.when(last)
        def _():
            store_state(g_ref, hb_out_ref)

    @pl.when(phase == 1)
    def _forward_and_outputs():
        @pl.when(first)
        def _():
            load_state(h0f_ref, hf_ref)

        acsf = acs[:, 0:H]
        dtf = dt[:, 0:H]
        dtb = dt[:, H:2 * H]
        totf = acs[CHUNK - 1:CHUNK, 0:H]
        totb = acs[CHUNK - 1:CHUNK, H:2 * H]
        col = _expand_exact(jnp.concatenate([acsf, exb], axis=1), ecol_ref[...])
        q3 = _split3(jnp.concatenate([acsf, exb, dtf, dtb], axis=1))
        qt = lax.dot_general(eye3_ref[...], q3, (((1,), (1,)), ((), ())),
                             preferred_element_type=F32)
        wide = jnp.concatenate([dtf * jnp.exp(totf - acsf), jnp.exp(acsf), jnp.exp(totb - exb)], axis=1)
        wide = _expand_exact(wide, ewide_ref[...])
        xsf = xs.astype(F32)
        xw = (xsf * wide[:, 0:D_SSM]).astype(BF16)
        lower = _iota((CHUNK, CHUNK), 1) <= _iota((CHUNK, CHUNK), 0)
        upper = _iota((CHUNK, CHUNK), 1) >= _iota((CHUNK, CHUNK), 0)
        for g in range(N_GROUPS):
            cols = slice(g * GW, (g + 1) * GW)
            bg = xbc_ref[:, D_SSM + g * N:D_SSM + (g + 1) * N]
            cg = xbc_ref[:, D_SSM + _BN + g * N:D_SSM + _BN + (g + 1) * N]
            cb = lax.dot_general(cg, bg, (((1,), (1,)), ((), ())), preferred_element_type=F32)
            hfg = hf_ref[:, cols]
            yf = jnp.dot(cg, hfg.astype(BF16), preferred_element_type=F32)
            yb = jnp.dot(cg, gin_ref[cloc, :, cols], preferred_element_type=F32)
            ybuf_ref[:, cols] = yf * wide[:, D_SSM + g * GW:D_SSM + (g + 1) * GW] \
                + yb * wide[:, 2 * D_SSM + g * GW:2 * D_SSM + (g + 1) * GW]
            upd = lax.dot_general(bg, xw[:, cols], (((0,), (0,)), ((), ())), preferred_element_type=F32)
            hf_ref[:, cols] = hfg * dec[:, cols] + upd
            for r in range(HEADS_PER_GROUP):
                h = g * HEADS_PER_GROUP + r
                colf = col[:, h * N:(h + 1) * N]
                colb = col[:, (H + h) * N:(H + h + 1) * N]
                mf = jnp.where(lower, jnp.exp(colf - qt[h:h + 1, :]), 0.0) * qt[2 * H + h:2 * H + h + 1, :]
                mb = jnp.where(upper, jnp.exp(qt[H + h:H + h + 1, :] - colb), 0.0) * qt[3 * H + h:3 * H + h + 1, :]
                m = (cb * (mf + mb)).astype(BF16)
                hs = slice(h * P, (h + 1) * P)
                ybuf_ref[:, hs] += jnp.dot(m, xs[:, hs], preferred_element_type=F32)

        yt = (ybuf_ref[...] + dsk_ref[...] * xsf) * _silu(z_ref[...].astype(F32))
        gw = D_SSM // N_GROUPS
        for g in range(N_GROUPS):
            seg = yt[:, g * gw:(g + 1) * gw]
            ms = jnp.mean(seg * seg, axis=-1, keepdims=True)
            y_ref[:, g * gw:(g + 1) * gw] = (seg * lax.rsqrt(ms + LN_EPS) * ng_ref[:, g * gw:(g + 1) * gw]).astype(BF16)

        @pl.when(last)
        def _():
            store_state(hf_ref, hf_out_ref)


def _ssd(lay, xbc_c, z, dt_raw, h0f, h0b, dt_bias, a_log, d_skip, norm_g):
    T = lay.n_tokens
    tabs = lay.ssd_step_tables()
    n_steps = tabs[0].shape[0]
    nsp = len(tabs)

    def by_chunk(s, *t):
        return (t[0][s], 0)

    def by_y(s, *t):
        return (t[1][s], 0)

    def by_sin(s, *t):
        return (t[6][s], 0, 0, 0)

    def by_sout(s, *t):
        return (t[7][s], 0, 0, 0)

    def const(s, *t):
        return (0, 0)

    H = N_HEADS
    tri = jnp.asarray(np.tril(np.ones((CHUNK, CHUNK))), dtype=BF16)
    eye3 = jnp.asarray(np.arange(4 * H)[:, None] == np.arange(12 * H)[None, :] % (4 * H), dtype=BF16)
    consts = [tri, _expand3(2 * H, HEADDIM), _expand3(H, HEADDIM), _expand3(2 * H, D_STATE),
              _expand3(3 * H, HEADDIM), eye3]
    sshape = (1, N_HEADS, HEADDIM, D_STATE)
    gs = pltpu.PrefetchScalarGridSpec(
        num_scalar_prefetch=nsp,
        grid=(n_steps,),
        in_specs=[pl.BlockSpec((CHUNK, D_XBC), by_chunk),
                  pl.BlockSpec((CHUNK, D_SSM), by_chunk),
                  pl.BlockSpec((CHUNK, 128), by_chunk),
                  pl.BlockSpec(sshape, by_sin),
                  pl.BlockSpec(sshape, by_sin),
                  pl.BlockSpec((1, 2 * N_HEADS), const),
                  pl.BlockSpec((1, 2 * N_HEADS), const),
                  pl.BlockSpec((1, D_SSM), const),
                  pl.BlockSpec((1, D_SSM), const)] + [pl.BlockSpec(c.shape, const) for c in consts],
        out_specs=[pl.BlockSpec((CHUNK, D_SSM), by_y),
                   pl.BlockSpec(sshape, by_sout),
                   pl.BlockSpec(sshape, by_sout)],
        scratch_shapes=[pltpu.VMEM((D_STATE, D_SSM), F32),
                        pltpu.VMEM((D_STATE, D_SSM), F32),
                        pltpu.VMEM((lay.max_chunks, D_STATE, D_SSM), BF16),
                        pltpu.VMEM((CHUNK, D_SSM), F32)])
    n_out = lay.n_prompt_seqs
    return pl.pallas_call(
        _ssd_kernel, grid_spec=gs,
        out_shape=(jax.ShapeDtypeStruct((T, D_SSM), BF16),
                   jax.ShapeDtypeStruct((n_out,) + sshape[1:], F32),
                   jax.ShapeDtypeStruct((n_out,) + sshape[1:], F32)),
        compiler_params=_cparams(("arbitrary",)),
        name="ssd",
    )(*tabs, xbc_c, z, dt_raw, h0f, h0b, dt_bias.reshape(1, -1), a_log.reshape(1, -1),
      jnp.repeat(d_skip, HEADDIM).reshape(1, -1), norm_g.reshape(1, -1), *consts)


def _outproj_kernel(row_ref, posb_ref, flag_ref, xp_ref, xs_ref, pos_ref, mod_ref, co_ref, ys_ref, wo_ref,
                    g_ref, b_ref, x1_ref, h2_ref, h2p_ref, *, alpha):
    i = pl.program_id(0)
    x = jnp.where(flag_ref[i] == 1, xs_ref[...] + pos_ref[...], xp_ref[...])
    r = row_ref[i]
    g1 = mod_ref[pl.ds(r, 1), 2 * D_MODEL:3 * D_MODEL]
    sh2 = mod_ref[pl.ds(r, 1), 3 * D_MODEL:4 * D_MODEL]
    sc2 = mod_ref[pl.ds(r, 1), 4 * D_MODEL:5 * D_MODEL]
    mix = jnp.dot(co_ref[...], wo_ref[0:D_CONV, :], preferred_element_type=F32) \
        + jnp.dot(ys_ref[...], wo_ref[D_CONV:, :], preferred_element_type=F32)
    x1 = _ln_rows(alpha * x + g1 * mix) * g_ref[...] + b_ref[...]
    x1_ref[...] = x1
    h2 = _ln_rows(x1) * (1.0 + sc2) + sh2
    h2_ref[...] = h2
    packed = _pack_halves(h2)
    for c in range(ROW_PARTS):
        h2p_ref[c] = packed[:, c * PART_WORDS:(c + 1) * PART_WORDS]


def _outproj(lay, xp, xs, pos, mod, conv_out, y_ssm, w_out, ln_g, ln_b, alpha, tm):
    T = lay.n_tokens
    row, posb, flag = lay.token_tile_tables(tm)
    npt = lay.n_prompt_tokens // tm

    def const(i, r, p, f):
        return (0, 0)

    def cur(i, r, p, f):
        return (i, 0)

    gs = pltpu.PrefetchScalarGridSpec(
        num_scalar_prefetch=3,
        grid=(T // tm,),
        in_specs=[pl.BlockSpec((tm, D_MODEL), lambda i, r, p, f: (jnp.minimum(i, npt - 1), 0)),
                  pl.BlockSpec((tm, D_MODEL), lambda i, r, p, f: (jnp.maximum(i - npt, 0), 0)),
                  pl.BlockSpec((tm, D_MODEL), lambda i, r, p, f: (p[i], 0)),
                  pl.BlockSpec((8, 6 * D_MODEL), const),
                  pl.BlockSpec((tm, D_CONV), cur),
                  pl.BlockSpec((tm, D_SSM), cur),
                  pl.BlockSpec((D_CONV + D_SSM, D_MODEL), const),
                  pl.BlockSpec((1, D_MODEL), const),
                  pl.BlockSpec((1, D_MODEL), const)],
        out_specs=[pl.BlockSpec((tm, D_MODEL), cur),
                   pl.BlockSpec((tm, D_MODEL), cur),
                   pl.BlockSpec((ROW_PARTS, tm, PART_WORDS), lambda i, r, p, f: (0, i, 0))])
    return pl.pallas_call(
        functools.partial(_outproj_kernel, alpha=alpha), grid_spec=gs,
        out_shape=(jax.ShapeDtypeStruct((T, D_MODEL), F32),
                   jax.ShapeDtypeStruct((T, D_MODEL), F32),
                   jax.ShapeDtypeStruct((ROW_PARTS, T, PART_WORDS), jnp.uint32)),
        compiler_params=_cparams(("arbitrary",)),
        name="outproj",
    )(row, posb, flag, xp, xs, pos, mod, conv_out, y_ssm, w_out, ln_g.reshape(1, -1), ln_b.reshape(1, -1))


def _route_kernel(h2_ref, wrt_ref, bias_ref, idx_ref, wts_ref, pos_ref, cnt_ref, carry_ref, *, tm):
    i = pl.program_id(0)

    @pl.when(i == 0)
    def _():
        carry_ref[...] = jnp.zeros_like(carry_ref)

    E, NG, EG = N_EXPERTS, N_EXPERT_GROUPS, EXPERTS_PER_GROUP
    neg = -jnp.inf
    logits = lax.dot_general(wrt_ref[...], h2_ref[...], (((1,), (1,)), ((), ())), precision=HI,
                             preferred_element_type=F32)
    s = jax.nn.sigmoid(logits)
    sel = s + bias_ref[...]
    sel3 = sel.reshape(NG, EG, tm)
    io3 = _iota((NG, EG, tm), 1)
    m1 = jnp.max(sel3, axis=1, keepdims=True)
    f1 = jnp.min(jnp.where(sel3 == m1, io3, EG), axis=1, keepdims=True)
    m2 = jnp.max(jnp.where(io3 == f1, neg, sel3), axis=1, keepdims=True)
    gscore = (m1 + m2).reshape(NG, tm)
    gio = _iota((NG, tm), 0)
    beaten = jnp.zeros((NG, tm), I32)
    for g in range(NG):
        row = gscore[g:g + 1, :]
        beats = jnp.where(row > gscore, 1, jnp.where(row == gscore, jnp.where(g < gio, 1, 0), 0))
        beaten = beaten + beats
    keep = (beaten < TOPK_GROUPS).astype(F32).reshape(NG, 1, tm)
    selm = jnp.where(keep > 0.5, sel3, neg).reshape(E, tm)
    eio = _iota((E, tm), 0)
    chosen = jnp.zeros((E, tm), F32)
    idxs, ws = [], []
    for k in range(TOP_K):
        m = jnp.max(selm, axis=0, keepdims=True)
        am = jnp.minimum(jnp.min(jnp.where(selm == m, eio, E), axis=0, keepdims=True), E - 1)
        hit = eio == am
        ws.append(jnp.sum(jnp.where(hit, s, 0.0), axis=0, keepdims=True))
        idxs.append(am)
        selm = jnp.where(hit, neg, selm)
        chosen = jnp.where(hit, 1.0, chosen)
    wsum = ws[0]
    for k in range(1, TOP_K):
        wsum = wsum + ws[k]
    before = (_iota((tm, tm), 0) < _iota((tm, tm), 1)).astype(BF16)
    prior = jnp.dot(chosen.astype(BF16), before, preferred_element_type=F32)
    carry = carry_ref[...]
    prior = prior + jnp.concatenate([carry] * (tm // 128), axis=1)
    for k in range(TOP_K):
        idx_ref[k:k + 1, :] = idxs[k]
        wts_ref[k:k + 1, :] = ws[k] / wsum * ROUTED_SCALE
        pos_ref[k:k + 1, :] = jnp.sum(jnp.where(eio == idxs[k], prior, 0.0), axis=0, keepdims=True).astype(I32)
    total = jnp.dot(chosen.astype(BF16), jnp.ones((tm, 128), BF16), preferred_element_type=F32)
    carry = carry + total
    carry_ref[...] = carry
    cnt_ref[...] = carry.astype(I32)


def _route(h2, w_router_t, router_bias, tm):
    T = h2.shape[0]
    bias_b = jnp.broadcast_to(router_bias.astype(F32)[:, None], (N_EXPERTS, tm))
    return pl.pallas_call(
        functools.partial(_route_kernel, tm=tm),
        grid=(T // tm,),
        in_specs=[pl.BlockSpec((tm, D_MODEL), lambda i: (i, 0)),
                  pl.BlockSpec((N_EXPERTS, D_MODEL), lambda i: (0, 0)),
                  pl.BlockSpec((N_EXPERTS, tm), lambda i: (0, 0))],
        out_specs=[pl.BlockSpec((TOP_K, tm), lambda i: (0, i)),
                   pl.BlockSpec((TOP_K, tm), lambda i: (0, i)),
                   pl.BlockSpec((TOP_K, tm), lambda i: (0, i)),
                   pl.BlockSpec((N_EXPERTS, 128), lambda i: (0, 0))],
        out_shape=(jax.ShapeDtypeStruct((TOP_K, T), I32),
                   jax.ShapeDtypeStruct((TOP_K, T), F32),
                   jax.ShapeDtypeStruct((TOP_K, T), I32),
                   jax.ShapeDtypeStruct((N_EXPERTS, 128), I32)),
        scratch_shapes=[pltpu.VMEM((N_EXPERTS, 128), F32)],
        compiler_params=_cparams(("arbitrary",)),
        name="route",
    )(h2, w_router_t, bias_b)


def _dest_kernel(idx_ref, pos_ref, start_ref, dest_ref):
    tm = idx_ref.shape[1]
    eio = _iota((N_EXPERTS, tm), 0)
    start = start_ref[...]
    for k in range(TOP_K):
        base = jnp.sum(jnp.where(eio == idx_ref[k:k + 1, :], start, 0.0), axis=0, keepdims=True)
        dest_ref[k:k + 1, :] = base.astype(I32) + pos_ref[k:k + 1, :]


def _dest(idx, pos, pad_start, tm):
    T = idx.shape[1]
    start_b = jnp.broadcast_to(pad_start.astype(F32)[:, None], (N_EXPERTS, tm))
    return pl.pallas_call(
        _dest_kernel,
        grid=(T // tm,),
        in_specs=[pl.BlockSpec((TOP_K, tm), lambda i: (0, i)),
                  pl.BlockSpec((TOP_K, tm), lambda i: (0, i)),
                  pl.BlockSpec((N_EXPERTS, tm), lambda i: (0, 0))],
        out_specs=pl.BlockSpec((TOP_K, tm), lambda i: (0, i)),
        out_shape=jax.ShapeDtypeStruct((TOP_K, T), I32),
        compiler_params=_cparams(("arbitrary",)),
        name="dest",
    )(idx, pos, start_b)


def _expert_kernel(bexp_ref, bidx_ref, new_ref, live_ref, x_ref, wgu_ref, wd_ref, y_ref, wgu_bf, wd_bf):
    i = pl.program_id(0)

    @pl.when(new_ref[i] == 1)
    def _():
        wgu_bf[...] = wgu_ref[0].astype(BF16)
        wd_bf[...] = wd_ref[0].astype(BF16)

    @pl.when(live_ref[i] == 1)
    def _():
        parts = [_unpack_halves(x_ref[c]) for c in range(ROW_PARTS)]
        chunks = [p[0] for p in parts] + [p[1] for p in parts]
        gu = jnp.zeros((EXPERT_BLOCK, 2 * D_EXPERT), F32)
        for j, xc in enumerate(chunks):
            gu = gu + jnp.dot(xc.astype(BF16), wgu_bf[j * PART_WORDS:(j + 1) * PART_WORDS, :],
                              preferred_element_type=F32)
        act = (_silu(gu[:, 0:D_EXPERT]) * gu[:, D_EXPERT:]).astype(BF16)
        packed = _pack_halves(jnp.dot(act, wd_bf[...], preferred_element_type=F32))
        for c in range(ROW_PARTS):
            y_ref[c] = packed[:, c * PART_WORDS:(c + 1) * PART_WORDS]


def _pack_halves(x):
    n = x.shape[1] // 2
    hi = lax.bitcast_convert_type(x[:, :n].astype(BF16).astype(F32), jnp.uint32)
    lo = lax.bitcast_convert_type(x[:, n:].astype(BF16).astype(F32), jnp.uint32)
    return hi | (lo >> 16)


def _unpack_halves(p):
    hi = lax.bitcast_convert_type(p & jnp.uint32(0xFFFF0000), F32)
    lo = lax.bitcast_convert_type(p << 16, F32)
    return hi, lo


def _expert(x_sorted, w_gu, w_down, bexp, bidx, new, live, n_blocks):
    n_rows = x_sorted.shape[1]
    gs = pltpu.PrefetchScalarGridSpec(
        num_scalar_prefetch=4,
        grid=(n_blocks,),
        in_specs=[pl.BlockSpec((ROW_PARTS, EXPERT_BLOCK, PART_WORDS), lambda i, e, b, n, l: (0, b[i], 0)),
                  pl.BlockSpec((1, D_MODEL, 2 * D_EXPERT), lambda i, e, b, n, l: (e[i], 0, 0)),
                  pl.BlockSpec((1, D_EXPERT, D_MODEL), lambda i, e, b, n, l: (e[i], 0, 0))],
        out_specs=pl.BlockSpec((ROW_PARTS, EXPERT_BLOCK, PART_WORDS), lambda i, e, b, n, l: (0, b[i], 0)),
        scratch_shapes=[pltpu.VMEM((D_MODEL, 2 * D_EXPERT), BF16),
                        pltpu.VMEM((D_EXPERT, D_MODEL), BF16)])
    return pl.pallas_call(
        _expert_kernel, grid_spec=gs,
        out_shape=jax.ShapeDtypeStruct((ROW_PARTS, n_rows, PART_WORDS), jnp.uint32),
        compiler_params=_cparams(("arbitrary",)),
        name="expert",
    )(bexp, bidx, new, live, x_sorted, w_gu, w_down)


def _combine_kernel(row_ref, h2_ref, x1_ref, wt_ref, mod_ref, wsg_ref, wsd_ref, g_ref, b_ref, yt_ref,
                    op_ref, os_ref, *, n_prompt_tiles, alpha):
    i = pl.program_id(0)
    h2 = h2_ref[...].astype(BF16)
    su = jnp.dot(h2, wsg_ref[...], preferred_element_type=F32)
    act = (_silu(su[:, 0:D_SHARED]) * su[:, D_SHARED:]).astype(BF16)
    moe = jnp.dot(act, wsd_ref[...], preferred_element_type=F32)
    wt = wt_ref[...]
    his, los = [], []
    for c in range(ROW_PARTS):
        rh = jnp.zeros((h2.shape[0], PART_WORDS), F32)
        rl = jnp.zeros((h2.shape[0], PART_WORDS), F32)
        for k in range(TOP_K):
            hi, lo = _unpack_halves(yt_ref[c, k])
            w = wt[:, k:k + 1]
            rh = rh + hi * w
            rl = rl + lo * w
        his.append(rh)
        los.append(rl)
    moe = moe + jnp.concatenate(his + los, axis=1)
    g2 = mod_ref[pl.ds(row_ref[i], 1), 5 * D_MODEL:6 * D_MODEL]
    out = _ln_rows(alpha * x1_ref[...] + g2 * moe) * g_ref[...] + b_ref[...]

    @pl.when(i < n_prompt_tiles)
    def _():
        op_ref[...] = out

    @pl.when(i >= n_prompt_tiles)
    def _():
        os_ref[...] = out


def _combine(lay, h2, x1, wts_tok, mod, w_sh_gu, w_sh_down, ln_g, ln_b, y_tok, alpha, tm):
    T = lay.n_tokens
    row, _, _ = lay.token_tile_tables(tm)
    npt = lay.n_prompt_tokens // tm

    def cur(i, r):
        return (i, 0)

    def const(i, r):
        return (0, 0)

    gs = pltpu.PrefetchScalarGridSpec(
        num_scalar_prefetch=1,
        grid=(T // tm,),
        in_specs=[pl.BlockSpec((tm, D_MODEL), cur),
                  pl.BlockSpec((tm, D_MODEL), cur),
                  pl.BlockSpec((tm, TOP_K), cur),
                  pl.BlockSpec((8, 6 * D_MODEL), const),
                  pl.BlockSpec((D_MODEL, 2 * D_SHARED), const),
                  pl.BlockSpec((D_SHARED, D_MODEL), const),
                  pl.BlockSpec((1, D_MODEL), const),
                  pl.BlockSpec((1, D_MODEL), const),
                  pl.BlockSpec((ROW_PARTS, TOP_K, tm, PART_WORDS), lambda i, r: (0, 0, i, 0))],
        out_specs=[pl.BlockSpec((tm, D_MODEL), lambda i, r: (jnp.minimum(i, npt - 1), 0)),
                   pl.BlockSpec((tm, D_MODEL), lambda i, r: (jnp.maximum(i - npt, 0), 0))])
    return pl.pallas_call(
        functools.partial(_combine_kernel, n_prompt_tiles=npt, alpha=alpha), grid_spec=gs,
        out_shape=(jax.ShapeDtypeStruct((lay.n_prompt_tokens, D_MODEL), F32),
                   jax.ShapeDtypeStruct((T - lay.n_prompt_tokens, D_MODEL), F32)),
        compiler_params=_cparams(("arbitrary",)),
        name="combine",
    )(row, h2, x1, wts_tok, mod, w_sh_gu, w_sh_down, ln_g.reshape(1, -1), ln_b.reshape(1, -1), y_tok)


_GATHER_WINDOW = 128


def _sc_gather(table, idx):
    n, d = idx.shape[0], table.shape[1]
    mesh = plsc.VectorSubcoreMesh(core_axis_name="core", subcore_axis_name="subcore")

    @pl.kernel(out_type=jax.ShapeDtypeStruct((n, d), table.dtype), mesh=mesh)
    def gather_kernel(table_hbm, idx_hbm, out_hbm):
        def body(idx_vmem, out_vmem):
            pltpu.sync_copy(table_hbm.at[idx_vmem.at[0]], out_vmem)

        pltpu.emit_pipeline(
            body,
            grid=(n // _GATHER_WINDOW,),
            in_specs=[pl.BlockSpec((1, _GATHER_WINDOW), index_map=lambda i: (0, i))],
            out_specs=[pl.BlockSpec((_GATHER_WINDOW, d), index_map=lambda i: (i, 0))],
            core_axis_name=("core", "subcore"),
            dimension_semantics=(pltpu.PARALLEL,),
        )(idx_hbm, out_hbm)

    return gather_kernel(table, idx.reshape(1, n))


def _sc_scatter(rows, idx, n_out, repeat):
    n, d = rows.shape
    tiles = n // _GATHER_WINDOW
    mesh = plsc.VectorSubcoreMesh(core_axis_name="core", subcore_axis_name="subcore")

    @pl.kernel(out_type=jax.ShapeDtypeStruct((n_out, d), rows.dtype), mesh=mesh, scratch_types=[])
    def scatter_kernel(rows_hbm, idx_hbm, out_hbm):
        def body(rows_vmem, idx_vmem):
            pltpu.sync_copy(rows_vmem, out_hbm.at[idx_vmem.at[0]])

        pltpu.emit_pipeline(
            body,
            grid=(repeat * tiles,),
            in_specs=[pl.BlockSpec((_GATHER_WINDOW, d), index_map=lambda i: (i % tiles, 0)),
                      pl.BlockSpec((1, _GATHER_WINDOW), index_map=lambda i: (0, i))],
            out_specs=[],
            core_axis_name=("core", "subcore"),
            dimension_semantics=(pltpu.PARALLEL,),
        )(rows_hbm, idx_hbm)

    return scatter_kernel(rows, idx.reshape(1, repeat * n))


class _Layout:
    def __init__(self, n_prompt_seqs, prompt_len, n_sample_seqs, sample_len):
        self.n_prompt_seqs, self.prompt_len = n_prompt_seqs, prompt_len
        self.n_sample_seqs, self.sample_len = n_sample_seqs, sample_len
        self.n_prompt_tokens = n_prompt_seqs * prompt_len
        self.n_tokens = self.n_prompt_tokens + n_sample_seqs * sample_len
        assert prompt_len % CONV_TILE == 0 and sample_len % CONV_TILE == 0
        self.max_chunks = max(prompt_len, sample_len) // CHUNK

    def token_tile_tables(self, tm):
        assert self.n_prompt_tokens % tm == 0 and self.sample_len % tm == 0
        npt = self.n_prompt_tokens // tm
        per_seq = self.sample_len // tm
        n = self.n_tokens // tm
        row = np.zeros(n, np.int32)
        posb = np.zeros(n, np.int32)
        flag = np.zeros(n, np.int32)
        for i in range(npt, n):
            j = i - npt
            row[i] = 1 + j // per_seq
            posb[i] = j % per_seq
            flag[i] = 1
        return jnp.asarray(row), jnp.asarray(posb), jnp.asarray(flag)

    def conv_tile_tables(self):
        lok, rok = [], []
        for n_seq, length in ((self.n_prompt_seqs, self.prompt_len), (self.n_sample_seqs, self.sample_len)):
            per = length // CONV_TILE
            for _ in range(n_seq):
                for j in range(per):
                    lok.append(int(j > 0))
                    rok.append(int(j < per - 1))
        return jnp.asarray(np.array(lok, np.int32)), jnp.asarray(np.array(rok, np.int32))

    def ssd_step_tables(self):
        cols = [[] for _ in range(9)]
        seqs = []
        c0 = self.n_prompt_tokens // CHUNK
        for j in range(self.n_sample_seqs):
            nc = self.sample_len // CHUNK
            seqs.append((c0 + j * nc, nc, 0, j, 0))
        for j in range(self.n_prompt_seqs):
            nc = self.prompt_len // CHUNK
            seqs.append((j * nc, nc, 1, 0, j))
        for base, nc, zero, sin, sout in seqs:
            for phase in (0, 1):
                order = range(nc - 1, -1, -1) if phase == 0 else range(nc)
                for n, c in enumerate(order):
                    vals = (base + c, base if phase == 0 else base + c, phase, int(n == 0), int(n == nc - 1),
                            zero, sin, sout, c)
                    for col, v in zip(cols, vals):
                        col.append(v)
        return tuple(jnp.asarray(np.array(col, np.int32)) for col in cols)


def _grid_pos_embed(n_tokens):
    rows = n_tokens // GRID_W
    quarter = D_MODEL // 4
    freq = jnp.exp(-math.log(10000.0) * jnp.arange(quarter, dtype=F32) / quarter)
    r = jnp.broadcast_to(jnp.arange(rows, dtype=F32)[:, None, None] * freq, (rows, GRID_W, quarter))
    cl = jnp.broadcast_to(jnp.arange(GRID_W, dtype=F32)[None, :, None] * freq, (rows, GRID_W, quarter))
    emb = jnp.concatenate([jnp.sin(r), jnp.cos(r), jnp.sin(cl), jnp.cos(cl)], axis=-1)
    return emb.reshape(rows * GRID_W, D_MODEL)


def _moe_plan(counts, n_blocks):
    blk = EXPERT_BLOCK
    padded = (counts + blk - 1) // blk * blk
    pad_end = jnp.cumsum(padded)
    pad_start = pad_end - padded
    n_used = pad_end[-1] // blk
    b = jnp.arange(n_blocks, dtype=I32)
    live = (b < n_used).astype(I32)
    bidx = jnp.minimum(b, jnp.maximum(n_used - 1, 0)).astype(I32)
    bexp = jnp.sum((pad_end[None, :] <= (bidx * blk)[:, None]).astype(I32), axis=1)
    bexp = jnp.minimum(bexp, N_EXPERTS - 1).astype(I32)
    new = jnp.concatenate([jnp.ones((1,), I32), (bexp[1:] != bexp[:-1]).astype(I32)])
    return pad_start.astype(I32), bexp, bidx, new, live


def _layer(lay, xp, xs, pos, cond8, h0f, h0b, lp, alpha, tm_proj=512, tm_route=256, tm_comb=256):
    (w_ada, b_ada, w_in, conv_w, conv_b, conv_ln_g, conv_ln_b, ssm_conv_w, ssm_conv_b, dt_bias, a_log,
     d_skip, ssm_norm_g, w_out, ln1_g, ln1_b, w_router, router_bias, w_exp_gu, w_exp_down, w_sh_gu,
     w_sh_down, ln2_g, ln2_b) = lp
    T = lay.n_tokens
    n_main = 2 * D_CONV + D_SSM + D_XBC
    w_main = w_in[:, :n_main].astype(BF16)
    w_dt = jnp.pad(w_in[:, n_main:], ((0, 0), (0, 128 - 2 * N_HEADS))).astype(BF16)

    mod = _ada(cond8, w_ada, b_ada)
    glu, z, xbc, dt_raw = _inproj(lay, xp, xs, pos, mod, w_main, w_dt, tm_proj)
    conv_out, xbc_c = _conv(lay, glu, xbc, conv_w, conv_b, conv_ln_g, conv_ln_b, ssm_conv_w, ssm_conv_b)
    y_ssm, hf, hb = _ssd(lay, xbc_c, z, dt_raw, h0f, h0b, dt_bias, a_log, d_skip, ssm_norm_g)
    x1, h2, h2p = _outproj(lay, xp, xs, pos, mod, conv_out, y_ssm, w_out.astype(BF16), ln1_g, ln1_b, alpha, tm_proj)

    idx, wts, posn, cnt = _route(h2, w_router.T, router_bias, tm_route)
    n_blocks = -(-T * TOP_K // EXPERT_BLOCK) + N_EXPERTS
    pad_start, bexp, bidx, new, live = _moe_plan(cnt[:, 0], n_blocks)
    dest2 = _dest(idx, posn, pad_start, 512)
    n_rows = n_blocks * EXPERT_BLOCK
    scatter_idx = jnp.concatenate([dest2 + c * n_rows for c in range(ROW_PARTS)], axis=1)
    x_sorted = _sc_scatter(h2p.reshape(ROW_PARTS * T, PART_WORDS), scatter_idx.reshape(-1),
                           ROW_PARTS * n_rows, TOP_K).reshape(ROW_PARTS, n_rows, PART_WORDS)
    y_sorted = _expert(x_sorted, w_exp_gu, w_exp_down, bexp, bidx, new, live, n_blocks)
    dest = dest2.reshape(-1)
    idx_parts = jnp.concatenate([dest + c * n_rows for c in range(ROW_PARTS)])
    y_tok = _sc_gather(y_sorted.reshape(ROW_PARTS * n_rows, PART_WORDS), idx_parts)
    y_tok = y_tok.reshape(ROW_PARTS, TOP_K, T, PART_WORDS)
    out_p, out_s = _combine(lay, h2, x1, wts.T, mod, w_sh_gu.astype(BF16), w_sh_down.astype(BF16),
                            ln2_g, ln2_b, y_tok, alpha, tm_comb)
    return out_p, out_s, hf, hb


def kernel(x_prompt, x_sample, state_ssd_fwd, state_ssd_bwd, c, c_ctx, w_ada, b_ada, w_in, conv_w, conv_b, conv_ln_g, conv_ln_b, ssm_conv_w, ssm_conv_b, dt_bias, a_log, d_skip, ssm_norm_g, w_out, ln1_g, ln1_b, w_router, router_bias, w_exp_gu, w_exp_down, w_sh_gu, w_sh_down, ln2_g, ln2_b):
    depth = w_ada.shape[0]
    assert depth == 1, "the prompt and latent passes are fused per layer; one layer is supported"
    bp, lp_, _ = x_prompt.shape
    bd, ld, _ = x_sample.shape
    lay = _Layout(bp, lp_, bd, ld)
    alpha = (2.0 * depth) ** 0.25
    stacked = (w_ada, b_ada, w_in, conv_w, conv_b, conv_ln_g, conv_ln_b, ssm_conv_w, ssm_conv_b,
               dt_bias, a_log, d_skip, ssm_norm_g, w_out, ln1_g, ln1_b, w_router, router_bias,
               w_exp_gu, w_exp_down, w_sh_gu, w_sh_down, ln2_g, ln2_b)
    lp = [w[0] for w in stacked]
    cond8 = jnp.concatenate([c_ctx[None, :], c, jnp.zeros((8 - 1 - bd, D_MODEL), F32)], axis=0)
    pos = _grid_pos_embed(ld)
    sshape = (bd, N_HEADS, HEADDIM, D_STATE)
    out_p, out_s, hf, hb = _layer(lay, x_prompt.reshape(bp * lp_, D_MODEL), x_sample.reshape(bd * ld, D_MODEL),
                                  pos, cond8, state_ssd_fwd[:, 0].reshape(sshape),
                                  state_ssd_bwd[:, 0].reshape(sshape), lp, alpha)
    return (out_p.reshape(bp, lp_, D_MODEL), out_s.reshape(bd, ld, D_MODEL),
            hf[:, None], hb[:, None])
```

```python
import functools
import math

import numpy as np
import jax
import jax.numpy as jnp
from jax import lax
from jax.experimental import pallas as pl
from jax.experimental.pallas import tpu as pltpu
from jax.experimental.pallas import tpu_sc as plsc

F32 = jnp.float32
BF16 = jnp.bfloat16
I32 = jnp.int32
HI = lax.Precision.HIGHEST

D_MODEL = 1024
GRID_W = 64
D_CONV = 1024
CONV_K = 31
N_HEADS = 16
HEADDIM = 64
D_SSM = N_HEADS * HEADDIM
N_GROUPS = 4
HEADS_PER_GROUP = N_HEADS // N_GROUPS
D_STATE = 128
SSM_CONV_K = 4
CHUNK = 128
D_XBC = D_SSM + 2 * N_GROUPS * D_STATE
N_EXPERTS = 256
TOP_K = 8
N_EXPERT_GROUPS = 8
EXPERTS_PER_GROUP = N_EXPERTS // N_EXPERT_GROUPS
TOPK_GROUPS = 4
D_EXPERT = 256
D_SHARED = 256
ROUTED_SCALE = 2.5
LN_EPS = 1e-5

CONV_TILE = 256
HALO = 16
EXPERT_BLOCK = 256
ROW_PARTS = 2
PART_WORDS = D_MODEL // 2 // ROW_PARTS
VMEM_LIMIT = 56 * 1024 * 1024


def _cparams(sem, vmem=VMEM_LIMIT):
    return pltpu.CompilerParams(dimension_semantics=sem, vmem_limit_bytes=vmem)


def _silu(x):
    return x * jax.nn.sigmoid(x)


def _ln_rows(x):
    mu = jnp.mean(x, axis=-1, keepdims=True)
    xc = x - mu
    var = jnp.mean(xc * xc, axis=-1, keepdims=True)
    return xc * lax.rsqrt(var + LN_EPS)


def _iota(shape, dim):
    return lax.broadcasted_iota(I32, shape, dim)


def _expand_matrix(n_in, width):
    return (_iota((n_in, n_in * width), 0) == _iota((n_in, n_in * width), 1) // width).astype(F32)


def _dot_hi(a, b):
    return jnp.dot(a, b, precision=HI, preferred_element_type=F32)


def _split3(x):
    hi = x.astype(BF16)
    r1 = x - hi.astype(F32)
    mid = r1.astype(BF16)
    lo = (r1 - mid.astype(F32)).astype(BF16)
    return jnp.concatenate([hi, mid, lo], axis=1)


def _expand3(n, width):
    rows = np.arange(3 * n)[:, None] % n
    cols = np.arange(n * width)[None, :] // width
    return jnp.asarray(rows == cols, dtype=BF16)


def _expand_exact(x, e3):
    return jnp.dot(_split3(x), e3, preferred_element_type=F32)


def _ada_kernel(c_ref, w_ref, b_ref, o_ref):
    o_ref[...] = _dot_hi(_silu(c_ref[...]), w_ref[...]) + b_ref[...]


def _ada(cond8, w_ada, b_ada):
    n = w_ada.shape[1]
    tn = 1024
    return pl.pallas_call(
        _ada_kernel,
        grid=(n // tn,),
        in_specs=[pl.BlockSpec((8, D_MODEL), lambda j: (0, 0)),
                  pl.BlockSpec((D_MODEL, tn), lambda j: (0, j)),
                  pl.BlockSpec((1, tn), lambda j: (0, j))],
        out_specs=pl.BlockSpec((8, tn), lambda j: (0, j)),
        out_shape=jax.ShapeDtypeStruct((8, n), F32),
        compiler_params=_cparams(("arbitrary",)),
        name="ada",
    )(cond8, w_ada, b_ada.reshape(1, n))


def _inproj_kernel(row_ref, posb_ref, flag_ref, xp_ref, xs_ref, pos_ref, mod_ref, wm_ref, wdt_ref,
                   glu_ref, z_ref, xbc_ref, dt_ref):
    i = pl.program_id(0)
    x = jnp.where(flag_ref[i] == 1, xs_ref[...] + pos_ref[...], xp_ref[...])
    r = row_ref[i]
    sh1 = mod_ref[pl.ds(r, 1), 0:D_MODEL]
    sc1 = mod_ref[pl.ds(r, 1), D_MODEL:2 * D_MODEL]
    h = (_ln_rows(x) * (1.0 + sc1) + sh1).astype(BF16)
    glu_ref[...] = jnp.dot(h, wm_ref[:, 0:2 * D_CONV], preferred_element_type=F32).astype(BF16)
    z_ref[...] = jnp.dot(h, wm_ref[:, 2 * D_CONV:2 * D_CONV + D_SSM], preferred_element_type=F32).astype(BF16)
    xbc_ref[...] = jnp.dot(h, wm_ref[:, 2 * D_CONV + D_SSM:], preferred_element_type=F32).astype(BF16)
    dt_ref[...] = jnp.dot(h, wdt_ref[...], preferred_element_type=F32)


def _inproj(lay, xp, xs, pos, mod, w_main, w_dt, tm):
    T = lay.n_tokens
    row, posb, flag = lay.token_tile_tables(tm)
    npt = lay.n_prompt_tokens // tm
    n_main = w_main.shape[1]
    gs = pltpu.PrefetchScalarGridSpec(
        num_scalar_prefetch=3,
        grid=(T // tm,),
        in_specs=[pl.BlockSpec((tm, D_MODEL), lambda i, r, p, f: (jnp.minimum(i, npt - 1), 0)),
                  pl.BlockSpec((tm, D_MODEL), lambda i, r, p, f: (jnp.maximum(i - npt, 0), 0)),
                  pl.BlockSpec((tm, D_MODEL), lambda i, r, p, f: (p[i], 0)),
                  pl.BlockSpec((8, 6 * D_MODEL), lambda i, r, p, f: (0, 0)),
                  pl.BlockSpec((D_MODEL, n_main), lambda i, r, p, f: (0, 0)),
                  pl.BlockSpec((D_MODEL, 128), lambda i, r, p, f: (0, 0))],
        out_specs=[pl.BlockSpec((tm, 2 * D_CONV), lambda i, r, p, f: (i, 0)),
                   pl.BlockSpec((tm, D_SSM), lambda i, r, p, f: (i, 0)),
                   pl.BlockSpec((tm, D_XBC), lambda i, r, p, f: (i, 0)),
                   pl.BlockSpec((tm, 128), lambda i, r, p, f: (i, 0))])
    return pl.pallas_call(
        _inproj_kernel, grid_spec=gs,
        out_shape=(jax.ShapeDtypeStruct((T, 2 * D_CONV), BF16),
                   jax.ShapeDtypeStruct((T, D_SSM), BF16),
                   jax.ShapeDtypeStruct((T, D_XBC), BF16),
                   jax.ShapeDtypeStruct((T, 128), F32)),
        compiler_params=_cparams(("arbitrary",)),
        name="inproj",
    )(row, posb, flag, xp, xs, pos, mod, w_main, w_dt)


_N_SHIFT = 8
_SHIFT_ROWS = CONV_TILE + 2 * HALO - _N_SHIFT
_ROW_BLOCK = 32


def _conv_kernel(lok_ref, rok_ref, glu_ref, glul_ref, glur_ref, xbc_ref, xbcl_ref, xbcr_ref,
                 cw_ref, cb_ref, lng_ref, lnb_ref, sw_ref, sb_ref, co_ref, xo_ref,
                 ext_ref, sh_ref, acc_ref, ext2_ref):
    i = pl.program_id(0)
    lok = lok_ref[i] == 1
    rok = rok_ref[i] == 1

    def glu(ref):
        v = ref[...].astype(F32)
        return v[:, 0:D_CONV] * jax.nn.sigmoid(v[:, D_CONV:])

    ext_ref[0:HALO, :] = jnp.where(lok, glu(glul_ref), 0.0)
    ext_ref[HALO:HALO + CONV_TILE, :] = glu(glu_ref)
    ext_ref[HALO + CONV_TILE:, :] = jnp.where(rok, glu(glur_ref), 0.0)
    for r in range(_N_SHIFT):
        sh_ref[r] = ext_ref[r:r + _SHIFT_ROWS, :]

    first = HALO - (CONV_K - 1) // 2

    def row_block(rb, carry):
        base = pl.multiple_of(rb * _ROW_BLOCK, _ROW_BLOCK)
        for j in range(D_CONV // 128):
            lanes = slice(j * 128, (j + 1) * 128)
            acc = jnp.zeros((_ROW_BLOCK, 128), F32) + cb_ref[:, lanes]
            for k in range(CONV_K):
                o = first + k
                win = sh_ref[o % _N_SHIFT, pl.ds(base + (o // _N_SHIFT) * _N_SHIFT, _ROW_BLOCK), lanes]
                acc = acc + win * cw_ref[k:k + 1, lanes]
            acc_ref[pl.ds(base, _ROW_BLOCK), lanes] = acc
        return carry

    lax.fori_loop(0, CONV_TILE // _ROW_BLOCK, row_block, 0)
    u = _ln_rows(acc_ref[...]) * lng_ref[...] + lnb_ref[...]
    co_ref[...] = _silu(u).astype(BF16)

    ext2_ref[0:HALO, :] = jnp.where(lok, xbcl_ref[...].astype(F32), 0.0)
    ext2_ref[HALO:HALO + CONV_TILE, :] = xbc_ref[...].astype(F32)
    ext2_ref[HALO + CONV_TILE:, :] = jnp.where(rok, xbcr_ref[...].astype(F32), 0.0)
    first2 = HALO - (SSM_CONV_K - 1) // 2
    half = D_XBC // 2
    for hcol in range(2):
        lanes = slice(hcol * half, (hcol + 1) * half)
        y = jnp.zeros((CONV_TILE, half), F32) + sb_ref[:, lanes]
        for k in range(SSM_CONV_K):
            y = y + ext2_ref[first2 + k:first2 + k + CONV_TILE, lanes] * sw_ref[k:k + 1, lanes]
        xo_ref[:, lanes] = _silu(y).astype(BF16)


def _conv(lay, glu, xbc, conv_w, conv_b, ln_g, ln_b, ssm_w, ssm_b):
    T = lay.n_tokens
    lok, rok = lay.conv_tile_tables()
    n_tiles = T // CONV_TILE
    hb = CONV_TILE // HALO
    n_hb = T // HALO

    def cur(i, l, r):
        return (i, 0)

    def left(i, l, r):
        return (jnp.maximum(i * hb - 1, 0), 0)

    def right(i, l, r):
        return (jnp.minimum((i + 1) * hb, n_hb - 1), 0)

    def const(i, l, r):
        return (0, 0)

    gs = pltpu.PrefetchScalarGridSpec(
        num_scalar_prefetch=2,
        grid=(n_tiles,),
        in_specs=[pl.BlockSpec((CONV_TILE, 2 * D_CONV), cur),
                  pl.BlockSpec((HALO, 2 * D_CONV), left),
                  pl.BlockSpec((HALO, 2 * D_CONV), right),
                  pl.BlockSpec((CONV_TILE, D_XBC), cur),
                  pl.BlockSpec((HALO, D_XBC), left),
                  pl.BlockSpec((HALO, D_XBC), right),
                  pl.BlockSpec((CONV_K, D_CONV), const),
                  pl.BlockSpec((1, D_CONV), const),
                  pl.BlockSpec((1, D_CONV), const),
                  pl.BlockSpec((1, D_CONV), const),
                  pl.BlockSpec((SSM_CONV_K, D_XBC), const),
                  pl.BlockSpec((1, D_XBC), const)],
        out_specs=[pl.BlockSpec((CONV_TILE, D_CONV), cur),
                   pl.BlockSpec((CONV_TILE, D_XBC), cur)],
        scratch_shapes=[pltpu.VMEM((CONV_TILE + 2 * HALO, D_CONV), F32),
                        pltpu.VMEM((_N_SHIFT, _SHIFT_ROWS, D_CONV), F32),
                        pltpu.VMEM((CONV_TILE, D_CONV), F32),
                        pltpu.VMEM((CONV_TILE + 2 * HALO, D_XBC), F32)])
    return pl.pallas_call(
        _conv_kernel, grid_spec=gs,
        out_shape=(jax.ShapeDtypeStruct((T, D_CONV), BF16),
                   jax.ShapeDtypeStruct((T, D_XBC), BF16)),
        compiler_params=_cparams(("arbitrary",)),
        name="conv",
    )(lok, rok, glu, glu, glu, xbc, xbc, xbc, conv_w, conv_b.reshape(1, -1), ln_g.reshape(1, -1),
      ln_b.reshape(1, -1), ssm_w, ssm_b.reshape(1, -1))


_BN = N_GROUPS * D_STATE


def _ssd_kernel(chunk_ref, yidx_ref, phase_ref, first_ref, last_ref, zero_ref, sin_ref, sout_ref, cloc_ref,
                xbc_ref, z_ref, dt_ref, h0f_ref, h0b_ref, dtb_ref, alog_ref, dsk_ref, ng_ref,
                tri_ref, edec_ref, ewb_ref, ecol_ref, ewide_ref, eye3_ref,
                y_ref, hf_out_ref, hb_out_ref,
                hf_ref, g_ref, gin_ref, ybuf_ref):
    s = pl.program_id(0)
    phase = phase_ref[s]
    first = first_ref[s] == 1
    last = last_ref[s] == 1
    zero = zero_ref[s] == 1
    cloc = cloc_ref[s]
    H, P, N = N_HEADS, HEADDIM, D_STATE

    GW = HEADS_PER_GROUP * P
    xs = xbc_ref[:, 0:D_SSM]
    dt = dt_ref[:, 0:2 * H] + dtb_ref[...]
    dt = jnp.maximum(dt, 0.0) + jnp.log1p(jnp.exp(-jnp.abs(dt)))
    a = dt * (-jnp.exp(alog_ref[...]))
    a3 = jnp.dot(tri_ref[...], _split3(a), preferred_element_type=F32)
    acs = a3[:, 0:2 * H] + a3[:, 2 * H:4 * H] + a3[:, 4 * H:6 * H]
    tot = acs[CHUNK - 8:CHUNK, :]
    dec = _expand_exact(jnp.exp(tot), edec_ref[...])[7:8, :]
    exb = acs[:, H:2 * H] - a[:, H:2 * H]

    def load_state(src_ref, dst_ref):
        for j in range(H // 2):
            pair = jnp.concatenate([src_ref[0, 2 * j], src_ref[0, 2 * j + 1]], axis=0)
            dst_ref[:, 2 * j * P:(2 * j + 2) * P] = jnp.where(zero, 0.0, pair.T)

    def store_state(src_ref, dst_ref):
        for j in range(H // 2):
            pair = src_ref[:, 2 * j * P:(2 * j + 2) * P].T
            dst_ref[0, 2 * j] = pair[0:P]
            dst_ref[0, 2 * j + 1] = pair[P:2 * P]

    @pl.when(phase == 0)
    def _backward_states():
        @pl.when(first)
        def _():
            load_state(h0b_ref, g_ref)

        wb = dt[:, H:2 * H] * jnp.exp(exb)
        xw = (xs.astype(F32) * _expand_exact(wb, ewb_ref[...])).astype(BF16)
        for g in range(N_GROUPS):
            cols = slice(g * GW, (g + 1) * GW)
            bg = xbc_ref[:, D_SSM + g * N:D_SSM + (g + 1) * N]
            gg = g_ref[:, cols]
            gin_ref[cloc, :, cols] = gg.astype(BF16)
            upd = lax.dot_general(bg, xw[:, cols], (((0,), (0,)), ((), ())), preferred_element_type=F32)
            g_ref[:, cols] = gg * dec[:, D_SSM + g * GW:D_SSM + (g + 1) * GW] + upd

        @pl.when(last)
        def _():
            store_state(g_ref, hb_out_ref)

    @pl.when(phase == 1)
    def _forward_and_outputs():
        @pl.when(first)
        def _():
            load_state(h0f_ref, hf_ref)

        acsf = acs[:, 0:H]
        dtf = dt[:, 0:H]
        dtb = dt[:, H:2 * H]
        totf = acs[CHUNK - 1:CHUNK, 0:H]
        totb = acs[CHUNK - 1:CHUNK, H:2 * H]
        col = _expand_exact(jnp.concatenate([acsf, exb], axis=1), ecol_ref[...])
        q3 = _split3(jnp.concatenate([acsf, exb, dtf, dtb], axis=1))
        qt = lax.dot_general(eye3_ref[...], q3, (((1,), (1,)), ((), ())),
                             preferred_element_type=F32)
        wide = jnp.concatenate([dtf * jnp.exp(totf - acsf), jnp.exp(acsf), jnp.exp(totb - exb)], axis=1)
        wide = _expand_exact(wide, ewide_ref[...])
        xsf = xs.astype(F32)
        xw = (xsf * wide[:, 0:D_SSM]).astype(BF16)
        lower = _iota((CHUNK, CHUNK), 1) <= _iota((CHUNK, CHUNK), 0)
        upper = _iota((CHUNK, CHUNK), 1) >= _iota((CHUNK, CHUNK), 0)
        for g in range(N_GROUPS):
            cols = slice(g * GW, (g + 1) * GW)
            bg = xbc_ref[:, D_SSM + g * N:D_SSM + (g + 1) * N]
            cg = xbc_ref[:, D_SSM + _BN + g * N:D_SSM + _BN + (g + 1) * N]
            cb = lax.dot_general(cg, bg, (((1,), (1,)), ((), ())), preferred_element_type=F32)
            hfg = hf_ref[:, cols]
            yf = jnp.dot(cg, hfg.astype(BF16), preferred_element_type=F32)
            yb = jnp.dot(cg, gin_ref[cloc, :, cols], preferred_element_type=F32)
            ybuf_ref[:, cols] = yf * wide[:, D_SSM + g * GW:D_SSM + (g + 1) * GW] \
                + yb * wide[:, 2 * D_SSM + g * GW:2 * D_SSM + (g + 1) * GW]
            upd = lax.dot_general(bg, xw[:, cols], (((0,), (0,)), ((), ())), preferred_element_type=F32)
            hf_ref[:, cols] = hfg * dec[:, cols] + upd
            for r in range(HEADS_PER_GROUP):
                h = g * HEADS_PER_GROUP + r
                colf = col[:, h * N:(h + 1) * N]
                colb = col[:, (H + h) * N:(H + h + 1) * N]
                mf = jnp.where(lower, jnp.exp(colf - qt[h:h + 1, :]), 0.0) * qt[2 * H + h:2 * H + h + 1, :]
                mb = jnp.where(upper, jnp.exp(qt[H + h:H + h + 1, :] - colb), 0.0) * qt[3 * H + h:3 * H + h + 1, :]
                m = (cb * (mf + mb)).astype(BF16)
                hs = slice(h * P, (h + 1) * P)
                ybuf_ref[:, hs] += jnp.dot(m, xs[:, hs], preferred_element_type=F32)

        yt = (ybuf_ref[...] + dsk_ref[...] * xsf) * _silu(z_ref[...].astype(F32))
        gw = D_SSM // N_GROUPS
        for g in range(N_GROUPS):
            seg = yt[:, g * gw:(g + 1) * gw]
            ms = jnp.mean(seg * seg, axis=-1, keepdims=True)
            y_ref[:, g * gw:(g + 1) * gw] = (seg * lax.rsqrt(ms + LN_EPS) * ng_ref[:, g * gw:(g + 1) * gw]).astype(BF16)

        @pl.when(last)
        def _():
            store_state(hf_ref, hf_out_ref)


def _ssd(lay, xbc_c, z, dt_raw, h0f, h0b, dt_bias, a_log, d_skip, norm_g):
    T = lay.n_tokens
    tabs = lay.ssd_step_tables()
    n_steps = tabs[0].shape[0]
    nsp = len(tabs)

    def by_chunk(s, *t):
        return (t[0][s], 0)

    def by_y(s, *t):
        return (t[1][s], 0)

    def by_sin(s, *t):
        return (t[6][s], 0, 0, 0)

    def by_sout(s, *t):
        return (t[7][s], 0, 0, 0)

    def const(s, *t):
        return (0, 0)

    H = N_HEADS
    tri = jnp.asarray(np.tril(np.ones((CHUNK, CHUNK))), dtype=BF16)
    eye3 = jnp.asarray(np.arange(4 * H)[:, None] == np.arange(12 * H)[None, :] % (4 * H), dtype=BF16)
    consts = [tri, _expand3(2 * H, HEADDIM), _expand3(H, HEADDIM), _expand3(2 * H, D_STATE),
              _expand3(3 * H, HEADDIM), eye3]
    sshape = (1, N_HEADS, HEADDIM, D_STATE)
    gs = pltpu.PrefetchScalarGridSpec(
        num_scalar_prefetch=nsp,
        grid=(n_steps,),
        in_specs=[pl.BlockSpec((CHUNK, D_XBC), by_chunk),
                  pl.BlockSpec((CHUNK, D_SSM), by_chunk),
                  pl.BlockSpec((CHUNK, 128), by_chunk),
                  pl.BlockSpec(sshape, by_sin),
                  pl.BlockSpec(sshape, by_sin),
                  pl.BlockSpec((1, 2 * N_HEADS), const),
                  pl.BlockSpec((1, 2 * N_HEADS), const),
                  pl.BlockSpec((1, D_SSM), const),
                  pl.BlockSpec((1, D_SSM), const)] + [pl.BlockSpec(c.shape, const) for c in consts],
        out_specs=[pl.BlockSpec((CHUNK, D_SSM), by_y),
                   pl.BlockSpec(sshape, by_sout),
                   pl.BlockSpec(sshape, by_sout)],
        scratch_shapes=[pltpu.VMEM((D_STATE, D_SSM), F32),
                        pltpu.VMEM((D_STATE, D_SSM), F32),
                        pltpu.VMEM((lay.max_chunks, D_STATE, D_SSM), BF16),
                        pltpu.VMEM((CHUNK, D_SSM), F32)])
    n_out = lay.n_prompt_seqs
    return pl.pallas_call(
        _ssd_kernel, grid_spec=gs,
        out_shape=(jax.ShapeDtypeStruct((T, D_SSM), BF16),
                   jax.ShapeDtypeStruct((n_out,) + sshape[1:], F32),
                   jax.ShapeDtypeStruct((n_out,) + sshape[1:], F32)),
        compiler_params=_cparams(("arbitrary",)),
        name="ssd",
    )(*tabs, xbc_c, z, dt_raw, h0f, h0b, dt_bias.reshape(1, -1), a_log.reshape(1, -1),
      jnp.repeat(d_skip, HEADDIM).reshape(1, -1), norm_g.reshape(1, -1), *consts)


def _outproj_kernel(row_ref, posb_ref, flag_ref, xp_ref, xs_ref, pos_ref, mod_ref, co_ref, ys_ref, wo_ref,
                    g_ref, b_ref, x1_ref, h2_ref, h2p_ref, *, alpha):
    i = pl.program_id(0)
    x = jnp.where(flag_ref[i] == 1, xs_ref[...] + pos_ref[...], xp_ref[...])
    r = row_ref[i]
    g1 = mod_ref[pl.ds(r, 1), 2 * D_MODEL:3 * D_MODEL]
    sh2 = mod_ref[pl.ds(r, 1), 3 * D_MODEL:4 * D_MODEL]
    sc2 = mod_ref[pl.ds(r, 1), 4 * D_MODEL:5 * D_MODEL]
    mix = jnp.dot(co_ref[...], wo_ref[0:D_CONV, :], preferred_element_type=F32) \
        + jnp.dot(ys_ref[...], wo_ref[D_CONV:, :], preferred_element_type=F32)
    x1 = _ln_rows(alpha * x + g1 * mix) * g_ref[...] + b_ref[...]
    x1_ref[...] = x1
    h2 = _ln_rows(x1) * (1.0 + sc2) + sh2
    h2_ref[...] = h2
    packed = _pack_halves(h2)
    for c in range(ROW_PARTS):
        h2p_ref[c] = packed[:, c * PART_WORDS:(c + 1) * PART_WORDS]


def _outproj(lay, xp, xs, pos, mod, conv_out, y_ssm, w_out, ln_g, ln_b, alpha, tm):
    T = lay.n_tokens
    row, posb, flag = lay.token_tile_tables(tm)
    npt = lay.n_prompt_tokens // tm

    def const(i, r, p, f):
        return (0, 0)

    def cur(i, r, p, f):
        return (i, 0)

    gs = pltpu.PrefetchScalarGridSpec(
        num_scalar_prefetch=3,
        grid=(T // tm,),
        in_specs=[pl.BlockSpec((tm, D_MODEL), lambda i, r, p, f: (jnp.minimum(i, npt - 1), 0)),
                  pl.BlockSpec((tm, D_MODEL), lambda i, r, p, f: (jnp.maximum(i - npt, 0), 0)),
                  pl.BlockSpec((tm, D_MODEL), lambda i, r, p, f: (p[i], 0)),
                  pl.BlockSpec((8, 6 * D_MODEL), const),
                  pl.BlockSpec((tm, D_CONV), cur),
                  pl.BlockSpec((tm, D_SSM), cur),
                  pl.BlockSpec((D_CONV + D_SSM, D_MODEL), const),
                  pl.BlockSpec((1, D_MODEL), const),
                  pl.BlockSpec((1, D_MODEL), const)],
        out_specs=[pl.BlockSpec((tm, D_MODEL), cur),
                   pl.BlockSpec((tm, D_MODEL), cur),
                   pl.BlockSpec((ROW_PARTS, tm, PART_WORDS), lambda i, r, p, f: (0, i, 0))])
    return pl.pallas_call(
        functools.partial(_outproj_kernel, alpha=alpha), grid_spec=gs,
        out_shape=(jax.ShapeDtypeStruct((T, D_MODEL), F32),
                   jax.ShapeDtypeStruct((T, D_MODEL), F32),
                   jax.ShapeDtypeStruct((ROW_PARTS, T, PART_WORDS), jnp.uint32)),
        compiler_params=_cparams(("arbitrary",)),
        name="outproj",
    )(row, posb, flag, xp, xs, pos, mod, conv_out, y_ssm, w_out, ln_g.reshape(1, -1), ln_b.reshape(1, -1))


def _route_kernel(h2_ref, wrt_ref, bias_ref, idx_ref, wts_ref, pos_ref, cnt_ref, carry_ref, *, tm):
    i = pl.program_id(0)

    @pl.when(i == 0)
    def _():
        carry_ref[...] = jnp.zeros_like(carry_ref)

    E, NG, EG = N_EXPERTS, N_EXPERT_GROUPS, EXPERTS_PER_GROUP
    neg = -jnp.inf
    logits = lax.dot_general(wrt_ref[...], h2_ref[...], (((1,), (1,)), ((), ())), precision=HI,
                             preferred_element_type=F32)
    s = jax.nn.sigmoid(logits)
    sel = s + bias_ref[...]
    sel3 = sel.reshape(NG, EG, tm)
    io3 = _iota((NG, EG, tm), 1)
    m1 = jnp.max(sel3, axis=1, keepdims=True)
    f1 = jnp.min(jnp.where(sel3 == m1, io3, EG), axis=1, keepdims=True)
    m2 = jnp.max(jnp.where(io3 == f1, neg, sel3), axis=1, keepdims=True)
    gscore = (m1 + m2).reshape(NG, tm)
    gio = _iota((NG, tm), 0)
    beaten = jnp.zeros((NG, tm), I32)
    for g in range(NG):
        row = gscore[g:g + 1, :]
        beats = jnp.where(row > gscore, 1, jnp.where(row == gscore, jnp.where(g < gio, 1, 0), 0))
        beaten = beaten + beats
    keep = (beaten < TOPK_GROUPS).astype(F32).reshape(NG, 1, tm)
    selm = jnp.where(keep > 0.5, sel3, neg).reshape(E, tm)
    eio = _iota((E, tm), 0)
    chosen = jnp.zeros((E, tm), F32)
    idxs, ws = [], []
    for k in range(TOP_K):
        m = jnp.max(selm, axis=0, keepdims=True)
        am = jnp.minimum(jnp.min(jnp.where(selm == m, eio, E), axis=0, keepdims=True), E - 1)
        hit = eio == am
        ws.append(jnp.sum(jnp.where(hit, s, 0.0), axis=0, keepdims=True))
        idxs.append(am)
        selm = jnp.where(hit, neg, selm)
        chosen = jnp.where(hit, 1.0, chosen)
    wsum = ws[0]
    for k in range(1, TOP_K):
        wsum = wsum + ws[k]
    before = (_iota((tm, tm), 0) < _iota((tm, tm), 1)).astype(BF16)
    prior = jnp.dot(chosen.astype(BF16), before, preferred_element_type=F32)
    carry = carry_ref[...]
    prior = prior + jnp.concatenate([carry] * (tm // 128), axis=1)
    for k in range(TOP_K):
        idx_ref[k:k + 1, :] = idxs[k]
        wts_ref[k:k + 1, :] = ws[k] / wsum * ROUTED_SCALE
        pos_ref[k:k + 1, :] = jnp.sum(jnp.where(eio == idxs[k], prior, 0.0), axis=0, keepdims=True).astype(I32)
    total = jnp.dot(chosen.astype(BF16), jnp.ones((tm, 128), BF16), preferred_element_type=F32)
    carry = carry + total
    carry_ref[...] = carry
    cnt_ref[...] = carry.astype(I32)


def _route(h2, w_router_t, router_bias, tm):
    T = h2.shape[0]
    bias_b = jnp.broadcast_to(router_bias.astype(F32)[:, None], (N_EXPERTS, tm))
    return pl.pallas_call(
        functools.partial(_route_kernel, tm=tm),
        grid=(T // tm,),
        in_specs=[pl.BlockSpec((tm, D_MODEL), lambda i: (i, 0)),
                  pl.BlockSpec((N_EXPERTS, D_MODEL), lambda i: (0, 0)),
                  pl.BlockSpec((N_EXPERTS, tm), lambda i: (0, 0))],
        out_specs=[pl.BlockSpec((TOP_K, tm), lambda i: (0, i)),
                   pl.BlockSpec((TOP_K, tm), lambda i: (0, i)),
                   pl.BlockSpec((TOP_K, tm), lambda i: (0, i)),
                   pl.BlockSpec((N_EXPERTS, 128), lambda i: (0, 0))],
        out_shape=(jax.ShapeDtypeStruct((TOP_K, T), I32),
                   jax.ShapeDtypeStruct((TOP_K, T), F32),
                   jax.ShapeDtypeStruct((TOP_K, T), I32),
                   jax.ShapeDtypeStruct((N_EXPERTS, 128), I32)),
        scratch_shapes=[pltpu.VMEM((N_EXPERTS, 128), F32)],
        compiler_params=_cparams(("arbitrary",)),
        name="route",
    )(h2, w_router_t, bias_b)


def _dest_kernel(idx_ref, pos_ref, start_ref, dest_ref):
    tm = idx_ref.shape[1]
    eio = _iota((N_EXPERTS, tm), 0)
    start = start_ref[...]
    for k in range(TOP_K):
        base = jnp.sum(jnp.where(eio == idx_ref[k:k + 1, :], start, 0.0), axis=0, keepdims=True)
        dest_ref[k:k + 1, :] = base.astype(I32) + pos_ref[k:k + 1, :]


def _dest(idx, pos, pad_start, tm):
    T = idx.shape[1]
    start_b = jnp.broadcast_to(pad_start.astype(F32)[:, None], (N_EXPERTS, tm))
    return pl.pallas_call(
        _dest_kernel,
        grid=(T // tm,),
        in_specs=[pl.BlockSpec((TOP_K, tm), lambda i: (0, i)),
                  pl.BlockSpec((TOP_K, tm), lambda i: (0, i)),
                  pl.BlockSpec((N_EXPERTS, tm), lambda i: (0, 0))],
        out_specs=pl.BlockSpec((TOP_K, tm), lambda i: (0, i)),
        out_shape=jax.ShapeDtypeStruct((TOP_K, T), I32),
        compiler_params=_cparams(("arbitrary",)),
        name="dest",
    )(idx, pos, start_b)


def _expert_kernel(bexp_ref, bidx_ref, new_ref, live_ref, next_ref, x_ref, wgu_hbm, wd_hbm, y_ref,
                   wgu_bf, wd_bf, wgu_stage, wd_stage, sems):
    i = pl.program_id(0)

    def fetch(e):
        return (pltpu.make_async_copy(wgu_hbm.at[e], wgu_stage, sems.at[0]),
                pltpu.make_async_copy(wd_hbm.at[e], wd_stage, sems.at[1]))

    @pl.when(i == 0)
    def _():
        for cp in fetch(bexp_ref[0]):
            cp.start()

    @pl.when(new_ref[i] == 1)
    def _():
        for cp in fetch(bexp_ref[i]):
            cp.wait()
        wgu_bf[...] = wgu_stage[...].astype(BF16)
        wd_bf[...] = wd_stage[...].astype(BF16)

        @pl.when(next_ref[i] >= 0)
        def _():
            for cp in fetch(next_ref[i]):
                cp.start()

    @pl.when(live_ref[i] == 1)
    def _():
        parts = [_unpack_halves(x_ref[c]) for c in range(ROW_PARTS)]
        chunks = [p[0] for p in parts] + [p[1] for p in parts]
        gu = jnp.zeros((EXPERT_BLOCK, 2 * D_EXPERT), F32)
        for j, xc in enumerate(chunks):
            gu = gu + jnp.dot(xc.astype(BF16), wgu_bf[j * PART_WORDS:(j + 1) * PART_WORDS, :],
                              preferred_element_type=F32)
        act = (_silu(gu[:, 0:D_EXPERT]) * gu[:, D_EXPERT:]).astype(BF16)
        packed = _pack_halves(jnp.dot(act, wd_bf[...], preferred_element_type=F32))
        for c in range(ROW_PARTS):
            y_ref[c] = packed[:, c * PART_WORDS:(c + 1) * PART_WORDS]


def _pack_halves(x):
    n = x.shape[1] // 2
    hi = lax.bitcast_convert_type(x[:, :n].astype(BF16).astype(F32), jnp.uint32)
    lo = lax.bitcast_convert_type(x[:, n:].astype(BF16).astype(F32), jnp.uint32)
    return hi | (lo >> 16)


def _unpack_halves(p):
    hi = lax.bitcast_convert_type(p & jnp.uint32(0xFFFF0000), F32)
    lo = lax.bitcast_convert_type(p << 16, F32)
    return hi, lo


def _expert(x_sorted, w_gu, w_down, bexp, bidx, new, live, nxt, n_blocks):
    n_rows = x_sorted.shape[1]
    gs = pltpu.PrefetchScalarGridSpec(
        num_scalar_prefetch=5,
        grid=(n_blocks,),
        in_specs=[pl.BlockSpec((ROW_PARTS, EXPERT_BLOCK, PART_WORDS), lambda i, e, b, n, l, x: (0, b[i], 0)),
                  pl.BlockSpec(memory_space=pl.ANY),
                  pl.BlockSpec(memory_space=pl.ANY)],
        out_specs=pl.BlockSpec((ROW_PARTS, EXPERT_BLOCK, PART_WORDS), lambda i, e, b, n, l, x: (0, b[i], 0)),
        scratch_shapes=[pltpu.VMEM((D_MODEL, 2 * D_EXPERT), BF16),
                        pltpu.VMEM((D_EXPERT, D_MODEL), BF16),
                        pltpu.VMEM((D_MODEL, 2 * D_EXPERT), F32),
                        pltpu.VMEM((D_EXPERT, D_MODEL), F32),
                        pltpu.SemaphoreType.DMA((2,))])
    return pl.pallas_call(
        _expert_kernel, grid_spec=gs,
        out_shape=jax.ShapeDtypeStruct((ROW_PARTS, n_rows, PART_WORDS), jnp.uint32),
        compiler_params=_cparams(("arbitrary",)),
        name="expert",
    )(bexp, bidx, new, live, nxt, x_sorted, w_gu, w_down)


def _combine_kernel(row_ref, h2_ref, x1_ref, wt_ref, mod_ref, wsg_ref, wsd_ref, g_ref, b_ref, yt_ref,
                    op_ref, os_ref, *, n_prompt_tiles, alpha):
    i = pl.program_id(0)
    h2 = h2_ref[...].astype(BF16)
    su = jnp.dot(h2, wsg_ref[...], preferred_element_type=F32)
    act = (_silu(su[:, 0:D_SHARED]) * su[:, D_SHARED:]).astype(BF16)
    moe = jnp.dot(act, wsd_ref[...], preferred_element_type=F32)
    wt = wt_ref[...]
    his, los = [], []
    for c in range(ROW_PARTS):
        rh = jnp.zeros((h2.shape[0], PART_WORDS), F32)
        rl = jnp.zeros((h2.shape[0], PART_WORDS), F32)
        for k in range(TOP_K):
            hi, lo = _unpack_halves(yt_ref[c, k])
            w = wt[:, k:k + 1]
            rh = rh + hi * w
            rl = rl + lo * w
        his.append(rh)
        los.append(rl)
    moe = moe + jnp.concatenate(his + los, axis=1)
    g2 = mod_ref[pl.ds(row_ref[i], 1), 5 * D_MODEL:6 * D_MODEL]
    out = _ln_rows(alpha * x1_ref[...] + g2 * moe) * g_ref[...] + b_ref[...]

    @pl.when(i < n_prompt_tiles)
    def _():
        op_ref[...] = out

    @pl.when(i >= n_prompt_tiles)
    def _():
        os_ref[...] = out


def _combine(lay, h2, x1, wts_tok, mod, w_sh_gu, w_sh_down, ln_g, ln_b, y_tok, alpha, tm):
    T = lay.n_tokens
    row, _, _ = lay.token_tile_tables(tm)
    npt = lay.n_prompt_tokens // tm

    def cur(i, r):
        return (i, 0)

    def const(i, r):
        return (0, 0)

    gs = pltpu.PrefetchScalarGridSpec(
        num_scalar_prefetch=1,
        grid=(T // tm,),
        in_specs=[pl.BlockSpec((tm, D_MODEL), cur),
                  pl.BlockSpec((tm, D_MODEL), cur),
                  pl.BlockSpec((tm, TOP_K), cur),
                  pl.BlockSpec((8, 6 * D_MODEL), const),
                  pl.BlockSpec((D_MODEL, 2 * D_SHARED), const),
                  pl.BlockSpec((D_SHARED, D_MODEL), const),
                  pl.BlockSpec((1, D_MODEL), const),
                  pl.BlockSpec((1, D_MODEL), const),
                  pl.BlockSpec((ROW_PARTS, TOP_K, tm, PART_WORDS), lambda i, r: (0, 0, i, 0))],
        out_specs=[pl.BlockSpec((tm, D_MODEL), lambda i, r: (jnp.minimum(i, npt - 1), 0)),
                   pl.BlockSpec((tm, D_MODEL), lambda i, r: (jnp.maximum(i - npt, 0), 0))])
    return pl.pallas_call(
        functools.partial(_combine_kernel, n_prompt_tiles=npt, alpha=alpha), grid_spec=gs,
        out_shape=(jax.ShapeDtypeStruct((lay.n_prompt_tokens, D_MODEL), F32),
                   jax.ShapeDtypeStruct((T - lay.n_prompt_tokens, D_MODEL), F32)),
        compiler_params=_cparams(("arbitrary",)),
        name="combine",
    )(row, h2, x1, wts_tok, mod, w_sh_gu, w_sh_down, ln_g.reshape(1, -1), ln_b.reshape(1, -1), y_tok)


_GATHER_WINDOW = 128


def _sc_gather(table, idx):
    n, d = idx.shape[0], table.shape[1]
    mesh = plsc.VectorSubcoreMesh(core_axis_name="core", subcore_axis_name="subcore")

    @pl.kernel(out_type=jax.ShapeDtypeStruct((n, d), table.dtype), mesh=mesh)
    def gather_kernel(table_hbm, idx_hbm, out_hbm):
        def body(idx_vmem, out_vmem):
            pltpu.sync_copy(table_hbm.at[idx_vmem.at[0]], out_vmem)

        pltpu.emit_pipeline(
            body,
            grid=(n // _GATHER_WINDOW,),
            in_specs=[pl.BlockSpec((1, _GATHER_WINDOW), index_map=lambda i: (0, i))],
            out_specs=[pl.BlockSpec((_GATHER_WINDOW, d), index_map=lambda i: (i, 0))],
            core_axis_name=("core", "subcore"),
            dimension_semantics=(pltpu.PARALLEL,),
        )(idx_hbm, out_hbm)

    return gather_kernel(table, idx.reshape(1, n))


def _sc_scatter(rows, idx, n_out, repeat):
    n, d = rows.shape
    tiles = n // _GATHER_WINDOW
    mesh = plsc.VectorSubcoreMesh(core_axis_name="core", subcore_axis_name="subcore")

    @pl.kernel(out_type=jax.ShapeDtypeStruct((n_out, d), rows.dtype), mesh=mesh, scratch_types=[])
    def scatter_kernel(rows_hbm, idx_hbm, out_hbm):
        def body(rows_vmem, idx_vmem):
            pltpu.sync_copy(rows_vmem, out_hbm.at[idx_vmem.at[0]])

        pltpu.emit_pipeline(
            body,
            grid=(repeat * tiles,),
            in_specs=[pl.BlockSpec((_GATHER_WINDOW, d), index_map=lambda i: (i % tiles, 0)),
                      pl.BlockSpec((1, _GATHER_WINDOW), index_map=lambda i: (0, i))],
            out_specs=[],
            core_axis_name=("core", "subcore"),
            dimension_semantics=(pltpu.PARALLEL,),
        )(rows_hbm, idx_hbm)

    return scatter_kernel(rows, idx.reshape(1, repeat * n))


class _Layout:
    def __init__(self, n_prompt_seqs, prompt_len, n_sample_seqs, sample_len):
        self.n_prompt_seqs, self.prompt_len = n_prompt_seqs, prompt_len
        self.n_sample_seqs, self.sample_len = n_sample_seqs, sample_len
        self.n_prompt_tokens = n_prompt_seqs * prompt_len
        self.n_tokens = self.n_prompt_tokens + n_sample_seqs * sample_len
        assert prompt_len % CONV_TILE == 0 and sample_len % CONV_TILE == 0
        self.max_chunks = max(prompt_len, sample_len) // CHUNK

    def token_tile_tables(self, tm):
        assert self.n_prompt_tokens % tm == 0 and self.sample_len % tm == 0
        npt = self.n_prompt_tokens // tm
        per_seq = self.sample_len // tm
        n = self.n_tokens // tm
        row = np.zeros(n, np.int32)
        posb = np.zeros(n, np.int32)
        flag = np.zeros(n, np.int32)
        for i in range(npt, n):
            j = i - npt
            row[i] = 1 + j // per_seq
            posb[i] = j % per_seq
            flag[i] = 1
        return jnp.asarray(row), jnp.asarray(posb), jnp.asarray(flag)

    def conv_tile_tables(self):
        lok, rok = [], []
        for n_seq, length in ((self.n_prompt_seqs, self.prompt_len), (self.n_sample_seqs, self.sample_len)):
            per = length // CONV_TILE
            for _ in range(n_seq):
                for j in range(per):
                    lok.append(int(j > 0))
                    rok.append(int(j < per - 1))
        return jnp.asarray(np.array(lok, np.int32)), jnp.asarray(np.array(rok, np.int32))

    def ssd_step_tables(self):
        cols = [[] for _ in range(9)]
        seqs = []
        c0 = self.n_prompt_tokens // CHUNK
        for j in range(self.n_sample_seqs):
            nc = self.sample_len // CHUNK
            seqs.append((c0 + j * nc, nc, 0, j, 0))
        for j in range(self.n_prompt_seqs):
            nc = self.prompt_len // CHUNK
            seqs.append((j * nc, nc, 1, 0, j))
        for base, nc, zero, sin, sout in seqs:
            for phase in (0, 1):
                order = range(nc - 1, -1, -1) if phase == 0 else range(nc)
                for n, c in enumerate(order):
                    vals = (base + c, base if phase == 0 else base + c, phase, int(n == 0), int(n == nc - 1),
                            zero, sin, sout, c)
                    for col, v in zip(cols, vals):
                        col.append(v)
        return tuple(jnp.asarray(np.array(col, np.int32)) for col in cols)


def _grid_pos_embed(n_tokens):
    rows = n_tokens // GRID_W
    quarter = D_MODEL // 4
    freq = jnp.exp(-math.log(10000.0) * jnp.arange(quarter, dtype=F32) / quarter)
    r = jnp.broadcast_to(jnp.arange(rows, dtype=F32)[:, None, None] * freq, (rows, GRID_W, quarter))
    cl = jnp.broadcast_to(jnp.arange(GRID_W, dtype=F32)[None, :, None] * freq, (rows, GRID_W, quarter))
    emb = jnp.concatenate([jnp.sin(r), jnp.cos(r), jnp.sin(cl), jnp.cos(cl)], axis=-1)
    return emb.reshape(rows * GRID_W, D_MODEL)


def _moe_plan(counts, n_blocks):
    blk = EXPERT_BLOCK
    padded = (counts + blk - 1) // blk * blk
    pad_end = jnp.cumsum(padded)
    pad_start = pad_end - padded
    n_used = pad_end[-1] // blk
    b = jnp.arange(n_blocks, dtype=I32)
    live = (b < n_used).astype(I32)
    bidx = jnp.minimum(b, jnp.maximum(n_used - 1, 0)).astype(I32)
    bexp = jnp.sum((pad_end[None, :] <= (bidx * blk)[:, None]).astype(I32), axis=1)
    bexp = jnp.minimum(bexp, N_EXPERTS - 1).astype(I32)
    new = jnp.concatenate([jnp.ones((1,), I32), (bexp[1:] != bexp[:-1]).astype(I32)])
    first_at = jnp.where(new == 1, b, n_blocks)
    next_at = jnp.concatenate([lax.cummin(first_at, reverse=True)[1:], jnp.full((1,), n_blocks, I32)])
    nxt = jnp.where(next_at < n_blocks, bexp[jnp.minimum(next_at, n_blocks - 1)], -1).astype(I32)
    return pad_start.astype(I32), bexp, bidx, new, live, nxt


def _layer(lay, xp, xs, pos, cond8, h0f, h0b, lp, alpha, tm_proj=512, tm_route=256, tm_comb=256):
    (w_ada, b_ada, w_in, conv_w, conv_b, conv_ln_g, conv_ln_b, ssm_conv_w, ssm_conv_b, dt_bias, a_log,
     d_skip, ssm_norm_g, w_out, ln1_g, ln1_b, w_router, router_bias, w_exp_gu, w_exp_down, w_sh_gu,
     w_sh_down, ln2_g, ln2_b) = lp
    T = lay.n_tokens
    n_main = 2 * D_CONV + D_SSM + D_XBC
    w_main = w_in[:, :n_main].astype(BF16)
    w_dt = jnp.pad(w_in[:, n_main:], ((0, 0), (0, 128 - 2 * N_HEADS))).astype(BF16)

    mod = _ada(cond8, w_ada, b_ada)
    glu, z, xbc, dt_raw = _inproj(lay, xp, xs, pos, mod, w_main, w_dt, tm_proj)
    conv_out, xbc_c = _conv(lay, glu, xbc, conv_w, conv_b, conv_ln_g, conv_ln_b, ssm_conv_w, ssm_conv_b)
    y_ssm, hf, hb = _ssd(lay, xbc_c, z, dt_raw, h0f, h0b, dt_bias, a_log, d_skip, ssm_norm_g)
    x1, h2, h2p = _outproj(lay, xp, xs, pos, mod, conv_out, y_ssm, w_out.astype(BF16), ln1_g, ln1_b, alpha, tm_proj)

    idx, wts, posn, cnt = _route(h2, w_router.T, router_bias, tm_route)
    n_blocks = -(-T * TOP_K // EXPERT_BLOCK) + N_EXPERTS
    pad_start, bexp, bidx, new, live, nxt = _moe_plan(cnt[:, 0], n_blocks)
    dest2 = _dest(idx, posn, pad_start, 512)
    n_rows = n_blocks * EXPERT_BLOCK
    scatter_idx = jnp.concatenate([dest2 + c * n_rows for c in range(ROW_PARTS)], axis=1)
    x_sorted = _sc_scatter(h2p.reshape(ROW_PARTS * T, PART_WORDS), scatter_idx.reshape(-1),
                           ROW_PARTS * n_rows, TOP_K).reshape(ROW_PARTS, n_rows, PART_WORDS)
    y_sorted = _expert(x_sorted, w_exp_gu, w_exp_down, bexp, bidx, new, live, nxt, n_blocks)
    dest = dest2.reshape(-1)
    idx_parts = jnp.concatenate([dest + c * n_rows for c in range(ROW_PARTS)])
    y_tok = _sc_gather(y_sorted.reshape(ROW_PARTS * n_rows, PART_WORDS), idx_parts)
    y_tok = y_tok.reshape(ROW_PARTS, TOP_K, T, PART_WORDS)
    out_p, out_s = _combine(lay, h2, x1, wts.T, mod, w_sh_gu.astype(BF16), w_sh_down.astype(BF16),
                            ln2_g, ln2_b, y_tok, alpha, tm_comb)
    return out_p, out_s, hf, hb


def kernel(x_prompt, x_sample, state_ssd_fwd, state_ssd_bwd, c, c_ctx, w_ada, b_ada, w_in, conv_w, conv_b, conv_ln_g, conv_ln_b, ssm_conv_w, ssm_conv_b, dt_bias, a_log, d_skip, ssm_norm_g, w_out, ln1_g, ln1_b, w_router, router_bias, w_exp_gu, w_exp_down, w_sh_gu, w_sh_down, ln2_g, ln2_b):
    depth = w_ada.shape[0]
    assert depth == 1, "the prompt and latent passes are fused per layer; one layer is supported"
    bp, lp_, _ = x_prompt.shape
    bd, ld, _ = x_sample.shape
    lay = _Layout(bp, lp_, bd, ld)
    alpha = (2.0 * depth) ** 0.25
    stacked = (w_ada, b_ada, w_in, conv_w, conv_b, conv_ln_g, conv_ln_b, ssm_conv_w, ssm_conv_b,
               dt_bias, a_log, d_skip, ssm_norm_g, w_out, ln1_g, ln1_b, w_router, router_bias,
               w_exp_gu, w_exp_down, w_sh_gu, w_sh_down, ln2_g, ln2_b)
    lp = [w[0] for w in stacked]
    cond8 = jnp.concatenate([c_ctx[None, :], c, jnp.zeros((8 - 1 - bd, D_MODEL), F32)], axis=0)
    pos = _grid_pos_embed(ld)
    sshape = (bd, N_HEADS, HEADDIM, D_STATE)
    out_p, out_s, hf, hb = _layer(lay, x_prompt.reshape(bp * lp_, D_MODEL), x_sample.reshape(bd * ld, D_MODEL),
                                  pos, cond8, state_ssd_fwd[:, 0].reshape(sshape),
                                  state_ssd_bwd[:, 0].reshape(sshape), lp, alpha)
    return (out_p.reshape(bp, lp_, D_MODEL), out_s.reshape(bd, ld, D_MODEL),
            hf[:, None], hb[:, None])
```

```python
import functools
import math

import numpy as np
import jax
import jax.numpy as jnp
from jax import lax
from jax.experimental import pallas as pl
from jax.experimental.pallas import tpu as pltpu
from jax.experimental.pallas import tpu_sc as plsc

F32 = jnp.float32
BF16 = jnp.bfloat16
I32 = jnp.int32
HI = lax.Precision.HIGHEST

D_MODEL = 1024
GRID_W = 64
D_CONV = 1024
CONV_K = 31
N_HEADS = 16
HEADDIM = 64
D_SSM = N_HEADS * HEADDIM
N_GROUPS = 4
HEADS_PER_GROUP = N_HEADS // N_GROUPS
D_STATE = 128
SSM_CONV_K = 4
CHUNK = 128
D_XBC = D_SSM + 2 * N_GROUPS * D_STATE
N_EXPERTS = 256
TOP_K = 8
N_EXPERT_GROUPS = 8
EXPERTS_PER_GROUP = N_EXPERTS // N_EXPERT_GROUPS
TOPK_GROUPS = 4
D_EXPERT = 256
D_SHARED = 256
ROUTED_SCALE = 2.5
LN_EPS = 1e-5

CONV_TILE = 256
HALO = 16
EXPERT_BLOCK = 256
ROW_PARTS = 2
PART_WORDS = D_MODEL // 2 // ROW_PARTS
VMEM_LIMIT = 56 * 1024 * 1024


def _cparams(sem, vmem=VMEM_LIMIT):
    return pltpu.CompilerParams(dimension_semantics=sem, vmem_limit_bytes=vmem)


def _silu(x):
    return x * jax.nn.sigmoid(x)


def _ln_rows(x):
    mu = jnp.mean(x, axis=-1, keepdims=True)
    xc = x - mu
    var = jnp.mean(xc * xc, axis=-1, keepdims=True)
    return xc * lax.rsqrt(var + LN_EPS)


def _iota(shape, dim):
    return lax.broadcasted_iota(I32, shape, dim)


def _expand_matrix(n_in, width):
    return (_iota((n_in, n_in * width), 0) == _iota((n_in, n_in * width), 1) // width).astype(F32)


def _dot_hi(a, b):
    return jnp.dot(a, b, precision=HI, preferred_element_type=F32)


def _split3(x):
    hi = x.astype(BF16)
    r1 = x - hi.astype(F32)
    mid = r1.astype(BF16)
    lo = (r1 - mid.astype(F32)).astype(BF16)
    return jnp.concatenate([hi, mid, lo], axis=1)


def _expand3(n, width):
    rows = np.arange(3 * n)[:, None] % n
    cols = np.arange(n * width)[None, :] // width
    return jnp.asarray(rows == cols, dtype=BF16)


def _expand_exact(x, e3):
    return jnp.dot(_split3(x), e3, preferred_element_type=F32)


def _ada_kernel(c_ref, w_ref, b_ref, o_ref):
    o_ref[...] = _dot_hi(_silu(c_ref[...]), w_ref[...]) + b_ref[...]


def _ada(cond8, w_ada, b_ada):
    n = w_ada.shape[1]
    tn = 1024
    return pl.pallas_call(
        _ada_kernel,
        grid=(n // tn,),
        in_specs=[pl.BlockSpec((8, D_MODEL), lambda j: (0, 0)),
                  pl.BlockSpec((D_MODEL, tn), lambda j: (0, j)),
                  pl.BlockSpec((1, tn), lambda j: (0, j))],
        out_specs=pl.BlockSpec((8, tn), lambda j: (0, j)),
        out_shape=jax.ShapeDtypeStruct((8, n), F32),
        compiler_params=_cparams(("arbitrary",)),
        name="ada",
    )(cond8, w_ada, b_ada.reshape(1, n))


def _inproj_kernel(row_ref, posb_ref, flag_ref, xp_ref, xs_ref, pos_ref, mod_ref, wm_ref, wdt_ref,
                   glu_ref, z_ref, xbc_ref, dt_ref):
    i = pl.program_id(0)
    x = jnp.where(flag_ref[i] == 1, xs_ref[...] + pos_ref[...], xp_ref[...])
    r = row_ref[i]
    sh1 = mod_ref[pl.ds(r, 1), 0:D_MODEL]
    sc1 = mod_ref[pl.ds(r, 1), D_MODEL:2 * D_MODEL]
    h = (_ln_rows(x) * (1.0 + sc1) + sh1).astype(BF16)
    glu_ref[...] = jnp.dot(h, wm_ref[:, 0:2 * D_CONV], preferred_element_type=F32).astype(BF16)
    z_ref[...] = jnp.dot(h, wm_ref[:, 2 * D_CONV:2 * D_CONV + D_SSM], preferred_element_type=F32).astype(BF16)
    xbc_ref[...] = jnp.dot(h, wm_ref[:, 2 * D_CONV + D_SSM:], preferred_element_type=F32).astype(BF16)
    dt_ref[...] = jnp.dot(h, wdt_ref[...], preferred_element_type=F32)


def _inproj(lay, xp, xs, pos, mod, w_main, w_dt, tm):
    T = lay.n_tokens
    row, posb, flag = lay.token_tile_tables(tm)
    npt = lay.n_prompt_tokens // tm
    n_main = w_main.shape[1]
    gs = pltpu.PrefetchScalarGridSpec(
        num_scalar_prefetch=3,
        grid=(T // tm,),
        in_specs=[pl.BlockSpec((tm, D_MODEL), lambda i, r, p, f: (jnp.minimum(i, npt - 1), 0)),
                  pl.BlockSpec((tm, D_MODEL), lambda i, r, p, f: (jnp.maximum(i - npt, 0), 0)),
                  pl.BlockSpec((tm, D_MODEL), lambda i, r, p, f: (p[i], 0)),
                  pl.BlockSpec((8, 6 * D_MODEL), lambda i, r, p, f: (0, 0)),
                  pl.BlockSpec((D_MODEL, n_main), lambda i, r, p, f: (0, 0)),
                  pl.BlockSpec((D_MODEL, 128), lambda i, r, p, f: (0, 0))],
        out_specs=[pl.BlockSpec((tm, 2 * D_CONV), lambda i, r, p, f: (i, 0)),
                   pl.BlockSpec((tm, D_SSM), lambda i, r, p, f: (i, 0)),
                   pl.BlockSpec((tm, D_XBC), lambda i, r, p, f: (i, 0)),
                   pl.BlockSpec((tm, 128), lambda i, r, p, f: (i, 0))])
    return pl.pallas_call(
        _inproj_kernel, grid_spec=gs,
        out_shape=(jax.ShapeDtypeStruct((T, 2 * D_CONV), BF16),
                   jax.ShapeDtypeStruct((T, D_SSM), BF16),
                   jax.ShapeDtypeStruct((T, D_XBC), BF16),
                   jax.ShapeDtypeStruct((T, 128), F32)),
        compiler_params=_cparams(("arbitrary",)),
        name="inproj",
    )(row, posb, flag, xp, xs, pos, mod, w_main, w_dt)


_N_SHIFT = 8
_SHIFT_ROWS = CONV_TILE + 2 * HALO - _N_SHIFT
_ROW_BLOCK = 64
_FILL_ROWS = 32
_SSM_ROWS, _SSM_LANES = 64, 256


def _conv_kernel(lok_ref, rok_ref, glu_ref, glul_ref, glur_ref, xbc_ref, xbcl_ref, xbcr_ref,
                 cw_ref, cb_ref, lng_ref, lnb_ref, sw_ref, sb_ref, co_ref, xo_ref,
                 ext_ref, sh_ref, acc_ref, ext2_ref):
    i = pl.program_id(0)
    lok = lok_ref[i] == 1
    rok = rok_ref[i] == 1

    def glu(v):
        v = v.astype(F32)
        return v[:, 0:D_CONV] * jax.nn.sigmoid(v[:, D_CONV:])

    def fill_ext(rb, carry):
        r0 = pl.multiple_of(rb * _FILL_ROWS, _FILL_ROWS)
        dst = pl.ds(pl.multiple_of(HALO + r0, HALO), _FILL_ROWS)
        ext_ref[dst, :] = glu(glu_ref[pl.ds(r0, _FILL_ROWS), :])
        ext2_ref[dst, :] = xbc_ref[pl.ds(r0, _FILL_ROWS), :].astype(F32)
        return carry

    ext_ref[0:HALO, :] = jnp.where(lok, glu(glul_ref[...]), 0.0)
    ext_ref[HALO + CONV_TILE:, :] = jnp.where(rok, glu(glur_ref[...]), 0.0)
    ext2_ref[0:HALO, :] = jnp.where(lok, xbcl_ref[...].astype(F32), 0.0)
    ext2_ref[HALO + CONV_TILE:, :] = jnp.where(rok, xbcr_ref[...].astype(F32), 0.0)
    lax.fori_loop(0, CONV_TILE // _FILL_ROWS, fill_ext, 0)
    for r in range(_N_SHIFT):
        sh_ref[r] = ext_ref[r:r + _SHIFT_ROWS, :]

    first = HALO - (CONV_K - 1) // 2

    for j in range(D_CONV // 128):
        lanes = slice(j * 128, (j + 1) * 128)
        taps = [jnp.broadcast_to(cw_ref[k:k + 1, lanes], (8, 128)) for k in range(CONV_K)]
        bias = jnp.broadcast_to(cb_ref[:, lanes], (8, 128))

        def row_block(rb, carry, lanes=lanes, taps=taps, bias=bias):
            base = pl.multiple_of(rb * _ROW_BLOCK, _ROW_BLOCK)
            for sub in range(_ROW_BLOCK // 8):
                acc = bias
                for k in range(CONV_K):
                    o = first + k
                    row0 = base + (o // _N_SHIFT) * _N_SHIFT + sub * 8
                    acc = acc + sh_ref[o % _N_SHIFT, pl.ds(row0, 8), lanes] * taps[k]
                acc_ref[pl.ds(base + sub * 8, 8), lanes] = acc
            return carry

        lax.fori_loop(0, CONV_TILE // _ROW_BLOCK, row_block, 0)
    u = _ln_rows(acc_ref[...]) * lng_ref[...] + lnb_ref[...]
    co_ref[...] = _silu(u).astype(BF16)

    first2 = HALO - (SSM_CONV_K - 1) // 2
    for rb in range(CONV_TILE // _SSM_ROWS):
        for lc in range(D_XBC // _SSM_LANES):
            lanes = slice(lc * _SSM_LANES, (lc + 1) * _SSM_LANES)
            y = jnp.zeros((_SSM_ROWS, _SSM_LANES), F32) + sb_ref[:, lanes]
            for k in range(SSM_CONV_K):
                r0 = first2 + k + rb * _SSM_ROWS
                y = y + ext2_ref[r0:r0 + _SSM_ROWS, lanes] * sw_ref[k:k + 1, lanes]
            xo_ref[rb * _SSM_ROWS:(rb + 1) * _SSM_ROWS, lanes] = _silu(y).astype(BF16)


def _conv(lay, glu, xbc, conv_w, conv_b, ln_g, ln_b, ssm_w, ssm_b):
    T = lay.n_tokens
    lok, rok = lay.conv_tile_tables()
    n_tiles = T // CONV_TILE
    hb = CONV_TILE // HALO
    n_hb = T // HALO

    def cur(i, l, r):
        return (i, 0)

    def left(i, l, r):
        return (jnp.maximum(i * hb - 1, 0), 0)

    def right(i, l, r):
        return (jnp.minimum((i + 1) * hb, n_hb - 1), 0)

    def const(i, l, r):
        return (0, 0)

    gs = pltpu.PrefetchScalarGridSpec(
        num_scalar_prefetch=2,
        grid=(n_tiles,),
        in_specs=[pl.BlockSpec((CONV_TILE, 2 * D_CONV), cur),
                  pl.BlockSpec((HALO, 2 * D_CONV), left),
                  pl.BlockSpec((HALO, 2 * D_CONV), right),
                  pl.BlockSpec((CONV_TILE, D_XBC), cur),
                  pl.BlockSpec((HALO, D_XBC), left),
                  pl.BlockSpec((HALO, D_XBC), right),
                  pl.BlockSpec((CONV_K, D_CONV), const),
                  pl.BlockSpec((1, D_CONV), const),
                  pl.BlockSpec((1, D_CONV), const),
                  pl.BlockSpec((1, D_CONV), const),
                  pl.BlockSpec((SSM_CONV_K, D_XBC), const),
                  pl.BlockSpec((1, D_XBC), const)],
        out_specs=[pl.BlockSpec((CONV_TILE, D_CONV), cur),
                   pl.BlockSpec((CONV_TILE, D_XBC), cur)],
        scratch_shapes=[pltpu.VMEM((CONV_TILE + 2 * HALO, D_CONV), F32),
                        pltpu.VMEM((_N_SHIFT, _SHIFT_ROWS, D_CONV), F32),
                        pltpu.VMEM((CONV_TILE, D_CONV), F32),
                        pltpu.VMEM((CONV_TILE + 2 * HALO, D_XBC), F32)])
    return pl.pallas_call(
        _conv_kernel, grid_spec=gs,
        out_shape=(jax.ShapeDtypeStruct((T, D_CONV), BF16),
                   jax.ShapeDtypeStruct((T, D_XBC), BF16)),
        compiler_params=_cparams(("arbitrary",)),
        name="conv",
    )(lok, rok, glu, glu, glu, xbc, xbc, xbc, conv_w, conv_b.reshape(1, -1), ln_g.reshape(1, -1),
      ln_b.reshape(1, -1), ssm_w, ssm_b.reshape(1, -1))


_BN = N_GROUPS * D_STATE


def _ssd_kernel(chunk_ref, yidx_ref, phase_ref, first_ref, last_ref, zero_ref, sin_ref, sout_ref, cloc_ref,
                xbc_ref, z_ref, dt_ref, h0f_ref, h0b_ref, dtb_ref, alog_ref, dsk_ref, ng_ref,
                tri_ref, edec_ref, ewb_ref, ecol_ref, ewide_ref, eye3_ref,
                y_ref, hf_out_ref, hb_out_ref,
                hf_ref, g_ref, gin_ref, ybuf_ref):
    s = pl.program_id(0)
    phase = phase_ref[s]
    first = first_ref[s] == 1
    last = last_ref[s] == 1
    zero = zero_ref[s] == 1
    cloc = cloc_ref[s]
    H, P, N = N_HEADS, HEADDIM, D_STATE

    GW = HEADS_PER_GROUP * P
    xs = xbc_ref[:, 0:D_SSM]
    dt = dt_ref[:, 0:2 * H] + dtb_ref[...]
    dt = jnp.maximum(dt, 0.0) + jnp.log1p(jnp.exp(-jnp.abs(dt)))
    a = dt * (-jnp.exp(alog_ref[...]))
    a3 = jnp.dot(tri_ref[...], _split3(a), preferred_element_type=F32)
    acs = a3[:, 0:2 * H] + a3[:, 2 * H:4 * H] + a3[:, 4 * H:6 * H]
    tot = acs[CHUNK - 8:CHUNK, :]
    dec = _expand_exact(jnp.exp(tot), edec_ref[...])[7:8, :]
    exb = acs[:, H:2 * H] - a[:, H:2 * H]

    def load_state(src_ref, dst_ref):
        for j in range(H // 2):
            pair = jnp.concatenate([src_ref[0, 2 * j], src_ref[0, 2 * j + 1]], axis=0)
            dst_ref[:, 2 * j * P:(2 * j + 2) * P] = jnp.where(zero, 0.0, pair.T)

    def store_state(src_ref, dst_ref):
        for j in range(H // 2):
            pair = src_ref[:, 2 * j * P:(2 * j + 2) * P].T
            dst_ref[0, 2 * j] = pair[0:P]
            dst_ref[0, 2 * j + 1] = pair[P:2 * P]

    @pl.when(phase == 0)
    def _backward_states():
        @pl.when(first)
        def _():
            load_state(h0b_ref, g_ref)

        wb = dt[:, H:2 * H] * jnp.exp(exb)
        xw = (xs.astype(F32) * _expand_exact(wb, ewb_ref[...])).astype(BF16)
        for g in range(N_GROUPS):
            cols = slice(g * GW, (g + 1) * GW)
            bg = xbc_ref[:, D_SSM + g * N:D_SSM + (g + 1) * N]
            gg = g_ref[:, cols]
            gin_ref[cloc, :, cols] = gg.astype(BF16)
            upd = lax.dot_general(bg, xw[:, cols], (((0,), (0,)), ((), ())), preferred_element_type=F32)
            g_ref[:, cols] = gg * dec[:, D_SSM + g * GW:D_SSM + (g + 1) * GW] + upd

        @pl.when(last)
        def _():
            store_state(g_ref, hb_out_ref)

    @pl.when(phase == 1)
    def _forward_and_outputs():
        @pl.when(first)
        def _():
            load_state(h0f_ref, hf_ref)

        acsf = acs[:, 0:H]
        dtf = dt[:, 0:H]
        dtb = dt[:, H:2 * H]
        totf = acs[CHUNK - 1:CHUNK, 0:H]
        totb = acs[CHUNK - 1:CHUNK, H:2 * H]
        col = _expand_exact(jnp.concatenate([acsf, exb], axis=1), ecol_ref[...])
        q3 = _split3(jnp.concatenate([acsf, exb, dtf, dtb], axis=1))
        qt = lax.dot_general(eye3_ref[...], q3, (((1,), (1,)), ((), ())),
                             preferred_element_type=F32)
        wide = jnp.concatenate([dtf * jnp.exp(totf - acsf), jnp.exp(acsf), jnp.exp(totb - exb)], axis=1)
        wide = _expand_exact(wide, ewide_ref[...])
        xsf = xs.astype(F32)
        xw = (xsf * wide[:, 0:D_SSM]).astype(BF16)
        lower = _iota((CHUNK, CHUNK), 1) <= _iota((CHUNK, CHUNK), 0)
        upper = _iota((CHUNK, CHUNK), 1) >= _iota((CHUNK, CHUNK), 0)
        for g in range(N_GROUPS):
            cols = slice(g * GW, (g + 1) * GW)
            bg = xbc_ref[:, D_SSM + g * N:D_SSM + (g + 1) * N]
            cg = xbc_ref[:, D_SSM + _BN + g * N:D_SSM + _BN + (g + 1) * N]
            cb = lax.dot_general(cg, bg, (((1,), (1,)), ((), ())), preferred_element_type=F32)
            hfg = hf_ref[:, cols]
            yf = jnp.dot(cg, hfg.astype(BF16), preferred_element_type=F32)
            yb = jnp.dot(cg, gin_ref[cloc, :, cols], preferred_element_type=F32)
            ybuf_ref[:, cols] = yf * wide[:, D_SSM + g * GW:D_SSM + (g + 1) * GW] \
                + yb * wide[:, 2 * D_SSM + g * GW:2 * D_SSM + (g + 1) * GW]
            upd = lax.dot_general(bg, xw[:, cols], (((0,), (0,)), ((), ())), preferred_element_type=F32)
            hf_ref[:, cols] = hfg * dec[:, cols] + upd
            for r in range(HEADS_PER_GROUP):
                h = g * HEADS_PER_GROUP + r
                colf = col[:, h * N:(h + 1) * N]
                colb = col[:, (H + h) * N:(H + h + 1) * N]
                mf = jnp.where(lower, jnp.exp(colf - qt[h:h + 1, :]), 0.0) * qt[2 * H + h:2 * H + h + 1, :]
                mb = jnp.where(upper, jnp.exp(qt[H + h:H + h + 1, :] - colb), 0.0) * qt[3 * H + h:3 * H + h + 1, :]
                m = (cb * (mf + mb)).astype(BF16)
                hs = slice(h * P, (h + 1) * P)
                ybuf_ref[:, hs] += jnp.dot(m, xs[:, hs], preferred_element_type=F32)

        yt = (ybuf_ref[...] + dsk_ref[...] * xsf) * _silu(z_ref[...].astype(F32))
        gw = D_SSM // N_GROUPS
        for g in range(N_GROUPS):
            seg = yt[:, g * gw:(g + 1) * gw]
            ms = jnp.mean(seg * seg, axis=-1, keepdims=True)
            y_ref[:, g * gw:(g + 1) * gw] = (seg * lax.rsqrt(ms + LN_EPS) * ng_ref[:, g * gw:(g + 1) * gw]).astype(BF16)

        @pl.when(last)
        def _():
            store_state(hf_ref, hf_out_ref)


def _ssd(lay, xbc_c, z, dt_raw, h0f, h0b, dt_bias, a_log, d_skip, norm_g):
    T = lay.n_tokens
    tabs = lay.ssd_step_tables()
    n_steps = tabs[0].shape[0]
    nsp = len(tabs)

    def by_chunk(s, *t):
        return (t[0][s], 0)

    def by_y(s, *t):
        return (t[1][s], 0)

    def by_sin(s, *t):
        return (t[6][s], 0, 0, 0)

    def by_sout(s, *t):
        return (t[7][s], 0, 0, 0)

    def const(s, *t):
        return (0, 0)

    H = N_HEADS
    tri = jnp.asarray(np.tril(np.ones((CHUNK, CHUNK))), dtype=BF16)
    eye3 = jnp.asarray(np.arange(4 * H)[:, None] == np.arange(12 * H)[None, :] % (4 * H), dtype=BF16)
    consts = [tri, _expand3(2 * H, HEADDIM), _expand3(H, HEADDIM), _expand3(2 * H, D_STATE),
              _expand3(3 * H, HEADDIM), eye3]
    sshape = (1, N_HEADS, HEADDIM, D_STATE)
    gs = pltpu.PrefetchScalarGridSpec(
        num_scalar_prefetch=nsp,
        grid=(n_steps,),
        in_specs=[pl.BlockSpec((CHUNK, D_XBC), by_chunk),
                  pl.BlockSpec((CHUNK, D_SSM), by_chunk),
                  pl.BlockSpec((CHUNK, 128), by_chunk),
                  pl.BlockSpec(sshape, by_sin),
                  pl.BlockSpec(sshape, by_sin),
                  pl.BlockSpec((1, 2 * N_HEADS), const),
                  pl.BlockSpec((1, 2 * N_HEADS), const),
                  pl.BlockSpec((1, D_SSM), const),
                  pl.BlockSpec((1, D_SSM), const)] + [pl.BlockSpec(c.shape, const) for c in consts],
        out_specs=[pl.BlockSpec((CHUNK, D_SSM), by_y),
                   pl.BlockSpec(sshape, by_sout),
                   pl.BlockSpec(sshape, by_sout)],
        scratch_shapes=[pltpu.VMEM((D_STATE, D_SSM), F32),
                        pltpu.VMEM((D_STATE, D_SSM), F32),
                        pltpu.VMEM((lay.max_chunks, D_STATE, D_SSM), BF16),
                        pltpu.VMEM((CHUNK, D_SSM), F32)])
    n_out = lay.n_prompt_seqs
    return pl.pallas_call(
        _ssd_kernel, grid_spec=gs,
        out_shape=(jax.ShapeDtypeStruct((T, D_SSM), BF16),
                   jax.ShapeDtypeStruct((n_out,) + sshape[1:], F32),
                   jax.ShapeDtypeStruct((n_out,) + sshape[1:], F32)),
        compiler_params=_cparams(("arbitrary",)),
        name="ssd",
    )(*tabs, xbc_c, z, dt_raw, h0f, h0b, dt_bias.reshape(1, -1), a_log.reshape(1, -1),
      jnp.repeat(d_skip, HEADDIM).reshape(1, -1), norm_g.reshape(1, -1), *consts)


def _outproj_kernel(row_ref, posb_ref, flag_ref, xp_ref, xs_ref, pos_ref, mod_ref, co_ref, ys_ref, wo_ref,
                    g_ref, b_ref, x1_ref, h2_ref, h2p_ref, *, alpha):
    i = pl.program_id(0)
    x = jnp.where(flag_ref[i] == 1, xs_ref[...] + pos_ref[...], xp_ref[...])
    r = row_ref[i]
    g1 = mod_ref[pl.ds(r, 1), 2 * D_MODEL:3 * D_MODEL]
    sh2 = mod_ref[pl.ds(r, 1), 3 * D_MODEL:4 * D_MODEL]
    sc2 = mod_ref[pl.ds(r, 1), 4 * D_MODEL:5 * D_MODEL]
    mix = jnp.dot(co_ref[...], wo_ref[0:D_CONV, :], preferred_element_type=F32) \
        + jnp.dot(ys_ref[...], wo_ref[D_CONV:, :], preferred_element_type=F32)
    x1 = _ln_rows(alpha * x + g1 * mix) * g_ref[...] + b_ref[...]
    x1_ref[...] = x1
    h2 = _ln_rows(x1) * (1.0 + sc2) + sh2
    h2_ref[...] = h2
    packed = _pack_halves(h2)
    for c in range(ROW_PARTS):
        h2p_ref[c] = packed[:, c * PART_WORDS:(c + 1) * PART_WORDS]


def _outproj(lay, xp, xs, pos, mod, conv_out, y_ssm, w_out, ln_g, ln_b, alpha, tm):
    T = lay.n_tokens
    row, posb, flag = lay.token_tile_tables(tm)
    npt = lay.n_prompt_tokens // tm

    def const(i, r, p, f):
        return (0, 0)

    def cur(i, r, p, f):
        return (i, 0)

    gs = pltpu.PrefetchScalarGridSpec(
        num_scalar_prefetch=3,
        grid=(T // tm,),
        in_specs=[pl.BlockSpec((tm, D_MODEL), lambda i, r, p, f: (jnp.minimum(i, npt - 1), 0)),
                  pl.BlockSpec((tm, D_MODEL), lambda i, r, p, f: (jnp.maximum(i - npt, 0), 0)),
                  pl.BlockSpec((tm, D_MODEL), lambda i, r, p, f: (p[i], 0)),
                  pl.BlockSpec((8, 6 * D_MODEL), const),
                  pl.BlockSpec((tm, D_CONV), cur),
                  pl.BlockSpec((tm, D_SSM), cur),
                  pl.BlockSpec((D_CONV + D_SSM, D_MODEL), const),
                  pl.BlockSpec((1, D_MODEL), const),
                  pl.BlockSpec((1, D_MODEL), const)],
        out_specs=[pl.BlockSpec((tm, D_MODEL), cur),
                   pl.BlockSpec((tm, D_MODEL), cur),
                   pl.BlockSpec((ROW_PARTS, tm, PART_WORDS), lambda i, r, p, f: (0, i, 0))])
    return pl.pallas_call(
        functools.partial(_outproj_kernel, alpha=alpha), grid_spec=gs,
        out_shape=(jax.ShapeDtypeStruct((T, D_MODEL), F32),
                   jax.ShapeDtypeStruct((T, D_MODEL), F32),
                   jax.ShapeDtypeStruct((ROW_PARTS, T, PART_WORDS), jnp.uint32)),
        compiler_params=_cparams(("arbitrary",)),
        name="outproj",
    )(row, posb, flag, xp, xs, pos, mod, conv_out, y_ssm, w_out, ln_g.reshape(1, -1), ln_b.reshape(1, -1))


def _route_kernel(h2_ref, wrt_ref, bias_ref, idx_ref, wts_ref, pos_ref, cnt_ref, carry_ref, *, tm):
    i = pl.program_id(0)

    @pl.when(i == 0)
    def _():
        carry_ref[...] = jnp.zeros_like(carry_ref)

    E, NG, EG = N_EXPERTS, N_EXPERT_GROUPS, EXPERTS_PER_GROUP
    neg = -jnp.inf
    logits = lax.dot_general(wrt_ref[...], h2_ref[...], (((1,), (1,)), ((), ())), precision=HI,
                             preferred_element_type=F32)
    s = jax.nn.sigmoid(logits)
    sel = s + bias_ref[...]
    sel3 = sel.reshape(NG, EG, tm)
    io3 = _iota((NG, EG, tm), 1)
    m1 = jnp.max(sel3, axis=1, keepdims=True)
    f1 = jnp.min(jnp.where(sel3 == m1, io3, EG), axis=1, keepdims=True)
    m2 = jnp.max(jnp.where(io3 == f1, neg, sel3), axis=1, keepdims=True)
    gscore = (m1 + m2).reshape(NG, tm)
    gio = _iota((NG, tm), 0)
    beaten = jnp.zeros((NG, tm), I32)
    for g in range(NG):
        row = gscore[g:g + 1, :]
        beats = jnp.where(row > gscore, 1, jnp.where(row == gscore, jnp.where(g < gio, 1, 0), 0))
        beaten = beaten + beats
    keep = (beaten < TOPK_GROUPS).astype(F32).reshape(NG, 1, tm)
    selm = jnp.where(keep > 0.5, sel3, neg).reshape(E, tm)
    eio = _iota((E, tm), 0)
    chosen = jnp.zeros((E, tm), F32)
    idxs, ws = [], []
    for k in range(TOP_K):
        m = jnp.max(selm, axis=0, keepdims=True)
        am = jnp.minimum(jnp.min(jnp.where(selm == m, eio, E), axis=0, keepdims=True), E - 1)
        hit = eio == am
        ws.append(jnp.sum(jnp.where(hit, s, 0.0), axis=0, keepdims=True))
        idxs.append(am)
        selm = jnp.where(hit, neg, selm)
        chosen = jnp.where(hit, 1.0, chosen)
    wsum = ws[0]
    for k in range(1, TOP_K):
        wsum = wsum + ws[k]
    before = (_iota((tm, tm), 0) < _iota((tm, tm), 1)).astype(BF16)
    prior = jnp.dot(chosen.astype(BF16), before, preferred_element_type=F32)
    carry = carry_ref[...]
    prior = prior + jnp.concatenate([carry] * (tm // 128), axis=1)
    for k in range(TOP_K):
        idx_ref[k:k + 1, :] = idxs[k]
        wts_ref[k:k + 1, :] = ws[k] / wsum * ROUTED_SCALE
        pos_ref[k:k + 1, :] = jnp.sum(jnp.where(eio == idxs[k], prior, 0.0), axis=0, keepdims=True).astype(I32)
    total = jnp.dot(chosen.astype(BF16), jnp.ones((tm, 128), BF16), preferred_element_type=F32)
    carry = carry + total
    carry_ref[...] = carry
    cnt_ref[...] = carry.astype(I32)


def _route(h2, w_router_t, router_bias, tm):
    T = h2.shape[0]
    bias_b = jnp.broadcast_to(router_bias.astype(F32)[:, None], (N_EXPERTS, tm))
    return pl.pallas_call(
        functools.partial(_route_kernel, tm=tm),
        grid=(T // tm,),
        in_specs=[pl.BlockSpec((tm, D_MODEL), lambda i: (i, 0)),
                  pl.BlockSpec((N_EXPERTS, D_MODEL), lambda i: (0, 0)),
                  pl.BlockSpec((N_EXPERTS, tm), lambda i: (0, 0))],
        out_specs=[pl.BlockSpec((TOP_K, tm), lambda i: (0, i)),
                   pl.BlockSpec((TOP_K, tm), lambda i: (0, i)),
                   pl.BlockSpec((TOP_K, tm), lambda i: (0, i)),
                   pl.BlockSpec((N_EXPERTS, 128), lambda i: (0, 0))],
        out_shape=(jax.ShapeDtypeStruct((TOP_K, T), I32),
                   jax.ShapeDtypeStruct((TOP_K, T), F32),
                   jax.ShapeDtypeStruct((TOP_K, T), I32),
                   jax.ShapeDtypeStruct((N_EXPERTS, 128), I32)),
        scratch_shapes=[pltpu.VMEM((N_EXPERTS, 128), F32)],
        compiler_params=_cparams(("arbitrary",)),
        name="route",
    )(h2, w_router_t, bias_b)


def _dest_kernel(idx_ref, pos_ref, start_ref, dest_ref):
    tm = idx_ref.shape[1]
    eio = _iota((N_EXPERTS, tm), 0)
    start = start_ref[...]
    for k in range(TOP_K):
        base = jnp.sum(jnp.where(eio == idx_ref[k:k + 1, :], start, 0.0), axis=0, keepdims=True)
        dest_ref[k:k + 1, :] = base.astype(I32) + pos_ref[k:k + 1, :]


def _dest(idx, pos, pad_start, tm):
    T = idx.shape[1]
    start_b = jnp.broadcast_to(pad_start.astype(F32)[:, None], (N_EXPERTS, tm))
    return pl.pallas_call(
        _dest_kernel,
        grid=(T // tm,),
        in_specs=[pl.BlockSpec((TOP_K, tm), lambda i: (0, i)),
                  pl.BlockSpec((TOP_K, tm), lambda i: (0, i)),
                  pl.BlockSpec((N_EXPERTS, tm), lambda i: (0, 0))],
        out_specs=pl.BlockSpec((TOP_K, tm), lambda i: (0, i)),
        out_shape=jax.ShapeDtypeStruct((TOP_K, T), I32),
        compiler_params=_cparams(("arbitrary",)),
        name="dest",
    )(idx, pos, start_b)


_WEIGHT_FETCH_CHUNKS = 4


def _expert_kernel(bexp_ref, bidx_ref, new_ref, live_ref, next_ref, x_ref, wgu_hbm, wd_hbm, y_ref,
                   wgu_bf, wd_bf, wgu_stage, wd_stage, sems):
    i = pl.program_id(0)

    def fetch(e):
        cps = []
        for c in range(_WEIGHT_FETCH_CHUNKS):
            rg = pl.ds(c * (D_MODEL // _WEIGHT_FETCH_CHUNKS), D_MODEL // _WEIGHT_FETCH_CHUNKS)
            rd = pl.ds(c * (D_EXPERT // _WEIGHT_FETCH_CHUNKS), D_EXPERT // _WEIGHT_FETCH_CHUNKS)
            cps.append(pltpu.make_async_copy(wgu_hbm.at[e, rg], wgu_stage.at[rg], sems.at[0]))
            cps.append(pltpu.make_async_copy(wd_hbm.at[e, rd], wd_stage.at[rd], sems.at[1]))
        return cps

    @pl.when(i == 0)
    def _():
        for cp in fetch(bexp_ref[0]):
            cp.start()

    @pl.when(new_ref[i] == 1)
    def _():
        for cp in fetch(bexp_ref[i]):
            cp.wait()
        wgu_bf[...] = wgu_stage[...].astype(BF16)
        wd_bf[...] = wd_stage[...].astype(BF16)

        @pl.when(next_ref[i] >= 0)
        def _():
            for cp in fetch(next_ref[i]):
                cp.start()

    @pl.when(live_ref[i] == 1)
    def _():
        parts = [_unpack_halves(x_ref[c]) for c in range(ROW_PARTS)]
        chunks = [p[0] for p in parts] + [p[1] for p in parts]
        gu = jnp.zeros((EXPERT_BLOCK, 2 * D_EXPERT), F32)
        for j, xc in enumerate(chunks):
            gu = gu + jnp.dot(xc.astype(BF16), wgu_bf[j * PART_WORDS:(j + 1) * PART_WORDS, :],
                              preferred_element_type=F32)
        act = (_silu(gu[:, 0:D_EXPERT]) * gu[:, D_EXPERT:]).astype(BF16)
        packed = _pack_halves(jnp.dot(act, wd_bf[...], preferred_element_type=F32))
        for c in range(ROW_PARTS):
            y_ref[c] = packed[:, c * PART_WORDS:(c + 1) * PART_WORDS]


def _pack_halves(x):
    n = x.shape[1] // 2
    hi = lax.bitcast_convert_type(x[:, :n].astype(BF16).astype(F32), jnp.uint32)
    lo = lax.bitcast_convert_type(x[:, n:].astype(BF16).astype(F32), jnp.uint32)
    return hi | (lo >> 16)


def _unpack_halves(p):
    hi = lax.bitcast_convert_type(p & jnp.uint32(0xFFFF0000), F32)
    lo = lax.bitcast_convert_type(p << 16, F32)
    return hi, lo


def _expert(x_sorted, w_gu, w_down, bexp, bidx, new, live, nxt, n_blocks):
    n_rows = x_sorted.shape[1]
    gs = pltpu.PrefetchScalarGridSpec(
        num_scalar_prefetch=5,
        grid=(n_blocks,),
        in_specs=[pl.BlockSpec((ROW_PARTS, EXPERT_BLOCK, PART_WORDS), lambda i, e, b, n, l, x: (0, b[i], 0)),
                  pl.BlockSpec(memory_space=pl.ANY),
                  pl.BlockSpec(memory_space=pl.ANY)],
        out_specs=pl.BlockSpec((ROW_PARTS, EXPERT_BLOCK, PART_WORDS), lambda i, e, b, n, l, x: (0, b[i], 0)),
        scratch_shapes=[pltpu.VMEM((D_MODEL, 2 * D_EXPERT), BF16),
                        pltpu.VMEM((D_EXPERT, D_MODEL), BF16),
                        pltpu.VMEM((D_MODEL, 2 * D_EXPERT), F32),
                        pltpu.VMEM((D_EXPERT, D_MODEL), F32),
                        pltpu.SemaphoreType.DMA((2,))])
    return pl.pallas_call(
        _expert_kernel, grid_spec=gs,
        out_shape=jax.ShapeDtypeStruct((ROW_PARTS, n_rows, PART_WORDS), jnp.uint32),
        compiler_params=_cparams(("arbitrary",)),
        name="expert",
    )(bexp, bidx, new, live, nxt, x_sorted, w_gu, w_down)


def _combine_kernel(row_ref, h2_ref, x1_ref, wt_ref, mod_ref, wsg_ref, wsd_ref, g_ref, b_ref, yt_ref,
                    op_ref, os_ref, *, n_prompt_tiles, alpha):
    i = pl.program_id(0)
    h2 = h2_ref[...].astype(BF16)
    su = jnp.dot(h2, wsg_ref[...], preferred_element_type=F32)
    act = (_silu(su[:, 0:D_SHARED]) * su[:, D_SHARED:]).astype(BF16)
    moe = jnp.dot(act, wsd_ref[...], preferred_element_type=F32)
    wt = wt_ref[...]
    his, los = [], []
    for c in range(ROW_PARTS):
        rh = jnp.zeros((h2.shape[0], PART_WORDS), F32)
        rl = jnp.zeros((h2.shape[0], PART_WORDS), F32)
        for k in range(TOP_K):
            hi, lo = _unpack_halves(yt_ref[c, k])
            w = wt[:, k:k + 1]
            rh = rh + hi * w
            rl = rl + lo * w
        his.append(rh)
        los.append(rl)
    moe = moe + jnp.concatenate(his + los, axis=1)
    g2 = mod_ref[pl.ds(row_ref[i], 1), 5 * D_MODEL:6 * D_MODEL]
    out = _ln_rows(alpha * x1_ref[...] + g2 * moe) * g_ref[...] + b_ref[...]

    @pl.when(i < n_prompt_tiles)
    def _():
        op_ref[...] = out

    @pl.when(i >= n_prompt_tiles)
    def _():
        os_ref[...] = out


def _combine(lay, h2, x1, wts_tok, mod, w_sh_gu, w_sh_down, ln_g, ln_b, y_tok, alpha, tm):
    T = lay.n_tokens
    row, _, _ = lay.token_tile_tables(tm)
    npt = lay.n_prompt_tokens // tm

    def cur(i, r):
        return (i, 0)

    def const(i, r):
        return (0, 0)

    gs = pltpu.PrefetchScalarGridSpec(
        num_scalar_prefetch=1,
        grid=(T // tm,),
        in_specs=[pl.BlockSpec((tm, D_MODEL), cur),
                  pl.BlockSpec((tm, D_MODEL), cur),
                  pl.BlockSpec((tm, TOP_K), cur),
                  pl.BlockSpec((8, 6 * D_MODEL), const),
                  pl.BlockSpec((D_MODEL, 2 * D_SHARED), const),
                  pl.BlockSpec((D_SHARED, D_MODEL), const),
                  pl.BlockSpec((1, D_MODEL), const),
                  pl.BlockSpec((1, D_MODEL), const),
                  pl.BlockSpec((ROW_PARTS, TOP_K, tm, PART_WORDS), lambda i, r: (0, 0, i, 0))],
        out_specs=[pl.BlockSpec((tm, D_MODEL), lambda i, r: (jnp.minimum(i, npt - 1), 0)),
                   pl.BlockSpec((tm, D_MODEL), lambda i, r: (jnp.maximum(i - npt, 0), 0))])
    return pl.pallas_call(
        functools.partial(_combine_kernel, n_prompt_tiles=npt, alpha=alpha), grid_spec=gs,
        out_shape=(jax.ShapeDtypeStruct((lay.n_prompt_tokens, D_MODEL), F32),
                   jax.ShapeDtypeStruct((T - lay.n_prompt_tokens, D_MODEL), F32)),
        compiler_params=_cparams(("arbitrary",)),
        name="combine",
    )(row, h2, x1, wts_tok, mod, w_sh_gu, w_sh_down, ln_g.reshape(1, -1), ln_b.reshape(1, -1), y_tok)


_GATHER_WINDOW = 128


def _sc_gather(table, idx):
    n, d = idx.shape[0], table.shape[1]
    mesh = plsc.VectorSubcoreMesh(core_axis_name="core", subcore_axis_name="subcore")

    @pl.kernel(out_type=jax.ShapeDtypeStruct((n, d), table.dtype), mesh=mesh)
    def gather_kernel(table_hbm, idx_hbm, out_hbm):
        def body(idx_vmem, out_vmem):
            pltpu.sync_copy(table_hbm.at[idx_vmem.at[0]], out_vmem)

        pltpu.emit_pipeline(
            body,
            grid=(n // _GATHER_WINDOW,),
            in_specs=[pl.BlockSpec((1, _GATHER_WINDOW), index_map=lambda i: (0, i))],
            out_specs=[pl.BlockSpec((_GATHER_WINDOW, d), index_map=lambda i: (i, 0))],
            core_axis_name=("core", "subcore"),
            dimension_semantics=(pltpu.PARALLEL,),
        )(idx_hbm, out_hbm)

    return gather_kernel(table, idx.reshape(1, n))


def _sc_scatter(rows, idx, n_out, repeat):
    n, d = rows.shape
    tiles = n // _GATHER_WINDOW
    mesh = plsc.VectorSubcoreMesh(core_axis_name="core", subcore_axis_name="subcore")

    @pl.kernel(out_type=jax.ShapeDtypeStruct((n_out, d), rows.dtype), mesh=mesh, scratch_types=[])
    def scatter_kernel(rows_hbm, idx_hbm, out_hbm):
        def body(rows_vmem, idx_vmem):
            pltpu.sync_copy(rows_vmem, out_hbm.at[idx_vmem.at[0]])

        pltpu.emit_pipeline(
            body,
            grid=(repeat * tiles,),
            in_specs=[pl.BlockSpec((_GATHER_WINDOW, d), index_map=lambda i: (i % tiles, 0)),
                      pl.BlockSpec((1, _GATHER_WINDOW), index_map=lambda i: (0, i))],
            out_specs=[],
            core_axis_name=("core", "subcore"),
            dimension_semantics=(pltpu.PARALLEL,),
        )(rows_hbm, idx_hbm)

    return scatter_kernel(rows, idx.reshape(1, repeat * n))


class _Layout:
    def __init__(self, n_prompt_seqs, prompt_len, n_sample_seqs, sample_len):
        self.n_prompt_seqs, self.prompt_len = n_prompt_seqs, prompt_len
        self.n_sample_seqs, self.sample_len = n_sample_seqs, sample_len
        self.n_prompt_tokens = n_prompt_seqs * prompt_len
        self.n_tokens = self.n_prompt_tokens + n_sample_seqs * sample_len
        assert prompt_len % CONV_TILE == 0 and sample_len % CONV_TILE == 0
        self.max_chunks = max(prompt_len, sample_len) // CHUNK

    def token_tile_tables(self, tm):
        assert self.n_prompt_tokens % tm == 0 and self.sample_len % tm == 0
        npt = self.n_prompt_tokens // tm
        per_seq = self.sample_len // tm
        n = self.n_tokens // tm
        row = np.zeros(n, np.int32)
        posb = np.zeros(n, np.int32)
        flag = np.zeros(n, np.int32)
        for i in range(npt, n):
            j = i - npt
            row[i] = 1 + j // per_seq
            posb[i] = j % per_seq
            flag[i] = 1
        return jnp.asarray(row), jnp.asarray(posb), jnp.asarray(flag)

    def conv_tile_tables(self):
        lok, rok = [], []
        for n_seq, length in ((self.n_prompt_seqs, self.prompt_len), (self.n_sample_seqs, self.sample_len)):
            per = length // CONV_TILE
            for _ in range(n_seq):
                for j in range(per):
                    lok.append(int(j > 0))
                    rok.append(int(j < per - 1))
        return jnp.asarray(np.array(lok, np.int32)), jnp.asarray(np.array(rok, np.int32))

    def ssd_step_tables(self):
        cols = [[] for _ in range(9)]
        seqs = []
        c0 = self.n_prompt_tokens // CHUNK
        for j in range(self.n_sample_seqs):
            nc = self.sample_len // CHUNK
            seqs.append((c0 + j * nc, nc, 0, j, 0))
        for j in range(self.n_prompt_seqs):
            nc = self.prompt_len // CHUNK
            seqs.append((j * nc, nc, 1, 0, j))
        for base, nc, zero, sin, sout in seqs:
            for phase in (0, 1):
                order = range(nc - 1, -1, -1) if phase == 0 else range(nc)
                for n, c in enumerate(order):
                    vals = (base + c, base if phase == 0 else base + c, phase, int(n == 0), int(n == nc - 1),
                            zero, sin, sout, c)
                    for col, v in zip(cols, vals):
                        col.append(v)
        return tuple(jnp.asarray(np.array(col, np.int32)) for col in cols)


def _grid_pos_embed(n_tokens):
    rows = n_tokens // GRID_W
    quarter = D_MODEL // 4
    freq = jnp.exp(-math.log(10000.0) * jnp.arange(quarter, dtype=F32) / quarter)
    r = jnp.broadcast_to(jnp.arange(rows, dtype=F32)[:, None, None] * freq, (rows, GRID_W, quarter))
    cl = jnp.broadcast_to(jnp.arange(GRID_W, dtype=F32)[None, :, None] * freq, (rows, GRID_W, quarter))
    emb = jnp.concatenate([jnp.sin(r), jnp.cos(r), jnp.sin(cl), jnp.cos(cl)], axis=-1)
    return emb.reshape(rows * GRID_W, D_MODEL)


def _moe_plan(counts, n_blocks):
    blk = EXPERT_BLOCK
    padded = (counts + blk - 1) // blk * blk
    pad_end = jnp.cumsum(padded)
    pad_start = pad_end - padded
    n_used = pad_end[-1] // blk
    b = jnp.arange(n_blocks, dtype=I32)
    live = (b < n_used).astype(I32)
    bidx = jnp.minimum(b, jnp.maximum(n_used - 1, 0)).astype(I32)
    bexp = jnp.sum((pad_end[None, :] <= (bidx * blk)[:, None]).astype(I32), axis=1)
    bexp = jnp.minimum(bexp, N_EXPERTS - 1).astype(I32)
    new = jnp.concatenate([jnp.ones((1,), I32), (bexp[1:] != bexp[:-1]).astype(I32)])
    first_at = jnp.where(new == 1, b, n_blocks)
    next_at = jnp.concatenate([lax.cummin(first_at, reverse=True)[1:], jnp.full((1,), n_blocks, I32)])
    nxt = jnp.where(next_at < n_blocks, bexp[jnp.minimum(next_at, n_blocks - 1)], -1).astype(I32)
    return pad_start.astype(I32), bexp, bidx, new, live, nxt


def _layer(lay, xp, xs, pos, cond8, h0f, h0b, lp, alpha, tm_proj=512, tm_route=256, tm_comb=256):
    (w_ada, b_ada, w_in, conv_w, conv_b, conv_ln_g, conv_ln_b, ssm_conv_w, ssm_conv_b, dt_bias, a_log,
     d_skip, ssm_norm_g, w_out, ln1_g, ln1_b, w_router, router_bias, w_exp_gu, w_exp_down, w_sh_gu,
     w_sh_down, ln2_g, ln2_b) = lp
    T = lay.n_tokens
    n_main = 2 * D_CONV + D_SSM + D_XBC
    w_main = w_in[:, :n_main].astype(BF16)
    w_dt = jnp.pad(w_in[:, n_main:], ((0, 0), (0, 128 - 2 * N_HEADS))).astype(BF16)

    mod = _ada(cond8, w_ada, b_ada)
    glu, z, xbc, dt_raw = _inproj(lay, xp, xs, pos, mod, w_main, w_dt, tm_proj)
    conv_out, xbc_c = _conv(lay, glu, xbc, conv_w, conv_b, conv_ln_g, conv_ln_b, ssm_conv_w, ssm_conv_b)
    y_ssm, hf, hb = _ssd(lay, xbc_c, z, dt_raw, h0f, h0b, dt_bias, a_log, d_skip, ssm_norm_g)
    x1, h2, h2p = _outproj(lay, xp, xs, pos, mod, conv_out, y_ssm, w_out.astype(BF16), ln1_g, ln1_b, alpha, tm_proj)

    idx, wts, posn, cnt = _route(h2, w_router.T, router_bias, tm_route)
    n_blocks = -(-T * TOP_K // EXPERT_BLOCK) + N_EXPERTS
    pad_start, bexp, bidx, new, live, nxt = _moe_plan(cnt[:, 0], n_blocks)
    dest2 = _dest(idx, posn, pad_start, 512)
    n_rows = n_blocks * EXPERT_BLOCK
    scatter_idx = jnp.concatenate([dest2 + c * n_rows for c in range(ROW_PARTS)], axis=1)
    x_sorted = _sc_scatter(h2p.reshape(ROW_PARTS * T, PART_WORDS), scatter_idx.reshape(-1),
                           ROW_PARTS * n_rows, TOP_K).reshape(ROW_PARTS, n_rows, PART_WORDS)
    y_sorted = _expert(x_sorted, w_exp_gu, w_exp_down, bexp, bidx, new, live, nxt, n_blocks)
    dest = dest2.reshape(-1)
    idx_parts = jnp.concatenate([dest + c * n_rows for c in range(ROW_PARTS)])
    y_tok = _sc_gather(y_sorted.reshape(ROW_PARTS * n_rows, PART_WORDS), idx_parts)
    y_tok = y_tok.reshape(ROW_PARTS, TOP_K, T, PART_WORDS)
    out_p, out_s = _combine(lay, h2, x1, wts.T, mod, w_sh_gu.astype(BF16), w_sh_down.astype(BF16),
                            ln2_g, ln2_b, y_tok, alpha, tm_comb)
    return out_p, out_s, hf, hb


def kernel(x_prompt, x_sample, state_ssd_fwd, state_ssd_bwd, c, c_ctx, w_ada, b_ada, w_in, conv_w, conv_b, conv_ln_g, conv_ln_b, ssm_conv_w, ssm_conv_b, dt_bias, a_log, d_skip, ssm_norm_g, w_out, ln1_g, ln1_b, w_router, router_bias, w_exp_gu, w_exp_down, w_sh_gu, w_sh_down, ln2_g, ln2_b):
    depth = w_ada.shape[0]
    assert depth == 1, "the prompt and latent passes are fused per layer; one layer is supported"
    bp, lp_, _ = x_prompt.shape
    bd, ld, _ = x_sample.shape
    lay = _Layout(bp, lp_, bd, ld)
    alpha = (2.0 * depth) ** 0.25
    stacked = (w_ada, b_ada, w_in, conv_w, conv_b, conv_ln_g, conv_ln_b, ssm_conv_w, ssm_conv_b,
               dt_bias, a_log, d_skip, ssm_norm_g, w_out, ln1_g, ln1_b, w_router, router_bias,
               w_exp_gu, w_exp_down, w_sh_gu, w_sh_down, ln2_g, ln2_b)
    lp = [w[0] for w in stacked]
    cond8 = jnp.concatenate([c_ctx[None, :], c, jnp.zeros((8 - 1 - bd, D_MODEL), F32)], axis=0)
    pos = _grid_pos_embed(ld)
    sshape = (bd, N_HEADS, HEADDIM, D_STATE)
    out_p, out_s, hf, hb = _layer(lay, x_prompt.reshape(bp * lp_, D_MODEL), x_sample.reshape(bd * ld, D_MODEL),
                                  pos, cond8, state_ssd_fwd[:, 0].reshape(sshape),
                                  state_ssd_bwd[:, 0].reshape(sshape), lp, alpha)
    return (out_p.reshape(bp, lp_, D_MODEL), out_s.reshape(bd, ld, D_MODEL),
            hf[:, None], hb[:, None])
```

```python
import functools
import math

import numpy as np
import jax
import jax.numpy as jnp
from jax import lax
from jax.experimental import pallas as pl
from jax.experimental.pallas import tpu as pltpu
from jax.experimental.pallas import tpu_sc as plsc

F32 = jnp.float32
BF16 = jnp.bfloat16
I32 = jnp.int32
HI = lax.Precision.HIGHEST

D_MODEL = 1024
GRID_W = 64
D_CONV = 1024
CONV_K = 31
N_HEADS = 16
HEADDIM = 64
D_SSM = N_HEADS * HEADDIM
N_GROUPS = 4
HEADS_PER_GROUP = N_HEADS // N_GROUPS
D_STATE = 128
SSM_CONV_K = 4
CHUNK = 128
D_XBC = D_SSM + 2 * N_GROUPS * D_STATE
N_EXPERTS = 256
TOP_K = 8
N_EXPERT_GROUPS = 8
EXPERTS_PER_GROUP = N_EXPERTS // N_EXPERT_GROUPS
TOPK_GROUPS = 4
D_EXPERT = 256
D_SHARED = 256
ROUTED_SCALE = 2.5
LN_EPS = 1e-5

CONV_TILE = 256
HALO = 16
EXPERT_BLOCK = 256
ROW_PARTS = 2
PART_WORDS = D_MODEL // 2 // ROW_PARTS
VMEM_LIMIT = 56 * 1024 * 1024


def _cparams(sem, vmem=VMEM_LIMIT):
    return pltpu.CompilerParams(dimension_semantics=sem, vmem_limit_bytes=vmem)


def _silu(x):
    return x * jax.nn.sigmoid(x)


def _ln_rows(x):
    mu = jnp.mean(x, axis=-1, keepdims=True)
    xc = x - mu
    var = jnp.mean(xc * xc, axis=-1, keepdims=True)
    return xc * lax.rsqrt(var + LN_EPS)


def _iota(shape, dim):
    return lax.broadcasted_iota(I32, shape, dim)


def _expand_matrix(n_in, width):
    return (_iota((n_in, n_in * width), 0) == _iota((n_in, n_in * width), 1) // width).astype(F32)


def _dot_hi(a, b):
    return jnp.dot(a, b, precision=HI, preferred_element_type=F32)


def _split3(x):
    hi = x.astype(BF16)
    r1 = x - hi.astype(F32)
    mid = r1.astype(BF16)
    lo = (r1 - mid.astype(F32)).astype(BF16)
    return jnp.concatenate([hi, mid, lo], axis=1)


def _expand3(n, width):
    rows = np.arange(3 * n)[:, None] % n
    cols = np.arange(n * width)[None, :] // width
    return jnp.asarray(rows == cols, dtype=BF16)


def _expand_exact(x, e3):
    return jnp.dot(_split3(x), e3, preferred_element_type=F32)


def _ada_kernel(c_ref, w_ref, b_ref, o_ref):
    o_ref[...] = _dot_hi(_silu(c_ref[...]), w_ref[...]) + b_ref[...]


def _ada(cond8, w_ada, b_ada):
    n = w_ada.shape[1]
    tn = 1024
    return pl.pallas_call(
        _ada_kernel,
        grid=(n // tn,),
        in_specs=[pl.BlockSpec((8, D_MODEL), lambda j: (0, 0)),
                  pl.BlockSpec((D_MODEL, tn), lambda j: (0, j)),
                  pl.BlockSpec((1, tn), lambda j: (0, j))],
        out_specs=pl.BlockSpec((8, tn), lambda j: (0, j)),
        out_shape=jax.ShapeDtypeStruct((8, n), F32),
        compiler_params=_cparams(("arbitrary",)),
        name="ada",
    )(cond8, w_ada, b_ada.reshape(1, n))


def _inproj_kernel(row_ref, posb_ref, flag_ref, xp_ref, xs_ref, pos_ref, mod_ref, wm_ref, wdt_ref,
                   glu_ref, z_ref, xbc_ref, dt_ref):
    i = pl.program_id(0)
    x = jnp.where(flag_ref[i] == 1, xs_ref[...] + pos_ref[...], xp_ref[...])
    r = row_ref[i]
    sh1 = mod_ref[pl.ds(r, 1), 0:D_MODEL]
    sc1 = mod_ref[pl.ds(r, 1), D_MODEL:2 * D_MODEL]
    h = (_ln_rows(x) * (1.0 + sc1) + sh1).astype(BF16)
    glu_ref[...] = jnp.dot(h, wm_ref[:, 0:2 * D_CONV], preferred_element_type=F32).astype(BF16)
    z_ref[...] = jnp.dot(h, wm_ref[:, 2 * D_CONV:2 * D_CONV + D_SSM], preferred_element_type=F32).astype(BF16)
    xbc_ref[...] = jnp.dot(h, wm_ref[:, 2 * D_CONV + D_SSM:], preferred_element_type=F32).astype(BF16)
    dt_ref[...] = jnp.dot(h, wdt_ref[...], preferred_element_type=F32)


def _inproj(lay, xp, xs, pos, mod, w_main, w_dt, tm):
    T = lay.n_tokens
    row, posb, flag = lay.token_tile_tables(tm)
    npt = lay.n_prompt_tokens // tm
    n_main = w_main.shape[1]
    gs = pltpu.PrefetchScalarGridSpec(
        num_scalar_prefetch=3,
        grid=(T // tm,),
        in_specs=[pl.BlockSpec((tm, D_MODEL), lambda i, r, p, f: (jnp.minimum(i, npt - 1), 0)),
                  pl.BlockSpec((tm, D_MODEL), lambda i, r, p, f: (jnp.maximum(i - npt, 0), 0)),
                  pl.BlockSpec((tm, D_MODEL), lambda i, r, p, f: (p[i], 0)),
                  pl.BlockSpec((8, 6 * D_MODEL), lambda i, r, p, f: (0, 0)),
                  pl.BlockSpec((D_MODEL, n_main), lambda i, r, p, f: (0, 0)),
                  pl.BlockSpec((D_MODEL, 128), lambda i, r, p, f: (0, 0))],
        out_specs=[pl.BlockSpec((tm, 2 * D_CONV), lambda i, r, p, f: (i, 0)),
                   pl.BlockSpec((tm, D_SSM), lambda i, r, p, f: (i, 0)),
                   pl.BlockSpec((tm, D_XBC), lambda i, r, p, f: (i, 0)),
                   pl.BlockSpec((tm, 128), lambda i, r, p, f: (i, 0))])
    return pl.pallas_call(
        _inproj_kernel, grid_spec=gs,
        out_shape=(jax.ShapeDtypeStruct((T, 2 * D_CONV), BF16),
                   jax.ShapeDtypeStruct((T, D_SSM), BF16),
                   jax.ShapeDtypeStruct((T, D_XBC), BF16),
                   jax.ShapeDtypeStruct((T, 128), F32)),
        compiler_params=_cparams(("arbitrary",)),
        name="inproj",
    )(row, posb, flag, xp, xs, pos, mod, w_main, w_dt)


_N_SHIFT = 8
_SHIFT_ROWS = CONV_TILE + 2 * HALO - _N_SHIFT
_ROW_BLOCK = 64
_FILL_ROWS = 32
_SSM_ROWS, _SSM_LANES = 64, 256


def _conv_kernel(lok_ref, rok_ref, glu_ref, glul_ref, glur_ref, xbc_ref, xbcl_ref, xbcr_ref,
                 cw_ref, cb_ref, lng_ref, lnb_ref, sw_ref, sb_ref, co_ref, xo_ref,
                 ext_ref, sh_ref, acc_ref, ext2_ref):
    i = pl.program_id(0)
    lok = lok_ref[i] == 1
    rok = rok_ref[i] == 1

    def glu(v):
        v = v.astype(F32)
        return v[:, 0:D_CONV] * jax.nn.sigmoid(v[:, D_CONV:])

    def fill_ext(rb, carry):
        r0 = pl.multiple_of(rb * _FILL_ROWS, _FILL_ROWS)
        dst = pl.ds(pl.multiple_of(HALO + r0, HALO), _FILL_ROWS)
        ext_ref[dst, :] = glu(glu_ref[pl.ds(r0, _FILL_ROWS), :])
        ext2_ref[dst, :] = xbc_ref[pl.ds(r0, _FILL_ROWS), :].astype(F32)
        return carry

    ext_ref[0:HALO, :] = jnp.where(lok, glu(glul_ref[...]), 0.0)
    ext_ref[HALO + CONV_TILE:, :] = jnp.where(rok, glu(glur_ref[...]), 0.0)
    ext2_ref[0:HALO, :] = jnp.where(lok, xbcl_ref[...].astype(F32), 0.0)
    ext2_ref[HALO + CONV_TILE:, :] = jnp.where(rok, xbcr_ref[...].astype(F32), 0.0)
    lax.fori_loop(0, CONV_TILE // _FILL_ROWS, fill_ext, 0)
    for r in range(_N_SHIFT):
        sh_ref[r] = ext_ref[r:r + _SHIFT_ROWS, :]

    first = HALO - (CONV_K - 1) // 2

    for j in range(D_CONV // 128):
        lanes = slice(j * 128, (j + 1) * 128)
        taps = [jnp.broadcast_to(cw_ref[k:k + 1, lanes], (8, 128)) for k in range(CONV_K)]
        bias = jnp.broadcast_to(cb_ref[:, lanes], (8, 128))

        def row_block(rb, carry, lanes=lanes, taps=taps, bias=bias):
            base = pl.multiple_of(rb * _ROW_BLOCK, _ROW_BLOCK)
            for sub in range(_ROW_BLOCK // 8):
                acc = bias
                for k in range(CONV_K):
                    o = first + k
                    row0 = base + (o // _N_SHIFT) * _N_SHIFT + sub * 8
                    acc = acc + sh_ref[o % _N_SHIFT, pl.ds(row0, 8), lanes] * taps[k]
                acc_ref[pl.ds(base + sub * 8, 8), lanes] = acc
            return carry

        lax.fori_loop(0, CONV_TILE // _ROW_BLOCK, row_block, 0)
    u = _ln_rows(acc_ref[...]) * lng_ref[...] + lnb_ref[...]
    co_ref[...] = _silu(u).astype(BF16)

    first2 = HALO - (SSM_CONV_K - 1) // 2
    for rb in range(CONV_TILE // _SSM_ROWS):
        for lc in range(D_XBC // _SSM_LANES):
            lanes = slice(lc * _SSM_LANES, (lc + 1) * _SSM_LANES)
            y = jnp.zeros((_SSM_ROWS, _SSM_LANES), F32) + sb_ref[:, lanes]
            for k in range(SSM_CONV_K):
                r0 = first2 + k + rb * _SSM_ROWS
                y = y + ext2_ref[r0:r0 + _SSM_ROWS, lanes] * sw_ref[k:k + 1, lanes]
            xo_ref[rb * _SSM_ROWS:(rb + 1) * _SSM_ROWS, lanes] = _silu(y).astype(BF16)


def _conv(lay, glu, xbc, conv_w, conv_b, ln_g, ln_b, ssm_w, ssm_b):
    T = lay.n_tokens
    lok, rok = lay.conv_tile_tables()
    n_tiles = T // CONV_TILE
    hb = CONV_TILE // HALO
    n_hb = T // HALO

    def cur(i, l, r):
        return (i, 0)

    def left(i, l, r):
        return (jnp.maximum(i * hb - 1, 0), 0)

    def right(i, l, r):
        return (jnp.minimum((i + 1) * hb, n_hb - 1), 0)

    def const(i, l, r):
        return (0, 0)

    gs = pltpu.PrefetchScalarGridSpec(
        num_scalar_prefetch=2,
        grid=(n_tiles,),
        in_specs=[pl.BlockSpec((CONV_TILE, 2 * D_CONV), cur),
                  pl.BlockSpec((HALO, 2 * D_CONV), left),
                  pl.BlockSpec((HALO, 2 * D_CONV), right),
                  pl.BlockSpec((CONV_TILE, D_XBC), cur),
                  pl.BlockSpec((HALO, D_XBC), left),
                  pl.BlockSpec((HALO, D_XBC), right),
                  pl.BlockSpec((CONV_K, D_CONV), const),
                  pl.BlockSpec((1, D_CONV), const),
                  pl.BlockSpec((1, D_CONV), const),
                  pl.BlockSpec((1, D_CONV), const),
                  pl.BlockSpec((SSM_CONV_K, D_XBC), const),
                  pl.BlockSpec((1, D_XBC), const)],
        out_specs=[pl.BlockSpec((CONV_TILE, D_CONV), cur),
                   pl.BlockSpec((CONV_TILE, D_XBC), cur)],
        scratch_shapes=[pltpu.VMEM((CONV_TILE + 2 * HALO, D_CONV), F32),
                        pltpu.VMEM((_N_SHIFT, _SHIFT_ROWS, D_CONV), F32),
                        pltpu.VMEM((CONV_TILE, D_CONV), F32),
                        pltpu.VMEM((CONV_TILE + 2 * HALO, D_XBC), F32)])
    return pl.pallas_call(
        _conv_kernel, grid_spec=gs,
        out_shape=(jax.ShapeDtypeStruct((T, D_CONV), BF16),
                   jax.ShapeDtypeStruct((T, D_XBC), BF16)),
        compiler_params=_cparams(("arbitrary",)),
        name="conv",
    )(lok, rok, glu, glu, glu, xbc, xbc, xbc, conv_w, conv_b.reshape(1, -1), ln_g.reshape(1, -1),
      ln_b.reshape(1, -1), ssm_w, ssm_b.reshape(1, -1))


_BN = N_GROUPS * D_STATE


def _ssd_kernel(chunk_ref, yidx_ref, phase_ref, first_ref, last_ref, zero_ref, sin_ref, sout_ref, cloc_ref,
                xbc_ref, z_ref, dt_ref, h0f_ref, h0b_ref, dtb_ref, alog_ref, dsk_ref, ng_ref,
                tri_ref, edec_ref, ewb_ref, ecol_ref, ewide_ref, eye3_ref,
                y_ref, hf_out_ref, hb_out_ref,
                hf_ref, g_ref, gin_ref, ybuf_ref):
    s = pl.program_id(0)
    phase = phase_ref[s]
    first = first_ref[s] == 1
    last = last_ref[s] == 1
    zero = zero_ref[s] == 1
    cloc = cloc_ref[s]
    H, P, N = N_HEADS, HEADDIM, D_STATE

    GW = HEADS_PER_GROUP * P
    xs = xbc_ref[:, 0:D_SSM]
    dt = dt_ref[:, 0:2 * H] + dtb_ref[...]
    dt = jnp.maximum(dt, 0.0) + jnp.log1p(jnp.exp(-jnp.abs(dt)))
    a = dt * (-jnp.exp(alog_ref[...]))
    a3 = jnp.dot(tri_ref[...], _split3(a), preferred_element_type=F32)
    acs = a3[:, 0:2 * H] + a3[:, 2 * H:4 * H] + a3[:, 4 * H:6 * H]
    tot = acs[CHUNK - 8:CHUNK, :]
    dec = _expand_exact(jnp.exp(tot), edec_ref[...])[7:8, :]
    exb = acs[:, H:2 * H] - a[:, H:2 * H]

    def load_state(src_ref, dst_ref):
        for j in range(H // 2):
            pair = jnp.concatenate([src_ref[0, 2 * j], src_ref[0, 2 * j + 1]], axis=0)
            dst_ref[:, 2 * j * P:(2 * j + 2) * P] = jnp.where(zero, 0.0, pair.T)

    def store_state(src_ref, dst_ref):
        for j in range(H // 2):
            pair = src_ref[:, 2 * j * P:(2 * j + 2) * P].T
            dst_ref[0, 2 * j] = pair[0:P]
            dst_ref[0, 2 * j + 1] = pair[P:2 * P]

    @pl.when(phase == 0)
    def _backward_states():
        @pl.when(first)
        def _():
            load_state(h0b_ref, g_ref)

        wb = dt[:, H:2 * H] * jnp.exp(exb)
        xw = (xs.astype(F32) * _expand_exact(wb, ewb_ref[...])).astype(BF16)
        for g in range(N_GROUPS):
            cols = slice(g * GW, (g + 1) * GW)
            bg = xbc_ref[:, D_SSM + g * N:D_SSM + (g + 1) * N]
            gg = g_ref[:, cols]
            gin_ref[cloc, :, cols] = gg.astype(BF16)
            upd = lax.dot_general(bg, xw[:, cols], (((0,), (0,)), ((), ())), preferred_element_type=F32)
            g_ref[:, cols] = gg * dec[:, D_SSM + g * GW:D_SSM + (g + 1) * GW] + upd

        @pl.when(last)
        def _():
            store_state(g_ref, hb_out_ref)

    @pl.when(phase == 1)
    def _forward_and_outputs():
        @pl.when(first)
        def _():
            load_state(h0f_ref, hf_ref)

        acsf = acs[:, 0:H]
        dtf = dt[:, 0:H]
        dtb = dt[:, H:2 * H]
        totf = acs[CHUNK - 1:CHUNK, 0:H]
        totb = acs[CHUNK - 1:CHUNK, H:2 * H]
        col = _expand_exact(jnp.concatenate([acsf, exb], axis=1), ecol_ref[...])
        q3 = _split3(jnp.concatenate([acsf, exb, dtf, dtb], axis=1))
        qt = lax.dot_general(eye3_ref[...], q3, (((1,), (1,)), ((), ())),
                             preferred_element_type=F32)
        wide = jnp.concatenate([dtf * jnp.exp(totf - acsf), jnp.exp(acsf), jnp.exp(totb - exb)], axis=1)
        wide = _expand_exact(wide, ewide_ref[...])
        xsf = xs.astype(F32)
        xw = (xsf * wide[:, 0:D_SSM]).astype(BF16)
        lower = _iota((CHUNK, CHUNK), 1) <= _iota((CHUNK, CHUNK), 0)
        upper = _iota((CHUNK, CHUNK), 1) >= _iota((CHUNK, CHUNK), 0)
        for g in range(N_GROUPS):
            cols = slice(g * GW, (g + 1) * GW)
            bg = xbc_ref[:, D_SSM + g * N:D_SSM + (g + 1) * N]
            cg = xbc_ref[:, D_SSM + _BN + g * N:D_SSM + _BN + (g + 1) * N]
            cb = lax.dot_general(cg, bg, (((1,), (1,)), ((), ())), preferred_element_type=F32)
            hfg = hf_ref[:, cols]
            yf = jnp.dot(cg, hfg.astype(BF16), preferred_element_type=F32)
            yb = jnp.dot(cg, gin_ref[cloc, :, cols], preferred_element_type=F32)
            ybuf_ref[:, cols] = yf * wide[:, D_SSM + g * GW:D_SSM + (g + 1) * GW] \
                + yb * wide[:, 2 * D_SSM + g * GW:2 * D_SSM + (g + 1) * GW]
            upd = lax.dot_general(bg, xw[:, cols], (((0,), (0,)), ((), ())), preferred_element_type=F32)
            hf_ref[:, cols] = hfg * dec[:, cols] + upd
            for r in range(HEADS_PER_GROUP):
                h = g * HEADS_PER_GROUP + r
                colf = col[:, h * N:(h + 1) * N]
                colb = col[:, (H + h) * N:(H + h + 1) * N]
                mf = jnp.where(lower, jnp.exp(colf - qt[h:h + 1, :]), 0.0) * qt[2 * H + h:2 * H + h + 1, :]
                mb = jnp.where(upper, jnp.exp(qt[H + h:H + h + 1, :] - colb), 0.0) * qt[3 * H + h:3 * H + h + 1, :]
                m = (cb * (mf + mb)).astype(BF16)
                hs = slice(h * P, (h + 1) * P)
                ybuf_ref[:, hs] += jnp.dot(m, xs[:, hs], preferred_element_type=F32)

        yt = (ybuf_ref[...] + dsk_ref[...] * xsf) * _silu(z_ref[...].astype(F32))
        gw = D_SSM // N_GROUPS
        for g in range(N_GROUPS):
            seg = yt[:, g * gw:(g + 1) * gw]
            ms = jnp.mean(seg * seg, axis=-1, keepdims=True)
            y_ref[:, g * gw:(g + 1) * gw] = (seg * lax.rsqrt(ms + LN_EPS) * ng_ref[:, g * gw:(g + 1) * gw]).astype(BF16)

        @pl.when(last)
        def _():
            store_state(hf_ref, hf_out_ref)


def _ssd(lay, xbc_c, z, dt_raw, h0f, h0b, dt_bias, a_log, d_skip, norm_g):
    T = lay.n_tokens
    tabs = lay.ssd_step_tables()
    n_steps = tabs[0].shape[0]
    nsp = len(tabs)

    def by_chunk(s, *t):
        return (t[0][s], 0)

    def by_y(s, *t):
        return (t[1][s], 0)

    def by_sin(s, *t):
        return (t[6][s], 0, 0, 0)

    def by_sout(s, *t):
        return (t[7][s], 0, 0, 0)

    def const(s, *t):
        return (0, 0)

    H = N_HEADS
    tri = jnp.asarray(np.tril(np.ones((CHUNK, CHUNK))), dtype=BF16)
    eye3 = jnp.asarray(np.arange(4 * H)[:, None] == np.arange(12 * H)[None, :] % (4 * H), dtype=BF16)
    consts = [tri, _expand3(2 * H, HEADDIM), _expand3(H, HEADDIM), _expand3(2 * H, D_STATE),
              _expand3(3 * H, HEADDIM), eye3]
    sshape = (1, N_HEADS, HEADDIM, D_STATE)
    gs = pltpu.PrefetchScalarGridSpec(
        num_scalar_prefetch=nsp,
        grid=(n_steps,),
        in_specs=[pl.BlockSpec((CHUNK, D_XBC), by_chunk),
                  pl.BlockSpec((CHUNK, D_SSM), by_chunk),
                  pl.BlockSpec((CHUNK, 128), by_chunk),
                  pl.BlockSpec(sshape, by_sin),
                  pl.BlockSpec(sshape, by_sin),
                  pl.BlockSpec((1, 2 * N_HEADS), const),
                  pl.BlockSpec((1, 2 * N_HEADS), const),
                  pl.BlockSpec((1, D_SSM), const),
                  pl.BlockSpec((1, D_SSM), const)] + [pl.BlockSpec(c.shape, const) for c in consts],
        out_specs=[pl.BlockSpec((CHUNK, D_SSM), by_y),
                   pl.BlockSpec(sshape, by_sout),
                   pl.BlockSpec(sshape, by_sout)],
        scratch_shapes=[pltpu.VMEM((D_STATE, D_SSM), F32),
                        pltpu.VMEM((D_STATE, D_SSM), F32),
                        pltpu.VMEM((lay.max_chunks, D_STATE, D_SSM), BF16),
                        pltpu.VMEM((CHUNK, D_SSM), F32)])
    n_out = lay.n_prompt_seqs
    return pl.pallas_call(
        _ssd_kernel, grid_spec=gs,
        out_shape=(jax.ShapeDtypeStruct((T, D_SSM), BF16),
                   jax.ShapeDtypeStruct((n_out,) + sshape[1:], F32),
                   jax.ShapeDtypeStruct((n_out,) + sshape[1:], F32)),
        compiler_params=_cparams(("arbitrary",)),
        name="ssd",
    )(*tabs, xbc_c, z, dt_raw, h0f, h0b, dt_bias.reshape(1, -1), a_log.reshape(1, -1),
      jnp.repeat(d_skip, HEADDIM).reshape(1, -1), norm_g.reshape(1, -1), *consts)


def _outproj_kernel(row_ref, posb_ref, flag_ref, xp_ref, xs_ref, pos_ref, mod_ref, co_ref, ys_ref, wo_ref,
                    g_ref, b_ref, x1_ref, h2_ref, h2p_ref, *, alpha):
    i = pl.program_id(0)
    x = jnp.where(flag_ref[i] == 1, xs_ref[...] + pos_ref[...], xp_ref[...])
    r = row_ref[i]
    g1 = mod_ref[pl.ds(r, 1), 2 * D_MODEL:3 * D_MODEL]
    sh2 = mod_ref[pl.ds(r, 1), 3 * D_MODEL:4 * D_MODEL]
    sc2 = mod_ref[pl.ds(r, 1), 4 * D_MODEL:5 * D_MODEL]
    mix = jnp.dot(co_ref[...], wo_ref[0:D_CONV, :], preferred_element_type=F32) \
        + jnp.dot(ys_ref[...], wo_ref[D_CONV:, :], preferred_element_type=F32)
    x1 = _ln_rows(alpha * x + g1 * mix) * g_ref[...] + b_ref[...]
    x1_ref[...] = x1
    h2 = _ln_rows(x1) * (1.0 + sc2) + sh2
    h2_ref[...] = h2
    packed = _pack_halves(h2)
    for c in range(ROW_PARTS):
        h2p_ref[c] = packed[:, c * PART_WORDS:(c + 1) * PART_WORDS]


def _outproj(lay, xp, xs, pos, mod, conv_out, y_ssm, w_out, ln_g, ln_b, alpha, tm):
    T = lay.n_tokens
    row, posb, flag = lay.token_tile_tables(tm)
    npt = lay.n_prompt_tokens // tm

    def const(i, r, p, f):
        return (0, 0)

    def cur(i, r, p, f):
        return (i, 0)

    gs = pltpu.PrefetchScalarGridSpec(
        num_scalar_prefetch=3,
        grid=(T // tm,),
        in_specs=[pl.BlockSpec((tm, D_MODEL), lambda i, r, p, f: (jnp.minimum(i, npt - 1), 0)),
                  pl.BlockSpec((tm, D_MODEL), lambda i, r, p, f: (jnp.maximum(i - npt, 0), 0)),
                  pl.BlockSpec((tm, D_MODEL), lambda i, r, p, f: (p[i], 0)),
                  pl.BlockSpec((8, 6 * D_MODEL), const),
                  pl.BlockSpec((tm, D_CONV), cur),
                  pl.BlockSpec((tm, D_SSM), cur),
                  pl.BlockSpec((D_CONV + D_SSM, D_MODEL), const),
                  pl.BlockSpec((1, D_MODEL), const),
                  pl.BlockSpec((1, D_MODEL), const)],
        out_specs=[pl.BlockSpec((tm, D_MODEL), cur),
                   pl.BlockSpec((tm, D_MODEL), cur),
                   pl.BlockSpec((ROW_PARTS, tm, PART_WORDS), lambda i, r, p, f: (0, i, 0))])
    return pl.pallas_call(
        functools.partial(_outproj_kernel, alpha=alpha), grid_spec=gs,
        out_shape=(jax.ShapeDtypeStruct((T, D_MODEL), F32),
                   jax.ShapeDtypeStruct((T, D_MODEL), F32),
                   jax.ShapeDtypeStruct((ROW_PARTS, T, PART_WORDS), jnp.uint32)),
        compiler_params=_cparams(("arbitrary",)),
        name="outproj",
    )(row, posb, flag, xp, xs, pos, mod, conv_out, y_ssm, w_out, ln_g.reshape(1, -1), ln_b.reshape(1, -1))


def _route_kernel(h2_ref, wrt_ref, bias_ref, idx_ref, wts_ref, pos_ref, cnt_ref, carry_ref, *, tm):
    i = pl.program_id(0)

    @pl.when(i == 0)
    def _():
        carry_ref[...] = jnp.zeros_like(carry_ref)

    E, NG, EG = N_EXPERTS, N_EXPERT_GROUPS, EXPERTS_PER_GROUP
    neg = -jnp.inf
    logits = lax.dot_general(wrt_ref[...], h2_ref[...], (((1,), (1,)), ((), ())), precision=HI,
                             preferred_element_type=F32)
    s = jax.nn.sigmoid(logits)
    sel = s + bias_ref[...]
    sel3 = sel.reshape(NG, EG, tm)
    io3 = _iota((NG, EG, tm), 1)
    m1 = jnp.max(sel3, axis=1, keepdims=True)
    f1 = jnp.min(jnp.where(sel3 == m1, io3, EG), axis=1, keepdims=True)
    m2 = jnp.max(jnp.where(io3 == f1, neg, sel3), axis=1, keepdims=True)
    gscore = (m1 + m2).reshape(NG, tm)
    gio = _iota((NG, tm), 0)
    beaten = jnp.zeros((NG, tm), I32)
    for g in range(NG):
        row = gscore[g:g + 1, :]
        beats = jnp.where(row > gscore, 1, jnp.where(row == gscore, jnp.where(g < gio, 1, 0), 0))
        beaten = beaten + beats
    keep = (beaten < TOPK_GROUPS).astype(F32).reshape(NG, 1, tm)
    selm = jnp.where(keep > 0.5, sel3, neg).reshape(E, tm)
    eio = _iota((E, tm), 0)
    chosen = jnp.zeros((E, tm), F32)
    idxs, ws = [], []
    for k in range(TOP_K):
        m = jnp.max(selm, axis=0, keepdims=True)
        am = jnp.minimum(jnp.min(jnp.where(selm == m, eio, E), axis=0, keepdims=True), E - 1)
        hit = eio == am
        ws.append(jnp.sum(jnp.where(hit, s, 0.0), axis=0, keepdims=True))
        idxs.append(am)
        selm = jnp.where(hit, neg, selm)
        chosen = jnp.where(hit, 1.0, chosen)
    wsum = ws[0]
    for k in range(1, TOP_K):
        wsum = wsum + ws[k]
    before = (_iota((tm, tm), 0) < _iota((tm, tm), 1)).astype(BF16)
    prior = jnp.dot(chosen.astype(BF16), before, preferred_element_type=F32)
    carry = carry_ref[...]
    prior = prior + jnp.concatenate([carry] * (tm // 128), axis=1)
    for k in range(TOP_K):
        idx_ref[k:k + 1, :] = idxs[k]
        wts_ref[k:k + 1, :] = ws[k] / wsum * ROUTED_SCALE
        pos_ref[k:k + 1, :] = jnp.sum(jnp.where(eio == idxs[k], prior, 0.0), axis=0, keepdims=True).astype(I32)
    total = jnp.dot(chosen.astype(BF16), jnp.ones((tm, 128), BF16), preferred_element_type=F32)
    carry = carry + total
    carry_ref[...] = carry
    cnt_ref[...] = carry.astype(I32)


def _route(h2, w_router_t, router_bias, tm):
    T = h2.shape[0]
    bias_b = jnp.broadcast_to(router_bias.astype(F32)[:, None], (N_EXPERTS, tm))
    return pl.pallas_call(
        functools.partial(_route_kernel, tm=tm),
        grid=(T // tm,),
        in_specs=[pl.BlockSpec((tm, D_MODEL), lambda i: (i, 0)),
                  pl.BlockSpec((N_EXPERTS, D_MODEL), lambda i: (0, 0)),
                  pl.BlockSpec((N_EXPERTS, tm), lambda i: (0, 0))],
        out_specs=[pl.BlockSpec((TOP_K, tm), lambda i: (0, i)),
                   pl.BlockSpec((TOP_K, tm), lambda i: (0, i)),
                   pl.BlockSpec((TOP_K, tm), lambda i: (0, i)),
                   pl.BlockSpec((N_EXPERTS, 128), lambda i: (0, 0))],
        out_shape=(jax.ShapeDtypeStruct((TOP_K, T), I32),
                   jax.ShapeDtypeStruct((TOP_K, T), F32),
                   jax.ShapeDtypeStruct((TOP_K, T), I32),
                   jax.ShapeDtypeStruct((N_EXPERTS, 128), I32)),
        scratch_shapes=[pltpu.VMEM((N_EXPERTS, 128), F32)],
        compiler_params=_cparams(("arbitrary",)),
        name="route",
    )(h2, w_router_t, bias_b)


def _dest_kernel(idx_ref, pos_ref, start_ref, dest_ref):
    tm = idx_ref.shape[1]
    eio = _iota((N_EXPERTS, tm), 0)
    start = start_ref[...]
    for k in range(TOP_K):
        base = jnp.sum(jnp.where(eio == idx_ref[k:k + 1, :], start, 0.0), axis=0, keepdims=True)
        dest_ref[k:k + 1, :] = base.astype(I32) + pos_ref[k:k + 1, :]


def _dest(idx, pos, pad_start, tm):
    T = idx.shape[1]
    start_b = jnp.broadcast_to(pad_start.astype(F32)[:, None], (N_EXPERTS, tm))
    return pl.pallas_call(
        _dest_kernel,
        grid=(T // tm,),
        in_specs=[pl.BlockSpec((TOP_K, tm), lambda i: (0, i)),
                  pl.BlockSpec((TOP_K, tm), lambda i: (0, i)),
                  pl.BlockSpec((N_EXPERTS, tm), lambda i: (0, 0))],
        out_specs=pl.BlockSpec((TOP_K, tm), lambda i: (0, i)),
        out_shape=jax.ShapeDtypeStruct((TOP_K, T), I32),
        compiler_params=_cparams(("arbitrary",)),
        name="dest",
    )(idx, pos, start_b)


_WEIGHT_FETCH_CHUNKS = 4


def _expert_kernel(base_ref, nblk_ref, nused_ref, x_hbm, wgu_hbm, wd_hbm, y_hbm,
                   wgu_bf, wd_bf, wgu_stage, wd_stage, xbuf, ybuf, sems, xsems, ysems):
    e = pl.program_id(0)
    n_used = nused_ref[0]

    def x_copy(b, slot):
        rows = pl.ds(pl.multiple_of(b * EXPERT_BLOCK, EXPERT_BLOCK), EXPERT_BLOCK)
        return pltpu.make_async_copy(x_hbm.at[:, rows, :], xbuf.at[slot], xsems.at[slot])

    def y_copy(b, slot):
        rows = pl.ds(pl.multiple_of(b * EXPERT_BLOCK, EXPERT_BLOCK), EXPERT_BLOCK)
        return pltpu.make_async_copy(ybuf.at[slot], y_hbm.at[:, rows, :], ysems.at[slot])

    def fetch(e):
        cps = []
        for c in range(_WEIGHT_FETCH_CHUNKS):
            rg = pl.ds(c * (D_MODEL // _WEIGHT_FETCH_CHUNKS), D_MODEL // _WEIGHT_FETCH_CHUNKS)
            rd = pl.ds(c * (D_EXPERT // _WEIGHT_FETCH_CHUNKS), D_EXPERT // _WEIGHT_FETCH_CHUNKS)
            cps.append(pltpu.make_async_copy(wgu_hbm.at[e, rg], wgu_stage.at[rg], sems.at[0]))
            cps.append(pltpu.make_async_copy(wd_hbm.at[e, rd], wd_stage.at[rd], sems.at[1]))
        return cps

    @pl.when(e == 0)
    def _():
        for cp in fetch(0):
            cp.start()

        @pl.when(n_used > 0)
        def _():
            x_copy(0, 0).start()

    for cp in fetch(e):
        cp.wait()
    wgu_bf[...] = wgu_stage[...].astype(BF16)
    wd_bf[...] = wd_stage[...].astype(BF16)

    @pl.when(e + 1 < N_EXPERTS)
    def _():
        for cp in fetch(e + 1):
            cp.start()

    def block(b, carry):
        slot = lax.rem(b, 2)
        x_copy(b, slot).wait()

        @pl.when(b + 1 < n_used)
        def _():
            x_copy(b + 1, 1 - slot).start()

        @pl.when(b >= 2)
        def _():
            y_copy(b - 2, slot).wait()

        parts = [_unpack_halves(xbuf[slot, c]) for c in range(ROW_PARTS)]
        chunks = [p[0] for p in parts] + [p[1] for p in parts]
        gu = jnp.zeros((EXPERT_BLOCK, 2 * D_EXPERT), F32)
        for j, xc in enumerate(chunks):
            gu = gu + jnp.dot(xc.astype(BF16), wgu_bf[j * PART_WORDS:(j + 1) * PART_WORDS, :],
                              preferred_element_type=F32)
        act = (_silu(gu[:, 0:D_EXPERT]) * gu[:, D_EXPERT:]).astype(BF16)
        packed = _pack_halves(jnp.dot(act, wd_bf[...], preferred_element_type=F32))
        for c in range(ROW_PARTS):
            ybuf[slot, c] = packed[:, c * PART_WORDS:(c + 1) * PART_WORDS]
        y_copy(b, slot).start()
        return carry

    lax.fori_loop(base_ref[e], base_ref[e] + nblk_ref[e], block, 0)

    @pl.when(e == N_EXPERTS - 1)
    def _():
        @pl.when(n_used >= 2)
        def _():
            y_copy(n_used - 2, lax.rem(n_used, 2)).wait()

        @pl.when(n_used >= 1)
        def _():
            y_copy(n_used - 1, lax.rem(n_used - 1, 2)).wait()


def _pack_halves(x):
    n = x.shape[1] // 2
    hi = lax.bitcast_convert_type(x[:, :n].astype(BF16).astype(F32), jnp.uint32)
    lo = lax.bitcast_convert_type(x[:, n:].astype(BF16).astype(F32), jnp.uint32)
    return hi | (lo >> 16)


def _unpack_halves(p):
    hi = lax.bitcast_convert_type(p & jnp.uint32(0xFFFF0000), F32)
    lo = lax.bitcast_convert_type(p << 16, F32)
    return hi, lo


def _expert(x_sorted, w_gu, w_down, blk_base, blk_count, n_used):
    n_rows = x_sorted.shape[1]
    blk_shape = (2, ROW_PARTS, EXPERT_BLOCK, PART_WORDS)
    gs = pltpu.PrefetchScalarGridSpec(
        num_scalar_prefetch=3,
        grid=(N_EXPERTS,),
        in_specs=[pl.BlockSpec(memory_space=pl.ANY),
                  pl.BlockSpec(memory_space=pl.ANY),
                  pl.BlockSpec(memory_space=pl.ANY)],
        out_specs=pl.BlockSpec(memory_space=pl.ANY),
        scratch_shapes=[pltpu.VMEM((D_MODEL, 2 * D_EXPERT), BF16),
                        pltpu.VMEM((D_EXPERT, D_MODEL), BF16),
                        pltpu.VMEM((D_MODEL, 2 * D_EXPERT), F32),
                        pltpu.VMEM((D_EXPERT, D_MODEL), F32),
                        pltpu.VMEM(blk_shape, jnp.uint32),
                        pltpu.VMEM(blk_shape, jnp.uint32),
                        pltpu.SemaphoreType.DMA((2,)),
                        pltpu.SemaphoreType.DMA((2,)),
                        pltpu.SemaphoreType.DMA((2,))])
    return pl.pallas_call(
        _expert_kernel, grid_spec=gs,
        out_shape=jax.ShapeDtypeStruct((ROW_PARTS, n_rows, PART_WORDS), jnp.uint32),
        compiler_params=_cparams(("arbitrary",)),
        name="expert",
    )(blk_base, blk_count, n_used, x_sorted, w_gu, w_down)


def _combine_kernel(row_ref, h2_ref, x1_ref, wt_ref, mod_ref, wsg_ref, wsd_ref, g_ref, b_ref, yt_ref,
                    op_ref, os_ref, *, n_prompt_tiles, alpha):
    i = pl.program_id(0)
    h2 = h2_ref[...].astype(BF16)
    su = jnp.dot(h2, wsg_ref[...], preferred_element_type=F32)
    act = (_silu(su[:, 0:D_SHARED]) * su[:, D_SHARED:]).astype(BF16)
    moe = jnp.dot(act, wsd_ref[...], preferred_element_type=F32)
    wt = wt_ref[...]
    his, los = [], []
    for c in range(ROW_PARTS):
        rh = jnp.zeros((h2.shape[0], PART_WORDS), F32)
        rl = jnp.zeros((h2.shape[0], PART_WORDS), F32)
        for k in range(TOP_K):
            hi, lo = _unpack_halves(yt_ref[c, k])
            w = wt[:, k:k + 1]
            rh = rh + hi * w
            rl = rl + lo * w
        his.append(rh)
        los.append(rl)
    moe = moe + jnp.concatenate(his + los, axis=1)
    g2 = mod_ref[pl.ds(row_ref[i], 1), 5 * D_MODEL:6 * D_MODEL]
    out = _ln_rows(alpha * x1_ref[...] + g2 * moe) * g_ref[...] + b_ref[...]

    @pl.when(i < n_prompt_tiles)
    def _():
        op_ref[...] = out

    @pl.when(i >= n_prompt_tiles)
    def _():
        os_ref[...] = out


def _combine(lay, h2, x1, wts_tok, mod, w_sh_gu, w_sh_down, ln_g, ln_b, y_tok, alpha, tm):
    T = lay.n_tokens
    row, _, _ = lay.token_tile_tables(tm)
    npt = lay.n_prompt_tokens // tm

    def cur(i, r):
        return (i, 0)

    def const(i, r):
        return (0, 0)

    gs = pltpu.PrefetchScalarGridSpec(
        num_scalar_prefetch=1,
        grid=(T // tm,),
        in_specs=[pl.BlockSpec((tm, D_MODEL), cur),
                  pl.BlockSpec((tm, D_MODEL), cur),
                  pl.BlockSpec((tm, TOP_K), cur),
                  pl.BlockSpec((8, 6 * D_MODEL), const),
                  pl.BlockSpec((D_MODEL, 2 * D_SHARED), const),
                  pl.BlockSpec((D_SHARED, D_MODEL), const),
                  pl.BlockSpec((1, D_MODEL), const),
                  pl.BlockSpec((1, D_MODEL), const),
                  pl.BlockSpec((ROW_PARTS, TOP_K, tm, PART_WORDS), lambda i, r: (0, 0, i, 0))],
        out_specs=[pl.BlockSpec((tm, D_MODEL), lambda i, r: (jnp.minimum(i, npt - 1), 0)),
                   pl.BlockSpec((tm, D_MODEL), lambda i, r: (jnp.maximum(i - npt, 0), 0))])
    return pl.pallas_call(
        functools.partial(_combine_kernel, n_prompt_tiles=npt, alpha=alpha), grid_spec=gs,
        out_shape=(jax.ShapeDtypeStruct((lay.n_prompt_tokens, D_MODEL), F32),
                   jax.ShapeDtypeStruct((T - lay.n_prompt_tokens, D_MODEL), F32)),
        compiler_params=_cparams(("arbitrary",)),
        name="combine",
    )(row, h2, x1, wts_tok, mod, w_sh_gu, w_sh_down, ln_g.reshape(1, -1), ln_b.reshape(1, -1), y_tok)


_GATHER_WINDOW = 128


def _sc_gather(table, idx):
    n, d = idx.shape[0], table.shape[1]
    mesh = plsc.VectorSubcoreMesh(core_axis_name="core", subcore_axis_name="subcore")

    @pl.kernel(out_type=jax.ShapeDtypeStruct((n, d), table.dtype), mesh=mesh)
    def gather_kernel(table_hbm, idx_hbm, out_hbm):
        def body(idx_vmem, out_vmem):
            pltpu.sync_copy(table_hbm.at[idx_vmem.at[0]], out_vmem)

        pltpu.emit_pipeline(
            body,
            grid=(n // _GATHER_WINDOW,),
            in_specs=[pl.BlockSpec((1, _GATHER_WINDOW), index_map=lambda i: (0, i))],
            out_specs=[pl.BlockSpec((_GATHER_WINDOW, d), index_map=lambda i: (i, 0))],
            core_axis_name=("core", "subcore"),
            dimension_semantics=(pltpu.PARALLEL,),
        )(idx_hbm, out_hbm)

    return gather_kernel(table, idx.reshape(1, n))


def _sc_scatter(rows, idx, n_out, repeat):
    n, d = rows.shape
    tiles = n // _GATHER_WINDOW
    mesh = plsc.VectorSubcoreMesh(core_axis_name="core", subcore_axis_name="subcore")

    @pl.kernel(out_type=jax.ShapeDtypeStruct((n_out, d), rows.dtype), mesh=mesh, scratch_types=[])
    def scatter_kernel(rows_hbm, idx_hbm, out_hbm):
        def body(rows_vmem, idx_vmem):
            pltpu.sync_copy(rows_vmem, out_hbm.at[idx_vmem.at[0]])

        pltpu.emit_pipeline(
            body,
            grid=(repeat * tiles,),
            in_specs=[pl.BlockSpec((_GATHER_WINDOW, d), index_map=lambda i: (i % tiles, 0)),
                      pl.BlockSpec((1, _GATHER_WINDOW), index_map=lambda i: (0, i))],
            out_specs=[],
            core_axis_name=("core", "subcore"),
            dimension_semantics=(pltpu.PARALLEL,),
        )(rows_hbm, idx_hbm)

    return scatter_kernel(rows, idx.reshape(1, repeat * n))


class _Layout:
    def __init__(self, n_prompt_seqs, prompt_len, n_sample_seqs, sample_len):
        self.n_prompt_seqs, self.prompt_len = n_prompt_seqs, prompt_len
        self.n_sample_seqs, self.sample_len = n_sample_seqs, sample_len
        self.n_prompt_tokens = n_prompt_seqs * prompt_len
        self.n_tokens = self.n_prompt_tokens + n_sample_seqs * sample_len
        assert prompt_len % CONV_TILE == 0 and sample_len % CONV_TILE == 0
        self.max_chunks = max(prompt_len, sample_len) // CHUNK

    def token_tile_tables(self, tm):
        assert self.n_prompt_tokens % tm == 0 and self.sample_len % tm == 0
        npt = self.n_prompt_tokens // tm
        per_seq = self.sample_len // tm
        n = self.n_tokens // tm
        row = np.zeros(n, np.int32)
        posb = np.zeros(n, np.int32)
        flag = np.zeros(n, np.int32)
        for i in range(npt, n):
            j = i - npt
            row[i] = 1 + j // per_seq
            posb[i] = j % per_seq
            flag[i] = 1
        return jnp.asarray(row), jnp.asarray(posb), jnp.asarray(flag)

    def conv_tile_tables(self):
        lok, rok = [], []
        for n_seq, length in ((self.n_prompt_seqs, self.prompt_len), (self.n_sample_seqs, self.sample_len)):
            per = length // CONV_TILE
            for _ in range(n_seq):
                for j in range(per):
                    lok.append(int(j > 0))
                    rok.append(int(j < per - 1))
        return jnp.asarray(np.array(lok, np.int32)), jnp.asarray(np.array(rok, np.int32))

    def ssd_step_tables(self):
        cols = [[] for _ in range(9)]
        seqs = []
        c0 = self.n_prompt_tokens // CHUNK
        for j in range(self.n_sample_seqs):
            nc = self.sample_len // CHUNK
            seqs.append((c0 + j * nc, nc, 0, j, 0))
        for j in range(self.n_prompt_seqs):
            nc = self.prompt_len // CHUNK
            seqs.append((j * nc, nc, 1, 0, j))
        for base, nc, zero, sin, sout in seqs:
            for phase in (0, 1):
                order = range(nc - 1, -1, -1) if phase == 0 else range(nc)
                for n, c in enumerate(order):
                    vals = (base + c, base if phase == 0 else base + c, phase, int(n == 0), int(n == nc - 1),
                            zero, sin, sout, c)
                    for col, v in zip(cols, vals):
                        col.append(v)
        return tuple(jnp.asarray(np.array(col, np.int32)) for col in cols)


def _grid_pos_embed(n_tokens):
    rows = n_tokens // GRID_W
    quarter = D_MODEL // 4
    freq = jnp.exp(-math.log(10000.0) * jnp.arange(quarter, dtype=F32) / quarter)
    r = jnp.broadcast_to(jnp.arange(rows, dtype=F32)[:, None, None] * freq, (rows, GRID_W, quarter))
    cl = jnp.broadcast_to(jnp.arange(GRID_W, dtype=F32)[None, :, None] * freq, (rows, GRID_W, quarter))
    emb = jnp.concatenate([jnp.sin(r), jnp.cos(r), jnp.sin(cl), jnp.cos(cl)], axis=-1)
    return emb.reshape(rows * GRID_W, D_MODEL)


def _moe_plan(counts):
    blk_count = (counts + EXPERT_BLOCK - 1) // EXPERT_BLOCK
    blk_end = jnp.cumsum(blk_count)
    blk_base = blk_end - blk_count
    return ((blk_base * EXPERT_BLOCK).astype(I32), blk_base.astype(I32), blk_count.astype(I32),
            blk_end[-1:].astype(I32))


def _layer(lay, xp, xs, pos, cond8, h0f, h0b, lp, alpha, tm_proj=512, tm_route=256, tm_comb=256):
    (w_ada, b_ada, w_in, conv_w, conv_b, conv_ln_g, conv_ln_b, ssm_conv_w, ssm_conv_b, dt_bias, a_log,
     d_skip, ssm_norm_g, w_out, ln1_g, ln1_b, w_router, router_bias, w_exp_gu, w_exp_down, w_sh_gu,
     w_sh_down, ln2_g, ln2_b) = lp
    T = lay.n_tokens
    n_main = 2 * D_CONV + D_SSM + D_XBC
    w_main = w_in[:, :n_main].astype(BF16)
    w_dt = jnp.pad(w_in[:, n_main:], ((0, 0), (0, 128 - 2 * N_HEADS))).astype(BF16)

    mod = _ada(cond8, w_ada, b_ada)
    glu, z, xbc, dt_raw = _inproj(lay, xp, xs, pos, mod, w_main, w_dt, tm_proj)
    conv_out, xbc_c = _conv(lay, glu, xbc, conv_w, conv_b, conv_ln_g, conv_ln_b, ssm_conv_w, ssm_conv_b)
    y_ssm, hf, hb = _ssd(lay, xbc_c, z, dt_raw, h0f, h0b, dt_bias, a_log, d_skip, ssm_norm_g)
    x1, h2, h2p = _outproj(lay, xp, xs, pos, mod, conv_out, y_ssm, w_out.astype(BF16), ln1_g, ln1_b, alpha, tm_proj)

    idx, wts, posn, cnt = _route(h2, w_router.T, router_bias, tm_route)
    n_blocks = -(-T * TOP_K // EXPERT_BLOCK) + N_EXPERTS
    pad_start, blk_base, blk_count, n_used = _moe_plan(cnt[:, 0])
    dest2 = _dest(idx, posn, pad_start, 512)
    n_rows = n_blocks * EXPERT_BLOCK
    scatter_idx = jnp.concatenate([dest2 + c * n_rows for c in range(ROW_PARTS)], axis=1)
    x_sorted = _sc_scatter(h2p.reshape(ROW_PARTS * T, PART_WORDS), scatter_idx.reshape(-1),
                           ROW_PARTS * n_rows, TOP_K).reshape(ROW_PARTS, n_rows, PART_WORDS)
    y_sorted = _expert(x_sorted, w_exp_gu, w_exp_down, blk_base, blk_count, n_used)
    dest = dest2.reshape(-1)
    idx_parts = jnp.concatenate([dest + c * n_rows for c in range(ROW_PARTS)])
    y_tok = _sc_gather(y_sorted.reshape(ROW_PARTS * n_rows, PART_WORDS), idx_parts)
    y_tok = y_tok.reshape(ROW_PARTS, TOP_K, T, PART_WORDS)
    out_p, out_s = _combine(lay, h2, x1, wts.T, mod, w_sh_gu.astype(BF16), w_sh_down.astype(BF16),
                            ln2_g, ln2_b, y_tok, alpha, tm_comb)
    return out_p, out_s, hf, hb


def kernel(x_prompt, x_sample, state_ssd_fwd, state_ssd_bwd, c, c_ctx, w_ada, b_ada, w_in, conv_w, conv_b, conv_ln_g, conv_ln_b, ssm_conv_w, ssm_conv_b, dt_bias, a_log, d_skip, ssm_norm_g, w_out, ln1_g, ln1_b, w_router, router_bias, w_exp_gu, w_exp_down, w_sh_gu, w_sh_down, ln2_g, ln2_b):
    depth = w_ada.shape[0]
    assert depth == 1, "the prompt and latent passes are fused per layer; one layer is supported"
    bp, lp_, _ = x_prompt.shape
    bd, ld, _ = x_sample.shape
    lay = _Layout(bp, lp_, bd, ld)
    alpha = (2.0 * depth) ** 0.25
    stacked = (w_ada, b_ada, w_in, conv_w, conv_b, conv_ln_g, conv_ln_b, ssm_conv_w, ssm_conv_b,
               dt_bias, a_log, d_skip, ssm_norm_g, w_out, ln1_g, ln1_b, w_router, router_bias,
               w_exp_gu, w_exp_down, w_sh_gu, w_sh_down, ln2_g, ln2_b)
    lp = [w[0] for w in stacked]
    cond8 = jnp.concatenate([c_ctx[None, :], c, jnp.zeros((8 - 1 - bd, D_MODEL), F32)], axis=0)
    pos = _grid_pos_embed(ld)
    sshape = (bd, N_HEADS, HEADDIM, D_STATE)
    out_p, out_s, hf, hb = _layer(lay, x_prompt.reshape(bp * lp_, D_MODEL), x_sample.reshape(bd * ld, D_MODEL),
                                  pos, cond8, state_ssd_fwd[:, 0].reshape(sshape),
                                  state_ssd_bwd[:, 0].reshape(sshape), lp, alpha)
    return (out_p.reshape(bp, lp_, D_MODEL), out_s.reshape(bd, ld, D_MODEL),
            hf[:, None], hb[:, None])
```

```python
import functools
import math

import numpy as np
import jax
import jax.numpy as jnp
from jax import lax
from jax.experimental import pallas as pl
from jax.experimental.pallas import tpu as pltpu
from jax.experimental.pallas import tpu_sc as plsc

F32 = jnp.float32
BF16 = jnp.bfloat16
I32 = jnp.int32
HI = lax.Precision.HIGHEST

D_MODEL = 1024
GRID_W = 64
D_CONV = 1024
CONV_K = 31
N_HEADS = 16
HEADDIM = 64
D_SSM = N_HEADS * HEADDIM
N_GROUPS = 4
HEADS_PER_GROUP = N_HEADS // N_GROUPS
D_STATE = 128
SSM_CONV_K = 4
CHUNK = 128
D_XBC = D_SSM + 2 * N_GROUPS * D_STATE
N_EXPERTS = 256
TOP_K = 8
N_EXPERT_GROUPS = 8
EXPERTS_PER_GROUP = N_EXPERTS // N_EXPERT_GROUPS
TOPK_GROUPS = 4
D_EXPERT = 256
D_SHARED = 256
ROUTED_SCALE = 2.5
LN_EPS = 1e-5

CONV_TILE = 256
HALO = 16
EXPERT_BLOCK = 256
ROW_PARTS = 2
PART_WORDS = D_MODEL // 2 // ROW_PARTS
VMEM_LIMIT = 56 * 1024 * 1024


def _cparams(sem, vmem=VMEM_LIMIT):
    return pltpu.CompilerParams(dimension_semantics=sem, vmem_limit_bytes=vmem)


def _silu(x):
    return x * jax.nn.sigmoid(x)


def _ln_rows(x):
    mu = jnp.mean(x, axis=-1, keepdims=True)
    xc = x - mu
    var = jnp.mean(xc * xc, axis=-1, keepdims=True)
    return xc * lax.rsqrt(var + LN_EPS)


def _iota(shape, dim):
    return lax.broadcasted_iota(I32, shape, dim)


def _expand_matrix(n_in, width):
    return (_iota((n_in, n_in * width), 0) == _iota((n_in, n_in * width), 1) // width).astype(F32)


def _dot_hi(a, b):
    return jnp.dot(a, b, precision=HI, preferred_element_type=F32)


def _split3(x):
    hi = x.astype(BF16)
    r1 = x - hi.astype(F32)
    mid = r1.astype(BF16)
    lo = (r1 - mid.astype(F32)).astype(BF16)
    return jnp.concatenate([hi, mid, lo], axis=1)


def _expand3(n, width):
    rows = np.arange(3 * n)[:, None] % n
    cols = np.arange(n * width)[None, :] // width
    return jnp.asarray(rows == cols, dtype=BF16)


def _expand_exact(x, e3):
    return jnp.dot(_split3(x), e3, preferred_element_type=F32)


def _ada_kernel(c_ref, w_ref, b_ref, o_ref):
    o_ref[...] = _dot_hi(_silu(c_ref[...]), w_ref[...]) + b_ref[...]


def _ada(cond8, w_ada, b_ada):
    n = w_ada.shape[1]
    tn = 1024
    return pl.pallas_call(
        _ada_kernel,
        grid=(n // tn,),
        in_specs=[pl.BlockSpec((8, D_MODEL), lambda j: (0, 0)),
                  pl.BlockSpec((D_MODEL, tn), lambda j: (0, j)),
                  pl.BlockSpec((1, tn), lambda j: (0, j))],
        out_specs=pl.BlockSpec((8, tn), lambda j: (0, j)),
        out_shape=jax.ShapeDtypeStruct((8, n), F32),
        compiler_params=_cparams(("arbitrary",)),
        name="ada",
    )(cond8, w_ada, b_ada.reshape(1, n))


def _inproj_kernel(row_ref, posb_ref, flag_ref, xp_ref, xs_ref, pos_ref, mod_ref, wm_ref, wdt_ref,
                   glu_ref, z_ref, xbc_ref, dt_ref):
    i = pl.program_id(0)
    x = jnp.where(flag_ref[i] == 1, xs_ref[...] + pos_ref[...], xp_ref[...])
    r = row_ref[i]
    sh1 = mod_ref[pl.ds(r, 1), 0:D_MODEL]
    sc1 = mod_ref[pl.ds(r, 1), D_MODEL:2 * D_MODEL]
    h = (_ln_rows(x) * (1.0 + sc1) + sh1).astype(BF16)
    glu_ref[...] = jnp.dot(h, wm_ref[:, 0:2 * D_CONV], preferred_element_type=F32).astype(BF16)
    z_ref[...] = jnp.dot(h, wm_ref[:, 2 * D_CONV:2 * D_CONV + D_SSM], preferred_element_type=F32).astype(BF16)
    xbc_ref[...] = jnp.dot(h, wm_ref[:, 2 * D_CONV + D_SSM:], preferred_element_type=F32).astype(BF16)
    dt_ref[...] = jnp.dot(h, wdt_ref[...], preferred_element_type=F32)


def _inproj(lay, xp, xs, pos, mod, w_main, w_dt, tm):
    T = lay.n_tokens
    row, posb, flag = lay.token_tile_tables(tm)
    npt = lay.n_prompt_tokens // tm
    n_main = w_main.shape[1]
    gs = pltpu.PrefetchScalarGridSpec(
        num_scalar_prefetch=3,
        grid=(T // tm,),
        in_specs=[pl.BlockSpec((tm, D_MODEL), lambda i, r, p, f: (jnp.minimum(i, npt - 1), 0)),
                  pl.BlockSpec((tm, D_MODEL), lambda i, r, p, f: (jnp.maximum(i - npt, 0), 0)),
                  pl.BlockSpec((tm, D_MODEL), lambda i, r, p, f: (p[i], 0)),
                  pl.BlockSpec((8, 6 * D_MODEL), lambda i, r, p, f: (0, 0)),
                  pl.BlockSpec((D_MODEL, n_main), lambda i, r, p, f: (0, 0)),
                  pl.BlockSpec((D_MODEL, 128), lambda i, r, p, f: (0, 0))],
        out_specs=[pl.BlockSpec((tm, 2 * D_CONV), lambda i, r, p, f: (i, 0)),
                   pl.BlockSpec((tm, D_SSM), lambda i, r, p, f: (i, 0)),
                   pl.BlockSpec((tm, D_XBC), lambda i, r, p, f: (i, 0)),
                   pl.BlockSpec((tm, 128), lambda i, r, p, f: (i, 0))])
    return pl.pallas_call(
        _inproj_kernel, grid_spec=gs,
        out_shape=(jax.ShapeDtypeStruct((T, 2 * D_CONV), BF16),
                   jax.ShapeDtypeStruct((T, D_SSM), BF16),
                   jax.ShapeDtypeStruct((T, D_XBC), BF16),
                   jax.ShapeDtypeStruct((T, 128), F32)),
        compiler_params=_cparams(("arbitrary",)),
        name="inproj",
    )(row, posb, flag, xp, xs, pos, mod, w_main, w_dt)


_N_SHIFT = 8
_SHIFT_ROWS = CONV_TILE + 2 * HALO - _N_SHIFT
_ROW_BLOCK = 64
_FILL_ROWS = 32
_SSM_ROWS, _SSM_LANES = 64, 256


def _conv_kernel(lok_ref, rok_ref, glu_ref, glul_ref, glur_ref, xbc_ref, xbcl_ref, xbcr_ref,
                 cw_ref, cb_ref, lng_ref, lnb_ref, sw_ref, sb_ref, co_ref, xo_ref,
                 ext_ref, sh_ref, acc_ref, ext2_ref):
    i = pl.program_id(0)
    lok = lok_ref[i] == 1
    rok = rok_ref[i] == 1

    def glu(v):
        v = v.astype(F32)
        return v[:, 0:D_CONV] * jax.nn.sigmoid(v[:, D_CONV:])

    def fill_ext(rb, carry):
        r0 = pl.multiple_of(rb * _FILL_ROWS, _FILL_ROWS)
        dst = pl.ds(pl.multiple_of(HALO + r0, HALO), _FILL_ROWS)
        ext_ref[dst, :] = glu(glu_ref[pl.ds(r0, _FILL_ROWS), :])
        ext2_ref[dst, :] = xbc_ref[pl.ds(r0, _FILL_ROWS), :].astype(F32)
        return carry

    ext_ref[0:HALO, :] = jnp.where(lok, glu(glul_ref[...]), 0.0)
    ext_ref[HALO + CONV_TILE:, :] = jnp.where(rok, glu(glur_ref[...]), 0.0)
    ext2_ref[0:HALO, :] = jnp.where(lok, xbcl_ref[...].astype(F32), 0.0)
    ext2_ref[HALO + CONV_TILE:, :] = jnp.where(rok, xbcr_ref[...].astype(F32), 0.0)
    lax.fori_loop(0, CONV_TILE // _FILL_ROWS, fill_ext, 0)
    for r in range(_N_SHIFT):
        sh_ref[r] = ext_ref[r:r + _SHIFT_ROWS, :]

    first = HALO - (CONV_K - 1) // 2

    for j in range(D_CONV // 128):
        lanes = slice(j * 128, (j + 1) * 128)
        taps = [jnp.broadcast_to(cw_ref[k:k + 1, lanes], (8, 128)) for k in range(CONV_K)]
        bias = jnp.broadcast_to(cb_ref[:, lanes], (8, 128))

        def row_block(rb, carry, lanes=lanes, taps=taps, bias=bias):
            base = pl.multiple_of(rb * _ROW_BLOCK, _ROW_BLOCK)
            for sub in range(_ROW_BLOCK // 8):
                acc = bias
                for k in range(CONV_K):
                    o = first + k
                    row0 = base + (o // _N_SHIFT) * _N_SHIFT + sub * 8
                    acc = acc + sh_ref[o % _N_SHIFT, pl.ds(row0, 8), lanes] * taps[k]
                acc_ref[pl.ds(base + sub * 8, 8), lanes] = acc
            return carry

        lax.fori_loop(0, CONV_TILE // _ROW_BLOCK, row_block, 0)
    u = _ln_rows(acc_ref[...]) * lng_ref[...] + lnb_ref[...]
    co_ref[...] = _silu(u).astype(BF16)

    first2 = HALO - (SSM_CONV_K - 1) // 2
    for rb in range(CONV_TILE // _SSM_ROWS):
        for lc in range(D_XBC // _SSM_LANES):
            lanes = slice(lc * _SSM_LANES, (lc + 1) * _SSM_LANES)
            y = jnp.zeros((_SSM_ROWS, _SSM_LANES), F32) + sb_ref[:, lanes]
            for k in range(SSM_CONV_K):
                r0 = first2 + k + rb * _SSM_ROWS
                y = y + ext2_ref[r0:r0 + _SSM_ROWS, lanes] * sw_ref[k:k + 1, lanes]
            xo_ref[rb * _SSM_ROWS:(rb + 1) * _SSM_ROWS, lanes] = _silu(y).astype(BF16)


def _conv(lay, glu, xbc, conv_w, conv_b, ln_g, ln_b, ssm_w, ssm_b):
    T = lay.n_tokens
    lok, rok = lay.conv_tile_tables()
    n_tiles = T // CONV_TILE
    hb = CONV_TILE // HALO
    n_hb = T // HALO

    def cur(i, l, r):
        return (i, 0)

    def left(i, l, r):
        return (jnp.maximum(i * hb - 1, 0), 0)

    def right(i, l, r):
        return (jnp.minimum((i + 1) * hb, n_hb - 1), 0)

    def const(i, l, r):
        return (0, 0)

    gs = pltpu.PrefetchScalarGridSpec(
        num_scalar_prefetch=2,
        grid=(n_tiles,),
        in_specs=[pl.BlockSpec((CONV_TILE, 2 * D_CONV), cur),
                  pl.BlockSpec((HALO, 2 * D_CONV), left),
                  pl.BlockSpec((HALO, 2 * D_CONV), right),
                  pl.BlockSpec((CONV_TILE, D_XBC), cur),
                  pl.BlockSpec((HALO, D_XBC), left),
                  pl.BlockSpec((HALO, D_XBC), right),
                  pl.BlockSpec((CONV_K, D_CONV), const),
                  pl.BlockSpec((1, D_CONV), const),
                  pl.BlockSpec((1, D_CONV), const),
                  pl.BlockSpec((1, D_CONV), const),
                  pl.BlockSpec((SSM_CONV_K, D_XBC), const),
                  pl.BlockSpec((1, D_XBC), const)],
        out_specs=[pl.BlockSpec((CONV_TILE, D_CONV), cur),
                   pl.BlockSpec((CONV_TILE, D_XBC), cur)],
        scratch_shapes=[pltpu.VMEM((CONV_TILE + 2 * HALO, D_CONV), F32),
                        pltpu.VMEM((_N_SHIFT, _SHIFT_ROWS, D_CONV), F32),
                        pltpu.VMEM((CONV_TILE, D_CONV), F32),
                        pltpu.VMEM((CONV_TILE + 2 * HALO, D_XBC), F32)])
    return pl.pallas_call(
        _conv_kernel, grid_spec=gs,
        out_shape=(jax.ShapeDtypeStruct((T, D_CONV), BF16),
                   jax.ShapeDtypeStruct((T, D_XBC), BF16)),
        compiler_params=_cparams(("arbitrary",)),
        name="conv",
    )(lok, rok, glu, glu, glu, xbc, xbc, xbc, conv_w, conv_b.reshape(1, -1), ln_g.reshape(1, -1),
      ln_b.reshape(1, -1), ssm_w, ssm_b.reshape(1, -1))


_BN = N_GROUPS * D_STATE


def _ssd_kernel(chunk_ref, yidx_ref, phase_ref, first_ref, last_ref, zero_ref, sin_ref, sout_ref, cloc_ref,
                xbc_ref, z_ref, dt_ref, h0f_ref, h0b_ref, dtb_ref, alog_ref, dsk_ref, ng_ref,
                tri_ref, edec_ref, ewb_ref, ecol_ref, ewide_ref, eye3_ref,
                y_ref, hf_out_ref, hb_out_ref,
                hf_ref, g_ref, gin_ref, ybuf_ref):
    s = pl.program_id(0)
    phase = phase_ref[s]
    first = first_ref[s] == 1
    last = last_ref[s] == 1
    zero = zero_ref[s] == 1
    cloc = cloc_ref[s]
    H, P, N = N_HEADS, HEADDIM, D_STATE

    GW = HEADS_PER_GROUP * P
    xs = xbc_ref[:, 0:D_SSM]
    dt = dt_ref[:, 0:2 * H] + dtb_ref[...]
    dt = jnp.maximum(dt, 0.0) + jnp.log1p(jnp.exp(-jnp.abs(dt)))
    a = dt * (-jnp.exp(alog_ref[...]))
    a3 = jnp.dot(tri_ref[...], _split3(a), preferred_element_type=F32)
    acs = a3[:, 0:2 * H] + a3[:, 2 * H:4 * H] + a3[:, 4 * H:6 * H]
    tot = acs[CHUNK - 8:CHUNK, :]
    dec = _expand_exact(jnp.exp(tot), edec_ref[...])[7:8, :]
    exb = acs[:, H:2 * H] - a[:, H:2 * H]

    def load_state(src_ref, dst_ref):
        for j in range(H // 2):
            pair = jnp.concatenate([src_ref[0, 2 * j], src_ref[0, 2 * j + 1]], axis=0)
            dst_ref[:, 2 * j * P:(2 * j + 2) * P] = jnp.where(zero, 0.0, pair.T)

    def store_state(src_ref, dst_ref):
        for j in range(H // 2):
            pair = src_ref[:, 2 * j * P:(2 * j + 2) * P].T
            dst_ref[0, 2 * j] = pair[0:P]
            dst_ref[0, 2 * j + 1] = pair[P:2 * P]

    @pl.when(phase == 0)
    def _backward_states():
        @pl.when(first)
        def _():
            load_state(h0b_ref, g_ref)

        wb = dt[:, H:2 * H] * jnp.exp(exb)
        xw = (xs.astype(F32) * _expand_exact(wb, ewb_ref[...])).astype(BF16)
        for g in range(N_GROUPS):
            cols = slice(g * GW, (g + 1) * GW)
            bg = xbc_ref[:, D_SSM + g * N:D_SSM + (g + 1) * N]
            gg = g_ref[:, cols]
            gin_ref[cloc, :, cols] = gg.astype(BF16)
            upd = lax.dot_general(bg, xw[:, cols], (((0,), (0,)), ((), ())), preferred_element_type=F32)
            g_ref[:, cols] = gg * dec[:, D_SSM + g * GW:D_SSM + (g + 1) * GW] + upd

        @pl.when(last)
        def _():
            store_state(g_ref, hb_out_ref)

    @pl.when(phase == 1)
    def _forward_and_outputs():
        @pl.when(first)
        def _():
            load_state(h0f_ref, hf_ref)

        acsf = acs[:, 0:H]
        dtf = dt[:, 0:H]
        dtb = dt[:, H:2 * H]
        totf = acs[CHUNK - 1:CHUNK, 0:H]
        totb = acs[CHUNK - 1:CHUNK, H:2 * H]
        col = _expand_exact(jnp.concatenate([acsf, exb], axis=1), ecol_ref[...])
        q3 = _split3(jnp.concatenate([acsf, exb, dtf, dtb], axis=1))
        qt = lax.dot_general(eye3_ref[...], q3, (((1,), (1,)), ((), ())),
                             preferred_element_type=F32)
        wide = jnp.concatenate([dtf * jnp.exp(totf - acsf), jnp.exp(acsf), jnp.exp(totb - exb)], axis=1)
        wide = _expand_exact(wide, ewide_ref[...])
        xsf = xs.astype(F32)
        xw = (xsf * wide[:, 0:D_SSM]).astype(BF16)
        lower = _iota((CHUNK, CHUNK), 1) <= _iota((CHUNK, CHUNK), 0)
        upper = _iota((CHUNK, CHUNK), 1) >= _iota((CHUNK, CHUNK), 0)
        for g in range(N_GROUPS):
            cols = slice(g * GW, (g + 1) * GW)
            bg = xbc_ref[:, D_SSM + g * N:D_SSM + (g + 1) * N]
            cg = xbc_ref[:, D_SSM + _BN + g * N:D_SSM + _BN + (g + 1) * N]
            cb = lax.dot_general(cg, bg, (((1,), (1,)), ((), ())), preferred_element_type=F32)
            hfg = hf_ref[:, cols]
            yf = jnp.dot(cg, hfg.astype(BF16), preferred_element_type=F32)
            yb = jnp.dot(cg, gin_ref[cloc, :, cols], preferred_element_type=F32)
            ybuf_ref[:, cols] = yf * wide[:, D_SSM + g * GW:D_SSM + (g + 1) * GW] \
                + yb * wide[:, 2 * D_SSM + g * GW:2 * D_SSM + (g + 1) * GW]
            upd = lax.dot_general(bg, xw[:, cols], (((0,), (0,)), ((), ())), preferred_element_type=F32)
            hf_ref[:, cols] = hfg * dec[:, cols] + upd
            for r in range(HEADS_PER_GROUP):
                h = g * HEADS_PER_GROUP + r
                colf = col[:, h * N:(h + 1) * N]
                colb = col[:, (H + h) * N:(H + h + 1) * N]
                mf = jnp.where(lower, jnp.exp(colf - qt[h:h + 1, :]), 0.0) * qt[2 * H + h:2 * H + h + 1, :]
                mb = jnp.where(upper, jnp.exp(qt[H + h:H + h + 1, :] - colb), 0.0) * qt[3 * H + h:3 * H + h + 1, :]
                m = (cb * (mf + mb)).astype(BF16)
                hs = slice(h * P, (h + 1) * P)
                ybuf_ref[:, hs] += jnp.dot(m, xs[:, hs], preferred_element_type=F32)

        yt = (ybuf_ref[...] + dsk_ref[...] * xsf) * _silu(z_ref[...].astype(F32))
        gw = D_SSM // N_GROUPS
        for g in range(N_GROUPS):
            seg = yt[:, g * gw:(g + 1) * gw]
            ms = jnp.mean(seg * seg, axis=-1, keepdims=True)
            y_ref[:, g * gw:(g + 1) * gw] = (seg * lax.rsqrt(ms + LN_EPS) * ng_ref[:, g * gw:(g + 1) * gw]).astype(BF16)

        @pl.when(last)
        def _():
            store_state(hf_ref, hf_out_ref)


def _ssd(lay, xbc_c, z, dt_raw, h0f, h0b, dt_bias, a_log, d_skip, norm_g):
    T = lay.n_tokens
    tabs = lay.ssd_step_tables()
    n_steps = tabs[0].shape[0]
    nsp = len(tabs)

    def by_chunk(s, *t):
        return (t[0][s], 0)

    def by_y(s, *t):
        return (t[1][s], 0)

    def by_sin(s, *t):
        return (t[6][s], 0, 0, 0)

    def by_sout(s, *t):
        return (t[7][s], 0, 0, 0)

    def const(s, *t):
        return (0, 0)

    H = N_HEADS
    tri = jnp.asarray(np.tril(np.ones((CHUNK, CHUNK))), dtype=BF16)
    eye3 = jnp.asarray(np.arange(4 * H)[:, None] == np.arange(12 * H)[None, :] % (4 * H), dtype=BF16)
    consts = [tri, _expand3(2 * H, HEADDIM), _expand3(H, HEADDIM), _expand3(2 * H, D_STATE),
              _expand3(3 * H, HEADDIM), eye3]
    sshape = (1, N_HEADS, HEADDIM, D_STATE)
    gs = pltpu.PrefetchScalarGridSpec(
        num_scalar_prefetch=nsp,
        grid=(n_steps,),
        in_specs=[pl.BlockSpec((CHUNK, D_XBC), by_chunk),
                  pl.BlockSpec((CHUNK, D_SSM), by_chunk),
                  pl.BlockSpec((CHUNK, 128), by_chunk),
                  pl.BlockSpec(sshape, by_sin),
                  pl.BlockSpec(sshape, by_sin),
                  pl.BlockSpec((1, 2 * N_HEADS), const),
                  pl.BlockSpec((1, 2 * N_HEADS), const),
                  pl.BlockSpec((1, D_SSM), const),
                  pl.BlockSpec((1, D_SSM), const)] + [pl.BlockSpec(c.shape, const) for c in consts],
        out_specs=[pl.BlockSpec((CHUNK, D_SSM), by_y),
                   pl.BlockSpec(sshape, by_sout),
                   pl.BlockSpec(sshape, by_sout)],
        scratch_shapes=[pltpu.VMEM((D_STATE, D_SSM), F32),
                        pltpu.VMEM((D_STATE, D_SSM), F32),
                        pltpu.VMEM((lay.max_chunks, D_STATE, D_SSM), BF16),
                        pltpu.VMEM((CHUNK, D_SSM), F32)])
    n_out = lay.n_prompt_seqs
    return pl.pallas_call(
        _ssd_kernel, grid_spec=gs,
        out_shape=(jax.ShapeDtypeStruct((T, D_SSM), BF16),
                   jax.ShapeDtypeStruct((n_out,) + sshape[1:], F32),
                   jax.ShapeDtypeStruct((n_out,) + sshape[1:], F32)),
        compiler_params=_cparams(("arbitrary",)),
        name="ssd",
    )(*tabs, xbc_c, z, dt_raw, h0f, h0b, dt_bias.reshape(1, -1), a_log.reshape(1, -1),
      jnp.repeat(d_skip, HEADDIM).reshape(1, -1), norm_g.reshape(1, -1), *consts)


def _outproj_kernel(row_ref, posb_ref, flag_ref, xp_ref, xs_ref, pos_ref, mod_ref, co_ref, ys_ref, wo_ref,
                    g_ref, b_ref, x1_ref, h2_ref, h2p_ref, *, alpha):
    i = pl.program_id(0)
    x = jnp.where(flag_ref[i] == 1, xs_ref[...] + pos_ref[...], xp_ref[...])
    r = row_ref[i]
    g1 = mod_ref[pl.ds(r, 1), 2 * D_MODEL:3 * D_MODEL]
    sh2 = mod_ref[pl.ds(r, 1), 3 * D_MODEL:4 * D_MODEL]
    sc2 = mod_ref[pl.ds(r, 1), 4 * D_MODEL:5 * D_MODEL]
    mix = jnp.dot(co_ref[...], wo_ref[0:D_CONV, :], preferred_element_type=F32) \
        + jnp.dot(ys_ref[...], wo_ref[D_CONV:, :], preferred_element_type=F32)
    x1 = _ln_rows(alpha * x + g1 * mix) * g_ref[...] + b_ref[...]
    x1_ref[...] = x1
    h2 = _ln_rows(x1) * (1.0 + sc2) + sh2
    h2_ref[...] = h2
    packed = _pack_halves(h2)
    for c in range(ROW_PARTS):
        h2p_ref[c] = packed[:, c * PART_WORDS:(c + 1) * PART_WORDS]


def _outproj(lay, xp, xs, pos, mod, conv_out, y_ssm, w_out, ln_g, ln_b, alpha, tm):
    T = lay.n_tokens
    row, posb, flag = lay.token_tile_tables(tm)
    npt = lay.n_prompt_tokens // tm

    def const(i, r, p, f):
        return (0, 0)

    def cur(i, r, p, f):
        return (i, 0)

    gs = pltpu.PrefetchScalarGridSpec(
        num_scalar_prefetch=3,
        grid=(T // tm,),
        in_specs=[pl.BlockSpec((tm, D_MODEL), lambda i, r, p, f: (jnp.minimum(i, npt - 1), 0)),
                  pl.BlockSpec((tm, D_MODEL), lambda i, r, p, f: (jnp.maximum(i - npt, 0), 0)),
                  pl.BlockSpec((tm, D_MODEL), lambda i, r, p, f: (p[i], 0)),
                  pl.BlockSpec((8, 6 * D_MODEL), const),
                  pl.BlockSpec((tm, D_CONV), cur),
                  pl.BlockSpec((tm, D_SSM), cur),
                  pl.BlockSpec((D_CONV + D_SSM, D_MODEL), const),
                  pl.BlockSpec((1, D_MODEL), const),
                  pl.BlockSpec((1, D_MODEL), const)],
        out_specs=[pl.BlockSpec((tm, D_MODEL), cur),
                   pl.BlockSpec((tm, D_MODEL), cur),
                   pl.BlockSpec((ROW_PARTS, tm, PART_WORDS), lambda i, r, p, f: (0, i, 0))])
    return pl.pallas_call(
        functools.partial(_outproj_kernel, alpha=alpha), grid_spec=gs,
        out_shape=(jax.ShapeDtypeStruct((T, D_MODEL), F32),
                   jax.ShapeDtypeStruct((T, D_MODEL), F32),
                   jax.ShapeDtypeStruct((ROW_PARTS, T, PART_WORDS), jnp.uint32)),
        compiler_params=_cparams(("arbitrary",)),
        name="outproj",
    )(row, posb, flag, xp, xs, pos, mod, conv_out, y_ssm, w_out, ln_g.reshape(1, -1), ln_b.reshape(1, -1))


def _route_kernel(h2_ref, wrt_ref, bias_ref, idx_ref, wts_ref, pos_ref, cnt_ref, carry_ref, *, tm):
    i = pl.program_id(0)

    @pl.when(i == 0)
    def _():
        carry_ref[...] = jnp.zeros_like(carry_ref)

    E, NG, EG = N_EXPERTS, N_EXPERT_GROUPS, EXPERTS_PER_GROUP
    neg = -jnp.inf
    logits = lax.dot_general(wrt_ref[...], h2_ref[...], (((1,), (1,)), ((), ())), precision=HI,
                             preferred_element_type=F32)
    s = jax.nn.sigmoid(logits)
    sel = s + bias_ref[...]
    sel3 = sel.reshape(NG, EG, tm)
    io3 = _iota((NG, EG, tm), 1)
    m1 = jnp.max(sel3, axis=1, keepdims=True)
    f1 = jnp.min(jnp.where(sel3 == m1, io3, EG), axis=1, keepdims=True)
    m2 = jnp.max(jnp.where(io3 == f1, neg, sel3), axis=1, keepdims=True)
    gscore = (m1 + m2).reshape(NG, tm)
    gio = _iota((NG, tm), 0)
    beaten = jnp.zeros((NG, tm), I32)
    for g in range(NG):
        row = gscore[g:g + 1, :]
        beats = jnp.where(row > gscore, 1, jnp.where(row == gscore, jnp.where(g < gio, 1, 0), 0))
        beaten = beaten + beats
    keep = (beaten < TOPK_GROUPS).astype(F32).reshape(NG, 1, tm)
    selm = jnp.where(keep > 0.5, sel3, neg).reshape(E, tm)
    eio = _iota((E, tm), 0)
    chosen = jnp.zeros((E, tm), F32)
    idxs, ws = [], []
    for k in range(TOP_K):
        m = jnp.max(selm, axis=0, keepdims=True)
        am = jnp.minimum(jnp.min(jnp.where(selm == m, eio, E), axis=0, keepdims=True), E - 1)
        hit = eio == am
        ws.append(jnp.sum(jnp.where(hit, s, 0.0), axis=0, keepdims=True))
        idxs.append(am)
        selm = jnp.where(hit, neg, selm)
        chosen = jnp.where(hit, 1.0, chosen)
    wsum = ws[0]
    for k in range(1, TOP_K):
        wsum = wsum + ws[k]
    before = (_iota((tm, tm), 0) < _iota((tm, tm), 1)).astype(BF16)
    prior = jnp.dot(chosen.astype(BF16), before, preferred_element_type=F32)
    carry = carry_ref[...]
    prior = prior + jnp.concatenate([carry] * (tm // 128), axis=1)
    for k in range(TOP_K):
        idx_ref[k:k + 1, :] = idxs[k]
        wts_ref[k:k + 1, :] = ws[k] / wsum * ROUTED_SCALE
        pos_ref[k:k + 1, :] = jnp.sum(jnp.where(eio == idxs[k], prior, 0.0), axis=0, keepdims=True).astype(I32)
    total = jnp.dot(chosen.astype(BF16), jnp.ones((tm, 128), BF16), preferred_element_type=F32)
    carry = carry + total
    carry_ref[...] = carry
    cnt_ref[...] = carry.astype(I32)


def _route(h2, w_router_t, router_bias, tm):
    T = h2.shape[0]
    bias_b = jnp.broadcast_to(router_bias.astype(F32)[:, None], (N_EXPERTS, tm))
    return pl.pallas_call(
        functools.partial(_route_kernel, tm=tm),
        grid=(T // tm,),
        in_specs=[pl.BlockSpec((tm, D_MODEL), lambda i: (i, 0)),
                  pl.BlockSpec((N_EXPERTS, D_MODEL), lambda i: (0, 0)),
                  pl.BlockSpec((N_EXPERTS, tm), lambda i: (0, 0))],
        out_specs=[pl.BlockSpec((TOP_K, tm), lambda i: (0, i)),
                   pl.BlockSpec((TOP_K, tm), lambda i: (0, i)),
                   pl.BlockSpec((TOP_K, tm), lambda i: (0, i)),
                   pl.BlockSpec((N_EXPERTS, 128), lambda i: (0, 0))],
        out_shape=(jax.ShapeDtypeStruct((TOP_K, T), I32),
                   jax.ShapeDtypeStruct((TOP_K, T), F32),
                   jax.ShapeDtypeStruct((TOP_K, T), I32),
                   jax.ShapeDtypeStruct((N_EXPERTS, 128), I32)),
        scratch_shapes=[pltpu.VMEM((N_EXPERTS, 128), F32)],
        compiler_params=_cparams(("arbitrary",)),
        name="route",
    )(h2, w_router_t, bias_b)


def _dest_kernel(idx_ref, pos_ref, start_ref, dest_ref):
    tm = idx_ref.shape[1]
    eio = _iota((N_EXPERTS, tm), 0)
    start = start_ref[...]
    for k in range(TOP_K):
        base = jnp.sum(jnp.where(eio == idx_ref[k:k + 1, :], start, 0.0), axis=0, keepdims=True)
        dest_ref[k:k + 1, :] = base.astype(I32) + pos_ref[k:k + 1, :]


def _dest(idx, pos, pad_start, tm):
    T = idx.shape[1]
    start_b = jnp.broadcast_to(pad_start.astype(F32)[:, None], (N_EXPERTS, tm))
    return pl.pallas_call(
        _dest_kernel,
        grid=(T // tm,),
        in_specs=[pl.BlockSpec((TOP_K, tm), lambda i: (0, i)),
                  pl.BlockSpec((TOP_K, tm), lambda i: (0, i)),
                  pl.BlockSpec((N_EXPERTS, tm), lambda i: (0, 0))],
        out_specs=pl.BlockSpec((TOP_K, tm), lambda i: (0, i)),
        out_shape=jax.ShapeDtypeStruct((TOP_K, T), I32),
        compiler_params=_cparams(("arbitrary",)),
        name="dest",
    )(idx, pos, start_b)


_WEIGHT_FETCH_CHUNKS = 4
_W_SLOTS = 3
_X_SLOTS = 3


def _expert_kernel(base_ref, nblk_ref, nused_ref, x_hbm, wgu_hbm, wd_hbm, y_hbm,
                   wgu_bf, wd_bf, wgu_stage, wd_stage, xbuf, ybuf, sems, xsems, ysems):
    e = pl.program_id(0)
    n_used = nused_ref[0]

    def x_copy(b, slot):
        rows = pl.ds(pl.multiple_of(b * EXPERT_BLOCK, EXPERT_BLOCK), EXPERT_BLOCK)
        return pltpu.make_async_copy(x_hbm.at[:, rows, :], xbuf.at[slot], xsems.at[slot])

    def y_copy(b, slot):
        rows = pl.ds(pl.multiple_of(b * EXPERT_BLOCK, EXPERT_BLOCK), EXPERT_BLOCK)
        return pltpu.make_async_copy(ybuf.at[slot], y_hbm.at[:, rows, :], ysems.at[slot])

    def fetch(ex):
        ws = lax.rem(ex, _W_SLOTS)
        cps = []
        for c in range(_WEIGHT_FETCH_CHUNKS):
            rg = pl.ds(c * (D_MODEL // _WEIGHT_FETCH_CHUNKS), D_MODEL // _WEIGHT_FETCH_CHUNKS)
            rd = pl.ds(c * (D_EXPERT // _WEIGHT_FETCH_CHUNKS), D_EXPERT // _WEIGHT_FETCH_CHUNKS)
            cps.append(pltpu.make_async_copy(wgu_hbm.at[ex, rg], wgu_stage.at[ws, rg], sems.at[ws, 0]))
            cps.append(pltpu.make_async_copy(wd_hbm.at[ex, rd], wd_stage.at[ws, rd], sems.at[ws, 1]))
        return cps

    @pl.when(e == 0)
    def _():
        for ex in range(_W_SLOTS):
            for cp in fetch(ex):
                cp.start()
        for b in range(_X_SLOTS - 1):
            @pl.when(b < n_used)
            def _(b=b):
                x_copy(b, b).start()

    for cp in fetch(e):
        cp.wait()
    ws = lax.rem(e, _W_SLOTS)
    wgu_bf[...] = wgu_stage[ws].astype(BF16)
    wd_bf[...] = wd_stage[ws].astype(BF16)

    @pl.when(e + _W_SLOTS < N_EXPERTS)
    def _():
        for cp in fetch(e + _W_SLOTS):
            cp.start()

    def block(b, carry):
        xs = lax.rem(b, _X_SLOTS)
        slot = lax.rem(b, 2)
        x_copy(b, xs).wait()

        @pl.when(b + _X_SLOTS - 1 < n_used)
        def _():
            x_copy(b + _X_SLOTS - 1, lax.rem(b + _X_SLOTS - 1, _X_SLOTS)).start()

        @pl.when(b >= 2)
        def _():
            y_copy(b - 2, slot).wait()

        parts = [_unpack_halves(xbuf[xs, c]) for c in range(ROW_PARTS)]
        chunks = [p[0] for p in parts] + [p[1] for p in parts]
        gu = jnp.zeros((EXPERT_BLOCK, 2 * D_EXPERT), F32)
        for j, xc in enumerate(chunks):
            gu = gu + jnp.dot(xc.astype(BF16), wgu_bf[j * PART_WORDS:(j + 1) * PART_WORDS, :],
                              preferred_element_type=F32)
        act = (_silu(gu[:, 0:D_EXPERT]) * gu[:, D_EXPERT:]).astype(BF16)
        packed = _pack_halves(jnp.dot(act, wd_bf[...], preferred_element_type=F32))
        for c in range(ROW_PARTS):
            ybuf[slot, c] = packed[:, c * PART_WORDS:(c + 1) * PART_WORDS]
        y_copy(b, slot).start()
        return carry

    lax.fori_loop(base_ref[e], base_ref[e] + nblk_ref[e], block, 0)

    @pl.when(e == N_EXPERTS - 1)
    def _():
        @pl.when(n_used >= 2)
        def _():
            y_copy(n_used - 2, lax.rem(n_used, 2)).wait()

        @pl.when(n_used >= 1)
        def _():
            y_copy(n_used - 1, lax.rem(n_used - 1, 2)).wait()


def _pack_halves(x):
    n = x.shape[1] // 2
    hi = lax.bitcast_convert_type(x[:, :n].astype(BF16).astype(F32), jnp.uint32)
    lo = lax.bitcast_convert_type(x[:, n:].astype(BF16).astype(F32), jnp.uint32)
    return hi | (lo >> 16)


def _unpack_halves(p):
    hi = lax.bitcast_convert_type(p & jnp.uint32(0xFFFF0000), F32)
    lo = lax.bitcast_convert_type(p << 16, F32)
    return hi, lo


def _expert(x_sorted, w_gu, w_down, blk_base, blk_count, n_used):
    n_rows = x_sorted.shape[1]
    blk_shape = (ROW_PARTS, EXPERT_BLOCK, PART_WORDS)
    gs = pltpu.PrefetchScalarGridSpec(
        num_scalar_prefetch=3,
        grid=(N_EXPERTS,),
        in_specs=[pl.BlockSpec(memory_space=pl.ANY),
                  pl.BlockSpec(memory_space=pl.ANY),
                  pl.BlockSpec(memory_space=pl.ANY)],
        out_specs=pl.BlockSpec(memory_space=pl.ANY),
        scratch_shapes=[pltpu.VMEM((D_MODEL, 2 * D_EXPERT), BF16),
                        pltpu.VMEM((D_EXPERT, D_MODEL), BF16),
                        pltpu.VMEM((_W_SLOTS, D_MODEL, 2 * D_EXPERT), F32),
                        pltpu.VMEM((_W_SLOTS, D_EXPERT, D_MODEL), F32),
                        pltpu.VMEM((_X_SLOTS,) + blk_shape, jnp.uint32),
                        pltpu.VMEM((2,) + blk_shape, jnp.uint32),
                        pltpu.SemaphoreType.DMA((_W_SLOTS, 2)),
                        pltpu.SemaphoreType.DMA((_X_SLOTS,)),
                        pltpu.SemaphoreType.DMA((2,))])
    return pl.pallas_call(
        _expert_kernel, grid_spec=gs,
        out_shape=jax.ShapeDtypeStruct((ROW_PARTS, n_rows, PART_WORDS), jnp.uint32),
        compiler_params=_cparams(("arbitrary",)),
        name="expert",
    )(blk_base, blk_count, n_used, x_sorted, w_gu, w_down)


def _combine_kernel(row_ref, h2_ref, x1_ref, wt_ref, mod_ref, wsg_ref, wsd_ref, g_ref, b_ref, yt_ref,
                    op_ref, os_ref, *, n_prompt_tiles, alpha):
    i = pl.program_id(0)
    h2 = h2_ref[...].astype(BF16)
    su = jnp.dot(h2, wsg_ref[...], preferred_element_type=F32)
    act = (_silu(su[:, 0:D_SHARED]) * su[:, D_SHARED:]).astype(BF16)
    moe = jnp.dot(act, wsd_ref[...], preferred_element_type=F32)
    wt = wt_ref[...]
    his, los = [], []
    for c in range(ROW_PARTS):
        rh = jnp.zeros((h2.shape[0], PART_WORDS), F32)
        rl = jnp.zeros((h2.shape[0], PART_WORDS), F32)
        for k in range(TOP_K):
            hi, lo = _unpack_halves(yt_ref[c, k])
            w = wt[:, k:k + 1]
            rh = rh + hi * w
            rl = rl + lo * w
        his.append(rh)
        los.append(rl)
    moe = moe + jnp.concatenate(his + los, axis=1)
    g2 = mod_ref[pl.ds(row_ref[i], 1), 5 * D_MODEL:6 * D_MODEL]
    out = _ln_rows(alpha * x1_ref[...] + g2 * moe) * g_ref[...] + b_ref[...]

    @pl.when(i < n_prompt_tiles)
    def _():
        op_ref[...] = out

    @pl.when(i >= n_prompt_tiles)
    def _():
        os_ref[...] = out


def _combine(lay, h2, x1, wts_tok, mod, w_sh_gu, w_sh_down, ln_g, ln_b, y_tok, alpha, tm):
    T = lay.n_tokens
    row, _, _ = lay.token_tile_tables(tm)
    npt = lay.n_prompt_tokens // tm

    def cur(i, r):
        return (i, 0)

    def const(i, r):
        return (0, 0)

    gs = pltpu.PrefetchScalarGridSpec(
        num_scalar_prefetch=1,
        grid=(T // tm,),
        in_specs=[pl.BlockSpec((tm, D_MODEL), cur),
                  pl.BlockSpec((tm, D_MODEL), cur),
                  pl.BlockSpec((tm, TOP_K), cur),
                  pl.BlockSpec((8, 6 * D_MODEL), const),
                  pl.BlockSpec((D_MODEL, 2 * D_SHARED), const),
                  pl.BlockSpec((D_SHARED, D_MODEL), const),
                  pl.BlockSpec((1, D_MODEL), const),
                  pl.BlockSpec((1, D_MODEL), const),
                  pl.BlockSpec((ROW_PARTS, TOP_K, tm, PART_WORDS), lambda i, r: (0, 0, i, 0))],
        out_specs=[pl.BlockSpec((tm, D_MODEL), lambda i, r: (jnp.minimum(i, npt - 1), 0)),
                   pl.BlockSpec((tm, D_MODEL), lambda i, r: (jnp.maximum(i - npt, 0), 0))])
    return pl.pallas_call(
        functools.partial(_combine_kernel, n_prompt_tiles=npt, alpha=alpha), grid_spec=gs,
        out_shape=(jax.ShapeDtypeStruct((lay.n_prompt_tokens, D_MODEL), F32),
                   jax.ShapeDtypeStruct((T - lay.n_prompt_tokens, D_MODEL), F32)),
        compiler_params=_cparams(("arbitrary",)),
        name="combine",
    )(row, h2, x1, wts_tok, mod, w_sh_gu, w_sh_down, ln_g.reshape(1, -1), ln_b.reshape(1, -1), y_tok)


_GATHER_WINDOW = 128


def _sc_gather(table, idx):
    n, d = idx.shape[0], table.shape[1]
    mesh = plsc.VectorSubcoreMesh(core_axis_name="core", subcore_axis_name="subcore")

    @pl.kernel(out_type=jax.ShapeDtypeStruct((n, d), table.dtype), mesh=mesh)
    def gather_kernel(table_hbm, idx_hbm, out_hbm):
        def body(idx_vmem, out_vmem):
            pltpu.sync_copy(table_hbm.at[idx_vmem.at[0]], out_vmem)

        pltpu.emit_pipeline(
            body,
            grid=(n // _GATHER_WINDOW,),
            in_specs=[pl.BlockSpec((1, _GATHER_WINDOW), index_map=lambda i: (0, i))],
            out_specs=[pl.BlockSpec((_GATHER_WINDOW, d), index_map=lambda i: (i, 0))],
            core_axis_name=("core", "subcore"),
            dimension_semantics=(pltpu.PARALLEL,),
        )(idx_hbm, out_hbm)

    return gather_kernel(table, idx.reshape(1, n))


def _sc_scatter(rows, idx, n_out, repeat):
    n, d = rows.shape
    tiles = n // _GATHER_WINDOW
    mesh = plsc.VectorSubcoreMesh(core_axis_name="core", subcore_axis_name="subcore")

    @pl.kernel(out_type=jax.ShapeDtypeStruct((n_out, d), rows.dtype), mesh=mesh, scratch_types=[])
    def scatter_kernel(rows_hbm, idx_hbm, out_hbm):
        def body(rows_vmem, idx_vmem):
            pltpu.sync_copy(rows_vmem, out_hbm.at[idx_vmem.at[0]])

        pltpu.emit_pipeline(
            body,
            grid=(repeat * tiles,),
            in_specs=[pl.BlockSpec((_GATHER_WINDOW, d), index_map=lambda i: (i % tiles, 0)),
                      pl.BlockSpec((1, _GATHER_WINDOW), index_map=lambda i: (0, i))],
            out_specs=[],
            core_axis_name=("core", "subcore"),
            dimension_semantics=(pltpu.PARALLEL,),
        )(rows_hbm, idx_hbm)

    return scatter_kernel(rows, idx.reshape(1, repeat * n))


class _Layout:
    def __init__(self, n_prompt_seqs, prompt_len, n_sample_seqs, sample_len):
        self.n_prompt_seqs, self.prompt_len = n_prompt_seqs, prompt_len
        self.n_sample_seqs, self.sample_len = n_sample_seqs, sample_len
        self.n_prompt_tokens = n_prompt_seqs * prompt_len
        self.n_tokens = self.n_prompt_tokens + n_sample_seqs * sample_len
        assert prompt_len % CONV_TILE == 0 and sample_len % CONV_TILE == 0
        self.max_chunks = max(prompt_len, sample_len) // CHUNK

    def token_tile_tables(self, tm):
        assert self.n_prompt_tokens % tm == 0 and self.sample_len % tm == 0
        npt = self.n_prompt_tokens // tm
        per_seq = self.sample_len // tm
        n = self.n_tokens // tm
        row = np.zeros(n, np.int32)
        posb = np.zeros(n, np.int32)
        flag = np.zeros(n, np.int32)
        for i in range(npt, n):
            j = i - npt
            row[i] = 1 + j // per_seq
            posb[i] = j % per_seq
            flag[i] = 1
        return jnp.asarray(row), jnp.asarray(posb), jnp.asarray(flag)

    def conv_tile_tables(self):
        lok, rok = [], []
        for n_seq, length in ((self.n_prompt_seqs, self.prompt_len), (self.n_sample_seqs, self.sample_len)):
            per = length // CONV_TILE
            for _ in range(n_seq):
                for j in range(per):
                    lok.append(int(j > 0))
                    rok.append(int(j < per - 1))
        return jnp.asarray(np.array(lok, np.int32)), jnp.asarray(np.array(rok, np.int32))

    def ssd_step_tables(self):
        cols = [[] for _ in range(9)]
        seqs = []
        c0 = self.n_prompt_tokens // CHUNK
        for j in range(self.n_sample_seqs):
            nc = self.sample_len // CHUNK
            seqs.append((c0 + j * nc, nc, 0, j, 0))
        for j in range(self.n_prompt_seqs):
            nc = self.prompt_len // CHUNK
            seqs.append((j * nc, nc, 1, 0, j))
        for base, nc, zero, sin, sout in seqs:
            for phase in (0, 1):
                order = range(nc - 1, -1, -1) if phase == 0 else range(nc)
                for n, c in enumerate(order):
                    vals = (base + c, base if phase == 0 else base + c, phase, int(n == 0), int(n == nc - 1),
                            zero, sin, sout, c)
                    for col, v in zip(cols, vals):
                        col.append(v)
        return tuple(jnp.asarray(np.array(col, np.int32)) for col in cols)


def _grid_pos_embed(n_tokens):
    rows = n_tokens // GRID_W
    quarter = D_MODEL // 4
    freq = jnp.exp(-math.log(10000.0) * jnp.arange(quarter, dtype=F32) / quarter)
    r = jnp.broadcast_to(jnp.arange(rows, dtype=F32)[:, None, None] * freq, (rows, GRID_W, quarter))
    cl = jnp.broadcast_to(jnp.arange(GRID_W, dtype=F32)[None, :, None] * freq, (rows, GRID_W, quarter))
    emb = jnp.concatenate([jnp.sin(r), jnp.cos(r), jnp.sin(cl), jnp.cos(cl)], axis=-1)
    return emb.reshape(rows * GRID_W, D_MODEL)


def _moe_plan(counts):
    blk_count = (counts + EXPERT_BLOCK - 1) // EXPERT_BLOCK
    blk_end = jnp.cumsum(blk_count)
    blk_base = blk_end - blk_count
    return ((blk_base * EXPERT_BLOCK).astype(I32), blk_base.astype(I32), blk_count.astype(I32),
            blk_end[-1:].astype(I32))


def _layer(lay, xp, xs, pos, cond8, h0f, h0b, lp, alpha, tm_proj=512, tm_route=256, tm_comb=256):
    (w_ada, b_ada, w_in, conv_w, conv_b, conv_ln_g, conv_ln_b, ssm_conv_w, ssm_conv_b, dt_bias, a_log,
     d_skip, ssm_norm_g, w_out, ln1_g, ln1_b, w_router, router_bias, w_exp_gu, w_exp_down, w_sh_gu,
     w_sh_down, ln2_g, ln2_b) = lp
    T = lay.n_tokens
    n_main = 2 * D_CONV + D_SSM + D_XBC
    w_main = w_in[:, :n_main].astype(BF16)
    w_dt = jnp.pad(w_in[:, n_main:], ((0, 0), (0, 128 - 2 * N_HEADS))).astype(BF16)

    mod = _ada(cond8, w_ada, b_ada)
    glu, z, xbc, dt_raw = _inproj(lay, xp, xs, pos, mod, w_main, w_dt, tm_proj)
    conv_out, xbc_c = _conv(lay, glu, xbc, conv_w, conv_b, conv_ln_g, conv_ln_b, ssm_conv_w, ssm_conv_b)
    y_ssm, hf, hb = _ssd(lay, xbc_c, z, dt_raw, h0f, h0b, dt_bias, a_log, d_skip, ssm_norm_g)
    x1, h2, h2p = _outproj(lay, xp, xs, pos, mod, conv_out, y_ssm, w_out.astype(BF16), ln1_g, ln1_b, alpha, tm_proj)

    idx, wts, posn, cnt = _route(h2, w_router.T, router_bias, tm_route)
    n_blocks = -(-T * TOP_K // EXPERT_BLOCK) + N_EXPERTS
    pad_start, blk_base, blk_count, n_used = _moe_plan(cnt[:, 0])
    dest2 = _dest(idx, posn, pad_start, 512)
    n_rows = n_blocks * EXPERT_BLOCK
    scatter_idx = jnp.concatenate([dest2 + c * n_rows for c in range(ROW_PARTS)], axis=1)
    x_sorted = _sc_scatter(h2p.reshape(ROW_PARTS * T, PART_WORDS), scatter_idx.reshape(-1),
                           ROW_PARTS * n_rows, TOP_K).reshape(ROW_PARTS, n_rows, PART_WORDS)
    y_sorted = _expert(x_sorted, w_exp_gu, w_exp_down, blk_base, blk_count, n_used)
    dest = dest2.reshape(-1)
    idx_parts = jnp.concatenate([dest + c * n_rows for c in range(ROW_PARTS)])
    y_tok = _sc_gather(y_sorted.reshape(ROW_PARTS * n_rows, PART_WORDS), idx_parts)
    y_tok = y_tok.reshape(ROW_PARTS, TOP_K, T, PART_WORDS)
    out_p, out_s = _combine(lay, h2, x1, wts.T, mod, w_sh_gu.astype(BF16), w_sh_down.astype(BF16),
                            ln2_g, ln2_b, y_tok, alpha, tm_comb)
    return out_p, out_s, hf, hb


def kernel(x_prompt, x_sample, state_ssd_fwd, state_ssd_bwd, c, c_ctx, w_ada, b_ada, w_in, conv_w, conv_b, conv_ln_g, conv_ln_b, ssm_conv_w, ssm_conv_b, dt_bias, a_log, d_skip, ssm_norm_g, w_out, ln1_g, ln1_b, w_router, router_bias, w_exp_gu, w_exp_down, w_sh_gu, w_sh_down, ln2_g, ln2_b):
    depth = w_ada.shape[0]
    assert depth == 1, "the prompt and latent passes are fused per layer; one layer is supported"
    bp, lp_, _ = x_prompt.shape
    bd, ld, _ = x_sample.shape
    lay = _Layout(bp, lp_, bd, ld)
    alpha = (2.0 * depth) ** 0.25
    stacked = (w_ada, b_ada, w_in, conv_w, conv_b, conv_ln_g, conv_ln_b, ssm_conv_w, ssm_conv_b,
               dt_bias, a_log, d_skip, ssm_norm_g, w_out, ln1_g, ln1_b, w_router, router_bias,
               w_exp_gu, w_exp_down, w_sh_gu, w_sh_down, ln2_g, ln2_b)
    lp = [w[0] for w in stacked]
    cond8 = jnp.concatenate([c_ctx[None, :], c, jnp.zeros((8 - 1 - bd, D_MODEL), F32)], axis=0)
    pos = _grid_pos_embed(ld)
    sshape = (bd, N_HEADS, HEADDIM, D_STATE)
    out_p, out_s, hf, hb = _layer(lay, x_prompt.reshape(bp * lp_, D_MODEL), x_sample.reshape(bd * ld, D_MODEL),
                                  pos, cond8, state_ssd_fwd[:, 0].reshape(sshape),
                                  state_ssd_bwd[:, 0].reshape(sshape), lp, alpha)
    return (out_p.reshape(bp, lp_, D_MODEL), out_s.reshape(bd, ld, D_MODEL),
            hf[:, None], hb[:, None])
```

```python
import functools
import math

import numpy as np
import jax
import jax.numpy as jnp
from jax import lax
from jax.experimental import pallas as pl
from jax.experimental.pallas import tpu as pltpu
from jax.experimental.pallas import tpu_sc as plsc

F32 = jnp.float32
BF16 = jnp.bfloat16
I32 = jnp.int32
HI = lax.Precision.HIGHEST

D_MODEL = 1024
GRID_W = 64
D_CONV = 1024
CONV_K = 31
N_HEADS = 16
HEADDIM = 64
D_SSM = N_HEADS * HEADDIM
N_GROUPS = 4
HEADS_PER_GROUP = N_HEADS // N_GROUPS
D_STATE = 128
SSM_CONV_K = 4
CHUNK = 128
D_XBC = D_SSM + 2 * N_GROUPS * D_STATE
N_EXPERTS = 256
TOP_K = 8
N_EXPERT_GROUPS = 8
EXPERTS_PER_GROUP = N_EXPERTS // N_EXPERT_GROUPS
TOPK_GROUPS = 4
D_EXPERT = 256
D_SHARED = 256
ROUTED_SCALE = 2.5
LN_EPS = 1e-5

CONV_TILE = 256
HALO = 16
EXPERT_BLOCK = 256
ROW_PARTS = 2
PART_WORDS = D_MODEL // 2 // ROW_PARTS
VMEM_LIMIT = 56 * 1024 * 1024


def _cparams(sem, vmem=VMEM_LIMIT):
    return pltpu.CompilerParams(dimension_semantics=sem, vmem_limit_bytes=vmem)


def _silu(x):
    return x * jax.nn.sigmoid(x)


def _ln_rows(x):
    mu = jnp.mean(x, axis=-1, keepdims=True)
    xc = x - mu
    var = jnp.mean(xc * xc, axis=-1, keepdims=True)
    return xc * lax.rsqrt(var + LN_EPS)


def _iota(shape, dim):
    return lax.broadcasted_iota(I32, shape, dim)


def _expand_matrix(n_in, width):
    return (_iota((n_in, n_in * width), 0) == _iota((n_in, n_in * width), 1) // width).astype(F32)


def _dot_hi(a, b):
    return jnp.dot(a, b, precision=HI, preferred_element_type=F32)


def _split3(x):
    hi = x.astype(BF16)
    r1 = x - hi.astype(F32)
    mid = r1.astype(BF16)
    lo = (r1 - mid.astype(F32)).astype(BF16)
    return jnp.concatenate([hi, mid, lo], axis=1)


def _expand3(n, width):
    rows = np.arange(3 * n)[:, None] % n
    cols = np.arange(n * width)[None, :] // width
    return jnp.asarray(rows == cols, dtype=BF16)


def _expand_exact(x, e3):
    return jnp.dot(_split3(x), e3, preferred_element_type=F32)


def _ada_kernel(c_ref, w_ref, b_ref, o_ref):
    o_ref[...] = _dot_hi(_silu(c_ref[...]), w_ref[...]) + b_ref[...]


def _ada(cond8, w_ada, b_ada):
    n = w_ada.shape[1]
    tn = 1024
    return pl.pallas_call(
        _ada_kernel,
        grid=(n // tn,),
        in_specs=[pl.BlockSpec((8, D_MODEL), lambda j: (0, 0)),
                  pl.BlockSpec((D_MODEL, tn), lambda j: (0, j)),
                  pl.BlockSpec((1, tn), lambda j: (0, j))],
        out_specs=pl.BlockSpec((8, tn), lambda j: (0, j)),
        out_shape=jax.ShapeDtypeStruct((8, n), F32),
        compiler_params=_cparams(("arbitrary",)),
        name="ada",
    )(cond8, w_ada, b_ada.reshape(1, n))


def _inproj_kernel(row_ref, posb_ref, flag_ref, xp_ref, xs_ref, pos_ref, mod_ref, wm_ref, wdt_ref,
                   glu_ref, z_ref, xbc_ref, dt_ref):
    i = pl.program_id(0)
    x = jnp.where(flag_ref[i] == 1, xs_ref[...] + pos_ref[...], xp_ref[...])
    r = row_ref[i]
    sh1 = mod_ref[pl.ds(r, 1), 0:D_MODEL]
    sc1 = mod_ref[pl.ds(r, 1), D_MODEL:2 * D_MODEL]
    h = (_ln_rows(x) * (1.0 + sc1) + sh1).astype(BF16)
    glu_ref[...] = jnp.dot(h, wm_ref[:, 0:2 * D_CONV], preferred_element_type=F32).astype(BF16)
    z_ref[...] = jnp.dot(h, wm_ref[:, 2 * D_CONV:2 * D_CONV + D_SSM], preferred_element_type=F32).astype(BF16)
    xbc_ref[...] = jnp.dot(h, wm_ref[:, 2 * D_CONV + D_SSM:], preferred_element_type=F32).astype(BF16)
    dt_ref[...] = jnp.dot(h, wdt_ref[...], preferred_element_type=F32)


def _inproj(lay, xp, xs, pos, mod, w_main, w_dt, tm):
    T = lay.n_tokens
    row, posb, flag = lay.token_tile_tables(tm)
    npt = lay.n_prompt_tokens // tm
    n_main = w_main.shape[1]
    gs = pltpu.PrefetchScalarGridSpec(
        num_scalar_prefetch=3,
        grid=(T // tm,),
        in_specs=[pl.BlockSpec((tm, D_MODEL), lambda i, r, p, f: (jnp.minimum(i, npt - 1), 0)),
                  pl.BlockSpec((tm, D_MODEL), lambda i, r, p, f: (jnp.maximum(i - npt, 0), 0)),
                  pl.BlockSpec((tm, D_MODEL), lambda i, r, p, f: (p[i], 0)),
                  pl.BlockSpec((8, 6 * D_MODEL), lambda i, r, p, f: (0, 0)),
                  pl.BlockSpec((D_MODEL, n_main), lambda i, r, p, f: (0, 0)),
                  pl.BlockSpec((D_MODEL, 128), lambda i, r, p, f: (0, 0))],
        out_specs=[pl.BlockSpec((tm, 2 * D_CONV), lambda i, r, p, f: (i, 0)),
                   pl.BlockSpec((tm, D_SSM), lambda i, r, p, f: (i, 0)),
                   pl.BlockSpec((tm, D_XBC), lambda i, r, p, f: (i, 0)),
                   pl.BlockSpec((tm, 128), lambda i, r, p, f: (i, 0))])
    return pl.pallas_call(
        _inproj_kernel, grid_spec=gs,
        out_shape=(jax.ShapeDtypeStruct((T, 2 * D_CONV), BF16),
                   jax.ShapeDtypeStruct((T, D_SSM), BF16),
                   jax.ShapeDtypeStruct((T, D_XBC), BF16),
                   jax.ShapeDtypeStruct((T, 128), F32)),
        compiler_params=_cparams(("arbitrary",)),
        name="inproj",
    )(row, posb, flag, xp, xs, pos, mod, w_main, w_dt)


_N_SHIFT = 8
_SHIFT_ROWS = CONV_TILE + 2 * HALO - _N_SHIFT
_ROW_BLOCK = 64
_FILL_ROWS = 32
_SSM_ROWS, _SSM_LANES = 64, 256


def _conv_kernel(lok_ref, rok_ref, glu_ref, glul_ref, glur_ref, xbc_ref, xbcl_ref, xbcr_ref,
                 cw_ref, cb_ref, lng_ref, lnb_ref, sw_ref, sb_ref, dt_ref, dtb_ref, alog_ref, tri_ref,
                 co_ref, xo_ref, pre_ref,
                 ext_ref, sh_ref, acc_ref, ext2_ref):
    i = pl.program_id(0)
    lok = lok_ref[i] == 1
    rok = rok_ref[i] == 1

    nh2 = 2 * N_HEADS
    dt = dt_ref[:, 0:nh2] + dtb_ref[...]
    dt = jnp.maximum(dt, 0.0) + jnp.log1p(jnp.exp(-jnp.abs(dt)))
    a = dt * (-jnp.exp(alog_ref[...]))
    a3 = jnp.dot(tri_ref[...], _split3(a), preferred_element_type=F32)
    acs = a3[:, 0:nh2] + a3[:, nh2:2 * nh2] + a3[:, 2 * nh2:3 * nh2]
    pre_ref[...] = jnp.concatenate([dt, acs, a, jnp.zeros((CONV_TILE, 128 - 3 * nh2), F32)], axis=1)

    def glu(v):
        v = v.astype(F32)
        return v[:, 0:D_CONV] * jax.nn.sigmoid(v[:, D_CONV:])

    def fill_ext(rb, carry):
        r0 = pl.multiple_of(rb * _FILL_ROWS, _FILL_ROWS)
        dst = pl.ds(pl.multiple_of(HALO + r0, HALO), _FILL_ROWS)
        ext_ref[dst, :] = glu(glu_ref[pl.ds(r0, _FILL_ROWS), :])
        ext2_ref[dst, :] = xbc_ref[pl.ds(r0, _FILL_ROWS), :].astype(F32)
        return carry

    ext_ref[0:HALO, :] = jnp.where(lok, glu(glul_ref[...]), 0.0)
    ext_ref[HALO + CONV_TILE:, :] = jnp.where(rok, glu(glur_ref[...]), 0.0)
    ext2_ref[0:HALO, :] = jnp.where(lok, xbcl_ref[...].astype(F32), 0.0)
    ext2_ref[HALO + CONV_TILE:, :] = jnp.where(rok, xbcr_ref[...].astype(F32), 0.0)
    lax.fori_loop(0, CONV_TILE // _FILL_ROWS, fill_ext, 0)
    for r in range(_N_SHIFT):
        sh_ref[r] = ext_ref[r:r + _SHIFT_ROWS, :]

    first = HALO - (CONV_K - 1) // 2

    for j in range(D_CONV // 128):
        lanes = slice(j * 128, (j + 1) * 128)
        taps = [jnp.broadcast_to(cw_ref[k:k + 1, lanes], (8, 128)) for k in range(CONV_K)]
        bias = jnp.broadcast_to(cb_ref[:, lanes], (8, 128))

        def row_block(rb, carry, lanes=lanes, taps=taps, bias=bias):
            base = pl.multiple_of(rb * _ROW_BLOCK, _ROW_BLOCK)
            for sub in range(_ROW_BLOCK // 8):
                acc = bias
                for k in range(CONV_K):
                    o = first + k
                    row0 = base + (o // _N_SHIFT) * _N_SHIFT + sub * 8
                    acc = acc + sh_ref[o % _N_SHIFT, pl.ds(row0, 8), lanes] * taps[k]
                acc_ref[pl.ds(base + sub * 8, 8), lanes] = acc
            return carry

        lax.fori_loop(0, CONV_TILE // _ROW_BLOCK, row_block, 0)
    u = _ln_rows(acc_ref[...]) * lng_ref[...] + lnb_ref[...]
    co_ref[...] = _silu(u).astype(BF16)

    first2 = HALO - (SSM_CONV_K - 1) // 2
    for rb in range(CONV_TILE // _SSM_ROWS):
        for lc in range(D_XBC // _SSM_LANES):
            lanes = slice(lc * _SSM_LANES, (lc + 1) * _SSM_LANES)
            y = jnp.zeros((_SSM_ROWS, _SSM_LANES), F32) + sb_ref[:, lanes]
            for k in range(SSM_CONV_K):
                r0 = first2 + k + rb * _SSM_ROWS
                y = y + ext2_ref[r0:r0 + _SSM_ROWS, lanes] * sw_ref[k:k + 1, lanes]
            xo_ref[rb * _SSM_ROWS:(rb + 1) * _SSM_ROWS, lanes] = _silu(y).astype(BF16)


def _conv(lay, glu, xbc, dt_raw, conv_w, conv_b, ln_g, ln_b, ssm_w, ssm_b, dt_bias, a_log):
    T = lay.n_tokens
    chunk_of = np.arange(CONV_TILE) // CHUNK
    tri = jnp.asarray((chunk_of[:, None] == chunk_of[None, :]) & np.tril(np.ones((CONV_TILE, CONV_TILE), bool)),
                      dtype=BF16)
    lok, rok = lay.conv_tile_tables()
    n_tiles = T // CONV_TILE
    hb = CONV_TILE // HALO
    n_hb = T // HALO

    def cur(i, l, r):
        return (i, 0)

    def left(i, l, r):
        return (jnp.maximum(i * hb - 1, 0), 0)

    def right(i, l, r):
        return (jnp.minimum((i + 1) * hb, n_hb - 1), 0)

    def const(i, l, r):
        return (0, 0)

    gs = pltpu.PrefetchScalarGridSpec(
        num_scalar_prefetch=2,
        grid=(n_tiles,),
        in_specs=[pl.BlockSpec((CONV_TILE, 2 * D_CONV), cur),
                  pl.BlockSpec((HALO, 2 * D_CONV), left),
                  pl.BlockSpec((HALO, 2 * D_CONV), right),
                  pl.BlockSpec((CONV_TILE, D_XBC), cur),
                  pl.BlockSpec((HALO, D_XBC), left),
                  pl.BlockSpec((HALO, D_XBC), right),
                  pl.BlockSpec((CONV_K, D_CONV), const),
                  pl.BlockSpec((1, D_CONV), const),
                  pl.BlockSpec((1, D_CONV), const),
                  pl.BlockSpec((1, D_CONV), const),
                  pl.BlockSpec((SSM_CONV_K, D_XBC), const),
                  pl.BlockSpec((1, D_XBC), const),
                  pl.BlockSpec((CONV_TILE, 128), cur),
                  pl.BlockSpec((1, 2 * N_HEADS), const),
                  pl.BlockSpec((1, 2 * N_HEADS), const),
                  pl.BlockSpec((CONV_TILE, CONV_TILE), const)],
        out_specs=[pl.BlockSpec((CONV_TILE, D_CONV), cur),
                   pl.BlockSpec((CONV_TILE, D_XBC), cur),
                   pl.BlockSpec((CONV_TILE, 128), cur)],
        scratch_shapes=[pltpu.VMEM((CONV_TILE + 2 * HALO, D_CONV), F32),
                        pltpu.VMEM((_N_SHIFT, _SHIFT_ROWS, D_CONV), F32),
                        pltpu.VMEM((CONV_TILE, D_CONV), F32),
                        pltpu.VMEM((CONV_TILE + 2 * HALO, D_XBC), F32)])
    return pl.pallas_call(
        _conv_kernel, grid_spec=gs,
        out_shape=(jax.ShapeDtypeStruct((T, D_CONV), BF16),
                   jax.ShapeDtypeStruct((T, D_XBC), BF16),
                   jax.ShapeDtypeStruct((T, 128), F32)),
        compiler_params=_cparams(("arbitrary",)),
        name="conv",
    )(lok, rok, glu, glu, glu, xbc, xbc, xbc, conv_w, conv_b.reshape(1, -1), ln_g.reshape(1, -1),
      ln_b.reshape(1, -1), ssm_w, ssm_b.reshape(1, -1), dt_raw, dt_bias.reshape(1, -1), a_log.reshape(1, -1), tri)


_BN = N_GROUPS * D_STATE


def _ssd_kernel(chunk_ref, yidx_ref, phase_ref, first_ref, last_ref, zero_ref, sin_ref, sout_ref, cloc_ref,
                xbc_ref, z_ref, pre_ref, h0f_ref, h0b_ref, dsk_ref, ng_ref,
                edec_ref, ewb_ref, ecol_ref, ewide_ref, eye3_ref,
                y_ref, hf_out_ref, hb_out_ref,
                hf_ref, g_ref, gin_ref, ybuf_ref):
    s = pl.program_id(0)
    phase = phase_ref[s]
    first = first_ref[s] == 1
    last = last_ref[s] == 1
    zero = zero_ref[s] == 1
    cloc = cloc_ref[s]
    H, P, N = N_HEADS, HEADDIM, D_STATE

    GW = HEADS_PER_GROUP * P
    xs = xbc_ref[:, 0:D_SSM]
    dt = pre_ref[:, 0:2 * H]
    acs = pre_ref[:, 2 * H:4 * H]
    a = pre_ref[:, 4 * H:6 * H]
    tot = acs[CHUNK - 8:CHUNK, :]
    dec = _expand_exact(jnp.exp(tot), edec_ref[...])[7:8, :]
    exb = acs[:, H:2 * H] - a[:, H:2 * H]

    def load_state(src_ref, dst_ref):
        for j in range(H // 2):
            pair = jnp.concatenate([src_ref[0, 2 * j], src_ref[0, 2 * j + 1]], axis=0)
            dst_ref[:, 2 * j * P:(2 * j + 2) * P] = jnp.where(zero, 0.0, pair.T)

    def store_state(src_ref, dst_ref):
        for j in range(H // 2):
            pair = src_ref[:, 2 * j * P:(2 * j + 2) * P].T
            dst_ref[0, 2 * j] = pair[0:P]
            dst_ref[0, 2 * j + 1] = pair[P:2 * P]

    @pl.when(phase == 0)
    def _backward_states():
        @pl.when(first)
        def _():
            load_state(h0b_ref, g_ref)

        wb = dt[:, H:2 * H] * jnp.exp(exb)
        xw = (xs.astype(F32) * _expand_exact(wb, ewb_ref[...])).astype(BF16)
        for g in range(N_GROUPS):
            cols = slice(g * GW, (g + 1) * GW)
            bg = xbc_ref[:, D_SSM + g * N:D_SSM + (g + 1) * N]
            gg = g_ref[:, cols]
            gin_ref[cloc, :, cols] = gg.astype(BF16)
            upd = lax.dot_general(bg, xw[:, cols], (((0,), (0,)), ((), ())), preferred_element_type=F32)
            g_ref[:, cols] = gg * dec[:, D_SSM + g * GW:D_SSM + (g + 1) * GW] + upd

        @pl.when(last)
        def _():
            store_state(g_ref, hb_out_ref)

    @pl.when(phase == 1)
    def _forward_and_outputs():
        @pl.when(first)
        def _():
            load_state(h0f_ref, hf_ref)

        acsf = acs[:, 0:H]
        dtf = dt[:, 0:H]
        dtb = dt[:, H:2 * H]
        totf = acs[CHUNK - 1:CHUNK, 0:H]
        totb = acs[CHUNK - 1:CHUNK, H:2 * H]
        col = _expand_exact(jnp.concatenate([acsf, exb], axis=1), ecol_ref[...])
        q3 = _split3(jnp.concatenate([acsf, exb, dtf, dtb], axis=1))
        qt = lax.dot_general(eye3_ref[...], q3, (((1,), (1,)), ((), ())),
                             preferred_element_type=F32)
        wide = jnp.concatenate([dtf * jnp.exp(totf - acsf), jnp.exp(acsf), jnp.exp(totb - exb)], axis=1)
        wide = _expand_exact(wide, ewide_ref[...])
        xsf = xs.astype(F32)
        xw = (xsf * wide[:, 0:D_SSM]).astype(BF16)
        lower = _iota((CHUNK, CHUNK), 1) <= _iota((CHUNK, CHUNK), 0)
        upper = _iota((CHUNK, CHUNK), 1) >= _iota((CHUNK, CHUNK), 0)
        for g in range(N_GROUPS):
            cols = slice(g * GW, (g + 1) * GW)
            bg = xbc_ref[:, D_SSM + g * N:D_SSM + (g + 1) * N]
            cg = xbc_ref[:, D_SSM + _BN + g * N:D_SSM + _BN + (g + 1) * N]
            cb = lax.dot_general(cg, bg, (((1,), (1,)), ((), ())), preferred_element_type=F32)
            hfg = hf_ref[:, cols]
            yf = jnp.dot(cg, hfg.astype(BF16), preferred_element_type=F32)
            yb = jnp.dot(cg, gin_ref[cloc, :, cols], preferred_element_type=F32)
            ybuf_ref[:, cols] = yf * wide[:, D_SSM + g * GW:D_SSM + (g + 1) * GW] \
                + yb * wide[:, 2 * D_SSM + g * GW:2 * D_SSM + (g + 1) * GW]
            upd = lax.dot_general(bg, xw[:, cols], (((0,), (0,)), ((), ())), preferred_element_type=F32)
            hf_ref[:, cols] = hfg * dec[:, cols] + upd
            for r in range(HEADS_PER_GROUP):
                h = g * HEADS_PER_GROUP + r
                colf = col[:, h * N:(h + 1) * N]
                colb = col[:, (H + h) * N:(H + h + 1) * N]
                mf = jnp.where(lower, jnp.exp(colf - qt[h:h + 1, :]), 0.0) * qt[2 * H + h:2 * H + h + 1, :]
                mb = jnp.where(upper, jnp.exp(qt[H + h:H + h + 1, :] - colb), 0.0) * qt[3 * H + h:3 * H + h + 1, :]
                m = (cb * (mf + mb)).astype(BF16)
                hs = slice(h * P, (h + 1) * P)
                ybuf_ref[:, hs] += jnp.dot(m, xs[:, hs], preferred_element_type=F32)

        yt = (ybuf_ref[...] + dsk_ref[...] * xsf) * _silu(z_ref[...].astype(F32))
        gw = D_SSM // N_GROUPS
        for g in range(N_GROUPS):
            seg = yt[:, g * gw:(g + 1) * gw]
            ms = jnp.mean(seg * seg, axis=-1, keepdims=True)
            y_ref[:, g * gw:(g + 1) * gw] = (seg * lax.rsqrt(ms + LN_EPS) * ng_ref[:, g * gw:(g + 1) * gw]).astype(BF16)

        @pl.when(last)
        def _():
            store_state(hf_ref, hf_out_ref)


def _ssd(lay, xbc_c, z, pre, h0f, h0b, d_skip, norm_g):
    T = lay.n_tokens
    tabs = lay.ssd_step_tables()
    n_steps = tabs[0].shape[0]
    nsp = len(tabs)

    def by_chunk(s, *t):
        return (t[0][s], 0)

    def by_y(s, *t):
        return (t[1][s], 0)

    def by_sin(s, *t):
        return (t[6][s], 0, 0, 0)

    def by_sout(s, *t):
        return (t[7][s], 0, 0, 0)

    def const(s, *t):
        return (0, 0)

    H = N_HEADS
    eye3 = jnp.asarray(np.arange(4 * H)[:, None] == np.arange(12 * H)[None, :] % (4 * H), dtype=BF16)
    consts = [_expand3(2 * H, HEADDIM), _expand3(H, HEADDIM), _expand3(2 * H, D_STATE),
              _expand3(3 * H, HEADDIM), eye3]
    sshape = (1, N_HEADS, HEADDIM, D_STATE)
    gs = pltpu.PrefetchScalarGridSpec(
        num_scalar_prefetch=nsp,
        grid=(n_steps,),
        in_specs=[pl.BlockSpec((CHUNK, D_XBC), by_chunk),
                  pl.BlockSpec((CHUNK, D_SSM), by_chunk),
                  pl.BlockSpec((CHUNK, 128), by_chunk),
                  pl.BlockSpec(sshape, by_sin),
                  pl.BlockSpec(sshape, by_sin),
                  pl.BlockSpec((1, D_SSM), const),
                  pl.BlockSpec((1, D_SSM), const)] + [pl.BlockSpec(c.shape, const) for c in consts],
        out_specs=[pl.BlockSpec((CHUNK, D_SSM), by_y),
                   pl.BlockSpec(sshape, by_sout),
                   pl.BlockSpec(sshape, by_sout)],
        scratch_shapes=[pltpu.VMEM((D_STATE, D_SSM), F32),
                        pltpu.VMEM((D_STATE, D_SSM), F32),
                        pltpu.VMEM((lay.max_chunks, D_STATE, D_SSM), BF16),
                        pltpu.VMEM((CHUNK, D_SSM), F32)])
    n_out = lay.n_prompt_seqs
    return pl.pallas_call(
        _ssd_kernel, grid_spec=gs,
        out_shape=(jax.ShapeDtypeStruct((T, D_SSM), BF16),
                   jax.ShapeDtypeStruct((n_out,) + sshape[1:], F32),
                   jax.ShapeDtypeStruct((n_out,) + sshape[1:], F32)),
        compiler_params=_cparams(("arbitrary",)),
        name="ssd",
    )(*tabs, xbc_c, z, pre, h0f, h0b,
      jnp.repeat(d_skip, HEADDIM).reshape(1, -1), norm_g.reshape(1, -1), *consts)


def _outproj_kernel(row_ref, posb_ref, flag_ref, xp_ref, xs_ref, pos_ref, mod_ref, co_ref, ys_ref, wo_ref,
                    g_ref, b_ref, x1_ref, h2_ref, h2p_ref, *, alpha):
    i = pl.program_id(0)
    x = jnp.where(flag_ref[i] == 1, xs_ref[...] + pos_ref[...], xp_ref[...])
    r = row_ref[i]
    g1 = mod_ref[pl.ds(r, 1), 2 * D_MODEL:3 * D_MODEL]
    sh2 = mod_ref[pl.ds(r, 1), 3 * D_MODEL:4 * D_MODEL]
    sc2 = mod_ref[pl.ds(r, 1), 4 * D_MODEL:5 * D_MODEL]
    mix = jnp.dot(co_ref[...], wo_ref[0:D_CONV, :], preferred_element_type=F32) \
        + jnp.dot(ys_ref[...], wo_ref[D_CONV:, :], preferred_element_type=F32)
    x1 = _ln_rows(alpha * x + g1 * mix) * g_ref[...] + b_ref[...]
    x1_ref[...] = x1
    h2 = _ln_rows(x1) * (1.0 + sc2) + sh2
    h2_ref[...] = h2
    packed = _pack_halves(h2)
    for c in range(ROW_PARTS):
        h2p_ref[c] = packed[:, c * PART_WORDS:(c + 1) * PART_WORDS]


def _outproj(lay, xp, xs, pos, mod, conv_out, y_ssm, w_out, ln_g, ln_b, alpha, tm):
    T = lay.n_tokens
    row, posb, flag = lay.token_tile_tables(tm)
    npt = lay.n_prompt_tokens // tm

    def const(i, r, p, f):
        return (0, 0)

    def cur(i, r, p, f):
        return (i, 0)

    gs = pltpu.PrefetchScalarGridSpec(
        num_scalar_prefetch=3,
        grid=(T // tm,),
        in_specs=[pl.BlockSpec((tm, D_MODEL), lambda i, r, p, f: (jnp.minimum(i, npt - 1), 0)),
                  pl.BlockSpec((tm, D_MODEL), lambda i, r, p, f: (jnp.maximum(i - npt, 0), 0)),
                  pl.BlockSpec((tm, D_MODEL), lambda i, r, p, f: (p[i], 0)),
                  pl.BlockSpec((8, 6 * D_MODEL), const),
                  pl.BlockSpec((tm, D_CONV), cur),
                  pl.BlockSpec((tm, D_SSM), cur),
                  pl.BlockSpec((D_CONV + D_SSM, D_MODEL), const),
                  pl.BlockSpec((1, D_MODEL), const),
                  pl.BlockSpec((1, D_MODEL), const)],
        out_specs=[pl.BlockSpec((tm, D_MODEL), cur),
                   pl.BlockSpec((tm, D_MODEL), cur),
                   pl.BlockSpec((ROW_PARTS, tm, PART_WORDS), lambda i, r, p, f: (0, i, 0))])
    return pl.pallas_call(
        functools.partial(_outproj_kernel, alpha=alpha), grid_spec=gs,
        out_shape=(jax.ShapeDtypeStruct((T, D_MODEL), F32),
                   jax.ShapeDtypeStruct((T, D_MODEL), F32),
                   jax.ShapeDtypeStruct((ROW_PARTS, T, PART_WORDS), jnp.uint32)),
        compiler_params=_cparams(("arbitrary",)),
        name="outproj",
    )(row, posb, flag, xp, xs, pos, mod, conv_out, y_ssm, w_out, ln_g.reshape(1, -1), ln_b.reshape(1, -1))


def _route_kernel(h2_ref, wrt_ref, bias_ref, idx_ref, wts_ref, pos_ref, cnt_ref, carry_ref, *, tm):
    i = pl.program_id(0)

    @pl.when(i == 0)
    def _():
        carry_ref[...] = jnp.zeros_like(carry_ref)

    E, NG, EG = N_EXPERTS, N_EXPERT_GROUPS, EXPERTS_PER_GROUP
    neg = -jnp.inf
    h = h2_ref[...]
    h_hi = h.astype(BF16)
    h_lo = (h - h_hi.astype(F32)).astype(BF16)
    h3 = jnp.concatenate([h_hi, h_lo, h_hi], axis=1)
    logits = lax.dot_general(wrt_ref[...], h3, (((1,), (1,)), ((), ())),
                             preferred_element_type=F32)
    s = jax.nn.sigmoid(logits)
    sel = s + bias_ref[...]
    sel3 = sel.reshape(NG, EG, tm)
    io3 = _iota((NG, EG, tm), 1)
    m1 = jnp.max(sel3, axis=1, keepdims=True)
    f1 = jnp.min(jnp.where(sel3 == m1, io3, EG), axis=1, keepdims=True)
    m2 = jnp.max(jnp.where(io3 == f1, neg, sel3), axis=1, keepdims=True)
    gscore = (m1 + m2).reshape(NG, tm)
    gio = _iota((NG, tm), 0)
    beaten = jnp.zeros((NG, tm), I32)
    for g in range(NG):
        row = gscore[g:g + 1, :]
        beats = jnp.where(row > gscore, 1, jnp.where(row == gscore, jnp.where(g < gio, 1, 0), 0))
        beaten = beaten + beats
    keep = (beaten < TOPK_GROUPS).astype(F32).reshape(NG, 1, tm)
    selm = jnp.where(keep > 0.5, sel3, neg).reshape(E, tm)
    eio = _iota((E, tm), 0)
    chosen = jnp.zeros((E, tm), F32)
    idxs, ws = [], []
    for k in range(TOP_K):
        m = jnp.max(selm, axis=0, keepdims=True)
        am = jnp.minimum(jnp.min(jnp.where(selm == m, eio, E), axis=0, keepdims=True), E - 1)
        hit = eio == am
        ws.append(jnp.sum(jnp.where(hit, s, 0.0), axis=0, keepdims=True))
        idxs.append(am)
        selm = jnp.where(hit, neg, selm)
        chosen = jnp.where(hit, 1.0, chosen)
    wsum = ws[0]
    for k in range(1, TOP_K):
        wsum = wsum + ws[k]
    before = (_iota((tm, tm), 0) < _iota((tm, tm), 1)).astype(BF16)
    prior = jnp.dot(chosen.astype(BF16), before, preferred_element_type=F32)
    carry = carry_ref[...]
    prior = prior + jnp.concatenate([carry] * (tm // 128), axis=1)
    for k in range(TOP_K):
        idx_ref[k:k + 1, :] = idxs[k]
        wts_ref[k:k + 1, :] = ws[k] / wsum * ROUTED_SCALE
        pos_ref[k:k + 1, :] = jnp.sum(jnp.where(eio == idxs[k], prior, 0.0), axis=0, keepdims=True).astype(I32)
    total = jnp.dot(chosen.astype(BF16), jnp.ones((tm, 128), BF16), preferred_element_type=F32)
    carry = carry + total
    carry_ref[...] = carry
    cnt_ref[...] = carry.astype(I32)


def _route(h2, w_router_t, router_bias, tm):
    T = h2.shape[0]
    bias_b = jnp.broadcast_to(router_bias.astype(F32)[:, None], (N_EXPERTS, tm))
    w_hi = w_router_t.astype(BF16)
    w_lo = (w_router_t - w_hi.astype(F32)).astype(BF16)
    w_router_t = jnp.concatenate([w_hi, w_hi, w_lo], axis=1)
    return pl.pallas_call(
        functools.partial(_route_kernel, tm=tm),
        grid=(T // tm,),
        in_specs=[pl.BlockSpec((tm, D_MODEL), lambda i: (i, 0)),
                  pl.BlockSpec((N_EXPERTS, 3 * D_MODEL), lambda i: (0, 0)),
                  pl.BlockSpec((N_EXPERTS, tm), lambda i: (0, 0))],
        out_specs=[pl.BlockSpec((TOP_K, tm), lambda i: (0, i)),
                   pl.BlockSpec((TOP_K, tm), lambda i: (0, i)),
                   pl.BlockSpec((TOP_K, tm), lambda i: (0, i)),
                   pl.BlockSpec((N_EXPERTS, 128), lambda i: (0, 0))],
        out_shape=(jax.ShapeDtypeStruct((TOP_K, T), I32),
                   jax.ShapeDtypeStruct((TOP_K, T), F32),
                   jax.ShapeDtypeStruct((TOP_K, T), I32),
                   jax.ShapeDtypeStruct((N_EXPERTS, 128), I32)),
        scratch_shapes=[pltpu.VMEM((N_EXPERTS, 128), F32)],
        compiler_params=_cparams(("arbitrary",)),
        name="route",
    )(h2, w_router_t, bias_b)


def _dest_kernel(idx_ref, pos_ref, start_ref, dest_ref):
    tm = idx_ref.shape[1]
    eio = _iota((N_EXPERTS, tm), 0)
    start = start_ref[...]
    for k in range(TOP_K):
        base = jnp.sum(jnp.where(eio == idx_ref[k:k + 1, :], start, 0.0), axis=0, keepdims=True)
        dest_ref[k:k + 1, :] = base.astype(I32) + pos_ref[k:k + 1, :]


def _dest(idx, pos, pad_start, tm):
    T = idx.shape[1]
    start_b = jnp.broadcast_to(pad_start.astype(F32)[:, None], (N_EXPERTS, tm))
    return pl.pallas_call(
        _dest_kernel,
        grid=(T // tm,),
        in_specs=[pl.BlockSpec((TOP_K, tm), lambda i: (0, i)),
                  pl.BlockSpec((TOP_K, tm), lambda i: (0, i)),
                  pl.BlockSpec((N_EXPERTS, tm), lambda i: (0, 0))],
        out_specs=pl.BlockSpec((TOP_K, tm), lambda i: (0, i)),
        out_shape=jax.ShapeDtypeStruct((TOP_K, T), I32),
        compiler_params=_cparams(("arbitrary",)),
        name="dest",
    )(idx, pos, start_b)


_WEIGHT_FETCH_CHUNKS = 4
_W_SLOTS = 5
_X_SLOTS = 4


def _expert_kernel(base_ref, nblk_ref, nused_ref, x_hbm, wgu_hbm, wd_hbm, y_hbm,
                   wgu_bf, wd_bf, wgu_stage, wd_stage, xbuf, ybuf, sems, xsems, ysems):
    e = pl.program_id(0)
    n_used = nused_ref[0]

    def x_copy(b, slot):
        rows = pl.ds(pl.multiple_of(b * EXPERT_BLOCK, EXPERT_BLOCK), EXPERT_BLOCK)
        return pltpu.make_async_copy(x_hbm.at[:, rows, :], xbuf.at[slot], xsems.at[slot])

    def y_copy(b, slot):
        rows = pl.ds(pl.multiple_of(b * EXPERT_BLOCK, EXPERT_BLOCK), EXPERT_BLOCK)
        return pltpu.make_async_copy(ybuf.at[slot], y_hbm.at[:, rows, :], ysems.at[slot])

    def fetch(ex):
        ws = lax.rem(ex, _W_SLOTS)
        cps = []
        for c in range(_WEIGHT_FETCH_CHUNKS):
            rg = pl.ds(c * (D_MODEL // _WEIGHT_FETCH_CHUNKS), D_MODEL // _WEIGHT_FETCH_CHUNKS)
            rd = pl.ds(c * (D_EXPERT // _WEIGHT_FETCH_CHUNKS), D_EXPERT // _WEIGHT_FETCH_CHUNKS)
            cps.append(pltpu.make_async_copy(wgu_hbm.at[ex, rg], wgu_stage.at[ws, rg], sems.at[ws, 0]))
            cps.append(pltpu.make_async_copy(wd_hbm.at[ex, rd], wd_stage.at[ws, rd], sems.at[ws, 1]))
        return cps

    @pl.when(e == 0)
    def _():
        for ex in range(_W_SLOTS):
            for cp in fetch(ex):
                cp.start()
        for b in range(_X_SLOTS - 1):
            @pl.when(b < n_used)
            def _(b=b):
                x_copy(b, b).start()

    for cp in fetch(e):
        cp.wait()
    ws = lax.rem(e, _W_SLOTS)
    wgu_bf[...] = wgu_stage[ws].astype(BF16)
    wd_bf[...] = wd_stage[ws].astype(BF16)

    @pl.when(e + _W_SLOTS < N_EXPERTS)
    def _():
        for cp in fetch(e + _W_SLOTS):
            cp.start()

    def block(b, carry):
        xs = lax.rem(b, _X_SLOTS)
        slot = lax.rem(b, 2)
        x_copy(b, xs).wait()

        @pl.when(b + _X_SLOTS - 1 < n_used)
        def _():
            x_copy(b + _X_SLOTS - 1, lax.rem(b + _X_SLOTS - 1, _X_SLOTS)).start()

        @pl.when(b >= 2)
        def _():
            y_copy(b - 2, slot).wait()

        parts = [_unpack_halves(xbuf[xs, c]) for c in range(ROW_PARTS)]
        chunks = [p[0] for p in parts] + [p[1] for p in parts]
        gu = jnp.zeros((EXPERT_BLOCK, 2 * D_EXPERT), F32)
        for j, xc in enumerate(chunks):
            gu = gu + jnp.dot(xc.astype(BF16), wgu_bf[j * PART_WORDS:(j + 1) * PART_WORDS, :],
                              preferred_element_type=F32)
        act = (_silu(gu[:, 0:D_EXPERT]) * gu[:, D_EXPERT:]).astype(BF16)
        packed = _pack_halves(jnp.dot(act, wd_bf[...], preferred_element_type=F32))
        for c in range(ROW_PARTS):
            ybuf[slot, c] = packed[:, c * PART_WORDS:(c + 1) * PART_WORDS]
        y_copy(b, slot).start()
        return carry

    lax.fori_loop(base_ref[e], base_ref[e] + nblk_ref[e], block, 0)

    @pl.when(e == N_EXPERTS - 1)
    def _():
        @pl.when(n_used >= 2)
        def _():
            y_copy(n_used - 2, lax.rem(n_used, 2)).wait()

        @pl.when(n_used >= 1)
        def _():
            y_copy(n_used - 1, lax.rem(n_used - 1, 2)).wait()


def _pack_halves(x):
    n = x.shape[1] // 2
    hi = lax.bitcast_convert_type(x[:, :n].astype(BF16).astype(F32), jnp.uint32)
    lo = lax.bitcast_convert_type(x[:, n:].astype(BF16).astype(F32), jnp.uint32)
    return hi | (lo >> 16)


def _unpack_halves(p):
    hi = lax.bitcast_convert_type(p & jnp.uint32(0xFFFF0000), F32)
    lo = lax.bitcast_convert_type(p << 16, F32)
    return hi, lo


def _expert(x_sorted, w_gu, w_down, blk_base, blk_count, n_used):
    n_rows = x_sorted.shape[1]
    blk_shape = (ROW_PARTS, EXPERT_BLOCK, PART_WORDS)
    gs = pltpu.PrefetchScalarGridSpec(
        num_scalar_prefetch=3,
        grid=(N_EXPERTS,),
        in_specs=[pl.BlockSpec(memory_space=pl.ANY),
                  pl.BlockSpec(memory_space=pl.ANY),
                  pl.BlockSpec(memory_space=pl.ANY)],
        out_specs=pl.BlockSpec(memory_space=pl.ANY),
        scratch_shapes=[pltpu.VMEM((D_MODEL, 2 * D_EXPERT), BF16),
                        pltpu.VMEM((D_EXPERT, D_MODEL), BF16),
                        pltpu.VMEM((_W_SLOTS, D_MODEL, 2 * D_EXPERT), F32),
                        pltpu.VMEM((_W_SLOTS, D_EXPERT, D_MODEL), F32),
                        pltpu.VMEM((_X_SLOTS,) + blk_shape, jnp.uint32),
                        pltpu.VMEM((2,) + blk_shape, jnp.uint32),
                        pltpu.SemaphoreType.DMA((_W_SLOTS, 2)),
                        pltpu.SemaphoreType.DMA((_X_SLOTS,)),
                        pltpu.SemaphoreType.DMA((2,))])
    return pl.pallas_call(
        _expert_kernel, grid_spec=gs,
        out_shape=jax.ShapeDtypeStruct((ROW_PARTS, n_rows, PART_WORDS), jnp.uint32),
        compiler_params=_cparams(("arbitrary",)),
        name="expert",
    )(blk_base, blk_count, n_used, x_sorted, w_gu, w_down)


def _combine_kernel(row_ref, h2_ref, x1_ref, wt_ref, mod_ref, wsg_ref, wsd_ref, g_ref, b_ref, yt_ref,
                    op_ref, os_ref, *, n_prompt_tiles, alpha):
    i = pl.program_id(0)
    h2 = h2_ref[...].astype(BF16)
    su = jnp.dot(h2, wsg_ref[...], preferred_element_type=F32)
    act = (_silu(su[:, 0:D_SHARED]) * su[:, D_SHARED:]).astype(BF16)
    moe = jnp.dot(act, wsd_ref[...], preferred_element_type=F32)
    wt = wt_ref[...]
    his, los = [], []
    for c in range(ROW_PARTS):
        rh = jnp.zeros((h2.shape[0], PART_WORDS), F32)
        rl = jnp.zeros((h2.shape[0], PART_WORDS), F32)
        for k in range(TOP_K):
            hi, lo = _unpack_halves(yt_ref[c, k])
            w = wt[:, k:k + 1]
            rh = rh + hi * w
            rl = rl + lo * w
        his.append(rh)
        los.append(rl)
    moe = moe + jnp.concatenate(his + los, axis=1)
    g2 = mod_ref[pl.ds(row_ref[i], 1), 5 * D_MODEL:6 * D_MODEL]
    out = _ln_rows(alpha * x1_ref[...] + g2 * moe) * g_ref[...] + b_ref[...]

    @pl.when(i < n_prompt_tiles)
    def _():
        op_ref[...] = out

    @pl.when(i >= n_prompt_tiles)
    def _():
        os_ref[...] = out


def _combine(lay, h2, x1, wts_tok, mod, w_sh_gu, w_sh_down, ln_g, ln_b, y_tok, alpha, tm):
    T = lay.n_tokens
    row, _, _ = lay.token_tile_tables(tm)
    npt = lay.n_prompt_tokens // tm

    def cur(i, r):
        return (i, 0)

    def const(i, r):
        return (0, 0)

    gs = pltpu.PrefetchScalarGridSpec(
        num_scalar_prefetch=1,
        grid=(T // tm,),
        in_specs=[pl.BlockSpec((tm, D_MODEL), cur),
                  pl.BlockSpec((tm, D_MODEL), cur),
                  pl.BlockSpec((tm, TOP_K), cur),
                  pl.BlockSpec((8, 6 * D_MODEL), const),
                  pl.BlockSpec((D_MODEL, 2 * D_SHARED), const),
                  pl.BlockSpec((D_SHARED, D_MODEL), const),
                  pl.BlockSpec((1, D_MODEL), const),
                  pl.BlockSpec((1, D_MODEL), const),
                  pl.BlockSpec((ROW_PARTS, TOP_K, tm, PART_WORDS), lambda i, r: (0, 0, i, 0))],
        out_specs=[pl.BlockSpec((tm, D_MODEL), lambda i, r: (jnp.minimum(i, npt - 1), 0)),
                   pl.BlockSpec((tm, D_MODEL), lambda i, r: (jnp.maximum(i - npt, 0), 0))])
    return pl.pallas_call(
        functools.partial(_combine_kernel, n_prompt_tiles=npt, alpha=alpha), grid_spec=gs,
        out_shape=(jax.ShapeDtypeStruct((lay.n_prompt_tokens, D_MODEL), F32),
                   jax.ShapeDtypeStruct((T - lay.n_prompt_tokens, D_MODEL), F32)),
        compiler_params=_cparams(("arbitrary",)),
        name="combine",
    )(row, h2, x1, wts_tok, mod, w_sh_gu, w_sh_down, ln_g.reshape(1, -1), ln_b.reshape(1, -1), y_tok)


_GATHER_WINDOW = 128


def _sc_gather(table, idx):
    n, d = idx.shape[0], table.shape[1]
    mesh = plsc.VectorSubcoreMesh(core_axis_name="core", subcore_axis_name="subcore")

    @pl.kernel(out_type=jax.ShapeDtypeStruct((n, d), table.dtype), mesh=mesh)
    def gather_kernel(table_hbm, idx_hbm, out_hbm):
        def body(idx_vmem, out_vmem):
            pltpu.sync_copy(table_hbm.at[idx_vmem.at[0]], out_vmem)

        pltpu.emit_pipeline(
            body,
            grid=(n // _GATHER_WINDOW,),
            in_specs=[pl.BlockSpec((1, _GATHER_WINDOW), index_map=lambda i: (0, i))],
            out_specs=[pl.BlockSpec((_GATHER_WINDOW, d), index_map=lambda i: (i, 0))],
            core_axis_name=("core", "subcore"),
            dimension_semantics=(pltpu.PARALLEL,),
        )(idx_hbm, out_hbm)

    return gather_kernel(table, idx.reshape(1, n))


def _sc_scatter(rows, idx, n_out, repeat):
    n, d = rows.shape
    tiles = n // _GATHER_WINDOW
    mesh = plsc.VectorSubcoreMesh(core_axis_name="core", subcore_axis_name="subcore")

    @pl.kernel(out_type=jax.ShapeDtypeStruct((n_out, d), rows.dtype), mesh=mesh, scratch_types=[])
    def scatter_kernel(rows_hbm, idx_hbm, out_hbm):
        def body(rows_vmem, idx_vmem):
            pltpu.sync_copy(rows_vmem, out_hbm.at[idx_vmem.at[0]])

        pltpu.emit_pipeline(
            body,
            grid=(repeat * tiles,),
            in_specs=[pl.BlockSpec((_GATHER_WINDOW, d), index_map=lambda i: (i % tiles, 0)),
                      pl.BlockSpec((1, _GATHER_WINDOW), index_map=lambda i: (0, i))],
            out_specs=[],
            core_axis_name=("core", "subcore"),
            dimension_semantics=(pltpu.PARALLEL,),
        )(rows_hbm, idx_hbm)

    return scatter_kernel(rows, idx.reshape(1, repeat * n))


class _Layout:
    def __init__(self, n_prompt_seqs, prompt_len, n_sample_seqs, sample_len):
        self.n_prompt_seqs, self.prompt_len = n_prompt_seqs, prompt_len
        self.n_sample_seqs, self.sample_len = n_sample_seqs, sample_len
        self.n_prompt_tokens = n_prompt_seqs * prompt_len
        self.n_tokens = self.n_prompt_tokens + n_sample_seqs * sample_len
        assert prompt_len % CONV_TILE == 0 and sample_len % CONV_TILE == 0
        self.max_chunks = max(prompt_len, sample_len) // CHUNK

    def token_tile_tables(self, tm):
        assert self.n_prompt_tokens % tm == 0 and self.sample_len % tm == 0
        npt = self.n_prompt_tokens // tm
        per_seq = self.sample_len // tm
        n = self.n_tokens // tm
        row = np.zeros(n, np.int32)
        posb = np.zeros(n, np.int32)
        flag = np.zeros(n, np.int32)
        for i in range(npt, n):
            j = i - npt
            row[i] = 1 + j // per_seq
            posb[i] = j % per_seq
            flag[i] = 1
        return jnp.asarray(row), jnp.asarray(posb), jnp.asarray(flag)

    def conv_tile_tables(self):
        lok, rok = [], []
        for n_seq, length in ((self.n_prompt_seqs, self.prompt_len), (self.n_sample_seqs, self.sample_len)):
            per = length // CONV_TILE
            for _ in range(n_seq):
                for j in range(per):
                    lok.append(int(j > 0))
                    rok.append(int(j < per - 1))
        return jnp.asarray(np.array(lok, np.int32)), jnp.asarray(np.array(rok, np.int32))

    def ssd_step_tables(self):
        cols = [[] for _ in range(9)]
        seqs = []
        c0 = self.n_prompt_tokens // CHUNK
        for j in range(self.n_sample_seqs):
            nc = self.sample_len // CHUNK
            seqs.append((c0 + j * nc, nc, 0, j, 0))
        for j in range(self.n_prompt_seqs):
            nc = self.prompt_len // CHUNK
            seqs.append((j * nc, nc, 1, 0, j))
        for base, nc, zero, sin, sout in seqs:
            for phase in (0, 1):
                order = range(nc - 1, -1, -1) if phase == 0 else range(nc)
                for n, c in enumerate(order):
                    vals = (base + c, base if phase == 0 else base + c, phase, int(n == 0), int(n == nc - 1),
                            zero, sin, sout, c)
                    for col, v in zip(cols, vals):
                        col.append(v)
        return tuple(jnp.asarray(np.array(col, np.int32)) for col in cols)


def _grid_pos_embed(n_tokens):
    rows = n_tokens // GRID_W
    quarter = D_MODEL // 4
    freq = jnp.exp(-math.log(10000.0) * jnp.arange(quarter, dtype=F32) / quarter)
    r = jnp.broadcast_to(jnp.arange(rows, dtype=F32)[:, None, None] * freq, (rows, GRID_W, quarter))
    cl = jnp.broadcast_to(jnp.arange(GRID_W, dtype=F32)[None, :, None] * freq, (rows, GRID_W, quarter))
    emb = jnp.concatenate([jnp.sin(r), jnp.cos(r), jnp.sin(cl), jnp.cos(cl)], axis=-1)
    return emb.reshape(rows * GRID_W, D_MODEL)


def _moe_plan(counts):
    blk_count = (counts + EXPERT_BLOCK - 1) // EXPERT_BLOCK
    blk_end = jnp.cumsum(blk_count)
    blk_base = blk_end - blk_count
    return ((blk_base * EXPERT_BLOCK).astype(I32), blk_base.astype(I32), blk_count.astype(I32),
            blk_end[-1:].astype(I32))


def _layer(lay, xp, xs, pos, cond8, h0f, h0b, lp, alpha, tm_proj=512, tm_route=256, tm_comb=256):
    (w_ada, b_ada, w_in, conv_w, conv_b, conv_ln_g, conv_ln_b, ssm_conv_w, ssm_conv_b, dt_bias, a_log,
     d_skip, ssm_norm_g, w_out, ln1_g, ln1_b, w_router, router_bias, w_exp_gu, w_exp_down, w_sh_gu,
     w_sh_down, ln2_g, ln2_b) = lp
    T = lay.n_tokens
    n_main = 2 * D_CONV + D_SSM + D_XBC
    w_main = w_in[:, :n_main].astype(BF16)
    w_dt = jnp.pad(w_in[:, n_main:], ((0, 0), (0, 128 - 2 * N_HEADS))).astype(BF16)

    mod = _ada(cond8, w_ada, b_ada)
    glu, z, xbc, dt_raw = _inproj(lay, xp, xs, pos, mod, w_main, w_dt, tm_proj)
    conv_out, xbc_c, pre = _conv(lay, glu, xbc, dt_raw, conv_w, conv_b, conv_ln_g, conv_ln_b, ssm_conv_w,
                                 ssm_conv_b, dt_bias, a_log)
    y_ssm, hf, hb = _ssd(lay, xbc_c, z, pre, h0f, h0b, d_skip, ssm_norm_g)
    x1, h2, h2p = _outproj(lay, xp, xs, pos, mod, conv_out, y_ssm, w_out.astype(BF16), ln1_g, ln1_b, alpha, tm_proj)

    idx, wts, posn, cnt = _route(h2, w_router.T, router_bias, tm_route)
    n_blocks = -(-T * TOP_K // EXPERT_BLOCK) + N_EXPERTS
    pad_start, blk_base, blk_count, n_used = _moe_plan(cnt[:, 0])
    dest2 = _dest(idx, posn, pad_start, 512)
    n_rows = n_blocks * EXPERT_BLOCK
    scatter_idx = jnp.concatenate([dest2 + c * n_rows for c in range(ROW_PARTS)], axis=1)
    x_sorted = _sc_scatter(h2p.reshape(ROW_PARTS * T, PART_WORDS), scatter_idx.reshape(-1),
                           ROW_PARTS * n_rows, TOP_K).reshape(ROW_PARTS, n_rows, PART_WORDS)
    y_sorted = _expert(x_sorted, w_exp_gu, w_exp_down, blk_base, blk_count, n_used)
    dest = dest2.reshape(-1)
    idx_parts = jnp.concatenate([dest + c * n_rows for c in range(ROW_PARTS)])
    y_tok = _sc_gather(y_sorted.reshape(ROW_PARTS * n_rows, PART_WORDS), idx_parts)
    y_tok = y_tok.reshape(ROW_PARTS, TOP_K, T, PART_WORDS)
    out_p, out_s = _combine(lay, h2, x1, wts.T, mod, w_sh_gu.astype(BF16), w_sh_down.astype(BF16),
                            ln2_g, ln2_b, y_tok, alpha, tm_comb)
    return out_p, out_s, hf, hb


def kernel(x_prompt, x_sample, state_ssd_fwd, state_ssd_bwd, c, c_ctx, w_ada, b_ada, w_in, conv_w, conv_b, conv_ln_g, conv_ln_b, ssm_conv_w, ssm_conv_b, dt_bias, a_log, d_skip, ssm_norm_g, w_out, ln1_g, ln1_b, w_router, router_bias, w_exp_gu, w_exp_down, w_sh_gu, w_sh_down, ln2_g, ln2_b):
    depth = w_ada.shape[0]
    assert depth == 1, "the prompt and latent passes are fused per layer; one layer is supported"
    bp, lp_, _ = x_prompt.shape
    bd, ld, _ = x_sample.shape
    lay = _Layout(bp, lp_, bd, ld)
    alpha = (2.0 * depth) ** 0.25
    stacked = (w_ada, b_ada, w_in, conv_w, conv_b, conv_ln_g, conv_ln_b, ssm_conv_w, ssm_conv_b,
               dt_bias, a_log, d_skip, ssm_norm_g, w_out, ln1_g, ln1_b, w_router, router_bias,
               w_exp_gu, w_exp_down, w_sh_gu, w_sh_down, ln2_g, ln2_b)
    lp = [w[0] for w in stacked]
    cond8 = jnp.concatenate([c_ctx[None, :], c, jnp.zeros((8 - 1 - bd, D_MODEL), F32)], axis=0)
    pos = _grid_pos_embed(ld)
    sshape = (bd, N_HEADS, HEADDIM, D_STATE)
    out_p, out_s, hf, hb = _layer(lay, x_prompt.reshape(bp * lp_, D_MODEL), x_sample.reshape(bd * ld, D_MODEL),
                                  pos, cond8, state_ssd_fwd[:, 0].reshape(sshape),
                                  state_ssd_bwd[:, 0].reshape(sshape), lp, alpha)
    return (out_p.reshape(bp, lp_, D_MODEL), out_s.reshape(bd, ld, D_MODEL),
            hf[:, None], hb[:, None])
```

```python
import functools
import math

import numpy as np
import jax
import jax.numpy as jnp
from jax import lax
from jax.experimental import pallas as pl
from jax.experimental.pallas import tpu as pltpu
from jax.experimental.pallas import tpu_sc as plsc

F32 = jnp.float32
BF16 = jnp.bfloat16
I32 = jnp.int32
HI = lax.Precision.HIGHEST

D_MODEL = 1024
GRID_W = 64
D_CONV = 1024
CONV_K = 31
N_HEADS = 16
HEADDIM = 64
D_SSM = N_HEADS * HEADDIM
N_GROUPS = 4
HEADS_PER_GROUP = N_HEADS // N_GROUPS
D_STATE = 128
SSM_CONV_K = 4
CHUNK = 128
D_XBC = D_SSM + 2 * N_GROUPS * D_STATE
N_EXPERTS = 256
TOP_K = 8
N_EXPERT_GROUPS = 8
EXPERTS_PER_GROUP = N_EXPERTS // N_EXPERT_GROUPS
TOPK_GROUPS = 4
D_EXPERT = 256
D_SHARED = 256
ROUTED_SCALE = 2.5
LN_EPS = 1e-5

CONV_TILE = 256
HALO = 16
EXPERT_BLOCK = 256
ROW_PARTS = 2
PART_WORDS = D_MODEL // 2 // ROW_PARTS
VMEM_LIMIT = 56 * 1024 * 1024


def _cparams(sem, vmem=VMEM_LIMIT):
    return pltpu.CompilerParams(dimension_semantics=sem, vmem_limit_bytes=vmem)


def _silu(x):
    return x * jax.nn.sigmoid(x)


def _ln_rows(x):
    mu = jnp.mean(x, axis=-1, keepdims=True)
    xc = x - mu
    var = jnp.mean(xc * xc, axis=-1, keepdims=True)
    return xc * lax.rsqrt(var + LN_EPS)


def _iota(shape, dim):
    return lax.broadcasted_iota(I32, shape, dim)


def _expand_matrix(n_in, width):
    return (_iota((n_in, n_in * width), 0) == _iota((n_in, n_in * width), 1) // width).astype(F32)


def _dot_hi(a, b):
    return jnp.dot(a, b, precision=HI, preferred_element_type=F32)


def _split3(x):
    hi = x.astype(BF16)
    r1 = x - hi.astype(F32)
    mid = r1.astype(BF16)
    lo = (r1 - mid.astype(F32)).astype(BF16)
    return jnp.concatenate([hi, mid, lo], axis=1)


def _expand3(n, width):
    rows = np.arange(3 * n)[:, None] % n
    cols = np.arange(n * width)[None, :] // width
    return jnp.asarray(rows == cols, dtype=BF16)


def _expand_exact(x, e3):
    return jnp.dot(_split3(x), e3, preferred_element_type=F32)


def _ada_kernel(c_ref, w_ref, b_ref, o_ref):
    o_ref[...] = _dot_hi(_silu(c_ref[...]), w_ref[...]) + b_ref[...]


def _ada(cond8, w_ada, b_ada):
    n = w_ada.shape[1]
    tn = 1024
    return pl.pallas_call(
        _ada_kernel,
        grid=(n // tn,),
        in_specs=[pl.BlockSpec((8, D_MODEL), lambda j: (0, 0)),
                  pl.BlockSpec((D_MODEL, tn), lambda j: (0, j)),
                  pl.BlockSpec((1, tn), lambda j: (0, j))],
        out_specs=pl.BlockSpec((8, tn), lambda j: (0, j)),
        out_shape=jax.ShapeDtypeStruct((8, n), F32),
        compiler_params=_cparams(("arbitrary",)),
        name="ada",
    )(cond8, w_ada, b_ada.reshape(1, n))


def _inproj_kernel(row_ref, posb_ref, flag_ref, xp_ref, xs_ref, pos_ref, mod_ref, wm_ref, wdt_ref,
                   glu_ref, z_ref, xbc_ref, dt_ref):
    i = pl.program_id(0)
    x = jnp.where(flag_ref[i] == 1, xs_ref[...] + pos_ref[...], xp_ref[...])
    r = row_ref[i]
    sh1 = mod_ref[pl.ds(r, 1), 0:D_MODEL]
    sc1 = mod_ref[pl.ds(r, 1), D_MODEL:2 * D_MODEL]
    h = (_ln_rows(x) * (1.0 + sc1) + sh1).astype(BF16)
    glu_ref[...] = jnp.dot(h, wm_ref[:, 0:2 * D_CONV], preferred_element_type=F32).astype(BF16)
    z_ref[...] = jnp.dot(h, wm_ref[:, 2 * D_CONV:2 * D_CONV + D_SSM], preferred_element_type=F32).astype(BF16)
    xbc_ref[...] = jnp.dot(h, wm_ref[:, 2 * D_CONV + D_SSM:], preferred_element_type=F32).astype(BF16)
    dt_ref[...] = jnp.dot(h, wdt_ref[...], preferred_element_type=F32)


def _inproj(lay, xp, xs, pos, mod, w_main, w_dt, tm):
    T = lay.n_tokens
    row, posb, flag = lay.token_tile_tables(tm)
    npt = lay.n_prompt_tokens // tm
    n_main = w_main.shape[1]
    gs = pltpu.PrefetchScalarGridSpec(
        num_scalar_prefetch=3,
        grid=(T // tm,),
        in_specs=[pl.BlockSpec((tm, D_MODEL), lambda i, r, p, f: (jnp.minimum(i, npt - 1), 0)),
                  pl.BlockSpec((tm, D_MODEL), lambda i, r, p, f: (jnp.maximum(i - npt, 0), 0)),
                  pl.BlockSpec((tm, D_MODEL), lambda i, r, p, f: (p[i], 0)),
                  pl.BlockSpec((8, 6 * D_MODEL), lambda i, r, p, f: (0, 0)),
                  pl.BlockSpec((D_MODEL, n_main), lambda i, r, p, f: (0, 0)),
                  pl.BlockSpec((D_MODEL, 128), lambda i, r, p, f: (0, 0))],
        out_specs=[pl.BlockSpec((tm, 2 * D_CONV), lambda i, r, p, f: (i, 0)),
                   pl.BlockSpec((tm, D_SSM), lambda i, r, p, f: (i, 0)),
                   pl.BlockSpec((tm, D_XBC), lambda i, r, p, f: (i, 0)),
                   pl.BlockSpec((tm, 128), lambda i, r, p, f: (i, 0))])
    return pl.pallas_call(
        _inproj_kernel, grid_spec=gs,
        out_shape=(jax.ShapeDtypeStruct((T, 2 * D_CONV), BF16),
                   jax.ShapeDtypeStruct((T, D_SSM), BF16),
                   jax.ShapeDtypeStruct((T, D_XBC), BF16),
                   jax.ShapeDtypeStruct((T, 128), F32)),
        compiler_params=_cparams(("arbitrary",)),
        name="inproj",
    )(row, posb, flag, xp, xs, pos, mod, w_main, w_dt)


_N_SHIFT = 8
_SHIFT_ROWS = CONV_TILE + 2 * HALO - _N_SHIFT
_ROW_BLOCK = 64
_FILL_ROWS = 32
_SSM_ROWS, _SSM_LANES = 64, 256


def _conv_kernel(lok_ref, rok_ref, glu_ref, glul_ref, glur_ref, xbc_ref, xbcl_ref, xbcr_ref,
                 cw_ref, cb_ref, lng_ref, lnb_ref, sw_ref, sb_ref, dt_ref, dtb_ref, alog_ref, tri_ref,
                 co_ref, xo_ref, pre_ref,
                 ext_ref, sh_ref, acc_ref, ext2_ref):
    i = pl.program_id(0)
    lok = lok_ref[i] == 1
    rok = rok_ref[i] == 1

    nh2 = 2 * N_HEADS
    dt = dt_ref[:, 0:nh2] + dtb_ref[...]
    dt = jnp.maximum(dt, 0.0) + jnp.log1p(jnp.exp(-jnp.abs(dt)))
    a = dt * (-jnp.exp(alog_ref[...]))
    a3 = jnp.dot(tri_ref[...], _split3(a), preferred_element_type=F32)
    acs = a3[:, 0:nh2] + a3[:, nh2:2 * nh2] + a3[:, 2 * nh2:3 * nh2]
    pre_ref[...] = jnp.concatenate([dt, acs, a, jnp.zeros((CONV_TILE, 128 - 3 * nh2), F32)], axis=1)

    def glu(v):
        v = v.astype(F32)
        return v[:, 0:D_CONV] * jax.nn.sigmoid(v[:, D_CONV:])

    def fill_ext(rb, carry):
        r0 = pl.multiple_of(rb * _FILL_ROWS, _FILL_ROWS)
        dst = pl.ds(pl.multiple_of(HALO + r0, HALO), _FILL_ROWS)
        ext_ref[dst, :] = glu(glu_ref[pl.ds(r0, _FILL_ROWS), :])
        ext2_ref[dst, :] = xbc_ref[pl.ds(r0, _FILL_ROWS), :].astype(F32)
        return carry

    ext_ref[0:HALO, :] = jnp.where(lok, glu(glul_ref[...]), 0.0)
    ext_ref[HALO + CONV_TILE:, :] = jnp.where(rok, glu(glur_ref[...]), 0.0)
    ext2_ref[0:HALO, :] = jnp.where(lok, xbcl_ref[...].astype(F32), 0.0)
    ext2_ref[HALO + CONV_TILE:, :] = jnp.where(rok, xbcr_ref[...].astype(F32), 0.0)
    lax.fori_loop(0, CONV_TILE // _FILL_ROWS, fill_ext, 0)
    for r in range(_N_SHIFT):
        sh_ref[r] = ext_ref[r:r + _SHIFT_ROWS, :]

    first = HALO - (CONV_K - 1) // 2

    for j in range(D_CONV // 128):
        lanes = slice(j * 128, (j + 1) * 128)
        taps = [jnp.broadcast_to(cw_ref[k:k + 1, lanes], (8, 128)) for k in range(CONV_K)]
        bias = jnp.broadcast_to(cb_ref[:, lanes], (8, 128))

        def row_block(rb, carry, lanes=lanes, taps=taps, bias=bias):
            base = pl.multiple_of(rb * _ROW_BLOCK, _ROW_BLOCK)
            for sub in range(_ROW_BLOCK // 8):
                acc = bias
                for k in range(CONV_K):
                    o = first + k
                    row0 = base + (o // _N_SHIFT) * _N_SHIFT + sub * 8
                    acc = acc + sh_ref[o % _N_SHIFT, pl.ds(row0, 8), lanes] * taps[k]
                acc_ref[pl.ds(base + sub * 8, 8), lanes] = acc
            return carry

        lax.fori_loop(0, CONV_TILE // _ROW_BLOCK, row_block, 0)
    u = _ln_rows(acc_ref[...]) * lng_ref[...] + lnb_ref[...]
    co_ref[...] = _silu(u).astype(BF16)

    first2 = HALO - (SSM_CONV_K - 1) // 2
    for rb in range(CONV_TILE // _SSM_ROWS):
        for lc in range(D_XBC // _SSM_LANES):
            lanes = slice(lc * _SSM_LANES, (lc + 1) * _SSM_LANES)
            y = jnp.zeros((_SSM_ROWS, _SSM_LANES), F32) + sb_ref[:, lanes]
            for k in range(SSM_CONV_K):
                r0 = first2 + k + rb * _SSM_ROWS
                y = y + ext2_ref[r0:r0 + _SSM_ROWS, lanes] * sw_ref[k:k + 1, lanes]
            xo_ref[rb * _SSM_ROWS:(rb + 1) * _SSM_ROWS, lanes] = _silu(y).astype(BF16)


def _conv(lay, glu, xbc, dt_raw, conv_w, conv_b, ln_g, ln_b, ssm_w, ssm_b, dt_bias, a_log):
    T = lay.n_tokens
    chunk_of = np.arange(CONV_TILE) // CHUNK
    tri = jnp.asarray((chunk_of[:, None] == chunk_of[None, :]) & np.tril(np.ones((CONV_TILE, CONV_TILE), bool)),
                      dtype=BF16)
    lok, rok = lay.conv_tile_tables()
    n_tiles = T // CONV_TILE
    hb = CONV_TILE // HALO
    n_hb = T // HALO

    def cur(i, l, r):
        return (i, 0)

    def left(i, l, r):
        return (jnp.maximum(i * hb - 1, 0), 0)

    def right(i, l, r):
        return (jnp.minimum((i + 1) * hb, n_hb - 1), 0)

    def const(i, l, r):
        return (0, 0)

    gs = pltpu.PrefetchScalarGridSpec(
        num_scalar_prefetch=2,
        grid=(n_tiles,),
        in_specs=[pl.BlockSpec((CONV_TILE, 2 * D_CONV), cur),
                  pl.BlockSpec((HALO, 2 * D_CONV), left),
                  pl.BlockSpec((HALO, 2 * D_CONV), right),
                  pl.BlockSpec((CONV_TILE, D_XBC), cur),
                  pl.BlockSpec((HALO, D_XBC), left),
                  pl.BlockSpec((HALO, D_XBC), right),
                  pl.BlockSpec((CONV_K, D_CONV), const),
                  pl.BlockSpec((1, D_CONV), const),
                  pl.BlockSpec((1, D_CONV), const),
                  pl.BlockSpec((1, D_CONV), const),
                  pl.BlockSpec((SSM_CONV_K, D_XBC), const),
                  pl.BlockSpec((1, D_XBC), const),
                  pl.BlockSpec((CONV_TILE, 128), cur),
                  pl.BlockSpec((1, 2 * N_HEADS), const),
                  pl.BlockSpec((1, 2 * N_HEADS), const),
                  pl.BlockSpec((CONV_TILE, CONV_TILE), const)],
        out_specs=[pl.BlockSpec((CONV_TILE, D_CONV), cur),
                   pl.BlockSpec((CONV_TILE, D_XBC), cur),
                   pl.BlockSpec((CONV_TILE, 128), cur)],
        scratch_shapes=[pltpu.VMEM((CONV_TILE + 2 * HALO, D_CONV), F32),
                        pltpu.VMEM((_N_SHIFT, _SHIFT_ROWS, D_CONV), F32),
                        pltpu.VMEM((CONV_TILE, D_CONV), F32),
                        pltpu.VMEM((CONV_TILE + 2 * HALO, D_XBC), F32)])
    return pl.pallas_call(
        _conv_kernel, grid_spec=gs,
        out_shape=(jax.ShapeDtypeStruct((T, D_CONV), BF16),
                   jax.ShapeDtypeStruct((T, D_XBC), BF16),
                   jax.ShapeDtypeStruct((T, 128), F32)),
        compiler_params=_cparams(("arbitrary",)),
        name="conv",
    )(lok, rok, glu, glu, glu, xbc, xbc, xbc, conv_w, conv_b.reshape(1, -1), ln_g.reshape(1, -1),
      ln_b.reshape(1, -1), ssm_w, ssm_b.reshape(1, -1), dt_raw, dt_bias.reshape(1, -1), a_log.reshape(1, -1), tri)


_BN = N_GROUPS * D_STATE


def _ssd_kernel(chunk_ref, yidx_ref, phase_ref, first_ref, last_ref, zero_ref, sin_ref, sout_ref, cloc_ref,
                xbc_ref, z_ref, pre_ref, h0f_ref, h0b_ref, dsk_ref, ng_ref,
                edec_ref, ewb_ref, ecol_ref, ewide_ref, eye3_ref,
                y_ref, hf_out_ref, hb_out_ref,
                hf_ref, g_ref, gin_ref, ybuf_ref):
    s = pl.program_id(0)
    phase = phase_ref[s]
    first = first_ref[s] == 1
    last = last_ref[s] == 1
    zero = zero_ref[s] == 1
    cloc = cloc_ref[s]
    H, P, N = N_HEADS, HEADDIM, D_STATE

    GW = HEADS_PER_GROUP * P
    xs = xbc_ref[:, 0:D_SSM]
    dt = pre_ref[:, 0:2 * H]
    acs = pre_ref[:, 2 * H:4 * H]
    a = pre_ref[:, 4 * H:6 * H]
    tot = acs[CHUNK - 8:CHUNK, :]
    dec = _expand_exact(jnp.exp(tot), edec_ref[...])[7:8, :]
    exb = acs[:, H:2 * H] - a[:, H:2 * H]

    def load_state(src_ref, dst_ref):
        for j in range(H // 2):
            pair = jnp.concatenate([src_ref[0, 2 * j], src_ref[0, 2 * j + 1]], axis=0)
            dst_ref[:, 2 * j * P:(2 * j + 2) * P] = jnp.where(zero, 0.0, pair.T)

    def store_state(src_ref, dst_ref):
        for j in range(H // 2):
            pair = src_ref[:, 2 * j * P:(2 * j + 2) * P].T
            dst_ref[0, 2 * j] = pair[0:P]
            dst_ref[0, 2 * j + 1] = pair[P:2 * P]

    @pl.when(phase == 0)
    def _backward_states():
        @pl.when(first)
        def _():
            load_state(h0b_ref, g_ref)

        wb = dt[:, H:2 * H] * jnp.exp(exb)
        xw = (xs.astype(F32) * _expand_exact(wb, ewb_ref[...])).astype(BF16)
        for g in range(N_GROUPS):
            cols = slice(g * GW, (g + 1) * GW)
            bg = xbc_ref[:, D_SSM + g * N:D_SSM + (g + 1) * N]
            gg = g_ref[:, cols]
            gin_ref[cloc, :, cols] = gg.astype(BF16)
            upd = lax.dot_general(bg, xw[:, cols], (((0,), (0,)), ((), ())), preferred_element_type=F32)
            g_ref[:, cols] = gg * dec[:, D_SSM + g * GW:D_SSM + (g + 1) * GW] + upd

        @pl.when(last)
        def _():
            store_state(g_ref, hb_out_ref)

    @pl.when(phase == 1)
    def _forward_and_outputs():
        @pl.when(first)
        def _():
            load_state(h0f_ref, hf_ref)

        acsf = acs[:, 0:H]
        dtf = dt[:, 0:H]
        dtb = dt[:, H:2 * H]
        totf = acs[CHUNK - 1:CHUNK, 0:H]
        totb = acs[CHUNK - 1:CHUNK, H:2 * H]
        col = _expand_exact(jnp.concatenate([acsf, exb], axis=1), ecol_ref[...])
        q3 = _split3(jnp.concatenate([acsf, exb, dtf, dtb], axis=1))
        qt = lax.dot_general(eye3_ref[...], q3, (((1,), (1,)), ((), ())),
                             preferred_element_type=F32)
        wide = jnp.concatenate([dtf * jnp.exp(totf - acsf), jnp.exp(acsf), jnp.exp(totb - exb)], axis=1)
        wide = _expand_exact(wide, ewide_ref[...])
        xsf = xs.astype(F32)
        xw = (xsf * wide[:, 0:D_SSM]).astype(BF16)
        lower = _iota((CHUNK, CHUNK), 1) <= _iota((CHUNK, CHUNK), 0)
        upper = _iota((CHUNK, CHUNK), 1) >= _iota((CHUNK, CHUNK), 0)
        for g in range(N_GROUPS):
            cols = slice(g * GW, (g + 1) * GW)
            bg = xbc_ref[:, D_SSM + g * N:D_SSM + (g + 1) * N]
            cg = xbc_ref[:, D_SSM + _BN + g * N:D_SSM + _BN + (g + 1) * N]
            cb = lax.dot_general(cg, bg, (((1,), (1,)), ((), ())), preferred_element_type=F32)
            hfg = hf_ref[:, cols]
            yf = jnp.dot(cg, hfg.astype(BF16), preferred_element_type=F32)
            yb = jnp.dot(cg, gin_ref[cloc, :, cols], preferred_element_type=F32)
            ybuf_ref[:, cols] = yf * wide[:, D_SSM + g * GW:D_SSM + (g + 1) * GW] \
                + yb * wide[:, 2 * D_SSM + g * GW:2 * D_SSM + (g + 1) * GW]
            upd = lax.dot_general(bg, xw[:, cols], (((0,), (0,)), ((), ())), preferred_element_type=F32)
            hf_ref[:, cols] = hfg * dec[:, cols] + upd
            for r in range(HEADS_PER_GROUP):
                h = g * HEADS_PER_GROUP + r
                colf = col[:, h * N:(h + 1) * N]
                colb = col[:, (H + h) * N:(H + h + 1) * N]
                mf = jnp.where(lower, jnp.exp(colf - qt[h:h + 1, :]), 0.0) * qt[2 * H + h:2 * H + h + 1, :]
                mb = jnp.where(upper, jnp.exp(qt[H + h:H + h + 1, :] - colb), 0.0) * qt[3 * H + h:3 * H + h + 1, :]
                m = (cb * (mf + mb)).astype(BF16)
                hs = slice(h * P, (h + 1) * P)
                ybuf_ref[:, hs] += jnp.dot(m, xs[:, hs], preferred_element_type=F32)

        yt = (ybuf_ref[...] + dsk_ref[...] * xsf) * _silu(z_ref[...].astype(F32))
        gw = D_SSM // N_GROUPS
        for g in range(N_GROUPS):
            seg = yt[:, g * gw:(g + 1) * gw]
            ms = jnp.mean(seg * seg, axis=-1, keepdims=True)
            y_ref[:, g * gw:(g + 1) * gw] = (seg * lax.rsqrt(ms + LN_EPS) * ng_ref[:, g * gw:(g + 1) * gw]).astype(BF16)

        @pl.when(last)
        def _():
            store_state(hf_ref, hf_out_ref)


def _ssd(lay, xbc_c, z, pre, h0f, h0b, d_skip, norm_g):
    T = lay.n_tokens
    tabs = lay.ssd_step_tables()
    n_steps = tabs[0].shape[0]
    nsp = len(tabs)

    def by_chunk(s, *t):
        return (t[0][s], 0)

    def by_y(s, *t):
        return (t[1][s], 0)

    def by_sin(s, *t):
        return (t[6][s], 0, 0, 0)

    def by_sout(s, *t):
        return (t[7][s], 0, 0, 0)

    def const(s, *t):
        return (0, 0)

    H = N_HEADS
    eye3 = jnp.asarray(np.arange(4 * H)[:, None] == np.arange(12 * H)[None, :] % (4 * H), dtype=BF16)
    consts = [_expand3(2 * H, HEADDIM), _expand3(H, HEADDIM), _expand3(2 * H, D_STATE),
              _expand3(3 * H, HEADDIM), eye3]
    sshape = (1, N_HEADS, HEADDIM, D_STATE)
    gs = pltpu.PrefetchScalarGridSpec(
        num_scalar_prefetch=nsp,
        grid=(n_steps,),
        in_specs=[pl.BlockSpec((CHUNK, D_XBC), by_chunk),
                  pl.BlockSpec((CHUNK, D_SSM), by_chunk),
                  pl.BlockSpec((CHUNK, 128), by_chunk),
                  pl.BlockSpec(sshape, by_sin),
                  pl.BlockSpec(sshape, by_sin),
                  pl.BlockSpec((1, D_SSM), const),
                  pl.BlockSpec((1, D_SSM), const)] + [pl.BlockSpec(c.shape, const) for c in consts],
        out_specs=[pl.BlockSpec((CHUNK, D_SSM), by_y),
                   pl.BlockSpec(sshape, by_sout),
                   pl.BlockSpec(sshape, by_sout)],
        scratch_shapes=[pltpu.VMEM((D_STATE, D_SSM), F32),
                        pltpu.VMEM((D_STATE, D_SSM), F32),
                        pltpu.VMEM((lay.max_chunks, D_STATE, D_SSM), BF16),
                        pltpu.VMEM((CHUNK, D_SSM), F32)])
    n_out = lay.n_prompt_seqs
    return pl.pallas_call(
        _ssd_kernel, grid_spec=gs,
        out_shape=(jax.ShapeDtypeStruct((T, D_SSM), BF16),
                   jax.ShapeDtypeStruct((n_out,) + sshape[1:], F32),
                   jax.ShapeDtypeStruct((n_out,) + sshape[1:], F32)),
        compiler_params=_cparams(("arbitrary",)),
        name="ssd",
    )(*tabs, xbc_c, z, pre, h0f, h0b,
      jnp.repeat(d_skip, HEADDIM).reshape(1, -1), norm_g.reshape(1, -1), *consts)


def _outproj_kernel(row_ref, posb_ref, flag_ref, xp_ref, xs_ref, pos_ref, mod_ref, co_ref, ys_ref, wo_ref,
                    g_ref, b_ref, x1_ref, h2_ref, h2p_ref, *, alpha):
    i = pl.program_id(0)
    x = jnp.where(flag_ref[i] == 1, xs_ref[...] + pos_ref[...], xp_ref[...])
    r = row_ref[i]
    g1 = mod_ref[pl.ds(r, 1), 2 * D_MODEL:3 * D_MODEL]
    sh2 = mod_ref[pl.ds(r, 1), 3 * D_MODEL:4 * D_MODEL]
    sc2 = mod_ref[pl.ds(r, 1), 4 * D_MODEL:5 * D_MODEL]
    mix = jnp.dot(co_ref[...], wo_ref[0:D_CONV, :], preferred_element_type=F32) \
        + jnp.dot(ys_ref[...], wo_ref[D_CONV:, :], preferred_element_type=F32)
    x1 = _ln_rows(alpha * x + g1 * mix) * g_ref[...] + b_ref[...]
    x1_ref[...] = x1
    h2 = _ln_rows(x1) * (1.0 + sc2) + sh2
    h2_ref[...] = h2
    packed = _pack_halves(h2)
    for c in range(ROW_PARTS):
        h2p_ref[c] = packed[:, c * PART_WORDS:(c + 1) * PART_WORDS]


def _outproj(lay, xp, xs, pos, mod, conv_out, y_ssm, w_out, ln_g, ln_b, alpha, tm):
    T = lay.n_tokens
    row, posb, flag = lay.token_tile_tables(tm)
    npt = lay.n_prompt_tokens // tm

    def const(i, r, p, f):
        return (0, 0)

    def cur(i, r, p, f):
        return (i, 0)

    gs = pltpu.PrefetchScalarGridSpec(
        num_scalar_prefetch=3,
        grid=(T // tm,),
        in_specs=[pl.BlockSpec((tm, D_MODEL), lambda i, r, p, f: (jnp.minimum(i, npt - 1), 0)),
                  pl.BlockSpec((tm, D_MODEL), lambda i, r, p, f: (jnp.maximum(i - npt, 0), 0)),
                  pl.BlockSpec((tm, D_MODEL), lambda i, r, p, f: (p[i], 0)),
                  pl.BlockSpec((8, 6 * D_MODEL), const),
                  pl.BlockSpec((tm, D_CONV), cur),
                  pl.BlockSpec((tm, D_SSM), cur),
                  pl.BlockSpec((D_CONV + D_SSM, D_MODEL), const),
                  pl.BlockSpec((1, D_MODEL), const),
                  pl.BlockSpec((1, D_MODEL), const)],
        out_specs=[pl.BlockSpec((tm, D_MODEL), cur),
                   pl.BlockSpec((tm, D_MODEL), cur),
                   pl.BlockSpec((ROW_PARTS, tm, PART_WORDS), lambda i, r, p, f: (0, i, 0))])
    return pl.pallas_call(
        functools.partial(_outproj_kernel, alpha=alpha), grid_spec=gs,
        out_shape=(jax.ShapeDtypeStruct((T, D_MODEL), F32),
                   jax.ShapeDtypeStruct((T, D_MODEL), F32),
                   jax.ShapeDtypeStruct((ROW_PARTS, T, PART_WORDS), jnp.uint32)),
        compiler_params=_cparams(("arbitrary",)),
        name="outproj",
    )(row, posb, flag, xp, xs, pos, mod, conv_out, y_ssm, w_out, ln_g.reshape(1, -1), ln_b.reshape(1, -1))


def _route_kernel(h2_ref, wrt_ref, bias_ref, idx_ref, wts_ref, pos_ref, cnt_ref, carry_ref, *, tm):
    i = pl.program_id(0)

    @pl.when(i == 0)
    def _():
        carry_ref[...] = jnp.zeros_like(carry_ref)

    E, NG, EG = N_EXPERTS, N_EXPERT_GROUPS, EXPERTS_PER_GROUP
    neg = -jnp.inf
    h = h2_ref[...]
    h_hi = h.astype(BF16)
    h_lo = (h - h_hi.astype(F32)).astype(BF16)
    h3 = jnp.concatenate([h_hi, h_lo, h_hi], axis=1)
    logits = lax.dot_general(wrt_ref[...], h3, (((1,), (1,)), ((), ())),
                             preferred_element_type=F32)
    s = jax.nn.sigmoid(logits)
    sel = s + bias_ref[...]
    sel3 = sel.reshape(NG, EG, tm)
    io3 = _iota((NG, EG, tm), 1)
    m1 = jnp.max(sel3, axis=1, keepdims=True)
    f1 = jnp.min(jnp.where(sel3 == m1, io3, EG), axis=1, keepdims=True)
    m2 = jnp.max(jnp.where(io3 == f1, neg, sel3), axis=1, keepdims=True)
    gscore = (m1 + m2).reshape(NG, tm)
    gio = _iota((NG, tm), 0)
    beaten = jnp.zeros((NG, tm), I32)
    for g in range(NG):
        row = gscore[g:g + 1, :]
        beats = jnp.where(row > gscore, 1, jnp.where(row == gscore, jnp.where(g < gio, 1, 0), 0))
        beaten = beaten + beats
    keep = (beaten < TOPK_GROUPS).astype(F32).reshape(NG, 1, tm)
    selm = jnp.where(keep > 0.5, sel3, neg).reshape(E, tm)
    eio = _iota((E, tm), 0)
    chosen = jnp.zeros((E, tm), F32)
    idxs, ws = [], []
    for k in range(TOP_K):
        m = jnp.max(selm, axis=0, keepdims=True)
        am = jnp.minimum(jnp.min(jnp.where(selm == m, eio, E), axis=0, keepdims=True), E - 1)
        hit = eio == am
        ws.append(jnp.sum(jnp.where(hit, s, 0.0), axis=0, keepdims=True))
        idxs.append(am)
        selm = jnp.where(hit, neg, selm)
        chosen = jnp.where(hit, 1.0, chosen)
    wsum = ws[0]
    for k in range(1, TOP_K):
        wsum = wsum + ws[k]
    before = (_iota((tm, tm), 0) < _iota((tm, tm), 1)).astype(BF16)
    prior = jnp.dot(chosen.astype(BF16), before, preferred_element_type=F32)
    carry = carry_ref[...]
    prior = prior + jnp.concatenate([carry] * (tm // 128), axis=1)
    for k in range(TOP_K):
        idx_ref[k:k + 1, :] = idxs[k]
        wts_ref[k:k + 1, :] = ws[k] / wsum * ROUTED_SCALE
        pos_ref[k:k + 1, :] = jnp.sum(jnp.where(eio == idxs[k], prior, 0.0), axis=0, keepdims=True).astype(I32)
    total = jnp.dot(chosen.astype(BF16), jnp.ones((tm, 128), BF16), preferred_element_type=F32)
    carry = carry + total
    carry_ref[...] = carry
    cnt_ref[...] = carry.astype(I32)


def _route(h2, w_router_t, router_bias, tm):
    T = h2.shape[0]
    bias_b = jnp.broadcast_to(router_bias.astype(F32)[:, None], (N_EXPERTS, tm))
    w_hi = w_router_t.astype(BF16)
    w_lo = (w_router_t - w_hi.astype(F32)).astype(BF16)
    w_router_t = jnp.concatenate([w_hi, w_hi, w_lo], axis=1)
    return pl.pallas_call(
        functools.partial(_route_kernel, tm=tm),
        grid=(T // tm,),
        in_specs=[pl.BlockSpec((tm, D_MODEL), lambda i: (i, 0)),
                  pl.BlockSpec((N_EXPERTS, 3 * D_MODEL), lambda i: (0, 0)),
                  pl.BlockSpec((N_EXPERTS, tm), lambda i: (0, 0))],
        out_specs=[pl.BlockSpec((TOP_K, tm), lambda i: (0, i)),
                   pl.BlockSpec((TOP_K, tm), lambda i: (0, i)),
                   pl.BlockSpec((TOP_K, tm), lambda i: (0, i)),
                   pl.BlockSpec((N_EXPERTS, 128), lambda i: (0, 0))],
        out_shape=(jax.ShapeDtypeStruct((TOP_K, T), I32),
                   jax.ShapeDtypeStruct((TOP_K, T), F32),
                   jax.ShapeDtypeStruct((TOP_K, T), I32),
                   jax.ShapeDtypeStruct((N_EXPERTS, 128), I32)),
        scratch_shapes=[pltpu.VMEM((N_EXPERTS, 128), F32)],
        compiler_params=_cparams(("arbitrary",)),
        name="route",
    )(h2, w_router_t, bias_b)


def _dest_kernel(idx_ref, pos_ref, start_ref, dest_ref):
    tm = idx_ref.shape[1]
    eio = _iota((N_EXPERTS, tm), 0)
    start = start_ref[...]
    for k in range(TOP_K):
        base = jnp.sum(jnp.where(eio == idx_ref[k:k + 1, :], start, 0.0), axis=0, keepdims=True)
        dest_ref[k:k + 1, :] = base.astype(I32) + pos_ref[k:k + 1, :]


def _dest(idx, pos, pad_start, tm):
    T = idx.shape[1]
    start_b = jnp.broadcast_to(pad_start.astype(F32)[:, None], (N_EXPERTS, tm))
    return pl.pallas_call(
        _dest_kernel,
        grid=(T // tm,),
        in_specs=[pl.BlockSpec((TOP_K, tm), lambda i: (0, i)),
                  pl.BlockSpec((TOP_K, tm), lambda i: (0, i)),
                  pl.BlockSpec((N_EXPERTS, tm), lambda i: (0, 0))],
        out_specs=pl.BlockSpec((TOP_K, tm), lambda i: (0, i)),
        out_shape=jax.ShapeDtypeStruct((TOP_K, T), I32),
        compiler_params=_cparams(("arbitrary",)),
        name="dest",
    )(idx, pos, start_b)


_WEIGHT_FETCH_CHUNKS = 4
_W_SLOTS = 5
_X_SLOTS = 4


def _expert_kernel(base_ref, nblk_ref, nused_ref, x_hbm, wgu_hbm, wd_hbm, y_hbm,
                   wgu_bf, wd_bf, wgu_stage, wd_stage, xbuf, ybuf, sems, xsems, ysems):
    e = pl.program_id(0)
    n_used = nused_ref[0]

    def x_copy(b, slot):
        rows = pl.ds(pl.multiple_of(b * EXPERT_BLOCK, EXPERT_BLOCK), EXPERT_BLOCK)
        return pltpu.make_async_copy(x_hbm.at[:, rows, :], xbuf.at[slot], xsems.at[slot])

    def y_copy(b, slot):
        rows = pl.ds(pl.multiple_of(b * EXPERT_BLOCK, EXPERT_BLOCK), EXPERT_BLOCK)
        return pltpu.make_async_copy(ybuf.at[slot], y_hbm.at[:, rows, :], ysems.at[slot])

    def fetch(ex):
        ws = lax.rem(ex, _W_SLOTS)
        cps = []
        for c in range(_WEIGHT_FETCH_CHUNKS):
            rg = pl.ds(c * (D_MODEL // _WEIGHT_FETCH_CHUNKS), D_MODEL // _WEIGHT_FETCH_CHUNKS)
            rd = pl.ds(c * (D_EXPERT // _WEIGHT_FETCH_CHUNKS), D_EXPERT // _WEIGHT_FETCH_CHUNKS)
            cps.append(pltpu.make_async_copy(wgu_hbm.at[ex, rg], wgu_stage.at[ws, rg], sems.at[ws, 0]))
            cps.append(pltpu.make_async_copy(wd_hbm.at[ex, rd], wd_stage.at[ws, rd], sems.at[ws, 1]))
        return cps

    @pl.when(e == 0)
    def _():
        for ex in range(_W_SLOTS):
            for cp in fetch(ex):
                cp.start()
        for b in range(_X_SLOTS - 1):
            @pl.when(b < n_used)
            def _(b=b):
                x_copy(b, b).start()

    for cp in fetch(e):
        cp.wait()
    ws = lax.rem(e, _W_SLOTS)
    wgu_bf[...] = wgu_stage[ws].astype(BF16)
    wd_bf[...] = wd_stage[ws].astype(BF16)

    @pl.when(e + _W_SLOTS < N_EXPERTS)
    def _():
        for cp in fetch(e + _W_SLOTS):
            cp.start()

    def block(b, carry):
        xs = lax.rem(b, _X_SLOTS)
        slot = lax.rem(b, 2)
        x_copy(b, xs).wait()

        @pl.when(b + _X_SLOTS - 1 < n_used)
        def _():
            x_copy(b + _X_SLOTS - 1, lax.rem(b + _X_SLOTS - 1, _X_SLOTS)).start()

        @pl.when(b >= 2)
        def _():
            y_copy(b - 2, slot).wait()

        parts = [_unpack_halves(xbuf[xs, c]) for c in range(ROW_PARTS)]
        chunks = [p[0] for p in parts] + [p[1] for p in parts]
        gu = jnp.zeros((EXPERT_BLOCK, 2 * D_EXPERT), F32)
        for j, xc in enumerate(chunks):
            gu = gu + jnp.dot(xc.astype(BF16), wgu_bf[j * PART_WORDS:(j + 1) * PART_WORDS, :],
                              preferred_element_type=F32)
        act = (_silu(gu[:, 0:D_EXPERT]) * gu[:, D_EXPERT:]).astype(BF16)
        packed = _pack_halves(jnp.dot(act, wd_bf[...], preferred_element_type=F32))
        for c in range(ROW_PARTS):
            ybuf[slot, c] = packed[:, c * PART_WORDS:(c + 1) * PART_WORDS]
        y_copy(b, slot).start()
        return carry

    lax.fori_loop(base_ref[e], base_ref[e] + nblk_ref[e], block, 0)

    @pl.when(e == N_EXPERTS - 1)
    def _():
        @pl.when(n_used >= 2)
        def _():
            y_copy(n_used - 2, lax.rem(n_used, 2)).wait()

        @pl.when(n_used >= 1)
        def _():
            y_copy(n_used - 1, lax.rem(n_used - 1, 2)).wait()


def _pack_halves(x):
    n = x.shape[1] // 2
    hi = lax.bitcast_convert_type(x[:, :n].astype(BF16).astype(F32), jnp.uint32)
    lo = lax.bitcast_convert_type(x[:, n:].astype(BF16).astype(F32), jnp.uint32)
    return hi | (lo >> 16)


def _unpack_halves(p):
    hi = lax.bitcast_convert_type(p & jnp.uint32(0xFFFF0000), F32)
    lo = lax.bitcast_convert_type(p << 16, F32)
    return hi, lo


def _expert(x_sorted, w_gu, w_down, blk_base, blk_count, n_used):
    n_rows = x_sorted.shape[1]
    blk_shape = (ROW_PARTS, EXPERT_BLOCK, PART_WORDS)
    gs = pltpu.PrefetchScalarGridSpec(
        num_scalar_prefetch=3,
        grid=(N_EXPERTS,),
        in_specs=[pl.BlockSpec(memory_space=pl.ANY),
                  pl.BlockSpec(memory_space=pl.ANY),
                  pl.BlockSpec(memory_space=pl.ANY)],
        out_specs=pl.BlockSpec(memory_space=pl.ANY),
        scratch_shapes=[pltpu.VMEM((D_MODEL, 2 * D_EXPERT), BF16),
                        pltpu.VMEM((D_EXPERT, D_MODEL), BF16),
                        pltpu.VMEM((_W_SLOTS, D_MODEL, 2 * D_EXPERT), F32),
                        pltpu.VMEM((_W_SLOTS, D_EXPERT, D_MODEL), F32),
                        pltpu.VMEM((_X_SLOTS,) + blk_shape, jnp.uint32),
                        pltpu.VMEM((2,) + blk_shape, jnp.uint32),
                        pltpu.SemaphoreType.DMA((_W_SLOTS, 2)),
                        pltpu.SemaphoreType.DMA((_X_SLOTS,)),
                        pltpu.SemaphoreType.DMA((2,))])
    return pl.pallas_call(
        _expert_kernel, grid_spec=gs,
        out_shape=jax.ShapeDtypeStruct((ROW_PARTS, n_rows, PART_WORDS), jnp.uint32),
        compiler_params=_cparams(("arbitrary",)),
        name="expert",
    )(blk_base, blk_count, n_used, x_sorted, w_gu, w_down)


def _combine_kernel(row_ref, h2_ref, x1_ref, wt_ref, mod_ref, wsg_ref, wsd_ref, g_ref, b_ref, yt_ref,
                    o_ref, *, first_tile, alpha):
    i = pl.program_id(0)
    h2 = h2_ref[...].astype(BF16)
    su = jnp.dot(h2, wsg_ref[...], preferred_element_type=F32)
    act = (_silu(su[:, 0:D_SHARED]) * su[:, D_SHARED:]).astype(BF16)
    moe = jnp.dot(act, wsd_ref[...], preferred_element_type=F32)
    wt = wt_ref[...]
    his, los = [], []
    for c in range(ROW_PARTS):
        rh = jnp.zeros((h2.shape[0], PART_WORDS), F32)
        rl = jnp.zeros((h2.shape[0], PART_WORDS), F32)
        for k in range(TOP_K):
            hi, lo = _unpack_halves(yt_ref[c, k])
            w = wt[:, k:k + 1]
            rh = rh + hi * w
            rl = rl + lo * w
        his.append(rh)
        los.append(rl)
    moe = moe + jnp.concatenate(his + los, axis=1)
    g2 = mod_ref[pl.ds(row_ref[first_tile + i], 1), 5 * D_MODEL:6 * D_MODEL]
    o_ref[...] = _ln_rows(alpha * x1_ref[...] + g2 * moe) * g_ref[...] + b_ref[...]


def _combine(lay, h2, x1, wts_tok, mod, w_sh_gu, w_sh_down, ln_g, ln_b, y_tok, alpha, tm, first_token, n_tok):
    row, _, _ = lay.token_tile_tables(tm)
    first_tile = first_token // tm

    def cur(i, r):
        return (first_tile + i, 0)

    def const(i, r):
        return (0, 0)

    gs = pltpu.PrefetchScalarGridSpec(
        num_scalar_prefetch=1,
        grid=(n_tok // tm,),
        in_specs=[pl.BlockSpec((tm, D_MODEL), cur),
                  pl.BlockSpec((tm, D_MODEL), cur),
                  pl.BlockSpec((tm, TOP_K), cur),
                  pl.BlockSpec((8, 6 * D_MODEL), const),
                  pl.BlockSpec((D_MODEL, 2 * D_SHARED), const),
                  pl.BlockSpec((D_SHARED, D_MODEL), const),
                  pl.BlockSpec((1, D_MODEL), const),
                  pl.BlockSpec((1, D_MODEL), const),
                  pl.BlockSpec((ROW_PARTS, TOP_K, tm, PART_WORDS), lambda i, r: (0, 0, i, 0))],
        out_specs=pl.BlockSpec((tm, D_MODEL), lambda i, r: (i, 0)))
    return pl.pallas_call(
        functools.partial(_combine_kernel, first_tile=first_tile, alpha=alpha), grid_spec=gs,
        out_shape=jax.ShapeDtypeStruct((n_tok, D_MODEL), F32),
        compiler_params=_cparams(("arbitrary",)),
        name="combine",
    )(row, h2, x1, wts_tok, mod, w_sh_gu, w_sh_down, ln_g.reshape(1, -1), ln_b.reshape(1, -1), y_tok)


_GATHER_WINDOW = 128


def _sc_gather(table, idx):
    n, d = idx.shape[0], table.shape[1]
    mesh = plsc.VectorSubcoreMesh(core_axis_name="core", subcore_axis_name="subcore")

    @pl.kernel(out_type=jax.ShapeDtypeStruct((n, d), table.dtype), mesh=mesh)
    def gather_kernel(table_hbm, idx_hbm, out_hbm):
        def body(idx_vmem, out_vmem):
            pltpu.sync_copy(table_hbm.at[idx_vmem.at[0]], out_vmem)

        pltpu.emit_pipeline(
            body,
            grid=(n // _GATHER_WINDOW,),
            in_specs=[pl.BlockSpec((1, _GATHER_WINDOW), index_map=lambda i: (0, i))],
            out_specs=[pl.BlockSpec((_GATHER_WINDOW, d), index_map=lambda i: (i, 0))],
            core_axis_name=("core", "subcore"),
            dimension_semantics=(pltpu.PARALLEL,),
        )(idx_hbm, out_hbm)

    return gather_kernel(table, idx.reshape(1, n))


def _sc_scatter(rows, idx, n_out, repeat):
    n, d = rows.shape
    mesh = plsc.VectorSubcoreMesh(core_axis_name="core", subcore_axis_name="subcore")

    @pl.kernel(out_type=jax.ShapeDtypeStruct((n_out, d), rows.dtype), mesh=mesh, scratch_types=[])
    def scatter_kernel(rows_hbm, idx_hbm, out_hbm):
        def body(rows_vmem, idx_vmem):
            for r in range(repeat):
                pltpu.sync_copy(rows_vmem, out_hbm.at[idx_vmem.at[r]])

        pltpu.emit_pipeline(
            body,
            grid=(n // _GATHER_WINDOW,),
            in_specs=[pl.BlockSpec((_GATHER_WINDOW, d), index_map=lambda i: (i, 0)),
                      pl.BlockSpec((repeat, _GATHER_WINDOW), index_map=lambda i: (0, i))],
            out_specs=[],
            core_axis_name=("core", "subcore"),
            dimension_semantics=(pltpu.PARALLEL,),
        )(rows_hbm, idx_hbm)

    return scatter_kernel(rows, idx.reshape(repeat, n))


class _Layout:
    def __init__(self, n_prompt_seqs, prompt_len, n_sample_seqs, sample_len):
        self.n_prompt_seqs, self.prompt_len = n_prompt_seqs, prompt_len
        self.n_sample_seqs, self.sample_len = n_sample_seqs, sample_len
        self.n_prompt_tokens = n_prompt_seqs * prompt_len
        self.n_tokens = self.n_prompt_tokens + n_sample_seqs * sample_len
        assert prompt_len % CONV_TILE == 0 and sample_len % CONV_TILE == 0
        self.max_chunks = max(prompt_len, sample_len) // CHUNK

    def token_tile_tables(self, tm):
        assert self.n_prompt_tokens % tm == 0 and self.sample_len % tm == 0
        npt = self.n_prompt_tokens // tm
        per_seq = self.sample_len // tm
        n = self.n_tokens // tm
        row = np.zeros(n, np.int32)
        posb = np.zeros(n, np.int32)
        flag = np.zeros(n, np.int32)
        for i in range(npt, n):
            j = i - npt
            row[i] = 1 + j // per_seq
            posb[i] = j % per_seq
            flag[i] = 1
        return jnp.asarray(row), jnp.asarray(posb), jnp.asarray(flag)

    def conv_tile_tables(self):
        lok, rok = [], []
        for n_seq, length in ((self.n_prompt_seqs, self.prompt_len), (self.n_sample_seqs, self.sample_len)):
            per = length // CONV_TILE
            for _ in range(n_seq):
                for j in range(per):
                    lok.append(int(j > 0))
                    rok.append(int(j < per - 1))
        return jnp.asarray(np.array(lok, np.int32)), jnp.asarray(np.array(rok, np.int32))

    def ssd_step_tables(self):
        cols = [[] for _ in range(9)]
        seqs = []
        c0 = self.n_prompt_tokens // CHUNK
        for j in range(self.n_sample_seqs):
            nc = self.sample_len // CHUNK
            seqs.append((c0 + j * nc, nc, 0, j, 0))
        for j in range(self.n_prompt_seqs):
            nc = self.prompt_len // CHUNK
            seqs.append((j * nc, nc, 1, 0, j))
        for base, nc, zero, sin, sout in seqs:
            for phase in (0, 1):
                order = range(nc - 1, -1, -1) if phase == 0 else range(nc)
                for n, c in enumerate(order):
                    vals = (base + c, base if phase == 0 else base + c, phase, int(n == 0), int(n == nc - 1),
                            zero, sin, sout, c)
                    for col, v in zip(cols, vals):
                        col.append(v)
        return tuple(jnp.asarray(np.array(col, np.int32)) for col in cols)


def _grid_pos_embed(n_tokens):
    rows = n_tokens // GRID_W
    quarter = D_MODEL // 4
    freq = jnp.exp(-math.log(10000.0) * jnp.arange(quarter, dtype=F32) / quarter)
    r = jnp.broadcast_to(jnp.arange(rows, dtype=F32)[:, None, None] * freq, (rows, GRID_W, quarter))
    cl = jnp.broadcast_to(jnp.arange(GRID_W, dtype=F32)[None, :, None] * freq, (rows, GRID_W, quarter))
    emb = jnp.concatenate([jnp.sin(r), jnp.cos(r), jnp.sin(cl), jnp.cos(cl)], axis=-1)
    return emb.reshape(rows * GRID_W, D_MODEL)


def _moe_plan(counts):
    blk_count = (counts + EXPERT_BLOCK - 1) // EXPERT_BLOCK
    blk_end = jnp.cumsum(blk_count)
    blk_base = blk_end - blk_count
    return ((blk_base * EXPERT_BLOCK).astype(I32), blk_base.astype(I32), blk_count.astype(I32),
            blk_end[-1:].astype(I32))


def _layer(lay, xp, xs, pos, cond8, h0f, h0b, lp, alpha, tm_proj=512, tm_route=256, tm_comb=256):
    (w_ada, b_ada, w_in, conv_w, conv_b, conv_ln_g, conv_ln_b, ssm_conv_w, ssm_conv_b, dt_bias, a_log,
     d_skip, ssm_norm_g, w_out, ln1_g, ln1_b, w_router, router_bias, w_exp_gu, w_exp_down, w_sh_gu,
     w_sh_down, ln2_g, ln2_b) = lp
    T = lay.n_tokens
    n_main = 2 * D_CONV + D_SSM + D_XBC
    w_main = w_in[:, :n_main].astype(BF16)
    w_dt = jnp.pad(w_in[:, n_main:], ((0, 0), (0, 128 - 2 * N_HEADS))).astype(BF16)

    mod = _ada(cond8, w_ada, b_ada)
    glu, z, xbc, dt_raw = _inproj(lay, xp, xs, pos, mod, w_main, w_dt, tm_proj)
    conv_out, xbc_c, pre = _conv(lay, glu, xbc, dt_raw, conv_w, conv_b, conv_ln_g, conv_ln_b, ssm_conv_w,
                                 ssm_conv_b, dt_bias, a_log)
    y_ssm, hf, hb = _ssd(lay, xbc_c, z, pre, h0f, h0b, d_skip, ssm_norm_g)
    x1, h2, h2p = _outproj(lay, xp, xs, pos, mod, conv_out, y_ssm, w_out.astype(BF16), ln1_g, ln1_b, alpha, tm_proj)

    idx, wts, posn, cnt = _route(h2, w_router.T, router_bias, tm_route)
    n_blocks = -(-T * TOP_K // EXPERT_BLOCK) + N_EXPERTS
    pad_start, blk_base, blk_count, n_used = _moe_plan(cnt[:, 0])
    dest2 = _dest(idx, posn, pad_start, 512)
    n_rows = n_blocks * EXPERT_BLOCK
    scatter_idx = jnp.concatenate([dest2 + c * n_rows for c in range(ROW_PARTS)], axis=1)
    x_sorted = _sc_scatter(h2p.reshape(ROW_PARTS * T, PART_WORDS), scatter_idx.reshape(-1),
                           ROW_PARTS * n_rows, TOP_K).reshape(ROW_PARTS, n_rows, PART_WORDS)
    y_sorted = _expert(x_sorted, w_exp_gu, w_exp_down, blk_base, blk_count, n_used)
    y_flat = y_sorted.reshape(ROW_PARTS * n_rows, PART_WORDS)
    wts_tok, wsg, wsd = wts.T, w_sh_gu.astype(BF16), w_sh_down.astype(BF16)
    outs = []
    for first, n_tok in ((0, lay.n_prompt_tokens), (lay.n_prompt_tokens, T - lay.n_prompt_tokens)):
        dest_g = dest2[:, first:first + n_tok].reshape(-1)
        idx_parts = jnp.concatenate([dest_g + c * n_rows for c in range(ROW_PARTS)])
        y_tok = _sc_gather(y_flat, idx_parts).reshape(ROW_PARTS, TOP_K, n_tok, PART_WORDS)
        outs.append(_combine(lay, h2, x1, wts_tok, mod, wsg, wsd, ln2_g, ln2_b, y_tok, alpha, tm_comb,
                             first, n_tok))
    return outs[0], outs[1], hf, hb


def kernel(x_prompt, x_sample, state_ssd_fwd, state_ssd_bwd, c, c_ctx, w_ada, b_ada, w_in, conv_w, conv_b, conv_ln_g, conv_ln_b, ssm_conv_w, ssm_conv_b, dt_bias, a_log, d_skip, ssm_norm_g, w_out, ln1_g, ln1_b, w_router, router_bias, w_exp_gu, w_exp_down, w_sh_gu, w_sh_down, ln2_g, ln2_b):
    depth = w_ada.shape[0]
    assert depth == 1, "the prompt and latent passes are fused per layer; one layer is supported"
    bp, lp_, _ = x_prompt.shape
    bd, ld, _ = x_sample.shape
    lay = _Layout(bp, lp_, bd, ld)
    alpha = (2.0 * depth) ** 0.25
    stacked = (w_ada, b_ada, w_in, conv_w, conv_b, conv_ln_g, conv_ln_b, ssm_conv_w, ssm_conv_b,
               dt_bias, a_log, d_skip, ssm_norm_g, w_out, ln1_g, ln1_b, w_router, router_bias,
               w_exp_gu, w_exp_down, w_sh_gu, w_sh_down, ln2_g, ln2_b)
    lp = [w[0] for w in stacked]
    cond8 = jnp.concatenate([c_ctx[None, :], c, jnp.zeros((8 - 1 - bd, D_MODEL), F32)], axis=0)
    pos = _grid_pos_embed(ld)
    sshape = (bd, N_HEADS, HEADDIM, D_STATE)
    out_p, out_s, hf, hb = _layer(lay, x_prompt.reshape(bp * lp_, D_MODEL), x_sample.reshape(bd * ld, D_MODEL),
                                  pos, cond8, state_ssd_fwd[:, 0].reshape(sshape),
                                  state_ssd_bwd[:, 0].reshape(sshape), lp, alpha)
    return (out_p.reshape(bp, lp_, D_MODEL), out_s.reshape(bd, ld, D_MODEL),
            hf[:, None], hb[:, None])
```

```python
import functools
import math

import numpy as np
import jax
import jax.numpy as jnp
from jax import lax
from jax.experimental import pallas as pl
from jax.experimental.pallas import tpu as pltpu
from jax.experimental.pallas import tpu_sc as plsc

F32 = jnp.float32
BF16 = jnp.bfloat16
I32 = jnp.int32
HI = lax.Precision.HIGHEST

D_MODEL = 1024
GRID_W = 64
D_CONV = 1024
CONV_K = 31
N_HEADS = 16
HEADDIM = 64
D_SSM = N_HEADS * HEADDIM
N_GROUPS = 4
HEADS_PER_GROUP = N_HEADS // N_GROUPS
D_STATE = 128
SSM_CONV_K = 4
CHUNK = 128
D_XBC = D_SSM + 2 * N_GROUPS * D_STATE
N_EXPERTS = 256
TOP_K = 8
N_EXPERT_GROUPS = 8
EXPERTS_PER_GROUP = N_EXPERTS // N_EXPERT_GROUPS
TOPK_GROUPS = 4
D_EXPERT = 256
D_SHARED = 256
ROUTED_SCALE = 2.5
LN_EPS = 1e-5

CONV_TILE = 256
HALO = 16
EXPERT_BLOCK = 256
ROW_PARTS = 2
PART_WORDS = D_MODEL // 2 // ROW_PARTS
VMEM_LIMIT = 56 * 1024 * 1024


def _cparams(sem, vmem=VMEM_LIMIT):
    return pltpu.CompilerParams(dimension_semantics=sem, vmem_limit_bytes=vmem)


def _silu(x):
    return x * jax.nn.sigmoid(x)


def _ln_rows(x):
    mu = jnp.mean(x, axis=-1, keepdims=True)
    xc = x - mu
    var = jnp.mean(xc * xc, axis=-1, keepdims=True)
    return xc * lax.rsqrt(var + LN_EPS)


def _iota(shape, dim):
    return lax.broadcasted_iota(I32, shape, dim)


def _expand_matrix(n_in, width):
    return (_iota((n_in, n_in * width), 0) == _iota((n_in, n_in * width), 1) // width).astype(F32)


def _dot_hi(a, b):
    return jnp.dot(a, b, precision=HI, preferred_element_type=F32)


def _split3(x):
    hi = x.astype(BF16)
    r1 = x - hi.astype(F32)
    mid = r1.astype(BF16)
    lo = (r1 - mid.astype(F32)).astype(BF16)
    return jnp.concatenate([hi, mid, lo], axis=1)


def _expand3(n, width):
    rows = np.arange(3 * n)[:, None] % n
    cols = np.arange(n * width)[None, :] // width
    return jnp.asarray(rows == cols, dtype=BF16)


def _expand_exact(x, e3):
    return jnp.dot(_split3(x), e3, preferred_element_type=F32)


def _ada_kernel(c_ref, w_ref, b_ref, o_ref):
    o_ref[...] = _dot_hi(_silu(c_ref[...]), w_ref[...]) + b_ref[...]


def _ada(cond8, w_ada, b_ada):
    n = w_ada.shape[1]
    tn = 1024
    return pl.pallas_call(
        _ada_kernel,
        grid=(n // tn,),
        in_specs=[pl.BlockSpec((8, D_MODEL), lambda j: (0, 0)),
                  pl.BlockSpec((D_MODEL, tn), lambda j: (0, j)),
                  pl.BlockSpec((1, tn), lambda j: (0, j))],
        out_specs=pl.BlockSpec((8, tn), lambda j: (0, j)),
        out_shape=jax.ShapeDtypeStruct((8, n), F32),
        compiler_params=_cparams(("arbitrary",)),
        name="ada",
    )(cond8, w_ada, b_ada.reshape(1, n))


def _inproj_kernel(row_ref, posb_ref, flag_ref, xp_ref, xs_ref, pos_ref, mod_ref, wm_ref, wdt_ref,
                   dtb_ref, alog_ref, tri_ref, u_ref, z_ref, xbc_ref, pre_ref):
    i = pl.program_id(0)
    x = jnp.where(flag_ref[i] == 1, xs_ref[...] + pos_ref[...], xp_ref[...])
    r = row_ref[i]
    sh1 = mod_ref[pl.ds(r, 1), 0:D_MODEL]
    sc1 = mod_ref[pl.ds(r, 1), D_MODEL:2 * D_MODEL]
    h = (_ln_rows(x) * (1.0 + sc1) + sh1).astype(BF16)
    glu_a = jnp.dot(h, wm_ref[:, 0:D_CONV], preferred_element_type=F32)
    glu_g = jnp.dot(h, wm_ref[:, D_CONV:2 * D_CONV], preferred_element_type=F32)
    u_ref[...] = (glu_a * jax.nn.sigmoid(glu_g)).astype(BF16)
    z = jnp.dot(h, wm_ref[:, 2 * D_CONV:2 * D_CONV + D_SSM], preferred_element_type=F32)
    z_ref[...] = _silu(z).astype(BF16)
    xbc_ref[...] = jnp.dot(h, wm_ref[:, 2 * D_CONV + D_SSM:], preferred_element_type=F32).astype(BF16)
    nh2 = 2 * N_HEADS
    dt = jnp.dot(h, wdt_ref[...], preferred_element_type=F32)[:, 0:nh2] + dtb_ref[...]
    dt = jnp.maximum(dt, 0.0) + jnp.log1p(jnp.exp(-jnp.abs(dt)))
    a = dt * (-jnp.exp(alog_ref[...]))
    a3 = jnp.dot(tri_ref[...], _split3(a), preferred_element_type=F32)
    acs = a3[:, 0:nh2] + a3[:, nh2:2 * nh2] + a3[:, 2 * nh2:3 * nh2]
    pre_ref[...] = jnp.concatenate([dt, acs, a, jnp.zeros((dt.shape[0], 128 - 3 * nh2), F32)], axis=1)


def _inproj(lay, xp, xs, pos, mod, w_main, w_dt, dt_bias, a_log, tm):
    T = lay.n_tokens
    row, posb, flag = lay.token_tile_tables(tm)
    npt = lay.n_prompt_tokens // tm
    n_main = w_main.shape[1]
    chunk_of = np.arange(tm) // CHUNK
    tri = jnp.asarray((chunk_of[:, None] == chunk_of[None, :]) & np.tril(np.ones((tm, tm), bool)), dtype=BF16)
    gs = pltpu.PrefetchScalarGridSpec(
        num_scalar_prefetch=3,
        grid=(T // tm,),
        in_specs=[pl.BlockSpec((tm, D_MODEL), lambda i, r, p, f: (jnp.minimum(i, npt - 1), 0)),
                  pl.BlockSpec((tm, D_MODEL), lambda i, r, p, f: (jnp.maximum(i - npt, 0), 0)),
                  pl.BlockSpec((tm, D_MODEL), lambda i, r, p, f: (p[i], 0)),
                  pl.BlockSpec((8, 6 * D_MODEL), lambda i, r, p, f: (0, 0)),
                  pl.BlockSpec((D_MODEL, n_main), lambda i, r, p, f: (0, 0)),
                  pl.BlockSpec((D_MODEL, 128), lambda i, r, p, f: (0, 0)),
                  pl.BlockSpec((1, 2 * N_HEADS), lambda i, r, p, f: (0, 0)),
                  pl.BlockSpec((1, 2 * N_HEADS), lambda i, r, p, f: (0, 0)),
                  pl.BlockSpec((tm, tm), lambda i, r, p, f: (0, 0))],
        out_specs=[pl.BlockSpec((tm, D_CONV), lambda i, r, p, f: (i, 0)),
                   pl.BlockSpec((tm, D_SSM), lambda i, r, p, f: (i, 0)),
                   pl.BlockSpec((tm, D_XBC), lambda i, r, p, f: (i, 0)),
                   pl.BlockSpec((tm, 128), lambda i, r, p, f: (i, 0))])
    return pl.pallas_call(
        _inproj_kernel, grid_spec=gs,
        out_shape=(jax.ShapeDtypeStruct((T, D_CONV), BF16),
                   jax.ShapeDtypeStruct((T, D_SSM), BF16),
                   jax.ShapeDtypeStruct((T, D_XBC), BF16),
                   jax.ShapeDtypeStruct((T, 128), F32)),
        compiler_params=_cparams(("arbitrary",)),
        name="inproj",
    )(row, posb, flag, xp, xs, pos, mod, w_main, w_dt, dt_bias.reshape(1, -1), a_log.reshape(1, -1), tri)


_N_SHIFT = 8
_SHIFT_ROWS = CONV_TILE + 2 * HALO - _N_SHIFT
_ROW_BLOCK = 64
_FILL_ROWS = 32
_SSM_ROWS, _SSM_LANES = 64, 256


def _conv_kernel(lok_ref, rok_ref, u_ref, ul_ref, ur_ref, xbc_ref, xbcl_ref, xbcr_ref,
                 cw_ref, cb_ref, lng_ref, lnb_ref, sw_ref, sb_ref, co_ref, xo_ref,
                 ext_ref, sh_ref, acc_ref, ext2_ref):
    i = pl.program_id(0)
    lok = lok_ref[i] == 1
    rok = rok_ref[i] == 1

    def fill_ext(rb, carry):
        r0 = pl.multiple_of(rb * _FILL_ROWS, _FILL_ROWS)
        dst = pl.ds(pl.multiple_of(HALO + r0, HALO), _FILL_ROWS)
        ext_ref[dst, :] = u_ref[pl.ds(r0, _FILL_ROWS), :].astype(F32)
        ext2_ref[dst, :] = xbc_ref[pl.ds(r0, _FILL_ROWS), :].astype(F32)
        return carry

    ext_ref[0:HALO, :] = jnp.where(lok, ul_ref[...].astype(F32), 0.0)
    ext_ref[HALO + CONV_TILE:, :] = jnp.where(rok, ur_ref[...].astype(F32), 0.0)
    ext2_ref[0:HALO, :] = jnp.where(lok, xbcl_ref[...].astype(F32), 0.0)
    ext2_ref[HALO + CONV_TILE:, :] = jnp.where(rok, xbcr_ref[...].astype(F32), 0.0)
    lax.fori_loop(0, CONV_TILE // _FILL_ROWS, fill_ext, 0)
    for r in range(_N_SHIFT):
        sh_ref[r] = ext_ref[r:r + _SHIFT_ROWS, :]

    first = HALO - (CONV_K - 1) // 2

    for j in range(D_CONV // 128):
        lanes = slice(j * 128, (j + 1) * 128)
        taps = [jnp.broadcast_to(cw_ref[k:k + 1, lanes], (8, 128)) for k in range(CONV_K)]
        bias = jnp.broadcast_to(cb_ref[:, lanes], (8, 128))

        def row_block(rb, carry, lanes=lanes, taps=taps, bias=bias):
            base = pl.multiple_of(rb * _ROW_BLOCK, _ROW_BLOCK)
            for sub in range(_ROW_BLOCK // 8):
                acc = bias
                for k in range(CONV_K):
                    o = first + k
                    row0 = base + (o // _N_SHIFT) * _N_SHIFT + sub * 8
                    acc = acc + sh_ref[o % _N_SHIFT, pl.ds(row0, 8), lanes] * taps[k]
                acc_ref[pl.ds(base + sub * 8, 8), lanes] = acc
            return carry

        lax.fori_loop(0, CONV_TILE // _ROW_BLOCK, row_block, 0)
    u = _ln_rows(acc_ref[...]) * lng_ref[...] + lnb_ref[...]
    co_ref[...] = _silu(u).astype(BF16)

    first2 = HALO - (SSM_CONV_K - 1) // 2
    for rb in range(CONV_TILE // _SSM_ROWS):
        for lc in range(D_XBC // _SSM_LANES):
            lanes = slice(lc * _SSM_LANES, (lc + 1) * _SSM_LANES)
            y = jnp.zeros((_SSM_ROWS, _SSM_LANES), F32) + sb_ref[:, lanes]
            for k in range(SSM_CONV_K):
                r0 = first2 + k + rb * _SSM_ROWS
                y = y + ext2_ref[r0:r0 + _SSM_ROWS, lanes] * sw_ref[k:k + 1, lanes]
            xo_ref[rb * _SSM_ROWS:(rb + 1) * _SSM_ROWS, lanes] = _silu(y).astype(BF16)


def _conv(lay, u, xbc, conv_w, conv_b, ln_g, ln_b, ssm_w, ssm_b):
    T = lay.n_tokens
    lok, rok = lay.conv_tile_tables()
    n_tiles = T // CONV_TILE
    hb = CONV_TILE // HALO
    n_hb = T // HALO

    def cur(i, l, r):
        return (i, 0)

    def left(i, l, r):
        return (jnp.maximum(i * hb - 1, 0), 0)

    def right(i, l, r):
        return (jnp.minimum((i + 1) * hb, n_hb - 1), 0)

    def const(i, l, r):
        return (0, 0)

    gs = pltpu.PrefetchScalarGridSpec(
        num_scalar_prefetch=2,
        grid=(n_tiles,),
        in_specs=[pl.BlockSpec((CONV_TILE, D_CONV), cur),
                  pl.BlockSpec((HALO, D_CONV), left),
                  pl.BlockSpec((HALO, D_CONV), right),
                  pl.BlockSpec((CONV_TILE, D_XBC), cur),
                  pl.BlockSpec((HALO, D_XBC), left),
                  pl.BlockSpec((HALO, D_XBC), right),
                  pl.BlockSpec((CONV_K, D_CONV), const),
                  pl.BlockSpec((1, D_CONV), const),
                  pl.BlockSpec((1, D_CONV), const),
                  pl.BlockSpec((1, D_CONV), const),
                  pl.BlockSpec((SSM_CONV_K, D_XBC), const),
                  pl.BlockSpec((1, D_XBC), const)],
        out_specs=[pl.BlockSpec((CONV_TILE, D_CONV), cur),
                   pl.BlockSpec((CONV_TILE, D_XBC), cur)],
        scratch_shapes=[pltpu.VMEM((CONV_TILE + 2 * HALO, D_CONV), F32),
                        pltpu.VMEM((_N_SHIFT, _SHIFT_ROWS, D_CONV), F32),
                        pltpu.VMEM((CONV_TILE, D_CONV), F32),
                        pltpu.VMEM((CONV_TILE + 2 * HALO, D_XBC), F32)])
    return pl.pallas_call(
        _conv_kernel, grid_spec=gs,
        out_shape=(jax.ShapeDtypeStruct((T, D_CONV), BF16),
                   jax.ShapeDtypeStruct((T, D_XBC), BF16)),
        compiler_params=_cparams(("arbitrary",)),
        name="conv",
    )(lok, rok, u, u, u, xbc, xbc, xbc, conv_w, conv_b.reshape(1, -1), ln_g.reshape(1, -1),
      ln_b.reshape(1, -1), ssm_w, ssm_b.reshape(1, -1))


_BN = N_GROUPS * D_STATE


def _ssd_kernel(chunk_ref, yidx_ref, phase_ref, first_ref, last_ref, zero_ref, sin_ref, sout_ref, cloc_ref,
                xbc_ref, z_ref, pre_ref, h0f_ref, h0b_ref, dsk_ref, ng_ref,
                edec_ref, ewb_ref, ecol_ref, ewide_ref, eye3_ref,
                y_ref, hf_out_ref, hb_out_ref,
                hf_ref, g_ref, gin_ref, ybuf_ref):
    s = pl.program_id(0)
    phase = phase_ref[s]
    first = first_ref[s] == 1
    last = last_ref[s] == 1
    zero = zero_ref[s] == 1
    cloc = cloc_ref[s]
    H, P, N = N_HEADS, HEADDIM, D_STATE

    GW = HEADS_PER_GROUP * P
    xs = xbc_ref[:, 0:D_SSM]
    dt = pre_ref[:, 0:2 * H]
    acs = pre_ref[:, 2 * H:4 * H]
    a = pre_ref[:, 4 * H:6 * H]
    tot = acs[CHUNK - 8:CHUNK, :]
    dec = _expand_exact(jnp.exp(tot), edec_ref[...])[7:8, :]
    exb = acs[:, H:2 * H] - a[:, H:2 * H]

    def load_state(src_ref, dst_ref):
        for j in range(H // 2):
            pair = jnp.concatenate([src_ref[0, 2 * j], src_ref[0, 2 * j + 1]], axis=0)
            dst_ref[:, 2 * j * P:(2 * j + 2) * P] = jnp.where(zero, 0.0, pair.T)

    def store_state(src_ref, dst_ref):
        for j in range(H // 2):
            pair = src_ref[:, 2 * j * P:(2 * j + 2) * P].T
            dst_ref[0, 2 * j] = pair[0:P]
            dst_ref[0, 2 * j + 1] = pair[P:2 * P]

    @pl.when(phase == 0)
    def _backward_states():
        @pl.when(first)
        def _():
            load_state(h0b_ref, g_ref)

        wb = dt[:, H:2 * H] * jnp.exp(exb)
        xw = (xs.astype(F32) * _expand_exact(wb, ewb_ref[...])).astype(BF16)
        for g in range(N_GROUPS):
            cols = slice(g * GW, (g + 1) * GW)
            bg = xbc_ref[:, D_SSM + g * N:D_SSM + (g + 1) * N]
            gg = g_ref[:, cols]
            gin_ref[cloc, :, cols] = gg.astype(BF16)
            upd = lax.dot_general(bg, xw[:, cols], (((0,), (0,)), ((), ())), preferred_element_type=F32)
            g_ref[:, cols] = gg * dec[:, D_SSM + g * GW:D_SSM + (g + 1) * GW] + upd

        @pl.when(last)
        def _():
            store_state(g_ref, hb_out_ref)

    @pl.when(phase == 1)
    def _forward_and_outputs():
        @pl.when(first)
        def _():
            load_state(h0f_ref, hf_ref)

        acsf = acs[:, 0:H]
        dtf = dt[:, 0:H]
        dtb = dt[:, H:2 * H]
        totf = acs[CHUNK - 1:CHUNK, 0:H]
        totb = acs[CHUNK - 1:CHUNK, H:2 * H]
        col = _expand_exact(jnp.concatenate([acsf, exb], axis=1), ecol_ref[...])
        q3 = _split3(jnp.concatenate([acsf, exb, dtf, dtb], axis=1))
        qt = lax.dot_general(eye3_ref[...], q3, (((1,), (1,)), ((), ())),
                             preferred_element_type=F32)
        wide = jnp.concatenate([dtf * jnp.exp(totf - acsf), jnp.exp(acsf), jnp.exp(totb - exb)], axis=1)
        wide = _expand_exact(wide, ewide_ref[...])
        xsf = xs.astype(F32)
        xw = (xsf * wide[:, 0:D_SSM]).astype(BF16)
        lower = _iota((CHUNK, CHUNK), 1) <= _iota((CHUNK, CHUNK), 0)
        upper = _iota((CHUNK, CHUNK), 1) >= _iota((CHUNK, CHUNK), 0)
        for g in range(N_GROUPS):
            cols = slice(g * GW, (g + 1) * GW)
            bg = xbc_ref[:, D_SSM + g * N:D_SSM + (g + 1) * N]
            cg = xbc_ref[:, D_SSM + _BN + g * N:D_SSM + _BN + (g + 1) * N]
            cb = lax.dot_general(cg, bg, (((1,), (1,)), ((), ())), preferred_element_type=F32)
            hfg = hf_ref[:, cols]
            yf = jnp.dot(cg, hfg.astype(BF16), preferred_element_type=F32)
            yb = jnp.dot(cg, gin_ref[cloc, :, cols], preferred_element_type=F32)
            ybuf_ref[:, cols] = yf * wide[:, D_SSM + g * GW:D_SSM + (g + 1) * GW] \
                + yb * wide[:, 2 * D_SSM + g * GW:2 * D_SSM + (g + 1) * GW]
            upd = lax.dot_general(bg, xw[:, cols], (((0,), (0,)), ((), ())), preferred_element_type=F32)
            hf_ref[:, cols] = hfg * dec[:, cols] + upd
            for r in range(HEADS_PER_GROUP):
                h = g * HEADS_PER_GROUP + r
                colf = col[:, h * N:(h + 1) * N]
                colb = col[:, (H + h) * N:(H + h + 1) * N]
                mf = jnp.where(lower, jnp.exp(colf - qt[h:h + 1, :]), 0.0) * qt[2 * H + h:2 * H + h + 1, :]
                mb = jnp.where(upper, jnp.exp(qt[H + h:H + h + 1, :] - colb), 0.0) * qt[3 * H + h:3 * H + h + 1, :]
                m = (cb * (mf + mb)).astype(BF16)
                hs = slice(h * P, (h + 1) * P)
                ybuf_ref[:, hs] += jnp.dot(m, xs[:, hs], preferred_element_type=F32)

        yt = (ybuf_ref[...] + dsk_ref[...] * xsf) * z_ref[...].astype(F32)
        gw = D_SSM // N_GROUPS
        for g in range(N_GROUPS):
            seg = yt[:, g * gw:(g + 1) * gw]
            ms = jnp.mean(seg * seg, axis=-1, keepdims=True)
            y_ref[:, g * gw:(g + 1) * gw] = (seg * lax.rsqrt(ms + LN_EPS) * ng_ref[:, g * gw:(g + 1) * gw]).astype(BF16)

        @pl.when(last)
        def _():
            store_state(hf_ref, hf_out_ref)


def _ssd(lay, xbc_c, z, pre, h0f, h0b, d_skip, norm_g):
    T = lay.n_tokens
    tabs = lay.ssd_step_tables()
    n_steps = tabs[0].shape[0]
    nsp = len(tabs)

    def by_chunk(s, *t):
        return (t[0][s], 0)

    def by_y(s, *t):
        return (t[1][s], 0)

    def by_sin(s, *t):
        return (t[6][s], 0, 0, 0)

    def by_sout(s, *t):
        return (t[7][s], 0, 0, 0)

    def const(s, *t):
        return (0, 0)

    H = N_HEADS
    eye3 = jnp.asarray(np.arange(4 * H)[:, None] == np.arange(12 * H)[None, :] % (4 * H), dtype=BF16)
    consts = [_expand3(2 * H, HEADDIM), _expand3(H, HEADDIM), _expand3(2 * H, D_STATE),
              _expand3(3 * H, HEADDIM), eye3]
    sshape = (1, N_HEADS, HEADDIM, D_STATE)
    gs = pltpu.PrefetchScalarGridSpec(
        num_scalar_prefetch=nsp,
        grid=(n_steps,),
        in_specs=[pl.BlockSpec((CHUNK, D_XBC), by_chunk),
                  pl.BlockSpec((CHUNK, D_SSM), by_chunk),
                  pl.BlockSpec((CHUNK, 128), by_chunk),
                  pl.BlockSpec(sshape, by_sin),
                  pl.BlockSpec(sshape, by_sin),
                  pl.BlockSpec((1, D_SSM), const),
                  pl.BlockSpec((1, D_SSM), const)] + [pl.BlockSpec(c.shape, const) for c in consts],
        out_specs=[pl.BlockSpec((CHUNK, D_SSM), by_y),
                   pl.BlockSpec(sshape, by_sout),
                   pl.BlockSpec(sshape, by_sout)],
        scratch_shapes=[pltpu.VMEM((D_STATE, D_SSM), F32),
                        pltpu.VMEM((D_STATE, D_SSM), F32),
                        pltpu.VMEM((lay.max_chunks, D_STATE, D_SSM), BF16),
                        pltpu.VMEM((CHUNK, D_SSM), F32)])
    n_out = lay.n_prompt_seqs
    return pl.pallas_call(
        _ssd_kernel, grid_spec=gs,
        out_shape=(jax.ShapeDtypeStruct((T, D_SSM), BF16),
                   jax.ShapeDtypeStruct((n_out,) + sshape[1:], F32),
                   jax.ShapeDtypeStruct((n_out,) + sshape[1:], F32)),
        compiler_params=_cparams(("arbitrary",)),
        name="ssd",
    )(*tabs, xbc_c, z, pre, h0f, h0b,
      jnp.repeat(d_skip, HEADDIM).reshape(1, -1), norm_g.reshape(1, -1), *consts)


def _outproj_kernel(row_ref, posb_ref, flag_ref, xp_ref, xs_ref, pos_ref, mod_ref, co_ref, ys_ref, wo_ref,
                    g_ref, b_ref, x1_ref, h2_ref, h2p_ref, *, alpha):
    i = pl.program_id(0)
    x = jnp.where(flag_ref[i] == 1, xs_ref[...] + pos_ref[...], xp_ref[...])
    r = row_ref[i]
    g1 = mod_ref[pl.ds(r, 1), 2 * D_MODEL:3 * D_MODEL]
    sh2 = mod_ref[pl.ds(r, 1), 3 * D_MODEL:4 * D_MODEL]
    sc2 = mod_ref[pl.ds(r, 1), 4 * D_MODEL:5 * D_MODEL]
    mix = jnp.dot(co_ref[...], wo_ref[0:D_CONV, :], preferred_element_type=F32) \
        + jnp.dot(ys_ref[...], wo_ref[D_CONV:, :], preferred_element_type=F32)
    x1 = _ln_rows(alpha * x + g1 * mix) * g_ref[...] + b_ref[...]
    x1_ref[...] = x1
    h2 = _ln_rows(x1) * (1.0 + sc2) + sh2
    h2_ref[...] = h2
    packed = _pack_halves(h2)
    for c in range(ROW_PARTS):
        h2p_ref[c] = packed[:, c * PART_WORDS:(c + 1) * PART_WORDS]


def _outproj(lay, xp, xs, pos, mod, conv_out, y_ssm, w_out, ln_g, ln_b, alpha, tm):
    T = lay.n_tokens
    row, posb, flag = lay.token_tile_tables(tm)
    npt = lay.n_prompt_tokens // tm

    def const(i, r, p, f):
        return (0, 0)

    def cur(i, r, p, f):
        return (i, 0)

    gs = pltpu.PrefetchScalarGridSpec(
        num_scalar_prefetch=3,
        grid=(T // tm,),
        in_specs=[pl.BlockSpec((tm, D_MODEL), lambda i, r, p, f: (jnp.minimum(i, npt - 1), 0)),
                  pl.BlockSpec((tm, D_MODEL), lambda i, r, p, f: (jnp.maximum(i - npt, 0), 0)),
                  pl.BlockSpec((tm, D_MODEL), lambda i, r, p, f: (p[i], 0)),
                  pl.BlockSpec((8, 6 * D_MODEL), const),
                  pl.BlockSpec((tm, D_CONV), cur),
                  pl.BlockSpec((tm, D_SSM), cur),
                  pl.BlockSpec((D_CONV + D_SSM, D_MODEL), const),
                  pl.BlockSpec((1, D_MODEL), const),
                  pl.BlockSpec((1, D_MODEL), const)],
        out_specs=[pl.BlockSpec((tm, D_MODEL), cur),
                   pl.BlockSpec((tm, D_MODEL), cur),
                   pl.BlockSpec((ROW_PARTS, tm, PART_WORDS), lambda i, r, p, f: (0, i, 0))])
    return pl.pallas_call(
        functools.partial(_outproj_kernel, alpha=alpha), grid_spec=gs,
        out_shape=(jax.ShapeDtypeStruct((T, D_MODEL), F32),
                   jax.ShapeDtypeStruct((T, D_MODEL), F32),
                   jax.ShapeDtypeStruct((ROW_PARTS, T, PART_WORDS), jnp.uint32)),
        compiler_params=_cparams(("arbitrary",)),
        name="outproj",
    )(row, posb, flag, xp, xs, pos, mod, conv_out, y_ssm, w_out, ln_g.reshape(1, -1), ln_b.reshape(1, -1))


def _route_kernel(h2_ref, wrt_ref, bias_ref, idx_ref, wts_ref, pos_ref, cnt_ref, carry_ref, *, tm):
    i = pl.program_id(0)

    @pl.when(i == 0)
    def _():
        carry_ref[...] = jnp.zeros_like(carry_ref)

    E, NG, EG = N_EXPERTS, N_EXPERT_GROUPS, EXPERTS_PER_GROUP
    neg = -jnp.inf
    h = h2_ref[...]
    h_hi = h.astype(BF16)
    h_lo = (h - h_hi.astype(F32)).astype(BF16)
    h3 = jnp.concatenate([h_hi, h_lo, h_hi], axis=1)
    logits = lax.dot_general(wrt_ref[...], h3, (((1,), (1,)), ((), ())),
                             preferred_element_type=F32)
    s = jax.nn.sigmoid(logits)
    sel = s + bias_ref[...]
    sel3 = sel.reshape(NG, EG, tm)
    io3 = _iota((NG, EG, tm), 1)
    m1 = jnp.max(sel3, axis=1, keepdims=True)
    f1 = jnp.min(jnp.where(sel3 == m1, io3, EG), axis=1, keepdims=True)
    m2 = jnp.max(jnp.where(io3 == f1, neg, sel3), axis=1, keepdims=True)
    gscore = (m1 + m2).reshape(NG, tm)
    gio = _iota((NG, tm), 0)
    beaten = jnp.zeros((NG, tm), I32)
    for g in range(NG):
        row = gscore[g:g + 1, :]
        beats = jnp.where(row > gscore, 1, jnp.where(row == gscore, jnp.where(g < gio, 1, 0), 0))
        beaten = beaten + beats
    keep = (beaten < TOPK_GROUPS).astype(F32).reshape(NG, 1, tm)
    selm = jnp.where(keep > 0.5, sel3, neg).reshape(E, tm)
    eio = _iota((E, tm), 0)
    chosen = jnp.zeros((E, tm), F32)
    idxs, ws = [], []
    for k in range(TOP_K):
        m = jnp.max(selm, axis=0, keepdims=True)
        am = jnp.minimum(jnp.min(jnp.where(selm == m, eio, E), axis=0, keepdims=True), E - 1)
        hit = eio == am
        ws.append(jnp.sum(jnp.where(hit, s, 0.0), axis=0, keepdims=True))
        idxs.append(am)
        selm = jnp.where(hit, neg, selm)
        chosen = jnp.where(hit, 1.0, chosen)
    wsum = ws[0]
    for k in range(1, TOP_K):
        wsum = wsum + ws[k]
    before = (_iota((tm, tm), 0) < _iota((tm, tm), 1)).astype(BF16)
    prior = jnp.dot(chosen.astype(BF16), before, preferred_element_type=F32)
    carry = carry_ref[...]
    prior = prior + jnp.concatenate([carry] * (tm // 128), axis=1)
    for k in range(TOP_K):
        idx_ref[k:k + 1, :] = idxs[k]
        wts_ref[k:k + 1, :] = ws[k] / wsum * ROUTED_SCALE
        pos_ref[k:k + 1, :] = jnp.sum(jnp.where(eio == idxs[k], prior, 0.0), axis=0, keepdims=True).astype(I32)
    total = jnp.dot(chosen.astype(BF16), jnp.ones((tm, 128), BF16), preferred_element_type=F32)
    carry = carry + total
    carry_ref[...] = carry
    cnt_ref[...] = carry.astype(I32)


def _route(h2, w_router_t, router_bias, tm):
    T = h2.shape[0]
    bias_b = jnp.broadcast_to(router_bias.astype(F32)[:, None], (N_EXPERTS, tm))
    w_hi = w_router_t.astype(BF16)
    w_lo = (w_router_t - w_hi.astype(F32)).astype(BF16)
    w_router_t = jnp.concatenate([w_hi, w_hi, w_lo], axis=1)
    return pl.pallas_call(
        functools.partial(_route_kernel, tm=tm),
        grid=(T // tm,),
        in_specs=[pl.BlockSpec((tm, D_MODEL), lambda i: (i, 0)),
                  pl.BlockSpec((N_EXPERTS, 3 * D_MODEL), lambda i: (0, 0)),
                  pl.BlockSpec((N_EXPERTS, tm), lambda i: (0, 0))],
        out_specs=[pl.BlockSpec((TOP_K, tm), lambda i: (0, i)),
                   pl.BlockSpec((TOP_K, tm), lambda i: (0, i)),
                   pl.BlockSpec((TOP_K, tm), lambda i: (0, i)),
                   pl.BlockSpec((N_EXPERTS, 128), lambda i: (0, 0))],
        out_shape=(jax.ShapeDtypeStruct((TOP_K, T), I32),
                   jax.ShapeDtypeStruct((TOP_K, T), F32),
                   jax.ShapeDtypeStruct((TOP_K, T), I32),
                   jax.ShapeDtypeStruct((N_EXPERTS, 128), I32)),
        scratch_shapes=[pltpu.VMEM((N_EXPERTS, 128), F32)],
        compiler_params=_cparams(("arbitrary",)),
        name="route",
    )(h2, w_router_t, bias_b)


def _dest_kernel(idx_ref, pos_ref, start_ref, dest_ref):
    tm = idx_ref.shape[1]
    eio = _iota((N_EXPERTS, tm), 0)
    start = start_ref[...]
    for k in range(TOP_K):
        base = jnp.sum(jnp.where(eio == idx_ref[k:k + 1, :], start, 0.0), axis=0, keepdims=True)
        dest_ref[k:k + 1, :] = base.astype(I32) + pos_ref[k:k + 1, :]


def _dest(idx, pos, pad_start, tm):
    T = idx.shape[1]
    start_b = jnp.broadcast_to(pad_start.astype(F32)[:, None], (N_EXPERTS, tm))
    return pl.pallas_call(
        _dest_kernel,
        grid=(T // tm,),
        in_specs=[pl.BlockSpec((TOP_K, tm), lambda i: (0, i)),
                  pl.BlockSpec((TOP_K, tm), lambda i: (0, i)),
                  pl.BlockSpec((N_EXPERTS, tm), lambda i: (0, 0))],
        out_specs=pl.BlockSpec((TOP_K, tm), lambda i: (0, i)),
        out_shape=jax.ShapeDtypeStruct((TOP_K, T), I32),
        compiler_params=_cparams(("arbitrary",)),
        name="dest",
    )(idx, pos, start_b)


_WEIGHT_FETCH_CHUNKS = 4
_W_SLOTS = 8
_X_SLOTS = 4


def _expert_kernel(base_ref, nblk_ref, nused_ref, x_hbm, wgu_hbm, wd_hbm, y_hbm,
                   wgu_bf, wd_bf, wgu_stage, wd_stage, xbuf, ybuf, sems, xsems, ysems):
    e = pl.program_id(0)
    n_used = nused_ref[0]

    def x_copy(b, slot):
        rows = pl.ds(pl.multiple_of(b * EXPERT_BLOCK, EXPERT_BLOCK), EXPERT_BLOCK)
        return pltpu.make_async_copy(x_hbm.at[:, rows, :], xbuf.at[slot], xsems.at[slot])

    def y_copy(b, slot):
        rows = pl.ds(pl.multiple_of(b * EXPERT_BLOCK, EXPERT_BLOCK), EXPERT_BLOCK)
        return pltpu.make_async_copy(ybuf.at[slot], y_hbm.at[:, rows, :], ysems.at[slot])

    def fetch(ex):
        ws = lax.rem(ex, _W_SLOTS)
        cps = []
        for c in range(_WEIGHT_FETCH_CHUNKS):
            rg = pl.ds(c * (D_MODEL // _WEIGHT_FETCH_CHUNKS), D_MODEL // _WEIGHT_FETCH_CHUNKS)
            rd = pl.ds(c * (D_EXPERT // _WEIGHT_FETCH_CHUNKS), D_EXPERT // _WEIGHT_FETCH_CHUNKS)
            cps.append(pltpu.make_async_copy(wgu_hbm.at[ex, rg], wgu_stage.at[ws, rg], sems.at[ws, 0]))
            cps.append(pltpu.make_async_copy(wd_hbm.at[ex, rd], wd_stage.at[ws, rd], sems.at[ws, 1]))
        return cps

    @pl.when(e == 0)
    def _():
        for ex in range(_W_SLOTS):
            for cp in fetch(ex):
                cp.start()
        for b in range(_X_SLOTS - 1):
            @pl.when(b < n_used)
            def _(b=b):
                x_copy(b, b).start()

    for cp in fetch(e):
        cp.wait()
    ws = lax.rem(e, _W_SLOTS)
    wgu_bf[...] = wgu_stage[ws].astype(BF16)
    wd_bf[...] = wd_stage[ws].astype(BF16)

    @pl.when(e + _W_SLOTS < N_EXPERTS)
    def _():
        for cp in fetch(e + _W_SLOTS):
            cp.start()

    def block(b, carry):
        xs = lax.rem(b, _X_SLOTS)
        slot = lax.rem(b, 2)
        x_copy(b, xs).wait()

        @pl.when(b + _X_SLOTS - 1 < n_used)
        def _():
            x_copy(b + _X_SLOTS - 1, lax.rem(b + _X_SLOTS - 1, _X_SLOTS)).start()

        @pl.when(b >= 2)
        def _():
            y_copy(b - 2, slot).wait()

        parts = [_unpack_halves(xbuf[xs, c]) for c in range(ROW_PARTS)]
        chunks = [p[0] for p in parts] + [p[1] for p in parts]
        gu = jnp.zeros((EXPERT_BLOCK, 2 * D_EXPERT), F32)
        for j, xc in enumerate(chunks):
            gu = gu + jnp.dot(xc.astype(BF16), wgu_bf[j * PART_WORDS:(j + 1) * PART_WORDS, :],
                              preferred_element_type=F32)
        act = (_silu(gu[:, 0:D_EXPERT]) * gu[:, D_EXPERT:]).astype(BF16)
        packed = _pack_halves(jnp.dot(act, wd_bf[...], preferred_element_type=F32))
        for c in range(ROW_PARTS):
            ybuf[slot, c] = packed[:, c * PART_WORDS:(c + 1) * PART_WORDS]
        y_copy(b, slot).start()
        return carry

    lax.fori_loop(base_ref[e], base_ref[e] + nblk_ref[e], block, 0)

    @pl.when(e == N_EXPERTS - 1)
    def _():
        @pl.when(n_used >= 2)
        def _():
            y_copy(n_used - 2, lax.rem(n_used, 2)).wait()

        @pl.when(n_used >= 1)
        def _():
            y_copy(n_used - 1, lax.rem(n_used - 1, 2)).wait()


def _pack_halves(x):
    n = x.shape[1] // 2
    hi = lax.bitcast_convert_type(x[:, :n].astype(BF16).astype(F32), jnp.uint32)
    lo = lax.bitcast_convert_type(x[:, n:].astype(BF16).astype(F32), jnp.uint32)
    return hi | (lo >> 16)


def _unpack_halves(p):
    hi = lax.bitcast_convert_type(p & jnp.uint32(0xFFFF0000), F32)
    lo = lax.bitcast_convert_type(p << 16, F32)
    return hi, lo


def _expert(x_sorted, w_gu, w_down, blk_base, blk_count, n_used):
    n_rows = x_sorted.shape[1]
    blk_shape = (ROW_PARTS, EXPERT_BLOCK, PART_WORDS)
    gs = pltpu.PrefetchScalarGridSpec(
        num_scalar_prefetch=3,
        grid=(N_EXPERTS,),
        in_specs=[pl.BlockSpec(memory_space=pl.ANY),
                  pl.BlockSpec(memory_space=pl.ANY),
                  pl.BlockSpec(memory_space=pl.ANY)],
        out_specs=pl.BlockSpec(memory_space=pl.ANY),
        scratch_shapes=[pltpu.VMEM((D_MODEL, 2 * D_EXPERT), BF16),
                        pltpu.VMEM((D_EXPERT, D_MODEL), BF16),
                        pltpu.VMEM((_W_SLOTS, D_MODEL, 2 * D_EXPERT), F32),
                        pltpu.VMEM((_W_SLOTS, D_EXPERT, D_MODEL), F32),
                        pltpu.VMEM((_X_SLOTS,) + blk_shape, jnp.uint32),
                        pltpu.VMEM((2,) + blk_shape, jnp.uint32),
                        pltpu.SemaphoreType.DMA((_W_SLOTS, 2)),
                        pltpu.SemaphoreType.DMA((_X_SLOTS,)),
                        pltpu.SemaphoreType.DMA((2,))])
    return pl.pallas_call(
        _expert_kernel, grid_spec=gs,
        out_shape=jax.ShapeDtypeStruct((ROW_PARTS, n_rows, PART_WORDS), jnp.uint32),
        compiler_params=_cparams(("arbitrary",)),
        name="expert",
    )(blk_base, blk_count, n_used, x_sorted, w_gu, w_down)


def _combine_kernel(row_ref, h2_ref, x1_ref, wt_ref, mod_ref, wsg_ref, wsd_ref, g_ref, b_ref, yt_ref,
                    o_ref, *, first_tile, alpha):
    i = pl.program_id(0)
    h2 = h2_ref[...].astype(BF16)
    su = jnp.dot(h2, wsg_ref[...], preferred_element_type=F32)
    act = (_silu(su[:, 0:D_SHARED]) * su[:, D_SHARED:]).astype(BF16)
    moe = jnp.dot(act, wsd_ref[...], preferred_element_type=F32)
    wt = wt_ref[...]
    his, los = [], []
    for c in range(ROW_PARTS):
        rh = jnp.zeros((h2.shape[0], PART_WORDS), F32)
        rl = jnp.zeros((h2.shape[0], PART_WORDS), F32)
        for k in range(TOP_K):
            hi, lo = _unpack_halves(yt_ref[c, k])
            w = wt[:, k:k + 1]
            rh = rh + hi * w
            rl = rl + lo * w
        his.append(rh)
        los.append(rl)
    moe = moe + jnp.concatenate(his + los, axis=1)
    g2 = mod_ref[pl.ds(row_ref[first_tile + i], 1), 5 * D_MODEL:6 * D_MODEL]
    o_ref[...] = _ln_rows(alpha * x1_ref[...] + g2 * moe) * g_ref[...] + b_ref[...]


def _combine(lay, h2, x1, wts_tok, mod, w_sh_gu, w_sh_down, ln_g, ln_b, y_tok, alpha, tm, first_token, n_tok):
    row, _, _ = lay.token_tile_tables(tm)
    first_tile = first_token // tm

    def cur(i, r):
        return (first_tile + i, 0)

    def const(i, r):
        return (0, 0)

    gs = pltpu.PrefetchScalarGridSpec(
        num_scalar_prefetch=1,
        grid=(n_tok // tm,),
        in_specs=[pl.BlockSpec((tm, D_MODEL), cur),
                  pl.BlockSpec((tm, D_MODEL), cur),
                  pl.BlockSpec((tm, TOP_K), cur),
                  pl.BlockSpec((8, 6 * D_MODEL), const),
                  pl.BlockSpec((D_MODEL, 2 * D_SHARED), const),
                  pl.BlockSpec((D_SHARED, D_MODEL), const),
                  pl.BlockSpec((1, D_MODEL), const),
                  pl.BlockSpec((1, D_MODEL), const),
                  pl.BlockSpec((ROW_PARTS, TOP_K, tm, PART_WORDS), lambda i, r: (0, 0, i, 0))],
        out_specs=pl.BlockSpec((tm, D_MODEL), lambda i, r: (i, 0)))
    return pl.pallas_call(
        functools.partial(_combine_kernel, first_tile=first_tile, alpha=alpha), grid_spec=gs,
        out_shape=jax.ShapeDtypeStruct((n_tok, D_MODEL), F32),
        compiler_params=_cparams(("arbitrary",)),
        name="combine",
    )(row, h2, x1, wts_tok, mod, w_sh_gu, w_sh_down, ln_g.reshape(1, -1), ln_b.reshape(1, -1), y_tok)


_GATHER_WINDOW = 128


def _sc_gather(table, idx):
    n, d = idx.shape[0], table.shape[1]
    mesh = plsc.VectorSubcoreMesh(core_axis_name="core", subcore_axis_name="subcore")

    @pl.kernel(out_type=jax.ShapeDtypeStruct((n, d), table.dtype), mesh=mesh)
    def gather_kernel(table_hbm, idx_hbm, out_hbm):
        def body(idx_vmem, out_vmem):
            pltpu.sync_copy(table_hbm.at[idx_vmem.at[0]], out_vmem)

        pltpu.emit_pipeline(
            body,
            grid=(n // _GATHER_WINDOW,),
            in_specs=[pl.BlockSpec((1, _GATHER_WINDOW), index_map=lambda i: (0, i))],
            out_specs=[pl.BlockSpec((_GATHER_WINDOW, d), index_map=lambda i: (i, 0))],
            core_axis_name=("core", "subcore"),
            dimension_semantics=(pltpu.PARALLEL,),
        )(idx_hbm, out_hbm)

    return gather_kernel(table, idx.reshape(1, n))


def _sc_scatter(rows, idx, n_out, repeat):
    n, d = rows.shape
    mesh = plsc.VectorSubcoreMesh(core_axis_name="core", subcore_axis_name="subcore")

    @pl.kernel(out_type=jax.ShapeDtypeStruct((n_out, d), rows.dtype), mesh=mesh, scratch_types=[])
    def scatter_kernel(rows_hbm, idx_hbm, out_hbm):
        def body(rows_vmem, idx_vmem):
            for r in range(repeat):
                pltpu.sync_copy(rows_vmem, out_hbm.at[idx_vmem.at[r]])

        pltpu.emit_pipeline(
            body,
            grid=(n // _GATHER_WINDOW,),
            in_specs=[pl.BlockSpec((_GATHER_WINDOW, d), index_map=lambda i: (i, 0)),
                      pl.BlockSpec((repeat, _GATHER_WINDOW), index_map=lambda i: (0, i))],
            out_specs=[],
            core_axis_name=("core", "subcore"),
            dimension_semantics=(pltpu.PARALLEL,),
        )(rows_hbm, idx_hbm)

    return scatter_kernel(rows, idx.reshape(repeat, n))


class _Layout:
    def __init__(self, n_prompt_seqs, prompt_len, n_sample_seqs, sample_len):
        self.n_prompt_seqs, self.prompt_len = n_prompt_seqs, prompt_len
        self.n_sample_seqs, self.sample_len = n_sample_seqs, sample_len
        self.n_prompt_tokens = n_prompt_seqs * prompt_len
        self.n_tokens = self.n_prompt_tokens + n_sample_seqs * sample_len
        assert prompt_len % CONV_TILE == 0 and sample_len % CONV_TILE == 0
        self.max_chunks = max(prompt_len, sample_len) // CHUNK

    def token_tile_tables(self, tm):
        assert self.n_prompt_tokens % tm == 0 and self.sample_len % tm == 0
        npt = self.n_prompt_tokens // tm
        per_seq = self.sample_len // tm
        n = self.n_tokens // tm
        row = np.zeros(n, np.int32)
        posb = np.zeros(n, np.int32)
        flag = np.zeros(n, np.int32)
        for i in range(npt, n):
            j = i - npt
            row[i] = 1 + j // per_seq
            posb[i] = j % per_seq
            flag[i] = 1
        return jnp.asarray(row), jnp.asarray(posb), jnp.asarray(flag)

    def conv_tile_tables(self):
        lok, rok = [], []
        for n_seq, length in ((self.n_prompt_seqs, self.prompt_len), (self.n_sample_seqs, self.sample_len)):
            per = length // CONV_TILE
            for _ in range(n_seq):
                for j in range(per):
                    lok.append(int(j > 0))
                    rok.append(int(j < per - 1))
        return jnp.asarray(np.array(lok, np.int32)), jnp.asarray(np.array(rok, np.int32))

    def ssd_step_tables(self):
        cols = [[] for _ in range(9)]
        seqs = []
        c0 = self.n_prompt_tokens // CHUNK
        for j in range(self.n_sample_seqs):
            nc = self.sample_len // CHUNK
            seqs.append((c0 + j * nc, nc, 0, j, 0))
        for j in range(self.n_prompt_seqs):
            nc = self.prompt_len // CHUNK
            seqs.append((j * nc, nc, 1, 0, j))
        for base, nc, zero, sin, sout in seqs:
            for phase in (0, 1):
                order = range(nc - 1, -1, -1) if phase == 0 else range(nc)
                for n, c in enumerate(order):
                    vals = (base + c, base if phase == 0 else base + c, phase, int(n == 0), int(n == nc - 1),
                            zero, sin, sout, c)
                    for col, v in zip(cols, vals):
                        col.append(v)
        return tuple(jnp.asarray(np.array(col, np.int32)) for col in cols)


def _grid_pos_embed(n_tokens):
    rows = n_tokens // GRID_W
    quarter = D_MODEL // 4
    freq = jnp.exp(-math.log(10000.0) * jnp.arange(quarter, dtype=F32) / quarter)
    r = jnp.broadcast_to(jnp.arange(rows, dtype=F32)[:, None, None] * freq, (rows, GRID_W, quarter))
    cl = jnp.broadcast_to(jnp.arange(GRID_W, dtype=F32)[None, :, None] * freq, (rows, GRID_W, quarter))
    emb = jnp.concatenate([jnp.sin(r), jnp.cos(r), jnp.sin(cl), jnp.cos(cl)], axis=-1)
    return emb.reshape(rows * GRID_W, D_MODEL)


def _moe_plan(counts):
    blk_count = (counts + EXPERT_BLOCK - 1) // EXPERT_BLOCK
    blk_end = jnp.cumsum(blk_count)
    blk_base = blk_end - blk_count
    return ((blk_base * EXPERT_BLOCK).astype(I32), blk_base.astype(I32), blk_count.astype(I32),
            blk_end[-1:].astype(I32))


def _layer(lay, xp, xs, pos, cond8, h0f, h0b, lp, alpha, tm_proj=512, tm_route=256, tm_comb=256):
    (w_ada, b_ada, w_in, conv_w, conv_b, conv_ln_g, conv_ln_b, ssm_conv_w, ssm_conv_b, dt_bias, a_log,
     d_skip, ssm_norm_g, w_out, ln1_g, ln1_b, w_router, router_bias, w_exp_gu, w_exp_down, w_sh_gu,
     w_sh_down, ln2_g, ln2_b) = lp
    T = lay.n_tokens
    n_main = 2 * D_CONV + D_SSM + D_XBC
    w_main = w_in[:, :n_main].astype(BF16)
    w_dt = jnp.pad(w_in[:, n_main:], ((0, 0), (0, 128 - 2 * N_HEADS))).astype(BF16)

    mod = _ada(cond8, w_ada, b_ada)
    u, z, xbc, pre = _inproj(lay, xp, xs, pos, mod, w_main, w_dt, dt_bias, a_log, tm_proj)
    conv_out, xbc_c = _conv(lay, u, xbc, conv_w, conv_b, conv_ln_g, conv_ln_b, ssm_conv_w, ssm_conv_b)
    y_ssm, hf, hb = _ssd(lay, xbc_c, z, pre, h0f, h0b, d_skip, ssm_norm_g)
    x1, h2, h2p = _outproj(lay, xp, xs, pos, mod, conv_out, y_ssm, w_out.astype(BF16), ln1_g, ln1_b, alpha, tm_proj)

    idx, wts, posn, cnt = _route(h2, w_router.T, router_bias, tm_route)
    n_blocks = -(-T * TOP_K // EXPERT_BLOCK) + N_EXPERTS
    pad_start, blk_base, blk_count, n_used = _moe_plan(cnt[:, 0])
    dest2 = _dest(idx, posn, pad_start, 512)
    n_rows = n_blocks * EXPERT_BLOCK
    scatter_idx = jnp.concatenate([dest2 + c * n_rows for c in range(ROW_PARTS)], axis=1)
    x_sorted = _sc_scatter(h2p.reshape(ROW_PARTS * T, PART_WORDS), scatter_idx.reshape(-1),
                           ROW_PARTS * n_rows, TOP_K).reshape(ROW_PARTS, n_rows, PART_WORDS)
    y_sorted = _expert(x_sorted, w_exp_gu, w_exp_down, blk_base, blk_count, n_used)
    y_flat = y_sorted.reshape(ROW_PARTS * n_rows, PART_WORDS)
    wts_tok, wsg, wsd = wts.T, w_sh_gu.astype(BF16), w_sh_down.astype(BF16)
    outs = []
    for first, n_tok in ((0, lay.n_prompt_tokens), (lay.n_prompt_tokens, T - lay.n_prompt_tokens)):
        dest_g = dest2[:, first:first + n_tok].reshape(-1)
        idx_parts = jnp.concatenate([dest_g + c * n_rows for c in range(ROW_PARTS)])
        y_tok = _sc_gather(y_flat, idx_parts).reshape(ROW_PARTS, TOP_K, n_tok, PART_WORDS)
        outs.append(_combine(lay, h2, x1, wts_tok, mod, wsg, wsd, ln2_g, ln2_b, y_tok, alpha, tm_comb,
                             first, n_tok))
    return outs[0], outs[1], hf, hb


def kernel(x_prompt, x_sample, state_ssd_fwd, state_ssd_bwd, c, c_ctx, w_ada, b_ada, w_in, conv_w, conv_b, conv_ln_g, conv_ln_b, ssm_conv_w, ssm_conv_b, dt_bias, a_log, d_skip, ssm_norm_g, w_out, ln1_g, ln1_b, w_router, router_bias, w_exp_gu, w_exp_down, w_sh_gu, w_sh_down, ln2_g, ln2_b):
    depth = w_ada.shape[0]
    assert depth == 1, "the prompt and latent passes are fused per layer; one layer is supported"
    bp, lp_, _ = x_prompt.shape
    bd, ld, _ = x_sample.shape
    lay = _Layout(bp, lp_, bd, ld)
    alpha = (2.0 * depth) ** 0.25
    stacked = (w_ada, b_ada, w_in, conv_w, conv_b, conv_ln_g, conv_ln_b, ssm_conv_w, ssm_conv_b,
               dt_bias, a_log, d_skip, ssm_norm_g, w_out, ln1_g, ln1_b, w_router, router_bias,
               w_exp_gu, w_exp_down, w_sh_gu, w_sh_down, ln2_g, ln2_b)
    lp = [w[0] for w in stacked]
    cond8 = jnp.concatenate([c_ctx[None, :], c, jnp.zeros((8 - 1 - bd, D_MODEL), F32)], axis=0)
    pos = _grid_pos_embed(ld)
    sshape = (bd, N_HEADS, HEADDIM, D_STATE)
    out_p, out_s, hf, hb = _layer(lay, x_prompt.reshape(bp * lp_, D_MODEL), x_sample.reshape(bd * ld, D_MODEL),
                                  pos, cond8, state_ssd_fwd[:, 0].reshape(sshape),
                                  state_ssd_bwd[:, 0].reshape(sshape), lp, alpha)
    return (out_p.reshape(bp, lp_, D_MODEL), out_s.reshape(bd, ld, D_MODEL),
            hf[:, None], hb[:, None])
```

```python
import functools
import math

import numpy as np
import jax
import jax.numpy as jnp
from jax import lax
from jax.experimental import pallas as pl
from jax.experimental.pallas import tpu as pltpu
from jax.experimental.pallas import tpu_sc as plsc

F32 = jnp.float32
BF16 = jnp.bfloat16
I32 = jnp.int32
HI = lax.Precision.HIGHEST

D_MODEL = 1024
GRID_W = 64
D_CONV = 1024
CONV_K = 31
N_HEADS = 16
HEADDIM = 64
D_SSM = N_HEADS * HEADDIM
N_GROUPS = 4
HEADS_PER_GROUP = N_HEADS // N_GROUPS
D_STATE = 128
SSM_CONV_K = 4
CHUNK = 128
D_XBC = D_SSM + 2 * N_GROUPS * D_STATE
N_EXPERTS = 256
TOP_K = 8
N_EXPERT_GROUPS = 8
EXPERTS_PER_GROUP = N_EXPERTS // N_EXPERT_GROUPS
TOPK_GROUPS = 4
D_EXPERT = 256
D_SHARED = 256
ROUTED_SCALE = 2.5
LN_EPS = 1e-5

CONV_TILE = 256
HALO = 16
EXPERT_BLOCK = 256
ROW_PARTS = 2
PART_WORDS = D_MODEL // 2 // ROW_PARTS
VMEM_LIMIT = 56 * 1024 * 1024


def _cparams(sem, vmem=VMEM_LIMIT):
    return pltpu.CompilerParams(dimension_semantics=sem, vmem_limit_bytes=vmem)


def _silu(x):
    return x * jax.nn.sigmoid(x)


def _ln_rows(x):
    mu = jnp.mean(x, axis=-1, keepdims=True)
    xc = x - mu
    var = jnp.mean(xc * xc, axis=-1, keepdims=True)
    return xc * lax.rsqrt(var + LN_EPS)


def _iota(shape, dim):
    return lax.broadcasted_iota(I32, shape, dim)


def _expand_matrix(n_in, width):
    return (_iota((n_in, n_in * width), 0) == _iota((n_in, n_in * width), 1) // width).astype(F32)


def _dot_hi(a, b):
    return jnp.dot(a, b, precision=HI, preferred_element_type=F32)


def _split3(x):
    hi = x.astype(BF16)
    r1 = x - hi.astype(F32)
    mid = r1.astype(BF16)
    lo = (r1 - mid.astype(F32)).astype(BF16)
    return jnp.concatenate([hi, mid, lo], axis=1)


def _expand3(n, width):
    rows = np.arange(3 * n)[:, None] % n
    cols = np.arange(n * width)[None, :] // width
    return jnp.asarray(rows == cols, dtype=BF16)


def _expand_exact(x, e3):
    return jnp.dot(_split3(x), e3, preferred_element_type=F32)


def _ada_kernel(c_ref, w_ref, b_ref, o_ref):
    o_ref[...] = _dot_hi(_silu(c_ref[...]), w_ref[...]) + b_ref[...]


def _ada(cond8, w_ada, b_ada):
    n = w_ada.shape[1]
    tn = 1024
    return pl.pallas_call(
        _ada_kernel,
        grid=(n // tn,),
        in_specs=[pl.BlockSpec((8, D_MODEL), lambda j: (0, 0)),
                  pl.BlockSpec((D_MODEL, tn), lambda j: (0, j)),
                  pl.BlockSpec((1, tn), lambda j: (0, j))],
        out_specs=pl.BlockSpec((8, tn), lambda j: (0, j)),
        out_shape=jax.ShapeDtypeStruct((8, n), F32),
        compiler_params=_cparams(("arbitrary",)),
        name="ada",
    )(cond8, w_ada, b_ada.reshape(1, n))


def _inproj_kernel(row_ref, posb_ref, flag_ref, xp_ref, xs_ref, pos_ref, mod_ref, wm_ref, wdt_ref,
                   dtb_ref, alog_ref, tri_ref, u_ref, z_ref, xbc_ref, pre_ref):
    i = pl.program_id(0)
    x = jnp.where(flag_ref[i] == 1, xs_ref[...] + pos_ref[...], xp_ref[...])
    r = row_ref[i]
    sh1 = mod_ref[pl.ds(r, 1), 0:D_MODEL]
    sc1 = mod_ref[pl.ds(r, 1), D_MODEL:2 * D_MODEL]
    h = (_ln_rows(x) * (1.0 + sc1) + sh1).astype(BF16)
    glu_a = jnp.dot(h, wm_ref[:, 0:D_CONV], preferred_element_type=F32)
    glu_g = jnp.dot(h, wm_ref[:, D_CONV:2 * D_CONV], preferred_element_type=F32)
    u_ref[...] = (glu_a * jax.nn.sigmoid(glu_g)).astype(BF16)
    z = jnp.dot(h, wm_ref[:, 2 * D_CONV:2 * D_CONV + D_SSM], preferred_element_type=F32)
    z_ref[...] = _silu(z).astype(BF16)
    xbc_ref[...] = jnp.dot(h, wm_ref[:, 2 * D_CONV + D_SSM:], preferred_element_type=F32).astype(BF16)
    nh2 = 2 * N_HEADS
    dt = jnp.dot(h, wdt_ref[...], preferred_element_type=F32)[:, 0:nh2] + dtb_ref[...]
    dt = jnp.maximum(dt, 0.0) + jnp.log1p(jnp.exp(-jnp.abs(dt)))
    a = dt * (-jnp.exp(alog_ref[...]))
    a3 = jnp.dot(tri_ref[...], _split3(a), preferred_element_type=F32)
    acs = a3[:, 0:nh2] + a3[:, nh2:2 * nh2] + a3[:, 2 * nh2:3 * nh2]
    pre_ref[...] = jnp.concatenate([dt, acs, a, jnp.zeros((dt.shape[0], 128 - 3 * nh2), F32)], axis=1)


def _inproj(lay, xp, xs, pos, mod, w_main, w_dt, dt_bias, a_log, tm):
    T = lay.n_tokens
    row, posb, flag = lay.token_tile_tables(tm)
    npt = lay.n_prompt_tokens // tm
    n_main = w_main.shape[1]
    chunk_of = np.arange(tm) // CHUNK
    tri = jnp.asarray((chunk_of[:, None] == chunk_of[None, :]) & np.tril(np.ones((tm, tm), bool)), dtype=BF16)
    gs = pltpu.PrefetchScalarGridSpec(
        num_scalar_prefetch=3,
        grid=(T // tm,),
        in_specs=[pl.BlockSpec((tm, D_MODEL), lambda i, r, p, f: (jnp.minimum(i, npt - 1), 0)),
                  pl.BlockSpec((tm, D_MODEL), lambda i, r, p, f: (jnp.maximum(i - npt, 0), 0)),
                  pl.BlockSpec((tm, D_MODEL), lambda i, r, p, f: (p[i], 0)),
                  pl.BlockSpec((8, 6 * D_MODEL), lambda i, r, p, f: (0, 0)),
                  pl.BlockSpec((D_MODEL, n_main), lambda i, r, p, f: (0, 0)),
                  pl.BlockSpec((D_MODEL, 128), lambda i, r, p, f: (0, 0)),
                  pl.BlockSpec((1, 2 * N_HEADS), lambda i, r, p, f: (0, 0)),
                  pl.BlockSpec((1, 2 * N_HEADS), lambda i, r, p, f: (0, 0)),
                  pl.BlockSpec((tm, tm), lambda i, r, p, f: (0, 0))],
        out_specs=[pl.BlockSpec((tm, D_CONV), lambda i, r, p, f: (i, 0)),
                   pl.BlockSpec((tm, D_SSM), lambda i, r, p, f: (i, 0)),
                   pl.BlockSpec((tm, D_XBC), lambda i, r, p, f: (i, 0)),
                   pl.BlockSpec((tm, 128), lambda i, r, p, f: (i, 0))])
    return pl.pallas_call(
        _inproj_kernel, grid_spec=gs,
        out_shape=(jax.ShapeDtypeStruct((T, D_CONV), BF16),
                   jax.ShapeDtypeStruct((T, D_SSM), BF16),
                   jax.ShapeDtypeStruct((T, D_XBC), BF16),
                   jax.ShapeDtypeStruct((T, 128), F32)),
        compiler_params=_cparams(("arbitrary",)),
        name="inproj",
    )(row, posb, flag, xp, xs, pos, mod, w_main, w_dt, dt_bias.reshape(1, -1), a_log.reshape(1, -1), tri)


_N_SHIFT = 8
_SHIFT_ROWS = CONV_TILE + 2 * HALO - _N_SHIFT
_ROW_BLOCK = 64
_FILL_ROWS = 32
_SSM_ROWS, _SSM_LANES = 64, 256


def _conv_kernel(lok_ref, rok_ref, u_ref, ul_ref, ur_ref, xbc_ref, xbcl_ref, xbcr_ref,
                 cw_ref, cb_ref, lng_ref, lnb_ref, sw_ref, sb_ref, co_ref, xo_ref,
                 ext_ref, sh_ref, acc_ref, ext2_ref):
    i = pl.program_id(0)
    lok = lok_ref[i] == 1
    rok = rok_ref[i] == 1

    def fill_ext(rb, carry):
        r0 = pl.multiple_of(rb * _FILL_ROWS, _FILL_ROWS)
        dst = pl.ds(pl.multiple_of(HALO + r0, HALO), _FILL_ROWS)
        ext_ref[dst, :] = u_ref[pl.ds(r0, _FILL_ROWS), :].astype(F32)
        ext2_ref[dst, :] = xbc_ref[pl.ds(r0, _FILL_ROWS), :].astype(F32)
        return carry

    ext_ref[0:HALO, :] = jnp.where(lok, ul_ref[...].astype(F32), 0.0)
    ext_ref[HALO + CONV_TILE:, :] = jnp.where(rok, ur_ref[...].astype(F32), 0.0)
    ext2_ref[0:HALO, :] = jnp.where(lok, xbcl_ref[...].astype(F32), 0.0)
    ext2_ref[HALO + CONV_TILE:, :] = jnp.where(rok, xbcr_ref[...].astype(F32), 0.0)
    lax.fori_loop(0, CONV_TILE // _FILL_ROWS, fill_ext, 0)
    for r in range(_N_SHIFT):
        sh_ref[r] = ext_ref[r:r + _SHIFT_ROWS, :]

    first = HALO - (CONV_K - 1) // 2

    for j in range(D_CONV // 128):
        lanes = slice(j * 128, (j + 1) * 128)
        taps = [jnp.broadcast_to(cw_ref[k:k + 1, lanes], (8, 128)) for k in range(CONV_K)]
        bias = jnp.broadcast_to(cb_ref[:, lanes], (8, 128))

        def row_block(rb, carry, lanes=lanes, taps=taps, bias=bias):
            base = pl.multiple_of(rb * _ROW_BLOCK, _ROW_BLOCK)
            for sub in range(_ROW_BLOCK // 8):
                acc = bias
                for k in range(CONV_K):
                    o = first + k
                    row0 = base + (o // _N_SHIFT) * _N_SHIFT + sub * 8
                    acc = acc + sh_ref[o % _N_SHIFT, pl.ds(row0, 8), lanes] * taps[k]
                acc_ref[pl.ds(base + sub * 8, 8), lanes] = acc
            return carry

        lax.fori_loop(0, CONV_TILE // _ROW_BLOCK, row_block, 0)
    u = _ln_rows(acc_ref[...]) * lng_ref[...] + lnb_ref[...]
    co_ref[...] = _silu(u).astype(BF16)

    first2 = HALO - (SSM_CONV_K - 1) // 2
    for rb in range(CONV_TILE // _SSM_ROWS):
        for lc in range(D_XBC // _SSM_LANES):
            lanes = slice(lc * _SSM_LANES, (lc + 1) * _SSM_LANES)
            y = jnp.zeros((_SSM_ROWS, _SSM_LANES), F32) + sb_ref[:, lanes]
            for k in range(SSM_CONV_K):
                r0 = first2 + k + rb * _SSM_ROWS
                y = y + ext2_ref[r0:r0 + _SSM_ROWS, lanes] * sw_ref[k:k + 1, lanes]
            xo_ref[rb * _SSM_ROWS:(rb + 1) * _SSM_ROWS, lanes] = _silu(y).astype(BF16)


def _conv(lay, u, xbc, conv_w, conv_b, ln_g, ln_b, ssm_w, ssm_b):
    T = lay.n_tokens
    lok, rok = lay.conv_tile_tables()
    n_tiles = T // CONV_TILE
    hb = CONV_TILE // HALO
    n_hb = T // HALO

    def cur(i, l, r):
        return (i, 0)

    def left(i, l, r):
        return (jnp.maximum(i * hb - 1, 0), 0)

    def right(i, l, r):
        return (jnp.minimum((i + 1) * hb, n_hb - 1), 0)

    def const(i, l, r):
        return (0, 0)

    gs = pltpu.PrefetchScalarGridSpec(
        num_scalar_prefetch=2,
        grid=(n_tiles,),
        in_specs=[pl.BlockSpec((CONV_TILE, D_CONV), cur),
                  pl.BlockSpec((HALO, D_CONV), left),
                  pl.BlockSpec((HALO, D_CONV), right),
                  pl.BlockSpec((CONV_TILE, D_XBC), cur),
                  pl.BlockSpec((HALO, D_XBC), left),
                  pl.BlockSpec((HALO, D_XBC), right),
                  pl.BlockSpec((CONV_K, D_CONV), const),
                  pl.BlockSpec((1, D_CONV), const),
                  pl.BlockSpec((1, D_CONV), const),
                  pl.BlockSpec((1, D_CONV), const),
                  pl.BlockSpec((SSM_CONV_K, D_XBC), const),
                  pl.BlockSpec((1, D_XBC), const)],
        out_specs=[pl.BlockSpec((CONV_TILE, D_CONV), cur),
                   pl.BlockSpec((CONV_TILE, D_XBC), cur)],
        scratch_shapes=[pltpu.VMEM((CONV_TILE + 2 * HALO, D_CONV), F32),
                        pltpu.VMEM((_N_SHIFT, _SHIFT_ROWS, D_CONV), F32),
                        pltpu.VMEM((CONV_TILE, D_CONV), F32),
                        pltpu.VMEM((CONV_TILE + 2 * HALO, D_XBC), F32)])
    return pl.pallas_call(
        _conv_kernel, grid_spec=gs,
        out_shape=(jax.ShapeDtypeStruct((T, D_CONV), BF16),
                   jax.ShapeDtypeStruct((T, D_XBC), BF16)),
        compiler_params=_cparams(("arbitrary",)),
        name="conv",
    )(lok, rok, u, u, u, xbc, xbc, xbc, conv_w, conv_b.reshape(1, -1), ln_g.reshape(1, -1),
      ln_b.reshape(1, -1), ssm_w, ssm_b.reshape(1, -1))


_BN = N_GROUPS * D_STATE


def _ssd_kernel(chunk_ref, yidx_ref, phase_ref, first_ref, last_ref, zero_ref, sin_ref, sout_ref, cloc_ref,
                xbc_ref, z_ref, pre_ref, h0f_ref, h0b_ref, dsk_ref, ng_ref,
                edec_ref, ewb_ref, ecol_ref, ewide_ref, eye3_ref,
                y_ref, hf_out_ref, hb_out_ref,
                hf_ref, g_ref, gin_ref, ybuf_ref):
    s = pl.program_id(0)
    phase = phase_ref[s]
    first = first_ref[s] == 1
    last = last_ref[s] == 1
    zero = zero_ref[s] == 1
    cloc = cloc_ref[s]
    H, P, N = N_HEADS, HEADDIM, D_STATE

    GW = HEADS_PER_GROUP * P
    xs = xbc_ref[:, 0:D_SSM]
    dt = pre_ref[:, 0:2 * H]
    acs = pre_ref[:, 2 * H:4 * H]
    a = pre_ref[:, 4 * H:6 * H]
    tot = acs[CHUNK - 8:CHUNK, :]
    dec = _expand_exact(jnp.exp(tot), edec_ref[...])[7:8, :]
    exb = acs[:, H:2 * H] - a[:, H:2 * H]

    def load_state(src_ref, dst_ref):
        for j in range(H // 2):
            pair = jnp.concatenate([src_ref[0, 2 * j], src_ref[0, 2 * j + 1]], axis=0)
            dst_ref[:, 2 * j * P:(2 * j + 2) * P] = jnp.where(zero, 0.0, pair.T)

    def store_state(src_ref, dst_ref):
        for j in range(H // 2):
            pair = src_ref[:, 2 * j * P:(2 * j + 2) * P].T
            dst_ref[0, 2 * j] = pair[0:P]
            dst_ref[0, 2 * j + 1] = pair[P:2 * P]

    @pl.when(phase == 0)
    def _backward_states():
        @pl.when(first)
        def _():
            load_state(h0b_ref, g_ref)

        wb = dt[:, H:2 * H] * jnp.exp(exb)
        xw = (xs.astype(F32) * _expand_exact(wb, ewb_ref[...])).astype(BF16)
        for g in range(N_GROUPS):
            cols = slice(g * GW, (g + 1) * GW)
            bg = xbc_ref[:, D_SSM + g * N:D_SSM + (g + 1) * N]
            gg = g_ref[:, cols]
            gin_ref[cloc, :, cols] = gg.astype(BF16)
            upd = lax.dot_general(bg, xw[:, cols], (((0,), (0,)), ((), ())), preferred_element_type=F32)
            g_ref[:, cols] = gg * dec[:, D_SSM + g * GW:D_SSM + (g + 1) * GW] + upd

        @pl.when(last)
        def _():
            store_state(g_ref, hb_out_ref)

    @pl.when(phase == 1)
    def _forward_and_outputs():
        @pl.when(first)
        def _():
            load_state(h0f_ref, hf_ref)

        acsf = acs[:, 0:H]
        dtf = dt[:, 0:H]
        dtb = dt[:, H:2 * H]
        totf = acs[CHUNK - 1:CHUNK, 0:H]
        totb = acs[CHUNK - 1:CHUNK, H:2 * H]
        col = _expand_exact(jnp.concatenate([acsf, exb], axis=1), ecol_ref[...])
        q3 = _split3(jnp.concatenate([acsf, exb, dtf, dtb], axis=1))
        qt = lax.dot_general(eye3_ref[...], q3, (((1,), (1,)), ((), ())),
                             preferred_element_type=F32)
        wide = jnp.concatenate([dtf * jnp.exp(totf - acsf), jnp.exp(acsf), jnp.exp(totb - exb)], axis=1)
        wide = _expand_exact(wide, ewide_ref[...])
        xsf = xs.astype(F32)
        xw = (xsf * wide[:, 0:D_SSM]).astype(BF16)
        lower = _iota((CHUNK, CHUNK), 1) <= _iota((CHUNK, CHUNK), 0)
        upper = _iota((CHUNK, CHUNK), 1) >= _iota((CHUNK, CHUNK), 0)
        for g in range(N_GROUPS):
            cols = slice(g * GW, (g + 1) * GW)
            bg = xbc_ref[:, D_SSM + g * N:D_SSM + (g + 1) * N]
            cg = xbc_ref[:, D_SSM + _BN + g * N:D_SSM + _BN + (g + 1) * N]
            cb = lax.dot_general(cg, bg, (((1,), (1,)), ((), ())), preferred_element_type=F32)
            hfg = hf_ref[:, cols]
            yf = jnp.dot(cg, hfg.astype(BF16), preferred_element_type=F32)
            yb = jnp.dot(cg, gin_ref[cloc, :, cols], preferred_element_type=F32)
            yg = yf * wide[:, D_SSM + g * GW:D_SSM + (g + 1) * GW] \
                + yb * wide[:, 2 * D_SSM + g * GW:2 * D_SSM + (g + 1) * GW]
            upd = lax.dot_general(bg, xw[:, cols], (((0,), (0,)), ((), ())), preferred_element_type=F32)
            hf_ref[:, cols] = hfg * dec[:, cols] + upd
            xg = xs[:, cols]
            head_of_lane = _iota((CHUNK, GW), 1) // P
            for r in range(HEADS_PER_GROUP):
                h = g * HEADS_PER_GROUP + r
                colf = col[:, h * N:(h + 1) * N]
                colb = col[:, (H + h) * N:(H + h + 1) * N]
                mf = jnp.where(lower, jnp.exp(colf - qt[h:h + 1, :]), 0.0) * qt[2 * H + h:2 * H + h + 1, :]
                mb = jnp.where(upper, jnp.exp(qt[H + h:H + h + 1, :] - colb), 0.0) * qt[3 * H + h:3 * H + h + 1, :]
                m = (cb * (mf + mb)).astype(BF16)
                xh = jnp.where(head_of_lane == r, xg, jnp.zeros_like(xg))
                yg = yg + jnp.dot(m, xh, preferred_element_type=F32)
            ybuf_ref[:, cols] = yg

        yt = (ybuf_ref[...] + dsk_ref[...] * xsf) * z_ref[...].astype(F32)
        gw = D_SSM // N_GROUPS
        for g in range(N_GROUPS):
            seg = yt[:, g * gw:(g + 1) * gw]
            ms = jnp.mean(seg * seg, axis=-1, keepdims=True)
            y_ref[:, g * gw:(g + 1) * gw] = (seg * lax.rsqrt(ms + LN_EPS) * ng_ref[:, g * gw:(g + 1) * gw]).astype(BF16)

        @pl.when(last)
        def _():
            store_state(hf_ref, hf_out_ref)


def _ssd(lay, xbc_c, z, pre, h0f, h0b, d_skip, norm_g):
    T = lay.n_tokens
    tabs = lay.ssd_step_tables()
    n_steps = tabs[0].shape[0]
    nsp = len(tabs)

    def by_chunk(s, *t):
        return (t[0][s], 0)

    def by_y(s, *t):
        return (t[1][s], 0)

    def by_sin(s, *t):
        return (t[6][s], 0, 0, 0)

    def by_sout(s, *t):
        return (t[7][s], 0, 0, 0)

    def const(s, *t):
        return (0, 0)

    H = N_HEADS
    eye3 = jnp.asarray(np.arange(4 * H)[:, None] == np.arange(12 * H)[None, :] % (4 * H), dtype=BF16)
    consts = [_expand3(2 * H, HEADDIM), _expand3(H, HEADDIM), _expand3(2 * H, D_STATE),
              _expand3(3 * H, HEADDIM), eye3]
    sshape = (1, N_HEADS, HEADDIM, D_STATE)
    gs = pltpu.PrefetchScalarGridSpec(
        num_scalar_prefetch=nsp,
        grid=(n_steps,),
        in_specs=[pl.BlockSpec((CHUNK, D_XBC), by_chunk),
                  pl.BlockSpec((CHUNK, D_SSM), by_chunk),
                  pl.BlockSpec((CHUNK, 128), by_chunk),
                  pl.BlockSpec(sshape, by_sin),
                  pl.BlockSpec(sshape, by_sin),
                  pl.BlockSpec((1, D_SSM), const),
                  pl.BlockSpec((1, D_SSM), const)] + [pl.BlockSpec(c.shape, const) for c in consts],
        out_specs=[pl.BlockSpec((CHUNK, D_SSM), by_y),
                   pl.BlockSpec(sshape, by_sout),
                   pl.BlockSpec(sshape, by_sout)],
        scratch_shapes=[pltpu.VMEM((D_STATE, D_SSM), F32),
                        pltpu.VMEM((D_STATE, D_SSM), F32),
                        pltpu.VMEM((lay.max_chunks, D_STATE, D_SSM), BF16),
                        pltpu.VMEM((CHUNK, D_SSM), F32)])
    n_out = lay.n_prompt_seqs
    return pl.pallas_call(
        _ssd_kernel, grid_spec=gs,
        out_shape=(jax.ShapeDtypeStruct((T, D_SSM), BF16),
                   jax.ShapeDtypeStruct((n_out,) + sshape[1:], F32),
                   jax.ShapeDtypeStruct((n_out,) + sshape[1:], F32)),
        compiler_params=_cparams(("arbitrary",)),
        name="ssd",
    )(*tabs, xbc_c, z, pre, h0f, h0b,
      jnp.repeat(d_skip, HEADDIM).reshape(1, -1), norm_g.reshape(1, -1), *consts)


def _outproj_kernel(row_ref, posb_ref, flag_ref, xp_ref, xs_ref, pos_ref, mod_ref, co_ref, ys_ref, wo_ref,
                    g_ref, b_ref, x1_ref, h2_ref, h2p_ref, *, alpha):
    i = pl.program_id(0)
    x = jnp.where(flag_ref[i] == 1, xs_ref[...] + pos_ref[...], xp_ref[...])
    r = row_ref[i]
    g1 = mod_ref[pl.ds(r, 1), 2 * D_MODEL:3 * D_MODEL]
    sh2 = mod_ref[pl.ds(r, 1), 3 * D_MODEL:4 * D_MODEL]
    sc2 = mod_ref[pl.ds(r, 1), 4 * D_MODEL:5 * D_MODEL]
    mix = jnp.dot(co_ref[...], wo_ref[0:D_CONV, :], preferred_element_type=F32) \
        + jnp.dot(ys_ref[...], wo_ref[D_CONV:, :], preferred_element_type=F32)
    x1 = _ln_rows(alpha * x + g1 * mix) * g_ref[...] + b_ref[...]
    x1_ref[...] = x1
    h2 = _ln_rows(x1) * (1.0 + sc2) + sh2
    h2_ref[...] = h2
    packed = _pack_halves(h2)
    for c in range(ROW_PARTS):
        h2p_ref[c] = packed[:, c * PART_WORDS:(c + 1) * PART_WORDS]


def _outproj(lay, xp, xs, pos, mod, conv_out, y_ssm, w_out, ln_g, ln_b, alpha, tm):
    T = lay.n_tokens
    row, posb, flag = lay.token_tile_tables(tm)
    npt = lay.n_prompt_tokens // tm

    def const(i, r, p, f):
        return (0, 0)

    def cur(i, r, p, f):
        return (i, 0)

    gs = pltpu.PrefetchScalarGridSpec(
        num_scalar_prefetch=3,
        grid=(T // tm,),
        in_specs=[pl.BlockSpec((tm, D_MODEL), lambda i, r, p, f: (jnp.minimum(i, npt - 1), 0)),
                  pl.BlockSpec((tm, D_MODEL), lambda i, r, p, f: (jnp.maximum(i - npt, 0), 0)),
                  pl.BlockSpec((tm, D_MODEL), lambda i, r, p, f: (p[i], 0)),
                  pl.BlockSpec((8, 6 * D_MODEL), const),
                  pl.BlockSpec((tm, D_CONV), cur),
                  pl.BlockSpec((tm, D_SSM), cur),
                  pl.BlockSpec((D_CONV + D_SSM, D_MODEL), const),
                  pl.BlockSpec((1, D_MODEL), const),
                  pl.BlockSpec((1, D_MODEL), const)],
        out_specs=[pl.BlockSpec((tm, D_MODEL), cur),
                   pl.BlockSpec((tm, D_MODEL), cur),
                   pl.BlockSpec((ROW_PARTS, tm, PART_WORDS), lambda i, r, p, f: (0, i, 0))])
    return pl.pallas_call(
        functools.partial(_outproj_kernel, alpha=alpha), grid_spec=gs,
        out_shape=(jax.ShapeDtypeStruct((T, D_MODEL), F32),
                   jax.ShapeDtypeStruct((T, D_MODEL), F32),
                   jax.ShapeDtypeStruct((ROW_PARTS, T, PART_WORDS), jnp.uint32)),
        compiler_params=_cparams(("arbitrary",)),
        name="outproj",
    )(row, posb, flag, xp, xs, pos, mod, conv_out, y_ssm, w_out, ln_g.reshape(1, -1), ln_b.reshape(1, -1))


def _route_kernel(h2_ref, wrt_ref, bias_ref, idx_ref, wts_ref, pos_ref, cnt_ref, carry_ref, *, tm):
    i = pl.program_id(0)

    @pl.when(i == 0)
    def _():
        carry_ref[...] = jnp.zeros_like(carry_ref)

    E, NG, EG = N_EXPERTS, N_EXPERT_GROUPS, EXPERTS_PER_GROUP
    neg = -jnp.inf
    h = h2_ref[...]
    h_hi = h.astype(BF16)
    h_lo = (h - h_hi.astype(F32)).astype(BF16)
    h3 = jnp.concatenate([h_hi, h_lo, h_hi], axis=1)
    logits = lax.dot_general(wrt_ref[...], h3, (((1,), (1,)), ((), ())),
                             preferred_element_type=F32)
    s = jax.nn.sigmoid(logits)
    sel = s + bias_ref[...]
    sel3 = sel.reshape(NG, EG, tm)
    io3 = _iota((NG, EG, tm), 1)
    m1 = jnp.max(sel3, axis=1, keepdims=True)
    f1 = jnp.min(jnp.where(sel3 == m1, io3, EG), axis=1, keepdims=True)
    m2 = jnp.max(jnp.where(io3 == f1, neg, sel3), axis=1, keepdims=True)
    gscore = (m1 + m2).reshape(NG, tm)
    gio = _iota((NG, tm), 0)
    beaten = jnp.zeros((NG, tm), I32)
    for g in range(NG):
        row = gscore[g:g + 1, :]
        beats = jnp.where(row > gscore, 1, jnp.where(row == gscore, jnp.where(g < gio, 1, 0), 0))
        beaten = beaten + beats
    keep = (beaten < TOPK_GROUPS).astype(F32).reshape(NG, 1, tm)
    selm = jnp.where(keep > 0.5, sel3, neg).reshape(E, tm)
    eio = _iota((E, tm), 0)
    chosen = jnp.zeros((E, tm), F32)
    idxs, ws = [], []
    for k in range(TOP_K):
        m = jnp.max(selm, axis=0, keepdims=True)
        am = jnp.minimum(jnp.min(jnp.where(selm == m, eio, E), axis=0, keepdims=True), E - 1)
        hit = eio == am
        ws.append(jnp.sum(jnp.where(hit, s, 0.0), axis=0, keepdims=True))
        idxs.append(am)
        selm = jnp.where(hit, neg, selm)
        chosen = jnp.where(hit, 1.0, chosen)
    wsum = ws[0]
    for k in range(1, TOP_K):
        wsum = wsum + ws[k]
    before = (_iota((tm, tm), 0) < _iota((tm, tm), 1)).astype(BF16)
    prior = jnp.dot(chosen.astype(BF16), before, preferred_element_type=F32)
    carry = carry_ref[...]
    prior = prior + jnp.concatenate([carry] * (tm // 128), axis=1)
    for k in range(TOP_K):
        idx_ref[k:k + 1, :] = idxs[k]
        wts_ref[k:k + 1, :] = ws[k] / wsum * ROUTED_SCALE
        pos_ref[k:k + 1, :] = jnp.sum(jnp.where(eio == idxs[k], prior, 0.0), axis=0, keepdims=True).astype(I32)
    total = jnp.dot(chosen.astype(BF16), jnp.ones((tm, 128), BF16), preferred_element_type=F32)
    carry = carry + total
    carry_ref[...] = carry
    cnt_ref[...] = carry.astype(I32)


def _route(h2, w_router_t, router_bias, tm):
    T = h2.shape[0]
    bias_b = jnp.broadcast_to(router_bias.astype(F32)[:, None], (N_EXPERTS, tm))
    w_hi = w_router_t.astype(BF16)
    w_lo = (w_router_t - w_hi.astype(F32)).astype(BF16)
    w_router_t = jnp.concatenate([w_hi, w_hi, w_lo], axis=1)
    return pl.pallas_call(
        functools.partial(_route_kernel, tm=tm),
        grid=(T // tm,),
        in_specs=[pl.BlockSpec((tm, D_MODEL), lambda i: (i, 0)),
                  pl.BlockSpec((N_EXPERTS, 3 * D_MODEL), lambda i: (0, 0)),
                  pl.BlockSpec((N_EXPERTS, tm), lambda i: (0, 0))],
        out_specs=[pl.BlockSpec((TOP_K, tm), lambda i: (0, i)),
                   pl.BlockSpec((TOP_K, tm), lambda i: (0, i)),
                   pl.BlockSpec((TOP_K, tm), lambda i: (0, i)),
                   pl.BlockSpec((N_EXPERTS, 128), lambda i: (0, 0))],
        out_shape=(jax.ShapeDtypeStruct((TOP_K, T), I32),
                   jax.ShapeDtypeStruct((TOP_K, T), F32),
                   jax.ShapeDtypeStruct((TOP_K, T), I32),
                   jax.ShapeDtypeStruct((N_EXPERTS, 128), I32)),
        scratch_shapes=[pltpu.VMEM((N_EXPERTS, 128), F32)],
        compiler_params=_cparams(("arbitrary",)),
        name="route",
    )(h2, w_router_t, bias_b)


def _dest_kernel(idx_ref, pos_ref, start_ref, dest_ref):
    tm = idx_ref.shape[1]
    eio = _iota((N_EXPERTS, tm), 0)
    start = start_ref[...]
    for k in range(TOP_K):
        base = jnp.sum(jnp.where(eio == idx_ref[k:k + 1, :], start, 0.0), axis=0, keepdims=True)
        dest_ref[k:k + 1, :] = base.astype(I32) + pos_ref[k:k + 1, :]


def _dest(idx, pos, pad_start, tm):
    T = idx.shape[1]
    start_b = jnp.broadcast_to(pad_start.astype(F32)[:, None], (N_EXPERTS, tm))
    return pl.pallas_call(
        _dest_kernel,
        grid=(T // tm,),
        in_specs=[pl.BlockSpec((TOP_K, tm), lambda i: (0, i)),
                  pl.BlockSpec((TOP_K, tm), lambda i: (0, i)),
                  pl.BlockSpec((N_EXPERTS, tm), lambda i: (0, 0))],
        out_specs=pl.BlockSpec((TOP_K, tm), lambda i: (0, i)),
        out_shape=jax.ShapeDtypeStruct((TOP_K, T), I32),
        compiler_params=_cparams(("arbitrary",)),
        name="dest",
    )(idx, pos, start_b)


_WEIGHT_FETCH_CHUNKS = 4
_W_SLOTS = 8
_X_SLOTS = 4


def _expert_kernel(base_ref, nblk_ref, nused_ref, x_hbm, wgu_hbm, wd_hbm, y_hbm,
                   wgu_bf, wd_bf, wgu_stage, wd_stage, xbuf, ybuf, next_ref, sems, xsems, ysems):
    e = pl.program_id(0)
    n_used = nused_ref[0]

    def x_copy(b, slot):
        rows = pl.ds(pl.multiple_of(b * EXPERT_BLOCK, EXPERT_BLOCK), EXPERT_BLOCK)
        return pltpu.make_async_copy(x_hbm.at[:, rows, :], xbuf.at[slot], xsems.at[slot])

    def y_copy(b, slot):
        rows = pl.ds(pl.multiple_of(b * EXPERT_BLOCK, EXPERT_BLOCK), EXPERT_BLOCK)
        return pltpu.make_async_copy(ybuf.at[slot], y_hbm.at[:, rows, :], ysems.at[slot])

    def prefetch(b_first, max_starts):
        for _ in range(max_starts):
            j = next_ref[0]

            @pl.when((j <= b_first + _X_SLOTS - 1) & (j < n_used))
            def _():
                x_copy(j, lax.rem(j, _X_SLOTS)).start()
                next_ref[0] = j + 1

    def fetch(ex):
        ws = lax.rem(ex, _W_SLOTS)
        cps = []
        for c in range(_WEIGHT_FETCH_CHUNKS):
            rg = pl.ds(c * (D_MODEL // _WEIGHT_FETCH_CHUNKS), D_MODEL // _WEIGHT_FETCH_CHUNKS)
            rd = pl.ds(c * (D_EXPERT // _WEIGHT_FETCH_CHUNKS), D_EXPERT // _WEIGHT_FETCH_CHUNKS)
            cps.append(pltpu.make_async_copy(wgu_hbm.at[ex, rg], wgu_stage.at[ws, rg], sems.at[ws, 0]))
            cps.append(pltpu.make_async_copy(wd_hbm.at[ex, rd], wd_stage.at[ws, rd], sems.at[ws, 1]))
        return cps

    @pl.when(e == 0)
    def _():
        for ex in range(_W_SLOTS):
            for cp in fetch(ex):
                cp.start()
        next_ref[0] = 0
        prefetch(0, _X_SLOTS - 1)

    for cp in fetch(e):
        cp.wait()
    ws = lax.rem(e, _W_SLOTS)
    wgu_bf[...] = wgu_stage[ws].astype(BF16)
    wd_bf[...] = wd_stage[ws].astype(BF16)

    @pl.when(e + _W_SLOTS < N_EXPERTS)
    def _():
        for cp in fetch(e + _W_SLOTS):
            cp.start()

    def compute(b):
        xs = lax.rem(b, _X_SLOTS)
        parts = [_unpack_halves(xbuf[xs, c]) for c in range(ROW_PARTS)]
        chunks = [p[0] for p in parts] + [p[1] for p in parts]
        x = jnp.concatenate([xc.astype(BF16) for xc in chunks], axis=1)
        gu = jnp.dot(x, wgu_bf[...], preferred_element_type=F32)
        act = (_silu(gu[:, 0:D_EXPERT]) * gu[:, D_EXPERT:]).astype(BF16)
        packed = _pack_halves(jnp.dot(act, wd_bf[...], preferred_element_type=F32))
        for c in range(ROW_PARTS):
            ybuf[lax.rem(b, 2), c] = packed[:, c * PART_WORDS:(c + 1) * PART_WORDS]

    def step(b, width):
        for d in range(width):
            x_copy(b + d, lax.rem(b + d, _X_SLOTS)).wait()
        prefetch(b, width)
        for d in range(width):
            @pl.when(b + d >= 2)
            def _(d=d):
                y_copy(b + d - 2, lax.rem(b + d, 2)).wait()
        for d in range(width):
            compute(b + d)
        for d in range(width):
            y_copy(b + d, lax.rem(b + d, 2)).start()

    lo = base_ref[e]
    n_pairs = nblk_ref[e] // 2

    def pair(i, carry):
        step(lo + 2 * i, 2)
        return carry

    lax.fori_loop(0, n_pairs, pair, 0)

    @pl.when(nblk_ref[e] % 2 == 1)
    def _():
        step(lo + 2 * n_pairs, 1)

    @pl.when(e == N_EXPERTS - 1)
    def _():
        @pl.when(n_used >= 2)
        def _():
            y_copy(n_used - 2, lax.rem(n_used, 2)).wait()

        @pl.when(n_used >= 1)
        def _():
            y_copy(n_used - 1, lax.rem(n_used - 1, 2)).wait()


def _pack_halves(x):
    n = x.shape[1] // 2
    hi = lax.bitcast_convert_type(x[:, :n].astype(BF16).astype(F32), jnp.uint32)
    lo = lax.bitcast_convert_type(x[:, n:].astype(BF16).astype(F32), jnp.uint32)
    return hi | (lo >> 16)


def _unpack_halves(p):
    hi = lax.bitcast_convert_type(p & jnp.uint32(0xFFFF0000), F32)
    lo = lax.bitcast_convert_type(p << 16, F32)
    return hi, lo


def _expert(x_sorted, w_gu, w_down, blk_base, blk_count, n_used):
    n_rows = x_sorted.shape[1]
    blk_shape = (ROW_PARTS, EXPERT_BLOCK, PART_WORDS)
    gs = pltpu.PrefetchScalarGridSpec(
        num_scalar_prefetch=3,
        grid=(N_EXPERTS,),
        in_specs=[pl.BlockSpec(memory_space=pl.ANY),
                  pl.BlockSpec(memory_space=pl.ANY),
                  pl.BlockSpec(memory_space=pl.ANY)],
        out_specs=pl.BlockSpec(memory_space=pl.ANY),
        scratch_shapes=[pltpu.VMEM((D_MODEL, 2 * D_EXPERT), BF16),
                        pltpu.VMEM((D_EXPERT, D_MODEL), BF16),
                        pltpu.VMEM((_W_SLOTS, D_MODEL, 2 * D_EXPERT), F32),
                        pltpu.VMEM((_W_SLOTS, D_EXPERT, D_MODEL), F32),
                        pltpu.VMEM((_X_SLOTS,) + blk_shape, jnp.uint32),
                        pltpu.VMEM((2,) + blk_shape, jnp.uint32),
                        pltpu.SMEM((1,), I32),
                        pltpu.SemaphoreType.DMA((_W_SLOTS, 2)),
                        pltpu.SemaphoreType.DMA((_X_SLOTS,)),
                        pltpu.SemaphoreType.DMA((2,))])
    return pl.pallas_call(
        _expert_kernel, grid_spec=gs,
        out_shape=jax.ShapeDtypeStruct((ROW_PARTS, n_rows, PART_WORDS), jnp.uint32),
        compiler_params=_cparams(("arbitrary",)),
        name="expert",
    )(blk_base, blk_count, n_used, x_sorted, w_gu, w_down)


def _combine_kernel(row_ref, h2_ref, x1_ref, wt_ref, mod_ref, wsg_ref, wsd_ref, g_ref, b_ref, yt_ref,
                    o_ref, *, first_tile, alpha):
    i = pl.program_id(0)
    h2 = h2_ref[...].astype(BF16)
    su = jnp.dot(h2, wsg_ref[...], preferred_element_type=F32)
    act = (_silu(su[:, 0:D_SHARED]) * su[:, D_SHARED:]).astype(BF16)
    moe = jnp.dot(act, wsd_ref[...], preferred_element_type=F32)
    wt = wt_ref[...]
    his, los = [], []
    for c in range(ROW_PARTS):
        rh = jnp.zeros((h2.shape[0], PART_WORDS), F32)
        rl = jnp.zeros((h2.shape[0], PART_WORDS), F32)
        for k in range(TOP_K):
            hi, lo = _unpack_halves(yt_ref[c, k])
            w = wt[:, k:k + 1]
            rh = rh + hi * w
            rl = rl + lo * w
        his.append(rh)
        los.append(rl)
    moe = moe + jnp.concatenate(his + los, axis=1)
    g2 = mod_ref[pl.ds(row_ref[first_tile + i], 1), 5 * D_MODEL:6 * D_MODEL]
    o_ref[...] = _ln_rows(alpha * x1_ref[...] + g2 * moe) * g_ref[...] + b_ref[...]


def _combine(lay, h2, x1, wts_tok, mod, w_sh_gu, w_sh_down, ln_g, ln_b, y_tok, alpha, tm, first_token, n_tok):
    row, _, _ = lay.token_tile_tables(tm)
    first_tile = first_token // tm

    def cur(i, r):
        return (first_tile + i, 0)

    def const(i, r):
        return (0, 0)

    gs = pltpu.PrefetchScalarGridSpec(
        num_scalar_prefetch=1,
        grid=(n_tok // tm,),
        in_specs=[pl.BlockSpec((tm, D_MODEL), cur),
                  pl.BlockSpec((tm, D_MODEL), cur),
                  pl.BlockSpec((tm, TOP_K), cur),
                  pl.BlockSpec((8, 6 * D_MODEL), const),
                  pl.BlockSpec((D_MODEL, 2 * D_SHARED), const),
                  pl.BlockSpec((D_SHARED, D_MODEL), const),
                  pl.BlockSpec((1, D_MODEL), const),
                  pl.BlockSpec((1, D_MODEL), const),
                  pl.BlockSpec((ROW_PARTS, TOP_K, tm, PART_WORDS), lambda i, r: (0, 0, i, 0))],
        out_specs=pl.BlockSpec((tm, D_MODEL), lambda i, r: (i, 0)))
    return pl.pallas_call(
        functools.partial(_combine_kernel, first_tile=first_tile, alpha=alpha), grid_spec=gs,
        out_shape=jax.ShapeDtypeStruct((n_tok, D_MODEL), F32),
        compiler_params=_cparams(("arbitrary",)),
        name="combine",
    )(row, h2, x1, wts_tok, mod, w_sh_gu, w_sh_down, ln_g.reshape(1, -1), ln_b.reshape(1, -1), y_tok)


_GATHER_WINDOW = 128


def _sc_gather(table, idx):
    n, d = idx.shape[0], table.shape[1]
    mesh = plsc.VectorSubcoreMesh(core_axis_name="core", subcore_axis_name="subcore")

    @pl.kernel(out_type=jax.ShapeDtypeStruct((n, d), table.dtype), mesh=mesh)
    def gather_kernel(table_hbm, idx_hbm, out_hbm):
        def body(idx_vmem, out_vmem):
            pltpu.sync_copy(table_hbm.at[idx_vmem.at[0]], out_vmem)

        pltpu.emit_pipeline(
            body,
            grid=(n // _GATHER_WINDOW,),
            in_specs=[pl.BlockSpec((1, _GATHER_WINDOW), index_map=lambda i: (0, i))],
            out_specs=[pl.BlockSpec((_GATHER_WINDOW, d), index_map=lambda i: (i, 0))],
            core_axis_name=("core", "subcore"),
            dimension_semantics=(pltpu.PARALLEL,),
        )(idx_hbm, out_hbm)

    return gather_kernel(table, idx.reshape(1, n))


def _sc_scatter(rows, idx, n_out, repeat):
    n, d = rows.shape
    mesh = plsc.VectorSubcoreMesh(core_axis_name="core", subcore_axis_name="subcore")

    @pl.kernel(out_type=jax.ShapeDtypeStruct((n_out, d), rows.dtype), mesh=mesh, scratch_types=[])
    def scatter_kernel(rows_hbm, idx_hbm, out_hbm):
        def body(rows_vmem, idx_vmem):
            for r in range(repeat):
                pltpu.sync_copy(rows_vmem, out_hbm.at[idx_vmem.at[r]])

        pltpu.emit_pipeline(
            body,
            grid=(n // _GATHER_WINDOW,),
            in_specs=[pl.BlockSpec((_GATHER_WINDOW, d), index_map=lambda i: (i, 0)),
                      pl.BlockSpec((repeat, _GATHER_WINDOW), index_map=lambda i: (0, i))],
            out_specs=[],
            core_axis_name=("core", "subcore"),
            dimension_semantics=(pltpu.PARALLEL,),
        )(rows_hbm, idx_hbm)

    return scatter_kernel(rows, idx.reshape(repeat, n))


class _Layout:
    def __init__(self, n_prompt_seqs, prompt_len, n_sample_seqs, sample_len):
        self.n_prompt_seqs, self.prompt_len = n_prompt_seqs, prompt_len
        self.n_sample_seqs, self.sample_len = n_sample_seqs, sample_len
        self.n_prompt_tokens = n_prompt_seqs * prompt_len
        self.n_tokens = self.n_prompt_tokens + n_sample_seqs * sample_len
        assert prompt_len % CONV_TILE == 0 and sample_len % CONV_TILE == 0
        self.max_chunks = max(prompt_len, sample_len) // CHUNK

    def token_tile_tables(self, tm):
        assert self.n_prompt_tokens % tm == 0 and self.sample_len % tm == 0
        npt = self.n_prompt_tokens // tm
        per_seq = self.sample_len // tm
        n = self.n_tokens // tm
        row = np.zeros(n, np.int32)
        posb = np.zeros(n, np.int32)
        flag = np.zeros(n, np.int32)
        for i in range(npt, n):
            j = i - npt
            row[i] = 1 + j // per_seq
            posb[i] = j % per_seq
            flag[i] = 1
        return jnp.asarray(row), jnp.asarray(posb), jnp.asarray(flag)

    def conv_tile_tables(self):
        lok, rok = [], []
        for n_seq, length in ((self.n_prompt_seqs, self.prompt_len), (self.n_sample_seqs, self.sample_len)):
            per = length // CONV_TILE
            for _ in range(n_seq):
                for j in range(per):
                    lok.append(int(j > 0))
                    rok.append(int(j < per - 1))
        return jnp.asarray(np.array(lok, np.int32)), jnp.asarray(np.array(rok, np.int32))

    def ssd_step_tables(self):
        cols = [[] for _ in range(9)]
        seqs = []
        c0 = self.n_prompt_tokens // CHUNK
        for j in range(self.n_sample_seqs):
            nc = self.sample_len // CHUNK
            seqs.append((c0 + j * nc, nc, 0, j, 0))
        for j in range(self.n_prompt_seqs):
            nc = self.prompt_len // CHUNK
            seqs.append((j * nc, nc, 1, 0, j))
        for base, nc, zero, sin, sout in seqs:
            for phase in (0, 1):
                order = range(nc - 1, -1, -1) if phase == 0 else range(nc)
                for n, c in enumerate(order):
                    vals = (base + c, base if phase == 0 else base + c, phase, int(n == 0), int(n == nc - 1),
                            zero, sin, sout, c)
                    for col, v in zip(cols, vals):
                        col.append(v)
        return tuple(jnp.asarray(np.array(col, np.int32)) for col in cols)


def _grid_pos_embed(n_tokens):
    rows = n_tokens // GRID_W
    quarter = D_MODEL // 4
    freq = jnp.exp(-math.log(10000.0) * jnp.arange(quarter, dtype=F32) / quarter)
    r = jnp.broadcast_to(jnp.arange(rows, dtype=F32)[:, None, None] * freq, (rows, GRID_W, quarter))
    cl = jnp.broadcast_to(jnp.arange(GRID_W, dtype=F32)[None, :, None] * freq, (rows, GRID_W, quarter))
    emb = jnp.concatenate([jnp.sin(r), jnp.cos(r), jnp.sin(cl), jnp.cos(cl)], axis=-1)
    return emb.reshape(rows * GRID_W, D_MODEL)


def _moe_plan(counts):
    blk_count = (counts + EXPERT_BLOCK - 1) // EXPERT_BLOCK
    blk_end = jnp.cumsum(blk_count)
    blk_base = blk_end - blk_count
    return ((blk_base * EXPERT_BLOCK).astype(I32), blk_base.astype(I32), blk_count.astype(I32),
            blk_end[-1:].astype(I32))


def _layer(lay, xp, xs, pos, cond8, h0f, h0b, lp, alpha, tm_proj=512, tm_route=256, tm_comb=256):
    (w_ada, b_ada, w_in, conv_w, conv_b, conv_ln_g, conv_ln_b, ssm_conv_w, ssm_conv_b, dt_bias, a_log,
     d_skip, ssm_norm_g, w_out, ln1_g, ln1_b, w_router, router_bias, w_exp_gu, w_exp_down, w_sh_gu,
     w_sh_down, ln2_g, ln2_b) = lp
    T = lay.n_tokens
    n_main = 2 * D_CONV + D_SSM + D_XBC
    w_main = w_in[:, :n_main].astype(BF16)
    w_dt = jnp.pad(w_in[:, n_main:], ((0, 0), (0, 128 - 2 * N_HEADS))).astype(BF16)

    mod = _ada(cond8, w_ada, b_ada)
    u, z, xbc, pre = _inproj(lay, xp, xs, pos, mod, w_main, w_dt, dt_bias, a_log, tm_proj)
    conv_out, xbc_c = _conv(lay, u, xbc, conv_w, conv_b, conv_ln_g, conv_ln_b, ssm_conv_w, ssm_conv_b)
    y_ssm, hf, hb = _ssd(lay, xbc_c, z, pre, h0f, h0b, d_skip, ssm_norm_g)
    x1, h2, h2p = _outproj(lay, xp, xs, pos, mod, conv_out, y_ssm, w_out.astype(BF16), ln1_g, ln1_b, alpha, tm_proj)

    idx, wts, posn, cnt = _route(h2, w_router.T, router_bias, tm_route)
    n_blocks = -(-T * TOP_K // EXPERT_BLOCK) + N_EXPERTS
    pad_start, blk_base, blk_count, n_used = _moe_plan(cnt[:, 0])
    dest2 = _dest(idx, posn, pad_start, 512)
    n_rows = n_blocks * EXPERT_BLOCK
    scatter_idx = jnp.concatenate([dest2 + c * n_rows for c in range(ROW_PARTS)], axis=1)
    x_sorted = _sc_scatter(h2p.reshape(ROW_PARTS * T, PART_WORDS), scatter_idx.reshape(-1),
                           ROW_PARTS * n_rows, TOP_K).reshape(ROW_PARTS, n_rows, PART_WORDS)
    y_sorted = _expert(x_sorted, w_exp_gu, w_exp_down, blk_base, blk_count, n_used)
    y_flat = y_sorted.reshape(ROW_PARTS * n_rows, PART_WORDS)
    wts_tok, wsg, wsd = wts.T, w_sh_gu.astype(BF16), w_sh_down.astype(BF16)
    outs = []
    for first, n_tok in ((0, lay.n_prompt_tokens), (lay.n_prompt_tokens, T - lay.n_prompt_tokens)):
        dest_g = dest2[:, first:first + n_tok].reshape(-1)
        idx_parts = jnp.concatenate([dest_g + c * n_rows for c in range(ROW_PARTS)])
        y_tok = _sc_gather(y_flat, idx_parts).reshape(ROW_PARTS, TOP_K, n_tok, PART_WORDS)
        outs.append(_combine(lay, h2, x1, wts_tok, mod, wsg, wsd, ln2_g, ln2_b, y_tok, alpha, tm_comb,
                             first, n_tok))
    return outs[0], outs[1], hf, hb


def kernel(x_prompt, x_sample, state_ssd_fwd, state_ssd_bwd, c, c_ctx, w_ada, b_ada, w_in, conv_w, conv_b, conv_ln_g, conv_ln_b, ssm_conv_w, ssm_conv_b, dt_bias, a_log, d_skip, ssm_norm_g, w_out, ln1_g, ln1_b, w_router, router_bias, w_exp_gu, w_exp_down, w_sh_gu, w_sh_down, ln2_g, ln2_b):
    depth = w_ada.shape[0]
    assert depth == 1, "the prompt and latent passes are fused per layer; one layer is supported"
    bp, lp_, _ = x_prompt.shape
    bd, ld, _ = x_sample.shape
    lay = _Layout(bp, lp_, bd, ld)
    alpha = (2.0 * depth) ** 0.25
    stacked = (w_ada, b_ada, w_in, conv_w, conv_b, conv_ln_g, conv_ln_b, ssm_conv_w, ssm_conv_b,
               dt_bias, a_log, d_skip, ssm_norm_g, w_out, ln1_g, ln1_b, w_router, router_bias,
               w_exp_gu, w_exp_down, w_sh_gu, w_sh_down, ln2_g, ln2_b)
    lp = [w[0] for w in stacked]
    cond8 = jnp.concatenate([c_ctx[None, :], c, jnp.zeros((8 - 1 - bd, D_MODEL), F32)], axis=0)
    pos = _grid_pos_embed(ld)
    sshape = (bd, N_HEADS, HEADDIM, D_STATE)
    out_p, out_s, hf, hb = _layer(lay, x_prompt.reshape(bp * lp_, D_MODEL), x_sample.reshape(bd * ld, D_MODEL),
                                  pos, cond8, state_ssd_fwd[:, 0].reshape(sshape),
                                  state_ssd_bwd[:, 0].reshape(sshape), lp, alpha)
    return (out_p.reshape(bp, lp_, D_MODEL), out_s.reshape(bd, ld, D_MODEL),
            hf[:, None], hb[:, None])
```

```python
import functools
import math

import numpy as np
import jax
import jax.numpy as jnp
from jax import lax
from jax.experimental import pallas as pl
from jax.experimental.pallas import tpu as pltpu
from jax.experimental.pallas import tpu_sc as plsc

F32 = jnp.float32
BF16 = jnp.bfloat16
I32 = jnp.int32
HI = lax.Precision.HIGHEST

D_MODEL = 1024
GRID_W = 64
D_CONV = 1024
CONV_K = 31
N_HEADS = 16
HEADDIM = 64
D_SSM = N_HEADS * HEADDIM
N_GROUPS = 4
HEADS_PER_GROUP = N_HEADS // N_GROUPS
D_STATE = 128
SSM_CONV_K = 4
CHUNK = 128
D_XBC = D_SSM + 2 * N_GROUPS * D_STATE
N_EXPERTS = 256
TOP_K = 8
N_EXPERT_GROUPS = 8
EXPERTS_PER_GROUP = N_EXPERTS // N_EXPERT_GROUPS
TOPK_GROUPS = 4
D_EXPERT = 256
D_SHARED = 256
ROUTED_SCALE = 2.5
LN_EPS = 1e-5

CONV_TILE = 256
HALO = 16
EXPERT_BLOCK = 256
ROW_PARTS = 2
PART_WORDS = D_MODEL // 2 // ROW_PARTS
VMEM_LIMIT = 56 * 1024 * 1024


def _cparams(sem, vmem=VMEM_LIMIT):
    return pltpu.CompilerParams(dimension_semantics=sem, vmem_limit_bytes=vmem)


def _silu(x):
    return x * jax.nn.sigmoid(x)


def _ln_rows(x):
    mu = jnp.mean(x, axis=-1, keepdims=True)
    xc = x - mu
    var = jnp.mean(xc * xc, axis=-1, keepdims=True)
    return xc * lax.rsqrt(var + LN_EPS)


def _iota(shape, dim):
    return lax.broadcasted_iota(I32, shape, dim)


def _expand_matrix(n_in, width):
    return (_iota((n_in, n_in * width), 0) == _iota((n_in, n_in * width), 1) // width).astype(F32)


def _dot_hi(a, b):
    return jnp.dot(a, b, precision=HI, preferred_element_type=F32)


def _split3(x):
    hi = x.astype(BF16)
    r1 = x - hi.astype(F32)
    mid = r1.astype(BF16)
    lo = (r1 - mid.astype(F32)).astype(BF16)
    return jnp.concatenate([hi, mid, lo], axis=1)


def _expand3(n, width):
    rows = np.arange(3 * n)[:, None] % n
    cols = np.arange(n * width)[None, :] // width
    return jnp.asarray(rows == cols, dtype=BF16)


def _expand_exact(x, e3):
    return jnp.dot(_split3(x), e3, preferred_element_type=F32)


def _ada_kernel(c_ref, w_ref, b_ref, o_ref):
    o_ref[...] = _dot_hi(_silu(c_ref[...]), w_ref[...]) + b_ref[...]


def _ada(cond8, w_ada, b_ada):
    n = w_ada.shape[1]
    tn = 1024
    return pl.pallas_call(
        _ada_kernel,
        grid=(n // tn,),
        in_specs=[pl.BlockSpec((8, D_MODEL), lambda j: (0, 0)),
                  pl.BlockSpec((D_MODEL, tn), lambda j: (0, j)),
                  pl.BlockSpec((1, tn), lambda j: (0, j))],
        out_specs=pl.BlockSpec((8, tn), lambda j: (0, j)),
        out_shape=jax.ShapeDtypeStruct((8, n), F32),
        compiler_params=_cparams(("arbitrary",)),
        name="ada",
    )(cond8, w_ada, b_ada.reshape(1, n))


def _inproj_kernel(row_ref, posb_ref, flag_ref, xp_ref, xs_ref, pos_ref, mod_ref, wm_ref, wdt_ref,
                   dtb_ref, alog_ref, tri_ref, u_ref, z_ref, xbc_ref, pre_ref):
    i = pl.program_id(0)
    x = jnp.where(flag_ref[i] == 1, xs_ref[...] + pos_ref[...], xp_ref[...])
    r = row_ref[i]
    sh1 = mod_ref[pl.ds(r, 1), 0:D_MODEL]
    sc1 = mod_ref[pl.ds(r, 1), D_MODEL:2 * D_MODEL]
    h = (_ln_rows(x) * (1.0 + sc1) + sh1).astype(BF16)
    glu_a = jnp.dot(h, wm_ref[:, 0:D_CONV], preferred_element_type=F32)
    glu_g = jnp.dot(h, wm_ref[:, D_CONV:2 * D_CONV], preferred_element_type=F32)
    u_ref[...] = (glu_a * jax.nn.sigmoid(glu_g)).astype(BF16)
    z = jnp.dot(h, wm_ref[:, 2 * D_CONV:2 * D_CONV + D_SSM], preferred_element_type=F32)
    z_ref[...] = _silu(z).astype(BF16)
    xbc_ref[...] = jnp.dot(h, wm_ref[:, 2 * D_CONV + D_SSM:], preferred_element_type=F32).astype(BF16)
    nh2 = 2 * N_HEADS
    dt = jnp.dot(h, wdt_ref[...], preferred_element_type=F32)[:, 0:nh2] + dtb_ref[...]
    dt = jnp.maximum(dt, 0.0) + jnp.log1p(jnp.exp(-jnp.abs(dt)))
    a = dt * (-jnp.exp(alog_ref[...]))
    a3 = jnp.dot(tri_ref[...], _split3(a), preferred_element_type=F32)
    acs = a3[:, 0:nh2] + a3[:, nh2:2 * nh2] + a3[:, 2 * nh2:3 * nh2]
    pre_ref[...] = jnp.concatenate([dt, acs, a, jnp.zeros((dt.shape[0], 128 - 3 * nh2), F32)], axis=1)


def _inproj(lay, xp, xs, pos, mod, w_main, w_dt, dt_bias, a_log, tm):
    T = lay.n_tokens
    row, posb, flag = lay.token_tile_tables(tm)
    npt = lay.n_prompt_tokens // tm
    n_main = w_main.shape[1]
    chunk_of = np.arange(tm) // CHUNK
    tri = jnp.asarray((chunk_of[:, None] == chunk_of[None, :]) & np.tril(np.ones((tm, tm), bool)), dtype=BF16)
    gs = pltpu.PrefetchScalarGridSpec(
        num_scalar_prefetch=3,
        grid=(T // tm,),
        in_specs=[pl.BlockSpec((tm, D_MODEL), lambda i, r, p, f: (jnp.minimum(i, npt - 1), 0)),
                  pl.BlockSpec((tm, D_MODEL), lambda i, r, p, f: (jnp.maximum(i - npt, 0), 0)),
                  pl.BlockSpec((tm, D_MODEL), lambda i, r, p, f: (p[i], 0)),
                  pl.BlockSpec((8, 6 * D_MODEL), lambda i, r, p, f: (0, 0)),
                  pl.BlockSpec((D_MODEL, n_main), lambda i, r, p, f: (0, 0)),
                  pl.BlockSpec((D_MODEL, 128), lambda i, r, p, f: (0, 0)),
                  pl.BlockSpec((1, 2 * N_HEADS), lambda i, r, p, f: (0, 0)),
                  pl.BlockSpec((1, 2 * N_HEADS), lambda i, r, p, f: (0, 0)),
                  pl.BlockSpec((tm, tm), lambda i, r, p, f: (0, 0))],
        out_specs=[pl.BlockSpec((tm, D_CONV), lambda i, r, p, f: (i, 0)),
                   pl.BlockSpec((tm, D_SSM), lambda i, r, p, f: (i, 0)),
                   pl.BlockSpec((tm, D_XBC), lambda i, r, p, f: (i, 0)),
                   pl.BlockSpec((tm, 128), lambda i, r, p, f: (i, 0))])
    return pl.pallas_call(
        _inproj_kernel, grid_spec=gs,
        out_shape=(jax.ShapeDtypeStruct((T, D_CONV), BF16),
                   jax.ShapeDtypeStruct((T, D_SSM), BF16),
                   jax.ShapeDtypeStruct((T, D_XBC), BF16),
                   jax.ShapeDtypeStruct((T, 128), F32)),
        compiler_params=_cparams(("arbitrary",)),
        name="inproj",
    )(row, posb, flag, xp, xs, pos, mod, w_main, w_dt, dt_bias.reshape(1, -1), a_log.reshape(1, -1), tri)


_N_SHIFT = 8
_SHIFT_ROWS = CONV_TILE + 2 * HALO - _N_SHIFT
_ROW_BLOCK = 64
_FILL_ROWS = 32
_SSM_ROWS, _SSM_LANES = 64, 256


def _conv_kernel(lok_ref, rok_ref, u_ref, ul_ref, ur_ref, xbc_ref, xbcl_ref, xbcr_ref,
                 cw_ref, cb_ref, lng_ref, lnb_ref, sw_ref, sb_ref, co_ref, xo_ref,
                 ext_ref, sh_ref, acc_ref, ext2_ref):
    i = pl.program_id(0)
    lok = lok_ref[i] == 1
    rok = rok_ref[i] == 1

    def fill_ext(rb, carry):
        r0 = pl.multiple_of(rb * _FILL_ROWS, _FILL_ROWS)
        dst = pl.ds(pl.multiple_of(HALO + r0, HALO), _FILL_ROWS)
        ext_ref[dst, :] = u_ref[pl.ds(r0, _FILL_ROWS), :].astype(F32)
        ext2_ref[dst, :] = xbc_ref[pl.ds(r0, _FILL_ROWS), :].astype(F32)
        return carry

    ext_ref[0:HALO, :] = jnp.where(lok, ul_ref[...].astype(F32), 0.0)
    ext_ref[HALO + CONV_TILE:, :] = jnp.where(rok, ur_ref[...].astype(F32), 0.0)
    ext2_ref[0:HALO, :] = jnp.where(lok, xbcl_ref[...].astype(F32), 0.0)
    ext2_ref[HALO + CONV_TILE:, :] = jnp.where(rok, xbcr_ref[...].astype(F32), 0.0)
    lax.fori_loop(0, CONV_TILE // _FILL_ROWS, fill_ext, 0)
    for r in range(_N_SHIFT):
        sh_ref[r] = ext_ref[r:r + _SHIFT_ROWS, :]

    first = HALO - (CONV_K - 1) // 2

    for j in range(D_CONV // 128):
        lanes = slice(j * 128, (j + 1) * 128)
        taps = [jnp.broadcast_to(cw_ref[k:k + 1, lanes], (8, 128)) for k in range(CONV_K)]
        bias = jnp.broadcast_to(cb_ref[:, lanes], (8, 128))

        def row_block(rb, carry, lanes=lanes, taps=taps, bias=bias):
            base = pl.multiple_of(rb * _ROW_BLOCK, _ROW_BLOCK)
            for sub in range(_ROW_BLOCK // 8):
                acc = bias
                for k in range(CONV_K):
                    o = first + k
                    row0 = base + (o // _N_SHIFT) * _N_SHIFT + sub * 8
                    acc = acc + sh_ref[o % _N_SHIFT, pl.ds(row0, 8), lanes] * taps[k]
                acc_ref[pl.ds(base + sub * 8, 8), lanes] = acc
            return carry

        lax.fori_loop(0, CONV_TILE // _ROW_BLOCK, row_block, 0)
    u = _ln_rows(acc_ref[...]) * lng_ref[...] + lnb_ref[...]
    co_ref[...] = _silu(u).astype(BF16)

    first2 = HALO - (SSM_CONV_K - 1) // 2
    for rb in range(CONV_TILE // _SSM_ROWS):
        for lc in range(D_XBC // _SSM_LANES):
            lanes = slice(lc * _SSM_LANES, (lc + 1) * _SSM_LANES)
            y = jnp.zeros((_SSM_ROWS, _SSM_LANES), F32) + sb_ref[:, lanes]
            for k in range(SSM_CONV_K):
                r0 = first2 + k + rb * _SSM_ROWS
                y = y + ext2_ref[r0:r0 + _SSM_ROWS, lanes] * sw_ref[k:k + 1, lanes]
            xo_ref[rb * _SSM_ROWS:(rb + 1) * _SSM_ROWS, lanes] = _silu(y).astype(BF16)


def _conv(lay, u, xbc, conv_w, conv_b, ln_g, ln_b, ssm_w, ssm_b):
    T = lay.n_tokens
    lok, rok = lay.conv_tile_tables()
    n_tiles = T // CONV_TILE
    hb = CONV_TILE // HALO
    n_hb = T // HALO

    def cur(i, l, r):
        return (i, 0)

    def left(i, l, r):
        return (jnp.maximum(i * hb - 1, 0), 0)

    def right(i, l, r):
        return (jnp.minimum((i + 1) * hb, n_hb - 1), 0)

    def const(i, l, r):
        return (0, 0)

    gs = pltpu.PrefetchScalarGridSpec(
        num_scalar_prefetch=2,
        grid=(n_tiles,),
        in_specs=[pl.BlockSpec((CONV_TILE, D_CONV), cur),
                  pl.BlockSpec((HALO, D_CONV), left),
                  pl.BlockSpec((HALO, D_CONV), right),
                  pl.BlockSpec((CONV_TILE, D_XBC), cur),
                  pl.BlockSpec((HALO, D_XBC), left),
                  pl.BlockSpec((HALO, D_XBC), right),
                  pl.BlockSpec((CONV_K, D_CONV), const),
                  pl.BlockSpec((1, D_CONV), const),
                  pl.BlockSpec((1, D_CONV), const),
                  pl.BlockSpec((1, D_CONV), const),
                  pl.BlockSpec((SSM_CONV_K, D_XBC), const),
                  pl.BlockSpec((1, D_XBC), const)],
        out_specs=[pl.BlockSpec((CONV_TILE, D_CONV), cur),
                   pl.BlockSpec((CONV_TILE, D_XBC), cur)],
        scratch_shapes=[pltpu.VMEM((CONV_TILE + 2 * HALO, D_CONV), F32),
                        pltpu.VMEM((_N_SHIFT, _SHIFT_ROWS, D_CONV), F32),
                        pltpu.VMEM((CONV_TILE, D_CONV), F32),
                        pltpu.VMEM((CONV_TILE + 2 * HALO, D_XBC), F32)])
    return pl.pallas_call(
        _conv_kernel, grid_spec=gs,
        out_shape=(jax.ShapeDtypeStruct((T, D_CONV), BF16),
                   jax.ShapeDtypeStruct((T, D_XBC), BF16)),
        compiler_params=_cparams(("arbitrary",)),
        name="conv",
    )(lok, rok, u, u, u, xbc, xbc, xbc, conv_w, conv_b.reshape(1, -1), ln_g.reshape(1, -1),
      ln_b.reshape(1, -1), ssm_w, ssm_b.reshape(1, -1))


_BN = N_GROUPS * D_STATE


def _ssd_kernel(chunk_ref, yidx_ref, phase_ref, first_ref, last_ref, zero_ref, sin_ref, sout_ref, cloc_ref,
                xbc_ref, z_ref, pre_ref, h0f_ref, h0b_ref, dsk_ref, ng_ref,
                edec_ref, ewb_ref, ecol_ref, ewide_ref, eye3_ref,
                y_ref, hf_out_ref, hb_out_ref,
                hf_ref, g_ref, gin_ref, ybuf_ref):
    s = pl.program_id(0)
    phase = phase_ref[s]
    first = first_ref[s] == 1
    last = last_ref[s] == 1
    zero = zero_ref[s] == 1
    cloc = cloc_ref[s]
    H, P, N = N_HEADS, HEADDIM, D_STATE

    GW = HEADS_PER_GROUP * P
    xs = xbc_ref[:, 0:D_SSM]
    dt = pre_ref[:, 0:2 * H]
    acs = pre_ref[:, 2 * H:4 * H]
    a = pre_ref[:, 4 * H:6 * H]
    tot = acs[CHUNK - 8:CHUNK, :]
    dec = _expand_exact(jnp.exp(tot), edec_ref[...])[7:8, :]
    exb = acs[:, H:2 * H] - a[:, H:2 * H]

    def load_state(src_ref, dst_ref):
        for j in range(H // 2):
            pair = jnp.concatenate([src_ref[0, 2 * j], src_ref[0, 2 * j + 1]], axis=0)
            dst_ref[:, 2 * j * P:(2 * j + 2) * P] = jnp.where(zero, 0.0, pair.T)

    def store_state(src_ref, dst_ref):
        for j in range(H // 2):
            pair = src_ref[:, 2 * j * P:(2 * j + 2) * P].T
            dst_ref[0, 2 * j] = pair[0:P]
            dst_ref[0, 2 * j + 1] = pair[P:2 * P]

    @pl.when(phase == 0)
    def _backward_states():
        @pl.when(first)
        def _():
            load_state(h0b_ref, g_ref)

        wb = dt[:, H:2 * H] * jnp.exp(exb)
        xw = (xs.astype(F32) * _expand_exact(wb, ewb_ref[...])).astype(BF16)
        for g in range(N_GROUPS):
            cols = slice(g * GW, (g + 1) * GW)
            bg = xbc_ref[:, D_SSM + g * N:D_SSM + (g + 1) * N]
            gg = g_ref[:, cols]
            gin_ref[cloc, :, cols] = gg.astype(BF16)
            upd = lax.dot_general(bg, xw[:, cols], (((0,), (0,)), ((), ())), preferred_element_type=F32)
            g_ref[:, cols] = gg * dec[:, D_SSM + g * GW:D_SSM + (g + 1) * GW] + upd

        @pl.when(last)
        def _():
            store_state(g_ref, hb_out_ref)

    @pl.when(phase == 1)
    def _forward_and_outputs():
        @pl.when(first)
        def _():
            load_state(h0f_ref, hf_ref)

        acsf = acs[:, 0:H]
        dtf = dt[:, 0:H]
        dtb = dt[:, H:2 * H]
        totf = acs[CHUNK - 1:CHUNK, 0:H]
        totb = acs[CHUNK - 1:CHUNK, H:2 * H]
        col = _expand_exact(jnp.concatenate([acsf, exb], axis=1), ecol_ref[...])
        q3 = _split3(jnp.concatenate([acsf, exb, dtf, dtb], axis=1))
        qt = lax.dot_general(eye3_ref[...], q3, (((1,), (1,)), ((), ())),
                             preferred_element_type=F32)
        wide = jnp.concatenate([dtf * jnp.exp(totf - acsf), jnp.exp(acsf), jnp.exp(totb - exb)], axis=1)
        wide = _expand_exact(wide, ewide_ref[...])
        xsf = xs.astype(F32)
        xw = (xsf * wide[:, 0:D_SSM]).astype(BF16)
        lower = _iota((CHUNK, CHUNK), 1) <= _iota((CHUNK, CHUNK), 0)
        upper = _iota((CHUNK, CHUNK), 1) >= _iota((CHUNK, CHUNK), 0)
        for g in range(N_GROUPS):
            cols = slice(g * GW, (g + 1) * GW)
            bg = xbc_ref[:, D_SSM + g * N:D_SSM + (g + 1) * N]
            cg = xbc_ref[:, D_SSM + _BN + g * N:D_SSM + _BN + (g + 1) * N]
            cb = lax.dot_general(cg, bg, (((1,), (1,)), ((), ())), preferred_element_type=F32)
            hfg = hf_ref[:, cols]
            yf = jnp.dot(cg, hfg.astype(BF16), preferred_element_type=F32)
            yb = jnp.dot(cg, gin_ref[cloc, :, cols], preferred_element_type=F32)
            yg = yf * wide[:, D_SSM + g * GW:D_SSM + (g + 1) * GW] \
                + yb * wide[:, 2 * D_SSM + g * GW:2 * D_SSM + (g + 1) * GW]
            upd = lax.dot_general(bg, xw[:, cols], (((0,), (0,)), ((), ())), preferred_element_type=F32)
            hf_ref[:, cols] = hfg * dec[:, cols] + upd
            xg = xs[:, cols]
            head_of_lane = _iota((CHUNK, GW), 1) // P
            for r in range(HEADS_PER_GROUP):
                h = g * HEADS_PER_GROUP + r
                colf = col[:, h * N:(h + 1) * N]
                colb = col[:, (H + h) * N:(H + h + 1) * N]
                mf = jnp.where(lower, jnp.exp(colf - qt[h:h + 1, :]), 0.0) * qt[2 * H + h:2 * H + h + 1, :]
                mb = jnp.where(upper, jnp.exp(qt[H + h:H + h + 1, :] - colb), 0.0) * qt[3 * H + h:3 * H + h + 1, :]
                m = (cb * (mf + mb)).astype(BF16)
                xh = jnp.where(head_of_lane == r, xg, jnp.zeros_like(xg))
                yg = yg + jnp.dot(m, xh, preferred_element_type=F32)
            ybuf_ref[:, cols] = yg

        yt = (ybuf_ref[...] + dsk_ref[...] * xsf) * z_ref[...].astype(F32)
        gw = D_SSM // N_GROUPS
        for g in range(N_GROUPS):
            seg = yt[:, g * gw:(g + 1) * gw]
            ms = jnp.mean(seg * seg, axis=-1, keepdims=True)
            y_ref[:, g * gw:(g + 1) * gw] = (seg * lax.rsqrt(ms + LN_EPS) * ng_ref[:, g * gw:(g + 1) * gw]).astype(BF16)

        @pl.when(last)
        def _():
            store_state(hf_ref, hf_out_ref)


def _ssd(lay, xbc_c, z, pre, h0f, h0b, d_skip, norm_g):
    T = lay.n_tokens
    tabs = lay.ssd_step_tables()
    n_steps = tabs[0].shape[0]
    nsp = len(tabs)

    def by_chunk(s, *t):
        return (t[0][s], 0)

    def by_y(s, *t):
        return (t[1][s], 0)

    def by_sin(s, *t):
        return (t[6][s], 0, 0, 0)

    def by_sout(s, *t):
        return (t[7][s], 0, 0, 0)

    def const(s, *t):
        return (0, 0)

    H = N_HEADS
    eye3 = jnp.asarray(np.arange(4 * H)[:, None] == np.arange(12 * H)[None, :] % (4 * H), dtype=BF16)
    consts = [_expand3(2 * H, HEADDIM), _expand3(H, HEADDIM), _expand3(2 * H, D_STATE),
              _expand3(3 * H, HEADDIM), eye3]
    sshape = (1, N_HEADS, HEADDIM, D_STATE)
    gs = pltpu.PrefetchScalarGridSpec(
        num_scalar_prefetch=nsp,
        grid=(n_steps,),
        in_specs=[pl.BlockSpec((CHUNK, D_XBC), by_chunk),
                  pl.BlockSpec((CHUNK, D_SSM), by_chunk),
                  pl.BlockSpec((CHUNK, 128), by_chunk),
                  pl.BlockSpec(sshape, by_sin),
                  pl.BlockSpec(sshape, by_sin),
                  pl.BlockSpec((1, D_SSM), const),
                  pl.BlockSpec((1, D_SSM), const)] + [pl.BlockSpec(c.shape, const) for c in consts],
        out_specs=[pl.BlockSpec((CHUNK, D_SSM), by_y),
                   pl.BlockSpec(sshape, by_sout),
                   pl.BlockSpec(sshape, by_sout)],
        scratch_shapes=[pltpu.VMEM((D_STATE, D_SSM), F32),
                        pltpu.VMEM((D_STATE, D_SSM), F32),
                        pltpu.VMEM((lay.max_chunks, D_STATE, D_SSM), BF16),
                        pltpu.VMEM((CHUNK, D_SSM), F32)])
    n_out = lay.n_prompt_seqs
    return pl.pallas_call(
        _ssd_kernel, grid_spec=gs,
        out_shape=(jax.ShapeDtypeStruct((T, D_SSM), BF16),
                   jax.ShapeDtypeStruct((n_out,) + sshape[1:], F32),
                   jax.ShapeDtypeStruct((n_out,) + sshape[1:], F32)),
        compiler_params=_cparams(("arbitrary",)),
        name="ssd",
    )(*tabs, xbc_c, z, pre, h0f, h0b,
      jnp.repeat(d_skip, HEADDIM).reshape(1, -1), norm_g.reshape(1, -1), *consts)


def _outproj_kernel(row_ref, posb_ref, flag_ref, xp_ref, xs_ref, pos_ref, mod_ref, co_ref, ys_ref, wo_ref,
                    g_ref, b_ref, x1_ref, h2_ref, h2p_ref, *, alpha):
    i = pl.program_id(0)
    x = jnp.where(flag_ref[i] == 1, xs_ref[...] + pos_ref[...], xp_ref[...])
    r = row_ref[i]
    g1 = mod_ref[pl.ds(r, 1), 2 * D_MODEL:3 * D_MODEL]
    sh2 = mod_ref[pl.ds(r, 1), 3 * D_MODEL:4 * D_MODEL]
    sc2 = mod_ref[pl.ds(r, 1), 4 * D_MODEL:5 * D_MODEL]
    mix = jnp.dot(co_ref[...], wo_ref[0:D_CONV, :], preferred_element_type=F32) \
        + jnp.dot(ys_ref[...], wo_ref[D_CONV:, :], preferred_element_type=F32)
    x1 = _ln_rows(alpha * x + g1 * mix) * g_ref[...] + b_ref[...]
    x1_ref[...] = x1
    h2 = _ln_rows(x1) * (1.0 + sc2) + sh2
    h2_ref[...] = h2
    packed = _pack_halves(h2)
    for c in range(ROW_PARTS):
        h2p_ref[c] = packed[:, c * PART_WORDS:(c + 1) * PART_WORDS]


def _outproj(lay, xp, xs, pos, mod, conv_out, y_ssm, w_out, ln_g, ln_b, alpha, tm):
    T = lay.n_tokens
    row, posb, flag = lay.token_tile_tables(tm)
    npt = lay.n_prompt_tokens // tm

    def const(i, r, p, f):
        return (0, 0)

    def cur(i, r, p, f):
        return (i, 0)

    gs = pltpu.PrefetchScalarGridSpec(
        num_scalar_prefetch=3,
        grid=(T // tm,),
        in_specs=[pl.BlockSpec((tm, D_MODEL), lambda i, r, p, f: (jnp.minimum(i, npt - 1), 0)),
                  pl.BlockSpec((tm, D_MODEL), lambda i, r, p, f: (jnp.maximum(i - npt, 0), 0)),
                  pl.BlockSpec((tm, D_MODEL), lambda i, r, p, f: (p[i], 0)),
                  pl.BlockSpec((8, 6 * D_MODEL), const),
                  pl.BlockSpec((tm, D_CONV), cur),
                  pl.BlockSpec((tm, D_SSM), cur),
                  pl.BlockSpec((D_CONV + D_SSM, D_MODEL), const),
                  pl.BlockSpec((1, D_MODEL), const),
                  pl.BlockSpec((1, D_MODEL), const)],
        out_specs=[pl.BlockSpec((tm, D_MODEL), cur),
                   pl.BlockSpec((tm, D_MODEL), cur),
                   pl.BlockSpec((ROW_PARTS, tm, PART_WORDS), lambda i, r, p, f: (0, i, 0))])
    return pl.pallas_call(
        functools.partial(_outproj_kernel, alpha=alpha), grid_spec=gs,
        out_shape=(jax.ShapeDtypeStruct((T, D_MODEL), F32),
                   jax.ShapeDtypeStruct((T, D_MODEL), F32),
                   jax.ShapeDtypeStruct((ROW_PARTS, T, PART_WORDS), jnp.uint32)),
        compiler_params=_cparams(("arbitrary",)),
        name="outproj",
    )(row, posb, flag, xp, xs, pos, mod, conv_out, y_ssm, w_out, ln_g.reshape(1, -1), ln_b.reshape(1, -1))


def _route_kernel(h2_ref, wrt_ref, bias_ref, idx_ref, wts_ref, pos_ref, cnt_ref, carry_ref, *, tm):
    i = pl.program_id(0)

    @pl.when(i == 0)
    def _():
        carry_ref[...] = jnp.zeros_like(carry_ref)

    E, NG, EG = N_EXPERTS, N_EXPERT_GROUPS, EXPERTS_PER_GROUP
    neg = -jnp.inf
    h = h2_ref[...]
    h_hi = h.astype(BF16)
    h_lo = (h - h_hi.astype(F32)).astype(BF16)
    h3 = jnp.concatenate([h_hi, h_lo, h_hi], axis=1)
    logits = lax.dot_general(wrt_ref[...], h3, (((1,), (1,)), ((), ())),
                             preferred_element_type=F32)
    s = jax.nn.sigmoid(logits)
    sel = s + bias_ref[...]
    sel3 = sel.reshape(NG, EG, tm)
    io3 = _iota((NG, EG, tm), 1)
    m1 = jnp.max(sel3, axis=1, keepdims=True)
    f1 = jnp.min(jnp.where(sel3 == m1, io3, EG), axis=1, keepdims=True)
    m2 = jnp.max(jnp.where(io3 == f1, neg, sel3), axis=1, keepdims=True)
    gscore = (m1 + m2).reshape(NG, tm)
    gio = _iota((NG, tm), 0)
    beaten = jnp.zeros((NG, tm), I32)
    for g in range(NG):
        row = gscore[g:g + 1, :]
        beats = jnp.where(row > gscore, 1, jnp.where(row == gscore, jnp.where(g < gio, 1, 0), 0))
        beaten = beaten + beats
    keep = (beaten < TOPK_GROUPS).astype(F32).reshape(NG, 1, tm)
    selm = jnp.where(keep > 0.5, sel3, neg).reshape(E, tm)
    eio = _iota((E, tm), 0)
    chosen = jnp.zeros((E, tm), F32)
    idxs, ws = [], []
    for k in range(TOP_K):
        m = jnp.max(selm, axis=0, keepdims=True)
        am = jnp.minimum(jnp.min(jnp.where(selm == m, eio, E), axis=0, keepdims=True), E - 1)
        hit = eio == am
        ws.append(jnp.sum(jnp.where(hit, s, 0.0), axis=0, keepdims=True))
        idxs.append(am)
        selm = jnp.where(hit, neg, selm)
        chosen = jnp.where(hit, 1.0, chosen)
    wsum = ws[0]
    for k in range(1, TOP_K):
        wsum = wsum + ws[k]
    before = (_iota((tm, tm), 0) < _iota((tm, tm), 1)).astype(BF16)
    prior = jnp.dot(chosen.astype(BF16), before, preferred_element_type=F32)
    carry = carry_ref[...]
    prior = prior + jnp.concatenate([carry] * (tm // 128), axis=1)
    for k in range(TOP_K):
        idx_ref[k:k + 1, :] = idxs[k]
        wts_ref[k:k + 1, :] = ws[k] / wsum * ROUTED_SCALE
        pos_ref[k:k + 1, :] = jnp.sum(jnp.where(eio == idxs[k], prior, 0.0), axis=0, keepdims=True).astype(I32)
    total = jnp.dot(chosen.astype(BF16), jnp.ones((tm, 128), BF16), preferred_element_type=F32)
    carry = carry + total
    carry_ref[...] = carry
    cnt_ref[...] = carry.astype(I32)


def _route(h2, w_router_t, router_bias, tm):
    T = h2.shape[0]
    bias_b = jnp.broadcast_to(router_bias.astype(F32)[:, None], (N_EXPERTS, tm))
    w_hi = w_router_t.astype(BF16)
    w_lo = (w_router_t - w_hi.astype(F32)).astype(BF16)
    w_router_t = jnp.concatenate([w_hi, w_hi, w_lo], axis=1)
    return pl.pallas_call(
        functools.partial(_route_kernel, tm=tm),
        grid=(T // tm,),
        in_specs=[pl.BlockSpec((tm, D_MODEL), lambda i: (i, 0)),
                  pl.BlockSpec((N_EXPERTS, 3 * D_MODEL), lambda i: (0, 0)),
                  pl.BlockSpec((N_EXPERTS, tm), lambda i: (0, 0))],
        out_specs=[pl.BlockSpec((TOP_K, tm), lambda i: (0, i)),
                   pl.BlockSpec((TOP_K, tm), lambda i: (0, i)),
                   pl.BlockSpec((TOP_K, tm), lambda i: (0, i)),
                   pl.BlockSpec((N_EXPERTS, 128), lambda i: (0, 0))],
        out_shape=(jax.ShapeDtypeStruct((TOP_K, T), I32),
                   jax.ShapeDtypeStruct((TOP_K, T), F32),
                   jax.ShapeDtypeStruct((TOP_K, T), I32),
                   jax.ShapeDtypeStruct((N_EXPERTS, 128), I32)),
        scratch_shapes=[pltpu.VMEM((N_EXPERTS, 128), F32)],
        compiler_params=_cparams(("arbitrary",)),
        name="route",
    )(h2, w_router_t, bias_b)


def _dest_kernel(idx_ref, pos_ref, start_ref, dest_ref):
    tm = idx_ref.shape[1]
    eio = _iota((N_EXPERTS, tm), 0)
    start = start_ref[...]
    for k in range(TOP_K):
        base = jnp.sum(jnp.where(eio == idx_ref[k:k + 1, :], start, 0.0), axis=0, keepdims=True)
        dest_ref[k:k + 1, :] = base.astype(I32) + pos_ref[k:k + 1, :]


def _dest(idx, pos, pad_start, tm):
    T = idx.shape[1]
    start_b = jnp.broadcast_to(pad_start.astype(F32)[:, None], (N_EXPERTS, tm))
    return pl.pallas_call(
        _dest_kernel,
        grid=(T // tm,),
        in_specs=[pl.BlockSpec((TOP_K, tm), lambda i: (0, i)),
                  pl.BlockSpec((TOP_K, tm), lambda i: (0, i)),
                  pl.BlockSpec((N_EXPERTS, tm), lambda i: (0, 0))],
        out_specs=pl.BlockSpec((TOP_K, tm), lambda i: (0, i)),
        out_shape=jax.ShapeDtypeStruct((TOP_K, T), I32),
        compiler_params=_cparams(("arbitrary",)),
        name="dest",
    )(idx, pos, start_b)


_WEIGHT_FETCH_CHUNKS = 4
_W_SLOTS = 8
_X_SLOTS = 6
_Y_SLOTS = 4


def _expert_kernel(base_ref, nblk_ref, nused_ref, x_hbm, wgu_hbm, wd_hbm, y_hbm,
                   wgu_bf, wd_bf, wgu_stage, wd_stage, xbuf, ybuf, next_ref, sems, xsems, ysems):
    e = pl.program_id(0)
    n_used = nused_ref[0]

    def x_copy(b, slot):
        rows = pl.ds(pl.multiple_of(b * EXPERT_BLOCK, EXPERT_BLOCK), EXPERT_BLOCK)
        return pltpu.make_async_copy(x_hbm.at[:, rows, :], xbuf.at[slot], xsems.at[slot])

    def y_copy(b, slot):
        rows = pl.ds(pl.multiple_of(b * EXPERT_BLOCK, EXPERT_BLOCK), EXPERT_BLOCK)
        return pltpu.make_async_copy(ybuf.at[slot], y_hbm.at[:, rows, :], ysems.at[slot])

    def prefetch(b_first, max_starts):
        for _ in range(max_starts):
            j = next_ref[0]

            @pl.when((j <= b_first + _X_SLOTS - 1) & (j < n_used))
            def _():
                x_copy(j, lax.rem(j, _X_SLOTS)).start()
                next_ref[0] = j + 1

    def fetch(ex):
        ws = lax.rem(ex, _W_SLOTS)
        cps = []
        for c in range(_WEIGHT_FETCH_CHUNKS):
            rg = pl.ds(c * (D_MODEL // _WEIGHT_FETCH_CHUNKS), D_MODEL // _WEIGHT_FETCH_CHUNKS)
            rd = pl.ds(c * (D_EXPERT // _WEIGHT_FETCH_CHUNKS), D_EXPERT // _WEIGHT_FETCH_CHUNKS)
            cps.append(pltpu.make_async_copy(wgu_hbm.at[ex, rg], wgu_stage.at[ws, rg], sems.at[ws, 0]))
            cps.append(pltpu.make_async_copy(wd_hbm.at[ex, rd], wd_stage.at[ws, rd], sems.at[ws, 1]))
        return cps

    @pl.when(e == 0)
    def _():
        for ex in range(_W_SLOTS):
            for cp in fetch(ex):
                cp.start()
        next_ref[0] = 0
        prefetch(0, _X_SLOTS - 1)

    for cp in fetch(e):
        cp.wait()
    ws = lax.rem(e, _W_SLOTS)
    wgu_bf[...] = wgu_stage[ws].astype(BF16)
    wd_bf[...] = wd_stage[ws].astype(BF16)

    @pl.when(e + _W_SLOTS < N_EXPERTS)
    def _():
        for cp in fetch(e + _W_SLOTS):
            cp.start()

    def compute(b):
        xs = lax.rem(b, _X_SLOTS)
        parts = [_unpack_halves(xbuf[xs, c]) for c in range(ROW_PARTS)]
        chunks = [p[0] for p in parts] + [p[1] for p in parts]
        x = jnp.concatenate([xc.astype(BF16) for xc in chunks], axis=1)
        gu = jnp.dot(x, wgu_bf[...], preferred_element_type=F32)
        act = (_silu(gu[:, 0:D_EXPERT]) * gu[:, D_EXPERT:]).astype(BF16)
        packed = _pack_halves(jnp.dot(act, wd_bf[...], preferred_element_type=F32))
        for c in range(ROW_PARTS):
            ybuf[lax.rem(b, _Y_SLOTS), c] = packed[:, c * PART_WORDS:(c + 1) * PART_WORDS]

    def step(b, width):
        for d in range(width):
            x_copy(b + d, lax.rem(b + d, _X_SLOTS)).wait()
        prefetch(b, width)
        for d in range(width):
            @pl.when(b + d >= _Y_SLOTS)
            def _(d=d):
                y_copy(b + d - _Y_SLOTS, lax.rem(b + d, _Y_SLOTS)).wait()
        for d in range(width):
            compute(b + d)
        for d in range(width):
            y_copy(b + d, lax.rem(b + d, _Y_SLOTS)).start()

    lo = base_ref[e]
    n_pairs = nblk_ref[e] // 2

    def pair(i, carry):
        step(lo + 2 * i, 2)
        return carry

    lax.fori_loop(0, n_pairs, pair, 0)

    @pl.when(nblk_ref[e] % 2 == 1)
    def _():
        step(lo + 2 * n_pairs, 1)

    @pl.when(e == N_EXPERTS - 1)
    def _():
        for d in range(1, _Y_SLOTS + 1):
            @pl.when(n_used >= d)
            def _(d=d):
                y_copy(n_used - d, lax.rem(n_used - d, _Y_SLOTS)).wait()


def _pack_halves(x):
    n = x.shape[1] // 2
    hi = lax.bitcast_convert_type(x[:, :n].astype(BF16).astype(F32), jnp.uint32)
    lo = lax.bitcast_convert_type(x[:, n:].astype(BF16).astype(F32), jnp.uint32)
    return hi | (lo >> 16)


def _unpack_halves(p):
    hi = lax.bitcast_convert_type(p & jnp.uint32(0xFFFF0000), F32)
    lo = lax.bitcast_convert_type(p << 16, F32)
    return hi, lo


def _expert(x_sorted, w_gu, w_down, blk_base, blk_count, n_used):
    n_rows = x_sorted.shape[1]
    blk_shape = (ROW_PARTS, EXPERT_BLOCK, PART_WORDS)
    gs = pltpu.PrefetchScalarGridSpec(
        num_scalar_prefetch=3,
        grid=(N_EXPERTS,),
        in_specs=[pl.BlockSpec(memory_space=pl.ANY),
                  pl.BlockSpec(memory_space=pl.ANY),
                  pl.BlockSpec(memory_space=pl.ANY)],
        out_specs=pl.BlockSpec(memory_space=pl.ANY),
        scratch_shapes=[pltpu.VMEM((D_MODEL, 2 * D_EXPERT), BF16),
                        pltpu.VMEM((D_EXPERT, D_MODEL), BF16),
                        pltpu.VMEM((_W_SLOTS, D_MODEL, 2 * D_EXPERT), F32),
                        pltpu.VMEM((_W_SLOTS, D_EXPERT, D_MODEL), F32),
                        pltpu.VMEM((_X_SLOTS,) + blk_shape, jnp.uint32),
                        pltpu.VMEM((_Y_SLOTS,) + blk_shape, jnp.uint32),
                        pltpu.SMEM((1,), I32),
                        pltpu.SemaphoreType.DMA((_W_SLOTS, 2)),
                        pltpu.SemaphoreType.DMA((_X_SLOTS,)),
                        pltpu.SemaphoreType.DMA((_Y_SLOTS,))])
    return pl.pallas_call(
        _expert_kernel, grid_spec=gs,
        out_shape=jax.ShapeDtypeStruct((ROW_PARTS, n_rows, PART_WORDS), jnp.uint32),
        compiler_params=_cparams(("arbitrary",)),
        name="expert",
    )(blk_base, blk_count, n_used, x_sorted, w_gu, w_down)


def _combine_kernel(row_ref, h2_ref, x1_ref, wt_ref, mod_ref, wsg_ref, wsd_ref, g_ref, b_ref, yt_ref,
                    o_ref, *, first_tile, alpha):
    i = pl.program_id(0)
    h2 = h2_ref[...].astype(BF16)
    su = jnp.dot(h2, wsg_ref[...], preferred_element_type=F32)
    act = (_silu(su[:, 0:D_SHARED]) * su[:, D_SHARED:]).astype(BF16)
    moe = jnp.dot(act, wsd_ref[...], preferred_element_type=F32)
    wt = wt_ref[...]
    his, los = [], []
    for c in range(ROW_PARTS):
        rh = jnp.zeros((h2.shape[0], PART_WORDS), F32)
        rl = jnp.zeros((h2.shape[0], PART_WORDS), F32)
        for k in range(TOP_K):
            hi, lo = _unpack_halves(yt_ref[c, k])
            w = wt[:, k:k + 1]
            rh = rh + hi * w
            rl = rl + lo * w
        his.append(rh)
        los.append(rl)
    moe = moe + jnp.concatenate(his + los, axis=1)
    g2 = mod_ref[pl.ds(row_ref[first_tile + i], 1), 5 * D_MODEL:6 * D_MODEL]
    o_ref[...] = _ln_rows(alpha * x1_ref[...] + g2 * moe) * g_ref[...] + b_ref[...]


def _combine(lay, h2, x1, wts_tok, mod, w_sh_gu, w_sh_down, ln_g, ln_b, y_tok, alpha, tm, first_token, n_tok):
    row, _, _ = lay.token_tile_tables(tm)
    first_tile = first_token // tm

    def cur(i, r):
        return (first_tile + i, 0)

    def const(i, r):
        return (0, 0)

    gs = pltpu.PrefetchScalarGridSpec(
        num_scalar_prefetch=1,
        grid=(n_tok // tm,),
        in_specs=[pl.BlockSpec((tm, D_MODEL), cur),
                  pl.BlockSpec((tm, D_MODEL), cur),
                  pl.BlockSpec((tm, TOP_K), cur),
                  pl.BlockSpec((8, 6 * D_MODEL), const),
                  pl.BlockSpec((D_MODEL, 2 * D_SHARED), const),
                  pl.BlockSpec((D_SHARED, D_MODEL), const),
                  pl.BlockSpec((1, D_MODEL), const),
                  pl.BlockSpec((1, D_MODEL), const),
                  pl.BlockSpec((ROW_PARTS, TOP_K, tm, PART_WORDS), lambda i, r: (0, 0, i, 0))],
        out_specs=pl.BlockSpec((tm, D_MODEL), lambda i, r: (i, 0)))
    return pl.pallas_call(
        functools.partial(_combine_kernel, first_tile=first_tile, alpha=alpha), grid_spec=gs,
        out_shape=jax.ShapeDtypeStruct((n_tok, D_MODEL), F32),
        compiler_params=_cparams(("arbitrary",)),
        name="combine",
    )(row, h2, x1, wts_tok, mod, w_sh_gu, w_sh_down, ln_g.reshape(1, -1), ln_b.reshape(1, -1), y_tok)


_GATHER_WINDOW = 128


def _sc_gather(table, idx):
    n, d = idx.shape[0], table.shape[1]
    mesh = plsc.VectorSubcoreMesh(core_axis_name="core", subcore_axis_name="subcore")

    @pl.kernel(out_type=jax.ShapeDtypeStruct((n, d), table.dtype), mesh=mesh)
    def gather_kernel(table_hbm, idx_hbm, out_hbm):
        def body(idx_vmem, out_vmem):
            pltpu.sync_copy(table_hbm.at[idx_vmem.at[0]], out_vmem)

        pltpu.emit_pipeline(
            body,
            grid=(n // _GATHER_WINDOW,),
            in_specs=[pl.BlockSpec((1, _GATHER_WINDOW), index_map=lambda i: (0, i))],
            out_specs=[pl.BlockSpec((_GATHER_WINDOW, d), index_map=lambda i: (i, 0))],
            core_axis_name=("core", "subcore"),
            dimension_semantics=(pltpu.PARALLEL,),
        )(idx_hbm, out_hbm)

    return gather_kernel(table, idx.reshape(1, n))


def _sc_scatter(rows, idx, n_out, repeat):
    n, d = rows.shape
    mesh = plsc.VectorSubcoreMesh(core_axis_name="core", subcore_axis_name="subcore")

    @pl.kernel(out_type=jax.ShapeDtypeStruct((n_out, d), rows.dtype), mesh=mesh, scratch_types=[])
    def scatter_kernel(rows_hbm, idx_hbm, out_hbm):
        def body(rows_vmem, idx_vmem):
            for r in range(repeat):
                pltpu.sync_copy(rows_vmem, out_hbm.at[idx_vmem.at[r]])

        pltpu.emit_pipeline(
            body,
            grid=(n // _GATHER_WINDOW,),
            in_specs=[pl.BlockSpec((_GATHER_WINDOW, d), index_map=lambda i: (i, 0)),
                      pl.BlockSpec((repeat, _GATHER_WINDOW), index_map=lambda i: (0, i))],
            out_specs=[],
            core_axis_name=("core", "subcore"),
            dimension_semantics=(pltpu.PARALLEL,),
        )(rows_hbm, idx_hbm)

    return scatter_kernel(rows, idx.reshape(repeat, n))


class _Layout:
    def __init__(self, n_prompt_seqs, prompt_len, n_sample_seqs, sample_len):
        self.n_prompt_seqs, self.prompt_len = n_prompt_seqs, prompt_len
        self.n_sample_seqs, self.sample_len = n_sample_seqs, sample_len
        self.n_prompt_tokens = n_prompt_seqs * prompt_len
        self.n_tokens = self.n_prompt_tokens + n_sample_seqs * sample_len
        assert prompt_len % CONV_TILE == 0 and sample_len % CONV_TILE == 0
        self.max_chunks = max(prompt_len, sample_len) // CHUNK

    def token_tile_tables(self, tm):
        assert self.n_prompt_tokens % tm == 0 and self.sample_len % tm == 0
        npt = self.n_prompt_tokens // tm
        per_seq = self.sample_len // tm
        n = self.n_tokens // tm
        row = np.zeros(n, np.int32)
        posb = np.zeros(n, np.int32)
        flag = np.zeros(n, np.int32)
        for i in range(npt, n):
            j = i - npt
            row[i] = 1 + j // per_seq
            posb[i] = j % per_seq
            flag[i] = 1
        return jnp.asarray(row), jnp.asarray(posb), jnp.asarray(flag)

    def conv_tile_tables(self):
        lok, rok = [], []
        for n_seq, length in ((self.n_prompt_seqs, self.prompt_len), (self.n_sample_seqs, self.sample_len)):
            per = length // CONV_TILE
            for _ in range(n_seq):
                for j in range(per):
                    lok.append(int(j > 0))
                    rok.append(int(j < per - 1))
        return jnp.asarray(np.array(lok, np.int32)), jnp.asarray(np.array(rok, np.int32))

    def ssd_step_tables(self):
        cols = [[] for _ in range(9)]
        seqs = []
        c0 = self.n_prompt_tokens // CHUNK
        for j in range(self.n_sample_seqs):
            nc = self.sample_len // CHUNK
            seqs.append((c0 + j * nc, nc, 0, j, 0))
        for j in range(self.n_prompt_seqs):
            nc = self.prompt_len // CHUNK
            seqs.append((j * nc, nc, 1, 0, j))
        for base, nc, zero, sin, sout in seqs:
            for phase in (0, 1):
                order = range(nc - 1, -1, -1) if phase == 0 else range(nc)
                for n, c in enumerate(order):
                    vals = (base + c, base if phase == 0 else base + c, phase, int(n == 0), int(n == nc - 1),
                            zero, sin, sout, c)
                    for col, v in zip(cols, vals):
                        col.append(v)
        return tuple(jnp.asarray(np.array(col, np.int32)) for col in cols)


def _grid_pos_embed(n_tokens):
    rows = n_tokens // GRID_W
    quarter = D_MODEL // 4
    freq = jnp.exp(-math.log(10000.0) * jnp.arange(quarter, dtype=F32) / quarter)
    r = jnp.broadcast_to(jnp.arange(rows, dtype=F32)[:, None, None] * freq, (rows, GRID_W, quarter))
    cl = jnp.broadcast_to(jnp.arange(GRID_W, dtype=F32)[None, :, None] * freq, (rows, GRID_W, quarter))
    emb = jnp.concatenate([jnp.sin(r), jnp.cos(r), jnp.sin(cl), jnp.cos(cl)], axis=-1)
    return emb.reshape(rows * GRID_W, D_MODEL)


def _moe_plan(counts):
    blk_count = (counts + EXPERT_BLOCK - 1) // EXPERT_BLOCK
    blk_end = jnp.cumsum(blk_count)
    blk_base = blk_end - blk_count
    return ((blk_base * EXPERT_BLOCK).astype(I32), blk_base.astype(I32), blk_count.astype(I32),
            blk_end[-1:].astype(I32))


def _layer(lay, xp, xs, pos, cond8, h0f, h0b, lp, alpha, tm_proj=512, tm_route=256, tm_comb=256):
    (w_ada, b_ada, w_in, conv_w, conv_b, conv_ln_g, conv_ln_b, ssm_conv_w, ssm_conv_b, dt_bias, a_log,
     d_skip, ssm_norm_g, w_out, ln1_g, ln1_b, w_router, router_bias, w_exp_gu, w_exp_down, w_sh_gu,
     w_sh_down, ln2_g, ln2_b) = lp
    T = lay.n_tokens
    n_main = 2 * D_CONV + D_SSM + D_XBC
    w_main = w_in[:, :n_main].astype(BF16)
    w_dt = jnp.pad(w_in[:, n_main:], ((0, 0), (0, 128 - 2 * N_HEADS))).astype(BF16)

    mod = _ada(cond8, w_ada, b_ada)
    u, z, xbc, pre = _inproj(lay, xp, xs, pos, mod, w_main, w_dt, dt_bias, a_log, tm_proj)
    conv_out, xbc_c = _conv(lay, u, xbc, conv_w, conv_b, conv_ln_g, conv_ln_b, ssm_conv_w, ssm_conv_b)
    y_ssm, hf, hb = _ssd(lay, xbc_c, z, pre, h0f, h0b, d_skip, ssm_norm_g)
    x1, h2, h2p = _outproj(lay, xp, xs, pos, mod, conv_out, y_ssm, w_out.astype(BF16), ln1_g, ln1_b, alpha, tm_proj)

    idx, wts, posn, cnt = _route(h2, w_router.T, router_bias, tm_route)
    n_blocks = -(-T * TOP_K // EXPERT_BLOCK) + N_EXPERTS
    pad_start, blk_base, blk_count, n_used = _moe_plan(cnt[:, 0])
    dest2 = _dest(idx, posn, pad_start, 512)
    n_rows = n_blocks * EXPERT_BLOCK
    scatter_idx = jnp.concatenate([dest2 + c * n_rows for c in range(ROW_PARTS)], axis=1)
    x_sorted = _sc_scatter(h2p.reshape(ROW_PARTS * T, PART_WORDS), scatter_idx.reshape(-1),
                           ROW_PARTS * n_rows, TOP_K).reshape(ROW_PARTS, n_rows, PART_WORDS)
    y_sorted = _expert(x_sorted, w_exp_gu, w_exp_down, blk_base, blk_count, n_used)
    y_flat = y_sorted.reshape(ROW_PARTS * n_rows, PART_WORDS)
    wts_tok, wsg, wsd = wts.T, w_sh_gu.astype(BF16), w_sh_down.astype(BF16)
    outs = []
    for first, n_tok in ((0, lay.n_prompt_tokens), (lay.n_prompt_tokens, T - lay.n_prompt_tokens)):
        dest_g = dest2[:, first:first + n_tok].reshape(-1)
        idx_parts = jnp.concatenate([dest_g + c * n_rows for c in range(ROW_PARTS)])
        y_tok = _sc_gather(y_flat, idx_parts).reshape(ROW_PARTS, TOP_K, n_tok, PART_WORDS)
        outs.append(_combine(lay, h2, x1, wts_tok, mod, wsg, wsd, ln2_g, ln2_b, y_tok, alpha, tm_comb,
                             first, n_tok))
    return outs[0], outs[1], hf, hb


def kernel(x_prompt, x_sample, state_ssd_fwd, state_ssd_bwd, c, c_ctx, w_ada, b_ada, w_in, conv_w, conv_b, conv_ln_g, conv_ln_b, ssm_conv_w, ssm_conv_b, dt_bias, a_log, d_skip, ssm_norm_g, w_out, ln1_g, ln1_b, w_router, router_bias, w_exp_gu, w_exp_down, w_sh_gu, w_sh_down, ln2_g, ln2_b):
    depth = w_ada.shape[0]
    assert depth == 1, "the prompt and latent passes are fused per layer; one layer is supported"
    bp, lp_, _ = x_prompt.shape
    bd, ld, _ = x_sample.shape
    lay = _Layout(bp, lp_, bd, ld)
    alpha = (2.0 * depth) ** 0.25
    stacked = (w_ada, b_ada, w_in, conv_w, conv_b, conv_ln_g, conv_ln_b, ssm_conv_w, ssm_conv_b,
               dt_bias, a_log, d_skip, ssm_norm_g, w_out, ln1_g, ln1_b, w_router, router_bias,
               w_exp_gu, w_exp_down, w_sh_gu, w_sh_down, ln2_g, ln2_b)
    lp = [w[0] for w in stacked]
    cond8 = jnp.concatenate([c_ctx[None, :], c, jnp.zeros((8 - 1 - bd, D_MODEL), F32)], axis=0)
    pos = _grid_pos_embed(ld)
    sshape = (bd, N_HEADS, HEADDIM, D_STATE)
    out_p, out_s, hf, hb = _layer(lay, x_prompt.reshape(bp * lp_, D_MODEL), x_sample.reshape(bd * ld, D_MODEL),
                                  pos, cond8, state_ssd_fwd[:, 0].reshape(sshape),
                                  state_ssd_bwd[:, 0].reshape(sshape), lp, alpha)
    return (out_p.reshape(bp, lp_, D_MODEL), out_s.reshape(bd, ld, D_MODEL),
            hf[:, None], hb[:, None])
```

```python
import functools
import math

import numpy as np
import jax
import jax.numpy as jnp
from jax import lax
from jax.experimental import pallas as pl
from jax.experimental.pallas import tpu as pltpu
from jax.experimental.pallas import tpu_sc as plsc

F32 = jnp.float32
BF16 = jnp.bfloat16
I32 = jnp.int32
HI = lax.Precision.HIGHEST

D_MODEL = 1024
GRID_W = 64
D_CONV = 1024
CONV_K = 31
N_HEADS = 16
HEADDIM = 64
D_SSM = N_HEADS * HEADDIM
N_GROUPS = 4
HEADS_PER_GROUP = N_HEADS // N_GROUPS
D_STATE = 128
SSM_CONV_K = 4
CHUNK = 128
D_XBC = D_SSM + 2 * N_GROUPS * D_STATE
N_EXPERTS = 256
TOP_K = 8
N_EXPERT_GROUPS = 8
EXPERTS_PER_GROUP = N_EXPERTS // N_EXPERT_GROUPS
TOPK_GROUPS = 4
D_EXPERT = 256
D_SHARED = 256
ROUTED_SCALE = 2.5
LN_EPS = 1e-5

CONV_TILE = 256
HALO = 16
EXPERT_BLOCK = 256
ROW_PARTS = 2
PART_WORDS = D_MODEL // 2 // ROW_PARTS
VMEM_LIMIT = 56 * 1024 * 1024


def _cparams(sem, vmem=VMEM_LIMIT):
    return pltpu.CompilerParams(dimension_semantics=sem, vmem_limit_bytes=vmem)


def _silu(x):
    return x * jax.nn.sigmoid(x)


def _ln_rows(x):
    mu = jnp.mean(x, axis=-1, keepdims=True)
    xc = x - mu
    var = jnp.mean(xc * xc, axis=-1, keepdims=True)
    return xc * lax.rsqrt(var + LN_EPS)


def _iota(shape, dim):
    return lax.broadcasted_iota(I32, shape, dim)


def _expand_matrix(n_in, width):
    return (_iota((n_in, n_in * width), 0) == _iota((n_in, n_in * width), 1) // width).astype(F32)


def _dot_hi(a, b):
    return jnp.dot(a, b, precision=HI, preferred_element_type=F32)


def _split3(x):
    hi = x.astype(BF16)
    r1 = x - hi.astype(F32)
    mid = r1.astype(BF16)
    lo = (r1 - mid.astype(F32)).astype(BF16)
    return jnp.concatenate([hi, mid, lo], axis=1)


def _expand3(n, width):
    rows = np.arange(3 * n)[:, None] % n
    cols = np.arange(n * width)[None, :] // width
    return jnp.asarray(rows == cols, dtype=BF16)


def _expand_exact(x, e3):
    return jnp.dot(_split3(x), e3, preferred_element_type=F32)


def _ada_kernel(c_ref, w_ref, b_ref, o_ref):
    o_ref[...] = _dot_hi(_silu(c_ref[...]), w_ref[...]) + b_ref[...]


def _ada(cond8, w_ada, b_ada):
    n = w_ada.shape[1]
    tn = 1024
    return pl.pallas_call(
        _ada_kernel,
        grid=(n // tn,),
        in_specs=[pl.BlockSpec((8, D_MODEL), lambda j: (0, 0)),
                  pl.BlockSpec((D_MODEL, tn), lambda j: (0, j)),
                  pl.BlockSpec((1, tn), lambda j: (0, j))],
        out_specs=pl.BlockSpec((8, tn), lambda j: (0, j)),
        out_shape=jax.ShapeDtypeStruct((8, n), F32),
        compiler_params=_cparams(("arbitrary",)),
        name="ada",
    )(cond8, w_ada, b_ada.reshape(1, n))


def _inproj_kernel(row_ref, posb_ref, flag_ref, xp_ref, xs_ref, pos_ref, mod_ref, wm_ref, wdt_ref,
                   dtb_ref, alog_ref, tri_ref, u_ref, z_ref, xbc_ref, pre_ref):
    i = pl.program_id(0)
    x = jnp.where(flag_ref[i] == 1, xs_ref[...] + pos_ref[...], xp_ref[...])
    r = row_ref[i]
    sh1 = mod_ref[pl.ds(r, 1), 0:D_MODEL]
    sc1 = mod_ref[pl.ds(r, 1), D_MODEL:2 * D_MODEL]
    h = (_ln_rows(x) * (1.0 + sc1) + sh1).astype(BF16)
    glu_a = jnp.dot(h, wm_ref[:, 0:D_CONV], preferred_element_type=F32)
    glu_g = jnp.dot(h, wm_ref[:, D_CONV:2 * D_CONV], preferred_element_type=F32)
    u_ref[...] = (glu_a * jax.nn.sigmoid(glu_g)).astype(BF16)
    z = jnp.dot(h, wm_ref[:, 2 * D_CONV:2 * D_CONV + D_SSM], preferred_element_type=F32)
    z_ref[...] = _silu(z).astype(BF16)
    xbc_ref[...] = jnp.dot(h, wm_ref[:, 2 * D_CONV + D_SSM:], preferred_element_type=F32).astype(BF16)
    nh2 = 2 * N_HEADS
    dt = jnp.dot(h, wdt_ref[...], preferred_element_type=F32)[:, 0:nh2] + dtb_ref[...]
    dt = jnp.maximum(dt, 0.0) + jnp.log1p(jnp.exp(-jnp.abs(dt)))
    a = dt * (-jnp.exp(alog_ref[...]))
    a3 = jnp.dot(tri_ref[...], _split3(a), preferred_element_type=F32)
    acs = a3[:, 0:nh2] + a3[:, nh2:2 * nh2] + a3[:, 2 * nh2:3 * nh2]
    pre_ref[...] = jnp.concatenate([dt, acs, a, jnp.zeros((dt.shape[0], 128 - 3 * nh2), F32)], axis=1)


def _inproj(lay, xp, xs, pos, mod, w_main, w_dt, dt_bias, a_log, tm):
    T = lay.n_tokens
    row, posb, flag = lay.token_tile_tables(tm)
    npt = lay.n_prompt_tokens // tm
    n_main = w_main.shape[1]
    chunk_of = np.arange(tm) // CHUNK
    tri = jnp.asarray((chunk_of[:, None] == chunk_of[None, :]) & np.tril(np.ones((tm, tm), bool)), dtype=BF16)
    gs = pltpu.PrefetchScalarGridSpec(
        num_scalar_prefetch=3,
        grid=(T // tm,),
        in_specs=[pl.BlockSpec((tm, D_MODEL), lambda i, r, p, f: (jnp.minimum(i, npt - 1), 0)),
                  pl.BlockSpec((tm, D_MODEL), lambda i, r, p, f: (jnp.maximum(i - npt, 0), 0)),
                  pl.BlockSpec((tm, D_MODEL), lambda i, r, p, f: (p[i], 0)),
                  pl.BlockSpec((8, 6 * D_MODEL), lambda i, r, p, f: (0, 0)),
                  pl.BlockSpec((D_MODEL, n_main), lambda i, r, p, f: (0, 0)),
                  pl.BlockSpec((D_MODEL, 128), lambda i, r, p, f: (0, 0)),
                  pl.BlockSpec((1, 2 * N_HEADS), lambda i, r, p, f: (0, 0)),
                  pl.BlockSpec((1, 2 * N_HEADS), lambda i, r, p, f: (0, 0)),
                  pl.BlockSpec((tm, tm), lambda i, r, p, f: (0, 0))],
        out_specs=[pl.BlockSpec((tm, D_CONV), lambda i, r, p, f: (i, 0)),
                   pl.BlockSpec((tm, D_SSM), lambda i, r, p, f: (i, 0)),
                   pl.BlockSpec((tm, D_XBC), lambda i, r, p, f: (i, 0)),
                   pl.BlockSpec((tm, 128), lambda i, r, p, f: (i, 0))])
    return pl.pallas_call(
        _inproj_kernel, grid_spec=gs,
        out_shape=(jax.ShapeDtypeStruct((T, D_CONV), BF16),
                   jax.ShapeDtypeStruct((T, D_SSM), BF16),
                   jax.ShapeDtypeStruct((T, D_XBC), BF16),
                   jax.ShapeDtypeStruct((T, 128), F32)),
        compiler_params=_cparams(("arbitrary",)),
        name="inproj",
    )(row, posb, flag, xp, xs, pos, mod, w_main, w_dt, dt_bias.reshape(1, -1), a_log.reshape(1, -1), tri)


_N_SHIFT = 8
_SHIFT_ROWS = CONV_TILE + 2 * HALO - _N_SHIFT
_ROW_BLOCK = 64
_FILL_ROWS = 32
_SSM_ROWS, _SSM_LANES = 64, 256


def _conv_kernel(lok_ref, rok_ref, u_ref, ul_ref, ur_ref, xbc_ref, xbcl_ref, xbcr_ref,
                 cw_ref, cb_ref, lng_ref, lnb_ref, sw_ref, sb_ref, co_ref, xo_ref,
                 ext_ref, sh_ref, acc_ref, ext2_ref):
    i = pl.program_id(0)
    lok = lok_ref[i] == 1
    rok = rok_ref[i] == 1

    def fill_ext(rb, carry):
        r0 = pl.multiple_of(rb * _FILL_ROWS, _FILL_ROWS)
        dst = pl.ds(pl.multiple_of(HALO + r0, HALO), _FILL_ROWS)
        ext_ref[dst, :] = u_ref[pl.ds(r0, _FILL_ROWS), :].astype(F32)
        ext2_ref[dst, :] = xbc_ref[pl.ds(r0, _FILL_ROWS), :].astype(F32)
        return carry

    ext_ref[0:HALO, :] = jnp.where(lok, ul_ref[...].astype(F32), 0.0)
    ext_ref[HALO + CONV_TILE:, :] = jnp.where(rok, ur_ref[...].astype(F32), 0.0)
    ext2_ref[0:HALO, :] = jnp.where(lok, xbcl_ref[...].astype(F32), 0.0)
    ext2_ref[HALO + CONV_TILE:, :] = jnp.where(rok, xbcr_ref[...].astype(F32), 0.0)
    lax.fori_loop(0, CONV_TILE // _FILL_ROWS, fill_ext, 0)
    for r in range(_N_SHIFT):
        sh_ref[r] = ext_ref[r:r + _SHIFT_ROWS, :]

    first = HALO - (CONV_K - 1) // 2

    for j in range(D_CONV // 128):
        lanes = slice(j * 128, (j + 1) * 128)
        taps = [jnp.broadcast_to(cw_ref[k:k + 1, lanes], (8, 128)) for k in range(CONV_K)]
        bias = jnp.broadcast_to(cb_ref[:, lanes], (8, 128))

        def row_block(rb, carry, lanes=lanes, taps=taps, bias=bias):
            base = pl.multiple_of(rb * _ROW_BLOCK, _ROW_BLOCK)
            for sub in range(_ROW_BLOCK // 8):
                acc = bias
                for k in range(CONV_K):
                    o = first + k
                    row0 = base + (o // _N_SHIFT) * _N_SHIFT + sub * 8
                    acc = acc + sh_ref[o % _N_SHIFT, pl.ds(row0, 8), lanes] * taps[k]
                acc_ref[pl.ds(base + sub * 8, 8), lanes] = acc
            return carry

        lax.fori_loop(0, CONV_TILE // _ROW_BLOCK, row_block, 0)
    u = _ln_rows(acc_ref[...]) * lng_ref[...] + lnb_ref[...]
    co_ref[...] = _silu(u).astype(BF16)

    first2 = HALO - (SSM_CONV_K - 1) // 2
    for rb in range(CONV_TILE // _SSM_ROWS):
        for lc in range(D_XBC // _SSM_LANES):
            lanes = slice(lc * _SSM_LANES, (lc + 1) * _SSM_LANES)
            y = jnp.zeros((_SSM_ROWS, _SSM_LANES), F32) + sb_ref[:, lanes]
            for k in range(SSM_CONV_K):
                r0 = first2 + k + rb * _SSM_ROWS
                y = y + ext2_ref[r0:r0 + _SSM_ROWS, lanes] * sw_ref[k:k + 1, lanes]
            xo_ref[rb * _SSM_ROWS:(rb + 1) * _SSM_ROWS, lanes] = _silu(y).astype(BF16)


def _conv(lay, u, xbc, conv_w, conv_b, ln_g, ln_b, ssm_w, ssm_b):
    T = lay.n_tokens
    lok, rok = lay.conv_tile_tables()
    n_tiles = T // CONV_TILE
    hb = CONV_TILE // HALO
    n_hb = T // HALO

    def cur(i, l, r):
        return (i, 0)

    def left(i, l, r):
        return (jnp.maximum(i * hb - 1, 0), 0)

    def right(i, l, r):
        return (jnp.minimum((i + 1) * hb, n_hb - 1), 0)

    def const(i, l, r):
        return (0, 0)

    gs = pltpu.PrefetchScalarGridSpec(
        num_scalar_prefetch=2,
        grid=(n_tiles,),
        in_specs=[pl.BlockSpec((CONV_TILE, D_CONV), cur),
                  pl.BlockSpec((HALO, D_CONV), left),
                  pl.BlockSpec((HALO, D_CONV), right),
                  pl.BlockSpec((CONV_TILE, D_XBC), cur),
                  pl.BlockSpec((HALO, D_XBC), left),
                  pl.BlockSpec((HALO, D_XBC), right),
                  pl.BlockSpec((CONV_K, D_CONV), const),
                  pl.BlockSpec((1, D_CONV), const),
                  pl.BlockSpec((1, D_CONV), const),
                  pl.BlockSpec((1, D_CONV), const),
                  pl.BlockSpec((SSM_CONV_K, D_XBC), const),
                  pl.BlockSpec((1, D_XBC), const)],
        out_specs=[pl.BlockSpec((CONV_TILE, D_CONV), cur),
                   pl.BlockSpec((CONV_TILE, D_XBC), cur)],
        scratch_shapes=[pltpu.VMEM((CONV_TILE + 2 * HALO, D_CONV), F32),
                        pltpu.VMEM((_N_SHIFT, _SHIFT_ROWS, D_CONV), F32),
                        pltpu.VMEM((CONV_TILE, D_CONV), F32),
                        pltpu.VMEM((CONV_TILE + 2 * HALO, D_XBC), F32)])
    return pl.pallas_call(
        _conv_kernel, grid_spec=gs,
        out_shape=(jax.ShapeDtypeStruct((T, D_CONV), BF16),
                   jax.ShapeDtypeStruct((T, D_XBC), BF16)),
        compiler_params=_cparams(("arbitrary",)),
        name="conv",
    )(lok, rok, u, u, u, xbc, xbc, xbc, conv_w, conv_b.reshape(1, -1), ln_g.reshape(1, -1),
      ln_b.reshape(1, -1), ssm_w, ssm_b.reshape(1, -1))


_BN = N_GROUPS * D_STATE


def _ssd_kernel(chunk_ref, yidx_ref, phase_ref, first_ref, last_ref, zero_ref, sin_ref, sout_ref, cloc_ref,
                xbc_ref, z_ref, pre_ref, h0f_ref, h0b_ref, dsk_ref, ng_ref,
                edec_ref, ewb_ref, ecol_ref, ewide_ref, eye3_ref,
                y_ref, hf_out_ref, hb_out_ref,
                hf_ref, g_ref, gin_ref, ybuf_ref):
    s = pl.program_id(0)
    phase = phase_ref[s]
    first = first_ref[s] == 1
    last = last_ref[s] == 1
    zero = zero_ref[s] == 1
    cloc = cloc_ref[s]
    H, P, N = N_HEADS, HEADDIM, D_STATE

    GW = HEADS_PER_GROUP * P
    xs = xbc_ref[:, 0:D_SSM]
    dt = pre_ref[:, 0:2 * H]
    acs = pre_ref[:, 2 * H:4 * H]
    a = pre_ref[:, 4 * H:6 * H]
    tot = acs[CHUNK - 8:CHUNK, :]
    dec = _expand_exact(jnp.exp(tot), edec_ref[...])[7:8, :]
    exb = acs[:, H:2 * H] - a[:, H:2 * H]

    def load_state(src_ref, dst_ref):
        for j in range(H // 2):
            pair = jnp.concatenate([src_ref[0, 2 * j], src_ref[0, 2 * j + 1]], axis=0)
            dst_ref[:, 2 * j * P:(2 * j + 2) * P] = jnp.where(zero, 0.0, pair.T)

    def store_state(src_ref, dst_ref):
        for j in range(H // 2):
            pair = src_ref[:, 2 * j * P:(2 * j + 2) * P].T
            dst_ref[0, 2 * j] = pair[0:P]
            dst_ref[0, 2 * j + 1] = pair[P:2 * P]

    @pl.when(phase == 0)
    def _backward_states():
        @pl.when(first)
        def _():
            load_state(h0b_ref, g_ref)

        wb = dt[:, H:2 * H] * jnp.exp(exb)
        xw = (xs.astype(F32) * _expand_exact(wb, ewb_ref[...])).astype(BF16)
        for g in range(N_GROUPS):
            cols = slice(g * GW, (g + 1) * GW)
            bg = xbc_ref[:, D_SSM + g * N:D_SSM + (g + 1) * N]
            gg = g_ref[:, cols]
            gin_ref[cloc, :, cols] = gg.astype(BF16)
            upd = lax.dot_general(bg, xw[:, cols], (((0,), (0,)), ((), ())), preferred_element_type=F32)
            g_ref[:, cols] = gg * dec[:, D_SSM + g * GW:D_SSM + (g + 1) * GW] + upd

        @pl.when(last)
        def _():
            store_state(g_ref, hb_out_ref)

    @pl.when(phase == 1)
    def _forward_and_outputs():
        @pl.when(first)
        def _():
            load_state(h0f_ref, hf_ref)

        acsf = acs[:, 0:H]
        dtf = dt[:, 0:H]
        dtb = dt[:, H:2 * H]
        totf = acs[CHUNK - 1:CHUNK, 0:H]
        totb = acs[CHUNK - 1:CHUNK, H:2 * H]
        col = _expand_exact(jnp.concatenate([acsf, exb], axis=1), ecol_ref[...])
        q3 = _split3(jnp.concatenate([acsf, exb, dtf, dtb], axis=1))
        qt = lax.dot_general(eye3_ref[...], q3, (((1,), (1,)), ((), ())),
                             preferred_element_type=F32)
        wide = jnp.concatenate([dtf * jnp.exp(totf - acsf), jnp.exp(acsf), jnp.exp(totb - exb)], axis=1)
        wide = _expand_exact(wide, ewide_ref[...])
        xsf = xs.astype(F32)
        xw = (xsf * wide[:, 0:D_SSM]).astype(BF16)
        lower = _iota((CHUNK, CHUNK), 1) <= _iota((CHUNK, CHUNK), 0)
        upper = _iota((CHUNK, CHUNK), 1) >= _iota((CHUNK, CHUNK), 0)
        for g in range(N_GROUPS):
            cols = slice(g * GW, (g + 1) * GW)
            bg = xbc_ref[:, D_SSM + g * N:D_SSM + (g + 1) * N]
            cg = xbc_ref[:, D_SSM + _BN + g * N:D_SSM + _BN + (g + 1) * N]
            cb = lax.dot_general(cg, bg, (((1,), (1,)), ((), ())), preferred_element_type=F32)
            hfg = hf_ref[:, cols]
            yf = jnp.dot(cg, hfg.astype(BF16), preferred_element_type=F32)
            yb = jnp.dot(cg, gin_ref[cloc, :, cols], preferred_element_type=F32)
            yg = yf * wide[:, D_SSM + g * GW:D_SSM + (g + 1) * GW] \
                + yb * wide[:, 2 * D_SSM + g * GW:2 * D_SSM + (g + 1) * GW]
            upd = lax.dot_general(bg, xw[:, cols], (((0,), (0,)), ((), ())), preferred_element_type=F32)
            hf_ref[:, cols] = hfg * dec[:, cols] + upd
            xg = xs[:, cols]
            head_of_lane = _iota((CHUNK, GW), 1) // P
            for r in range(HEADS_PER_GROUP):
                h = g * HEADS_PER_GROUP + r
                colf = col[:, h * N:(h + 1) * N]
                colb = col[:, (H + h) * N:(H + h + 1) * N]
                mf = jnp.where(lower, jnp.exp(colf - qt[h:h + 1, :]), 0.0) * qt[2 * H + h:2 * H + h + 1, :]
                mb = jnp.where(upper, jnp.exp(qt[H + h:H + h + 1, :] - colb), 0.0) * qt[3 * H + h:3 * H + h + 1, :]
                m = (cb * (mf + mb)).astype(BF16)
                xh = jnp.where(head_of_lane == r, xg, jnp.zeros_like(xg))
                yg = yg + jnp.dot(m, xh, preferred_element_type=F32)
            ybuf_ref[:, cols] = yg

        yt = (ybuf_ref[...] + dsk_ref[...] * xsf) * z_ref[...].astype(F32)
        gw = D_SSM // N_GROUPS
        for g in range(N_GROUPS):
            seg = yt[:, g * gw:(g + 1) * gw]
            ms = jnp.mean(seg * seg, axis=-1, keepdims=True)
            y_ref[:, g * gw:(g + 1) * gw] = (seg * lax.rsqrt(ms + LN_EPS) * ng_ref[:, g * gw:(g + 1) * gw]).astype(BF16)

        @pl.when(last)
        def _():
            store_state(hf_ref, hf_out_ref)


def _ssd(lay, xbc_c, z, pre, h0f, h0b, d_skip, norm_g):
    T = lay.n_tokens
    tabs = lay.ssd_step_tables()
    n_steps = tabs[0].shape[0]
    nsp = len(tabs)

    def by_chunk(s, *t):
        return (t[0][s], 0)

    def by_y(s, *t):
        return (t[1][s], 0)

    def by_sin(s, *t):
        return (t[6][s], 0, 0, 0)

    def by_sout(s, *t):
        return (t[7][s], 0, 0, 0)

    def const(s, *t):
        return (0, 0)

    H = N_HEADS
    eye3 = jnp.asarray(np.arange(4 * H)[:, None] == np.arange(12 * H)[None, :] % (4 * H), dtype=BF16)
    consts = [_expand3(2 * H, HEADDIM), _expand3(H, HEADDIM), _expand3(2 * H, D_STATE),
              _expand3(3 * H, HEADDIM), eye3]
    sshape = (1, N_HEADS, HEADDIM, D_STATE)
    gs = pltpu.PrefetchScalarGridSpec(
        num_scalar_prefetch=nsp,
        grid=(n_steps,),
        in_specs=[pl.BlockSpec((CHUNK, D_XBC), by_chunk),
                  pl.BlockSpec((CHUNK, D_SSM), by_chunk),
                  pl.BlockSpec((CHUNK, 128), by_chunk),
                  pl.BlockSpec(sshape, by_sin),
                  pl.BlockSpec(sshape, by_sin),
                  pl.BlockSpec((1, D_SSM), const),
                  pl.BlockSpec((1, D_SSM), const)] + [pl.BlockSpec(c.shape, const) for c in consts],
        out_specs=[pl.BlockSpec((CHUNK, D_SSM), by_y),
                   pl.BlockSpec(sshape, by_sout),
                   pl.BlockSpec(sshape, by_sout)],
        scratch_shapes=[pltpu.VMEM((D_STATE, D_SSM), F32),
                        pltpu.VMEM((D_STATE, D_SSM), F32),
                        pltpu.VMEM((lay.max_chunks, D_STATE, D_SSM), BF16),
                        pltpu.VMEM((CHUNK, D_SSM), F32)])
    n_out = lay.n_prompt_seqs
    return pl.pallas_call(
        _ssd_kernel, grid_spec=gs,
        out_shape=(jax.ShapeDtypeStruct((T, D_SSM), BF16),
                   jax.ShapeDtypeStruct((n_out,) + sshape[1:], F32),
                   jax.ShapeDtypeStruct((n_out,) + sshape[1:], F32)),
        compiler_params=_cparams(("arbitrary",)),
        name="ssd",
    )(*tabs, xbc_c, z, pre, h0f, h0b,
      jnp.repeat(d_skip, HEADDIM).reshape(1, -1), norm_g.reshape(1, -1), *consts)


def _outproj_kernel(row_ref, posb_ref, flag_ref, xp_ref, xs_ref, pos_ref, mod_ref, co_ref, ys_ref, wo_ref,
                    g_ref, b_ref, x1_ref, h2p_ref, *, alpha):
    i = pl.program_id(0)
    x = jnp.where(flag_ref[i] == 1, xs_ref[...] + pos_ref[...], xp_ref[...])
    r = row_ref[i]
    g1 = mod_ref[pl.ds(r, 1), 2 * D_MODEL:3 * D_MODEL]
    sh2 = mod_ref[pl.ds(r, 1), 3 * D_MODEL:4 * D_MODEL]
    sc2 = mod_ref[pl.ds(r, 1), 4 * D_MODEL:5 * D_MODEL]
    mix = jnp.dot(co_ref[...], wo_ref[0:D_CONV, :], preferred_element_type=F32) \
        + jnp.dot(ys_ref[...], wo_ref[D_CONV:, :], preferred_element_type=F32)
    x1 = _ln_rows(alpha * x + g1 * mix) * g_ref[...] + b_ref[...]
    x1_ref[...] = x1
    packed = _pack_halves(_ln_rows(x1) * (1.0 + sc2) + sh2)
    for c in range(ROW_PARTS):
        h2p_ref[c] = packed[:, c * PART_WORDS:(c + 1) * PART_WORDS]


def _outproj(lay, xp, xs, pos, mod, conv_out, y_ssm, w_out, ln_g, ln_b, alpha, tm):
    T = lay.n_tokens
    row, posb, flag = lay.token_tile_tables(tm)
    npt = lay.n_prompt_tokens // tm

    def const(i, r, p, f):
        return (0, 0)

    def cur(i, r, p, f):
        return (i, 0)

    gs = pltpu.PrefetchScalarGridSpec(
        num_scalar_prefetch=3,
        grid=(T // tm,),
        in_specs=[pl.BlockSpec((tm, D_MODEL), lambda i, r, p, f: (jnp.minimum(i, npt - 1), 0)),
                  pl.BlockSpec((tm, D_MODEL), lambda i, r, p, f: (jnp.maximum(i - npt, 0), 0)),
                  pl.BlockSpec((tm, D_MODEL), lambda i, r, p, f: (p[i], 0)),
                  pl.BlockSpec((8, 6 * D_MODEL), const),
                  pl.BlockSpec((tm, D_CONV), cur),
                  pl.BlockSpec((tm, D_SSM), cur),
                  pl.BlockSpec((D_CONV + D_SSM, D_MODEL), const),
                  pl.BlockSpec((1, D_MODEL), const),
                  pl.BlockSpec((1, D_MODEL), const)],
        out_specs=[pl.BlockSpec((tm, D_MODEL), cur),
                   pl.BlockSpec((ROW_PARTS, tm, PART_WORDS), lambda i, r, p, f: (0, i, 0))])
    return pl.pallas_call(
        functools.partial(_outproj_kernel, alpha=alpha), grid_spec=gs,
        out_shape=(jax.ShapeDtypeStruct((T, D_MODEL), F32),
                   jax.ShapeDtypeStruct((ROW_PARTS, T, PART_WORDS), jnp.uint32)),
        compiler_params=_cparams(("arbitrary",)),
        name="outproj",
    )(row, posb, flag, xp, xs, pos, mod, conv_out, y_ssm, w_out, ln_g.reshape(1, -1), ln_b.reshape(1, -1))


def _route_kernel(h2_ref, wrt_ref, bias_ref, idx_ref, wts_ref, pos_ref, cnt_ref, carry_ref, *, tm):
    i = pl.program_id(0)

    @pl.when(i == 0)
    def _():
        carry_ref[...] = jnp.zeros_like(carry_ref)

    E, NG, EG = N_EXPERTS, N_EXPERT_GROUPS, EXPERTS_PER_GROUP
    neg = -jnp.inf
    logits = lax.dot_general(wrt_ref[...], _unpack_rows(h2_ref), (((1,), (1,)), ((), ())),
                             preferred_element_type=F32)
    s = jax.nn.sigmoid(logits)
    sel = s + bias_ref[...]
    sel3 = sel.reshape(NG, EG, tm)
    io3 = _iota((NG, EG, tm), 1)
    m1 = jnp.max(sel3, axis=1, keepdims=True)
    f1 = jnp.min(jnp.where(sel3 == m1, io3, EG), axis=1, keepdims=True)
    m2 = jnp.max(jnp.where(io3 == f1, neg, sel3), axis=1, keepdims=True)
    gscore = (m1 + m2).reshape(NG, tm)
    gio = _iota((NG, tm), 0)
    beaten = jnp.zeros((NG, tm), I32)
    for g in range(NG):
        row = gscore[g:g + 1, :]
        beats = jnp.where(row > gscore, 1, jnp.where(row == gscore, jnp.where(g < gio, 1, 0), 0))
        beaten = beaten + beats
    keep = (beaten < TOPK_GROUPS).astype(F32).reshape(NG, 1, tm)
    selm = jnp.where(keep > 0.5, sel3, neg).reshape(E, tm)
    eio = _iota((E, tm), 0)
    chosen = jnp.zeros((E, tm), F32)
    idxs, ws = [], []
    for k in range(TOP_K):
        m = jnp.max(selm, axis=0, keepdims=True)
        am = jnp.minimum(jnp.min(jnp.where(selm == m, eio, E), axis=0, keepdims=True), E - 1)
        hit = eio == am
        ws.append(jnp.sum(jnp.where(hit, s, 0.0), axis=0, keepdims=True))
        idxs.append(am)
        selm = jnp.where(hit, neg, selm)
        chosen = jnp.where(hit, 1.0, chosen)
    wsum = ws[0]
    for k in range(1, TOP_K):
        wsum = wsum + ws[k]
    before = (_iota((tm, tm), 0) < _iota((tm, tm), 1)).astype(BF16)
    prior = jnp.dot(chosen.astype(BF16), before, preferred_element_type=F32)
    carry = carry_ref[...]
    prior = prior + jnp.concatenate([carry] * (tm // 128), axis=1)
    for k in range(TOP_K):
        idx_ref[k:k + 1, :] = idxs[k]
        wts_ref[k:k + 1, :] = ws[k] / wsum * ROUTED_SCALE
        pos_ref[k:k + 1, :] = jnp.sum(jnp.where(eio == idxs[k], prior, 0.0), axis=0, keepdims=True).astype(I32)
    total = jnp.dot(chosen.astype(BF16), jnp.ones((tm, 128), BF16), preferred_element_type=F32)
    carry = carry + total
    carry_ref[...] = carry
    cnt_ref[...] = carry.astype(I32)


def _route(h2p, w_router_t, router_bias, tm):
    T = h2p.shape[1]
    bias_b = jnp.broadcast_to(router_bias.astype(F32)[:, None], (N_EXPERTS, tm))
    w_router_t = w_router_t.astype(BF16)
    return pl.pallas_call(
        functools.partial(_route_kernel, tm=tm),
        grid=(T // tm,),
        in_specs=[pl.BlockSpec((ROW_PARTS, tm, PART_WORDS), lambda i: (0, i, 0)),
                  pl.BlockSpec((N_EXPERTS, D_MODEL), lambda i: (0, 0)),
                  pl.BlockSpec((N_EXPERTS, tm), lambda i: (0, 0))],
        out_specs=[pl.BlockSpec((TOP_K, tm), lambda i: (0, i)),
                   pl.BlockSpec((TOP_K, tm), lambda i: (0, i)),
                   pl.BlockSpec((TOP_K, tm), lambda i: (0, i)),
                   pl.BlockSpec((N_EXPERTS, 128), lambda i: (0, 0))],
        out_shape=(jax.ShapeDtypeStruct((TOP_K, T), I32),
                   jax.ShapeDtypeStruct((TOP_K, T), F32),
                   jax.ShapeDtypeStruct((TOP_K, T), I32),
                   jax.ShapeDtypeStruct((N_EXPERTS, 128), I32)),
        scratch_shapes=[pltpu.VMEM((N_EXPERTS, 128), F32)],
        compiler_params=_cparams(("arbitrary",)),
        name="route",
    )(h2p, w_router_t, bias_b)


def _dest_kernel(idx_ref, pos_ref, start_ref, dest_ref):
    tm = idx_ref.shape[1]
    eio = _iota((N_EXPERTS, tm), 0)
    start = start_ref[...]
    for k in range(TOP_K):
        base = jnp.sum(jnp.where(eio == idx_ref[k:k + 1, :], start, 0.0), axis=0, keepdims=True)
        dest_ref[k:k + 1, :] = base.astype(I32) + pos_ref[k:k + 1, :]


def _dest(idx, pos, pad_start, tm):
    T = idx.shape[1]
    start_b = jnp.broadcast_to(pad_start.astype(F32)[:, None], (N_EXPERTS, tm))
    return pl.pallas_call(
        _dest_kernel,
        grid=(T // tm,),
        in_specs=[pl.BlockSpec((TOP_K, tm), lambda i: (0, i)),
                  pl.BlockSpec((TOP_K, tm), lambda i: (0, i)),
                  pl.BlockSpec((N_EXPERTS, tm), lambda i: (0, 0))],
        out_specs=pl.BlockSpec((TOP_K, tm), lambda i: (0, i)),
        out_shape=jax.ShapeDtypeStruct((TOP_K, T), I32),
        compiler_params=_cparams(("arbitrary",)),
        name="dest",
    )(idx, pos, start_b)


_WEIGHT_FETCH_CHUNKS = 4
_W_SLOTS = 8
_X_SLOTS = 6
_Y_SLOTS = 4


def _expert_kernel(base_ref, nblk_ref, nused_ref, x_hbm, wgu_hbm, wd_hbm, y_hbm,
                   wgu_bf, wd_bf, wgu_stage, wd_stage, xbuf, ybuf, next_ref, sems, xsems, ysems):
    e = pl.program_id(0)
    n_used = nused_ref[0]

    def x_copy(b, slot):
        rows = pl.ds(pl.multiple_of(b * EXPERT_BLOCK, EXPERT_BLOCK), EXPERT_BLOCK)
        return pltpu.make_async_copy(x_hbm.at[:, rows, :], xbuf.at[slot], xsems.at[slot])

    def y_copy(b, slot):
        rows = pl.ds(pl.multiple_of(b * EXPERT_BLOCK, EXPERT_BLOCK), EXPERT_BLOCK)
        return pltpu.make_async_copy(ybuf.at[slot], y_hbm.at[:, rows, :], ysems.at[slot])

    def prefetch(b_first, max_starts):
        for _ in range(max_starts):
            j = next_ref[0]

            @pl.when((j <= b_first + _X_SLOTS - 1) & (j < n_used))
            def _():
                x_copy(j, lax.rem(j, _X_SLOTS)).start()
                next_ref[0] = j + 1

    def fetch(ex):
        ws = lax.rem(ex, _W_SLOTS)
        cps = []
        for c in range(_WEIGHT_FETCH_CHUNKS):
            rg = pl.ds(c * (D_MODEL // _WEIGHT_FETCH_CHUNKS), D_MODEL // _WEIGHT_FETCH_CHUNKS)
            rd = pl.ds(c * (D_EXPERT // _WEIGHT_FETCH_CHUNKS), D_EXPERT // _WEIGHT_FETCH_CHUNKS)
            cps.append(pltpu.make_async_copy(wgu_hbm.at[ex, rg], wgu_stage.at[ws, rg], sems.at[ws, 0]))
            cps.append(pltpu.make_async_copy(wd_hbm.at[ex, rd], wd_stage.at[ws, rd], sems.at[ws, 1]))
        return cps

    @pl.when(e == 0)
    def _():
        for ex in range(_W_SLOTS):
            for cp in fetch(ex):
                cp.start()
        next_ref[0] = 0
        prefetch(0, _X_SLOTS - 1)

    for cp in fetch(e):
        cp.wait()
    ws = lax.rem(e, _W_SLOTS)
    wgu_bf[...] = wgu_stage[ws].astype(BF16)
    wd_bf[...] = wd_stage[ws].astype(BF16)

    @pl.when(e + _W_SLOTS < N_EXPERTS)
    def _():
        for cp in fetch(e + _W_SLOTS):
            cp.start()

    def compute(b):
        xs = lax.rem(b, _X_SLOTS)
        parts = [_unpack_halves(xbuf[xs, c]) for c in range(ROW_PARTS)]
        chunks = [p[0] for p in parts] + [p[1] for p in parts]
        x = jnp.concatenate([xc.astype(BF16) for xc in chunks], axis=1)
        gu = jnp.dot(x, wgu_bf[...], preferred_element_type=F32)
        act = (_silu(gu[:, 0:D_EXPERT]) * gu[:, D_EXPERT:]).astype(BF16)
        packed = _pack_halves(jnp.dot(act, wd_bf[...], preferred_element_type=F32))
        for c in range(ROW_PARTS):
            ybuf[lax.rem(b, _Y_SLOTS), c] = packed[:, c * PART_WORDS:(c + 1) * PART_WORDS]

    def step(b, width):
        for d in range(width):
            x_copy(b + d, lax.rem(b + d, _X_SLOTS)).wait()
        prefetch(b, width)
        for d in range(width):
            @pl.when(b + d >= _Y_SLOTS)
            def _(d=d):
                y_copy(b + d - _Y_SLOTS, lax.rem(b + d, _Y_SLOTS)).wait()
        for d in range(width):
            compute(b + d)
        for d in range(width):
            y_copy(b + d, lax.rem(b + d, _Y_SLOTS)).start()

    lo = base_ref[e]
    n_pairs = nblk_ref[e] // 2

    def pair(i, carry):
        step(lo + 2 * i, 2)
        return carry

    lax.fori_loop(0, n_pairs, pair, 0)

    @pl.when(nblk_ref[e] % 2 == 1)
    def _():
        step(lo + 2 * n_pairs, 1)

    @pl.when(e == N_EXPERTS - 1)
    def _():
        for d in range(1, _Y_SLOTS + 1):
            @pl.when(n_used >= d)
            def _(d=d):
                y_copy(n_used - d, lax.rem(n_used - d, _Y_SLOTS)).wait()


def _pack_halves(x):
    n = x.shape[1] // 2
    hi = lax.bitcast_convert_type(x[:, :n].astype(BF16).astype(F32), jnp.uint32)
    lo = lax.bitcast_convert_type(x[:, n:].astype(BF16).astype(F32), jnp.uint32)
    return hi | (lo >> 16)


def _unpack_halves(p):
    hi = lax.bitcast_convert_type(p & jnp.uint32(0xFFFF0000), F32)
    lo = lax.bitcast_convert_type(p << 16, F32)
    return hi, lo


def _unpack_rows(ref):
    parts = [_unpack_halves(ref[c]) for c in range(ROW_PARTS)]
    chunks = [p[0] for p in parts] + [p[1] for p in parts]
    return jnp.concatenate([xc.astype(BF16) for xc in chunks], axis=1)


def _expert(x_sorted, w_gu, w_down, blk_base, blk_count, n_used):
    n_rows = x_sorted.shape[1]
    blk_shape = (ROW_PARTS, EXPERT_BLOCK, PART_WORDS)
    gs = pltpu.PrefetchScalarGridSpec(
        num_scalar_prefetch=3,
        grid=(N_EXPERTS,),
        in_specs=[pl.BlockSpec(memory_space=pl.ANY),
                  pl.BlockSpec(memory_space=pl.ANY),
                  pl.BlockSpec(memory_space=pl.ANY)],
        out_specs=pl.BlockSpec(memory_space=pl.ANY),
        scratch_shapes=[pltpu.VMEM((D_MODEL, 2 * D_EXPERT), BF16),
                        pltpu.VMEM((D_EXPERT, D_MODEL), BF16),
                        pltpu.VMEM((_W_SLOTS, D_MODEL, 2 * D_EXPERT), F32),
                        pltpu.VMEM((_W_SLOTS, D_EXPERT, D_MODEL), F32),
                        pltpu.VMEM((_X_SLOTS,) + blk_shape, jnp.uint32),
                        pltpu.VMEM((_Y_SLOTS,) + blk_shape, jnp.uint32),
                        pltpu.SMEM((1,), I32),
                        pltpu.SemaphoreType.DMA((_W_SLOTS, 2)),
                        pltpu.SemaphoreType.DMA((_X_SLOTS,)),
                        pltpu.SemaphoreType.DMA((_Y_SLOTS,))])
    return pl.pallas_call(
        _expert_kernel, grid_spec=gs,
        out_shape=jax.ShapeDtypeStruct((ROW_PARTS, n_rows, PART_WORDS), jnp.uint32),
        compiler_params=_cparams(("arbitrary",)),
        name="expert",
    )(blk_base, blk_count, n_used, x_sorted, w_gu, w_down)


def _combine_kernel(row_ref, h2_ref, x1_ref, wt_ref, mod_ref, wsg_ref, wsd_ref, g_ref, b_ref, yt_ref,
                    o_ref, *, first_tile, alpha):
    i = pl.program_id(0)
    h2 = _unpack_rows(h2_ref)
    su = jnp.dot(h2, wsg_ref[...], preferred_element_type=F32)
    act = (_silu(su[:, 0:D_SHARED]) * su[:, D_SHARED:]).astype(BF16)
    moe = jnp.dot(act, wsd_ref[...], preferred_element_type=F32)
    wt = wt_ref[...]
    his, los = [], []
    for c in range(ROW_PARTS):
        rh = jnp.zeros((h2.shape[0], PART_WORDS), F32)
        rl = jnp.zeros((h2.shape[0], PART_WORDS), F32)
        for k in range(TOP_K):
            hi, lo = _unpack_halves(yt_ref[c, k])
            w = wt[:, k:k + 1]
            rh = rh + hi * w
            rl = rl + lo * w
        his.append(rh)
        los.append(rl)
    moe = moe + jnp.concatenate(his + los, axis=1)
    g2 = mod_ref[pl.ds(row_ref[first_tile + i], 1), 5 * D_MODEL:6 * D_MODEL]
    o_ref[...] = _ln_rows(alpha * x1_ref[...] + g2 * moe) * g_ref[...] + b_ref[...]


def _combine(lay, h2, x1, wts_tok, mod, w_sh_gu, w_sh_down, ln_g, ln_b, y_tok, alpha, tm, first_token, n_tok):
    row, _, _ = lay.token_tile_tables(tm)
    first_tile = first_token // tm

    def cur(i, r):
        return (first_tile + i, 0)

    def const(i, r):
        return (0, 0)

    gs = pltpu.PrefetchScalarGridSpec(
        num_scalar_prefetch=1,
        grid=(n_tok // tm,),
        in_specs=[pl.BlockSpec((ROW_PARTS, tm, PART_WORDS), lambda i, r: (0, first_tile + i, 0)),
                  pl.BlockSpec((tm, D_MODEL), cur),
                  pl.BlockSpec((tm, TOP_K), cur),
                  pl.BlockSpec((8, 6 * D_MODEL), const),
                  pl.BlockSpec((D_MODEL, 2 * D_SHARED), const),
                  pl.BlockSpec((D_SHARED, D_MODEL), const),
                  pl.BlockSpec((1, D_MODEL), const),
                  pl.BlockSpec((1, D_MODEL), const),
                  pl.BlockSpec((ROW_PARTS, TOP_K, tm, PART_WORDS), lambda i, r: (0, 0, i, 0))],
        out_specs=pl.BlockSpec((tm, D_MODEL), lambda i, r: (i, 0)))
    return pl.pallas_call(
        functools.partial(_combine_kernel, first_tile=first_tile, alpha=alpha), grid_spec=gs,
        out_shape=jax.ShapeDtypeStruct((n_tok, D_MODEL), F32),
        compiler_params=_cparams(("arbitrary",)),
        name="combine",
    )(row, h2, x1, wts_tok, mod, w_sh_gu, w_sh_down, ln_g.reshape(1, -1), ln_b.reshape(1, -1), y_tok)


_GATHER_WINDOW = 128


def _sc_gather(table, idx):
    n, d = idx.shape[0], table.shape[1]
    mesh = plsc.VectorSubcoreMesh(core_axis_name="core", subcore_axis_name="subcore")

    @pl.kernel(out_type=jax.ShapeDtypeStruct((n, d), table.dtype), mesh=mesh)
    def gather_kernel(table_hbm, idx_hbm, out_hbm):
        def body(idx_vmem, out_vmem):
            pltpu.sync_copy(table_hbm.at[idx_vmem.at[0]], out_vmem)

        pltpu.emit_pipeline(
            body,
            grid=(n // _GATHER_WINDOW,),
            in_specs=[pl.BlockSpec((1, _GATHER_WINDOW), index_map=lambda i: (0, i))],
            out_specs=[pl.BlockSpec((_GATHER_WINDOW, d), index_map=lambda i: (i, 0))],
            core_axis_name=("core", "subcore"),
            dimension_semantics=(pltpu.PARALLEL,),
        )(idx_hbm, out_hbm)

    return gather_kernel(table, idx.reshape(1, n))


def _sc_scatter(rows, idx, n_out, repeat):
    n, d = rows.shape
    mesh = plsc.VectorSubcoreMesh(core_axis_name="core", subcore_axis_name="subcore")

    @pl.kernel(out_type=jax.ShapeDtypeStruct((n_out, d), rows.dtype), mesh=mesh, scratch_types=[])
    def scatter_kernel(rows_hbm, idx_hbm, out_hbm):
        def body(rows_vmem, idx_vmem):
            for r in range(repeat):
                pltpu.sync_copy(rows_vmem, out_hbm.at[idx_vmem.at[r]])

        pltpu.emit_pipeline(
            body,
            grid=(n // _GATHER_WINDOW,),
            in_specs=[pl.BlockSpec((_GATHER_WINDOW, d), index_map=lambda i: (i, 0)),
                      pl.BlockSpec((repeat, _GATHER_WINDOW), index_map=lambda i: (0, i))],
            out_specs=[],
            core_axis_name=("core", "subcore"),
            dimension_semantics=(pltpu.PARALLEL,),
        )(rows_hbm, idx_hbm)

    return scatter_kernel(rows, idx.reshape(repeat, n))


class _Layout:
    def __init__(self, n_prompt_seqs, prompt_len, n_sample_seqs, sample_len):
        self.n_prompt_seqs, self.prompt_len = n_prompt_seqs, prompt_len
        self.n_sample_seqs, self.sample_len = n_sample_seqs, sample_len
        self.n_prompt_tokens = n_prompt_seqs * prompt_len
        self.n_tokens = self.n_prompt_tokens + n_sample_seqs * sample_len
        assert prompt_len % CONV_TILE == 0 and sample_len % CONV_TILE == 0
        self.max_chunks = max(prompt_len, sample_len) // CHUNK

    def token_tile_tables(self, tm):
        assert self.n_prompt_tokens % tm == 0 and self.sample_len % tm == 0
        npt = self.n_prompt_tokens // tm
        per_seq = self.sample_len // tm
        n = self.n_tokens // tm
        row = np.zeros(n, np.int32)
        posb = np.zeros(n, np.int32)
        flag = np.zeros(n, np.int32)
        for i in range(npt, n):
            j = i - npt
            row[i] = 1 + j // per_seq
            posb[i] = j % per_seq
            flag[i] = 1
        return jnp.asarray(row), jnp.asarray(posb), jnp.asarray(flag)

    def conv_tile_tables(self):
        lok, rok = [], []
        for n_seq, length in ((self.n_prompt_seqs, self.prompt_len), (self.n_sample_seqs, self.sample_len)):
            per = length // CONV_TILE
            for _ in range(n_seq):
                for j in range(per):
                    lok.append(int(j > 0))
                    rok.append(int(j < per - 1))
        return jnp.asarray(np.array(lok, np.int32)), jnp.asarray(np.array(rok, np.int32))

    def ssd_step_tables(self):
        cols = [[] for _ in range(9)]
        seqs = []
        c0 = self.n_prompt_tokens // CHUNK
        for j in range(self.n_sample_seqs):
            nc = self.sample_len // CHUNK
            seqs.append((c0 + j * nc, nc, 0, j, 0))
        for j in range(self.n_prompt_seqs):
            nc = self.prompt_len // CHUNK
            seqs.append((j * nc, nc, 1, 0, j))
        for base, nc, zero, sin, sout in seqs:
            for phase in (0, 1):
                order = range(nc - 1, -1, -1) if phase == 0 else range(nc)
                for n, c in enumerate(order):
                    vals = (base + c, base if phase == 0 else base + c, phase, int(n == 0), int(n == nc - 1),
                            zero, sin, sout, c)
                    for col, v in zip(cols, vals):
                        col.append(v)
        return tuple(jnp.asarray(np.array(col, np.int32)) for col in cols)


def _grid_pos_embed(n_tokens):
    rows = n_tokens // GRID_W
    quarter = D_MODEL // 4
    freq = jnp.exp(-math.log(10000.0) * jnp.arange(quarter, dtype=F32) / quarter)
    r = jnp.broadcast_to(jnp.arange(rows, dtype=F32)[:, None, None] * freq, (rows, GRID_W, quarter))
    cl = jnp.broadcast_to(jnp.arange(GRID_W, dtype=F32)[None, :, None] * freq, (rows, GRID_W, quarter))
    emb = jnp.concatenate([jnp.sin(r), jnp.cos(r), jnp.sin(cl), jnp.cos(cl)], axis=-1)
    return emb.reshape(rows * GRID_W, D_MODEL)


def _moe_plan(counts):
    blk_count = (counts + EXPERT_BLOCK - 1) // EXPERT_BLOCK
    blk_end = jnp.cumsum(blk_count)
    blk_base = blk_end - blk_count
    return ((blk_base * EXPERT_BLOCK).astype(I32), blk_base.astype(I32), blk_count.astype(I32),
            blk_end[-1:].astype(I32))


def _layer(lay, xp, xs, pos, cond8, h0f, h0b, lp, alpha, tm_proj=512, tm_route=256, tm_comb=256):
    (w_ada, b_ada, w_in, conv_w, conv_b, conv_ln_g, conv_ln_b, ssm_conv_w, ssm_conv_b, dt_bias, a_log,
     d_skip, ssm_norm_g, w_out, ln1_g, ln1_b, w_router, router_bias, w_exp_gu, w_exp_down, w_sh_gu,
     w_sh_down, ln2_g, ln2_b) = lp
    T = lay.n_tokens
    n_main = 2 * D_CONV + D_SSM + D_XBC
    w_main = w_in[:, :n_main].astype(BF16)
    w_dt = jnp.pad(w_in[:, n_main:], ((0, 0), (0, 128 - 2 * N_HEADS))).astype(BF16)

    mod = _ada(cond8, w_ada, b_ada)
    u, z, xbc, pre = _inproj(lay, xp, xs, pos, mod, w_main, w_dt, dt_bias, a_log, tm_proj)
    conv_out, xbc_c = _conv(lay, u, xbc, conv_w, conv_b, conv_ln_g, conv_ln_b, ssm_conv_w, ssm_conv_b)
    y_ssm, hf, hb = _ssd(lay, xbc_c, z, pre, h0f, h0b, d_skip, ssm_norm_g)
    x1, h2p = _outproj(lay, xp, xs, pos, mod, conv_out, y_ssm, w_out.astype(BF16), ln1_g, ln1_b, alpha, tm_proj)

    idx, wts, posn, cnt = _route(h2p, w_router.T, router_bias, tm_route)
    n_blocks = -(-T * TOP_K // EXPERT_BLOCK) + N_EXPERTS
    pad_start, blk_base, blk_count, n_used = _moe_plan(cnt[:, 0])
    dest2 = _dest(idx, posn, pad_start, 512)
    n_rows = n_blocks * EXPERT_BLOCK
    scatter_idx = jnp.concatenate([dest2 + c * n_rows for c in range(ROW_PARTS)], axis=1)
    x_sorted = _sc_scatter(h2p.reshape(ROW_PARTS * T, PART_WORDS), scatter_idx.reshape(-1),
                           ROW_PARTS * n_rows, TOP_K).reshape(ROW_PARTS, n_rows, PART_WORDS)
    y_sorted = _expert(x_sorted, w_exp_gu, w_exp_down, blk_base, blk_count, n_used)
    y_flat = y_sorted.reshape(ROW_PARTS * n_rows, PART_WORDS)
    wts_tok, wsg, wsd = wts.T, w_sh_gu.astype(BF16), w_sh_down.astype(BF16)
    outs = []
    for first, n_tok in ((0, lay.n_prompt_tokens), (lay.n_prompt_tokens, T - lay.n_prompt_tokens)):
        dest_g = dest2[:, first:first + n_tok].reshape(-1)
        idx_parts = jnp.concatenate([dest_g + c * n_rows for c in range(ROW_PARTS)])
        y_tok = _sc_gather(y_flat, idx_parts).reshape(ROW_PARTS, TOP_K, n_tok, PART_WORDS)
        outs.append(_combine(lay, h2p, x1, wts_tok, mod, wsg, wsd, ln2_g, ln2_b, y_tok, alpha, tm_comb,
                             first, n_tok))
    return outs[0], outs[1], hf, hb


def kernel(x_prompt, x_sample, state_ssd_fwd, state_ssd_bwd, c, c_ctx, w_ada, b_ada, w_in, conv_w, conv_b, conv_ln_g, conv_ln_b, ssm_conv_w, ssm_conv_b, dt_bias, a_log, d_skip, ssm_norm_g, w_out, ln1_g, ln1_b, w_router, router_bias, w_exp_gu, w_exp_down, w_sh_gu, w_sh_down, ln2_g, ln2_b):
    depth = w_ada.shape[0]
    assert depth == 1, "the prompt and latent passes are fused per layer; one layer is supported"
    bp, lp_, _ = x_prompt.shape
    bd, ld, _ = x_sample.shape
    lay = _Layout(bp, lp_, bd, ld)
    alpha = (2.0 * depth) ** 0.25
    stacked = (w_ada, b_ada, w_in, conv_w, conv_b, conv_ln_g, conv_ln_b, ssm_conv_w, ssm_conv_b,
               dt_bias, a_log, d_skip, ssm_norm_g, w_out, ln1_g, ln1_b, w_router, router_bias,
               w_exp_gu, w_exp_down, w_sh_gu, w_sh_down, ln2_g, ln2_b)
    lp = [w[0] for w in stacked]
    cond8 = jnp.concatenate([c_ctx[None, :], c, jnp.zeros((8 - 1 - bd, D_MODEL), F32)], axis=0)
    pos = _grid_pos_embed(ld)
    sshape = (bd, N_HEADS, HEADDIM, D_STATE)
    out_p, out_s, hf, hb = _layer(lay, x_prompt.reshape(bp * lp_, D_MODEL), x_sample.reshape(bd * ld, D_MODEL),
                                  pos, cond8, state_ssd_fwd[:, 0].reshape(sshape),
                                  state_ssd_bwd[:, 0].reshape(sshape), lp, alpha)
    return (out_p.reshape(bp, lp_, D_MODEL), out_s.reshape(bd, ld, D_MODEL),
            hf[:, None], hb[:, None])
```

```python
import functools
import math

import numpy as np
import jax
import jax.numpy as jnp
from jax import lax
from jax.experimental import pallas as pl
from jax.experimental.pallas import tpu as pltpu
from jax.experimental.pallas import tpu_sc as plsc

F32 = jnp.float32
BF16 = jnp.bfloat16
I32 = jnp.int32
HI = lax.Precision.HIGHEST

D_MODEL = 1024
GRID_W = 64
D_CONV = 1024
CONV_K = 31
N_HEADS = 16
HEADDIM = 64
D_SSM = N_HEADS * HEADDIM
N_GROUPS = 4
HEADS_PER_GROUP = N_HEADS // N_GROUPS
D_STATE = 128
SSM_CONV_K = 4
CHUNK = 128
D_XBC = D_SSM + 2 * N_GROUPS * D_STATE
N_EXPERTS = 256
TOP_K = 8
N_EXPERT_GROUPS = 8
EXPERTS_PER_GROUP = N_EXPERTS // N_EXPERT_GROUPS
TOPK_GROUPS = 4
D_EXPERT = 256
D_SHARED = 256
ROUTED_SCALE = 2.5
LN_EPS = 1e-5

CONV_TILE = 256
HALO = 16
EXPERT_BLOCK = 256
ROW_PARTS = 2
PART_WORDS = D_MODEL // 2 // ROW_PARTS
VMEM_LIMIT = 56 * 1024 * 1024


def _cparams(sem, vmem=VMEM_LIMIT):
    return pltpu.CompilerParams(dimension_semantics=sem, vmem_limit_bytes=vmem)


def _silu(x):
    return x * jax.nn.sigmoid(x)


def _ln_rows(x):
    mu = jnp.mean(x, axis=-1, keepdims=True)
    xc = x - mu
    var = jnp.mean(xc * xc, axis=-1, keepdims=True)
    return xc * lax.rsqrt(var + LN_EPS)


def _iota(shape, dim):
    return lax.broadcasted_iota(I32, shape, dim)


def _expand_matrix(n_in, width):
    return (_iota((n_in, n_in * width), 0) == _iota((n_in, n_in * width), 1) // width).astype(F32)


def _dot_hi(a, b):
    return jnp.dot(a, b, precision=HI, preferred_element_type=F32)


def _split3(x):
    hi = x.astype(BF16)
    r1 = x - hi.astype(F32)
    mid = r1.astype(BF16)
    lo = (r1 - mid.astype(F32)).astype(BF16)
    return jnp.concatenate([hi, mid, lo], axis=1)


def _expand3(n, width):
    rows = np.arange(3 * n)[:, None] % n
    cols = np.arange(n * width)[None, :] // width
    return jnp.asarray(rows == cols, dtype=BF16)


def _expand_exact(x, e3):
    return jnp.dot(_split3(x), e3, preferred_element_type=F32)


def _ada_kernel(c_ref, w_ref, b_ref, o_ref):
    o_ref[...] = _dot_hi(_silu(c_ref[...]), w_ref[...]) + b_ref[...]


def _ada(cond8, w_ada, b_ada):
    n = w_ada.shape[1]
    tn = 1024
    return pl.pallas_call(
        _ada_kernel,
        grid=(n // tn,),
        in_specs=[pl.BlockSpec((8, D_MODEL), lambda j: (0, 0)),
                  pl.BlockSpec((D_MODEL, tn), lambda j: (0, j)),
                  pl.BlockSpec((1, tn), lambda j: (0, j))],
        out_specs=pl.BlockSpec((8, tn), lambda j: (0, j)),
        out_shape=jax.ShapeDtypeStruct((8, n), F32),
        compiler_params=_cparams(("arbitrary",)),
        name="ada",
    )(cond8, w_ada, b_ada.reshape(1, n))


def _inproj_kernel(row_ref, posb_ref, flag_ref, xp_ref, xs_ref, pos_ref, mod_ref, wm_ref, wdt_ref,
                   dtb_ref, alog_ref, tri_ref, u_ref, z_ref, xbc_ref, pre_ref):
    i = pl.program_id(0)
    x = jnp.where(flag_ref[i] == 1, xs_ref[...] + pos_ref[...], xp_ref[...])
    r = row_ref[i]
    sh1 = mod_ref[pl.ds(r, 1), 0:D_MODEL]
    sc1 = mod_ref[pl.ds(r, 1), D_MODEL:2 * D_MODEL]
    h = (_ln_rows(x) * (1.0 + sc1) + sh1).astype(BF16)
    nh2 = 2 * N_HEADS
    dt = jnp.dot(h, wdt_ref[...], preferred_element_type=F32)[:, 0:nh2] + dtb_ref[...]
    dt = jnp.maximum(dt, 0.0) + jnp.log1p(jnp.exp(-jnp.abs(dt)))
    a = dt * (-jnp.exp(alog_ref[...]))
    a_split = _split3(a)
    glu_a = jnp.dot(h, wm_ref[:, 0:D_CONV], preferred_element_type=F32)
    glu_g = jnp.dot(h, wm_ref[:, D_CONV:2 * D_CONV], preferred_element_type=F32)
    u_ref[...] = (glu_a * jax.nn.sigmoid(glu_g)).astype(BF16)
    a3 = jnp.dot(tri_ref[...], a_split, preferred_element_type=F32)
    z = jnp.dot(h, wm_ref[:, 2 * D_CONV:2 * D_CONV + D_SSM], preferred_element_type=F32)
    z_ref[...] = _silu(z).astype(BF16)
    xbc_ref[...] = jnp.dot(h, wm_ref[:, 2 * D_CONV + D_SSM:], preferred_element_type=F32).astype(BF16)
    acs = a3[:, 0:nh2] + a3[:, nh2:2 * nh2] + a3[:, 2 * nh2:3 * nh2]
    pre_ref[...] = jnp.concatenate([dt, acs, a, jnp.zeros((dt.shape[0], 128 - 3 * nh2), F32)], axis=1)


def _inproj(lay, xp, xs, pos, mod, w_main, w_dt, dt_bias, a_log, tm):
    T = lay.n_tokens
    row, posb, flag = lay.token_tile_tables(tm)
    npt = lay.n_prompt_tokens // tm
    n_main = w_main.shape[1]
    chunk_of = np.arange(tm) // CHUNK
    tri = jnp.asarray((chunk_of[:, None] == chunk_of[None, :]) & np.tril(np.ones((tm, tm), bool)), dtype=BF16)
    gs = pltpu.PrefetchScalarGridSpec(
        num_scalar_prefetch=3,
        grid=(T // tm,),
        in_specs=[pl.BlockSpec((tm, D_MODEL), lambda i, r, p, f: (jnp.minimum(i, npt - 1), 0)),
                  pl.BlockSpec((tm, D_MODEL), lambda i, r, p, f: (jnp.maximum(i - npt, 0), 0)),
                  pl.BlockSpec((tm, D_MODEL), lambda i, r, p, f: (p[i], 0)),
                  pl.BlockSpec((8, 6 * D_MODEL), lambda i, r, p, f: (0, 0)),
                  pl.BlockSpec((D_MODEL, n_main), lambda i, r, p, f: (0, 0)),
                  pl.BlockSpec((D_MODEL, 128), lambda i, r, p, f: (0, 0)),
                  pl.BlockSpec((1, 2 * N_HEADS), lambda i, r, p, f: (0, 0)),
                  pl.BlockSpec((1, 2 * N_HEADS), lambda i, r, p, f: (0, 0)),
                  pl.BlockSpec((tm, tm), lambda i, r, p, f: (0, 0))],
        out_specs=[pl.BlockSpec((tm, D_CONV), lambda i, r, p, f: (i, 0)),
                   pl.BlockSpec((tm, D_SSM), lambda i, r, p, f: (i, 0)),
                   pl.BlockSpec((tm, D_XBC), lambda i, r, p, f: (i, 0)),
                   pl.BlockSpec((tm, 128), lambda i, r, p, f: (i, 0))])
    return pl.pallas_call(
        _inproj_kernel, grid_spec=gs,
        out_shape=(jax.ShapeDtypeStruct((T, D_CONV), BF16),
                   jax.ShapeDtypeStruct((T, D_SSM), BF16),
                   jax.ShapeDtypeStruct((T, D_XBC), BF16),
                   jax.ShapeDtypeStruct((T, 128), F32)),
        compiler_params=_cparams(("arbitrary",)),
        name="inproj",
    )(row, posb, flag, xp, xs, pos, mod, w_main, w_dt, dt_bias.reshape(1, -1), a_log.reshape(1, -1), tri)


_N_SHIFT = 8
_SHIFT_ROWS = CONV_TILE + 2 * HALO - _N_SHIFT
_ROW_BLOCK = 64
_FILL_ROWS = 32
_SSM_ROWS, _SSM_LANES = 64, 256


def _conv_kernel(lok_ref, rok_ref, u_ref, ul_ref, ur_ref, xbc_ref, xbcl_ref, xbcr_ref,
                 cw_ref, cb_ref, lng_ref, lnb_ref, sw_ref, sb_ref, co_ref, xo_ref,
                 ext_ref, sh_ref, acc_ref, ext2_ref):
    i = pl.program_id(0)
    lok = lok_ref[i] == 1
    rok = rok_ref[i] == 1

    def fill_ext(rb, carry):
        r0 = pl.multiple_of(rb * _FILL_ROWS, _FILL_ROWS)
        dst = pl.ds(pl.multiple_of(HALO + r0, HALO), _FILL_ROWS)
        ext_ref[dst, :] = u_ref[pl.ds(r0, _FILL_ROWS), :].astype(F32)
        ext2_ref[dst, :] = xbc_ref[pl.ds(r0, _FILL_ROWS), :].astype(F32)
        return carry

    ext_ref[0:HALO, :] = jnp.where(lok, ul_ref[...].astype(F32), 0.0)
    ext_ref[HALO + CONV_TILE:, :] = jnp.where(rok, ur_ref[...].astype(F32), 0.0)
    ext2_ref[0:HALO, :] = jnp.where(lok, xbcl_ref[...].astype(F32), 0.0)
    ext2_ref[HALO + CONV_TILE:, :] = jnp.where(rok, xbcr_ref[...].astype(F32), 0.0)
    lax.fori_loop(0, CONV_TILE // _FILL_ROWS, fill_ext, 0)
    for r in range(_N_SHIFT):
        sh_ref[r] = ext_ref[r:r + _SHIFT_ROWS, :]

    first = HALO - (CONV_K - 1) // 2

    for j in range(D_CONV // 128):
        lanes = slice(j * 128, (j + 1) * 128)
        taps = [jnp.broadcast_to(cw_ref[k:k + 1, lanes], (8, 128)) for k in range(CONV_K)]
        bias = jnp.broadcast_to(cb_ref[:, lanes], (8, 128))

        def row_block(rb, carry, lanes=lanes, taps=taps, bias=bias):
            base = pl.multiple_of(rb * _ROW_BLOCK, _ROW_BLOCK)
            for sub in range(_ROW_BLOCK // 8):
                acc = bias
                for k in range(CONV_K):
                    o = first + k
                    row0 = base + (o // _N_SHIFT) * _N_SHIFT + sub * 8
                    acc = acc + sh_ref[o % _N_SHIFT, pl.ds(row0, 8), lanes] * taps[k]
                acc_ref[pl.ds(base + sub * 8, 8), lanes] = acc
            return carry

        lax.fori_loop(0, CONV_TILE // _ROW_BLOCK, row_block, 0)
    u = _ln_rows(acc_ref[...]) * lng_ref[...] + lnb_ref[...]
    co_ref[...] = _silu(u).astype(BF16)

    first2 = HALO - (SSM_CONV_K - 1) // 2
    for rb in range(CONV_TILE // _SSM_ROWS):
        for lc in range(D_XBC // _SSM_LANES):
            lanes = slice(lc * _SSM_LANES, (lc + 1) * _SSM_LANES)
            y = jnp.zeros((_SSM_ROWS, _SSM_LANES), F32) + sb_ref[:, lanes]
            for k in range(SSM_CONV_K):
                r0 = first2 + k + rb * _SSM_ROWS
                y = y + ext2_ref[r0:r0 + _SSM_ROWS, lanes] * sw_ref[k:k + 1, lanes]
            xo_ref[rb * _SSM_ROWS:(rb + 1) * _SSM_ROWS, lanes] = _silu(y).astype(BF16)


def _conv(lay, u, xbc, conv_w, conv_b, ln_g, ln_b, ssm_w, ssm_b):
    T = lay.n_tokens
    lok, rok = lay.conv_tile_tables()
    n_tiles = T // CONV_TILE
    hb = CONV_TILE // HALO
    n_hb = T // HALO

    def cur(i, l, r):
        return (i, 0)

    def left(i, l, r):
        return (jnp.maximum(i * hb - 1, 0), 0)

    def right(i, l, r):
        return (jnp.minimum((i + 1) * hb, n_hb - 1), 0)

    def const(i, l, r):
        return (0, 0)

    gs = pltpu.PrefetchScalarGridSpec(
        num_scalar_prefetch=2,
        grid=(n_tiles,),
        in_specs=[pl.BlockSpec((CONV_TILE, D_CONV), cur),
                  pl.BlockSpec((HALO, D_CONV), left),
                  pl.BlockSpec((HALO, D_CONV), right),
                  pl.BlockSpec((CONV_TILE, D_XBC), cur),
                  pl.BlockSpec((HALO, D_XBC), left),
                  pl.BlockSpec((HALO, D_XBC), right),
                  pl.BlockSpec((CONV_K, D_CONV), const),
                  pl.BlockSpec((1, D_CONV), const),
                  pl.BlockSpec((1, D_CONV), const),
                  pl.BlockSpec((1, D_CONV), const),
                  pl.BlockSpec((SSM_CONV_K, D_XBC), const),
                  pl.BlockSpec((1, D_XBC), const)],
        out_specs=[pl.BlockSpec((CONV_TILE, D_CONV), cur),
                   pl.BlockSpec((CONV_TILE, D_XBC), cur)],
        scratch_shapes=[pltpu.VMEM((CONV_TILE + 2 * HALO, D_CONV), F32),
                        pltpu.VMEM((_N_SHIFT, _SHIFT_ROWS, D_CONV), F32),
                        pltpu.VMEM((CONV_TILE, D_CONV), F32),
                        pltpu.VMEM((CONV_TILE + 2 * HALO, D_XBC), F32)])
    return pl.pallas_call(
        _conv_kernel, grid_spec=gs,
        out_shape=(jax.ShapeDtypeStruct((T, D_CONV), BF16),
                   jax.ShapeDtypeStruct((T, D_XBC), BF16)),
        compiler_params=_cparams(("arbitrary",)),
        name="conv",
    )(lok, rok, u, u, u, xbc, xbc, xbc, conv_w, conv_b.reshape(1, -1), ln_g.reshape(1, -1),
      ln_b.reshape(1, -1), ssm_w, ssm_b.reshape(1, -1))


_BN = N_GROUPS * D_STATE


def _ssd_kernel(chunk_ref, yidx_ref, phase_ref, first_ref, last_ref, zero_ref, sin_ref, sout_ref, cloc_ref,
                xbc_ref, z_ref, pre_ref, h0f_ref, h0b_ref, dsk_ref, ng_ref,
                edec_ref, ewb_ref, ecol_ref, ewide_ref, eye3_ref,
                y_ref, hf_out_ref, hb_out_ref,
                hf_ref, g_ref, gin_ref, ybuf_ref):
    s = pl.program_id(0)
    phase = phase_ref[s]
    first = first_ref[s] == 1
    last = last_ref[s] == 1
    zero = zero_ref[s] == 1
    cloc = cloc_ref[s]
    H, P, N = N_HEADS, HEADDIM, D_STATE

    GW = HEADS_PER_GROUP * P
    xs = xbc_ref[:, 0:D_SSM]
    dt = pre_ref[:, 0:2 * H]
    acs = pre_ref[:, 2 * H:4 * H]
    a = pre_ref[:, 4 * H:6 * H]
    tot = acs[CHUNK - 8:CHUNK, :]
    dec = _expand_exact(jnp.exp(tot), edec_ref[...])[7:8, :]
    exb = acs[:, H:2 * H] - a[:, H:2 * H]

    def load_state(src_ref, dst_ref):
        for j in range(H // 2):
            pair = jnp.concatenate([src_ref[0, 2 * j], src_ref[0, 2 * j + 1]], axis=0)
            dst_ref[:, 2 * j * P:(2 * j + 2) * P] = jnp.where(zero, 0.0, pair.T)

    def store_state(src_ref, dst_ref):
        for j in range(H // 2):
            pair = src_ref[:, 2 * j * P:(2 * j + 2) * P].T
            dst_ref[0, 2 * j] = pair[0:P]
            dst_ref[0, 2 * j + 1] = pair[P:2 * P]

    @pl.when(phase == 0)
    def _backward_states():
        @pl.when(first)
        def _():
            load_state(h0b_ref, g_ref)

        wb = dt[:, H:2 * H] * jnp.exp(exb)
        xw = (xs.astype(F32) * _expand_exact(wb, ewb_ref[...])).astype(BF16)
        for g in range(N_GROUPS):
            cols = slice(g * GW, (g + 1) * GW)
            bg = xbc_ref[:, D_SSM + g * N:D_SSM + (g + 1) * N]
            gg = g_ref[:, cols]
            gin_ref[cloc, :, cols] = gg.astype(BF16)
            upd = lax.dot_general(bg, xw[:, cols], (((0,), (0,)), ((), ())), preferred_element_type=F32)
            g_ref[:, cols] = gg * dec[:, D_SSM + g * GW:D_SSM + (g + 1) * GW] + upd

        @pl.when(last)
        def _():
            store_state(g_ref, hb_out_ref)

    @pl.when(phase == 1)
    def _forward_and_outputs():
        @pl.when(first)
        def _():
            load_state(h0f_ref, hf_ref)

        acsf = acs[:, 0:H]
        dtf = dt[:, 0:H]
        dtb = dt[:, H:2 * H]
        totf = acs[CHUNK - 1:CHUNK, 0:H]
        totb = acs[CHUNK - 1:CHUNK, H:2 * H]
        col = _expand_exact(jnp.concatenate([acsf, exb], axis=1), ecol_ref[...])
        q3 = _split3(jnp.concatenate([acsf, exb, dtf, dtb], axis=1))
        qt = lax.dot_general(eye3_ref[...], q3, (((1,), (1,)), ((), ())),
                             preferred_element_type=F32)
        wide = jnp.concatenate([dtf * jnp.exp(totf - acsf), jnp.exp(acsf), jnp.exp(totb - exb)], axis=1)
        wide = _expand_exact(wide, ewide_ref[...])
        xsf = xs.astype(F32)
        xw = (xsf * wide[:, 0:D_SSM]).astype(BF16)
        lower = _iota((CHUNK, CHUNK), 1) <= _iota((CHUNK, CHUNK), 0)
        upper = _iota((CHUNK, CHUNK), 1) >= _iota((CHUNK, CHUNK), 0)
        for g in range(N_GROUPS):
            cols = slice(g * GW, (g + 1) * GW)
            bg = xbc_ref[:, D_SSM + g * N:D_SSM + (g + 1) * N]
            cg = xbc_ref[:, D_SSM + _BN + g * N:D_SSM + _BN + (g + 1) * N]
            cb = lax.dot_general(cg, bg, (((1,), (1,)), ((), ())), preferred_element_type=F32)
            hfg = hf_ref[:, cols]
            yf = jnp.dot(cg, hfg.astype(BF16), preferred_element_type=F32)
            yb = jnp.dot(cg, gin_ref[cloc, :, cols], preferred_element_type=F32)
            yg = yf * wide[:, D_SSM + g * GW:D_SSM + (g + 1) * GW] \
                + yb * wide[:, 2 * D_SSM + g * GW:2 * D_SSM + (g + 1) * GW]
            upd = lax.dot_general(bg, xw[:, cols], (((0,), (0,)), ((), ())), preferred_element_type=F32)
            hf_ref[:, cols] = hfg * dec[:, cols] + upd
            xg = xs[:, cols]
            head_of_lane = _iota((CHUNK, GW), 1) // P
            for r in range(HEADS_PER_GROUP):
                h = g * HEADS_PER_GROUP + r
                colf = col[:, h * N:(h + 1) * N]
                colb = col[:, (H + h) * N:(H + h + 1) * N]
                mf = jnp.where(lower, jnp.exp(colf - qt[h:h + 1, :]), 0.0) * qt[2 * H + h:2 * H + h + 1, :]
                mb = jnp.where(upper, jnp.exp(qt[H + h:H + h + 1, :] - colb), 0.0) * qt[3 * H + h:3 * H + h + 1, :]
                m = (cb * (mf + mb)).astype(BF16)
                xh = jnp.where(head_of_lane == r, xg, jnp.zeros_like(xg))
                yg = yg + jnp.dot(m, xh, preferred_element_type=F32)
            ybuf_ref[:, cols] = yg

        yt = (ybuf_ref[...] + dsk_ref[...] * xsf) * z_ref[...].astype(F32)
        gw = D_SSM // N_GROUPS
        for g in range(N_GROUPS):
            seg = yt[:, g * gw:(g + 1) * gw]
            ms = jnp.mean(seg * seg, axis=-1, keepdims=True)
            y_ref[:, g * gw:(g + 1) * gw] = (seg * lax.rsqrt(ms + LN_EPS) * ng_ref[:, g * gw:(g + 1) * gw]).astype(BF16)

        @pl.when(last)
        def _():
            store_state(hf_ref, hf_out_ref)


def _ssd(lay, xbc_c, z, pre, h0f, h0b, d_skip, norm_g):
    T = lay.n_tokens
    tabs = lay.ssd_step_tables()
    n_steps = tabs[0].shape[0]
    nsp = len(tabs)

    def by_chunk(s, *t):
        return (t[0][s], 0)

    def by_y(s, *t):
        return (t[1][s], 0)

    def by_sin(s, *t):
        return (t[6][s], 0, 0, 0)

    def by_sout(s, *t):
        return (t[7][s], 0, 0, 0)

    def const(s, *t):
        return (0, 0)

    H = N_HEADS
    eye3 = jnp.asarray(np.arange(4 * H)[:, None] == np.arange(12 * H)[None, :] % (4 * H), dtype=BF16)
    consts = [_expand3(2 * H, HEADDIM), _expand3(H, HEADDIM), _expand3(2 * H, D_STATE),
              _expand3(3 * H, HEADDIM), eye3]
    sshape = (1, N_HEADS, HEADDIM, D_STATE)
    gs = pltpu.PrefetchScalarGridSpec(
        num_scalar_prefetch=nsp,
        grid=(n_steps,),
        in_specs=[pl.BlockSpec((CHUNK, D_XBC), by_chunk),
                  pl.BlockSpec((CHUNK, D_SSM), by_chunk),
                  pl.BlockSpec((CHUNK, 128), by_chunk),
                  pl.BlockSpec(sshape, by_sin),
                  pl.BlockSpec(sshape, by_sin),
                  pl.BlockSpec((1, D_SSM), const),
                  pl.BlockSpec((1, D_SSM), const)] + [pl.BlockSpec(c.shape, const) for c in consts],
        out_specs=[pl.BlockSpec((CHUNK, D_SSM), by_y),
                   pl.BlockSpec(sshape, by_sout),
                   pl.BlockSpec(sshape, by_sout)],
        scratch_shapes=[pltpu.VMEM((D_STATE, D_SSM), F32),
                        pltpu.VMEM((D_STATE, D_SSM), F32),
                        pltpu.VMEM((lay.max_chunks, D_STATE, D_SSM), BF16),
                        pltpu.VMEM((CHUNK, D_SSM), F32)])
    n_out = lay.n_prompt_seqs
    return pl.pallas_call(
        _ssd_kernel, grid_spec=gs,
        out_shape=(jax.ShapeDtypeStruct((T, D_SSM), BF16),
                   jax.ShapeDtypeStruct((n_out,) + sshape[1:], F32),
                   jax.ShapeDtypeStruct((n_out,) + sshape[1:], F32)),
        compiler_params=_cparams(("arbitrary",)),
        name="ssd",
    )(*tabs, xbc_c, z, pre, h0f, h0b,
      jnp.repeat(d_skip, HEADDIM).reshape(1, -1), norm_g.reshape(1, -1), *consts)


def _outproj_kernel(row_ref, posb_ref, flag_ref, xp_ref, xs_ref, pos_ref, mod_ref, co_ref, ys_ref, wo_ref,
                    g_ref, b_ref, x1_ref, h2p_ref, *, alpha):
    i = pl.program_id(0)
    x = jnp.where(flag_ref[i] == 1, xs_ref[...] + pos_ref[...], xp_ref[...])
    r = row_ref[i]
    g1 = mod_ref[pl.ds(r, 1), 2 * D_MODEL:3 * D_MODEL]
    sh2 = mod_ref[pl.ds(r, 1), 3 * D_MODEL:4 * D_MODEL]
    sc2 = mod_ref[pl.ds(r, 1), 4 * D_MODEL:5 * D_MODEL]
    mix = jnp.dot(co_ref[...], wo_ref[0:D_CONV, :], preferred_element_type=F32) \
        + jnp.dot(ys_ref[...], wo_ref[D_CONV:, :], preferred_element_type=F32)
    x1 = _ln_rows(alpha * x + g1 * mix) * g_ref[...] + b_ref[...]
    x1_ref[...] = x1
    packed = _pack_halves(_ln_rows(x1) * (1.0 + sc2) + sh2)
    for c in range(ROW_PARTS):
        h2p_ref[c] = packed[:, c * PART_WORDS:(c + 1) * PART_WORDS]


def _outproj(lay, xp, xs, pos, mod, conv_out, y_ssm, w_out, ln_g, ln_b, alpha, tm):
    T = lay.n_tokens
    row, posb, flag = lay.token_tile_tables(tm)
    npt = lay.n_prompt_tokens // tm

    def const(i, r, p, f):
        return (0, 0)

    def cur(i, r, p, f):
        return (i, 0)

    gs = pltpu.PrefetchScalarGridSpec(
        num_scalar_prefetch=3,
        grid=(T // tm,),
        in_specs=[pl.BlockSpec((tm, D_MODEL), lambda i, r, p, f: (jnp.minimum(i, npt - 1), 0)),
                  pl.BlockSpec((tm, D_MODEL), lambda i, r, p, f: (jnp.maximum(i - npt, 0), 0)),
                  pl.BlockSpec((tm, D_MODEL), lambda i, r, p, f: (p[i], 0)),
                  pl.BlockSpec((8, 6 * D_MODEL), const),
                  pl.BlockSpec((tm, D_CONV), cur),
                  pl.BlockSpec((tm, D_SSM), cur),
                  pl.BlockSpec((D_CONV + D_SSM, D_MODEL), const),
                  pl.BlockSpec((1, D_MODEL), const),
                  pl.BlockSpec((1, D_MODEL), const)],
        out_specs=[pl.BlockSpec((tm, D_MODEL), cur),
                   pl.BlockSpec((ROW_PARTS, tm, PART_WORDS), lambda i, r, p, f: (0, i, 0))])
    return pl.pallas_call(
        functools.partial(_outproj_kernel, alpha=alpha), grid_spec=gs,
        out_shape=(jax.ShapeDtypeStruct((T, D_MODEL), F32),
                   jax.ShapeDtypeStruct((ROW_PARTS, T, PART_WORDS), jnp.uint32)),
        compiler_params=_cparams(("arbitrary",)),
        name="outproj",
    )(row, posb, flag, xp, xs, pos, mod, conv_out, y_ssm, w_out, ln_g.reshape(1, -1), ln_b.reshape(1, -1))


def _route_kernel(h2_ref, wrt_ref, bias_ref, idx_ref, wts_ref, pos_ref, cnt_ref, carry_ref, *, tm):
    i = pl.program_id(0)

    @pl.when(i == 0)
    def _():
        carry_ref[...] = jnp.zeros_like(carry_ref)

    E, NG, EG = N_EXPERTS, N_EXPERT_GROUPS, EXPERTS_PER_GROUP
    neg = -jnp.inf
    logits = lax.dot_general(wrt_ref[...], _unpack_rows(h2_ref), (((1,), (1,)), ((), ())),
                             preferred_element_type=F32)
    s = jax.nn.sigmoid(logits)
    sel = s + bias_ref[...]
    sel3 = sel.reshape(NG, EG, tm)
    io3 = _iota((NG, EG, tm), 1)
    m1 = jnp.max(sel3, axis=1, keepdims=True)
    f1 = jnp.min(jnp.where(sel3 == m1, io3, EG), axis=1, keepdims=True)
    m2 = jnp.max(jnp.where(io3 == f1, neg, sel3), axis=1, keepdims=True)
    gscore = (m1 + m2).reshape(NG, tm)
    gio = _iota((NG, tm), 0)
    beaten = jnp.zeros((NG, tm), I32)
    for g in range(NG):
        row = gscore[g:g + 1, :]
        beats = jnp.where(row > gscore, 1, jnp.where(row == gscore, jnp.where(g < gio, 1, 0), 0))
        beaten = beaten + beats
    keep = (beaten < TOPK_GROUPS).astype(F32).reshape(NG, 1, tm)
    selm = jnp.where(keep > 0.5, sel3, neg).reshape(E, tm)
    eio = _iota((E, tm), 0)
    chosen = jnp.zeros((E, tm), F32)
    idxs, ws = [], []
    for k in range(TOP_K):
        m = jnp.max(selm, axis=0, keepdims=True)
        am = jnp.minimum(jnp.min(jnp.where(selm == m, eio, E), axis=0, keepdims=True), E - 1)
        hit = eio == am
        ws.append(jnp.sum(jnp.where(hit, s, 0.0), axis=0, keepdims=True))
        idxs.append(am)
        selm = jnp.where(hit, neg, selm)
        chosen = jnp.where(hit, 1.0, chosen)
    wsum = ws[0]
    for k in range(1, TOP_K):
        wsum = wsum + ws[k]
    before = (_iota((tm, tm), 0) < _iota((tm, tm), 1)).astype(BF16)
    prior = jnp.dot(chosen.astype(BF16), before, preferred_element_type=F32)
    carry = carry_ref[...]
    prior = prior + jnp.concatenate([carry] * (tm // 128), axis=1)
    for k in range(TOP_K):
        idx_ref[k:k + 1, :] = idxs[k]
        wts_ref[k:k + 1, :] = ws[k] / wsum * ROUTED_SCALE
        pos_ref[k:k + 1, :] = jnp.sum(jnp.where(eio == idxs[k], prior, 0.0), axis=0, keepdims=True).astype(I32)
    total = jnp.dot(chosen.astype(BF16), jnp.ones((tm, 128), BF16), preferred_element_type=F32)
    carry = carry + total
    carry_ref[...] = carry
    cnt_ref[...] = carry.astype(I32)


def _route(h2p, w_router_t, router_bias, tm):
    T = h2p.shape[1]
    bias_b = jnp.broadcast_to(router_bias.astype(F32)[:, None], (N_EXPERTS, tm))
    w_router_t = w_router_t.astype(BF16)
    return pl.pallas_call(
        functools.partial(_route_kernel, tm=tm),
        grid=(T // tm,),
        in_specs=[pl.BlockSpec((ROW_PARTS, tm, PART_WORDS), lambda i: (0, i, 0)),
                  pl.BlockSpec((N_EXPERTS, D_MODEL), lambda i: (0, 0)),
                  pl.BlockSpec((N_EXPERTS, tm), lambda i: (0, 0))],
        out_specs=[pl.BlockSpec((TOP_K, tm), lambda i: (0, i)),
                   pl.BlockSpec((TOP_K, tm), lambda i: (0, i)),
                   pl.BlockSpec((TOP_K, tm), lambda i: (0, i)),
                   pl.BlockSpec((N_EXPERTS, 128), lambda i: (0, 0))],
        out_shape=(jax.ShapeDtypeStruct((TOP_K, T), I32),
                   jax.ShapeDtypeStruct((TOP_K, T), F32),
                   jax.ShapeDtypeStruct((TOP_K, T), I32),
                   jax.ShapeDtypeStruct((N_EXPERTS, 128), I32)),
        scratch_shapes=[pltpu.VMEM((N_EXPERTS, 128), F32)],
        compiler_params=_cparams(("arbitrary",)),
        name="route",
    )(h2p, w_router_t, bias_b)


def _dest_kernel(idx_ref, pos_ref, start_ref, dest_ref):
    tm = idx_ref.shape[1]
    eio = _iota((N_EXPERTS, tm), 0)
    start = start_ref[...]
    for k in range(TOP_K):
        base = jnp.sum(jnp.where(eio == idx_ref[k:k + 1, :], start, 0.0), axis=0, keepdims=True)
        dest_ref[k:k + 1, :] = base.astype(I32) + pos_ref[k:k + 1, :]


def _dest(idx, pos, pad_start, tm):
    T = idx.shape[1]
    start_b = jnp.broadcast_to(pad_start.astype(F32)[:, None], (N_EXPERTS, tm))
    return pl.pallas_call(
        _dest_kernel,
        grid=(T // tm,),
        in_specs=[pl.BlockSpec((TOP_K, tm), lambda i: (0, i)),
                  pl.BlockSpec((TOP_K, tm), lambda i: (0, i)),
                  pl.BlockSpec((N_EXPERTS, tm), lambda i: (0, 0))],
        out_specs=pl.BlockSpec((TOP_K, tm), lambda i: (0, i)),
        out_shape=jax.ShapeDtypeStruct((TOP_K, T), I32),
        compiler_params=_cparams(("arbitrary",)),
        name="dest",
    )(idx, pos, start_b)


_WEIGHT_FETCH_CHUNKS = 4
_W_SLOTS = 8
_X_SLOTS = 6
_Y_SLOTS = 4


def _expert_kernel(base_ref, nblk_ref, nused_ref, x_hbm, wgu_hbm, wd_hbm, y_hbm,
                   wgu_bf, wd_bf, wgu_stage, wd_stage, xbuf, ybuf, next_ref, sems, xsems, ysems):
    e = pl.program_id(0)
    n_used = nused_ref[0]

    def x_copy(b, slot):
        rows = pl.ds(pl.multiple_of(b * EXPERT_BLOCK, EXPERT_BLOCK), EXPERT_BLOCK)
        return pltpu.make_async_copy(x_hbm.at[:, rows, :], xbuf.at[slot], xsems.at[slot])

    def y_copy(b, slot):
        rows = pl.ds(pl.multiple_of(b * EXPERT_BLOCK, EXPERT_BLOCK), EXPERT_BLOCK)
        return pltpu.make_async_copy(ybuf.at[slot], y_hbm.at[:, rows, :], ysems.at[slot])

    def prefetch(b_first, max_starts):
        for _ in range(max_starts):
            j = next_ref[0]

            @pl.when((j <= b_first + _X_SLOTS - 1) & (j < n_used))
            def _():
                x_copy(j, lax.rem(j, _X_SLOTS)).start()
                next_ref[0] = j + 1

    def fetch(ex):
        ws = lax.rem(ex, _W_SLOTS)
        cps = []
        for c in range(_WEIGHT_FETCH_CHUNKS):
            rg = pl.ds(c * (D_MODEL // _WEIGHT_FETCH_CHUNKS), D_MODEL // _WEIGHT_FETCH_CHUNKS)
            rd = pl.ds(c * (D_EXPERT // _WEIGHT_FETCH_CHUNKS), D_EXPERT // _WEIGHT_FETCH_CHUNKS)
            cps.append(pltpu.make_async_copy(wgu_hbm.at[ex, rg], wgu_stage.at[ws, rg], sems.at[ws, 0]))
            cps.append(pltpu.make_async_copy(wd_hbm.at[ex, rd], wd_stage.at[ws, rd], sems.at[ws, 1]))
        return cps

    @pl.when(e == 0)
    def _():
        for ex in range(_W_SLOTS):
            for cp in fetch(ex):
                cp.start()
        next_ref[0] = 0
        prefetch(0, _X_SLOTS - 1)

    for cp in fetch(e):
        cp.wait()
    ws = lax.rem(e, _W_SLOTS)
    wgu_bf[...] = wgu_stage[ws].astype(BF16)
    wd_bf[...] = wd_stage[ws].astype(BF16)

    @pl.when(e + _W_SLOTS < N_EXPERTS)
    def _():
        for cp in fetch(e + _W_SLOTS):
            cp.start()

    def compute(b):
        xs = lax.rem(b, _X_SLOTS)
        gu = jnp.dot(_unpack_rows(xbuf.at[xs]), wgu_bf[...], preferred_element_type=F32)
        act = (_silu(gu[:, 0:D_EXPERT]) * gu[:, D_EXPERT:]).astype(BF16)
        packed = _pack_halves(jnp.dot(act, wd_bf[...], preferred_element_type=F32))
        for c in range(ROW_PARTS):
            ybuf[lax.rem(b, _Y_SLOTS), c] = packed[:, c * PART_WORDS:(c + 1) * PART_WORDS]

    def step(b, width):
        for d in range(width):
            x_copy(b + d, lax.rem(b + d, _X_SLOTS)).wait()
        prefetch(b, width)
        for d in range(width):
            @pl.when(b + d >= _Y_SLOTS)
            def _(d=d):
                y_copy(b + d - _Y_SLOTS, lax.rem(b + d, _Y_SLOTS)).wait()
        for d in range(width):
            compute(b + d)
        for d in range(width):
            y_copy(b + d, lax.rem(b + d, _Y_SLOTS)).start()

    lo = base_ref[e]
    n_pairs = nblk_ref[e] // 2

    def pair(i, carry):
        step(lo + 2 * i, 2)
        return carry

    lax.fori_loop(0, n_pairs, pair, 0)

    @pl.when(nblk_ref[e] % 2 == 1)
    def _():
        step(lo + 2 * n_pairs, 1)

    @pl.when(e == N_EXPERTS - 1)
    def _():
        for d in range(1, _Y_SLOTS + 1):
            @pl.when(n_used >= d)
            def _(d=d):
                y_copy(n_used - d, lax.rem(n_used - d, _Y_SLOTS)).wait()


def _pack_halves(x):
    n = x.shape[1] // 2
    hi = lax.bitcast_convert_type(x[:, :n].astype(BF16).astype(F32), jnp.uint32)
    lo = lax.bitcast_convert_type(x[:, n:].astype(BF16).astype(F32), jnp.uint32)
    return hi | (lo >> 16)


def _unpack_halves(p):
    hi = lax.bitcast_convert_type(p & jnp.uint32(0xFFFF0000), F32)
    lo = lax.bitcast_convert_type(p << 16, F32)
    return hi, lo


def _unpack_rows(ref):
    parts = [_unpack_halves(ref[c]) for c in range(ROW_PARTS)]
    chunks = [p[0] for p in parts] + [p[1] for p in parts]
    return jnp.concatenate([xc.astype(BF16) for xc in chunks], axis=1)


def _expert(x_sorted, w_gu, w_down, blk_base, blk_count, n_used):
    n_rows = x_sorted.shape[1]
    blk_shape = (ROW_PARTS, EXPERT_BLOCK, PART_WORDS)
    gs = pltpu.PrefetchScalarGridSpec(
        num_scalar_prefetch=3,
        grid=(N_EXPERTS,),
        in_specs=[pl.BlockSpec(memory_space=pl.ANY),
                  pl.BlockSpec(memory_space=pl.ANY),
                  pl.BlockSpec(memory_space=pl.ANY)],
        out_specs=pl.BlockSpec(memory_space=pl.ANY),
        scratch_shapes=[pltpu.VMEM((D_MODEL, 2 * D_EXPERT), BF16),
                        pltpu.VMEM((D_EXPERT, D_MODEL), BF16),
                        pltpu.VMEM((_W_SLOTS, D_MODEL, 2 * D_EXPERT), F32),
                        pltpu.VMEM((_W_SLOTS, D_EXPERT, D_MODEL), F32),
                        pltpu.VMEM((_X_SLOTS,) + blk_shape, jnp.uint32),
                        pltpu.VMEM((_Y_SLOTS,) + blk_shape, jnp.uint32),
                        pltpu.SMEM((1,), I32),
                        pltpu.SemaphoreType.DMA((_W_SLOTS, 2)),
                        pltpu.SemaphoreType.DMA((_X_SLOTS,)),
                        pltpu.SemaphoreType.DMA((_Y_SLOTS,))])
    return pl.pallas_call(
        _expert_kernel, grid_spec=gs,
        out_shape=jax.ShapeDtypeStruct((ROW_PARTS, n_rows, PART_WORDS), jnp.uint32),
        compiler_params=_cparams(("arbitrary",)),
        name="expert",
    )(blk_base, blk_count, n_used, x_sorted, w_gu, w_down)


def _combine_kernel(row_ref, h2_ref, x1_ref, wt_ref, mod_ref, wsg_ref, wsd_ref, g_ref, b_ref, yt_ref,
                    o_ref, *, first_tile, alpha):
    i = pl.program_id(0)
    h2 = _unpack_rows(h2_ref)
    su = jnp.dot(h2, wsg_ref[...], preferred_element_type=F32)
    act = (_silu(su[:, 0:D_SHARED]) * su[:, D_SHARED:]).astype(BF16)
    moe = jnp.dot(act, wsd_ref[...], preferred_element_type=F32)
    wt = wt_ref[...]
    his, los = [], []
    for c in range(ROW_PARTS):
        rh = jnp.zeros((h2.shape[0], PART_WORDS), F32)
        rl = jnp.zeros((h2.shape[0], PART_WORDS), F32)
        for k in range(TOP_K):
            hi, lo = _unpack_halves(yt_ref[c, k])
            w = wt[:, k:k + 1]
            rh = rh + hi * w
            rl = rl + lo * w
        his.append(rh)
        los.append(rl)
    moe = moe + jnp.concatenate(his + los, axis=1)
    g2 = mod_ref[pl.ds(row_ref[first_tile + i], 1), 5 * D_MODEL:6 * D_MODEL]
    o_ref[...] = _ln_rows(alpha * x1_ref[...] + g2 * moe) * g_ref[...] + b_ref[...]


def _combine(lay, h2, x1, wts_tok, mod, w_sh_gu, w_sh_down, ln_g, ln_b, y_tok, alpha, tm, first_token, n_tok):
    row, _, _ = lay.token_tile_tables(tm)
    first_tile = first_token // tm

    def cur(i, r):
        return (first_tile + i, 0)

    def const(i, r):
        return (0, 0)

    gs = pltpu.PrefetchScalarGridSpec(
        num_scalar_prefetch=1,
        grid=(n_tok // tm,),
        in_specs=[pl.BlockSpec((ROW_PARTS, tm, PART_WORDS), lambda i, r: (0, first_tile + i, 0)),
                  pl.BlockSpec((tm, D_MODEL), cur),
                  pl.BlockSpec((tm, TOP_K), cur),
                  pl.BlockSpec((8, 6 * D_MODEL), const),
                  pl.BlockSpec((D_MODEL, 2 * D_SHARED), const),
                  pl.BlockSpec((D_SHARED, D_MODEL), const),
                  pl.BlockSpec((1, D_MODEL), const),
                  pl.BlockSpec((1, D_MODEL), const),
                  pl.BlockSpec((ROW_PARTS, TOP_K, tm, PART_WORDS), lambda i, r: (0, 0, i, 0))],
        out_specs=pl.BlockSpec((tm, D_MODEL), lambda i, r: (i, 0)))
    return pl.pallas_call(
        functools.partial(_combine_kernel, first_tile=first_tile, alpha=alpha), grid_spec=gs,
        out_shape=jax.ShapeDtypeStruct((n_tok, D_MODEL), F32),
        compiler_params=_cparams(("arbitrary",)),
        name="combine",
    )(row, h2, x1, wts_tok, mod, w_sh_gu, w_sh_down, ln_g.reshape(1, -1), ln_b.reshape(1, -1), y_tok)


_GATHER_WINDOW = 128


def _sc_gather(table, idx):
    n, d = idx.shape[0], table.shape[1]
    mesh = plsc.VectorSubcoreMesh(core_axis_name="core", subcore_axis_name="subcore")

    @pl.kernel(out_type=jax.ShapeDtypeStruct((n, d), table.dtype), mesh=mesh)
    def gather_kernel(table_hbm, idx_hbm, out_hbm):
        def body(idx_vmem, out_vmem):
            pltpu.sync_copy(table_hbm.at[idx_vmem.at[0]], out_vmem)

        pltpu.emit_pipeline(
            body,
            grid=(n // _GATHER_WINDOW,),
            in_specs=[pl.BlockSpec((1, _GATHER_WINDOW), index_map=lambda i: (0, i))],
            out_specs=[pl.BlockSpec((_GATHER_WINDOW, d), index_map=lambda i: (i, 0))],
            core_axis_name=("core", "subcore"),
            dimension_semantics=(pltpu.PARALLEL,),
        )(idx_hbm, out_hbm)

    return gather_kernel(table, idx.reshape(1, n))


def _sc_scatter(rows, idx, n_out, repeat):
    n, d = rows.shape
    mesh = plsc.VectorSubcoreMesh(core_axis_name="core", subcore_axis_name="subcore")

    @pl.kernel(out_type=jax.ShapeDtypeStruct((n_out, d), rows.dtype), mesh=mesh, scratch_types=[])
    def scatter_kernel(rows_hbm, idx_hbm, out_hbm):
        def body(rows_vmem, idx_vmem):
            for r in range(repeat):
                pltpu.sync_copy(rows_vmem, out_hbm.at[idx_vmem.at[r]])

        pltpu.emit_pipeline(
            body,
            grid=(n // _GATHER_WINDOW,),
            in_specs=[pl.BlockSpec((_GATHER_WINDOW, d), index_map=lambda i: (i, 0)),
                      pl.BlockSpec((repeat, _GATHER_WINDOW), index_map=lambda i: (0, i))],
            out_specs=[],
            core_axis_name=("core", "subcore"),
            dimension_semantics=(pltpu.PARALLEL,),
        )(rows_hbm, idx_hbm)

    return scatter_kernel(rows, idx.reshape(repeat, n))


class _Layout:
    def __init__(self, n_prompt_seqs, prompt_len, n_sample_seqs, sample_len):
        self.n_prompt_seqs, self.prompt_len = n_prompt_seqs, prompt_len
        self.n_sample_seqs, self.sample_len = n_sample_seqs, sample_len
        self.n_prompt_tokens = n_prompt_seqs * prompt_len
        self.n_tokens = self.n_prompt_tokens + n_sample_seqs * sample_len
        assert prompt_len % CONV_TILE == 0 and sample_len % CONV_TILE == 0
        self.max_chunks = max(prompt_len, sample_len) // CHUNK

    def token_tile_tables(self, tm):
        assert self.n_prompt_tokens % tm == 0 and self.sample_len % tm == 0
        npt = self.n_prompt_tokens // tm
        per_seq = self.sample_len // tm
        n = self.n_tokens // tm
        row = np.zeros(n, np.int32)
        posb = np.zeros(n, np.int32)
        flag = np.zeros(n, np.int32)
        for i in range(npt, n):
            j = i - npt
            row[i] = 1 + j // per_seq
            posb[i] = j % per_seq
            flag[i] = 1
        return jnp.asarray(row), jnp.asarray(posb), jnp.asarray(flag)

    def conv_tile_tables(self):
        lok, rok = [], []
        for n_seq, length in ((self.n_prompt_seqs, self.prompt_len), (self.n_sample_seqs, self.sample_len)):
            per = length // CONV_TILE
            for _ in range(n_seq):
                for j in range(per):
                    lok.append(int(j > 0))
                    rok.append(int(j < per - 1))
        return jnp.asarray(np.array(lok, np.int32)), jnp.asarray(np.array(rok, np.int32))

    def ssd_step_tables(self):
        cols = [[] for _ in range(9)]
        seqs = []
        c0 = self.n_prompt_tokens // CHUNK
        for j in range(self.n_sample_seqs):
            nc = self.sample_len // CHUNK
            seqs.append((c0 + j * nc, nc, 0, j, 0))
        for j in range(self.n_prompt_seqs):
            nc = self.prompt_len // CHUNK
            seqs.append((j * nc, nc, 1, 0, j))
        for base, nc, zero, sin, sout in seqs:
            for phase in (0, 1):
                order = range(nc - 1, -1, -1) if phase == 0 else range(nc)
                for n, c in enumerate(order):
                    vals = (base + c, base if phase == 0 else base + c, phase, int(n == 0), int(n == nc - 1),
                            zero, sin, sout, c)
                    for col, v in zip(cols, vals):
                        col.append(v)
        return tuple(jnp.asarray(np.array(col, np.int32)) for col in cols)


def _grid_pos_embed(n_tokens):
    rows = n_tokens // GRID_W
    quarter = D_MODEL // 4
    freq = jnp.exp(-math.log(10000.0) * jnp.arange(quarter, dtype=F32) / quarter)
    r = jnp.broadcast_to(jnp.arange(rows, dtype=F32)[:, None, None] * freq, (rows, GRID_W, quarter))
    cl = jnp.broadcast_to(jnp.arange(GRID_W, dtype=F32)[None, :, None] * freq, (rows, GRID_W, quarter))
    emb = jnp.concatenate([jnp.sin(r), jnp.cos(r), jnp.sin(cl), jnp.cos(cl)], axis=-1)
    return emb.reshape(rows * GRID_W, D_MODEL)


def _moe_plan(counts):
    blk_count = (counts + EXPERT_BLOCK - 1) // EXPERT_BLOCK
    blk_end = jnp.cumsum(blk_count)
    blk_base = blk_end - blk_count
    return ((blk_base * EXPERT_BLOCK).astype(I32), blk_base.astype(I32), blk_count.astype(I32),
            blk_end[-1:].astype(I32))


def _layer(lay, xp, xs, pos, cond8, h0f, h0b, lp, alpha, tm_proj=512, tm_route=256, tm_comb=256):
    (w_ada, b_ada, w_in, conv_w, conv_b, conv_ln_g, conv_ln_b, ssm_conv_w, ssm_conv_b, dt_bias, a_log,
     d_skip, ssm_norm_g, w_out, ln1_g, ln1_b, w_router, router_bias, w_exp_gu, w_exp_down, w_sh_gu,
     w_sh_down, ln2_g, ln2_b) = lp
    T = lay.n_tokens
    n_main = 2 * D_CONV + D_SSM + D_XBC
    w_main = w_in[:, :n_main].astype(BF16)
    w_dt = jnp.pad(w_in[:, n_main:], ((0, 0), (0, 128 - 2 * N_HEADS))).astype(BF16)

    mod = _ada(cond8, w_ada, b_ada)
    u, z, xbc, pre = _inproj(lay, xp, xs, pos, mod, w_main, w_dt, dt_bias, a_log, tm_proj)
    conv_out, xbc_c = _conv(lay, u, xbc, conv_w, conv_b, conv_ln_g, conv_ln_b, ssm_conv_w, ssm_conv_b)
    y_ssm, hf, hb = _ssd(lay, xbc_c, z, pre, h0f, h0b, d_skip, ssm_norm_g)
    x1, h2p = _outproj(lay, xp, xs, pos, mod, conv_out, y_ssm, w_out.astype(BF16), ln1_g, ln1_b, alpha, tm_proj)

    idx, wts, posn, cnt = _route(h2p, w_router.T, router_bias, tm_route)
    n_blocks = -(-T * TOP_K // EXPERT_BLOCK) + N_EXPERTS
    pad_start, blk_base, blk_count, n_used = _moe_plan(cnt[:, 0])
    dest2 = _dest(idx, posn, pad_start, 512)
    n_rows = n_blocks * EXPERT_BLOCK
    scatter_idx = jnp.concatenate([dest2 + c * n_rows for c in range(ROW_PARTS)], axis=1)
    x_sorted = _sc_scatter(h2p.reshape(ROW_PARTS * T, PART_WORDS), scatter_idx.reshape(-1),
                           ROW_PARTS * n_rows, TOP_K).reshape(ROW_PARTS, n_rows, PART_WORDS)
    y_sorted = _expert(x_sorted, w_exp_gu, w_exp_down, blk_base, blk_count, n_used)
    y_flat = y_sorted.reshape(ROW_PARTS * n_rows, PART_WORDS)
    wts_tok, wsg, wsd = wts.T, w_sh_gu.astype(BF16), w_sh_down.astype(BF16)
    outs = []
    for first, n_tok in ((0, lay.n_prompt_tokens), (lay.n_prompt_tokens, T - lay.n_prompt_tokens)):
        dest_g = dest2[:, first:first + n_tok].reshape(-1)
        idx_parts = jnp.concatenate([dest_g + c * n_rows for c in range(ROW_PARTS)])
        y_tok = _sc_gather(y_flat, idx_parts).reshape(ROW_PARTS, TOP_K, n_tok, PART_WORDS)
        outs.append(_combine(lay, h2p, x1, wts_tok, mod, wsg, wsd, ln2_g, ln2_b, y_tok, alpha, tm_comb,
                             first, n_tok))
    return outs[0], outs[1], hf, hb


def kernel(x_prompt, x_sample, state_ssd_fwd, state_ssd_bwd, c, c_ctx, w_ada, b_ada, w_in, conv_w, conv_b, conv_ln_g, conv_ln_b, ssm_conv_w, ssm_conv_b, dt_bias, a_log, d_skip, ssm_norm_g, w_out, ln1_g, ln1_b, w_router, router_bias, w_exp_gu, w_exp_down, w_sh_gu, w_sh_down, ln2_g, ln2_b):
    depth = w_ada.shape[0]
    assert depth == 1, "the prompt and latent passes are fused per layer; one layer is supported"
    bp, lp_, _ = x_prompt.shape
    bd, ld, _ = x_sample.shape
    lay = _Layout(bp, lp_, bd, ld)
    alpha = (2.0 * depth) ** 0.25
    stacked = (w_ada, b_ada, w_in, conv_w, conv_b, conv_ln_g, conv_ln_b, ssm_conv_w, ssm_conv_b,
               dt_bias, a_log, d_skip, ssm_norm_g, w_out, ln1_g, ln1_b, w_router, router_bias,
               w_exp_gu, w_exp_down, w_sh_gu, w_sh_down, ln2_g, ln2_b)
    lp = [w[0] for w in stacked]
    cond8 = jnp.concatenate([c_ctx[None, :], c, jnp.zeros((8 - 1 - bd, D_MODEL), F32)], axis=0)
    pos = _grid_pos_embed(ld)
    sshape = (bd, N_HEADS, HEADDIM, D_STATE)
    out_p, out_s, hf, hb = _layer(lay, x_prompt.reshape(bp * lp_, D_MODEL), x_sample.reshape(bd * ld, D_MODEL),
                                  pos, cond8, state_ssd_fwd[:, 0].reshape(sshape),
                                  state_ssd_bwd[:, 0].reshape(sshape), lp, alpha)
    return (out_p.reshape(bp, lp_, D_MODEL), out_s.reshape(bd, ld, D_MODEL),
            hf[:, None], hb[:, None])
```

```python
import functools
import math

import numpy as np
import jax
import jax.numpy as jnp
from jax import lax
from jax.experimental import pallas as pl
from jax.experimental.pallas import tpu as pltpu
from jax.experimental.pallas import tpu_sc as plsc

F32 = jnp.float32
BF16 = jnp.bfloat16
I32 = jnp.int32
HI = lax.Precision.HIGHEST

D_MODEL = 1024
GRID_W = 64
D_CONV = 1024
CONV_K = 31
N_HEADS = 16
HEADDIM = 64
D_SSM = N_HEADS * HEADDIM
N_GROUPS = 4
HEADS_PER_GROUP = N_HEADS // N_GROUPS
D_STATE = 128
SSM_CONV_K = 4
CHUNK = 128
D_XBC = D_SSM + 2 * N_GROUPS * D_STATE
N_EXPERTS = 256
TOP_K = 8
N_EXPERT_GROUPS = 8
EXPERTS_PER_GROUP = N_EXPERTS // N_EXPERT_GROUPS
TOPK_GROUPS = 4
D_EXPERT = 256
D_SHARED = 256
ROUTED_SCALE = 2.5
LN_EPS = 1e-5

CONV_TILE = 256
HALO = 16
EXPERT_BLOCK = 256
ROW_PARTS = 2
PART_WORDS = D_MODEL // 2 // ROW_PARTS
VMEM_LIMIT = 56 * 1024 * 1024


def _cparams(sem, vmem=VMEM_LIMIT):
    return pltpu.CompilerParams(dimension_semantics=sem, vmem_limit_bytes=vmem)


def _silu(x):
    return x * jax.nn.sigmoid(x)


def _ln_rows(x):
    mu = jnp.mean(x, axis=-1, keepdims=True)
    xc = x - mu
    var = jnp.mean(xc * xc, axis=-1, keepdims=True)
    return xc * lax.rsqrt(var + LN_EPS)


def _iota(shape, dim):
    return lax.broadcasted_iota(I32, shape, dim)


def _expand_matrix(n_in, width):
    return (_iota((n_in, n_in * width), 0) == _iota((n_in, n_in * width), 1) // width).astype(F32)


def _dot_hi(a, b):
    return jnp.dot(a, b, precision=HI, preferred_element_type=F32)


def _split3(x):
    hi = x.astype(BF16)
    r1 = x - hi.astype(F32)
    mid = r1.astype(BF16)
    lo = (r1 - mid.astype(F32)).astype(BF16)
    return jnp.concatenate([hi, mid, lo], axis=1)


def _expand3(n, width):
    rows = np.arange(3 * n)[:, None] % n
    cols = np.arange(n * width)[None, :] // width
    return jnp.asarray(rows == cols, dtype=BF16)


def _expand_exact(x, e3):
    return jnp.dot(_split3(x), e3, preferred_element_type=F32)


def _ada_kernel(c_ref, w_ref, b_ref, o_ref):
    o_ref[...] = _dot_hi(_silu(c_ref[...]), w_ref[...]) + b_ref[...]


def _ada(cond8, w_ada, b_ada):
    n = w_ada.shape[1]
    tn = 1024
    return pl.pallas_call(
        _ada_kernel,
        grid=(n // tn,),
        in_specs=[pl.BlockSpec((8, D_MODEL), lambda j: (0, 0)),
                  pl.BlockSpec((D_MODEL, tn), lambda j: (0, j)),
                  pl.BlockSpec((1, tn), lambda j: (0, j))],
        out_specs=pl.BlockSpec((8, tn), lambda j: (0, j)),
        out_shape=jax.ShapeDtypeStruct((8, n), F32),
        compiler_params=_cparams(("arbitrary",)),
        name="ada",
    )(cond8, w_ada, b_ada.reshape(1, n))


def _inproj_kernel(row_ref, posb_ref, flag_ref, xp_ref, xs_ref, pos_ref, mod_ref, wm_ref, wdt_ref,
                   dtb_ref, alog_ref, tri_ref, u_ref, z_ref, xbc_ref, pre_ref):
    i = pl.program_id(0)
    x = jnp.where(flag_ref[i] == 1, xs_ref[...] + pos_ref[...], xp_ref[...])
    r = row_ref[i]
    sh1 = mod_ref[pl.ds(r, 1), 0:D_MODEL]
    sc1 = mod_ref[pl.ds(r, 1), D_MODEL:2 * D_MODEL]
    h = (_ln_rows(x) * (1.0 + sc1) + sh1).astype(BF16)
    nh2 = 2 * N_HEADS
    dt = jnp.dot(h, wdt_ref[...], preferred_element_type=F32)[:, 0:nh2] + dtb_ref[...]
    dt = jnp.maximum(dt, 0.0) + jnp.log1p(jnp.exp(-jnp.abs(dt)))
    a = dt * (-jnp.exp(alog_ref[...]))
    a_split = _split3(a)
    glu_a = jnp.dot(h, wm_ref[:, 0:D_CONV], preferred_element_type=F32)
    glu_g = jnp.dot(h, wm_ref[:, D_CONV:2 * D_CONV], preferred_element_type=F32)
    u_ref[...] = (glu_a * jax.nn.sigmoid(glu_g)).astype(BF16)
    a3 = jnp.dot(tri_ref[...], a_split, preferred_element_type=F32)
    z = jnp.dot(h, wm_ref[:, 2 * D_CONV:2 * D_CONV + D_SSM], preferred_element_type=F32)
    z_ref[...] = _silu(z).astype(BF16)
    xbc_ref[...] = jnp.dot(h, wm_ref[:, 2 * D_CONV + D_SSM:], preferred_element_type=F32).astype(BF16)
    acs = a3[:, 0:nh2] + a3[:, nh2:2 * nh2] + a3[:, 2 * nh2:3 * nh2]
    pre_ref[...] = jnp.concatenate([dt, acs, a, jnp.zeros((dt.shape[0], 128 - 3 * nh2), F32)], axis=1)


def _inproj(lay, xp, xs, pos, mod, w_main, w_dt, dt_bias, a_log, tm):
    T = lay.n_tokens
    row, posb, flag = lay.token_tile_tables(tm)
    npt = lay.n_prompt_tokens // tm
    n_main = w_main.shape[1]
    chunk_of = np.arange(tm) // CHUNK
    tri = jnp.asarray((chunk_of[:, None] == chunk_of[None, :]) & np.tril(np.ones((tm, tm), bool)), dtype=BF16)
    gs = pltpu.PrefetchScalarGridSpec(
        num_scalar_prefetch=3,
        grid=(T // tm,),
        in_specs=[pl.BlockSpec((tm, D_MODEL), lambda i, r, p, f: (jnp.minimum(i, npt - 1), 0)),
                  pl.BlockSpec((tm, D_MODEL), lambda i, r, p, f: (jnp.maximum(i - npt, 0), 0)),
                  pl.BlockSpec((tm, D_MODEL), lambda i, r, p, f: (p[i], 0)),
                  pl.BlockSpec((8, 6 * D_MODEL), lambda i, r, p, f: (0, 0)),
                  pl.BlockSpec((D_MODEL, n_main), lambda i, r, p, f: (0, 0)),
                  pl.BlockSpec((D_MODEL, 128), lambda i, r, p, f: (0, 0)),
                  pl.BlockSpec((1, 2 * N_HEADS), lambda i, r, p, f: (0, 0)),
                  pl.BlockSpec((1, 2 * N_HEADS), lambda i, r, p, f: (0, 0)),
                  pl.BlockSpec((tm, tm), lambda i, r, p, f: (0, 0))],
        out_specs=[pl.BlockSpec((tm, D_CONV), lambda i, r, p, f: (i, 0)),
                   pl.BlockSpec((tm, D_SSM), lambda i, r, p, f: (i, 0)),
                   pl.BlockSpec((tm, D_XBC), lambda i, r, p, f: (i, 0)),
                   pl.BlockSpec((tm, 128), lambda i, r, p, f: (i, 0))])
    return pl.pallas_call(
        _inproj_kernel, grid_spec=gs,
        out_shape=(jax.ShapeDtypeStruct((T, D_CONV), BF16),
                   jax.ShapeDtypeStruct((T, D_SSM), BF16),
                   jax.ShapeDtypeStruct((T, D_XBC), BF16),
                   jax.ShapeDtypeStruct((T, 128), F32)),
        compiler_params=_cparams(("arbitrary",)),
        name="inproj",
    )(row, posb, flag, xp, xs, pos, mod, w_main, w_dt, dt_bias.reshape(1, -1), a_log.reshape(1, -1), tri)


_N_SHIFT = 8
_SHIFT_ROWS = CONV_TILE + 2 * HALO - _N_SHIFT
_ROW_BLOCK = 64
_FILL_ROWS = 32
_SSM_ROWS, _SSM_LANES = 64, 256


def _conv_kernel(lok_ref, rok_ref, u_ref, ul_ref, ur_ref, xbc_ref, xbcl_ref, xbcr_ref,
                 cw_ref, cb_ref, lng_ref, lnb_ref, sw_ref, sb_ref, co_ref, xo_ref,
                 ext_ref, sh_ref, acc_ref, ext2_ref):
    i = pl.program_id(0)
    lok = lok_ref[i] == 1
    rok = rok_ref[i] == 1

    def fill_ext(rb, carry):
        r0 = pl.multiple_of(rb * _FILL_ROWS, _FILL_ROWS)
        dst = pl.ds(pl.multiple_of(HALO + r0, HALO), _FILL_ROWS)
        ext_ref[dst, :] = u_ref[pl.ds(r0, _FILL_ROWS), :].astype(F32)
        ext2_ref[dst, :] = xbc_ref[pl.ds(r0, _FILL_ROWS), :].astype(F32)
        return carry

    ext_ref[0:HALO, :] = jnp.where(lok, ul_ref[...].astype(F32), 0.0)
    ext_ref[HALO + CONV_TILE:, :] = jnp.where(rok, ur_ref[...].astype(F32), 0.0)
    ext2_ref[0:HALO, :] = jnp.where(lok, xbcl_ref[...].astype(F32), 0.0)
    ext2_ref[HALO + CONV_TILE:, :] = jnp.where(rok, xbcr_ref[...].astype(F32), 0.0)
    lax.fori_loop(0, CONV_TILE // _FILL_ROWS, fill_ext, 0)
    for r in range(_N_SHIFT):
        sh_ref[r] = ext_ref[r:r + _SHIFT_ROWS, :]

    first = HALO - (CONV_K - 1) // 2

    for j in range(D_CONV // 128):
        lanes = slice(j * 128, (j + 1) * 128)
        taps = [jnp.broadcast_to(cw_ref[k:k + 1, lanes], (8, 128)) for k in range(CONV_K)]
        bias = jnp.broadcast_to(cb_ref[:, lanes], (8, 128))

        def row_block(rb, carry, lanes=lanes, taps=taps, bias=bias):
            base = pl.multiple_of(rb * _ROW_BLOCK, _ROW_BLOCK)
            for sub in range(_ROW_BLOCK // 8):
                acc = bias
                for k in range(CONV_K):
                    o = first + k
                    row0 = base + (o // _N_SHIFT) * _N_SHIFT + sub * 8
                    acc = acc + sh_ref[o % _N_SHIFT, pl.ds(row0, 8), lanes] * taps[k]
                acc_ref[pl.ds(base + sub * 8, 8), lanes] = acc
            return carry

        lax.fori_loop(0, CONV_TILE // _ROW_BLOCK, row_block, 0)
    u = _ln_rows(acc_ref[...]) * lng_ref[...] + lnb_ref[...]
    co_ref[...] = _silu(u).astype(BF16)

    first2 = HALO - (SSM_CONV_K - 1) // 2
    for rb in range(CONV_TILE // _SSM_ROWS):
        for lc in range(D_XBC // _SSM_LANES):
            lanes = slice(lc * _SSM_LANES, (lc + 1) * _SSM_LANES)
            y = jnp.zeros((_SSM_ROWS, _SSM_LANES), F32) + sb_ref[:, lanes]
            for k in range(SSM_CONV_K):
                r0 = first2 + k + rb * _SSM_ROWS
                y = y + ext2_ref[r0:r0 + _SSM_ROWS, lanes] * sw_ref[k:k + 1, lanes]
            xo_ref[rb * _SSM_ROWS:(rb + 1) * _SSM_ROWS, lanes] = _silu(y).astype(BF16)


def _conv(lay, u, xbc, conv_w, conv_b, ln_g, ln_b, ssm_w, ssm_b):
    T = lay.n_tokens
    lok, rok = lay.conv_tile_tables()
    n_tiles = T // CONV_TILE
    hb = CONV_TILE // HALO
    n_hb = T // HALO

    def cur(i, l, r):
        return (i, 0)

    def left(i, l, r):
        return (jnp.maximum(i * hb - 1, 0), 0)

    def right(i, l, r):
        return (jnp.minimum((i + 1) * hb, n_hb - 1), 0)

    def const(i, l, r):
        return (0, 0)

    gs = pltpu.PrefetchScalarGridSpec(
        num_scalar_prefetch=2,
        grid=(n_tiles,),
        in_specs=[pl.BlockSpec((CONV_TILE, D_CONV), cur),
                  pl.BlockSpec((HALO, D_CONV), left),
                  pl.BlockSpec((HALO, D_CONV), right),
                  pl.BlockSpec((CONV_TILE, D_XBC), cur),
                  pl.BlockSpec((HALO, D_XBC), left),
                  pl.BlockSpec((HALO, D_XBC), right),
                  pl.BlockSpec((CONV_K, D_CONV), const),
                  pl.BlockSpec((1, D_CONV), const),
                  pl.BlockSpec((1, D_CONV), const),
                  pl.BlockSpec((1, D_CONV), const),
                  pl.BlockSpec((SSM_CONV_K, D_XBC), const),
                  pl.BlockSpec((1, D_XBC), const)],
        out_specs=[pl.BlockSpec((CONV_TILE, D_CONV), cur),
                   pl.BlockSpec((CONV_TILE, D_XBC), cur)],
        scratch_shapes=[pltpu.VMEM((CONV_TILE + 2 * HALO, D_CONV), F32),
                        pltpu.VMEM((_N_SHIFT, _SHIFT_ROWS, D_CONV), F32),
                        pltpu.VMEM((CONV_TILE, D_CONV), F32),
                        pltpu.VMEM((CONV_TILE + 2 * HALO, D_XBC), F32)])
    return pl.pallas_call(
        _conv_kernel, grid_spec=gs,
        out_shape=(jax.ShapeDtypeStruct((T, D_CONV), BF16),
                   jax.ShapeDtypeStruct((T, D_XBC), BF16)),
        compiler_params=_cparams(("arbitrary",)),
        name="conv",
    )(lok, rok, u, u, u, xbc, xbc, xbc, conv_w, conv_b.reshape(1, -1), ln_g.reshape(1, -1),
      ln_b.reshape(1, -1), ssm_w, ssm_b.reshape(1, -1))


_BN = N_GROUPS * D_STATE


def _ssd_kernel(chunk_ref, yidx_ref, phase_ref, first_ref, last_ref, zero_ref, sin_ref, sout_ref, cloc_ref,
                xbc_ref, z_ref, pre_ref, h0f_ref, h0b_ref, dsk_ref, ng_ref,
                edec_ref, ewb_ref, ecol_ref, ewide_ref, eye3_ref,
                y_ref, hf_out_ref, hb_out_ref,
                hf_ref, g_ref, gin_ref, ybuf_ref):
    s = pl.program_id(0)
    phase = phase_ref[s]
    first = first_ref[s] == 1
    last = last_ref[s] == 1
    zero = zero_ref[s] == 1
    cloc = cloc_ref[s]
    H, P, N = N_HEADS, HEADDIM, D_STATE

    GW = HEADS_PER_GROUP * P
    xs = xbc_ref[:, 0:D_SSM]
    dt = pre_ref[:, 0:2 * H]
    acs = pre_ref[:, 2 * H:4 * H]
    a = pre_ref[:, 4 * H:6 * H]
    tot = acs[CHUNK - 8:CHUNK, :]
    dec = _expand_exact(jnp.exp(tot), edec_ref[...])[7:8, :]
    exb = acs[:, H:2 * H] - a[:, H:2 * H]

    def load_state(src_ref, dst_ref):
        for j in range(H // 2):
            pair = jnp.concatenate([src_ref[0, 2 * j], src_ref[0, 2 * j + 1]], axis=0)
            dst_ref[:, 2 * j * P:(2 * j + 2) * P] = jnp.where(zero, 0.0, pair.T)

    def store_state(src_ref, dst_ref):
        for j in range(H // 2):
            pair = src_ref[:, 2 * j * P:(2 * j + 2) * P].T
            dst_ref[0, 2 * j] = pair[0:P]
            dst_ref[0, 2 * j + 1] = pair[P:2 * P]

    @pl.when(phase == 0)
    def _backward_states():
        @pl.when(first)
        def _():
            load_state(h0b_ref, g_ref)

        wb = dt[:, H:2 * H] * jnp.exp(exb)
        xw = (xs.astype(F32) * _expand_exact(wb, ewb_ref[...])).astype(BF16)
        for g in range(N_GROUPS):
            cols = slice(g * GW, (g + 1) * GW)
            bg = xbc_ref[:, D_SSM + g * N:D_SSM + (g + 1) * N]
            gg = g_ref[:, cols]
            gin_ref[cloc, :, cols] = gg.astype(BF16)
            upd = lax.dot_general(bg, xw[:, cols], (((0,), (0,)), ((), ())), preferred_element_type=F32)
            g_ref[:, cols] = gg * dec[:, D_SSM + g * GW:D_SSM + (g + 1) * GW] + upd

        @pl.when(last)
        def _():
            store_state(g_ref, hb_out_ref)

    @pl.when(phase == 1)
    def _forward_and_outputs():
        @pl.when(first)
        def _():
            load_state(h0f_ref, hf_ref)

        acsf = acs[:, 0:H]
        dtf = dt[:, 0:H]
        dtb = dt[:, H:2 * H]
        totf = acs[CHUNK - 1:CHUNK, 0:H]
        totb = acs[CHUNK - 1:CHUNK, H:2 * H]
        col = _expand_exact(jnp.concatenate([acsf, exb], axis=1), ecol_ref[...])
        q3 = _split3(jnp.concatenate([acsf, exb, dtf, dtb], axis=1))
        qt = lax.dot_general(eye3_ref[...], q3, (((1,), (1,)), ((), ())),
                             preferred_element_type=F32)
        wide = jnp.concatenate([dtf * jnp.exp(totf - acsf), jnp.exp(acsf), jnp.exp(totb - exb)], axis=1)
        wide = _expand_exact(wide, ewide_ref[...])
        xsf = xs.astype(F32)
        xw = (xsf * wide[:, 0:D_SSM]).astype(BF16)
        lower = _iota((CHUNK, CHUNK), 1) <= _iota((CHUNK, CHUNK), 0)
        upper = _iota((CHUNK, CHUNK), 1) >= _iota((CHUNK, CHUNK), 0)
        for g in range(N_GROUPS):
            cols = slice(g * GW, (g + 1) * GW)
            bg = xbc_ref[:, D_SSM + g * N:D_SSM + (g + 1) * N]
            cg = xbc_ref[:, D_SSM + _BN + g * N:D_SSM + _BN + (g + 1) * N]
            cb = lax.dot_general(cg, bg, (((1,), (1,)), ((), ())), preferred_element_type=F32)
            hfg = hf_ref[:, cols]
            yf = jnp.dot(cg, hfg.astype(BF16), preferred_element_type=F32)
            yb = jnp.dot(cg, gin_ref[cloc, :, cols], preferred_element_type=F32)
            yg = yf * wide[:, D_SSM + g * GW:D_SSM + (g + 1) * GW] \
                + yb * wide[:, 2 * D_SSM + g * GW:2 * D_SSM + (g + 1) * GW]
            upd = lax.dot_general(bg, xw[:, cols], (((0,), (0,)), ((), ())), preferred_element_type=F32)
            hf_ref[:, cols] = hfg * dec[:, cols] + upd
            xg = xs[:, cols]
            head_of_lane = _iota((CHUNK, GW), 1) // P
            for r in range(HEADS_PER_GROUP):
                h = g * HEADS_PER_GROUP + r
                colf = col[:, h * N:(h + 1) * N]
                colb = col[:, (H + h) * N:(H + h + 1) * N]
                mf = jnp.where(lower, jnp.exp(colf - qt[h:h + 1, :]), 0.0) * qt[2 * H + h:2 * H + h + 1, :]
                mb = jnp.where(upper, jnp.exp(qt[H + h:H + h + 1, :] - colb), 0.0) * qt[3 * H + h:3 * H + h + 1, :]
                m = (cb * (mf + mb)).astype(BF16)
                xh = jnp.where(head_of_lane == r, xg, jnp.zeros_like(xg))
                yg = yg + jnp.dot(m, xh, preferred_element_type=F32)
            ybuf_ref[:, cols] = yg

        yt = (ybuf_ref[...] + dsk_ref[...] * xsf) * z_ref[...].astype(F32)
        gw = D_SSM // N_GROUPS
        for g in range(N_GROUPS):
            seg = yt[:, g * gw:(g + 1) * gw]
            ms = jnp.mean(seg * seg, axis=-1, keepdims=True)
            y_ref[:, g * gw:(g + 1) * gw] = (seg * lax.rsqrt(ms + LN_EPS) * ng_ref[:, g * gw:(g + 1) * gw]).astype(BF16)

        @pl.when(last)
        def _():
            store_state(hf_ref, hf_out_ref)


def _ssd(lay, xbc_c, z, pre, h0f, h0b, d_skip, norm_g):
    T = lay.n_tokens
    tabs = lay.ssd_step_tables()
    n_steps = tabs[0].shape[0]
    nsp = len(tabs)

    def by_chunk(s, *t):
        return (t[0][s], 0)

    def by_y(s, *t):
        return (t[1][s], 0)

    def by_sin(s, *t):
        return (t[6][s], 0, 0, 0)

    def by_sout(s, *t):
        return (t[7][s], 0, 0, 0)

    def const(s, *t):
        return (0, 0)

    H = N_HEADS
    eye3 = jnp.asarray(np.arange(4 * H)[:, None] == np.arange(12 * H)[None, :] % (4 * H), dtype=BF16)
    consts = [_expand3(2 * H, HEADDIM), _expand3(H, HEADDIM), _expand3(2 * H, D_STATE),
              _expand3(3 * H, HEADDIM), eye3]
    sshape = (1, N_HEADS, HEADDIM, D_STATE)
    gs = pltpu.PrefetchScalarGridSpec(
        num_scalar_prefetch=nsp,
        grid=(n_steps,),
        in_specs=[pl.BlockSpec((CHUNK, D_XBC), by_chunk),
                  pl.BlockSpec((CHUNK, D_SSM), by_chunk),
                  pl.BlockSpec((CHUNK, 128), by_chunk),
                  pl.BlockSpec(sshape, by_sin),
                  pl.BlockSpec(sshape, by_sin),
                  pl.BlockSpec((1, D_SSM), const),
                  pl.BlockSpec((1, D_SSM), const)] + [pl.BlockSpec(c.shape, const) for c in consts],
        out_specs=[pl.BlockSpec((CHUNK, D_SSM), by_y),
                   pl.BlockSpec(sshape, by_sout),
                   pl.BlockSpec(sshape, by_sout)],
        scratch_shapes=[pltpu.VMEM((D_STATE, D_SSM), F32),
                        pltpu.VMEM((D_STATE, D_SSM), F32),
                        pltpu.VMEM((lay.max_chunks, D_STATE, D_SSM), BF16),
                        pltpu.VMEM((CHUNK, D_SSM), F32)])
    n_out = lay.n_prompt_seqs
    return pl.pallas_call(
        _ssd_kernel, grid_spec=gs,
        out_shape=(jax.ShapeDtypeStruct((T, D_SSM), BF16),
                   jax.ShapeDtypeStruct((n_out,) + sshape[1:], F32),
                   jax.ShapeDtypeStruct((n_out,) + sshape[1:], F32)),
        compiler_params=_cparams(("arbitrary",)),
        name="ssd",
    )(*tabs, xbc_c, z, pre, h0f, h0b,
      jnp.repeat(d_skip, HEADDIM).reshape(1, -1), norm_g.reshape(1, -1), *consts)


def _outproj_kernel(row_ref, posb_ref, flag_ref, xp_ref, xs_ref, pos_ref, mod_ref, co_ref, ys_ref, wo_ref,
                    g_ref, b_ref, x1_ref, h2p_ref, *, alpha):
    i = pl.program_id(0)
    x = jnp.where(flag_ref[i] == 1, xs_ref[...] + pos_ref[...], xp_ref[...])
    r = row_ref[i]
    g1 = mod_ref[pl.ds(r, 1), 2 * D_MODEL:3 * D_MODEL]
    sh2 = mod_ref[pl.ds(r, 1), 3 * D_MODEL:4 * D_MODEL]
    sc2 = mod_ref[pl.ds(r, 1), 4 * D_MODEL:5 * D_MODEL]
    mix = jnp.dot(co_ref[...], wo_ref[0:D_CONV, :], preferred_element_type=F32) \
        + jnp.dot(ys_ref[...], wo_ref[D_CONV:, :], preferred_element_type=F32)
    x1 = _ln_rows(alpha * x + g1 * mix) * g_ref[...] + b_ref[...]
    x1_ref[...] = x1
    packed = _pack_halves(_ln_rows(x1) * (1.0 + sc2) + sh2)
    for c in range(ROW_PARTS):
        h2p_ref[c] = packed[:, c * PART_WORDS:(c + 1) * PART_WORDS]


def _outproj(lay, xp, xs, pos, mod, conv_out, y_ssm, w_out, ln_g, ln_b, alpha, tm):
    T = lay.n_tokens
    row, posb, flag = lay.token_tile_tables(tm)
    npt = lay.n_prompt_tokens // tm

    def const(i, r, p, f):
        return (0, 0)

    def cur(i, r, p, f):
        return (i, 0)

    gs = pltpu.PrefetchScalarGridSpec(
        num_scalar_prefetch=3,
        grid=(T // tm,),
        in_specs=[pl.BlockSpec((tm, D_MODEL), lambda i, r, p, f: (jnp.minimum(i, npt - 1), 0)),
                  pl.BlockSpec((tm, D_MODEL), lambda i, r, p, f: (jnp.maximum(i - npt, 0), 0)),
                  pl.BlockSpec((tm, D_MODEL), lambda i, r, p, f: (p[i], 0)),
                  pl.BlockSpec((8, 6 * D_MODEL), const),
                  pl.BlockSpec((tm, D_CONV), cur),
                  pl.BlockSpec((tm, D_SSM), cur),
                  pl.BlockSpec((D_CONV + D_SSM, D_MODEL), const),
                  pl.BlockSpec((1, D_MODEL), const),
                  pl.BlockSpec((1, D_MODEL), const)],
        out_specs=[pl.BlockSpec((tm, D_MODEL), cur),
                   pl.BlockSpec((ROW_PARTS, tm, PART_WORDS), lambda i, r, p, f: (0, i, 0))])
    return pl.pallas_call(
        functools.partial(_outproj_kernel, alpha=alpha), grid_spec=gs,
        out_shape=(jax.ShapeDtypeStruct((T, D_MODEL), F32),
                   jax.ShapeDtypeStruct((ROW_PARTS, T, PART_WORDS), jnp.uint32)),
        compiler_params=_cparams(("arbitrary",)),
        name="outproj",
    )(row, posb, flag, xp, xs, pos, mod, conv_out, y_ssm, w_out, ln_g.reshape(1, -1), ln_b.reshape(1, -1))


def _route_kernel(h2_ref, wrt_ref, bias_ref, idx_ref, wts_ref, pos_ref, cnt_ref, carry_ref, *, tm):
    i = pl.program_id(0)

    @pl.when(i == 0)
    def _():
        carry_ref[...] = jnp.zeros_like(carry_ref)

    E, NG, EG = N_EXPERTS, N_EXPERT_GROUPS, EXPERTS_PER_GROUP
    neg = -jnp.inf
    logits = lax.dot_general(wrt_ref[...], _unpack_rows(h2_ref), (((1,), (1,)), ((), ())),
                             preferred_element_type=F32)
    s = jax.nn.sigmoid(logits)
    sel = s + bias_ref[...]
    sel3 = sel.reshape(NG, EG, tm)
    io3 = _iota((NG, EG, tm), 1)
    m1 = jnp.max(sel3, axis=1, keepdims=True)
    f1 = jnp.min(jnp.where(sel3 == m1, io3, EG), axis=1, keepdims=True)
    m2 = jnp.max(jnp.where(io3 == f1, neg, sel3), axis=1, keepdims=True)
    gscore = (m1 + m2).reshape(NG, tm)
    gio = _iota((NG, tm), 0)
    beaten = jnp.zeros((NG, tm), I32)
    for g in range(NG):
        row = gscore[g:g + 1, :]
        beats = jnp.where(row > gscore, 1, jnp.where(row == gscore, jnp.where(g < gio, 1, 0), 0))
        beaten = beaten + beats
    keep = (beaten < TOPK_GROUPS).astype(F32).reshape(NG, 1, tm)
    selm = jnp.where(keep > 0.5, sel3, neg).reshape(E, tm)
    eio = _iota((E, tm), 0)
    chosen = jnp.zeros((E, tm), F32)
    idxs, ws = [], []
    for k in range(TOP_K):
        m = jnp.max(selm, axis=0, keepdims=True)
        am = jnp.minimum(jnp.min(jnp.where(selm == m, eio, E), axis=0, keepdims=True), E - 1)
        hit = eio == am
        ws.append(jnp.sum(jnp.where(hit, s, 0.0), axis=0, keepdims=True))
        idxs.append(am)
        selm = jnp.where(hit, neg, selm)
        chosen = jnp.where(hit, 1.0, chosen)
    wsum = ws[0]
    for k in range(1, TOP_K):
        wsum = wsum + ws[k]
    before = (_iota((tm, tm), 0) < _iota((tm, tm), 1)).astype(BF16)
    prior = jnp.dot(chosen.astype(BF16), before, preferred_element_type=F32)
    carry = carry_ref[...]
    prior = prior + jnp.concatenate([carry] * (tm // 128), axis=1)
    for k in range(TOP_K):
        idx_ref[k:k + 1, :] = idxs[k]
        wts_ref[k:k + 1, :] = ws[k] / wsum * ROUTED_SCALE
        pos_ref[k:k + 1, :] = jnp.sum(jnp.where(eio == idxs[k], prior, 0.0), axis=0, keepdims=True).astype(I32)
    total = jnp.dot(chosen.astype(BF16), jnp.ones((tm, 128), BF16), preferred_element_type=F32)
    carry = carry + total
    carry_ref[...] = carry
    cnt_ref[...] = carry.astype(I32)


def _route(h2p, w_router_t, router_bias, tm):
    T = h2p.shape[1]
    bias_b = jnp.broadcast_to(router_bias.astype(F32)[:, None], (N_EXPERTS, tm))
    w_router_t = w_router_t.astype(BF16)
    return pl.pallas_call(
        functools.partial(_route_kernel, tm=tm),
        grid=(T // tm,),
        in_specs=[pl.BlockSpec((ROW_PARTS, tm, PART_WORDS), lambda i: (0, i, 0)),
                  pl.BlockSpec((N_EXPERTS, D_MODEL), lambda i: (0, 0)),
                  pl.BlockSpec((N_EXPERTS, tm), lambda i: (0, 0))],
        out_specs=[pl.BlockSpec((TOP_K, tm), lambda i: (0, i)),
                   pl.BlockSpec((TOP_K, tm), lambda i: (0, i)),
                   pl.BlockSpec((TOP_K, tm), lambda i: (0, i)),
                   pl.BlockSpec((N_EXPERTS, 128), lambda i: (0, 0))],
        out_shape=(jax.ShapeDtypeStruct((TOP_K, T), I32),
                   jax.ShapeDtypeStruct((TOP_K, T), F32),
                   jax.ShapeDtypeStruct((TOP_K, T), I32),
                   jax.ShapeDtypeStruct((N_EXPERTS, 128), I32)),
        scratch_shapes=[pltpu.VMEM((N_EXPERTS, 128), F32)],
        compiler_params=_cparams(("arbitrary",)),
        name="route",
    )(h2p, w_router_t, bias_b)


def _dest_kernel(idx_ref, pos_ref, start_ref, dest_ref):
    tm = idx_ref.shape[1]
    eio = _iota((N_EXPERTS, tm), 0)
    start = start_ref[...]
    for k in range(TOP_K):
        base = jnp.sum(jnp.where(eio == idx_ref[k:k + 1, :], start, 0.0), axis=0, keepdims=True)
        dest_ref[k:k + 1, :] = base.astype(I32) + pos_ref[k:k + 1, :]


def _dest(idx, pos, pad_start, tm):
    T = idx.shape[1]
    start_b = jnp.broadcast_to(pad_start.astype(F32)[:, None], (N_EXPERTS, tm))
    return pl.pallas_call(
        _dest_kernel,
        grid=(T // tm,),
        in_specs=[pl.BlockSpec((TOP_K, tm), lambda i: (0, i)),
                  pl.BlockSpec((TOP_K, tm), lambda i: (0, i)),
                  pl.BlockSpec((N_EXPERTS, tm), lambda i: (0, 0))],
        out_specs=pl.BlockSpec((TOP_K, tm), lambda i: (0, i)),
        out_shape=jax.ShapeDtypeStruct((TOP_K, T), I32),
        compiler_params=_cparams(("arbitrary",)),
        name="dest",
    )(idx, pos, start_b)


_WEIGHT_FETCH_CHUNKS = 4
_W_SLOTS = 8
_X_SLOTS = 6
_Y_SLOTS = 4


def _expert_kernel(base_ref, nblk_ref, nused_ref, x_hbm, wgu_hbm, wd_hbm, y_hbm,
                   wgu_bf, wd_bf, wgu_stage, wd_stage, xbuf, ybuf, next_ref, sems, xsems, ysems):
    e = pl.program_id(0)
    n_used = nused_ref[0]

    def x_copy(b, slot):
        rows = pl.ds(pl.multiple_of(b * EXPERT_BLOCK, EXPERT_BLOCK), EXPERT_BLOCK)
        return pltpu.make_async_copy(x_hbm.at[:, rows, :], xbuf.at[slot], xsems.at[slot])

    def y_copy(b, slot):
        rows = pl.ds(pl.multiple_of(b * EXPERT_BLOCK, EXPERT_BLOCK), EXPERT_BLOCK)
        return pltpu.make_async_copy(ybuf.at[slot], y_hbm.at[:, rows, :], ysems.at[slot])

    def prefetch(b_first, max_starts):
        for _ in range(max_starts):
            j = next_ref[0]

            @pl.when((j <= b_first + _X_SLOTS - 1) & (j < n_used))
            def _():
                x_copy(j, lax.rem(j, _X_SLOTS)).start()
                next_ref[0] = j + 1

    def fetch(ex):
        ws = lax.rem(ex, _W_SLOTS)
        cps = []
        for c in range(_WEIGHT_FETCH_CHUNKS):
            rg = pl.ds(c * (D_MODEL // _WEIGHT_FETCH_CHUNKS), D_MODEL // _WEIGHT_FETCH_CHUNKS)
            rd = pl.ds(c * (D_EXPERT // _WEIGHT_FETCH_CHUNKS), D_EXPERT // _WEIGHT_FETCH_CHUNKS)
            cps.append(pltpu.make_async_copy(wgu_hbm.at[ex, rg], wgu_stage.at[ws, rg], sems.at[ws, 0]))
            cps.append(pltpu.make_async_copy(wd_hbm.at[ex, rd], wd_stage.at[ws, rd], sems.at[ws, 1]))
        return cps

    @pl.when(e == 0)
    def _():
        for ex in range(_W_SLOTS):
            for cp in fetch(ex):
                cp.start()
        next_ref[0] = 0
        prefetch(0, _X_SLOTS - 1)

    for cp in fetch(e):
        cp.wait()
    ws = lax.rem(e, _W_SLOTS)
    wgu_bf[...] = wgu_stage[ws].astype(BF16)
    wd_bf[...] = wd_stage[ws].astype(BF16)

    @pl.when(e + _W_SLOTS < N_EXPERTS)
    def _():
        for cp in fetch(e + _W_SLOTS):
            cp.start()

    def compute(b):
        xs = lax.rem(b, _X_SLOTS)
        gu = jnp.dot(_unpack_rows(xbuf.at[xs]), wgu_bf[...], preferred_element_type=F32)
        act = (_silu(gu[:, 0:D_EXPERT]) * gu[:, D_EXPERT:]).astype(BF16)
        packed = _pack_halves(jnp.dot(act, wd_bf[...], preferred_element_type=F32))
        for c in range(ROW_PARTS):
            ybuf[lax.rem(b, _Y_SLOTS), c] = packed[:, c * PART_WORDS:(c + 1) * PART_WORDS]

    def step(b, width):
        for d in range(width):
            x_copy(b + d, lax.rem(b + d, _X_SLOTS)).wait()
        prefetch(b, width)
        for d in range(width):
            @pl.when(b + d >= _Y_SLOTS)
            def _(d=d):
                y_copy(b + d - _Y_SLOTS, lax.rem(b + d, _Y_SLOTS)).wait()
        for d in range(width):
            compute(b + d)
        for d in range(width):
            y_copy(b + d, lax.rem(b + d, _Y_SLOTS)).start()

    lo = base_ref[e]
    n_pairs = nblk_ref[e] // 2

    def pair(i, carry):
        step(lo + 2 * i, 2)
        return carry

    lax.fori_loop(0, n_pairs, pair, 0)

    @pl.when(nblk_ref[e] % 2 == 1)
    def _():
        step(lo + 2 * n_pairs, 1)

    @pl.when(e == N_EXPERTS - 1)
    def _():
        for d in range(1, _Y_SLOTS + 1):
            @pl.when(n_used >= d)
            def _(d=d):
                y_copy(n_used - d, lax.rem(n_used - d, _Y_SLOTS)).wait()


def _pack_halves(x):
    n = x.shape[1] // 2
    hi = lax.bitcast_convert_type(x[:, :n].astype(BF16).astype(F32), jnp.uint32)
    lo = lax.bitcast_convert_type(x[:, n:].astype(BF16).astype(F32), jnp.uint32)
    return hi | (lo >> 16)


def _unpack_halves(p):
    hi = lax.bitcast_convert_type(p & jnp.uint32(0xFFFF0000), F32)
    lo = lax.bitcast_convert_type(p << 16, F32)
    return hi, lo


def _unpack_rows(ref):
    parts = [_unpack_halves(ref[c]) for c in range(ROW_PARTS)]
    chunks = [p[0] for p in parts] + [p[1] for p in parts]
    return jnp.concatenate([xc.astype(BF16) for xc in chunks], axis=1)


def _expert(x_sorted, w_gu, w_down, blk_base, blk_count, n_used):
    n_rows = x_sorted.shape[1]
    blk_shape = (ROW_PARTS, EXPERT_BLOCK, PART_WORDS)
    gs = pltpu.PrefetchScalarGridSpec(
        num_scalar_prefetch=3,
        grid=(N_EXPERTS,),
        in_specs=[pl.BlockSpec(memory_space=pl.ANY),
                  pl.BlockSpec(memory_space=pl.ANY),
                  pl.BlockSpec(memory_space=pl.ANY)],
        out_specs=pl.BlockSpec(memory_space=pl.ANY),
        scratch_shapes=[pltpu.VMEM((D_MODEL, 2 * D_EXPERT), BF16),
                        pltpu.VMEM((D_EXPERT, D_MODEL), BF16),
                        pltpu.VMEM((_W_SLOTS, D_MODEL, 2 * D_EXPERT), F32),
                        pltpu.VMEM((_W_SLOTS, D_EXPERT, D_MODEL), F32),
                        pltpu.VMEM((_X_SLOTS,) + blk_shape, jnp.uint32),
                        pltpu.VMEM((_Y_SLOTS,) + blk_shape, jnp.uint32),
                        pltpu.SMEM((1,), I32),
                        pltpu.SemaphoreType.DMA((_W_SLOTS, 2)),
                        pltpu.SemaphoreType.DMA((_X_SLOTS,)),
                        pltpu.SemaphoreType.DMA((_Y_SLOTS,))])
    return pl.pallas_call(
        _expert_kernel, grid_spec=gs,
        out_shape=jax.ShapeDtypeStruct((ROW_PARTS, n_rows, PART_WORDS), jnp.uint32),
        compiler_params=_cparams(("arbitrary",)),
        name="expert",
    )(blk_base, blk_count, n_used, x_sorted, w_gu, w_down)


def _combine_kernel(row_ref, h2_ref, x1_ref, wt_ref, mod_ref, wsg_ref, wsd_ref, g_ref, b_ref, yt_ref,
                    o_ref, *, first_tile, alpha):
    i = pl.program_id(0)
    h2 = _unpack_rows(h2_ref)
    su = jnp.dot(h2, wsg_ref[...], preferred_element_type=F32)
    act = (_silu(su[:, 0:D_SHARED]) * su[:, D_SHARED:]).astype(BF16)
    moe = jnp.dot(act, wsd_ref[...], preferred_element_type=F32)
    wt = wt_ref[...]
    his, los = [], []
    for c in range(ROW_PARTS):
        rh = jnp.zeros((h2.shape[0], PART_WORDS), F32)
        rl = jnp.zeros((h2.shape[0], PART_WORDS), F32)
        for k in range(TOP_K):
            hi, lo = _unpack_halves(yt_ref[c, k])
            w = wt[:, k:k + 1]
            rh = rh + hi * w
            rl = rl + lo * w
        his.append(rh)
        los.append(rl)
    moe = moe + jnp.concatenate(his + los, axis=1)
    g2 = mod_ref[pl.ds(row_ref[first_tile + i], 1), 5 * D_MODEL:6 * D_MODEL]
    o_ref[...] = _ln_rows(alpha * x1_ref[...] + g2 * moe) * g_ref[...] + b_ref[...]


def _combine(lay, h2, x1, wts_tok, mod, w_sh_gu, w_sh_down, ln_g, ln_b, y_tok, alpha, tm, first_token, n_tok):
    row, _, _ = lay.token_tile_tables(tm)
    first_tile = first_token // tm

    def cur(i, r):
        return (first_tile + i, 0)

    def const(i, r):
        return (0, 0)

    gs = pltpu.PrefetchScalarGridSpec(
        num_scalar_prefetch=1,
        grid=(n_tok // tm,),
        in_specs=[pl.BlockSpec((ROW_PARTS, tm, PART_WORDS), lambda i, r: (0, first_tile + i, 0)),
                  pl.BlockSpec((tm, D_MODEL), cur),
                  pl.BlockSpec((tm, TOP_K), cur),
                  pl.BlockSpec((8, 6 * D_MODEL), const),
                  pl.BlockSpec((D_MODEL, 2 * D_SHARED), const),
                  pl.BlockSpec((D_SHARED, D_MODEL), const),
                  pl.BlockSpec((1, D_MODEL), const),
                  pl.BlockSpec((1, D_MODEL), const),
                  pl.BlockSpec((ROW_PARTS, TOP_K, tm, PART_WORDS), lambda i, r: (0, 0, i, 0))],
        out_specs=pl.BlockSpec((tm, D_MODEL), lambda i, r: (i, 0)))
    return pl.pallas_call(
        functools.partial(_combine_kernel, first_tile=first_tile, alpha=alpha), grid_spec=gs,
        out_shape=jax.ShapeDtypeStruct((n_tok, D_MODEL), F32),
        compiler_params=_cparams(("arbitrary",)),
        name="combine",
    )(row, h2, x1, wts_tok, mod, w_sh_gu, w_sh_down, ln_g.reshape(1, -1), ln_b.reshape(1, -1), y_tok)


_GATHER_WINDOW = 128


def _sc_gather(table, idx):
    n, d = idx.shape[0], table.shape[1]
    mesh = plsc.VectorSubcoreMesh(core_axis_name="core", subcore_axis_name="subcore")

    @pl.kernel(out_type=jax.ShapeDtypeStruct((n, d), table.dtype), mesh=mesh)
    def gather_kernel(table_hbm, idx_hbm, out_hbm):
        def body(idx_vmem, out_vmem):
            pltpu.sync_copy(table_hbm.at[idx_vmem.at[0]], out_vmem)

        pltpu.emit_pipeline(
            body,
            grid=(n // _GATHER_WINDOW,),
            in_specs=[pl.BlockSpec((1, _GATHER_WINDOW), index_map=lambda i: (0, i))],
            out_specs=[pl.BlockSpec((_GATHER_WINDOW, d), index_map=lambda i: (i, 0))],
            core_axis_name=("core", "subcore"),
            dimension_semantics=(pltpu.PARALLEL,),
        )(idx_hbm, out_hbm)

    return gather_kernel(table, idx.reshape(1, n))


def _sc_scatter(rows, idx, n_out, repeat):
    n, d = rows.shape
    mesh = plsc.VectorSubcoreMesh(core_axis_name="core", subcore_axis_name="subcore")

    @pl.kernel(out_type=jax.ShapeDtypeStruct((n_out, d), rows.dtype), mesh=mesh, scratch_types=[])
    def scatter_kernel(rows_hbm, idx_hbm, out_hbm):
        def body(rows_vmem, idx_vmem):
            for r in range(repeat):
                pltpu.sync_copy(rows_vmem, out_hbm.at[idx_vmem.at[r]])

        pltpu.emit_pipeline(
            body,
            grid=(n // _GATHER_WINDOW,),
            in_specs=[pl.BlockSpec((_GATHER_WINDOW, d), index_map=lambda i: (i, 0)),
                      pl.BlockSpec((repeat, _GATHER_WINDOW), index_map=lambda i: (0, i))],
            out_specs=[],
            core_axis_name=("core", "subcore"),
            dimension_semantics=(pltpu.PARALLEL,),
        )(rows_hbm, idx_hbm)

    return scatter_kernel(rows, idx.reshape(repeat, n))


class _Layout:
    def __init__(self, n_prompt_seqs, prompt_len, n_sample_seqs, sample_len):
        self.n_prompt_seqs, self.prompt_len = n_prompt_seqs, prompt_len
        self.n_sample_seqs, self.sample_len = n_sample_seqs, sample_len
        self.n_prompt_tokens = n_prompt_seqs * prompt_len
        self.n_tokens = self.n_prompt_tokens + n_sample_seqs * sample_len
        assert prompt_len % CONV_TILE == 0 and sample_len % CONV_TILE == 0
        self.max_chunks = max(prompt_len, sample_len) // CHUNK

    def token_tile_tables(self, tm):
        assert self.n_prompt_tokens % tm == 0 and self.sample_len % tm == 0
        npt = self.n_prompt_tokens // tm
        per_seq = self.sample_len // tm
        n = self.n_tokens // tm
        row = np.zeros(n, np.int32)
        posb = np.zeros(n, np.int32)
        flag = np.zeros(n, np.int32)
        for i in range(npt, n):
            j = i - npt
            row[i] = 1 + j // per_seq
            posb[i] = j % per_seq
            flag[i] = 1
        return jnp.asarray(row), jnp.asarray(posb), jnp.asarray(flag)

    def conv_tile_tables(self):
        lok, rok = [], []
        for n_seq, length in ((self.n_prompt_seqs, self.prompt_len), (self.n_sample_seqs, self.sample_len)):
            per = length // CONV_TILE
            for _ in range(n_seq):
                for j in range(per):
                    lok.append(int(j > 0))
                    rok.append(int(j < per - 1))
        return jnp.asarray(np.array(lok, np.int32)), jnp.asarray(np.array(rok, np.int32))

    def ssd_step_tables(self):
        cols = [[] for _ in range(9)]
        seqs = []
        c0 = self.n_prompt_tokens // CHUNK
        for j in range(self.n_sample_seqs):
            nc = self.sample_len // CHUNK
            seqs.append((c0 + j * nc, nc, 0, j, 0))
        for j in range(self.n_prompt_seqs):
            nc = self.prompt_len // CHUNK
            seqs.append((j * nc, nc, 1, 0, j))
        for base, nc, zero, sin, sout in seqs:
            for phase in (0, 1):
                order = range(nc - 1, -1, -1) if phase == 0 else range(nc)
                for n, c in enumerate(order):
                    vals = (base + c, base if phase == 0 else base + c, phase, int(n == 0), int(n == nc - 1),
                            zero, sin, sout, c)
                    for col, v in zip(cols, vals):
                        col.append(v)
        return tuple(jnp.asarray(np.array(col, np.int32)) for col in cols)


def _grid_pos_embed(n_tokens):
    rows = n_tokens // GRID_W
    quarter = D_MODEL // 4
    freq = jnp.exp(-math.log(10000.0) * jnp.arange(quarter, dtype=F32) / quarter)
    r = jnp.broadcast_to(jnp.arange(rows, dtype=F32)[:, None, None] * freq, (rows, GRID_W, quarter))
    cl = jnp.broadcast_to(jnp.arange(GRID_W, dtype=F32)[None, :, None] * freq, (rows, GRID_W, quarter))
    emb = jnp.concatenate([jnp.sin(r), jnp.cos(r), jnp.sin(cl), jnp.cos(cl)], axis=-1)
    return emb.reshape(rows * GRID_W, D_MODEL)


def _moe_plan(counts):
    blk_count = (counts + EXPERT_BLOCK - 1) // EXPERT_BLOCK
    blk_end = jnp.cumsum(blk_count)
    blk_base = blk_end - blk_count
    return ((blk_base * EXPERT_BLOCK).astype(I32), blk_base.astype(I32), blk_count.astype(I32),
            blk_end[-1:].astype(I32))


def _layer(lay, xp, xs, pos, cond8, h0f, h0b, lp, alpha, tm_proj=512, tm_route=512, tm_comb=512):
    (w_ada, b_ada, w_in, conv_w, conv_b, conv_ln_g, conv_ln_b, ssm_conv_w, ssm_conv_b, dt_bias, a_log,
     d_skip, ssm_norm_g, w_out, ln1_g, ln1_b, w_router, router_bias, w_exp_gu, w_exp_down, w_sh_gu,
     w_sh_down, ln2_g, ln2_b) = lp
    T = lay.n_tokens
    n_main = 2 * D_CONV + D_SSM + D_XBC
    w_main = w_in[:, :n_main].astype(BF16)
    w_dt = jnp.pad(w_in[:, n_main:], ((0, 0), (0, 128 - 2 * N_HEADS))).astype(BF16)

    mod = _ada(cond8, w_ada, b_ada)
    u, z, xbc, pre = _inproj(lay, xp, xs, pos, mod, w_main, w_dt, dt_bias, a_log, tm_proj)
    conv_out, xbc_c = _conv(lay, u, xbc, conv_w, conv_b, conv_ln_g, conv_ln_b, ssm_conv_w, ssm_conv_b)
    y_ssm, hf, hb = _ssd(lay, xbc_c, z, pre, h0f, h0b, d_skip, ssm_norm_g)
    x1, h2p = _outproj(lay, xp, xs, pos, mod, conv_out, y_ssm, w_out.astype(BF16), ln1_g, ln1_b, alpha, tm_proj)

    idx, wts, posn, cnt = _route(h2p, w_router.T, router_bias, tm_route)
    n_blocks = -(-T * TOP_K // EXPERT_BLOCK) + N_EXPERTS
    pad_start, blk_base, blk_count, n_used = _moe_plan(cnt[:, 0])
    dest2 = _dest(idx, posn, pad_start, min(2048, T))
    n_rows = n_blocks * EXPERT_BLOCK
    scatter_idx = jnp.concatenate([dest2 + c * n_rows for c in range(ROW_PARTS)], axis=1)
    x_sorted = _sc_scatter(h2p.reshape(ROW_PARTS * T, PART_WORDS), scatter_idx.reshape(-1),
                           ROW_PARTS * n_rows, TOP_K).reshape(ROW_PARTS, n_rows, PART_WORDS)
    y_sorted = _expert(x_sorted, w_exp_gu, w_exp_down, blk_base, blk_count, n_used)
    y_flat = y_sorted.reshape(ROW_PARTS * n_rows, PART_WORDS)
    wts_tok, wsg, wsd = wts.T, w_sh_gu.astype(BF16), w_sh_down.astype(BF16)
    outs = []
    for first, n_tok in ((0, lay.n_prompt_tokens), (lay.n_prompt_tokens, T - lay.n_prompt_tokens)):
        dest_g = dest2[:, first:first + n_tok].reshape(-1)
        idx_parts = jnp.concatenate([dest_g + c * n_rows for c in range(ROW_PARTS)])
        y_tok = _sc_gather(y_flat, idx_parts).reshape(ROW_PARTS, TOP_K, n_tok, PART_WORDS)
        outs.append(_combine(lay, h2p, x1, wts_tok, mod, wsg, wsd, ln2_g, ln2_b, y_tok, alpha, tm_comb,
                             first, n_tok))
    return outs[0], outs[1], hf, hb


def kernel(x_prompt, x_sample, state_ssd_fwd, state_ssd_bwd, c, c_ctx, w_ada, b_ada, w_in, conv_w, conv_b, conv_ln_g, conv_ln_b, ssm_conv_w, ssm_conv_b, dt_bias, a_log, d_skip, ssm_norm_g, w_out, ln1_g, ln1_b, w_router, router_bias, w_exp_gu, w_exp_down, w_sh_gu, w_sh_down, ln2_g, ln2_b):
    depth = w_ada.shape[0]
    assert depth == 1, "the prompt and latent passes are fused per layer; one layer is supported"
    bp, lp_, _ = x_prompt.shape
    bd, ld, _ = x_sample.shape
    lay = _Layout(bp, lp_, bd, ld)
    alpha = (2.0 * depth) ** 0.25
    stacked = (w_ada, b_ada, w_in, conv_w, conv_b, conv_ln_g, conv_ln_b, ssm_conv_w, ssm_conv_b,
               dt_bias, a_log, d_skip, ssm_norm_g, w_out, ln1_g, ln1_b, w_router, router_bias,
               w_exp_gu, w_exp_down, w_sh_gu, w_sh_down, ln2_g, ln2_b)
    lp = [w[0] for w in stacked]
    cond8 = jnp.concatenate([c_ctx[None, :], c, jnp.zeros((8 - 1 - bd, D_MODEL), F32)], axis=0)
    pos = _grid_pos_embed(ld)
    sshape = (bd, N_HEADS, HEADDIM, D_STATE)
    out_p, out_s, hf, hb = _layer(lay, x_prompt.reshape(bp * lp_, D_MODEL), x_sample.reshape(bd * ld, D_MODEL),
                                  pos, cond8, state_ssd_fwd[:, 0].reshape(sshape),
                                  state_ssd_bwd[:, 0].reshape(sshape), lp, alpha)
    return (out_p.reshape(bp, lp_, D_MODEL), out_s.reshape(bd, ld, D_MODEL),
            hf[:, None], hb[:, None])
```

```python
import functools
import math

import numpy as np
import jax
import jax.numpy as jnp
from jax import lax
from jax.experimental import pallas as pl
from jax.experimental.pallas import tpu as pltpu
from jax.experimental.pallas import tpu_sc as plsc

F32 = jnp.float32
BF16 = jnp.bfloat16
I32 = jnp.int32
HI = lax.Precision.HIGHEST

D_MODEL = 1024
GRID_W = 64
D_CONV = 1024
CONV_K = 31
N_HEADS = 16
HEADDIM = 64
D_SSM = N_HEADS * HEADDIM
N_GROUPS = 4
HEADS_PER_GROUP = N_HEADS // N_GROUPS
D_STATE = 128
SSM_CONV_K = 4
CHUNK = 128
D_XBC = D_SSM + 2 * N_GROUPS * D_STATE
N_EXPERTS = 256
TOP_K = 8
N_EXPERT_GROUPS = 8
EXPERTS_PER_GROUP = N_EXPERTS // N_EXPERT_GROUPS
TOPK_GROUPS = 4
D_EXPERT = 256
D_SHARED = 256
ROUTED_SCALE = 2.5
LN_EPS = 1e-5

CONV_TILE = 256
HALO = 16
EXPERT_BLOCK = 256
ROW_PARTS = 2
PART_WORDS = D_MODEL // 2 // ROW_PARTS
VMEM_LIMIT = 56 * 1024 * 1024


def _cparams(sem, vmem=VMEM_LIMIT):
    return pltpu.CompilerParams(dimension_semantics=sem, vmem_limit_bytes=vmem)


def _silu(x):
    return x * jax.nn.sigmoid(x)


def _ln_rows(x):
    mu = jnp.mean(x, axis=-1, keepdims=True)
    xc = x - mu
    var = jnp.mean(xc * xc, axis=-1, keepdims=True)
    return xc * lax.rsqrt(var + LN_EPS)


def _iota(shape, dim):
    return lax.broadcasted_iota(I32, shape, dim)


def _expand_matrix(n_in, width):
    return (_iota((n_in, n_in * width), 0) == _iota((n_in, n_in * width), 1) // width).astype(F32)


def _dot_hi(a, b):
    return jnp.dot(a, b, precision=HI, preferred_element_type=F32)


def _split3(x):
    hi = x.astype(BF16)
    r1 = x - hi.astype(F32)
    mid = r1.astype(BF16)
    lo = (r1 - mid.astype(F32)).astype(BF16)
    return jnp.concatenate([hi, mid, lo], axis=1)


def _expand3(n, width):
    rows = np.arange(3 * n)[:, None] % n
    cols = np.arange(n * width)[None, :] // width
    return jnp.asarray(rows == cols, dtype=BF16)


def _expand_exact(x, e3):
    return jnp.dot(_split3(x), e3, preferred_element_type=F32)


def _ada_kernel(c_ref, w_ref, b_ref, o_ref):
    o_ref[...] = _dot_hi(_silu(c_ref[...]), w_ref[...]) + b_ref[...]


def _ada(cond8, w_ada, b_ada):
    n = w_ada.shape[1]
    tn = 1024
    return pl.pallas_call(
        _ada_kernel,
        grid=(n // tn,),
        in_specs=[pl.BlockSpec((8, D_MODEL), lambda j: (0, 0)),
                  pl.BlockSpec((D_MODEL, tn), lambda j: (0, j)),
                  pl.BlockSpec((1, tn), lambda j: (0, j))],
        out_specs=pl.BlockSpec((8, tn), lambda j: (0, j)),
        out_shape=jax.ShapeDtypeStruct((8, n), F32),
        compiler_params=_cparams(("arbitrary",)),
        name="ada",
    )(cond8, w_ada, b_ada.reshape(1, n))


def _inproj_kernel(row_ref, posb_ref, flag_ref, xp_ref, xs_ref, pos_ref, mod_ref, wm_ref, wdt_ref,
                   dtb_ref, alog_ref, tri_ref, u_ref, z_ref, xbc_ref, pre_ref):
    i = pl.program_id(0)
    x = jnp.where(flag_ref[i] == 1, xs_ref[...] + pos_ref[...], xp_ref[...])
    r = row_ref[i]
    sh1 = mod_ref[pl.ds(r, 1), 0:D_MODEL]
    sc1 = mod_ref[pl.ds(r, 1), D_MODEL:2 * D_MODEL]
    h = (_ln_rows(x) * (1.0 + sc1) + sh1).astype(BF16)
    nh2 = 2 * N_HEADS
    dt = jnp.dot(h, wdt_ref[...], preferred_element_type=F32)[:, 0:nh2] + dtb_ref[...]
    dt = jnp.maximum(dt, 0.0) + jnp.log1p(jnp.exp(-jnp.abs(dt)))
    a = dt * (-jnp.exp(alog_ref[...]))
    a_split = _split3(a)
    glu_a = jnp.dot(h, wm_ref[:, 0:D_CONV], preferred_element_type=F32)
    glu_g = jnp.dot(h, wm_ref[:, D_CONV:2 * D_CONV], preferred_element_type=F32)
    u_ref[...] = (glu_a * jax.nn.sigmoid(glu_g)).astype(BF16)
    a3 = jnp.dot(tri_ref[...], a_split, preferred_element_type=F32)
    z = jnp.dot(h, wm_ref[:, 2 * D_CONV:2 * D_CONV + D_SSM], preferred_element_type=F32)
    z_ref[...] = _silu(z).astype(BF16)
    xbc_ref[...] = jnp.dot(h, wm_ref[:, 2 * D_CONV + D_SSM:], preferred_element_type=F32).astype(BF16)
    acs = a3[:, 0:nh2] + a3[:, nh2:2 * nh2] + a3[:, 2 * nh2:3 * nh2]
    pre_ref[...] = jnp.concatenate([dt, acs, a, jnp.zeros((dt.shape[0], 128 - 3 * nh2), F32)], axis=1)


def _inproj(lay, xp, xs, pos, mod, w_main, w_dt, dt_bias, a_log, tm):
    T = lay.n_tokens
    row, posb, flag = lay.token_tile_tables(tm)
    npt = lay.n_prompt_tokens // tm
    n_main = w_main.shape[1]
    chunk_of = np.arange(tm) // CHUNK
    tri = jnp.asarray((chunk_of[:, None] == chunk_of[None, :]) & np.tril(np.ones((tm, tm), bool)), dtype=BF16)
    gs = pltpu.PrefetchScalarGridSpec(
        num_scalar_prefetch=3,
        grid=(T // tm,),
        in_specs=[pl.BlockSpec((tm, D_MODEL), lambda i, r, p, f: (jnp.minimum(i, npt - 1), 0)),
                  pl.BlockSpec((tm, D_MODEL), lambda i, r, p, f: (jnp.maximum(i - npt, 0), 0)),
                  pl.BlockSpec((tm, D_MODEL), lambda i, r, p, f: (p[i], 0)),
                  pl.BlockSpec((8, 6 * D_MODEL), lambda i, r, p, f: (0, 0)),
                  pl.BlockSpec((D_MODEL, n_main), lambda i, r, p, f: (0, 0)),
                  pl.BlockSpec((D_MODEL, 128), lambda i, r, p, f: (0, 0)),
                  pl.BlockSpec((1, 2 * N_HEADS), lambda i, r, p, f: (0, 0)),
                  pl.BlockSpec((1, 2 * N_HEADS), lambda i, r, p, f: (0, 0)),
                  pl.BlockSpec((tm, tm), lambda i, r, p, f: (0, 0))],
        out_specs=[pl.BlockSpec((tm, D_CONV), lambda i, r, p, f: (i, 0)),
                   pl.BlockSpec((tm, D_SSM), lambda i, r, p, f: (i, 0)),
                   pl.BlockSpec((tm, D_XBC), lambda i, r, p, f: (i, 0)),
                   pl.BlockSpec((tm, 128), lambda i, r, p, f: (i, 0))])
    return pl.pallas_call(
        _inproj_kernel, grid_spec=gs,
        out_shape=(jax.ShapeDtypeStruct((T, D_CONV), BF16),
                   jax.ShapeDtypeStruct((T, D_SSM), BF16),
                   jax.ShapeDtypeStruct((T, D_XBC), BF16),
                   jax.ShapeDtypeStruct((T, 128), F32)),
        compiler_params=_cparams(("arbitrary",)),
        name="inproj",
    )(row, posb, flag, xp, xs, pos, mod, w_main, w_dt, dt_bias.reshape(1, -1), a_log.reshape(1, -1), tri)


_N_SHIFT = 8
_SHIFT_ROWS = CONV_TILE + 2 * HALO - _N_SHIFT
_ROW_BLOCK = 64
_FILL_ROWS = 32
_SSM_ROWS, _SSM_LANES = 64, 256


def _conv_kernel(lok_ref, rok_ref, u_ref, ul_ref, ur_ref, xbc_ref, xbcl_ref, xbcr_ref,
                 cw_ref, cb_ref, lng_ref, lnb_ref, sw_ref, sb_ref, co_ref, xo_ref,
                 ext_ref, sh_ref, acc_ref, ext2_ref):
    i = pl.program_id(0)
    lok = lok_ref[i] == 1
    rok = rok_ref[i] == 1

    def fill_ext(rb, carry):
        r0 = pl.multiple_of(rb * _FILL_ROWS, _FILL_ROWS)
        dst = pl.ds(pl.multiple_of(HALO + r0, HALO), _FILL_ROWS)
        ext_ref[dst, :] = u_ref[pl.ds(r0, _FILL_ROWS), :].astype(F32)
        ext2_ref[dst, :] = xbc_ref[pl.ds(r0, _FILL_ROWS), :].astype(F32)
        return carry

    ext_ref[0:HALO, :] = jnp.where(lok, ul_ref[...].astype(F32), 0.0)
    ext_ref[HALO + CONV_TILE:, :] = jnp.where(rok, ur_ref[...].astype(F32), 0.0)
    ext2_ref[0:HALO, :] = jnp.where(lok, xbcl_ref[...].astype(F32), 0.0)
    ext2_ref[HALO + CONV_TILE:, :] = jnp.where(rok, xbcr_ref[...].astype(F32), 0.0)
    lax.fori_loop(0, CONV_TILE // _FILL_ROWS, fill_ext, 0)
    for r in range(_N_SHIFT):
        sh_ref[r] = ext_ref[r:r + _SHIFT_ROWS, :]

    first = HALO - (CONV_K - 1) // 2

    for j in range(D_CONV // 128):
        lanes = slice(j * 128, (j + 1) * 128)
        taps = [jnp.broadcast_to(cw_ref[k:k + 1, lanes], (8, 128)) for k in range(CONV_K)]
        bias = jnp.broadcast_to(cb_ref[:, lanes], (8, 128))

        def row_block(rb, carry, lanes=lanes, taps=taps, bias=bias):
            base = pl.multiple_of(rb * _ROW_BLOCK, _ROW_BLOCK)
            for sub in range(_ROW_BLOCK // 8):
                acc = bias
                for k in range(CONV_K):
                    o = first + k
                    row0 = base + (o // _N_SHIFT) * _N_SHIFT + sub * 8
                    acc = acc + sh_ref[o % _N_SHIFT, pl.ds(row0, 8), lanes] * taps[k]
                acc_ref[pl.ds(base + sub * 8, 8), lanes] = acc
            return carry

        lax.fori_loop(0, CONV_TILE // _ROW_BLOCK, row_block, 0)
    u = _ln_rows(acc_ref[...]) * lng_ref[...] + lnb_ref[...]
    co_ref[...] = _silu(u).astype(BF16)

    first2 = HALO - (SSM_CONV_K - 1) // 2
    for rb in range(CONV_TILE // _SSM_ROWS):
        for lc in range(D_XBC // _SSM_LANES):
            lanes = slice(lc * _SSM_LANES, (lc + 1) * _SSM_LANES)
            y = jnp.zeros((_SSM_ROWS, _SSM_LANES), F32) + sb_ref[:, lanes]
            for k in range(SSM_CONV_K):
                r0 = first2 + k + rb * _SSM_ROWS
                y = y + ext2_ref[r0:r0 + _SSM_ROWS, lanes] * sw_ref[k:k + 1, lanes]
            xo_ref[rb * _SSM_ROWS:(rb + 1) * _SSM_ROWS, lanes] = _silu(y).astype(BF16)


def _conv(lay, u, xbc, conv_w, conv_b, ln_g, ln_b, ssm_w, ssm_b):
    T = lay.n_tokens
    lok, rok = lay.conv_tile_tables()
    n_tiles = T // CONV_TILE
    hb = CONV_TILE // HALO
    n_hb = T // HALO

    def cur(i, l, r):
        return (i, 0)

    def left(i, l, r):
        return (jnp.maximum(i * hb - 1, 0), 0)

    def right(i, l, r):
        return (jnp.minimum((i + 1) * hb, n_hb - 1), 0)

    def const(i, l, r):
        return (0, 0)

    gs = pltpu.PrefetchScalarGridSpec(
        num_scalar_prefetch=2,
        grid=(n_tiles,),
        in_specs=[pl.BlockSpec((CONV_TILE, D_CONV), cur),
                  pl.BlockSpec((HALO, D_CONV), left),
                  pl.BlockSpec((HALO, D_CONV), right),
                  pl.BlockSpec((CONV_TILE, D_XBC), cur),
                  pl.BlockSpec((HALO, D_XBC), left),
                  pl.BlockSpec((HALO, D_XBC), right),
                  pl.BlockSpec((CONV_K, D_CONV), const),
                  pl.BlockSpec((1, D_CONV), const),
                  pl.BlockSpec((1, D_CONV), const),
                  pl.BlockSpec((1, D_CONV), const),
                  pl.BlockSpec((SSM_CONV_K, D_XBC), const),
                  pl.BlockSpec((1, D_XBC), const)],
        out_specs=[pl.BlockSpec((CONV_TILE, D_CONV), cur),
                   pl.BlockSpec((CONV_TILE, D_XBC), cur)],
        scratch_shapes=[pltpu.VMEM((CONV_TILE + 2 * HALO, D_CONV), F32),
                        pltpu.VMEM((_N_SHIFT, _SHIFT_ROWS, D_CONV), F32),
                        pltpu.VMEM((CONV_TILE, D_CONV), F32),
                        pltpu.VMEM((CONV_TILE + 2 * HALO, D_XBC), F32)])
    return pl.pallas_call(
        _conv_kernel, grid_spec=gs,
        out_shape=(jax.ShapeDtypeStruct((T, D_CONV), BF16),
                   jax.ShapeDtypeStruct((T, D_XBC), BF16)),
        compiler_params=_cparams(("arbitrary",)),
        name="conv",
    )(lok, rok, u, u, u, xbc, xbc, xbc, conv_w, conv_b.reshape(1, -1), ln_g.reshape(1, -1),
      ln_b.reshape(1, -1), ssm_w, ssm_b.reshape(1, -1))


_BN = N_GROUPS * D_STATE


def _ssd_kernel(chunk_ref, yidx_ref, phase_ref, first_ref, last_ref, zero_ref, sin_ref, sout_ref, cloc_ref,
                xbc_ref, z_ref, pre_ref, h0f_ref, h0b_ref, dsk_ref, ng_ref,
                edec_ref, ewb_ref, ecol_ref, ewide_ref, eye3_ref,
                y_ref, hf_out_ref, hb_out_ref,
                hf_ref, g_ref, gin_ref, ybuf_ref):
    s = pl.program_id(0)
    phase = phase_ref[s]
    first = first_ref[s] == 1
    last = last_ref[s] == 1
    zero = zero_ref[s] == 1
    cloc = cloc_ref[s]
    H, P, N = N_HEADS, HEADDIM, D_STATE

    GW = HEADS_PER_GROUP * P

    def chunk_inputs(r0):
        rows = slice(r0, r0 + CHUNK)
        xs = xbc_ref[rows, 0:D_SSM]
        dt = pre_ref[rows, 0:2 * H]
        acs = pre_ref[rows, 2 * H:4 * H]
        a = pre_ref[rows, 4 * H:6 * H]
        tot = acs[CHUNK - 8:CHUNK, :]
        dec = _expand_exact(jnp.exp(tot), edec_ref[...])[7:8, :]
        exb = acs[:, H:2 * H] - a[:, H:2 * H]
        return rows, xs, dt, acs, dec, exb

    def load_state(src_ref, dst_ref):
        for j in range(H // 2):
            pair = jnp.concatenate([src_ref[0, 2 * j], src_ref[0, 2 * j + 1]], axis=0)
            dst_ref[:, 2 * j * P:(2 * j + 2) * P] = jnp.where(zero, 0.0, pair.T)

    def store_state(src_ref, dst_ref):
        for j in range(H // 2):
            pair = src_ref[:, 2 * j * P:(2 * j + 2) * P].T
            dst_ref[0, 2 * j] = pair[0:P]
            dst_ref[0, 2 * j + 1] = pair[P:2 * P]

    @pl.when(phase == 0)
    def _backward_states():
        @pl.when(first)
        def _():
            load_state(h0b_ref, g_ref)

        for half in (1, 0):
            rows, xs, dt, acs, dec, exb = chunk_inputs(half * CHUNK)
            wb = dt[:, H:2 * H] * jnp.exp(exb)
            xw = (xs.astype(F32) * _expand_exact(wb, ewb_ref[...])).astype(BF16)
            for g in range(N_GROUPS):
                cols = slice(g * GW, (g + 1) * GW)
                bg = xbc_ref[rows, D_SSM + g * N:D_SSM + (g + 1) * N]
                gg = g_ref[:, cols]
                gin_ref[2 * cloc + half, :, cols] = gg.astype(BF16)
                upd = lax.dot_general(bg, xw[:, cols], (((0,), (0,)), ((), ())), preferred_element_type=F32)
                g_ref[:, cols] = gg * dec[:, D_SSM + g * GW:D_SSM + (g + 1) * GW] + upd

        @pl.when(last)
        def _():
            store_state(g_ref, hb_out_ref)

    def forward_chunk(half):
        rows, xs, dt, acs, dec, exb = chunk_inputs(half * CHUNK)
        acsf = acs[:, 0:H]
        dtf = dt[:, 0:H]
        dtb = dt[:, H:2 * H]
        totf = acs[CHUNK - 1:CHUNK, 0:H]
        totb = acs[CHUNK - 1:CHUNK, H:2 * H]
        col = _expand_exact(jnp.concatenate([acsf, exb], axis=1), ecol_ref[...])
        q3 = _split3(jnp.concatenate([acsf, exb, dtf, dtb], axis=1))
        qt = lax.dot_general(eye3_ref[...], q3, (((1,), (1,)), ((), ())),
                             preferred_element_type=F32)
        wide = jnp.concatenate([dtf * jnp.exp(totf - acsf), jnp.exp(acsf), jnp.exp(totb - exb)], axis=1)
        wide = _expand_exact(wide, ewide_ref[...])
        xsf = xs.astype(F32)
        xw = (xsf * wide[:, 0:D_SSM]).astype(BF16)
        lower = _iota((CHUNK, CHUNK), 1) <= _iota((CHUNK, CHUNK), 0)
        upper = _iota((CHUNK, CHUNK), 1) >= _iota((CHUNK, CHUNK), 0)
        for g in range(N_GROUPS):
            cols = slice(g * GW, (g + 1) * GW)
            bg = xbc_ref[rows, D_SSM + g * N:D_SSM + (g + 1) * N]
            cg = xbc_ref[rows, D_SSM + _BN + g * N:D_SSM + _BN + (g + 1) * N]
            cb = lax.dot_general(cg, bg, (((1,), (1,)), ((), ())), preferred_element_type=F32)
            hfg = hf_ref[:, cols]
            yf = jnp.dot(cg, hfg.astype(BF16), preferred_element_type=F32)
            yb = jnp.dot(cg, gin_ref[2 * cloc + half, :, cols], preferred_element_type=F32)
            yg = yf * wide[:, D_SSM + g * GW:D_SSM + (g + 1) * GW] \
                + yb * wide[:, 2 * D_SSM + g * GW:2 * D_SSM + (g + 1) * GW]
            upd = lax.dot_general(bg, xw[:, cols], (((0,), (0,)), ((), ())), preferred_element_type=F32)
            hf_ref[:, cols] = hfg * dec[:, cols] + upd
            xg = xs[:, cols]
            head_of_lane = _iota((CHUNK, GW), 1) // P
            for r in range(HEADS_PER_GROUP):
                h = g * HEADS_PER_GROUP + r
                colf = col[:, h * N:(h + 1) * N]
                colb = col[:, (H + h) * N:(H + h + 1) * N]
                mf = jnp.where(lower, jnp.exp(colf - qt[h:h + 1, :]), 0.0) * qt[2 * H + h:2 * H + h + 1, :]
                mb = jnp.where(upper, jnp.exp(qt[H + h:H + h + 1, :] - colb), 0.0) * qt[3 * H + h:3 * H + h + 1, :]
                m = (cb * (mf + mb)).astype(BF16)
                xh = jnp.where(head_of_lane == r, xg, jnp.zeros_like(xg))
                yg = yg + jnp.dot(m, xh, preferred_element_type=F32)
            ybuf_ref[rows, cols] = yg

        yt = (ybuf_ref[rows, :] + dsk_ref[...] * xsf) * z_ref[rows, :].astype(F32)
        gw = D_SSM // N_GROUPS
        for g in range(N_GROUPS):
            seg = yt[:, g * gw:(g + 1) * gw]
            ms = jnp.mean(seg * seg, axis=-1, keepdims=True)
            y_ref[rows, g * gw:(g + 1) * gw] = \
                (seg * lax.rsqrt(ms + LN_EPS) * ng_ref[:, g * gw:(g + 1) * gw]).astype(BF16)

    @pl.when(phase == 1)
    def _forward_and_outputs():
        @pl.when(first)
        def _():
            load_state(h0f_ref, hf_ref)

        for half in (0, 1):
            forward_chunk(half)

        @pl.when(last)
        def _():
            store_state(hf_ref, hf_out_ref)


def _ssd(lay, xbc_c, z, pre, h0f, h0b, d_skip, norm_g):
    T = lay.n_tokens
    tabs = lay.ssd_step_tables()
    n_steps = tabs[0].shape[0]
    nsp = len(tabs)

    def by_chunk(s, *t):
        return (t[0][s], 0)

    def by_y(s, *t):
        return (t[1][s], 0)

    def by_sin(s, *t):
        return (t[6][s], 0, 0, 0)

    def by_sout(s, *t):
        return (t[7][s], 0, 0, 0)

    def const(s, *t):
        return (0, 0)

    H = N_HEADS
    eye3 = jnp.asarray(np.arange(4 * H)[:, None] == np.arange(12 * H)[None, :] % (4 * H), dtype=BF16)
    consts = [_expand3(2 * H, HEADDIM), _expand3(H, HEADDIM), _expand3(2 * H, D_STATE),
              _expand3(3 * H, HEADDIM), eye3]
    sshape = (1, N_HEADS, HEADDIM, D_STATE)
    gs = pltpu.PrefetchScalarGridSpec(
        num_scalar_prefetch=nsp,
        grid=(n_steps,),
        in_specs=[pl.BlockSpec((2 * CHUNK, D_XBC), by_chunk),
                  pl.BlockSpec((2 * CHUNK, D_SSM), by_chunk),
                  pl.BlockSpec((2 * CHUNK, 128), by_chunk),
                  pl.BlockSpec(sshape, by_sin),
                  pl.BlockSpec(sshape, by_sin),
                  pl.BlockSpec((1, D_SSM), const),
                  pl.BlockSpec((1, D_SSM), const)] + [pl.BlockSpec(c.shape, const) for c in consts],
        out_specs=[pl.BlockSpec((2 * CHUNK, D_SSM), by_y),
                   pl.BlockSpec(sshape, by_sout),
                   pl.BlockSpec(sshape, by_sout)],
        scratch_shapes=[pltpu.VMEM((D_STATE, D_SSM), F32),
                        pltpu.VMEM((D_STATE, D_SSM), F32),
                        pltpu.VMEM((lay.max_chunks, D_STATE, D_SSM), BF16),
                        pltpu.VMEM((2 * CHUNK, D_SSM), F32)])
    n_out = lay.n_prompt_seqs
    return pl.pallas_call(
        _ssd_kernel, grid_spec=gs,
        out_shape=(jax.ShapeDtypeStruct((T, D_SSM), BF16),
                   jax.ShapeDtypeStruct((n_out,) + sshape[1:], F32),
                   jax.ShapeDtypeStruct((n_out,) + sshape[1:], F32)),
        compiler_params=_cparams(("arbitrary",)),
        name="ssd",
    )(*tabs, xbc_c, z, pre, h0f, h0b,
      jnp.repeat(d_skip, HEADDIM).reshape(1, -1), norm_g.reshape(1, -1), *consts)


def _outproj_kernel(row_ref, posb_ref, flag_ref, xp_ref, xs_ref, pos_ref, mod_ref, co_ref, ys_ref, wo_ref,
                    g_ref, b_ref, x1_ref, h2p_ref, *, alpha):
    i = pl.program_id(0)
    x = jnp.where(flag_ref[i] == 1, xs_ref[...] + pos_ref[...], xp_ref[...])
    r = row_ref[i]
    g1 = mod_ref[pl.ds(r, 1), 2 * D_MODEL:3 * D_MODEL]
    sh2 = mod_ref[pl.ds(r, 1), 3 * D_MODEL:4 * D_MODEL]
    sc2 = mod_ref[pl.ds(r, 1), 4 * D_MODEL:5 * D_MODEL]
    mix = jnp.dot(co_ref[...], wo_ref[0:D_CONV, :], preferred_element_type=F32) \
        + jnp.dot(ys_ref[...], wo_ref[D_CONV:, :], preferred_element_type=F32)
    x1 = _ln_rows(alpha * x + g1 * mix) * g_ref[...] + b_ref[...]
    x1_ref[...] = x1
    packed = _pack_halves(_ln_rows(x1) * (1.0 + sc2) + sh2)
    for c in range(ROW_PARTS):
        h2p_ref[c] = packed[:, c * PART_WORDS:(c + 1) * PART_WORDS]


def _outproj(lay, xp, xs, pos, mod, conv_out, y_ssm, w_out, ln_g, ln_b, alpha, tm):
    T = lay.n_tokens
    row, posb, flag = lay.token_tile_tables(tm)
    npt = lay.n_prompt_tokens // tm

    def const(i, r, p, f):
        return (0, 0)

    def cur(i, r, p, f):
        return (i, 0)

    gs = pltpu.PrefetchScalarGridSpec(
        num_scalar_prefetch=3,
        grid=(T // tm,),
        in_specs=[pl.BlockSpec((tm, D_MODEL), lambda i, r, p, f: (jnp.minimum(i, npt - 1), 0)),
                  pl.BlockSpec((tm, D_MODEL), lambda i, r, p, f: (jnp.maximum(i - npt, 0), 0)),
                  pl.BlockSpec((tm, D_MODEL), lambda i, r, p, f: (p[i], 0)),
                  pl.BlockSpec((8, 6 * D_MODEL), const),
                  pl.BlockSpec((tm, D_CONV), cur),
                  pl.BlockSpec((tm, D_SSM), cur),
                  pl.BlockSpec((D_CONV + D_SSM, D_MODEL), const),
                  pl.BlockSpec((1, D_MODEL), const),
                  pl.BlockSpec((1, D_MODEL), const)],
        out_specs=[pl.BlockSpec((tm, D_MODEL), cur),
                   pl.BlockSpec((ROW_PARTS, tm, PART_WORDS), lambda i, r, p, f: (0, i, 0))])
    return pl.pallas_call(
        functools.partial(_outproj_kernel, alpha=alpha), grid_spec=gs,
        out_shape=(jax.ShapeDtypeStruct((T, D_MODEL), F32),
                   jax.ShapeDtypeStruct((ROW_PARTS, T, PART_WORDS), jnp.uint32)),
        compiler_params=_cparams(("arbitrary",)),
        name="outproj",
    )(row, posb, flag, xp, xs, pos, mod, conv_out, y_ssm, w_out, ln_g.reshape(1, -1), ln_b.reshape(1, -1))


def _route_kernel(h2_ref, wrt_ref, bias_ref, idx_ref, wts_ref, pos_ref, cnt_ref, carry_ref, *, tm):
    i = pl.program_id(0)

    @pl.when(i == 0)
    def _():
        carry_ref[...] = jnp.zeros_like(carry_ref)

    E, NG, EG = N_EXPERTS, N_EXPERT_GROUPS, EXPERTS_PER_GROUP
    neg = -jnp.inf
    logits = lax.dot_general(wrt_ref[...], _unpack_rows(h2_ref), (((1,), (1,)), ((), ())),
                             preferred_element_type=F32)
    s = jax.nn.sigmoid(logits)
    sel = s + bias_ref[...]
    sel3 = sel.reshape(NG, EG, tm)
    io3 = _iota((NG, EG, tm), 1)
    m1 = jnp.max(sel3, axis=1, keepdims=True)
    f1 = jnp.min(jnp.where(sel3 == m1, io3, EG), axis=1, keepdims=True)
    m2 = jnp.max(jnp.where(io3 == f1, neg, sel3), axis=1, keepdims=True)
    gscore = (m1 + m2).reshape(NG, tm)
    gio = _iota((NG, tm), 0)
    beaten = jnp.zeros((NG, tm), I32)
    for g in range(NG):
        row = gscore[g:g + 1, :]
        beats = jnp.where(row > gscore, 1, jnp.where(row == gscore, jnp.where(g < gio, 1, 0), 0))
        beaten = beaten + beats
    keep = (beaten < TOPK_GROUPS).astype(F32).reshape(NG, 1, tm)
    selm = jnp.where(keep > 0.5, sel3, neg).reshape(E, tm)
    eio = _iota((E, tm), 0)
    chosen = jnp.zeros((E, tm), F32)
    idxs, ws = [], []
    for k in range(TOP_K):
        m = jnp.max(selm, axis=0, keepdims=True)
        am = jnp.minimum(jnp.min(jnp.where(selm == m, eio, E), axis=0, keepdims=True), E - 1)
        hit = eio == am
        ws.append(jnp.sum(jnp.where(hit, s, 0.0), axis=0, keepdims=True))
        idxs.append(am)
        selm = jnp.where(hit, neg, selm)
        chosen = jnp.where(hit, 1.0, chosen)
    wsum = ws[0]
    for k in range(1, TOP_K):
        wsum = wsum + ws[k]
    before = (_iota((tm, tm), 0) < _iota((tm, tm), 1)).astype(BF16)
    prior = jnp.dot(chosen.astype(BF16), before, preferred_element_type=F32)
    carry = carry_ref[...]
    prior = prior + jnp.concatenate([carry] * (tm // 128), axis=1)
    for k in range(TOP_K):
        idx_ref[k:k + 1, :] = idxs[k]
        wts_ref[k:k + 1, :] = ws[k] / wsum * ROUTED_SCALE
        pos_ref[k:k + 1, :] = jnp.sum(jnp.where(eio == idxs[k], prior, 0.0), axis=0, keepdims=True).astype(I32)
    total = jnp.dot(chosen.astype(BF16), jnp.ones((tm, 128), BF16), preferred_element_type=F32)
    carry = carry + total
    carry_ref[...] = carry
    cnt_ref[...] = carry.astype(I32)


def _route(h2p, w_router_t, router_bias, tm):
    T = h2p.shape[1]
    bias_b = jnp.broadcast_to(router_bias.astype(F32)[:, None], (N_EXPERTS, tm))
    w_router_t = w_router_t.astype(BF16)
    return pl.pallas_call(
        functools.partial(_route_kernel, tm=tm),
        grid=(T // tm,),
        in_specs=[pl.BlockSpec((ROW_PARTS, tm, PART_WORDS), lambda i: (0, i, 0)),
                  pl.BlockSpec((N_EXPERTS, D_MODEL), lambda i: (0, 0)),
                  pl.BlockSpec((N_EXPERTS, tm), lambda i: (0, 0))],
        out_specs=[pl.BlockSpec((TOP_K, tm), lambda i: (0, i)),
                   pl.BlockSpec((TOP_K, tm), lambda i: (0, i)),
                   pl.BlockSpec((TOP_K, tm), lambda i: (0, i)),
                   pl.BlockSpec((N_EXPERTS, 128), lambda i: (0, 0))],
        out_shape=(jax.ShapeDtypeStruct((TOP_K, T), I32),
                   jax.ShapeDtypeStruct((TOP_K, T), F32),
                   jax.ShapeDtypeStruct((TOP_K, T), I32),
                   jax.ShapeDtypeStruct((N_EXPERTS, 128), I32)),
        scratch_shapes=[pltpu.VMEM((N_EXPERTS, 128), F32)],
        compiler_params=_cparams(("arbitrary",)),
        name="route",
    )(h2p, w_router_t, bias_b)


def _dest_kernel(idx_ref, pos_ref, start_ref, dest_ref):
    tm = idx_ref.shape[1]
    eio = _iota((N_EXPERTS, tm), 0)
    start = start_ref[...]
    for k in range(TOP_K):
        base = jnp.sum(jnp.where(eio == idx_ref[k:k + 1, :], start, 0.0), axis=0, keepdims=True)
        dest_ref[k:k + 1, :] = base.astype(I32) + pos_ref[k:k + 1, :]


def _dest(idx, pos, pad_start, tm):
    T = idx.shape[1]
    start_b = jnp.broadcast_to(pad_start.astype(F32)[:, None], (N_EXPERTS, tm))
    return pl.pallas_call(
        _dest_kernel,
        grid=(T // tm,),
        in_specs=[pl.BlockSpec((TOP_K, tm), lambda i: (0, i)),
                  pl.BlockSpec((TOP_K, tm), lambda i: (0, i)),
                  pl.BlockSpec((N_EXPERTS, tm), lambda i: (0, 0))],
        out_specs=pl.BlockSpec((TOP_K, tm), lambda i: (0, i)),
        out_shape=jax.ShapeDtypeStruct((TOP_K, T), I32),
        compiler_params=_cparams(("arbitrary",)),
        name="dest",
    )(idx, pos, start_b)


_WEIGHT_FETCH_CHUNKS = 4
_W_SLOTS = 8
_X_SLOTS = 6
_Y_SLOTS = 4


def _expert_kernel(base_ref, nblk_ref, nused_ref, x_hbm, wgu_hbm, wd_hbm, y_hbm,
                   wgu_bf, wd_bf, wgu_stage, wd_stage, xbuf, ybuf, next_ref, sems, xsems, ysems):
    e = pl.program_id(0)
    n_used = nused_ref[0]

    def x_copy(b, slot):
        rows = pl.ds(pl.multiple_of(b * EXPERT_BLOCK, EXPERT_BLOCK), EXPERT_BLOCK)
        return pltpu.make_async_copy(x_hbm.at[:, rows, :], xbuf.at[slot], xsems.at[slot])

    def y_copy(b, slot):
        rows = pl.ds(pl.multiple_of(b * EXPERT_BLOCK, EXPERT_BLOCK), EXPERT_BLOCK)
        return pltpu.make_async_copy(ybuf.at[slot], y_hbm.at[:, rows, :], ysems.at[slot])

    def prefetch(b_first, max_starts):
        for _ in range(max_starts):
            j = next_ref[0]

            @pl.when((j <= b_first + _X_SLOTS - 1) & (j < n_used))
            def _():
                x_copy(j, lax.rem(j, _X_SLOTS)).start()
                next_ref[0] = j + 1

    def fetch(ex):
        ws = lax.rem(ex, _W_SLOTS)
        cps = []
        for c in range(_WEIGHT_FETCH_CHUNKS):
            rg = pl.ds(c * (D_MODEL // _WEIGHT_FETCH_CHUNKS), D_MODEL // _WEIGHT_FETCH_CHUNKS)
            rd = pl.ds(c * (D_EXPERT // _WEIGHT_FETCH_CHUNKS), D_EXPERT // _WEIGHT_FETCH_CHUNKS)
            cps.append(pltpu.make_async_copy(wgu_hbm.at[ex, rg], wgu_stage.at[ws, rg], sems.at[ws, 0]))
            cps.append(pltpu.make_async_copy(wd_hbm.at[ex, rd], wd_stage.at[ws, rd], sems.at[ws, 1]))
        return cps

    @pl.when(e == 0)
    def _():
        for ex in range(_W_SLOTS):
            for cp in fetch(ex):
                cp.start()
        next_ref[0] = 0
        prefetch(0, _X_SLOTS - 1)

    for cp in fetch(e):
        cp.wait()
    ws = lax.rem(e, _W_SLOTS)
    wgu_bf[...] = wgu_stage[ws].astype(BF16)
    wd_bf[...] = wd_stage[ws].astype(BF16)

    @pl.when(e + _W_SLOTS < N_EXPERTS)
    def _():
        for cp in fetch(e + _W_SLOTS):
            cp.start()

    def compute(b):
        xs = lax.rem(b, _X_SLOTS)
        gu = jnp.dot(_unpack_rows(xbuf.at[xs]), wgu_bf[...], preferred_element_type=F32)
        act = (_silu(gu[:, 0:D_EXPERT]) * gu[:, D_EXPERT:]).astype(BF16)
        packed = _pack_halves(jnp.dot(act, wd_bf[...], preferred_element_type=F32))
        for c in range(ROW_PARTS):
            ybuf[lax.rem(b, _Y_SLOTS), c] = packed[:, c * PART_WORDS:(c + 1) * PART_WORDS]

    def step(b, width):
        for d in range(width):
            x_copy(b + d, lax.rem(b + d, _X_SLOTS)).wait()
        prefetch(b, width)
        for d in range(width):
            @pl.when(b + d >= _Y_SLOTS)
            def _(d=d):
                y_copy(b + d - _Y_SLOTS, lax.rem(b + d, _Y_SLOTS)).wait()
        for d in range(width):
            compute(b + d)
        for d in range(width):
            y_copy(b + d, lax.rem(b + d, _Y_SLOTS)).start()

    lo = base_ref[e]
    n_pairs = nblk_ref[e] // 2

    def pair(i, carry):
        step(lo + 2 * i, 2)
        return carry

    lax.fori_loop(0, n_pairs, pair, 0)

    @pl.when(nblk_ref[e] % 2 == 1)
    def _():
        step(lo + 2 * n_pairs, 1)

    @pl.when(e == N_EXPERTS - 1)
    def _():
        for d in range(1, _Y_SLOTS + 1):
            @pl.when(n_used >= d)
            def _(d=d):
                y_copy(n_used - d, lax.rem(n_used - d, _Y_SLOTS)).wait()


def _pack_halves(x):
    n = x.shape[1] // 2
    hi = lax.bitcast_convert_type(x[:, :n].astype(BF16).astype(F32), jnp.uint32)
    lo = lax.bitcast_convert_type(x[:, n:].astype(BF16).astype(F32), jnp.uint32)
    return hi | (lo >> 16)


def _unpack_halves(p):
    hi = lax.bitcast_convert_type(p & jnp.uint32(0xFFFF0000), F32)
    lo = lax.bitcast_convert_type(p << 16, F32)
    return hi, lo


def _unpack_rows(ref):
    parts = [_unpack_halves(ref[c]) for c in range(ROW_PARTS)]
    chunks = [p[0] for p in parts] + [p[1] for p in parts]
    return jnp.concatenate([xc.astype(BF16) for xc in chunks], axis=1)


def _expert(x_sorted, w_gu, w_down, blk_base, blk_count, n_used):
    n_rows = x_sorted.shape[1]
    blk_shape = (ROW_PARTS, EXPERT_BLOCK, PART_WORDS)
    gs = pltpu.PrefetchScalarGridSpec(
        num_scalar_prefetch=3,
        grid=(N_EXPERTS,),
        in_specs=[pl.BlockSpec(memory_space=pl.ANY),
                  pl.BlockSpec(memory_space=pl.ANY),
                  pl.BlockSpec(memory_space=pl.ANY)],
        out_specs=pl.BlockSpec(memory_space=pl.ANY),
        scratch_shapes=[pltpu.VMEM((D_MODEL, 2 * D_EXPERT), BF16),
                        pltpu.VMEM((D_EXPERT, D_MODEL), BF16),
                        pltpu.VMEM((_W_SLOTS, D_MODEL, 2 * D_EXPERT), F32),
                        pltpu.VMEM((_W_SLOTS, D_EXPERT, D_MODEL), F32),
                        pltpu.VMEM((_X_SLOTS,) + blk_shape, jnp.uint32),
                        pltpu.VMEM((_Y_SLOTS,) + blk_shape, jnp.uint32),
                        pltpu.SMEM((1,), I32),
                        pltpu.SemaphoreType.DMA((_W_SLOTS, 2)),
                        pltpu.SemaphoreType.DMA((_X_SLOTS,)),
                        pltpu.SemaphoreType.DMA((_Y_SLOTS,))])
    return pl.pallas_call(
        _expert_kernel, grid_spec=gs,
        out_shape=jax.ShapeDtypeStruct((ROW_PARTS, n_rows, PART_WORDS), jnp.uint32),
        compiler_params=_cparams(("arbitrary",)),
        name="expert",
    )(blk_base, blk_count, n_used, x_sorted, w_gu, w_down)


def _combine_kernel(row_ref, h2_ref, x1_ref, wt_ref, mod_ref, wsg_ref, wsd_ref, g_ref, b_ref, yt_ref,
                    o_ref, *, first_tile, alpha):
    i = pl.program_id(0)
    h2 = _unpack_rows(h2_ref)
    su = jnp.dot(h2, wsg_ref[...], preferred_element_type=F32)
    act = (_silu(su[:, 0:D_SHARED]) * su[:, D_SHARED:]).astype(BF16)
    moe = jnp.dot(act, wsd_ref[...], preferred_element_type=F32)
    wt = wt_ref[...]
    his, los = [], []
    for c in range(ROW_PARTS):
        rh = jnp.zeros((h2.shape[0], PART_WORDS), F32)
        rl = jnp.zeros((h2.shape[0], PART_WORDS), F32)
        for k in range(TOP_K):
            hi, lo = _unpack_halves(yt_ref[c, k])
            w = wt[:, k:k + 1]
            rh = rh + hi * w
            rl = rl + lo * w
        his.append(rh)
        los.append(rl)
    moe = moe + jnp.concatenate(his + los, axis=1)
    g2 = mod_ref[pl.ds(row_ref[first_tile + i], 1), 5 * D_MODEL:6 * D_MODEL]
    o_ref[...] = _ln_rows(alpha * x1_ref[...] + g2 * moe) * g_ref[...] + b_ref[...]


def _combine(lay, h2, x1, wts_tok, mod, w_sh_gu, w_sh_down, ln_g, ln_b, y_tok, alpha, tm, first_token, n_tok):
    row, _, _ = lay.token_tile_tables(tm)
    first_tile = first_token // tm

    def cur(i, r):
        return (first_tile + i, 0)

    def const(i, r):
        return (0, 0)

    gs = pltpu.PrefetchScalarGridSpec(
        num_scalar_prefetch=1,
        grid=(n_tok // tm,),
        in_specs=[pl.BlockSpec((ROW_PARTS, tm, PART_WORDS), lambda i, r: (0, first_tile + i, 0)),
                  pl.BlockSpec((tm, D_MODEL), cur),
                  pl.BlockSpec((tm, TOP_K), cur),
                  pl.BlockSpec((8, 6 * D_MODEL), const),
                  pl.BlockSpec((D_MODEL, 2 * D_SHARED), const),
                  pl.BlockSpec((D_SHARED, D_MODEL), const),
                  pl.BlockSpec((1, D_MODEL), const),
                  pl.BlockSpec((1, D_MODEL), const),
                  pl.BlockSpec((ROW_PARTS, TOP_K, tm, PART_WORDS), lambda i, r: (0, 0, i, 0))],
        out_specs=pl.BlockSpec((tm, D_MODEL), lambda i, r: (i, 0)))
    return pl.pallas_call(
        functools.partial(_combine_kernel, first_tile=first_tile, alpha=alpha), grid_spec=gs,
        out_shape=jax.ShapeDtypeStruct((n_tok, D_MODEL), F32),
        compiler_params=_cparams(("arbitrary",)),
        name="combine",
    )(row, h2, x1, wts_tok, mod, w_sh_gu, w_sh_down, ln_g.reshape(1, -1), ln_b.reshape(1, -1), y_tok)


_GATHER_WINDOW = 128


def _sc_gather(table, idx):
    n, d = idx.shape[0], table.shape[1]
    mesh = plsc.VectorSubcoreMesh(core_axis_name="core", subcore_axis_name="subcore")

    @pl.kernel(out_type=jax.ShapeDtypeStruct((n, d), table.dtype), mesh=mesh)
    def gather_kernel(table_hbm, idx_hbm, out_hbm):
        def body(idx_vmem, out_vmem):
            pltpu.sync_copy(table_hbm.at[idx_vmem.at[0]], out_vmem)

        pltpu.emit_pipeline(
            body,
            grid=(n // _GATHER_WINDOW,),
            in_specs=[pl.BlockSpec((1, _GATHER_WINDOW), index_map=lambda i: (0, i))],
            out_specs=[pl.BlockSpec((_GATHER_WINDOW, d), index_map=lambda i: (i, 0))],
            core_axis_name=("core", "subcore"),
            dimension_semantics=(pltpu.PARALLEL,),
        )(idx_hbm, out_hbm)

    return gather_kernel(table, idx.reshape(1, n))


def _sc_scatter(rows, idx, n_out, repeat):
    n, d = rows.shape
    mesh = plsc.VectorSubcoreMesh(core_axis_name="core", subcore_axis_name="subcore")

    @pl.kernel(out_type=jax.ShapeDtypeStruct((n_out, d), rows.dtype), mesh=mesh, scratch_types=[])
    def scatter_kernel(rows_hbm, idx_hbm, out_hbm):
        def body(rows_vmem, idx_vmem):
            for r in range(repeat):
                pltpu.sync_copy(rows_vmem, out_hbm.at[idx_vmem.at[r]])

        pltpu.emit_pipeline(
            body,
            grid=(n // _GATHER_WINDOW,),
            in_specs=[pl.BlockSpec((_GATHER_WINDOW, d), index_map=lambda i: (i, 0)),
                      pl.BlockSpec((repeat, _GATHER_WINDOW), index_map=lambda i: (0, i))],
            out_specs=[],
            core_axis_name=("core", "subcore"),
            dimension_semantics=(pltpu.PARALLEL,),
        )(rows_hbm, idx_hbm)

    return scatter_kernel(rows, idx.reshape(repeat, n))


class _Layout:
    def __init__(self, n_prompt_seqs, prompt_len, n_sample_seqs, sample_len):
        self.n_prompt_seqs, self.prompt_len = n_prompt_seqs, prompt_len
        self.n_sample_seqs, self.sample_len = n_sample_seqs, sample_len
        self.n_prompt_tokens = n_prompt_seqs * prompt_len
        self.n_tokens = self.n_prompt_tokens + n_sample_seqs * sample_len
        assert prompt_len % CONV_TILE == 0 and sample_len % CONV_TILE == 0
        self.max_chunks = max(prompt_len, sample_len) // CHUNK

    def token_tile_tables(self, tm):
        assert self.n_prompt_tokens % tm == 0 and self.sample_len % tm == 0
        npt = self.n_prompt_tokens // tm
        per_seq = self.sample_len // tm
        n = self.n_tokens // tm
        row = np.zeros(n, np.int32)
        posb = np.zeros(n, np.int32)
        flag = np.zeros(n, np.int32)
        for i in range(npt, n):
            j = i - npt
            row[i] = 1 + j // per_seq
            posb[i] = j % per_seq
            flag[i] = 1
        return jnp.asarray(row), jnp.asarray(posb), jnp.asarray(flag)

    def conv_tile_tables(self):
        lok, rok = [], []
        for n_seq, length in ((self.n_prompt_seqs, self.prompt_len), (self.n_sample_seqs, self.sample_len)):
            per = length // CONV_TILE
            for _ in range(n_seq):
                for j in range(per):
                    lok.append(int(j > 0))
                    rok.append(int(j < per - 1))
        return jnp.asarray(np.array(lok, np.int32)), jnp.asarray(np.array(rok, np.int32))

    def ssd_step_tables(self):
        cols = [[] for _ in range(9)]
        seqs = []
        pair_rows = 2 * CHUNK
        c0 = self.n_prompt_tokens // pair_rows
        for j in range(self.n_sample_seqs):
            nc = self.sample_len // pair_rows
            seqs.append((c0 + j * nc, nc, 0, j, 0))
        for j in range(self.n_prompt_seqs):
            nc = self.prompt_len // pair_rows
            seqs.append((j * nc, nc, 1, 0, j))
        for base, nc, zero, sin, sout in seqs:
            for phase in (0, 1):
                order = range(nc - 1, -1, -1) if phase == 0 else range(nc)
                for n, c in enumerate(order):
                    vals = (base + c, base if phase == 0 else base + c, phase, int(n == 0), int(n == nc - 1),
                            zero, sin, sout, c)
                    for col, v in zip(cols, vals):
                        col.append(v)
        return tuple(jnp.asarray(np.array(col, np.int32)) for col in cols)


def _grid_pos_embed(n_tokens):
    rows = n_tokens // GRID_W
    quarter = D_MODEL // 4
    freq = jnp.exp(-math.log(10000.0) * jnp.arange(quarter, dtype=F32) / quarter)
    r = jnp.broadcast_to(jnp.arange(rows, dtype=F32)[:, None, None] * freq, (rows, GRID_W, quarter))
    cl = jnp.broadcast_to(jnp.arange(GRID_W, dtype=F32)[None, :, None] * freq, (rows, GRID_W, quarter))
    emb = jnp.concatenate([jnp.sin(r), jnp.cos(r), jnp.sin(cl), jnp.cos(cl)], axis=-1)
    return emb.reshape(rows * GRID_W, D_MODEL)


def _moe_plan(counts):
    blk_count = (counts + EXPERT_BLOCK - 1) // EXPERT_BLOCK
    blk_end = jnp.cumsum(blk_count)
    blk_base = blk_end - blk_count
    return ((blk_base * EXPERT_BLOCK).astype(I32), blk_base.astype(I32), blk_count.astype(I32),
            blk_end[-1:].astype(I32))


def _layer(lay, xp, xs, pos, cond8, h0f, h0b, lp, alpha, tm_proj=512, tm_route=512, tm_comb=512):
    (w_ada, b_ada, w_in, conv_w, conv_b, conv_ln_g, conv_ln_b, ssm_conv_w, ssm_conv_b, dt_bias, a_log,
     d_skip, ssm_norm_g, w_out, ln1_g, ln1_b, w_router, router_bias, w_exp_gu, w_exp_down, w_sh_gu,
     w_sh_down, ln2_g, ln2_b) = lp
    T = lay.n_tokens
    n_main = 2 * D_CONV + D_SSM + D_XBC
    w_main = w_in[:, :n_main].astype(BF16)
    w_dt = jnp.pad(w_in[:, n_main:], ((0, 0), (0, 128 - 2 * N_HEADS))).astype(BF16)

    mod = _ada(cond8, w_ada, b_ada)
    u, z, xbc, pre = _inproj(lay, xp, xs, pos, mod, w_main, w_dt, dt_bias, a_log, tm_proj)
    conv_out, xbc_c = _conv(lay, u, xbc, conv_w, conv_b, conv_ln_g, conv_ln_b, ssm_conv_w, ssm_conv_b)
    y_ssm, hf, hb = _ssd(lay, xbc_c, z, pre, h0f, h0b, d_skip, ssm_norm_g)
    x1, h2p = _outproj(lay, xp, xs, pos, mod, conv_out, y_ssm, w_out.astype(BF16), ln1_g, ln1_b, alpha, tm_proj)

    idx, wts, posn, cnt = _route(h2p, w_router.T, router_bias, tm_route)
    n_blocks = -(-T * TOP_K // EXPERT_BLOCK) + N_EXPERTS
    pad_start, blk_base, blk_count, n_used = _moe_plan(cnt[:, 0])
    dest2 = _dest(idx, posn, pad_start, min(2048, T))
    n_rows = n_blocks * EXPERT_BLOCK
    scatter_idx = jnp.concatenate([dest2 + c * n_rows for c in range(ROW_PARTS)], axis=1)
    x_sorted = _sc_scatter(h2p.reshape(ROW_PARTS * T, PART_WORDS), scatter_idx.reshape(-1),
                           ROW_PARTS * n_rows, TOP_K).reshape(ROW_PARTS, n_rows, PART_WORDS)
    y_sorted = _expert(x_sorted, w_exp_gu, w_exp_down, blk_base, blk_count, n_used)
    y_flat = y_sorted.reshape(ROW_PARTS * n_rows, PART_WORDS)
    wts_tok, wsg, wsd = wts.T, w_sh_gu.astype(BF16), w_sh_down.astype(BF16)
    outs = []
    for first, n_tok in ((0, lay.n_prompt_tokens), (lay.n_prompt_tokens, T - lay.n_prompt_tokens)):
        dest_g = dest2[:, first:first + n_tok].reshape(-1)
        idx_parts = jnp.concatenate([dest_g + c * n_rows for c in range(ROW_PARTS)])
        y_tok = _sc_gather(y_flat, idx_parts).reshape(ROW_PARTS, TOP_K, n_tok, PART_WORDS)
        outs.append(_combine(lay, h2p, x1, wts_tok, mod, wsg, wsd, ln2_g, ln2_b, y_tok, alpha, tm_comb,
                             first, n_tok))
    return outs[0], outs[1], hf, hb


def kernel(x_prompt, x_sample, state_ssd_fwd, state_ssd_bwd, c, c_ctx, w_ada, b_ada, w_in, conv_w, conv_b, conv_ln_g, conv_ln_b, ssm_conv_w, ssm_conv_b, dt_bias, a_log, d_skip, ssm_norm_g, w_out, ln1_g, ln1_b, w_router, router_bias, w_exp_gu, w_exp_down, w_sh_gu, w_sh_down, ln2_g, ln2_b):
    depth = w_ada.shape[0]
    assert depth == 1, "the prompt and latent passes are fused per layer; one layer is supported"
    bp, lp_, _ = x_prompt.shape
    bd, ld, _ = x_sample.shape
    lay = _Layout(bp, lp_, bd, ld)
    alpha = (2.0 * depth) ** 0.25
    stacked = (w_ada, b_ada, w_in, conv_w, conv_b, conv_ln_g, conv_ln_b, ssm_conv_w, ssm_conv_b,
               dt_bias, a_log, d_skip, ssm_norm_g, w_out, ln1_g, ln1_b, w_router, router_bias,
               w_exp_gu, w_exp_down, w_sh_gu, w_sh_down, ln2_g, ln2_b)
    lp = [w[0] for w in stacked]
    cond8 = jnp.concatenate([c_ctx[None, :], c, jnp.zeros((8 - 1 - bd, D_MODEL), F32)], axis=0)
    pos = _grid_pos_embed(ld)
    sshape = (bd, N_HEADS, HEADDIM, D_STATE)
    out_p, out_s, hf, hb = _layer(lay, x_prompt.reshape(bp * lp_, D_MODEL), x_sample.reshape(bd * ld, D_MODEL),
                                  pos, cond8, state_ssd_fwd[:, 0].reshape(sshape),
                                  state_ssd_bwd[:, 0].reshape(sshape), lp, alpha)
    return (out_p.reshape(bp, lp_, D_MODEL), out_s.reshape(bd, ld, D_MODEL),
            hf[:, None], hb[:, None])
```

```python
import functools
import math

import numpy as np
import jax
import jax.numpy as jnp
from jax import lax
from jax.experimental import pallas as pl
from jax.experimental.pallas import tpu as pltpu
from jax.experimental.pallas import tpu_sc as plsc

F32 = jnp.float32
BF16 = jnp.bfloat16
I32 = jnp.int32
HI = lax.Precision.HIGHEST

D_MODEL = 1024
GRID_W = 64
D_CONV = 1024
CONV_K = 31
N_HEADS = 16
HEADDIM = 64
D_SSM = N_HEADS * HEADDIM
N_GROUPS = 4
HEADS_PER_GROUP = N_HEADS // N_GROUPS
D_STATE = 128
SSM_CONV_K = 4
CHUNK = 128
D_XBC = D_SSM + 2 * N_GROUPS * D_STATE
N_EXPERTS = 256
TOP_K = 8
N_EXPERT_GROUPS = 8
EXPERTS_PER_GROUP = N_EXPERTS // N_EXPERT_GROUPS
TOPK_GROUPS = 4
D_EXPERT = 256
D_SHARED = 256
ROUTED_SCALE = 2.5
LN_EPS = 1e-5

CONV_TILE = 256
HALO = 16
EXPERT_BLOCK = 256
ROW_PARTS = 2
PART_WORDS = D_MODEL // 2 // ROW_PARTS
VMEM_LIMIT = 56 * 1024 * 1024


def _cparams(sem, vmem=VMEM_LIMIT):
    return pltpu.CompilerParams(dimension_semantics=sem, vmem_limit_bytes=vmem)


def _silu(x):
    return x * jax.nn.sigmoid(x)


def _ln_rows(x):
    mu = jnp.mean(x, axis=-1, keepdims=True)
    xc = x - mu
    var = jnp.mean(xc * xc, axis=-1, keepdims=True)
    return xc * lax.rsqrt(var + LN_EPS)


def _iota(shape, dim):
    return lax.broadcasted_iota(I32, shape, dim)


def _dot_hi(a, b):
    return jnp.dot(a, b, precision=HI, preferred_element_type=F32)


def _split3(x):
    hi = x.astype(BF16)
    r1 = x - hi.astype(F32)
    mid = r1.astype(BF16)
    lo = (r1 - mid.astype(F32)).astype(BF16)
    return jnp.concatenate([hi, mid, lo], axis=1)


def _expand3(n, width):
    rows = np.arange(3 * n)[:, None] % n
    cols = np.arange(n * width)[None, :] // width
    return jnp.asarray(rows == cols, dtype=BF16)


def _expand_exact(x, e3):
    return jnp.dot(_split3(x), e3, preferred_element_type=F32)


def _ada_kernel(c_ref, w_ref, b_ref, o_ref):
    o_ref[...] = _dot_hi(_silu(c_ref[...]), w_ref[...]) + b_ref[...]


def _ada(cond8, w_ada, b_ada):
    n = w_ada.shape[1]
    tn = 1024
    return pl.pallas_call(
        _ada_kernel,
        grid=(n // tn,),
        in_specs=[pl.BlockSpec((8, D_MODEL), lambda j: (0, 0)),
                  pl.BlockSpec((D_MODEL, tn), lambda j: (0, j)),
                  pl.BlockSpec((1, tn), lambda j: (0, j))],
        out_specs=pl.BlockSpec((8, tn), lambda j: (0, j)),
        out_shape=jax.ShapeDtypeStruct((8, n), F32),
        compiler_params=_cparams(("arbitrary",)),
        name="ada",
    )(cond8, w_ada, b_ada.reshape(1, n))


def _inproj_kernel(row_ref, posb_ref, flag_ref, xp_ref, xs_ref, pos_ref, mod_ref, wm_ref, wdt_ref,
                   dtb_ref, alog_ref, tri_ref, u_ref, z_ref, xbc_ref, pre_ref):
    i = pl.program_id(0)
    x = jnp.where(flag_ref[i] == 1, xs_ref[...] + pos_ref[...], xp_ref[...])
    r = row_ref[i]
    sh1 = mod_ref[pl.ds(r, 1), 0:D_MODEL]
    sc1 = mod_ref[pl.ds(r, 1), D_MODEL:2 * D_MODEL]
    h = (_ln_rows(x) * (1.0 + sc1) + sh1).astype(BF16)
    nh2 = 2 * N_HEADS
    dt = jnp.dot(h, wdt_ref[...], preferred_element_type=F32)[:, 0:nh2] + dtb_ref[...]
    dt = jnp.maximum(dt, 0.0) + jnp.log1p(jnp.exp(-jnp.abs(dt)))
    a = dt * (-jnp.exp(alog_ref[...]))
    a_split = _split3(a)
    glu_a = jnp.dot(h, wm_ref[:, 0:D_CONV], preferred_element_type=F32)
    glu_g = jnp.dot(h, wm_ref[:, D_CONV:2 * D_CONV], preferred_element_type=F32)
    u_ref[...] = (glu_a * jax.nn.sigmoid(glu_g)).astype(BF16)
    a3 = jnp.dot(tri_ref[...], a_split, preferred_element_type=F32)
    z = jnp.dot(h, wm_ref[:, 2 * D_CONV:2 * D_CONV + D_SSM], preferred_element_type=F32)
    z_ref[...] = _silu(z).astype(BF16)
    xbc_ref[...] = jnp.dot(h, wm_ref[:, 2 * D_CONV + D_SSM:], preferred_element_type=F32).astype(BF16)
    acs = a3[:, 0:nh2] + a3[:, nh2:2 * nh2] + a3[:, 2 * nh2:3 * nh2]
    pre_ref[...] = jnp.concatenate([dt, acs, a, jnp.zeros((dt.shape[0], 128 - 3 * nh2), F32)], axis=1)


def _inproj(lay, xp, xs, pos, mod, w_main, w_dt, dt_bias, a_log, tm):
    T = lay.n_tokens
    row, posb, flag = lay.token_tile_tables(tm)
    npt = lay.n_prompt_tokens // tm
    n_main = w_main.shape[1]
    chunk_of = np.arange(tm) // CHUNK
    tri = jnp.asarray((chunk_of[:, None] == chunk_of[None, :]) & np.tril(np.ones((tm, tm), bool)), dtype=BF16)
    gs = pltpu.PrefetchScalarGridSpec(
        num_scalar_prefetch=3,
        grid=(T // tm,),
        in_specs=[pl.BlockSpec((tm, D_MODEL), lambda i, r, p, f: (jnp.minimum(i, npt - 1), 0)),
                  pl.BlockSpec((tm, D_MODEL), lambda i, r, p, f: (jnp.maximum(i - npt, 0), 0)),
                  pl.BlockSpec((tm, D_MODEL), lambda i, r, p, f: (p[i], 0)),
                  pl.BlockSpec((8, 6 * D_MODEL), lambda i, r, p, f: (0, 0)),
                  pl.BlockSpec((D_MODEL, n_main), lambda i, r, p, f: (0, 0)),
                  pl.BlockSpec((D_MODEL, 128), lambda i, r, p, f: (0, 0)),
                  pl.BlockSpec((1, 2 * N_HEADS), lambda i, r, p, f: (0, 0)),
                  pl.BlockSpec((1, 2 * N_HEADS), lambda i, r, p, f: (0, 0)),
                  pl.BlockSpec((tm, tm), lambda i, r, p, f: (0, 0))],
        out_specs=[pl.BlockSpec((tm, D_CONV), lambda i, r, p, f: (i, 0)),
                   pl.BlockSpec((tm, D_SSM), lambda i, r, p, f: (i, 0)),
                   pl.BlockSpec((tm, D_XBC), lambda i, r, p, f: (i, 0)),
                   pl.BlockSpec((tm, 128), lambda i, r, p, f: (i, 0))])
    return pl.pallas_call(
        _inproj_kernel, grid_spec=gs,
        out_shape=(jax.ShapeDtypeStruct((T, D_CONV), BF16),
                   jax.ShapeDtypeStruct((T, D_SSM), BF16),
                   jax.ShapeDtypeStruct((T, D_XBC), BF16),
                   jax.ShapeDtypeStruct((T, 128), F32)),
        compiler_params=_cparams(("arbitrary",)),
        name="inproj",
    )(row, posb, flag, xp, xs, pos, mod, w_main, w_dt, dt_bias.reshape(1, -1), a_log.reshape(1, -1), tri)


_N_SHIFT = 8
_SHIFT_ROWS = CONV_TILE + 2 * HALO - _N_SHIFT
_ROW_BLOCK = 64
_FILL_ROWS = 32
_SSM_ROWS, _SSM_LANES = 64, 256
_NORM_ROWS = 64


def _conv_kernel(lok_ref, rok_ref, u_ref, ul_ref, ur_ref, xbc_ref, xbcl_ref, xbcr_ref,
                 cw_ref, cb_ref, lng_ref, lnb_ref, sw_ref, sb_ref, co_ref, xo_ref,
                 ext_ref, sh_ref, acc_ref, ext2_ref):
    i = pl.program_id(0)
    lok = lok_ref[i] == 1
    rok = rok_ref[i] == 1

    def fill_ext(rb, carry):
        r0 = pl.multiple_of(rb * _FILL_ROWS, _FILL_ROWS)
        dst = pl.ds(pl.multiple_of(HALO + r0, HALO), _FILL_ROWS)
        ext_ref[dst, :] = u_ref[pl.ds(r0, _FILL_ROWS), :].astype(F32)
        ext2_ref[dst, :] = xbc_ref[pl.ds(r0, _FILL_ROWS), :].astype(F32)
        return carry

    ext_ref[0:HALO, :] = jnp.where(lok, ul_ref[...].astype(F32), 0.0)
    ext_ref[HALO + CONV_TILE:, :] = jnp.where(rok, ur_ref[...].astype(F32), 0.0)
    ext2_ref[0:HALO, :] = jnp.where(lok, xbcl_ref[...].astype(F32), 0.0)
    ext2_ref[HALO + CONV_TILE:, :] = jnp.where(rok, xbcr_ref[...].astype(F32), 0.0)
    lax.fori_loop(0, CONV_TILE // _FILL_ROWS, fill_ext, 0)
    for r in range(_N_SHIFT):
        sh_ref[r] = ext_ref[r:r + _SHIFT_ROWS, :]

    first = HALO - (CONV_K - 1) // 2

    for j in range(D_CONV // 128):
        lanes = slice(j * 128, (j + 1) * 128)
        taps = [jnp.broadcast_to(cw_ref[k:k + 1, lanes], (8, 128)) for k in range(CONV_K)]
        bias = jnp.broadcast_to(cb_ref[:, lanes], (8, 128))

        def row_block(rb, carry, lanes=lanes, taps=taps, bias=bias):
            base = pl.multiple_of(rb * _ROW_BLOCK, _ROW_BLOCK)
            for sub in range(_ROW_BLOCK // 8):
                acc = bias
                for k in range(CONV_K):
                    o = first + k
                    row0 = base + (o // _N_SHIFT) * _N_SHIFT + sub * 8
                    acc = acc + sh_ref[o % _N_SHIFT, pl.ds(row0, 8), lanes] * taps[k]
                acc_ref[pl.ds(base + sub * 8, 8), lanes] = acc
            return carry

        lax.fori_loop(0, CONV_TILE // _ROW_BLOCK, row_block, 0)
    for part in range(CONV_TILE // _NORM_ROWS):
        rows = slice(part * _NORM_ROWS, (part + 1) * _NORM_ROWS)
        u = _ln_rows(acc_ref[rows, :]) * lng_ref[...] + lnb_ref[...]
        co_ref[rows, :] = _silu(u).astype(BF16)

    first2 = HALO - (SSM_CONV_K - 1) // 2
    for rb in range(CONV_TILE // _SSM_ROWS):
        for lc in range(D_XBC // _SSM_LANES):
            lanes = slice(lc * _SSM_LANES, (lc + 1) * _SSM_LANES)
            y = jnp.zeros((_SSM_ROWS, _SSM_LANES), F32) + sb_ref[:, lanes]
            for k in range(SSM_CONV_K):
                r0 = first2 + k + rb * _SSM_ROWS
                y = y + ext2_ref[r0:r0 + _SSM_ROWS, lanes] * sw_ref[k:k + 1, lanes]
            xo_ref[rb * _SSM_ROWS:(rb + 1) * _SSM_ROWS, lanes] = _silu(y).astype(BF16)


def _conv(lay, u, xbc, conv_w, conv_b, ln_g, ln_b, ssm_w, ssm_b):
    T = lay.n_tokens
    lok, rok = lay.conv_tile_tables()
    n_tiles = T // CONV_TILE
    hb = CONV_TILE // HALO
    n_hb = T // HALO

    def cur(i, l, r):
        return (i, 0)

    def left(i, l, r):
        return (jnp.maximum(i * hb - 1, 0), 0)

    def right(i, l, r):
        return (jnp.minimum((i + 1) * hb, n_hb - 1), 0)

    def const(i, l, r):
        return (0, 0)

    gs = pltpu.PrefetchScalarGridSpec(
        num_scalar_prefetch=2,
        grid=(n_tiles,),
        in_specs=[pl.BlockSpec((CONV_TILE, D_CONV), cur),
                  pl.BlockSpec((HALO, D_CONV), left),
                  pl.BlockSpec((HALO, D_CONV), right),
                  pl.BlockSpec((CONV_TILE, D_XBC), cur),
                  pl.BlockSpec((HALO, D_XBC), left),
                  pl.BlockSpec((HALO, D_XBC), right),
                  pl.BlockSpec((CONV_K, D_CONV), const),
                  pl.BlockSpec((1, D_CONV), const),
                  pl.BlockSpec((1, D_CONV), const),
                  pl.BlockSpec((1, D_CONV), const),
                  pl.BlockSpec((SSM_CONV_K, D_XBC), const),
                  pl.BlockSpec((1, D_XBC), const)],
        out_specs=[pl.BlockSpec((CONV_TILE, D_CONV), cur),
                   pl.BlockSpec((CONV_TILE, D_XBC), cur)],
        scratch_shapes=[pltpu.VMEM((CONV_TILE + 2 * HALO, D_CONV), F32),
                        pltpu.VMEM((_N_SHIFT, _SHIFT_ROWS, D_CONV), F32),
                        pltpu.VMEM((CONV_TILE, D_CONV), F32),
                        pltpu.VMEM((CONV_TILE + 2 * HALO, D_XBC), F32)])
    return pl.pallas_call(
        _conv_kernel, grid_spec=gs,
        out_shape=(jax.ShapeDtypeStruct((T, D_CONV), BF16),
                   jax.ShapeDtypeStruct((T, D_XBC), BF16)),
        compiler_params=_cparams(("arbitrary",)),
        name="conv",
    )(lok, rok, u, u, u, xbc, xbc, xbc, conv_w, conv_b.reshape(1, -1), ln_g.reshape(1, -1),
      ln_b.reshape(1, -1), ssm_w, ssm_b.reshape(1, -1))


_BN = N_GROUPS * D_STATE


def _ssd_kernel(chunk_ref, yidx_ref, phase_ref, first_ref, last_ref, zero_ref, sin_ref, sout_ref, cloc_ref,
                xbc_ref, z_ref, pre_ref, h0f_ref, h0b_ref, dsk_ref, ng_ref,
                edec_ref, ewb_ref, ecol_ref, ewide_ref, eye3_ref,
                y_ref, hf_out_ref, hb_out_ref,
                hf_ref, g_ref, gin_ref, ybuf_ref):
    s = pl.program_id(0)
    phase = phase_ref[s]
    first = first_ref[s] == 1
    last = last_ref[s] == 1
    zero = zero_ref[s] == 1
    cloc = cloc_ref[s]
    H, P, N = N_HEADS, HEADDIM, D_STATE

    GW = HEADS_PER_GROUP * P

    def chunk_inputs(r0):
        rows = slice(r0, r0 + CHUNK)
        xs = xbc_ref[rows, 0:D_SSM]
        dt = pre_ref[rows, 0:2 * H]
        acs = pre_ref[rows, 2 * H:4 * H]
        a = pre_ref[rows, 4 * H:6 * H]
        tot = acs[CHUNK - 8:CHUNK, :]
        dec = _expand_exact(jnp.exp(tot), edec_ref[...])[7:8, :]
        exb = acs[:, H:2 * H] - a[:, H:2 * H]
        return rows, xs, dt, acs, dec, exb

    def load_state(src_ref, dst_ref):
        for j in range(H // 2):
            pair = jnp.concatenate([src_ref[0, 2 * j], src_ref[0, 2 * j + 1]], axis=0)
            dst_ref[:, 2 * j * P:(2 * j + 2) * P] = jnp.where(zero, 0.0, pair.T)

    def store_state(src_ref, dst_ref):
        for j in range(H // 2):
            pair = src_ref[:, 2 * j * P:(2 * j + 2) * P].T
            dst_ref[0, 2 * j] = pair[0:P]
            dst_ref[0, 2 * j + 1] = pair[P:2 * P]

    @pl.when(phase == 0)
    def _backward_states():
        @pl.when(first)
        def _():
            load_state(h0b_ref, g_ref)

        for half in (1, 0):
            rows, xs, dt, acs, dec, exb = chunk_inputs(half * CHUNK)
            wb = dt[:, H:2 * H] * jnp.exp(exb)
            xw = (xs.astype(F32) * _expand_exact(wb, ewb_ref[...])).astype(BF16)
            for g in range(N_GROUPS):
                cols = slice(g * GW, (g + 1) * GW)
                bg = xbc_ref[rows, D_SSM + g * N:D_SSM + (g + 1) * N]
                gg = g_ref[:, cols]
                gin_ref[2 * cloc + half, :, cols] = gg.astype(BF16)
                upd = lax.dot_general(bg, xw[:, cols], (((0,), (0,)), ((), ())), preferred_element_type=F32)
                g_ref[:, cols] = gg * dec[:, D_SSM + g * GW:D_SSM + (g + 1) * GW] + upd

        @pl.when(last)
        def _():
            store_state(g_ref, hb_out_ref)

    def forward_chunk(half):
        rows, xs, dt, acs, dec, exb = chunk_inputs(half * CHUNK)
        acsf = acs[:, 0:H]
        dtf = dt[:, 0:H]
        dtb = dt[:, H:2 * H]
        totf = acs[CHUNK - 1:CHUNK, 0:H]
        totb = acs[CHUNK - 1:CHUNK, H:2 * H]
        col = _expand_exact(jnp.concatenate([acsf, exb], axis=1), ecol_ref[...])
        q3 = _split3(jnp.concatenate([acsf, exb, dtf, dtb], axis=1))
        qt = lax.dot_general(eye3_ref[...], q3, (((1,), (1,)), ((), ())),
                             preferred_element_type=F32)
        wide = jnp.concatenate([dtf * jnp.exp(totf - acsf), jnp.exp(acsf), jnp.exp(totb - exb)], axis=1)
        wide = _expand_exact(wide, ewide_ref[...])
        xsf = xs.astype(F32)
        xw = (xsf * wide[:, 0:D_SSM]).astype(BF16)
        lower = _iota((CHUNK, CHUNK), 1) <= _iota((CHUNK, CHUNK), 0)
        upper = _iota((CHUNK, CHUNK), 1) >= _iota((CHUNK, CHUNK), 0)
        for g in range(N_GROUPS):
            cols = slice(g * GW, (g + 1) * GW)
            bg = xbc_ref[rows, D_SSM + g * N:D_SSM + (g + 1) * N]
            cg = xbc_ref[rows, D_SSM + _BN + g * N:D_SSM + _BN + (g + 1) * N]
            cb = lax.dot_general(cg, bg, (((1,), (1,)), ((), ())), preferred_element_type=F32)
            hfg = hf_ref[:, cols]
            yf = jnp.dot(cg, hfg.astype(BF16), preferred_element_type=F32)
            yb = jnp.dot(cg, gin_ref[2 * cloc + half, :, cols], preferred_element_type=F32)
            yg = yf * wide[:, D_SSM + g * GW:D_SSM + (g + 1) * GW] \
                + yb * wide[:, 2 * D_SSM + g * GW:2 * D_SSM + (g + 1) * GW]
            upd = lax.dot_general(bg, xw[:, cols], (((0,), (0,)), ((), ())), preferred_element_type=F32)
            hf_ref[:, cols] = hfg * dec[:, cols] + upd
            xg = xs[:, cols]
            head_of_lane = _iota((CHUNK, GW), 1) // P
            for r in range(HEADS_PER_GROUP):
                h = g * HEADS_PER_GROUP + r
                colf = col[:, h * N:(h + 1) * N]
                colb = col[:, (H + h) * N:(H + h + 1) * N]
                mf = jnp.where(lower, jnp.exp(colf - qt[h:h + 1, :]), 0.0) * qt[2 * H + h:2 * H + h + 1, :]
                mb = jnp.where(upper, jnp.exp(qt[H + h:H + h + 1, :] - colb), 0.0) * qt[3 * H + h:3 * H + h + 1, :]
                m = (cb * (mf + mb)).astype(BF16)
                xh = jnp.where(head_of_lane == r, xg, jnp.zeros_like(xg))
                yg = yg + jnp.dot(m, xh, preferred_element_type=F32)
            ybuf_ref[rows, cols] = yg

        yt = (ybuf_ref[rows, :] + dsk_ref[...] * xsf) * z_ref[rows, :].astype(F32)
        gw = D_SSM // N_GROUPS
        for g in range(N_GROUPS):
            seg = yt[:, g * gw:(g + 1) * gw]
            ms = jnp.mean(seg * seg, axis=-1, keepdims=True)
            y_ref[rows, g * gw:(g + 1) * gw] = \
                (seg * lax.rsqrt(ms + LN_EPS) * ng_ref[:, g * gw:(g + 1) * gw]).astype(BF16)

    @pl.when(phase == 1)
    def _forward_and_outputs():
        @pl.when(first)
        def _():
            load_state(h0f_ref, hf_ref)

        for half in (0, 1):
            forward_chunk(half)

        @pl.when(last)
        def _():
            store_state(hf_ref, hf_out_ref)


def _ssd(lay, xbc_c, z, pre, h0f, h0b, d_skip, norm_g):
    T = lay.n_tokens
    tabs = lay.ssd_step_tables()
    n_steps = tabs[0].shape[0]
    nsp = len(tabs)

    def by_chunk(s, *t):
        return (t[0][s], 0)

    def by_y(s, *t):
        return (t[1][s], 0)

    def by_sin(s, *t):
        return (t[6][s], 0, 0, 0)

    def by_sout(s, *t):
        return (t[7][s], 0, 0, 0)

    def const(s, *t):
        return (0, 0)

    H = N_HEADS
    eye3 = jnp.asarray(np.arange(4 * H)[:, None] == np.arange(12 * H)[None, :] % (4 * H), dtype=BF16)
    consts = [_expand3(2 * H, HEADDIM), _expand3(H, HEADDIM), _expand3(2 * H, D_STATE),
              _expand3(3 * H, HEADDIM), eye3]
    sshape = (1, N_HEADS, HEADDIM, D_STATE)
    gs = pltpu.PrefetchScalarGridSpec(
        num_scalar_prefetch=nsp,
        grid=(n_steps,),
        in_specs=[pl.BlockSpec((2 * CHUNK, D_XBC), by_chunk),
                  pl.BlockSpec((2 * CHUNK, D_SSM), by_chunk),
                  pl.BlockSpec((2 * CHUNK, 128), by_chunk),
                  pl.BlockSpec(sshape, by_sin),
                  pl.BlockSpec(sshape, by_sin),
                  pl.BlockSpec((1, D_SSM), const),
                  pl.BlockSpec((1, D_SSM), const)] + [pl.BlockSpec(c.shape, const) for c in consts],
        out_specs=[pl.BlockSpec((2 * CHUNK, D_SSM), by_y),
                   pl.BlockSpec(sshape, by_sout),
                   pl.BlockSpec(sshape, by_sout)],
        scratch_shapes=[pltpu.VMEM((D_STATE, D_SSM), F32),
                        pltpu.VMEM((D_STATE, D_SSM), F32),
                        pltpu.VMEM((lay.max_chunks, D_STATE, D_SSM), BF16),
                        pltpu.VMEM((2 * CHUNK, D_SSM), F32)])
    n_out = lay.n_prompt_seqs
    return pl.pallas_call(
        _ssd_kernel, grid_spec=gs,
        out_shape=(jax.ShapeDtypeStruct((T, D_SSM), BF16),
                   jax.ShapeDtypeStruct((n_out,) + sshape[1:], F32),
                   jax.ShapeDtypeStruct((n_out,) + sshape[1:], F32)),
        compiler_params=_cparams(("arbitrary",)),
        name="ssd",
    )(*tabs, xbc_c, z, pre, h0f, h0b,
      jnp.repeat(d_skip, HEADDIM).reshape(1, -1), norm_g.reshape(1, -1), *consts)


def _outproj_kernel(row_ref, posb_ref, flag_ref, xp_ref, xs_ref, pos_ref, mod_ref, co_ref, ys_ref, wo_ref,
                    g_ref, b_ref, x1_ref, h2p_ref, *, alpha):
    i = pl.program_id(0)
    x = jnp.where(flag_ref[i] == 1, xs_ref[...] + pos_ref[...], xp_ref[...])
    r = row_ref[i]
    g1 = mod_ref[pl.ds(r, 1), 2 * D_MODEL:3 * D_MODEL]
    sh2 = mod_ref[pl.ds(r, 1), 3 * D_MODEL:4 * D_MODEL]
    sc2 = mod_ref[pl.ds(r, 1), 4 * D_MODEL:5 * D_MODEL]
    mix = jnp.dot(jnp.concatenate([co_ref[...], ys_ref[...]], axis=1), wo_ref[...], preferred_element_type=F32)
    x1 = _ln_rows(alpha * x + g1 * mix) * g_ref[...] + b_ref[...]
    x1_ref[...] = x1
    packed = _pack_halves(_ln_rows(x1) * (1.0 + sc2) + sh2)
    for c in range(ROW_PARTS):
        h2p_ref[c] = packed[:, c * PART_WORDS:(c + 1) * PART_WORDS]


def _outproj(lay, xp, xs, pos, mod, conv_out, y_ssm, w_out, ln_g, ln_b, alpha, tm):
    T = lay.n_tokens
    row, posb, flag = lay.token_tile_tables(tm)
    npt = lay.n_prompt_tokens // tm

    def const(i, r, p, f):
        return (0, 0)

    def cur(i, r, p, f):
        return (i, 0)

    gs = pltpu.PrefetchScalarGridSpec(
        num_scalar_prefetch=3,
        grid=(T // tm,),
        in_specs=[pl.BlockSpec((tm, D_MODEL), lambda i, r, p, f: (jnp.minimum(i, npt - 1), 0)),
                  pl.BlockSpec((tm, D_MODEL), lambda i, r, p, f: (jnp.maximum(i - npt, 0), 0)),
                  pl.BlockSpec((tm, D_MODEL), lambda i, r, p, f: (p[i], 0)),
                  pl.BlockSpec((8, 6 * D_MODEL), const),
                  pl.BlockSpec((tm, D_CONV), cur),
                  pl.BlockSpec((tm, D_SSM), cur),
                  pl.BlockSpec((D_CONV + D_SSM, D_MODEL), const),
                  pl.BlockSpec((1, D_MODEL), const),
                  pl.BlockSpec((1, D_MODEL), const)],
        out_specs=[pl.BlockSpec((tm, D_MODEL), cur),
                   pl.BlockSpec((ROW_PARTS, tm, PART_WORDS), lambda i, r, p, f: (0, i, 0))])
    return pl.pallas_call(
        functools.partial(_outproj_kernel, alpha=alpha), grid_spec=gs,
        out_shape=(jax.ShapeDtypeStruct((T, D_MODEL), F32),
                   jax.ShapeDtypeStruct((ROW_PARTS, T, PART_WORDS), jnp.uint32)),
        compiler_params=_cparams(("arbitrary",)),
        name="outproj",
    )(row, posb, flag, xp, xs, pos, mod, conv_out, y_ssm, w_out, ln_g.reshape(1, -1), ln_b.reshape(1, -1))


def _route_kernel(h2_ref, wrt_ref, bias_ref, idx_ref, wts_ref, pos_ref, cnt_ref, carry_ref, *, tm):
    i = pl.program_id(0)

    @pl.when(i == 0)
    def _():
        carry_ref[...] = jnp.zeros_like(carry_ref)

    E, NG, EG = N_EXPERTS, N_EXPERT_GROUPS, EXPERTS_PER_GROUP
    neg = -jnp.inf
    logits = lax.dot_general(wrt_ref[...], _unpack_rows(h2_ref), (((1,), (1,)), ((), ())),
                             preferred_element_type=F32)
    s = jax.nn.sigmoid(logits)
    sel = s + bias_ref[...]
    sel3 = sel.reshape(NG, EG, tm)
    io3 = _iota((NG, EG, tm), 1)
    m1 = jnp.max(sel3, axis=1, keepdims=True)
    f1 = jnp.min(jnp.where(sel3 == m1, io3, EG), axis=1, keepdims=True)
    m2 = jnp.max(jnp.where(io3 == f1, neg, sel3), axis=1, keepdims=True)
    gscore = (m1 + m2).reshape(NG, tm)
    gio = _iota((NG, tm), 0)
    beaten = jnp.zeros((NG, tm), I32)
    for g in range(NG):
        row = gscore[g:g + 1, :]
        beats = jnp.where(row > gscore, 1, jnp.where(row == gscore, jnp.where(g < gio, 1, 0), 0))
        beaten = beaten + beats
    keep = (beaten < TOPK_GROUPS).astype(F32).reshape(NG, 1, tm)
    selm = jnp.where(keep > 0.5, sel3, neg).reshape(E, tm)
    eio = _iota((E, tm), 0)
    chosen = jnp.zeros((E, tm), F32)
    idxs, ws = [], []
    for k in range(TOP_K):
        m = jnp.max(selm, axis=0, keepdims=True)
        am = jnp.minimum(jnp.min(jnp.where(selm == m, eio, E), axis=0, keepdims=True), E - 1)
        hit = eio == am
        ws.append(jnp.sum(jnp.where(hit, s, 0.0), axis=0, keepdims=True))
        idxs.append(am)
        selm = jnp.where(hit, neg, selm)
        chosen = jnp.where(hit, 1.0, chosen)
    wsum = ws[0]
    for k in range(1, TOP_K):
        wsum = wsum + ws[k]
    before = (_iota((tm, tm), 0) < _iota((tm, tm), 1)).astype(BF16)
    prior = jnp.dot(chosen.astype(BF16), before, preferred_element_type=F32)
    carry = carry_ref[...]
    prior = prior + jnp.concatenate([carry] * (tm // 128), axis=1)
    for k in range(TOP_K):
        idx_ref[k:k + 1, :] = idxs[k]
        wts_ref[k:k + 1, :] = ws[k] / wsum * ROUTED_SCALE
        pos_ref[k:k + 1, :] = jnp.sum(jnp.where(eio == idxs[k], prior, 0.0), axis=0, keepdims=True).astype(I32)
    total = jnp.dot(chosen.astype(BF16), jnp.ones((tm, 128), BF16), preferred_element_type=F32)
    carry = carry + total
    carry_ref[...] = carry
    cnt_ref[...] = carry.astype(I32)


def _route(h2p, w_router_t, router_bias, tm):
    T = h2p.shape[1]
    bias_b = jnp.broadcast_to(router_bias.astype(F32)[:, None], (N_EXPERTS, tm))
    w_router_t = w_router_t.astype(BF16)
    return pl.pallas_call(
        functools.partial(_route_kernel, tm=tm),
        grid=(T // tm,),
        in_specs=[pl.BlockSpec((ROW_PARTS, tm, PART_WORDS), lambda i: (0, i, 0)),
                  pl.BlockSpec((N_EXPERTS, D_MODEL), lambda i: (0, 0)),
                  pl.BlockSpec((N_EXPERTS, tm), lambda i: (0, 0))],
        out_specs=[pl.BlockSpec((TOP_K, tm), lambda i: (0, i)),
                   pl.BlockSpec((TOP_K, tm), lambda i: (0, i)),
                   pl.BlockSpec((TOP_K, tm), lambda i: (0, i)),
                   pl.BlockSpec((N_EXPERTS, 128), lambda i: (0, 0))],
        out_shape=(jax.ShapeDtypeStruct((TOP_K, T), I32),
                   jax.ShapeDtypeStruct((TOP_K, T), F32),
                   jax.ShapeDtypeStruct((TOP_K, T), I32),
                   jax.ShapeDtypeStruct((N_EXPERTS, 128), I32)),
        scratch_shapes=[pltpu.VMEM((N_EXPERTS, 128), F32)],
        compiler_params=_cparams(("arbitrary",)),
        name="route",
    )(h2p, w_router_t, bias_b)


def _dest_kernel(idx_ref, pos_ref, start_ref, dest_ref):
    tm = idx_ref.shape[1]
    eio = _iota((N_EXPERTS, tm), 0)
    start = start_ref[...]
    for k in range(TOP_K):
        base = jnp.sum(jnp.where(eio == idx_ref[k:k + 1, :], start, 0.0), axis=0, keepdims=True)
        dest_ref[k:k + 1, :] = base.astype(I32) + pos_ref[k:k + 1, :]


def _dest(idx, pos, pad_start, tm):
    T = idx.shape[1]
    start_b = jnp.broadcast_to(pad_start.astype(F32)[:, None], (N_EXPERTS, tm))
    return pl.pallas_call(
        _dest_kernel,
        grid=(T // tm,),
        in_specs=[pl.BlockSpec((TOP_K, tm), lambda i: (0, i)),
                  pl.BlockSpec((TOP_K, tm), lambda i: (0, i)),
                  pl.BlockSpec((N_EXPERTS, tm), lambda i: (0, 0))],
        out_specs=pl.BlockSpec((TOP_K, tm), lambda i: (0, i)),
        out_shape=jax.ShapeDtypeStruct((TOP_K, T), I32),
        compiler_params=_cparams(("arbitrary",)),
        name="dest",
    )(idx, pos, start_b)


_WEIGHT_FETCH_CHUNKS = 4
_W_SLOTS = 8
_X_SLOTS = 6
_Y_SLOTS = 4


def _expert_kernel(base_ref, nblk_ref, nused_ref, x_hbm, wgu_hbm, wd_hbm, y_hbm,
                   wgu_bf, wd_bf, wgu_stage, wd_stage, xbuf, ybuf, next_ref, sems, xsems, ysems):
    e = pl.program_id(0)
    n_used = nused_ref[0]

    def x_copy(b, slot):
        rows = pl.ds(pl.multiple_of(b * EXPERT_BLOCK, EXPERT_BLOCK), EXPERT_BLOCK)
        return pltpu.make_async_copy(x_hbm.at[:, rows, :], xbuf.at[slot], xsems.at[slot])

    def y_copy(b, slot):
        rows = pl.ds(pl.multiple_of(b * EXPERT_BLOCK, EXPERT_BLOCK), EXPERT_BLOCK)
        return pltpu.make_async_copy(ybuf.at[slot], y_hbm.at[:, rows, :], ysems.at[slot])

    def prefetch(b_first, max_starts):
        for _ in range(max_starts):
            j = next_ref[0]

            @pl.when((j <= b_first + _X_SLOTS - 1) & (j < n_used))
            def _():
                x_copy(j, lax.rem(j, _X_SLOTS)).start()
                next_ref[0] = j + 1

    def fetch(ex):
        ws = lax.rem(ex, _W_SLOTS)
        cps = []
        for c in range(_WEIGHT_FETCH_CHUNKS):
            rg = pl.ds(c * (D_MODEL // _WEIGHT_FETCH_CHUNKS), D_MODEL // _WEIGHT_FETCH_CHUNKS)
            rd = pl.ds(c * (D_EXPERT // _WEIGHT_FETCH_CHUNKS), D_EXPERT // _WEIGHT_FETCH_CHUNKS)
            cps.append(pltpu.make_async_copy(wgu_hbm.at[ex, rg], wgu_stage.at[ws, rg], sems.at[ws, 0]))
            cps.append(pltpu.make_async_copy(wd_hbm.at[ex, rd], wd_stage.at[ws, rd], sems.at[ws, 1]))
        return cps

    @pl.when(e == 0)
    def _():
        for ex in range(_W_SLOTS):
            for cp in fetch(ex):
                cp.start()
        next_ref[0] = 0
        prefetch(0, _X_SLOTS - 1)

    for cp in fetch(e):
        cp.wait()
    ws = lax.rem(e, _W_SLOTS)
    wgu_bf[...] = wgu_stage[ws].astype(BF16)
    wd_bf[...] = wd_stage[ws].astype(BF16)

    @pl.when(e + _W_SLOTS < N_EXPERTS)
    def _():
        for cp in fetch(e + _W_SLOTS):
            cp.start()

    def compute(b):
        xs = lax.rem(b, _X_SLOTS)
        gu = jnp.dot(_unpack_rows(xbuf.at[xs]), wgu_bf[...], preferred_element_type=F32)
        act = (_silu(gu[:, 0:D_EXPERT]) * gu[:, D_EXPERT:]).astype(BF16)
        packed = _pack_halves(jnp.dot(act, wd_bf[...], preferred_element_type=F32))
        for c in range(ROW_PARTS):
            ybuf[lax.rem(b, _Y_SLOTS), c] = packed[:, c * PART_WORDS:(c + 1) * PART_WORDS]

    def step(b, width):
        for d in range(width):
            x_copy(b + d, lax.rem(b + d, _X_SLOTS)).wait()
        prefetch(b, width)
        for d in range(width):
            @pl.when(b + d >= _Y_SLOTS)
            def _(d=d):
                y_copy(b + d - _Y_SLOTS, lax.rem(b + d, _Y_SLOTS)).wait()
        for d in range(width):
            compute(b + d)
        for d in range(width):
            y_copy(b + d, lax.rem(b + d, _Y_SLOTS)).start()

    lo = base_ref[e]
    n_pairs = nblk_ref[e] // 2

    def pair(i, carry):
        step(lo + 2 * i, 2)
        return carry

    lax.fori_loop(0, n_pairs, pair, 0)

    @pl.when(nblk_ref[e] % 2 == 1)
    def _():
        step(lo + 2 * n_pairs, 1)

    @pl.when(e == N_EXPERTS - 1)
    def _():
        for d in range(1, _Y_SLOTS + 1):
            @pl.when(n_used >= d)
            def _(d=d):
                y_copy(n_used - d, lax.rem(n_used - d, _Y_SLOTS)).wait()


def _pack_halves(x):
    n = x.shape[1] // 2
    hi = lax.bitcast_convert_type(x[:, :n].astype(BF16).astype(F32), jnp.uint32)
    lo = lax.bitcast_convert_type(x[:, n:].astype(BF16).astype(F32), jnp.uint32)
    return hi | (lo >> 16)


def _unpack_halves(p):
    hi = lax.bitcast_convert_type(p & jnp.uint32(0xFFFF0000), F32)
    lo = lax.bitcast_convert_type(p << 16, F32)
    return hi, lo


def _unpack_rows(ref):
    parts = [_unpack_halves(ref[c]) for c in range(ROW_PARTS)]
    chunks = [p[0] for p in parts] + [p[1] for p in parts]
    return jnp.concatenate([xc.astype(BF16) for xc in chunks], axis=1)


def _expert(x_sorted, w_gu, w_down, blk_base, blk_count, n_used):
    n_rows = x_sorted.shape[1]
    blk_shape = (ROW_PARTS, EXPERT_BLOCK, PART_WORDS)
    gs = pltpu.PrefetchScalarGridSpec(
        num_scalar_prefetch=3,
        grid=(N_EXPERTS,),
        in_specs=[pl.BlockSpec(memory_space=pl.ANY),
                  pl.BlockSpec(memory_space=pl.ANY),
                  pl.BlockSpec(memory_space=pl.ANY)],
        out_specs=pl.BlockSpec(memory_space=pl.ANY),
        scratch_shapes=[pltpu.VMEM((D_MODEL, 2 * D_EXPERT), BF16),
                        pltpu.VMEM((D_EXPERT, D_MODEL), BF16),
                        pltpu.VMEM((_W_SLOTS, D_MODEL, 2 * D_EXPERT), F32),
                        pltpu.VMEM((_W_SLOTS, D_EXPERT, D_MODEL), F32),
                        pltpu.VMEM((_X_SLOTS,) + blk_shape, jnp.uint32),
                        pltpu.VMEM((_Y_SLOTS,) + blk_shape, jnp.uint32),
                        pltpu.SMEM((1,), I32),
                        pltpu.SemaphoreType.DMA((_W_SLOTS, 2)),
                        pltpu.SemaphoreType.DMA((_X_SLOTS,)),
                        pltpu.SemaphoreType.DMA((_Y_SLOTS,))])
    return pl.pallas_call(
        _expert_kernel, grid_spec=gs,
        out_shape=jax.ShapeDtypeStruct((ROW_PARTS, n_rows, PART_WORDS), jnp.uint32),
        compiler_params=_cparams(("arbitrary",)),
        name="expert",
    )(blk_base, blk_count, n_used, x_sorted, w_gu, w_down)


def _combine_kernel(row_ref, h2_ref, x1_ref, wt_ref, mod_ref, wsg_ref, wsd_ref, g_ref, b_ref, yt_ref,
                    o_ref, *, first_tile, alpha):
    i = pl.program_id(0)
    h2 = _unpack_rows(h2_ref)
    su = jnp.dot(h2, wsg_ref[...], preferred_element_type=F32)
    act = (_silu(su[:, 0:D_SHARED]) * su[:, D_SHARED:]).astype(BF16)
    moe = jnp.dot(act, wsd_ref[...], preferred_element_type=F32)
    wt = wt_ref[...]
    his, los = [], []
    for c in range(ROW_PARTS):
        rh = jnp.zeros((h2.shape[0], PART_WORDS), F32)
        rl = jnp.zeros((h2.shape[0], PART_WORDS), F32)
        for k in range(TOP_K):
            hi, lo = _unpack_halves(yt_ref[c, k])
            w = wt[:, k:k + 1]
            rh = rh + hi * w
            rl = rl + lo * w
        his.append(rh)
        los.append(rl)
    moe = moe + jnp.concatenate(his + los, axis=1)
    g2 = mod_ref[pl.ds(row_ref[first_tile + i], 1), 5 * D_MODEL:6 * D_MODEL]
    o_ref[...] = _ln_rows(alpha * x1_ref[...] + g2 * moe) * g_ref[...] + b_ref[...]


def _combine(lay, h2, x1, wts_tok, mod, w_sh_gu, w_sh_down, ln_g, ln_b, y_tok, alpha, tm, first_token, n_tok):
    row, _, _ = lay.token_tile_tables(tm)
    first_tile = first_token // tm

    def cur(i, r):
        return (first_tile + i, 0)

    def const(i, r):
        return (0, 0)

    gs = pltpu.PrefetchScalarGridSpec(
        num_scalar_prefetch=1,
        grid=(n_tok // tm,),
        in_specs=[pl.BlockSpec((ROW_PARTS, tm, PART_WORDS), lambda i, r: (0, first_tile + i, 0)),
                  pl.BlockSpec((tm, D_MODEL), cur),
                  pl.BlockSpec((tm, TOP_K), cur),
                  pl.BlockSpec((8, 6 * D_MODEL), const),
                  pl.BlockSpec((D_MODEL, 2 * D_SHARED), const),
                  pl.BlockSpec((D_SHARED, D_MODEL), const),
                  pl.BlockSpec((1, D_MODEL), const),
                  pl.BlockSpec((1, D_MODEL), const),
                  pl.BlockSpec((ROW_PARTS, TOP_K, tm, PART_WORDS), lambda i, r: (0, 0, i, 0))],
        out_specs=pl.BlockSpec((tm, D_MODEL), lambda i, r: (i, 0)))
    return pl.pallas_call(
        functools.partial(_combine_kernel, first_tile=first_tile, alpha=alpha), grid_spec=gs,
        out_shape=jax.ShapeDtypeStruct((n_tok, D_MODEL), F32),
        compiler_params=_cparams(("arbitrary",)),
        name="combine",
    )(row, h2, x1, wts_tok, mod, w_sh_gu, w_sh_down, ln_g.reshape(1, -1), ln_b.reshape(1, -1), y_tok)


_GATHER_WINDOW = 128


def _sc_gather(table, idx):
    n, d = idx.shape[0], table.shape[1]
    mesh = plsc.VectorSubcoreMesh(core_axis_name="core", subcore_axis_name="subcore")

    @pl.kernel(out_type=jax.ShapeDtypeStruct((n, d), table.dtype), mesh=mesh)
    def gather_kernel(table_hbm, idx_hbm, out_hbm):
        def body(idx_vmem, out_vmem):
            pltpu.sync_copy(table_hbm.at[idx_vmem.at[0]], out_vmem)

        pltpu.emit_pipeline(
            body,
            grid=(n // _GATHER_WINDOW,),
            in_specs=[pl.BlockSpec((1, _GATHER_WINDOW), index_map=lambda i: (0, i))],
            out_specs=[pl.BlockSpec((_GATHER_WINDOW, d), index_map=lambda i: (i, 0))],
            core_axis_name=("core", "subcore"),
            dimension_semantics=(pltpu.PARALLEL,),
        )(idx_hbm, out_hbm)

    return gather_kernel(table, idx.reshape(1, n))


def _sc_scatter(rows, idx, n_out, repeat):
    n, d = rows.shape
    mesh = plsc.VectorSubcoreMesh(core_axis_name="core", subcore_axis_name="subcore")

    @pl.kernel(out_type=jax.ShapeDtypeStruct((n_out, d), rows.dtype), mesh=mesh, scratch_types=[])
    def scatter_kernel(rows_hbm, idx_hbm, out_hbm):
        def body(rows_vmem, idx_vmem):
            for r in range(repeat):
                pltpu.sync_copy(rows_vmem, out_hbm.at[idx_vmem.at[r]])

        pltpu.emit_pipeline(
            body,
            grid=(n // _GATHER_WINDOW,),
            in_specs=[pl.BlockSpec((_GATHER_WINDOW, d), index_map=lambda i: (i, 0)),
                      pl.BlockSpec((repeat, _GATHER_WINDOW), index_map=lambda i: (0, i))],
            out_specs=[],
            core_axis_name=("core", "subcore"),
            dimension_semantics=(pltpu.PARALLEL,),
        )(rows_hbm, idx_hbm)

    return scatter_kernel(rows, idx.reshape(repeat, n))


class _Layout:
    def __init__(self, n_prompt_seqs, prompt_len, n_sample_seqs, sample_len):
        self.n_prompt_seqs, self.prompt_len = n_prompt_seqs, prompt_len
        self.n_sample_seqs, self.sample_len = n_sample_seqs, sample_len
        self.n_prompt_tokens = n_prompt_seqs * prompt_len
        self.n_tokens = self.n_prompt_tokens + n_sample_seqs * sample_len
        assert prompt_len % CONV_TILE == 0 and sample_len % CONV_TILE == 0
        self.max_chunks = max(prompt_len, sample_len) // CHUNK

    def token_tile_tables(self, tm):
        assert self.n_prompt_tokens % tm == 0 and self.sample_len % tm == 0
        npt = self.n_prompt_tokens // tm
        per_seq = self.sample_len // tm
        n = self.n_tokens // tm
        row = np.zeros(n, np.int32)
        posb = np.zeros(n, np.int32)
        flag = np.zeros(n, np.int32)
        for i in range(npt, n):
            j = i - npt
            row[i] = 1 + j // per_seq
            posb[i] = j % per_seq
            flag[i] = 1
        return jnp.asarray(row), jnp.asarray(posb), jnp.asarray(flag)

    def conv_tile_tables(self):
        lok, rok = [], []
        for n_seq, length in ((self.n_prompt_seqs, self.prompt_len), (self.n_sample_seqs, self.sample_len)):
            per = length // CONV_TILE
            for _ in range(n_seq):
                for j in range(per):
                    lok.append(int(j > 0))
                    rok.append(int(j < per - 1))
        return jnp.asarray(np.array(lok, np.int32)), jnp.asarray(np.array(rok, np.int32))

    def ssd_step_tables(self):
        cols = [[] for _ in range(9)]
        seqs = []
        pair_rows = 2 * CHUNK
        c0 = self.n_prompt_tokens // pair_rows
        for j in range(self.n_sample_seqs):
            nc = self.sample_len // pair_rows
            seqs.append((c0 + j * nc, nc, 0, j, 0))
        for j in range(self.n_prompt_seqs):
            nc = self.prompt_len // pair_rows
            seqs.append((j * nc, nc, 1, 0, j))
        for base, nc, zero, sin, sout in seqs:
            for phase in (0, 1):
                order = range(nc - 1, -1, -1) if phase == 0 else range(nc)
                for n, c in enumerate(order):
                    vals = (base + c, base if phase == 0 else base + c, phase, int(n == 0), int(n == nc - 1),
                            zero, sin, sout, c)
                    for col, v in zip(cols, vals):
                        col.append(v)
        return tuple(jnp.asarray(np.array(col, np.int32)) for col in cols)


def _grid_pos_embed(n_tokens):
    rows = n_tokens // GRID_W
    quarter = D_MODEL // 4
    freq = jnp.exp(-math.log(10000.0) * jnp.arange(quarter, dtype=F32) / quarter)
    r = jnp.broadcast_to(jnp.arange(rows, dtype=F32)[:, None, None] * freq, (rows, GRID_W, quarter))
    cl = jnp.broadcast_to(jnp.arange(GRID_W, dtype=F32)[None, :, None] * freq, (rows, GRID_W, quarter))
    emb = jnp.concatenate([jnp.sin(r), jnp.cos(r), jnp.sin(cl), jnp.cos(cl)], axis=-1)
    return emb.reshape(rows * GRID_W, D_MODEL)


def _moe_plan(counts):
    blk_count = (counts + EXPERT_BLOCK - 1) // EXPERT_BLOCK
    blk_end = jnp.cumsum(blk_count)
    blk_base = blk_end - blk_count
    return ((blk_base * EXPERT_BLOCK).astype(I32), blk_base.astype(I32), blk_count.astype(I32),
            blk_end[-1:].astype(I32))


def _layer(lay, xp, xs, pos, cond8, h0f, h0b, lp, alpha, tm_proj=512, tm_route=512, tm_comb=512):
    (w_ada, b_ada, w_in, conv_w, conv_b, conv_ln_g, conv_ln_b, ssm_conv_w, ssm_conv_b, dt_bias, a_log,
     d_skip, ssm_norm_g, w_out, ln1_g, ln1_b, w_router, router_bias, w_exp_gu, w_exp_down, w_sh_gu,
     w_sh_down, ln2_g, ln2_b) = lp
    T = lay.n_tokens
    n_main = 2 * D_CONV + D_SSM + D_XBC
    w_main = w_in[:, :n_main].astype(BF16)
    w_dt = jnp.pad(w_in[:, n_main:], ((0, 0), (0, 128 - 2 * N_HEADS))).astype(BF16)

    mod = _ada(cond8, w_ada, b_ada)
    u, z, xbc, pre = _inproj(lay, xp, xs, pos, mod, w_main, w_dt, dt_bias, a_log, tm_proj)
    conv_out, xbc_c = _conv(lay, u, xbc, conv_w, conv_b, conv_ln_g, conv_ln_b, ssm_conv_w, ssm_conv_b)
    y_ssm, hf, hb = _ssd(lay, xbc_c, z, pre, h0f, h0b, d_skip, ssm_norm_g)
    x1, h2p = _outproj(lay, xp, xs, pos, mod, conv_out, y_ssm, w_out.astype(BF16), ln1_g, ln1_b, alpha, tm_proj)

    idx, wts, posn, cnt = _route(h2p, w_router.T, router_bias, tm_route)
    n_blocks = -(-T * TOP_K // EXPERT_BLOCK) + N_EXPERTS
    pad_start, blk_base, blk_count, n_used = _moe_plan(cnt[:, 0])
    dest2 = _dest(idx, posn, pad_start, min(2048, T))
    n_rows = n_blocks * EXPERT_BLOCK
    scatter_idx = jnp.concatenate([dest2 + c * n_rows for c in range(ROW_PARTS)], axis=1)
    x_sorted = _sc_scatter(h2p.reshape(ROW_PARTS * T, PART_WORDS), scatter_idx.reshape(-1),
                           ROW_PARTS * n_rows, TOP_K).reshape(ROW_PARTS, n_rows, PART_WORDS)
    y_sorted = _expert(x_sorted, w_exp_gu, w_exp_down, blk_base, blk_count, n_used)
    y_flat = y_sorted.reshape(ROW_PARTS * n_rows, PART_WORDS)
    wts_tok, wsg, wsd = wts.T, w_sh_gu.astype(BF16), w_sh_down.astype(BF16)
    outs = []
    for first, n_tok in ((0, lay.n_prompt_tokens), (lay.n_prompt_tokens, T - lay.n_prompt_tokens)):
        dest_g = dest2[:, first:first + n_tok].reshape(-1)
        idx_parts = jnp.concatenate([dest_g + c * n_rows for c in range(ROW_PARTS)])
        y_tok = _sc_gather(y_flat, idx_parts).reshape(ROW_PARTS, TOP_K, n_tok, PART_WORDS)
        outs.append(_combine(lay, h2p, x1, wts_tok, mod, wsg, wsd, ln2_g, ln2_b, y_tok, alpha, tm_comb,
                             first, n_tok))
    return outs[0], outs[1], hf, hb


def kernel(x_prompt, x_sample, state_ssd_fwd, state_ssd_bwd, c, c_ctx, w_ada, b_ada, w_in, conv_w, conv_b, conv_ln_g, conv_ln_b, ssm_conv_w, ssm_conv_b, dt_bias, a_log, d_skip, ssm_norm_g, w_out, ln1_g, ln1_b, w_router, router_bias, w_exp_gu, w_exp_down, w_sh_gu, w_sh_down, ln2_g, ln2_b):
    depth = w_ada.shape[0]
    assert depth == 1, "the prompt and latent passes are fused per layer; one layer is supported"
    bp, lp_, _ = x_prompt.shape
    bd, ld, _ = x_sample.shape
    lay = _Layout(bp, lp_, bd, ld)
    alpha = (2.0 * depth) ** 0.25
    stacked = (w_ada, b_ada, w_in, conv_w, conv_b, conv_ln_g, conv_ln_b, ssm_conv_w, ssm_conv_b,
               dt_bias, a_log, d_skip, ssm_norm_g, w_out, ln1_g, ln1_b, w_router, router_bias,
               w_exp_gu, w_exp_down, w_sh_gu, w_sh_down, ln2_g, ln2_b)
    lp = [w[0] for w in stacked]
    cond8 = jnp.concatenate([c_ctx[None, :], c, jnp.zeros((8 - 1 - bd, D_MODEL), F32)], axis=0)
    pos = _grid_pos_embed(ld)
    sshape = (bd, N_HEADS, HEADDIM, D_STATE)
    out_p, out_s, hf, hb = _layer(lay, x_prompt.reshape(bp * lp_, D_MODEL), x_sample.reshape(bd * ld, D_MODEL),
                                  pos, cond8, state_ssd_fwd[:, 0].reshape(sshape),
                                  state_ssd_bwd[:, 0].reshape(sshape), lp, alpha)
    return (out_p.reshape(bp, lp_, D_MODEL), out_s.reshape(bd, ld, D_MODEL),
            hf[:, None], hb[:, None])
```

```python
import functools
import math

import numpy as np
import jax
import jax.numpy as jnp
from jax import lax
from jax.experimental import pallas as pl
from jax.experimental.pallas import tpu as pltpu
from jax.experimental.pallas import tpu_sc as plsc

F32 = jnp.float32
BF16 = jnp.bfloat16
I32 = jnp.int32
HI = lax.Precision.HIGHEST

D_MODEL = 1024
GRID_W = 64
D_CONV = 1024
CONV_K = 31
N_HEADS = 16
HEADDIM = 64
D_SSM = N_HEADS * HEADDIM
N_GROUPS = 4
HEADS_PER_GROUP = N_HEADS // N_GROUPS
D_STATE = 128
SSM_CONV_K = 4
CHUNK = 128
D_XBC = D_SSM + 2 * N_GROUPS * D_STATE
N_EXPERTS = 256
TOP_K = 8
N_EXPERT_GROUPS = 8
EXPERTS_PER_GROUP = N_EXPERTS // N_EXPERT_GROUPS
TOPK_GROUPS = 4
D_EXPERT = 256
D_SHARED = 256
ROUTED_SCALE = 2.5
LN_EPS = 1e-5

CONV_TILE = 256
HALO = 16
EXPERT_BLOCK = 256
ROW_PARTS = 2
PART_WORDS = D_MODEL // 2 // ROW_PARTS
VMEM_LIMIT = 56 * 1024 * 1024


def _cparams(sem, vmem=VMEM_LIMIT):
    return pltpu.CompilerParams(dimension_semantics=sem, vmem_limit_bytes=vmem)


def _silu(x):
    return x * jax.nn.sigmoid(x)


def _ln_rows(x):
    mu = jnp.mean(x, axis=-1, keepdims=True)
    xc = x - mu
    var = jnp.mean(xc * xc, axis=-1, keepdims=True)
    return xc * lax.rsqrt(var + LN_EPS)


def _iota(shape, dim):
    return lax.broadcasted_iota(I32, shape, dim)


def _dot_hi(a, b):
    return jnp.dot(a, b, precision=HI, preferred_element_type=F32)


def _split3(x):
    hi = x.astype(BF16)
    r1 = x - hi.astype(F32)
    mid = r1.astype(BF16)
    lo = (r1 - mid.astype(F32)).astype(BF16)
    return jnp.concatenate([hi, mid, lo], axis=1)


def _expand3(n, width):
    rows = np.arange(3 * n)[:, None] % n
    cols = np.arange(n * width)[None, :] // width
    return jnp.asarray(rows == cols, dtype=BF16)


def _expand_exact(x, e3):
    return jnp.dot(_split3(x), e3, preferred_element_type=F32)


def _ada_kernel(c_ref, w_ref, b_ref, o_ref):
    o_ref[...] = _dot_hi(_silu(c_ref[...]), w_ref[...]) + b_ref[...]


def _ada(cond8, w_ada, b_ada):
    n = w_ada.shape[1]
    tn = 1024
    return pl.pallas_call(
        _ada_kernel,
        grid=(n // tn,),
        in_specs=[pl.BlockSpec((8, D_MODEL), lambda j: (0, 0)),
                  pl.BlockSpec((D_MODEL, tn), lambda j: (0, j)),
                  pl.BlockSpec((1, tn), lambda j: (0, j))],
        out_specs=pl.BlockSpec((8, tn), lambda j: (0, j)),
        out_shape=jax.ShapeDtypeStruct((8, n), F32),
        compiler_params=_cparams(("arbitrary",)),
        name="ada",
    )(cond8, w_ada, b_ada.reshape(1, n))


def _inproj_kernel(row_ref, posb_ref, flag_ref, xp_ref, xs_ref, pos_ref, mod_ref, wm_ref, wdt_ref,
                   dtb_ref, alog_ref, tri_ref, u_ref, z_ref, xbc_ref, pre_ref):
    i = pl.program_id(0)
    x = jnp.where(flag_ref[i] == 1, xs_ref[...] + pos_ref[...], xp_ref[...])
    r = row_ref[i]
    sh1 = mod_ref[pl.ds(r, 1), 0:D_MODEL]
    sc1 = mod_ref[pl.ds(r, 1), D_MODEL:2 * D_MODEL]
    h = (_ln_rows(x) * (1.0 + sc1) + sh1).astype(BF16)
    nh2 = 2 * N_HEADS
    dt = jnp.dot(h, wdt_ref[...], preferred_element_type=F32)[:, 0:nh2] + dtb_ref[...]
    dt = jnp.maximum(dt, 0.0) + jnp.log1p(jnp.exp(-jnp.abs(dt)))
    a = dt * (-jnp.exp(alog_ref[...]))
    a_split = _split3(a)
    glu_a = jnp.dot(h, wm_ref[:, 0:D_CONV], preferred_element_type=F32)
    glu_g = jnp.dot(h, wm_ref[:, D_CONV:2 * D_CONV], preferred_element_type=F32)
    u_ref[...] = (glu_a * jax.nn.sigmoid(glu_g)).astype(BF16)
    a3 = jnp.dot(tri_ref[...], a_split, preferred_element_type=F32)
    z = jnp.dot(h, wm_ref[:, 2 * D_CONV:2 * D_CONV + D_SSM], preferred_element_type=F32)
    z_ref[...] = _silu(z).astype(BF16)
    xbc_ref[...] = jnp.dot(h, wm_ref[:, 2 * D_CONV + D_SSM:], preferred_element_type=F32).astype(BF16)
    acs = a3[:, 0:nh2] + a3[:, nh2:2 * nh2] + a3[:, 2 * nh2:3 * nh2]
    pre_ref[...] = jnp.concatenate([dt, acs, a, jnp.zeros((dt.shape[0], 128 - 3 * nh2), F32)], axis=1)


def _inproj(lay, xp, xs, pos, mod, w_main, w_dt, dt_bias, a_log, tm):
    T = lay.n_tokens
    row, posb, flag = lay.token_tile_tables(tm)
    npt = lay.n_prompt_tokens // tm
    n_main = w_main.shape[1]
    chunk_of = np.arange(tm) // CHUNK
    tri = jnp.asarray((chunk_of[:, None] == chunk_of[None, :]) & np.tril(np.ones((tm, tm), bool)), dtype=BF16)
    gs = pltpu.PrefetchScalarGridSpec(
        num_scalar_prefetch=3,
        grid=(T // tm,),
        in_specs=[pl.BlockSpec((tm, D_MODEL), lambda i, r, p, f: (jnp.minimum(i, npt - 1), 0)),
                  pl.BlockSpec((tm, D_MODEL), lambda i, r, p, f: (jnp.maximum(i - npt, 0), 0)),
                  pl.BlockSpec((tm, D_MODEL), lambda i, r, p, f: (p[i], 0)),
                  pl.BlockSpec((8, 6 * D_MODEL), lambda i, r, p, f: (0, 0)),
                  pl.BlockSpec((D_MODEL, n_main), lambda i, r, p, f: (0, 0)),
                  pl.BlockSpec((D_MODEL, 128), lambda i, r, p, f: (0, 0)),
                  pl.BlockSpec((1, 2 * N_HEADS), lambda i, r, p, f: (0, 0)),
                  pl.BlockSpec((1, 2 * N_HEADS), lambda i, r, p, f: (0, 0)),
                  pl.BlockSpec((tm, tm), lambda i, r, p, f: (0, 0))],
        out_specs=[pl.BlockSpec((tm, D_CONV), lambda i, r, p, f: (i, 0)),
                   pl.BlockSpec((tm, D_SSM), lambda i, r, p, f: (i, 0)),
                   pl.BlockSpec((tm, D_XBC), lambda i, r, p, f: (i, 0)),
                   pl.BlockSpec((tm, 128), lambda i, r, p, f: (i, 0))])
    return pl.pallas_call(
        _inproj_kernel, grid_spec=gs,
        out_shape=(jax.ShapeDtypeStruct((T, D_CONV), BF16),
                   jax.ShapeDtypeStruct((T, D_SSM), BF16),
                   jax.ShapeDtypeStruct((T, D_XBC), BF16),
                   jax.ShapeDtypeStruct((T, 128), F32)),
        compiler_params=_cparams(("arbitrary",)),
        name="inproj",
    )(row, posb, flag, xp, xs, pos, mod, w_main, w_dt, dt_bias.reshape(1, -1), a_log.reshape(1, -1), tri)


_N_SHIFT = 8
_SHIFT_ROWS = CONV_TILE + 2 * HALO - _N_SHIFT
_ROW_BLOCK = 64
_FILL_ROWS = 32
_SSM_ROWS, _SSM_LANES = 64, 256
_NORM_ROWS = 64


def _conv_kernel(lok_ref, rok_ref, u_ref, ul_ref, ur_ref, xbc_ref, xbcl_ref, xbcr_ref,
                 cw_ref, cb_ref, lng_ref, lnb_ref, sw_ref, sb_ref, co_ref, xo_ref,
                 ext_ref, sh_ref, acc_ref, ext2_ref):
    i = pl.program_id(0)
    lok = lok_ref[i] == 1
    rok = rok_ref[i] == 1

    def fill_ext(rb, carry):
        r0 = pl.multiple_of(rb * _FILL_ROWS, _FILL_ROWS)
        dst = pl.ds(pl.multiple_of(HALO + r0, HALO), _FILL_ROWS)
        ext_ref[dst, :] = u_ref[pl.ds(r0, _FILL_ROWS), :].astype(F32)
        ext2_ref[dst, :] = xbc_ref[pl.ds(r0, _FILL_ROWS), :].astype(F32)
        return carry

    ext_ref[0:HALO, :] = jnp.where(lok, ul_ref[...].astype(F32), 0.0)
    ext_ref[HALO + CONV_TILE:, :] = jnp.where(rok, ur_ref[...].astype(F32), 0.0)
    ext2_ref[0:HALO, :] = jnp.where(lok, xbcl_ref[...].astype(F32), 0.0)
    ext2_ref[HALO + CONV_TILE:, :] = jnp.where(rok, xbcr_ref[...].astype(F32), 0.0)
    lax.fori_loop(0, CONV_TILE // _FILL_ROWS, fill_ext, 0)
    for r in range(_N_SHIFT):
        sh_ref[r] = ext_ref[r:r + _SHIFT_ROWS, :]

    first = HALO - (CONV_K - 1) // 2

    for j in range(D_CONV // 128):
        lanes = slice(j * 128, (j + 1) * 128)
        taps = [jnp.broadcast_to(cw_ref[k:k + 1, lanes], (8, 128)) for k in range(CONV_K)]
        bias = jnp.broadcast_to(cb_ref[:, lanes], (8, 128))

        def row_block(rb, carry, lanes=lanes, taps=taps, bias=bias):
            base = pl.multiple_of(rb * _ROW_BLOCK, _ROW_BLOCK)
            for sub in range(_ROW_BLOCK // 8):
                acc = bias
                for k in range(CONV_K):
                    o = first + k
                    row0 = base + (o // _N_SHIFT) * _N_SHIFT + sub * 8
                    acc = acc + sh_ref[o % _N_SHIFT, pl.ds(row0, 8), lanes] * taps[k]
                acc_ref[pl.ds(base + sub * 8, 8), lanes] = acc
            return carry

        lax.fori_loop(0, CONV_TILE // _ROW_BLOCK, row_block, 0)
    for part in range(CONV_TILE // _NORM_ROWS):
        rows = slice(part * _NORM_ROWS, (part + 1) * _NORM_ROWS)
        u = _ln_rows(acc_ref[rows, :]) * lng_ref[...] + lnb_ref[...]
        co_ref[rows, :] = _silu(u).astype(BF16)

    first2 = HALO - (SSM_CONV_K - 1) // 2
    for rb in range(CONV_TILE // _SSM_ROWS):
        for lc in range(D_XBC // _SSM_LANES):
            lanes = slice(lc * _SSM_LANES, (lc + 1) * _SSM_LANES)
            y = jnp.zeros((_SSM_ROWS, _SSM_LANES), F32) + sb_ref[:, lanes]
            for k in range(SSM_CONV_K):
                r0 = first2 + k + rb * _SSM_ROWS
                y = y + ext2_ref[r0:r0 + _SSM_ROWS, lanes] * sw_ref[k:k + 1, lanes]
            xo_ref[rb * _SSM_ROWS:(rb + 1) * _SSM_ROWS, lanes] = _silu(y).astype(BF16)


def _conv(lay, u, xbc, conv_w, conv_b, ln_g, ln_b, ssm_w, ssm_b):
    T = lay.n_tokens
    lok, rok = lay.conv_tile_tables()
    n_tiles = T // CONV_TILE
    hb = CONV_TILE // HALO
    n_hb = T // HALO

    def cur(i, l, r):
        return (i, 0)

    def left(i, l, r):
        return (jnp.maximum(i * hb - 1, 0), 0)

    def right(i, l, r):
        return (jnp.minimum((i + 1) * hb, n_hb - 1), 0)

    def const(i, l, r):
        return (0, 0)

    gs = pltpu.PrefetchScalarGridSpec(
        num_scalar_prefetch=2,
        grid=(n_tiles,),
        in_specs=[pl.BlockSpec((CONV_TILE, D_CONV), cur),
                  pl.BlockSpec((HALO, D_CONV), left),
                  pl.BlockSpec((HALO, D_CONV), right),
                  pl.BlockSpec((CONV_TILE, D_XBC), cur),
                  pl.BlockSpec((HALO, D_XBC), left),
                  pl.BlockSpec((HALO, D_XBC), right),
                  pl.BlockSpec((CONV_K, D_CONV), const),
                  pl.BlockSpec((1, D_CONV), const),
                  pl.BlockSpec((1, D_CONV), const),
                  pl.BlockSpec((1, D_CONV), const),
                  pl.BlockSpec((SSM_CONV_K, D_XBC), const),
                  pl.BlockSpec((1, D_XBC), const)],
        out_specs=[pl.BlockSpec((CONV_TILE, D_CONV), cur),
                   pl.BlockSpec((CONV_TILE, D_XBC), cur)],
        scratch_shapes=[pltpu.VMEM((CONV_TILE + 2 * HALO, D_CONV), F32),
                        pltpu.VMEM((_N_SHIFT, _SHIFT_ROWS, D_CONV), F32),
                        pltpu.VMEM((CONV_TILE, D_CONV), F32),
                        pltpu.VMEM((CONV_TILE + 2 * HALO, D_XBC), F32)])
    return pl.pallas_call(
        _conv_kernel, grid_spec=gs,
        out_shape=(jax.ShapeDtypeStruct((T, D_CONV), BF16),
                   jax.ShapeDtypeStruct((T, D_XBC), BF16)),
        compiler_params=_cparams(("arbitrary",)),
        name="conv",
    )(lok, rok, u, u, u, xbc, xbc, xbc, conv_w, conv_b.reshape(1, -1), ln_g.reshape(1, -1),
      ln_b.reshape(1, -1), ssm_w, ssm_b.reshape(1, -1))


_BN = N_GROUPS * D_STATE


def _ssd_kernel(chunk_ref, yidx_ref, phase_ref, first_ref, last_ref, zero_ref, sin_ref, sout_ref, cloc_ref,
                xbc_ref, z_ref, pre_ref, h0f_ref, h0b_ref, dsk_ref, ng_ref,
                edec_ref, ewb_ref, ecol_ref, ewide_ref, eye3_ref,
                y_ref, hf_out_ref, hb_out_ref,
                hf_ref, g_ref, gin_ref, ybuf_ref):
    s = pl.program_id(0)
    phase = phase_ref[s]
    first = first_ref[s] == 1
    last = last_ref[s] == 1
    zero = zero_ref[s] == 1
    cloc = cloc_ref[s]
    H, P, N = N_HEADS, HEADDIM, D_STATE

    GW = HEADS_PER_GROUP * P

    def chunk_inputs(r0):
        rows = slice(r0, r0 + CHUNK)
        xs = xbc_ref[rows, 0:D_SSM]
        dt = pre_ref[rows, 0:2 * H]
        acs = pre_ref[rows, 2 * H:4 * H]
        a = pre_ref[rows, 4 * H:6 * H]
        tot = acs[CHUNK - 8:CHUNK, :]
        dec = _expand_exact(jnp.exp(tot), edec_ref[...])[7:8, :]
        exb = acs[:, H:2 * H] - a[:, H:2 * H]
        return rows, xs, dt, acs, dec, exb

    def load_state(src_ref, dst_ref):
        for j in range(H // 2):
            pair = jnp.concatenate([src_ref[0, 2 * j], src_ref[0, 2 * j + 1]], axis=0)
            dst_ref[:, 2 * j * P:(2 * j + 2) * P] = jnp.where(zero, 0.0, pair.T)

    def store_state(src_ref, dst_ref):
        for j in range(H // 2):
            pair = src_ref[:, 2 * j * P:(2 * j + 2) * P].T
            dst_ref[0, 2 * j] = pair[0:P]
            dst_ref[0, 2 * j + 1] = pair[P:2 * P]

    @pl.when(phase == 0)
    def _backward_states():
        @pl.when(first)
        def _():
            load_state(h0b_ref, g_ref)

        for half in (1, 0):
            rows, xs, dt, acs, dec, exb = chunk_inputs(half * CHUNK)
            wb = dt[:, H:2 * H] * jnp.exp(exb)
            xw = (xs.astype(F32) * _expand_exact(wb, ewb_ref[...])).astype(BF16)
            for g in range(N_GROUPS):
                cols = slice(g * GW, (g + 1) * GW)
                bg = xbc_ref[rows, D_SSM + g * N:D_SSM + (g + 1) * N]
                gg = g_ref[:, cols]
                gin_ref[2 * cloc + half, :, cols] = gg.astype(BF16)
                upd = lax.dot_general(bg, xw[:, cols], (((0,), (0,)), ((), ())), preferred_element_type=F32)
                g_ref[:, cols] = gg * dec[:, D_SSM + g * GW:D_SSM + (g + 1) * GW] + upd

        @pl.when(last)
        def _():
            store_state(g_ref, hb_out_ref)

    def forward_chunk(half):
        rows, xs, dt, acs, dec, exb = chunk_inputs(half * CHUNK)
        acsf = acs[:, 0:H]
        dtf = dt[:, 0:H]
        dtb = dt[:, H:2 * H]
        totf = acs[CHUNK - 1:CHUNK, 0:H]
        totb = acs[CHUNK - 1:CHUNK, H:2 * H]
        col = _expand_exact(jnp.concatenate([acsf, exb], axis=1), ecol_ref[...])
        q3 = _split3(jnp.concatenate([acsf, exb, dtf, dtb], axis=1))
        qt = lax.dot_general(eye3_ref[...], q3, (((1,), (1,)), ((), ())),
                             preferred_element_type=F32)
        wide = jnp.concatenate([dtf * jnp.exp(totf - acsf), jnp.exp(acsf), jnp.exp(totb - exb)], axis=1)
        wide = _expand_exact(wide, ewide_ref[...])
        xsf = xs.astype(F32)
        xw = (xsf * wide[:, 0:D_SSM]).astype(BF16)
        lower = _iota((CHUNK, CHUNK), 1) <= _iota((CHUNK, CHUNK), 0)
        upper = _iota((CHUNK, CHUNK), 1) >= _iota((CHUNK, CHUNK), 0)
        for g in range(N_GROUPS):
            cols = slice(g * GW, (g + 1) * GW)
            bg = xbc_ref[rows, D_SSM + g * N:D_SSM + (g + 1) * N]
            cg = xbc_ref[rows, D_SSM + _BN + g * N:D_SSM + _BN + (g + 1) * N]
            cb = lax.dot_general(cg, bg, (((1,), (1,)), ((), ())), preferred_element_type=F32)
            hfg = hf_ref[:, cols]
            yf = jnp.dot(cg, hfg.astype(BF16), preferred_element_type=F32)
            yb = jnp.dot(cg, gin_ref[2 * cloc + half, :, cols], preferred_element_type=F32)
            yg = yf * wide[:, D_SSM + g * GW:D_SSM + (g + 1) * GW] \
                + yb * wide[:, 2 * D_SSM + g * GW:2 * D_SSM + (g + 1) * GW]
            upd = lax.dot_general(bg, xw[:, cols], (((0,), (0,)), ((), ())), preferred_element_type=F32)
            hf_ref[:, cols] = hfg * dec[:, cols] + upd
            xg = xs[:, cols]
            head_of_lane = _iota((CHUNK, GW), 1) // P
            for r in range(HEADS_PER_GROUP):
                h = g * HEADS_PER_GROUP + r
                colf = col[:, h * N:(h + 1) * N]
                colb = col[:, (H + h) * N:(H + h + 1) * N]
                mf = jnp.where(lower, jnp.exp(colf - qt[h:h + 1, :]), 0.0) * qt[2 * H + h:2 * H + h + 1, :]
                mb = jnp.where(upper, jnp.exp(qt[H + h:H + h + 1, :] - colb), 0.0) * qt[3 * H + h:3 * H + h + 1, :]
                m = (cb * (mf + mb)).astype(BF16)
                xh = jnp.where(head_of_lane == r, xg, jnp.zeros_like(xg))
                yg = yg + jnp.dot(m, xh, preferred_element_type=F32)
            ybuf_ref[rows, cols] = yg

        yt = (ybuf_ref[rows, :] + dsk_ref[...] * xsf) * z_ref[rows, :].astype(F32)
        gw = D_SSM // N_GROUPS
        for g in range(N_GROUPS):
            seg = yt[:, g * gw:(g + 1) * gw]
            ms = jnp.mean(seg * seg, axis=-1, keepdims=True)
            y_ref[rows, g * gw:(g + 1) * gw] = \
                (seg * lax.rsqrt(ms + LN_EPS) * ng_ref[:, g * gw:(g + 1) * gw]).astype(BF16)

    @pl.when(phase == 1)
    def _forward_and_outputs():
        @pl.when(first)
        def _():
            load_state(h0f_ref, hf_ref)

        for half in (0, 1):
            forward_chunk(half)

        @pl.when(last)
        def _():
            store_state(hf_ref, hf_out_ref)


def _ssd(lay, xbc_c, z, pre, h0f, h0b, d_skip, norm_g):
    T = lay.n_tokens
    tabs = lay.ssd_step_tables()
    n_steps = tabs[0].shape[0]
    nsp = len(tabs)

    def by_chunk(s, *t):
        return (t[0][s], 0)

    def by_y(s, *t):
        return (t[1][s], 0)

    def by_sin(s, *t):
        return (t[6][s], 0, 0, 0)

    def by_sout(s, *t):
        return (t[7][s], 0, 0, 0)

    def const(s, *t):
        return (0, 0)

    H = N_HEADS
    eye3 = jnp.asarray(np.arange(4 * H)[:, None] == np.arange(12 * H)[None, :] % (4 * H), dtype=BF16)
    consts = [_expand3(2 * H, HEADDIM), _expand3(H, HEADDIM), _expand3(2 * H, D_STATE),
              _expand3(3 * H, HEADDIM), eye3]
    sshape = (1, N_HEADS, HEADDIM, D_STATE)
    gs = pltpu.PrefetchScalarGridSpec(
        num_scalar_prefetch=nsp,
        grid=(n_steps,),
        in_specs=[pl.BlockSpec((2 * CHUNK, D_XBC), by_chunk),
                  pl.BlockSpec((2 * CHUNK, D_SSM), by_chunk),
                  pl.BlockSpec((2 * CHUNK, 128), by_chunk),
                  pl.BlockSpec(sshape, by_sin),
                  pl.BlockSpec(sshape, by_sin),
                  pl.BlockSpec((1, D_SSM), const),
                  pl.BlockSpec((1, D_SSM), const)] + [pl.BlockSpec(c.shape, const) for c in consts],
        out_specs=[pl.BlockSpec((2 * CHUNK, D_SSM), by_y),
                   pl.BlockSpec(sshape, by_sout),
                   pl.BlockSpec(sshape, by_sout)],
        scratch_shapes=[pltpu.VMEM((D_STATE, D_SSM), F32),
                        pltpu.VMEM((D_STATE, D_SSM), F32),
                        pltpu.VMEM((lay.max_chunks, D_STATE, D_SSM), BF16),
                        pltpu.VMEM((2 * CHUNK, D_SSM), F32)])
    n_out = lay.n_prompt_seqs
    return pl.pallas_call(
        _ssd_kernel, grid_spec=gs,
        out_shape=(jax.ShapeDtypeStruct((T, D_SSM), BF16),
                   jax.ShapeDtypeStruct((n_out,) + sshape[1:], F32),
                   jax.ShapeDtypeStruct((n_out,) + sshape[1:], F32)),
        compiler_params=_cparams(("arbitrary",)),
        name="ssd",
    )(*tabs, xbc_c, z, pre, h0f, h0b,
      jnp.repeat(d_skip, HEADDIM).reshape(1, -1), norm_g.reshape(1, -1), *consts)


def _outproj_kernel(row_ref, posb_ref, flag_ref, xp_ref, xs_ref, pos_ref, mod_ref, co_ref, ys_ref, wo_ref,
                    g_ref, b_ref, x1_ref, h2p_ref, *, alpha):
    i = pl.program_id(0)
    x = jnp.where(flag_ref[i] == 1, xs_ref[...] + pos_ref[...], xp_ref[...])
    r = row_ref[i]
    g1 = mod_ref[pl.ds(r, 1), 2 * D_MODEL:3 * D_MODEL]
    sh2 = mod_ref[pl.ds(r, 1), 3 * D_MODEL:4 * D_MODEL]
    sc2 = mod_ref[pl.ds(r, 1), 4 * D_MODEL:5 * D_MODEL]
    mix = jnp.dot(jnp.concatenate([co_ref[...], ys_ref[...]], axis=1), wo_ref[...], preferred_element_type=F32)
    x1 = _ln_rows(alpha * x + g1 * mix) * g_ref[...] + b_ref[...]
    x1_ref[...] = x1
    packed = _pack_halves(_ln_rows(x1) * (1.0 + sc2) + sh2)
    for c in range(ROW_PARTS):
        h2p_ref[c] = packed[:, c * PART_WORDS:(c + 1) * PART_WORDS]


def _outproj(lay, xp, xs, pos, mod, conv_out, y_ssm, w_out, ln_g, ln_b, alpha, tm):
    T = lay.n_tokens
    row, posb, flag = lay.token_tile_tables(tm)
    npt = lay.n_prompt_tokens // tm

    def const(i, r, p, f):
        return (0, 0)

    def cur(i, r, p, f):
        return (i, 0)

    gs = pltpu.PrefetchScalarGridSpec(
        num_scalar_prefetch=3,
        grid=(T // tm,),
        in_specs=[pl.BlockSpec((tm, D_MODEL), lambda i, r, p, f: (jnp.minimum(i, npt - 1), 0)),
                  pl.BlockSpec((tm, D_MODEL), lambda i, r, p, f: (jnp.maximum(i - npt, 0), 0)),
                  pl.BlockSpec((tm, D_MODEL), lambda i, r, p, f: (p[i], 0)),
                  pl.BlockSpec((8, 6 * D_MODEL), const),
                  pl.BlockSpec((tm, D_CONV), cur),
                  pl.BlockSpec((tm, D_SSM), cur),
                  pl.BlockSpec((D_CONV + D_SSM, D_MODEL), const),
                  pl.BlockSpec((1, D_MODEL), const),
                  pl.BlockSpec((1, D_MODEL), const)],
        out_specs=[pl.BlockSpec((tm, D_MODEL), cur),
                   pl.BlockSpec((ROW_PARTS, tm, PART_WORDS), lambda i, r, p, f: (0, i, 0))])
    return pl.pallas_call(
        functools.partial(_outproj_kernel, alpha=alpha), grid_spec=gs,
        out_shape=(jax.ShapeDtypeStruct((T, D_MODEL), F32),
                   jax.ShapeDtypeStruct((ROW_PARTS, T, PART_WORDS), jnp.uint32)),
        compiler_params=_cparams(("arbitrary",)),
        name="outproj",
    )(row, posb, flag, xp, xs, pos, mod, conv_out, y_ssm, w_out, ln_g.reshape(1, -1), ln_b.reshape(1, -1))


def _route_kernel(h2_ref, wrt_ref, bias_ref, idx_ref, wts_ref, pos_ref, cnt_ref, carry_ref, *, tm):
    i = pl.program_id(0)

    @pl.when(i == 0)
    def _():
        carry_ref[...] = jnp.zeros_like(carry_ref)

    E, NG, EG = N_EXPERTS, N_EXPERT_GROUPS, EXPERTS_PER_GROUP
    neg = -jnp.inf
    logits = lax.dot_general(wrt_ref[...], _unpack_rows(h2_ref), (((1,), (1,)), ((), ())),
                             preferred_element_type=F32)
    s = jax.nn.sigmoid(logits)
    sel = s + bias_ref[...]
    sel3 = sel.reshape(NG, EG, tm)
    io3 = _iota((NG, EG, tm), 1)
    m1 = jnp.max(sel3, axis=1, keepdims=True)
    f1 = jnp.min(jnp.where(sel3 == m1, io3, EG), axis=1, keepdims=True)
    m2 = jnp.max(jnp.where(io3 == f1, neg, sel3), axis=1, keepdims=True)
    gscore = (m1 + m2).reshape(NG, tm)
    gio = _iota((NG, tm), 0)
    beaten = jnp.zeros((NG, tm), I32)
    for g in range(NG):
        row = gscore[g:g + 1, :]
        beats = jnp.where(row > gscore, 1, jnp.where(row == gscore, jnp.where(g < gio, 1, 0), 0))
        beaten = beaten + beats
    keep = (beaten < TOPK_GROUPS).astype(F32).reshape(NG, 1, tm)
    selm = jnp.where(keep > 0.5, sel3, neg).reshape(E, tm)
    eio = _iota((E, tm), 0)
    chosen = jnp.zeros((E, tm), F32)
    idxs, ws = [], []
    for k in range(TOP_K):
        m = jnp.max(selm, axis=0, keepdims=True)
        am = jnp.minimum(jnp.min(jnp.where(selm == m, eio, E), axis=0, keepdims=True), E - 1)
        hit = eio == am
        ws.append(jnp.sum(jnp.where(hit, s, 0.0), axis=0, keepdims=True))
        idxs.append(am)
        selm = jnp.where(hit, neg, selm)
        chosen = jnp.where(hit, 1.0, chosen)
    wsum = ws[0]
    for k in range(1, TOP_K):
        wsum = wsum + ws[k]
    before = (_iota((tm, tm), 0) < _iota((tm, tm), 1)).astype(BF16)
    prior = jnp.dot(chosen.astype(BF16), before, preferred_element_type=F32)
    carry = carry_ref[...]
    prior = prior + jnp.concatenate([carry] * (tm // 128), axis=1)
    for k in range(TOP_K):
        idx_ref[k:k + 1, :] = idxs[k]
        wts_ref[k:k + 1, :] = ws[k] / wsum * ROUTED_SCALE
        pos_ref[k:k + 1, :] = jnp.sum(jnp.where(eio == idxs[k], prior, 0.0), axis=0, keepdims=True).astype(I32)
    total = jnp.dot(chosen.astype(BF16), jnp.ones((tm, 128), BF16), preferred_element_type=F32)
    carry = carry + total
    carry_ref[...] = carry
    cnt_ref[...] = carry.astype(I32)


def _route(h2p, w_router_t, router_bias, tm):
    T = h2p.shape[1]
    bias_b = jnp.broadcast_to(router_bias.astype(F32)[:, None], (N_EXPERTS, tm))
    w_router_t = w_router_t.astype(BF16)
    return pl.pallas_call(
        functools.partial(_route_kernel, tm=tm),
        grid=(T // tm,),
        in_specs=[pl.BlockSpec((ROW_PARTS, tm, PART_WORDS), lambda i: (0, i, 0)),
                  pl.BlockSpec((N_EXPERTS, D_MODEL), lambda i: (0, 0)),
                  pl.BlockSpec((N_EXPERTS, tm), lambda i: (0, 0))],
        out_specs=[pl.BlockSpec((TOP_K, tm), lambda i: (0, i)),
                   pl.BlockSpec((TOP_K, tm), lambda i: (0, i)),
                   pl.BlockSpec((TOP_K, tm), lambda i: (0, i)),
                   pl.BlockSpec((N_EXPERTS, 128), lambda i: (0, 0))],
        out_shape=(jax.ShapeDtypeStruct((TOP_K, T), I32),
                   jax.ShapeDtypeStruct((TOP_K, T), F32),
                   jax.ShapeDtypeStruct((TOP_K, T), I32),
                   jax.ShapeDtypeStruct((N_EXPERTS, 128), I32)),
        scratch_shapes=[pltpu.VMEM((N_EXPERTS, 128), F32)],
        compiler_params=_cparams(("arbitrary",)),
        name="route",
    )(h2p, w_router_t, bias_b)


def _dest_kernel(idx_ref, pos_ref, start_ref, dest_ref):
    tm = idx_ref.shape[1]
    eio = _iota((N_EXPERTS, tm), 0)
    start = start_ref[...]
    for k in range(TOP_K):
        base = jnp.sum(jnp.where(eio == idx_ref[k:k + 1, :], start, 0.0), axis=0, keepdims=True)
        dest_ref[k:k + 1, :] = base.astype(I32) + pos_ref[k:k + 1, :]


def _dest(idx, pos, pad_start, tm):
    T = idx.shape[1]
    start_b = jnp.broadcast_to(pad_start.astype(F32)[:, None], (N_EXPERTS, tm))
    return pl.pallas_call(
        _dest_kernel,
        grid=(T // tm,),
        in_specs=[pl.BlockSpec((TOP_K, tm), lambda i: (0, i)),
                  pl.BlockSpec((TOP_K, tm), lambda i: (0, i)),
                  pl.BlockSpec((N_EXPERTS, tm), lambda i: (0, 0))],
        out_specs=pl.BlockSpec((TOP_K, tm), lambda i: (0, i)),
        out_shape=jax.ShapeDtypeStruct((TOP_K, T), I32),
        compiler_params=_cparams(("arbitrary",)),
        name="dest",
    )(idx, pos, start_b)


_WEIGHT_FETCH_CHUNKS = 4
_W_SLOTS = 8
_X_SLOTS = 6
_Y_SLOTS = 4


def _expert_kernel(base_ref, nblk_ref, nused_ref, x_hbm, wgu_hbm, wd_hbm, y_hbm,
                   wgu_bf, wd_bf, wgu_stage, wd_stage, xbuf, ybuf, next_ref, sems, xsems, ysems):
    e = pl.program_id(0)
    n_used = nused_ref[0]

    def x_copy(b, slot):
        rows = pl.ds(pl.multiple_of(b * EXPERT_BLOCK, EXPERT_BLOCK), EXPERT_BLOCK)
        return pltpu.make_async_copy(x_hbm.at[:, rows, :], xbuf.at[slot], xsems.at[slot])

    def y_copy(b, slot):
        rows = pl.ds(pl.multiple_of(b * EXPERT_BLOCK, EXPERT_BLOCK), EXPERT_BLOCK)
        return pltpu.make_async_copy(ybuf.at[slot], y_hbm.at[:, rows, :], ysems.at[slot])

    def prefetch(b_first, max_starts):
        for _ in range(max_starts):
            j = next_ref[0]

            @pl.when((j <= b_first + _X_SLOTS - 1) & (j < n_used))
            def _():
                x_copy(j, lax.rem(j, _X_SLOTS)).start()
                next_ref[0] = j + 1

    def fetch(ex):
        ws = lax.rem(ex, _W_SLOTS)
        cps = []
        for c in range(_WEIGHT_FETCH_CHUNKS):
            rg = pl.ds(c * (D_MODEL // _WEIGHT_FETCH_CHUNKS), D_MODEL // _WEIGHT_FETCH_CHUNKS)
            rd = pl.ds(c * (D_EXPERT // _WEIGHT_FETCH_CHUNKS), D_EXPERT // _WEIGHT_FETCH_CHUNKS)
            cps.append(pltpu.make_async_copy(wgu_hbm.at[ex, rg], wgu_stage.at[ws, rg], sems.at[ws, 0]))
            cps.append(pltpu.make_async_copy(wd_hbm.at[ex, rd], wd_stage.at[ws, rd], sems.at[ws, 1]))
        return cps

    @pl.when(e == 0)
    def _():
        for ex in range(_W_SLOTS):
            for cp in fetch(ex):
                cp.start()
        next_ref[0] = 0
        prefetch(0, _X_SLOTS - 1)

    for cp in fetch(e):
        cp.wait()
    ws = lax.rem(e, _W_SLOTS)
    wgu_bf[...] = wgu_stage[ws].astype(BF16)
    wd_bf[...] = wd_stage[ws].astype(BF16)

    @pl.when(e + _W_SLOTS < N_EXPERTS)
    def _():
        for cp in fetch(e + _W_SLOTS):
            cp.start()

    def compute(b):
        xs = lax.rem(b, _X_SLOTS)
        gu = jnp.dot(_unpack_rows(xbuf.at[xs]), wgu_bf[...], preferred_element_type=F32)
        act = (_silu(gu[:, 0:D_EXPERT]) * gu[:, D_EXPERT:]).astype(BF16)
        packed = _pack_halves(jnp.dot(act, wd_bf[...], preferred_element_type=F32))
        for c in range(ROW_PARTS):
            ybuf[lax.rem(b, _Y_SLOTS), c] = packed[:, c * PART_WORDS:(c + 1) * PART_WORDS]

    def step(b, width):
        for d in range(width):
            x_copy(b + d, lax.rem(b + d, _X_SLOTS)).wait()
        prefetch(b, width)
        for d in range(width):
            @pl.when(b + d >= _Y_SLOTS)
            def _(d=d):
                y_copy(b + d - _Y_SLOTS, lax.rem(b + d, _Y_SLOTS)).wait()
        for d in range(width):
            compute(b + d)
        for d in range(width):
            y_copy(b + d, lax.rem(b + d, _Y_SLOTS)).start()

    lo = base_ref[e]
    n_pairs = nblk_ref[e] // 2

    def pair(i, carry):
        step(lo + 2 * i, 2)
        return carry

    lax.fori_loop(0, n_pairs, pair, 0)

    @pl.when(nblk_ref[e] % 2 == 1)
    def _():
        step(lo + 2 * n_pairs, 1)

    @pl.when(e == N_EXPERTS - 1)
    def _():
        for d in range(1, _Y_SLOTS + 1):
            @pl.when(n_used >= d)
            def _(d=d):
                y_copy(n_used - d, lax.rem(n_used - d, _Y_SLOTS)).wait()


def _pack_halves(x):
    n = x.shape[1] // 2
    hi = lax.bitcast_convert_type(x[:, :n].astype(BF16).astype(F32), jnp.uint32)
    lo = lax.bitcast_convert_type(x[:, n:].astype(BF16).astype(F32), jnp.uint32)
    return hi | (lo >> 16)


def _unpack_halves(p):
    hi = lax.bitcast_convert_type(p & jnp.uint32(0xFFFF0000), F32)
    lo = lax.bitcast_convert_type(p << 16, F32)
    return hi, lo


def _unpack_rows(ref):
    parts = [_unpack_halves(ref[c]) for c in range(ROW_PARTS)]
    chunks = [p[0] for p in parts] + [p[1] for p in parts]
    return jnp.concatenate([xc.astype(BF16) for xc in chunks], axis=1)


def _expert(x_sorted, w_gu, w_down, blk_base, blk_count, n_used):
    n_rows = x_sorted.shape[1]
    blk_shape = (ROW_PARTS, EXPERT_BLOCK, PART_WORDS)
    gs = pltpu.PrefetchScalarGridSpec(
        num_scalar_prefetch=3,
        grid=(N_EXPERTS,),
        in_specs=[pl.BlockSpec(memory_space=pl.ANY),
                  pl.BlockSpec(memory_space=pl.ANY),
                  pl.BlockSpec(memory_space=pl.ANY)],
        out_specs=pl.BlockSpec(memory_space=pl.ANY),
        scratch_shapes=[pltpu.VMEM((D_MODEL, 2 * D_EXPERT), BF16),
                        pltpu.VMEM((D_EXPERT, D_MODEL), BF16),
                        pltpu.VMEM((_W_SLOTS, D_MODEL, 2 * D_EXPERT), F32),
                        pltpu.VMEM((_W_SLOTS, D_EXPERT, D_MODEL), F32),
                        pltpu.VMEM((_X_SLOTS,) + blk_shape, jnp.uint32),
                        pltpu.VMEM((_Y_SLOTS,) + blk_shape, jnp.uint32),
                        pltpu.SMEM((1,), I32),
                        pltpu.SemaphoreType.DMA((_W_SLOTS, 2)),
                        pltpu.SemaphoreType.DMA((_X_SLOTS,)),
                        pltpu.SemaphoreType.DMA((_Y_SLOTS,))])
    return pl.pallas_call(
        _expert_kernel, grid_spec=gs,
        out_shape=jax.ShapeDtypeStruct((ROW_PARTS, n_rows, PART_WORDS), jnp.uint32),
        compiler_params=_cparams(("arbitrary",)),
        name="expert",
    )(blk_base, blk_count, n_used, x_sorted, w_gu, w_down)


def _combine_kernel(row_ref, h2_ref, x1_ref, wt_ref, mod_ref, wsg_ref, wsd_ref, g_ref, b_ref, yt_ref,
                    o_ref, *, first_tile, alpha):
    i = pl.program_id(0)
    h2 = _unpack_rows(h2_ref)
    su = jnp.dot(h2, wsg_ref[...], preferred_element_type=F32)
    act = (_silu(su[:, 0:D_SHARED]) * su[:, D_SHARED:]).astype(BF16)
    moe = jnp.dot(act, wsd_ref[...], preferred_element_type=F32)
    wt = wt_ref[...].T
    his, los = [], []
    for c in range(ROW_PARTS):
        rh = jnp.zeros((h2.shape[0], PART_WORDS), F32)
        rl = jnp.zeros((h2.shape[0], PART_WORDS), F32)
        for k in range(TOP_K):
            hi, lo = _unpack_halves(yt_ref[c, k])
            w = wt[:, k:k + 1]
            rh = rh + hi * w
            rl = rl + lo * w
        his.append(rh)
        los.append(rl)
    moe = moe + jnp.concatenate(his + los, axis=1)
    g2 = mod_ref[pl.ds(row_ref[first_tile + i], 1), 5 * D_MODEL:6 * D_MODEL]
    o_ref[...] = _ln_rows(alpha * x1_ref[...] + g2 * moe) * g_ref[...] + b_ref[...]


def _combine(lay, h2, x1, wts_tok, mod, w_sh_gu, w_sh_down, ln_g, ln_b, y_tok, alpha, tm, first_token, n_tok):
    row, _, _ = lay.token_tile_tables(tm)
    first_tile = first_token // tm

    def cur(i, r):
        return (first_tile + i, 0)

    def const(i, r):
        return (0, 0)

    gs = pltpu.PrefetchScalarGridSpec(
        num_scalar_prefetch=1,
        grid=(n_tok // tm,),
        in_specs=[pl.BlockSpec((ROW_PARTS, tm, PART_WORDS), lambda i, r: (0, first_tile + i, 0)),
                  pl.BlockSpec((tm, D_MODEL), cur),
                  pl.BlockSpec((TOP_K, tm), lambda i, r: (0, first_tile + i)),
                  pl.BlockSpec((8, 6 * D_MODEL), const),
                  pl.BlockSpec((D_MODEL, 2 * D_SHARED), const),
                  pl.BlockSpec((D_SHARED, D_MODEL), const),
                  pl.BlockSpec((1, D_MODEL), const),
                  pl.BlockSpec((1, D_MODEL), const),
                  pl.BlockSpec((ROW_PARTS, TOP_K, tm, PART_WORDS), lambda i, r: (0, 0, i, 0))],
        out_specs=pl.BlockSpec((tm, D_MODEL), lambda i, r: (i, 0)))
    return pl.pallas_call(
        functools.partial(_combine_kernel, first_tile=first_tile, alpha=alpha), grid_spec=gs,
        out_shape=jax.ShapeDtypeStruct((n_tok, D_MODEL), F32),
        compiler_params=_cparams(("arbitrary",)),
        name="combine",
    )(row, h2, x1, wts_tok, mod, w_sh_gu, w_sh_down, ln_g.reshape(1, -1), ln_b.reshape(1, -1), y_tok)


_GATHER_WINDOW = 128


def _sc_gather(table, idx):
    n, d = idx.shape[0], table.shape[1]
    mesh = plsc.VectorSubcoreMesh(core_axis_name="core", subcore_axis_name="subcore")

    @pl.kernel(out_type=jax.ShapeDtypeStruct((n, d), table.dtype), mesh=mesh)
    def gather_kernel(table_hbm, idx_hbm, out_hbm):
        def body(idx_vmem, out_vmem):
            pltpu.sync_copy(table_hbm.at[idx_vmem.at[0]], out_vmem)

        pltpu.emit_pipeline(
            body,
            grid=(n // _GATHER_WINDOW,),
            in_specs=[pl.BlockSpec((1, _GATHER_WINDOW), index_map=lambda i: (0, i))],
            out_specs=[pl.BlockSpec((_GATHER_WINDOW, d), index_map=lambda i: (i, 0))],
            core_axis_name=("core", "subcore"),
            dimension_semantics=(pltpu.PARALLEL,),
        )(idx_hbm, out_hbm)

    return gather_kernel(table, idx.reshape(1, n))


def _sc_scatter(rows, idx, n_out, repeat):
    n, d = rows.shape
    mesh = plsc.VectorSubcoreMesh(core_axis_name="core", subcore_axis_name="subcore")

    @pl.kernel(out_type=jax.ShapeDtypeStruct((n_out, d), rows.dtype), mesh=mesh, scratch_types=[])
    def scatter_kernel(rows_hbm, idx_hbm, out_hbm):
        def body(rows_vmem, idx_vmem):
            for r in range(repeat):
                pltpu.sync_copy(rows_vmem, out_hbm.at[idx_vmem.at[r]])

        pltpu.emit_pipeline(
            body,
            grid=(n // _GATHER_WINDOW,),
            in_specs=[pl.BlockSpec((_GATHER_WINDOW, d), index_map=lambda i: (i, 0)),
                      pl.BlockSpec((repeat, _GATHER_WINDOW), index_map=lambda i: (0, i))],
            out_specs=[],
            core_axis_name=("core", "subcore"),
            dimension_semantics=(pltpu.PARALLEL,),
        )(rows_hbm, idx_hbm)

    return scatter_kernel(rows, idx.reshape(repeat, n))


class _Layout:
    def __init__(self, n_prompt_seqs, prompt_len, n_sample_seqs, sample_len):
        self.n_prompt_seqs, self.prompt_len = n_prompt_seqs, prompt_len
        self.n_sample_seqs, self.sample_len = n_sample_seqs, sample_len
        self.n_prompt_tokens = n_prompt_seqs * prompt_len
        self.n_tokens = self.n_prompt_tokens + n_sample_seqs * sample_len
        assert prompt_len % CONV_TILE == 0 and sample_len % CONV_TILE == 0
        self.max_chunks = max(prompt_len, sample_len) // CHUNK

    def token_tile_tables(self, tm):
        assert self.n_prompt_tokens % tm == 0 and self.sample_len % tm == 0
        npt = self.n_prompt_tokens // tm
        per_seq = self.sample_len // tm
        n = self.n_tokens // tm
        row = np.zeros(n, np.int32)
        posb = np.zeros(n, np.int32)
        flag = np.zeros(n, np.int32)
        for i in range(npt, n):
            j = i - npt
            row[i] = 1 + j // per_seq
            posb[i] = j % per_seq
            flag[i] = 1
        return jnp.asarray(row), jnp.asarray(posb), jnp.asarray(flag)

    def conv_tile_tables(self):
        lok, rok = [], []
        for n_seq, length in ((self.n_prompt_seqs, self.prompt_len), (self.n_sample_seqs, self.sample_len)):
            per = length // CONV_TILE
            for _ in range(n_seq):
                for j in range(per):
                    lok.append(int(j > 0))
                    rok.append(int(j < per - 1))
        return jnp.asarray(np.array(lok, np.int32)), jnp.asarray(np.array(rok, np.int32))

    def ssd_step_tables(self):
        cols = [[] for _ in range(9)]
        seqs = []
        pair_rows = 2 * CHUNK
        c0 = self.n_prompt_tokens // pair_rows
        for j in range(self.n_sample_seqs):
            nc = self.sample_len // pair_rows
            seqs.append((c0 + j * nc, nc, 0, j, 0))
        for j in range(self.n_prompt_seqs):
            nc = self.prompt_len // pair_rows
            seqs.append((j * nc, nc, 1, 0, j))
        for base, nc, zero, sin, sout in seqs:
            for phase in (0, 1):
                order = range(nc - 1, -1, -1) if phase == 0 else range(nc)
                for n, c in enumerate(order):
                    vals = (base + c, base if phase == 0 else base + c, phase, int(n == 0), int(n == nc - 1),
                            zero, sin, sout, c)
                    for col, v in zip(cols, vals):
                        col.append(v)
        return tuple(jnp.asarray(np.array(col, np.int32)) for col in cols)


def _grid_pos_embed(n_tokens):
    rows = n_tokens // GRID_W
    quarter = D_MODEL // 4
    freq = jnp.exp(-math.log(10000.0) * jnp.arange(quarter, dtype=F32) / quarter)
    r = jnp.broadcast_to(jnp.arange(rows, dtype=F32)[:, None, None] * freq, (rows, GRID_W, quarter))
    cl = jnp.broadcast_to(jnp.arange(GRID_W, dtype=F32)[None, :, None] * freq, (rows, GRID_W, quarter))
    emb = jnp.concatenate([jnp.sin(r), jnp.cos(r), jnp.sin(cl), jnp.cos(cl)], axis=-1)
    return emb.reshape(rows * GRID_W, D_MODEL)


def _moe_plan(counts):
    blk_count = (counts + EXPERT_BLOCK - 1) // EXPERT_BLOCK
    blk_end = jnp.cumsum(blk_count)
    blk_base = blk_end - blk_count
    return ((blk_base * EXPERT_BLOCK).astype(I32), blk_base.astype(I32), blk_count.astype(I32),
            blk_end[-1:].astype(I32))


def _layer(lay, xp, xs, pos, cond8, h0f, h0b, lp, alpha, tm_proj=512, tm_route=512, tm_comb=512):
    (w_ada, b_ada, w_in, conv_w, conv_b, conv_ln_g, conv_ln_b, ssm_conv_w, ssm_conv_b, dt_bias, a_log,
     d_skip, ssm_norm_g, w_out, ln1_g, ln1_b, w_router, router_bias, w_exp_gu, w_exp_down, w_sh_gu,
     w_sh_down, ln2_g, ln2_b) = lp
    T = lay.n_tokens
    n_main = 2 * D_CONV + D_SSM + D_XBC
    w_main = w_in[:, :n_main].astype(BF16)
    w_dt = jnp.pad(w_in[:, n_main:], ((0, 0), (0, 128 - 2 * N_HEADS))).astype(BF16)

    mod = _ada(cond8, w_ada, b_ada)
    u, z, xbc, pre = _inproj(lay, xp, xs, pos, mod, w_main, w_dt, dt_bias, a_log, tm_proj)
    conv_out, xbc_c = _conv(lay, u, xbc, conv_w, conv_b, conv_ln_g, conv_ln_b, ssm_conv_w, ssm_conv_b)
    y_ssm, hf, hb = _ssd(lay, xbc_c, z, pre, h0f, h0b, d_skip, ssm_norm_g)
    x1, h2p = _outproj(lay, xp, xs, pos, mod, conv_out, y_ssm, w_out.astype(BF16), ln1_g, ln1_b, alpha, tm_proj)

    idx, wts, posn, cnt = _route(h2p, w_router.T, router_bias, tm_route)
    n_blocks = -(-T * TOP_K // EXPERT_BLOCK) + N_EXPERTS
    pad_start, blk_base, blk_count, n_used = _moe_plan(cnt[:, 0])
    dest2 = _dest(idx, posn, pad_start, min(2048, T))
    n_rows = n_blocks * EXPERT_BLOCK
    scatter_idx = jnp.concatenate([dest2 + c * n_rows for c in range(ROW_PARTS)], axis=1)
    x_sorted = _sc_scatter(h2p.reshape(ROW_PARTS * T, PART_WORDS), scatter_idx.reshape(-1),
                           ROW_PARTS * n_rows, TOP_K).reshape(ROW_PARTS, n_rows, PART_WORDS)
    y_sorted = _expert(x_sorted, w_exp_gu, w_exp_down, blk_base, blk_count, n_used)
    y_flat = y_sorted.reshape(ROW_PARTS * n_rows, PART_WORDS)
    wts_tok, wsg, wsd = wts, w_sh_gu.astype(BF16), w_sh_down.astype(BF16)
    outs = []
    for first, n_tok in ((0, lay.n_prompt_tokens), (lay.n_prompt_tokens, T - lay.n_prompt_tokens)):
        dest_g = dest2[:, first:first + n_tok].reshape(-1)
        idx_parts = jnp.concatenate([dest_g + c * n_rows for c in range(ROW_PARTS)])
        y_tok = _sc_gather(y_flat, idx_parts).reshape(ROW_PARTS, TOP_K, n_tok, PART_WORDS)
        outs.append(_combine(lay, h2p, x1, wts_tok, mod, wsg, wsd, ln2_g, ln2_b, y_tok, alpha, tm_comb,
                             first, n_tok))
    return outs[0], outs[1], hf, hb


def kernel(x_prompt, x_sample, state_ssd_fwd, state_ssd_bwd, c, c_ctx, w_ada, b_ada, w_in, conv_w, conv_b, conv_ln_g, conv_ln_b, ssm_conv_w, ssm_conv_b, dt_bias, a_log, d_skip, ssm_norm_g, w_out, ln1_g, ln1_b, w_router, router_bias, w_exp_gu, w_exp_down, w_sh_gu, w_sh_down, ln2_g, ln2_b):
    depth = w_ada.shape[0]
    assert depth == 1, "the prompt and latent passes are fused per layer; one layer is supported"
    bp, lp_, _ = x_prompt.shape
    bd, ld, _ = x_sample.shape
    lay = _Layout(bp, lp_, bd, ld)
    alpha = (2.0 * depth) ** 0.25
    stacked = (w_ada, b_ada, w_in, conv_w, conv_b, conv_ln_g, conv_ln_b, ssm_conv_w, ssm_conv_b,
               dt_bias, a_log, d_skip, ssm_norm_g, w_out, ln1_g, ln1_b, w_router, router_bias,
               w_exp_gu, w_exp_down, w_sh_gu, w_sh_down, ln2_g, ln2_b)
    lp = [w[0] for w in stacked]
    cond8 = jnp.concatenate([c_ctx[None, :], c, jnp.zeros((8 - 1 - bd, D_MODEL), F32)], axis=0)
    pos = _grid_pos_embed(ld)
    sshape = (bd, N_HEADS, HEADDIM, D_STATE)
    out_p, out_s, hf, hb = _layer(lay, x_prompt.reshape(bp * lp_, D_MODEL), x_sample.reshape(bd * ld, D_MODEL),
                                  pos, cond8, state_ssd_fwd[:, 0].reshape(sshape),
                                  state_ssd_bwd[:, 0].reshape(sshape), lp, alpha)
    return (out_p.reshape(bp, lp_, D_MODEL), out_s.reshape(bd, ld, D_MODEL),
            hf[:, None], hb[:, None])
```
